```python
import jax, jax.numpy as jnp
from jax import lax
import numpy as np

D_MODEL = 1024
BATCH = 1
SEQ = 16384
DEPTH = 2
DEC_BATCH = 128
DEC_SEQ = 1
PAST_LEN = 16384
PAGE_SIZE = 128

MIX_WIDTH = D_MODEL
HEAD_DIM = 64
ATTN_WIDTH = MIX_WIDTH // 2
N_HEADS = ATTN_WIDTH // HEAD_DIM
N_KV_HEADS = 2
GROUP = N_HEADS // N_KV_HEADS
KV_WIDTH = N_KV_HEADS * HEAD_DIM
POOL_WIDTH = MIX_WIDTH - ATTN_WIDTH
POOL_WINDOWS = (2, 4, 8, 16)
N_POOL_GROUPS = len(POOL_WINDOWS)
POOL_GC = POOL_WIDTH // N_POOL_GROUPS
POOL_PREFIX = max(POOL_WINDOWS) - 1
IN_WIDTH = ATTN_WIDTH + 2 * KV_WIDTH + POOL_WIDTH
WINDOW = 128
ATTN_BLOCK = WINDOW
ROPE_THETA = 500000.0
ROT_DIM = HEAD_DIM // 4
N_EXPERTS = 32
TOP_K = 4
D_FF = D_MODEL
SWIGLU_ALPHA = 1.702
SWIGLU_LIMIT = 7.0
MOE_BLOCK = 128
PLE_DIM = 256
EPS = 1e-5

kernel_name = "hymba_swa_pool_moe_ple_step"

F32 = jnp.float32


def rms_norm(x, g):
    xf = x.astype(F32)
    y = xf * lax.rsqrt(jnp.mean(xf * xf, axis=-1, keepdims=True) + EPS)
    return (y * g.astype(F32)).astype(x.dtype)


def rope(x, pos):
    half = ROT_DIM // 2
    inv = ROPE_THETA ** (-jnp.arange(half, dtype=F32) / half)
    ang = pos.astype(F32)[:, None] * inv
    cos = jnp.cos(ang)[:, None, :]
    sin = jnp.sin(ang)[:, None, :]
    xf = x.astype(F32)
    x1 = xf[..., :half]
    x2 = xf[..., half:ROT_DIM]
    out = jnp.concatenate([x1 * cos - x2 * sin, x2 * cos + x1 * sin, xf[..., ROT_DIM:]], axis=-1)
    return out.astype(x.dtype)


def sink_attention(q, k, v, q_pos, k_pos, sinks):
    s = jnp.einsum('...qhgd,...khd->...hgqk', q, k).astype(F32) * (HEAD_DIM ** -0.5)
    diff = q_pos[..., :, None] - k_pos[..., None, :]
    mask = (diff >= 0) & (diff < WINDOW) & (k_pos[..., None, :] >= 0)
    s = jnp.where(mask[..., None, None, :, :], s, -1e30)
    sink = sinks.astype(F32).reshape(N_KV_HEADS, GROUP, 1, 1)
    m = jnp.maximum(jnp.max(s, axis=-1, keepdims=True), sink)
    e = jnp.exp(s - m)
    probs = e / (jnp.sum(e, axis=-1, keepdims=True) + jnp.exp(sink - m))
    return jnp.einsum('...hgqk,...khd->...qhgd', probs.astype(v.dtype), v)


def pool_mix(u_ext, start_pos, w_pool, pool_scale):
    t = u_ext.shape[1] - POOL_PREFIX
    uf = u_ext.astype(F32)
    cs = jnp.concatenate([jnp.zeros_like(uf[:, :1]), jnp.cumsum(uf, axis=1)], axis=1)
    pos = start_pos + jnp.arange(t)
    cur = uf[:, POOL_PREFIX:]
    diffs = []
    for gi, w in enumerate(POOL_WINDOWS):
        sl = slice(gi * POOL_GC, (gi + 1) * POOL_GC)
        wsum = cs[:, POOL_PREFIX + 1:, sl] - cs[:, POOL_PREFIX + 1 - w: POOL_PREFIX + 1 - w + t, sl]
        cnt = jnp.minimum(pos + 1, w).astype(F32)[None, :, None]
        diffs.append(wsum / cnt - cur[..., sl])
    d = jnp.stack(diffs, axis=2).astype(u_ext.dtype)
    y = jnp.einsum('ntgc,gcd->ntgd', d, w_pool)
    return y.reshape(u_ext.shape[0], t, POOL_WIDTH) * pool_scale


def moe(h, w_router, b_router, w_gate_up, b_gate_up, w_down, b_down):
    n, d = h.shape
    logits = (h @ w_router).astype(F32) + b_router.astype(F32)
    top_val, top_idx = lax.top_k(logits, TOP_K)
    gates = jax.nn.softmax(top_val, axis=-1).astype(h.dtype)
    nk = n * TOP_K
    flat_e = top_idx.reshape(nk)
    flat_tok = jnp.repeat(jnp.arange(n, dtype=jnp.int32), TOP_K)
    flat_g = gates.reshape(nk)
    order = jnp.argsort(flat_e)
    e_sorted = flat_e[order]
    counts = jnp.bincount(flat_e, length=N_EXPERTS)
    padded = (counts + MOE_BLOCK - 1) // MOE_BLOCK * MOE_BLOCK
    pad_end = jnp.cumsum(padded)
    pad_start = pad_end - padded
    start = jnp.cumsum(counts) - counts
    dest = pad_start[e_sorted] + jnp.arange(nk) - start[e_sorted]
    n_rows = -(-(nk + N_EXPERTS * (MOE_BLOCK - 1)) // MOE_BLOCK) * MOE_BLOCK
    n_blocks = n_rows // MOE_BLOCK
    tok_buf = jnp.full((n_rows,), n, jnp.int32).at[dest].set(flat_tok[order])
    gate_buf = jnp.zeros((n_rows,), h.dtype).at[dest].set(flat_g[order])
    blk_e = jnp.minimum(jnp.searchsorted(pad_end, jnp.arange(n_blocks) * MOE_BLOCK, side='right'), N_EXPERTS - 1)
    h_pad = jnp.concatenate([h, jnp.zeros((1, d), h.dtype)], axis=0)

    def expert_block(args):
        tok, e = args
        xb = h_pad[tok]
        gu = xb @ w_gate_up[e] + b_gate_up[e]
        g = jnp.minimum(gu[:, :D_FF], SWIGLU_LIMIT)
        up = jnp.clip(gu[:, D_FF:], -SWIGLU_LIMIT, SWIGLU_LIMIT)
        act = (up + 1) * (g * jax.nn.sigmoid(SWIGLU_ALPHA * g))
        return act @ w_down[e] + b_down[e]

    y = lax.map(expert_block, (tok_buf.reshape(n_blocks, MOE_BLOCK), blk_e))
    y = y.reshape(n_rows, d) * gate_buf[:, None]
    return jnp.zeros((n + 1, d), h.dtype).at[tok_buf].add(y)[:n]


def layer(x, p, k_past, v_past, pool_past, start_pos,
          norm_attn, w_in, q_norm, k_norm, attn_sinks, w_pool, pool_scale, w_out,
          norm_ffn, w_router, b_router, w_gate_up, b_gate_up, w_down, b_down,
          norm_ple, w_ple_gate, w_ple_proj):
    n, t, _ = x.shape
    h = rms_norm(x, norm_attn)
    proj = h @ w_in
    q, k, v, u = jnp.split(proj, [ATTN_WIDTH, ATTN_WIDTH + KV_WIDTH, ATTN_WIDTH + 2 * KV_WIDTH], axis=-1)
    pos = start_pos + jnp.arange(t)
    q = rope(rms_norm(q.reshape(n, t, N_HEADS, HEAD_DIM), q_norm), pos)
    k = rope(rms_norm(k.reshape(n, t, N_KV_HEADS, HEAD_DIM), k_norm), pos)
    v = v.reshape(n, t, N_KV_HEADS, HEAD_DIM)
    q = q.reshape(n, t, N_KV_HEADS, GROUP, HEAD_DIM)

    if k_past is None:
        nb = t // ATTN_BLOCK
        qb = q.reshape(n, nb, ATTN_BLOCK, N_KV_HEADS, GROUP, HEAD_DIM)
        kp = jnp.pad(k, ((0, 0), (ATTN_BLOCK, 0), (0, 0), (0, 0))).reshape(n, nb + 1, ATTN_BLOCK, N_KV_HEADS, HEAD_DIM)
        vp = jnp.pad(v, ((0, 0), (ATTN_BLOCK, 0), (0, 0), (0, 0))).reshape(n, nb + 1, ATTN_BLOCK, N_KV_HEADS, HEAD_DIM)
        kband = jnp.concatenate([kp[:, :-1], kp[:, 1:]], axis=2)
        vband = jnp.concatenate([vp[:, :-1], vp[:, 1:]], axis=2)
        posp = (jnp.arange(t + ATTN_BLOCK) - ATTN_BLOCK).reshape(nb + 1, ATTN_BLOCK)
        kpos = jnp.concatenate([posp[:-1], posp[1:]], axis=1)
        qpos = pos.reshape(nb, ATTN_BLOCK)
        a = sink_attention(qb, kband, vband, qpos, kpos, attn_sinks).reshape(n, t, ATTN_WIDTH)
        new_k = k[:, -WINDOW:]
        new_v = v[:, -WINDOW:]
        prefix = jnp.zeros((n, POOL_PREFIX, POOL_WIDTH), u.dtype)
    else:
        wb = k_past.shape[1]
        kc = jnp.concatenate([k_past, k], axis=1)
        vc = jnp.concatenate([v_past, v], axis=1)
        kpos = jnp.concatenate([start_pos - wb + jnp.arange(wb), pos])
        a = sink_attention(q, kc, vc, pos, kpos, attn_sinks).reshape(n, t, ATTN_WIDTH)
        new_k = kc[:, -wb:]
        new_v = vc[:, -wb:]
        prefix = pool_past

    u_ext = jnp.concatenate([prefix, u], axis=1)
    z = pool_mix(u_ext, start_pos, w_pool, pool_scale)
    new_pool = u_ext[:, -POOL_PREFIX:]

    x = x + jnp.concatenate([a, z], axis=-1) @ w_out
    hf = rms_norm(x, norm_ffn).reshape(n * t, D_MODEL)
    x = x + moe(hf, w_router, b_router, w_gate_up, b_gate_up, w_down, b_down).reshape(n, t, D_MODEL)
    gate = jax.nn.sigmoid(rms_norm(x, norm_ple) @ w_ple_gate)
    x = x + gate * (p @ w_ple_proj)
    return x, new_k, new_v, new_pool


def setup_inputs(seed: int = 0) -> dict:
    key = jax.random.key(seed)
    ks = jax.random.split(key, 26)
    wb = min(WINDOW, PAST_LEN)
    nrm = jax.random.normal

    def gain(k, shape):
        return 1.0 + 0.1 * nrm(k, shape, F32)

    return {
        "x_prompt": nrm(ks[0], (BATCH, SEQ, D_MODEL), F32),
        "x_sample": nrm(ks[1], (DEC_BATCH, DEC_SEQ, D_MODEL), F32),
        "cache_k": nrm(ks[2], (DEPTH, DEC_BATCH, wb, N_KV_HEADS, HEAD_DIM), F32),
        "cache_v": nrm(ks[3], (DEPTH, DEC_BATCH, wb, N_KV_HEADS, HEAD_DIM), F32),
        "state_pool": nrm(ks[4], (DEPTH, DEC_BATCH, POOL_PREFIX, POOL_WIDTH), F32),
        "p_prompt": nrm(ks[5], (DEPTH, BATCH, SEQ, PLE_DIM), F32),
        "p_sample": nrm(ks[6], (DEPTH, DEC_BATCH, DEC_SEQ, PLE_DIM), F32),
        "norm_attn": gain(ks[7], (DEPTH, D_MODEL)),
        "w_in": nrm(ks[8], (DEPTH, D_MODEL, IN_WIDTH), F32) * D_MODEL ** -0.5,
        "q_norm": gain(ks[9], (DEPTH, HEAD_DIM)),
        "k_norm": gain(ks[10], (DEPTH, HEAD_DIM)),
        "attn_sinks": nrm(ks[11], (DEPTH, N_HEADS), F32),
        "w_pool": nrm(ks[12], (DEPTH, N_POOL_GROUPS, POOL_GC, POOL_GC), F32) * POOL_GC ** -0.5,
        "pool_scale": gain(ks[13], (DEPTH, POOL_WIDTH)),
        "w_out": nrm(ks[14], (DEPTH, MIX_WIDTH, D_MODEL), F32) * MIX_WIDTH ** -0.5,
        "norm_ffn": gain(ks[15], (DEPTH, D_MODEL)),
        "w_router": nrm(ks[16], (DEPTH, D_MODEL, N_EXPERTS), F32) * D_MODEL ** -0.5,
        "b_router": 0.01 * nrm(ks[17], (DEPTH, N_EXPERTS), F32),
        "w_gate_up": nrm(ks[18], (DEPTH, N_EXPERTS, D_MODEL, 2 * D_FF), F32) * D_MODEL ** -0.5,
        "b_gate_up": 0.02 * nrm(ks[19], (DEPTH, N_EXPERTS, 2 * D_FF), F32),
        "w_down": nrm(ks[20], (DEPTH, N_EXPERTS, D_FF, D_MODEL), F32) * D_FF ** -0.5,
        "b_down": 0.02 * nrm(ks[21], (DEPTH, N_EXPERTS, D_MODEL), F32),
        "norm_ple": gain(ks[22], (DEPTH, D_MODEL)),
        "w_ple_gate": nrm(ks[23], (DEPTH, D_MODEL, D_MODEL), F32) * D_MODEL ** -0.5,
        "w_ple_proj": nrm(ks[24], (DEPTH, PLE_DIM, D_MODEL), F32) * PLE_DIM ** -0.5,
    }


def reference(x_prompt, x_sample, cache_k, cache_v, state_pool, p_prompt, p_sample,
              norm_attn, w_in, q_norm, k_norm, attn_sinks, w_pool, pool_scale, w_out,
              norm_ffn, w_router, b_router, w_gate_up, b_gate_up, w_down, b_down,
              norm_ple, w_ple_gate, w_ple_proj):
    yp = x_prompt
    ys = x_sample
    kp_l, vp_l, pp_l, ks_l, vs_l, ps_l = [], [], [], [], [], []
    for i in range(DEPTH):
        params = (norm_attn[i], w_in[i], q_norm[i], k_norm[i], attn_sinks[i], w_pool[i], pool_scale[i], w_out[i],
                  norm_ffn[i], w_router[i], b_router[i], w_gate_up[i], b_gate_up[i], w_down[i], b_down[i],
                  norm_ple[i], w_ple_gate[i], w_ple_proj[i])
        yp, kp, vp, pp = layer(yp, p_prompt[i], None, None, None, 0, *params)
        ys, kss, vss, pss = layer(ys, p_sample[i], cache_k[i], cache_v[i], state_pool[i], PAST_LEN, *params)
        kp_l.append(kp)
        vp_l.append(vp)
        pp_l.append(pp)
        ks_l.append(kss)
        vs_l.append(vss)
        ps_l.append(pss)
    return (yp, ys, jnp.stack(kp_l), jnp.stack(vp_l), jnp.stack(pp_l), jnp.stack(ks_l), jnp.stack(vs_l), jnp.stack(ps_l))
```

```python
import functools

import jax
import jax.numpy as jnp
import numpy as np
from jax import lax
from jax.experimental import pallas as pl
from jax.experimental.pallas import tpu as pltpu

F32 = jnp.float32
BF16 = jnp.bfloat16
U32 = jnp.uint32
I32 = jnp.int32

D_MODEL = 1024
HEAD_DIM = 64
N_HEADS = 8
N_KV_HEADS = 2
GROUP = N_HEADS // N_KV_HEADS
ATTN_WIDTH = N_HEADS * HEAD_DIM
KV_WIDTH = N_KV_HEADS * HEAD_DIM
POOL_WIDTH = 512
POOL_WINDOWS = (2, 4, 8, 16)
POOL_GC = POOL_WIDTH // len(POOL_WINDOWS)
POOL_PREFIX = max(POOL_WINDOWS) - 1
POOL_HALO = POOL_PREFIX + 1
IN_WIDTH = ATTN_WIDTH + 2 * KV_WIDTH + POOL_WIDTH
WINDOW = 128
ROPE_THETA = 500000.0
ROT_DIM = HEAD_DIM // 4
N_EXPERTS = 32
TOP_K = 4
D_FF = 1024
SWIGLU_ALPHA = 1.702
SWIGLU_LIMIT = 7.0
PLE_DIM = 256
PAST_LEN = 16384
EPS = 1e-5
NEG_INF = -1e30

LANES = 128
Q_TILES = ATTN_WIDTH // LANES

MIX_ROWS = 256
SAMPLE_CHUNK = 16
ROUTE_ROWS = 512
MOE_BM = 256
GATHER_CHUNK = 1024
PLE_ROWS = 256
VMEM_LIMIT = 56 * 1024 * 1024


def _rms(x, g):
    return x * lax.rsqrt(jnp.mean(x * x, axis=-1, keepdims=True) + EPS) * g


def _mm(a, b, nt=False):
    dims = (((1,), (1 if nt else 0,)), ((), ()))
    if b.dtype == F32:
        return lax.dot_general(a.astype(F32), b, dims, preferred_element_type=F32, precision=lax.Precision.HIGHEST)
    return lax.dot_general(a.astype(BF16), b, dims, preferred_element_type=F32)


def _head_norm_rope(t, hmean, gain, cos, sin_a, sin_b):
    t = t * lax.rsqrt(_mm(t * t, hmean) + EPS) * gain
    return t * cos + pltpu.roll(t, ROT_DIM // 2, axis=1) * sin_a + pltpu.roll(t, LANES - ROT_DIM // 2, axis=1) * sin_b


def _softmax_pv(s, sink, v, ones):
    m = jnp.maximum(jnp.max(s, axis=-1, keepdims=True), sink)
    e = jnp.exp(s - m).astype(v.dtype)
    den = _mm(e, ones) + jnp.exp(sink - m)
    return _mm(e, v) / den


def _mixer_prompt_kernel(sinks_ref, x_ref, cos_ref, sa_ref, sb_ref, g_ref, win_ref, qn_ref, kn_ref, hm_ref,
                         wpool_ref, pscale_ref, wout_ref,
                         x1_ref, klast_ref, vlast_ref, ulast_ref,
                         kprev, vprev, uext, mix, *, row_offset):
    i = pl.program_id(0)
    rows = x_ref.shape[0]
    n_sub = rows // WINDOW
    cdt = win_ref.dtype
    row0 = row_offset + i * rows

    @pl.when(i == 0)
    def _():
        kprev[...] = jnp.zeros_like(kprev)
        vprev[...] = jnp.zeros_like(vprev)
        uext[0:POOL_HALO, :] = jnp.zeros((POOL_HALO, POOL_WIDTH), F32)

    x = x_ref[...]
    proj = _mm(_rms(x, g_ref[...]), win_ref[...])
    cos, sin_a, sin_b = cos_ref[...], sa_ref[...], sb_ref[...]
    hmean = hm_ref[...]

    k = _head_norm_rope(proj[:, ATTN_WIDTH:ATTN_WIDTH + KV_WIDTH], hmean, kn_ref[...], cos, sin_a, sin_b)
    v = proj[:, ATTN_WIDTH + KV_WIDTH:ATTN_WIDTH + 2 * KV_WIDTH]
    u = proj[:, ATTN_WIDTH + 2 * KV_WIDTH:]
    klast_ref[...] = k[rows - WINDOW:, :]
    vlast_ref[...] = v[rows - WINDOW:, :]
    ulast_ref[...] = u[rows - POOL_HALO:, :]
    k_c = k.astype(cdt)
    v_c = v.astype(cdt)

    lane = lax.broadcasted_iota(I32, (WINDOW, LANES), 1)
    left = lane < HEAD_DIM
    qi = lax.broadcasted_iota(I32, (WINDOW, 2 * WINDOW), 0)
    kj = lax.broadcasted_iota(I32, (WINDOW, 2 * WINDOW), 1)
    band = (kj - qi >= 1) & (kj - qi <= WINDOW)
    ones = jnp.ones((2 * WINDOW, LANES), cdt)
    scale = HEAD_DIM ** -0.5

    for c in range(n_sub):
        r0 = c * WINDOW
        if c == 0:
            k_cat = jnp.concatenate([kprev[...], k_c[0:WINDOW]], axis=0)
            v_cat = jnp.concatenate([vprev[...], v_c[0:WINDOW]], axis=0)
            mask = band & (kj + (row0 - WINDOW) >= 0)
        else:
            k_cat = k_c[r0 - WINDOW:r0 + WINDOW]
            v_cat = v_c[r0 - WINDOW:r0 + WINDOW]
            mask = band
        mask2 = jnp.concatenate([mask, mask], axis=0)
        for j in range(Q_TILES):
            qt = _head_norm_rope(proj[r0:r0 + WINDOW, j * LANES:(j + 1) * LANES], hmean, qn_ref[...],
                                 cos[r0:r0 + WINDOW], sin_a[r0:r0 + WINDOW], sin_b[r0:r0 + WINDOW]) * scale
            q2 = jnp.concatenate([jnp.where(left, qt, 0.0), jnp.where(left, 0.0, qt)], axis=0)
            s = jnp.where(mask2, _mm(q2, k_cat, nt=True), NEG_INF)
            sink = jnp.concatenate([jnp.full((WINDOW, 1), sinks_ref[j], F32),
                                    jnp.full((WINDOW, 1), sinks_ref[j + Q_TILES], F32)], axis=0)
            o = _softmax_pv(s, sink, v_cat, ones)
            mix[r0:r0 + WINDOW, j * LANES:(j + 1) * LANES] = jnp.where(left, o[0:WINDOW], o[WINDOW:]).astype(cdt)

    kprev[...] = k_c[rows - WINDOW:]
    vprev[...] = v_c[rows - WINDOW:]

    uext[POOL_HALO:POOL_HALO + rows, :] = u
    pos1 = (lax.broadcasted_iota(I32, (rows, 1), 0) + row0 + 1).astype(F32)
    for gi, w in enumerate(POOL_WINDOWS):
        cols = slice(gi * POOL_GC, (gi + 1) * POOL_GC)
        wsum = u[:, cols]
        for sft in range(1, w):
            wsum = wsum + uext[POOL_HALO - sft:POOL_HALO - sft + rows, cols]
        d = wsum / jnp.minimum(pos1, float(w)) - u[:, cols]
        y = _mm(d, wpool_ref[gi]) * pscale_ref[:, cols]
        mix[:, ATTN_WIDTH + gi * POOL_GC:ATTN_WIDTH + (gi + 1) * POOL_GC] = y.astype(cdt)
    uext[0:POOL_HALO, :] = u[rows - POOL_HALO:, :]

    x1_ref[...] = x + _mm(mix[...], wout_ref[...])


def _mixer_prompt(x, row_offset, sinks, tabs, g_attn, w_in, qn, kn, hmean, w_pool, pscale, w_out):
    t = x.shape[0]
    rows = MIX_ROWS
    cdt = w_in.dtype
    assert t % rows == 0 and rows % WINDOW == 0 and rows >= POOL_HALO
    const = lambda shape: pl.BlockSpec(shape, lambda i, *_: (0,) * len(shape))
    row_blk = lambda width: pl.BlockSpec((rows, width), lambda i, *_: (i, 0))
    grid_spec = pltpu.PrefetchScalarGridSpec(
        num_scalar_prefetch=1,
        grid=(t // rows,),
        in_specs=[row_blk(D_MODEL), row_blk(LANES), row_blk(LANES), row_blk(LANES),
                  const((1, D_MODEL)), const((D_MODEL, IN_WIDTH)), const((1, LANES)), const((1, LANES)),
                  const((LANES, LANES)), const((len(POOL_WINDOWS), POOL_GC, POOL_GC)), const((1, POOL_WIDTH)),
                  const((D_MODEL, D_MODEL))],
        out_specs=[row_blk(D_MODEL), const((WINDOW, KV_WIDTH)), const((WINDOW, KV_WIDTH)),
                   const((POOL_HALO, POOL_WIDTH))],
        scratch_shapes=[pltpu.VMEM((WINDOW, KV_WIDTH), cdt), pltpu.VMEM((WINDOW, KV_WIDTH), cdt),
                        pltpu.VMEM((POOL_HALO + rows, POOL_WIDTH), F32), pltpu.VMEM((rows, D_MODEL), cdt)],
    )
    return pl.pallas_call(
        functools.partial(_mixer_prompt_kernel, row_offset=row_offset),
        grid_spec=grid_spec,
        out_shape=[jax.ShapeDtypeStruct((t, D_MODEL), F32), jax.ShapeDtypeStruct((WINDOW, KV_WIDTH), F32),
                   jax.ShapeDtypeStruct((WINDOW, KV_WIDTH), F32), jax.ShapeDtypeStruct((POOL_HALO, POOL_WIDTH), F32)],
        compiler_params=pltpu.CompilerParams(dimension_semantics=("arbitrary",), vmem_limit_bytes=VMEM_LIMIT),
        name="mixer_prompt",
    )(sinks, x, *tabs, g_attn, w_in, qn, kn, hmean, w_pool, pscale, w_out)


def _mixer_sample_kernel(x_ref, ck_ref, cv_ref, st_ref, cos_ref, sa_ref, sb_ref, sink8_ref, g_ref, win_ref, qn_ref,
                         kn_ref, hm_ref, wpool_ref, pscale_ref, wout_ref,
                         x1_ref, nk_ref, nv_ref, nst_ref, o8, *, pos):
    nb = x_ref.shape[0]
    wb = ck_ref.shape[1]
    x = x_ref[...]
    h = _rms(x, g_ref[...]).astype(BF16)
    proj = jnp.dot(h, win_ref[...], preferred_element_type=F32)
    cos, sin_a, sin_b = cos_ref[...], sa_ref[...], sb_ref[...]
    hmean = hm_ref[...]
    k = _head_norm_rope(proj[:, ATTN_WIDTH:ATTN_WIDTH + KV_WIDTH], hmean, kn_ref[...], cos, sin_a, sin_b)
    v = proj[:, ATTN_WIDTH + KV_WIDTH:ATTN_WIDTH + 2 * KV_WIDTH]
    u = proj[:, ATTN_WIDTH + 2 * KV_WIDTH:]

    nk_ref[:, 0:wb - 1, :] = ck_ref[:, 1:wb, :]
    nv_ref[:, 0:wb - 1, :] = cv_ref[:, 1:wb, :]
    for b in range(nb):
        nk_ref[b, wb - 1:wb, :] = k[b:b + 1, :]
        nv_ref[b, wb - 1:wb, :] = v[b:b + 1, :]

    r8 = lax.broadcasted_iota(I32, (nb * 8, LANES), 0)
    lane8 = lax.broadcasted_iota(I32, (nb * 8, LANES), 1)
    keep = (lane8 < HEAD_DIM) == (r8 % 2 == 0)
    rep = (lax.broadcasted_iota(I32, (nb * 8, nb), 0) // 8 == lax.broadcasted_iota(I32, (nb * 8, nb), 1)).astype(BF16)
    q8 = jnp.zeros((nb * 8, LANES), F32)
    scale = HEAD_DIM ** -0.5
    for j in range(Q_TILES):
        qt = _head_norm_rope(proj[:, j * LANES:(j + 1) * LANES], hmean, qn_ref[...], cos, sin_a, sin_b) * scale
        qrep = jnp.dot(rep, qt.astype(BF16), preferred_element_type=F32)
        q8 = jnp.where(keep & ((r8 % 8) // 2 == j), qrep, q8)
    q8 = q8.astype(BF16)

    sink8 = sink8_ref[:, 0:1]
    ones_bf = jnp.ones((wb, LANES), BF16)
    assert pos >= wb - 1 and wb <= WINDOW
    for b in range(nb):
        kb = nk_ref[b].astype(BF16)
        vb = nv_ref[b].astype(BF16)
        s = lax.dot_general(q8[b * 8:(b + 1) * 8], kb, (((1,), (1,)), ((), ())), preferred_element_type=F32)
        o8[b * 8:(b + 1) * 8, :] = _softmax_pv(s, sink8, vb, ones_bf)
    o8m = jnp.where(keep, o8[...], 0.0).astype(BF16)

    a_tiles = []
    sel_r = lax.broadcasted_iota(I32, (nb, nb * 8), 1)
    sel_b = lax.broadcasted_iota(I32, (nb, nb * 8), 0)
    for j in range(Q_TILES):
        sel = ((sel_r // 8 == sel_b) & ((sel_r % 8) // 2 == j)).astype(BF16)
        a_tiles.append(jnp.dot(sel, o8m, preferred_element_type=F32))

    z_tiles = []
    for gi, w in enumerate(POOL_WINDOWS):
        cols = slice(gi * POOL_GC, (gi + 1) * POOL_GC)
        wsum = u[:, cols]
        for sft in range(1, w):
            wsum = wsum + st_ref[POOL_PREFIX - sft, :, cols]
        d = wsum / float(min(pos + 1, w)) - u[:, cols]
        z_tiles.append(jnp.dot(d.astype(BF16), wpool_ref[gi], preferred_element_type=F32) * pscale_ref[:, cols])
    nst_ref[0:POOL_PREFIX - 1] = st_ref[1:POOL_PREFIX]
    nst_ref[POOL_PREFIX - 1] = u

    mixv = jnp.concatenate(a_tiles + z_tiles, axis=1).astype(BF16)
    x1_ref[...] = x + jnp.dot(mixv, wout_ref[...], preferred_element_type=F32)


def _mixer_sample(x, ck, cv, st, pos, tabs, sink8, g_attn, w_in, qn, kn, hmean, w_pool, pscale, w_out):
    n, wb = ck.shape[0], ck.shape[1]
    nb = SAMPLE_CHUNK
    assert n % nb == 0
    const = lambda shape: pl.BlockSpec(shape, lambda i: (0,) * len(shape))
    cache_blk = pl.BlockSpec((nb, wb, KV_WIDTH), lambda i: (i, 0, 0))
    st_blk = pl.BlockSpec((POOL_PREFIX, nb, POOL_WIDTH), lambda i: (0, i, 0))
    x_blk = pl.BlockSpec((nb, D_MODEL), lambda i: (i, 0))
    return pl.pallas_call(
        functools.partial(_mixer_sample_kernel, pos=pos),
        grid=(n // nb,),
        in_specs=[x_blk, cache_blk, cache_blk, st_blk, const((1, LANES)), const((1, LANES)), const((1, LANES)),
                  const((8, LANES)), const((1, D_MODEL)), const((D_MODEL, IN_WIDTH)), const((1, LANES)),
                  const((1, LANES)), const((LANES, LANES)), const((len(POOL_WINDOWS), POOL_GC, POOL_GC)),
                  const((1, POOL_WIDTH)), const((D_MODEL, D_MODEL))],
        out_specs=[x_blk, cache_blk, cache_blk, st_blk],
        out_shape=[jax.ShapeDtypeStruct((n, D_MODEL), F32), jax.ShapeDtypeStruct(ck.shape, F32),
                   jax.ShapeDtypeStruct(cv.shape, F32), jax.ShapeDtypeStruct(st.shape, F32)],
        scratch_shapes=[pltpu.VMEM((nb * 8, LANES), F32)],
        compiler_params=pltpu.CompilerParams(dimension_semantics=("arbitrary",), vmem_limit_bytes=VMEM_LIMIT),
        name="mixer_sample",
    )(x, ck, cv, st, *tabs, sink8, g_attn, w_in, qn, kn, hmean, w_pool, pscale, w_out)


def _route_kernel(x1_ref, g_ref, wr_ref, br_ref, cnt_in_ref, idx_ref, gate_ref, pos_ref, cnt_ref, counts):
    i = pl.program_id(0)
    rows = x1_ref.shape[0]

    @pl.when(i == 0)
    def _():
        counts[...] = cnt_in_ref[...]

    h = _rms(x1_ref[...], g_ref[...])

    logits = lax.dot_general(wr_ref[...], h, (((1,), (1,)), ((), ())), preferred_element_type=F32,
                             precision=lax.Precision.HIGHEST) + br_ref[...]
    eid = lax.broadcasted_iota(I32, (N_EXPERTS, rows), 0).astype(F32)
    work = logits
    vals, hots = [], []
    for kk in range(TOP_K):
        m = jnp.max(work, axis=0, keepdims=True)
        first = jnp.min(jnp.where(work == m, eid, float(N_EXPERTS)), axis=0, keepdims=True)
        hot = eid == first
        work = jnp.where(hot, -jnp.inf, work)
        vals.append(m)
        hots.append(hot)
        idx_ref[kk:kk + 1, :] = first.astype(I32)
    es = [jnp.exp(vv - vals[0]) for vv in vals]
    den = es[0] + es[1] + es[2] + es[3]
    for kk in range(TOP_K):
        gate_ref[kk:kk + 1, :] = es[kk] / den

    chosen = hots[0] | hots[1] | hots[2] | hots[3]
    before = (lax.broadcasted_iota(I32, (rows, rows), 0) < lax.broadcasted_iota(I32, (rows, rows), 1)).astype(BF16)
    rank = jnp.dot(chosen.astype(BF16), before, preferred_element_type=F32) + counts[...]
    for kk in range(TOP_K):
        pos_ref[kk:kk + 1, :] = jnp.sum(jnp.where(hots[kk], rank, 0.0), axis=0, keepdims=True).astype(I32)
    counts[...] = counts[...] + jnp.sum(chosen.astype(F32), axis=1, keepdims=True)
    cnt_ref[...] = counts[...]


def _route(x1, g_ffn, wr_t, br, cnt_in, rows):
    n = x1.shape[0]
    assert n % rows == 0
    const = lambda shape: pl.BlockSpec(shape, lambda i: (0,) * len(shape))
    tok_blk = pl.BlockSpec((TOP_K, rows), lambda i: (0, i))
    return pl.pallas_call(
        _route_kernel,
        grid=(n // rows,),
        in_specs=[pl.BlockSpec((rows, D_MODEL), lambda i: (i, 0)), const((1, D_MODEL)), const((N_EXPERTS, D_MODEL)),
                  const((N_EXPERTS, 1)), const((N_EXPERTS, 1))],
        out_specs=[tok_blk, tok_blk, tok_blk, const((N_EXPERTS, 1))],
        out_shape=[jax.ShapeDtypeStruct((TOP_K, n), I32),
                   jax.ShapeDtypeStruct((TOP_K, n), F32), jax.ShapeDtypeStruct((TOP_K, n), I32),
                   jax.ShapeDtypeStruct((N_EXPERTS, 1), F32)],
        scratch_shapes=[pltpu.VMEM((N_EXPERTS, 1), F32)],
        compiler_params=pltpu.CompilerParams(dimension_semantics=("arbitrary",), vmem_limit_bytes=VMEM_LIMIT),
        name="route",
    )(x1, g_ffn, wr_t, br, cnt_in)


def _gather_kernel(idx_ref, a_ref, b_ref, out_ref, sems, *, chunk, n_a):
    i = pl.program_id(0)
    n_steps = pl.num_programs(0)
    slot = i % 2
    base = i * chunk

    def issue(r, carry):
        tok = idx_ref[0, r]

        @pl.when(tok < n_a)
        def _():
            pltpu.make_async_copy(a_ref.at[pl.ds(tok, 1)], out_ref.at[pl.ds(base + r, 1)], sems.at[slot]).start()

        @pl.when(tok >= n_a)
        def _():
            pltpu.make_async_copy(b_ref.at[pl.ds(tok - n_a, 1)], out_ref.at[pl.ds(base + r, 1)], sems.at[slot]).start()

        return carry

    lax.fori_loop(0, chunk, issue, 0)

    def wait_chunk(s):
        pltpu.make_async_copy(out_ref.at[pl.ds(0, chunk)], out_ref.at[pl.ds(0, chunk)], sems.at[s]).wait()

    @pl.when(i > 0)
    def _():
        wait_chunk(1 - slot)

    @pl.when(i == n_steps - 1)
    def _():
        wait_chunk(slot)


def _gather_rows(src_a, src_b, idx):
    n_out = idx.shape[0]
    chunk = GATHER_CHUNK
    assert n_out % chunk == 0 and src_a.shape[1:] == src_b.shape[1:] and src_a.dtype == src_b.dtype
    width = src_a.shape[1]
    any_spec = pl.BlockSpec(memory_space=pl.ANY)
    return pl.pallas_call(
        functools.partial(_gather_kernel, chunk=chunk, n_a=src_a.shape[0]),
        grid=(n_out // chunk,),
        in_specs=[pl.BlockSpec((1, chunk), lambda i: (0, i), memory_space=pltpu.SMEM), any_spec, any_spec],
        out_specs=any_spec,
        out_shape=jax.ShapeDtypeStruct((n_out, width), src_a.dtype),
        scratch_shapes=[pltpu.SemaphoreType.DMA((2,))],
        compiler_params=pltpu.CompilerParams(dimension_semantics=("arbitrary",), has_side_effects=True),
        name="gather_rows",
    )(idx.reshape(1, n_out), src_a, src_b)


def _moe_kernel(blk_e_ref, nblk_ref, xs_ref, g_ref, wgu_ref, bgu_ref, wd_ref, bd_ref, y_ref, wgu_bf, wd_bf):
    i = pl.program_id(0)
    e = blk_e_ref[i]
    used = i < nblk_ref[0]
    new_expert = (i == 0) | (blk_e_ref[jnp.maximum(i - 1, 0)] != e)

    @pl.when(used & new_expert)
    def _():
        wgu_bf[...] = wgu_ref[0].astype(BF16)
        wd_bf[...] = wd_ref[0].astype(BF16)

    @pl.when(used)
    def _():
        xb = _rms(xs_ref[...], g_ref[...]).astype(BF16)
        gu =jnp.dot(xb, wgu_bf[...], preferred_element_type=F32) + bgu_ref[0]
        g = jnp.minimum(gu[:, :D_FF], SWIGLU_LIMIT)
        up = jnp.clip(gu[:, D_FF:], -SWIGLU_LIMIT, SWIGLU_LIMIT)
        act = (up + 1.0) * (g * jax.nn.sigmoid(SWIGLU_ALPHA * g))
        y_ref[...] = jnp.dot(act.astype(BF16), wd_bf[...], preferred_element_type=F32) + bd_ref[0]

    @pl.when(jnp.logical_not(used))
    def _():
        y_ref[...] = jnp.zeros_like(y_ref)


def _moe(xs, blk_e, nblk, g_ffn, w_gu, b_gu, w_d, b_d):
    n_rows = xs.shape[0]
    bm = MOE_BM
    assert n_rows % bm == 0
    grid_spec = pltpu.PrefetchScalarGridSpec(
        num_scalar_prefetch=2,
        grid=(n_rows // bm,),
        in_specs=[pl.BlockSpec((bm, D_MODEL), lambda i, be, nb: (i, 0)),
                  pl.BlockSpec((1, D_MODEL), lambda i, be, nb: (0, 0)),
                  pl.BlockSpec((1, D_MODEL, 2 * D_FF), lambda i, be, nb: (be[i], 0, 0)),
                  pl.BlockSpec((1, 1, 2 * D_FF), lambda i, be, nb: (be[i], 0, 0)),
                  pl.BlockSpec((1, D_FF, D_MODEL), lambda i, be, nb: (be[i], 0, 0)),
                  pl.BlockSpec((1, 1, D_MODEL), lambda i, be, nb: (be[i], 0, 0))],
        out_specs=pl.BlockSpec((bm, D_MODEL), lambda i, be, nb: (i, 0)),
        scratch_shapes=[pltpu.VMEM((D_MODEL, 2 * D_FF), BF16), pltpu.VMEM((D_FF, D_MODEL), BF16)],
    )
    return pl.pallas_call(
        _moe_kernel,
        grid_spec=grid_spec,
        out_shape=jax.ShapeDtypeStruct((n_rows, D_MODEL), F32),
        compiler_params=pltpu.CompilerParams(dimension_semantics=("arbitrary",), vmem_limit_bytes=VMEM_LIMIT),
        name="moe_experts",
    )(blk_e, nblk, xs, g_ffn, w_gu, b_gu.reshape(N_EXPERTS, 1, 2 * D_FF), w_d, b_d.reshape(N_EXPERTS, 1, D_MODEL))


def _ple_kernel(x1_ref, y4_ref, gates_ref, p_ref, g_ref, wg_ref, wp_ref, out_ref):
    x2 = x1_ref[...]
    gates = gates_ref[...]
    for kk in range(TOP_K):
        x2 = x2 + y4_ref[kk] * gates[:, kk:kk + 1]
    hp = _rms(x2, g_ref[...]).astype(BF16)
    gate = jax.nn.sigmoid(jnp.dot(hp, wg_ref[...], preferred_element_type=F32))
    pp = jnp.dot(p_ref[...].astype(BF16), wp_ref[...], preferred_element_type=F32)
    out_ref[...] = x2 + gate * pp


def _ple(x1, y4, gates_t, p, g_ple, w_gate, w_proj, rows):
    n = x1.shape[0]
    assert n % rows == 0
    const = lambda shape: pl.BlockSpec(shape, lambda i: (0,) * len(shape))
    return pl.pallas_call(
        _ple_kernel,
        grid=(n // rows,),
        in_specs=[pl.BlockSpec((rows, D_MODEL), lambda i: (i, 0)),
                  pl.BlockSpec((TOP_K, rows, D_MODEL), lambda i: (0, i, 0)),
                  pl.BlockSpec((rows, TOP_K), lambda i: (i, 0)),
                  pl.BlockSpec((rows, PLE_DIM), lambda i: (i, 0)),
                  const((1, D_MODEL)), const((D_MODEL, D_MODEL)), const((PLE_DIM, D_MODEL))],
        out_specs=pl.BlockSpec((rows, D_MODEL), lambda i: (i, 0)),
        out_shape=jax.ShapeDtypeStruct((n, D_MODEL), F32),
        compiler_params=pltpu.CompilerParams(dimension_semantics=("arbitrary",), vmem_limit_bytes=VMEM_LIMIT),
        name="combine_ple",
    )(x1, y4, gates_t, p, g_ple, w_gate, w_proj)


def _q_perm():
    perm = np.empty((ATTN_WIDTH,), np.int32)
    for j in range(Q_TILES):
        for s in range(2):
            for d in range(HEAD_DIM):
                perm[j * LANES + s * HEAD_DIM + d] = (j + Q_TILES * s) * HEAD_DIM + d
    return perm


def _rope_tables(pos):
    half = ROT_DIM // 2
    inv = ROPE_THETA ** (-jnp.arange(half, dtype=F32) / half)
    ang = pos.astype(F32)[:, None] * inv
    cos, sin = jnp.cos(ang), jnp.sin(ang)
    n = pos.shape[0]
    zeros = jnp.zeros((n, HEAD_DIM - ROT_DIM), F32)
    cos_h = jnp.concatenate([cos, cos, jnp.ones_like(zeros)], axis=1)
    sa_h = jnp.concatenate([jnp.zeros_like(sin), sin, zeros], axis=1)
    sb_h = jnp.concatenate([-sin, jnp.zeros_like(sin), zeros], axis=1)
    return tuple(jnp.concatenate([t, t], axis=1) for t in (cos_h, sa_h, sb_h))


def _layer(xp, xs, ck, cv, st, pp, ps, past_len, refine_tail,
           norm_attn, w_in, q_norm, k_norm, attn_sinks, w_pool, pool_scale, w_out,
           norm_ffn, w_router, b_router, w_gate_up, b_gate_up, w_down, b_down,
           norm_ple, w_ple_gate, w_ple_proj):
    t, ns = xp.shape[0], xs.shape[0]
    perm = _q_perm()
    w_in_p = jnp.concatenate([w_in[:, perm], w_in[:, ATTN_WIDTH:]], axis=1)
    w_out_p = jnp.concatenate([w_out[perm], w_out[ATTN_WIDTH:]], axis=0)
    g_attn = norm_attn.reshape(1, D_MODEL)
    qn = jnp.tile(q_norm, 2).reshape(1, LANES)
    kn = jnp.tile(k_norm, 2).reshape(1, LANES)
    lane = np.arange(LANES)
    hmean = jnp.asarray((lane[:, None] // HEAD_DIM == lane[None, :] // HEAD_DIM) / HEAD_DIM, F32)
    pscale = pool_scale.reshape(1, POOL_WIDTH)
    sink8 = jnp.broadcast_to(attn_sinks[np.array([j + Q_TILES * s for j in range(Q_TILES) for s in range(2)])][:, None],
                             (8, LANES))

    mix_args = (g_attn, w_in_p.astype(BF16), qn, kn, hmean.astype(BF16), w_pool.astype(BF16), pscale,
                w_out_p.astype(BF16))
    tabs = _rope_tables(jnp.arange(t))
    x1p, nk_p, nv_p, nu_p = _mixer_prompt(xp, 0, attn_sinks, tabs, *mix_args)
    if refine_tail:
        tail = 2 * MIX_ROWS
        hi_args = (g_attn, w_in_p, qn, kn, hmean, w_pool, pscale, w_out_p)
        x1_hi, nk_p, nv_p, nu_p = _mixer_prompt(xp[t - tail:], t - tail, attn_sinks,
                                                tuple(tb[t - tail:] for tb in tabs), *hi_args)
        x1p = lax.dynamic_update_slice(x1p, x1_hi[tail - MIX_ROWS:], (t - MIX_ROWS, 0))
    st_t = jnp.transpose(st, (1, 0, 2))
    x1s, nk_s, nv_s, nst_t = _mixer_sample(xs, ck.reshape(ns, -1, KV_WIDTH), cv.reshape(ns, -1, KV_WIDTH), st_t,
                                           past_len, _rope_tables(jnp.full((1,), past_len)), sink8, *mix_args)

    g_ffn = norm_ffn.reshape(1, D_MODEL)
    wr_t = w_router.T
    br = b_router.reshape(N_EXPERTS, 1)
    idx_p, gate_p, pos_p, cnt_p = _route(x1p, g_ffn, wr_t, br, jnp.zeros((N_EXPERTS, 1), F32), ROUTE_ROWS)
    idx_s, gate_s, pos_s, cnt = _route(x1s, g_ffn, wr_t, br, cnt_p, ns)

    n_tok = t + ns
    bm = MOE_BM
    n_rows = -(-(n_tok * TOP_K + N_EXPERTS * (bm - 1)) // GATHER_CHUNK) * GATHER_CHUNK
    n_blocks = n_rows // bm
    counts = cnt.reshape(N_EXPERTS).astype(I32)
    padded = (counts + bm - 1) // bm * bm
    pad_end = jnp.cumsum(padded)
    pad_start = pad_end - padded
    idx_all = jnp.concatenate([idx_p, idx_s], axis=1)
    dest = pad_start[idx_all] + jnp.concatenate([pos_p, pos_s], axis=1)
    tok_ids = jnp.broadcast_to(jnp.arange(n_tok, dtype=I32)[None, :], (TOP_K, n_tok))
    tok_of_row = jnp.zeros((n_rows,), I32).at[dest.reshape(-1)].set(tok_ids.reshape(-1))
    blk_start = jnp.arange(n_blocks, dtype=I32) * bm
    blk_e = jnp.minimum(jnp.sum((pad_end[None, :] <= blk_start[:, None]).astype(I32), axis=1), N_EXPERTS - 1)
    nblk = (pad_end[-1:] // bm).astype(I32)

    xs_rows = _gather_rows(x1p, x1s, tok_of_row)
    y = _moe(xs_rows, blk_e, nblk, g_ffn, w_gate_up, b_gate_up, w_down, b_down)

    def pad_idx(d):
        flat = d.reshape(-1)
        pad = -flat.shape[0] % GATHER_CHUNK
        return jnp.concatenate([flat, jnp.zeros((pad,), I32)]), flat.shape[0]

    di_p, n_p4 = pad_idx(dest[:, :t])
    di_s, n_s4 = pad_idx(dest[:, t:])
    y4p = _gather_rows(y, y, di_p)[:n_p4].reshape(TOP_K, t, D_MODEL)
    y4s = _gather_rows(y, y, di_s)[:n_s4].reshape(TOP_K, ns, D_MODEL)

    g_ple = norm_ple.reshape(1, D_MODEL)
    wg_bf = w_ple_gate.astype(BF16)
    wp_bf = w_ple_proj.astype(BF16)
    yp = _ple(x1p, y4p, gate_p.T, pp, g_ple, wg_bf, wp_bf, PLE_ROWS)
    ys = _ple(x1s, y4s, gate_s.T, ps, g_ple, wg_bf, wp_bf, ns)
    new_pool_s = jnp.transpose(nst_t, (1, 0, 2))
    return yp, ys, nk_p, nv_p, nu_p[POOL_HALO - POOL_PREFIX:], nk_s, nv_s, new_pool_s


def kernel(x_prompt, x_sample, cache_k, cache_v, state_pool, p_prompt, p_sample, norm_attn, w_in, q_norm, k_norm,
           attn_sinks, w_pool, pool_scale, w_out, norm_ffn, w_router, b_router, w_gate_up, b_gate_up, w_down, b_down,
           norm_ple, w_ple_gate, w_ple_proj):
    depth = norm_attn.shape[0]
    batch, seq, d = x_prompt.shape
    ns, dec_seq, _ = x_sample.shape
    wb = cache_k.shape[2]
    assert batch == 1 and dec_seq == 1 and d == D_MODEL and wb == WINDOW
    assert cache_k.shape[3:] == (N_KV_HEADS, HEAD_DIM) and state_pool.shape[2:] == (POOL_PREFIX, POOL_WIDTH)
    past_len = PAST_LEN
    yp = x_prompt.reshape(seq, d)
    ys = x_sample.reshape(ns, d)
    outs = [[] for _ in range(6)]
    for i in range(depth):
        res = _layer(yp, ys, cache_k[i], cache_v[i], state_pool[i], p_prompt[i, 0], p_sample[i, :, 0], past_len,
                     i < depth - 1,
                     norm_attn[i], w_in[i], q_norm[i], k_norm[i], attn_sinks[i], w_pool[i], pool_scale[i], w_out[i],
                     norm_ffn[i], w_router[i], b_router[i], w_gate_up[i], b_gate_up[i], w_down[i], b_down[i],
                     norm_ple[i], w_ple_gate[i], w_ple_proj[i])
        yp, ys = res[0], res[1]
        kv_shape = (1, WINDOW, N_KV_HEADS, HEAD_DIM)
        outs[0].append(res[2].reshape(kv_shape))
        outs[1].append(res[3].reshape(kv_shape))
        outs[2].append(res[4].reshape(1, POOL_PREFIX, POOL_WIDTH))
        outs[3].append(res[5].reshape(ns, wb, N_KV_HEADS, HEAD_DIM))
        outs[4].append(res[6].reshape(ns, wb, N_KV_HEADS, HEAD_DIM))
        outs[5].append(res[7])
    return (yp.reshape(batch, seq, d), ys.reshape(ns, dec_seq, d)) + tuple(jnp.stack(o) for o in outs)
```

```python
import functools

import jax
import jax.numpy as jnp
import numpy as np
from jax import lax
from jax.experimental import pallas as pl
from jax.experimental.pallas import tpu as pltpu
from jax.experimental.pallas import tpu_sc as plsc

F32 = jnp.float32
BF16 = jnp.bfloat16
U32 = jnp.uint32
I32 = jnp.int32

D_MODEL = 1024
HEAD_DIM = 64
N_HEADS = 8
N_KV_HEADS = 2
GROUP = N_HEADS // N_KV_HEADS
ATTN_WIDTH = N_HEADS * HEAD_DIM
KV_WIDTH = N_KV_HEADS * HEAD_DIM
POOL_WIDTH = 512
POOL_WINDOWS = (2, 4, 8, 16)
POOL_GC = POOL_WIDTH // len(POOL_WINDOWS)
POOL_PREFIX = max(POOL_WINDOWS) - 1
POOL_HALO = POOL_PREFIX + 1
IN_WIDTH = ATTN_WIDTH + 2 * KV_WIDTH + POOL_WIDTH
WINDOW = 128
ROPE_THETA = 500000.0
ROT_DIM = HEAD_DIM // 4
N_EXPERTS = 32
TOP_K = 4
D_FF = 1024
SWIGLU_ALPHA = 1.702
SWIGLU_LIMIT = 7.0
PLE_DIM = 256
PAST_LEN = 16384
EPS = 1e-5
NEG_INF = -1e30

LANES = 128
Q_TILES = ATTN_WIDTH // LANES

MIX_ROWS = 256
SAMPLE_CHUNK = 16
ROUTE_ROWS = 384
MOE_BM = 256
SC_CORES = 2
SC_SUBCORES = 16
SC_WORKERS = SC_CORES * SC_SUBCORES
SC_CHUNK = 32
PLE_ROWS = 256
VMEM_LIMIT = 56 * 1024 * 1024


def _rms(x, g):
    return x * lax.rsqrt(jnp.mean(x * x, axis=-1, keepdims=True) + EPS) * g


def _mm(a, b, nt=False):
    dims = (((1,), (1 if nt else 0,)), ((), ()))
    if b.dtype == F32:
        return lax.dot_general(a.astype(F32), b, dims, preferred_element_type=F32, precision=lax.Precision.HIGHEST)
    return lax.dot_general(a.astype(BF16), b, dims, preferred_element_type=F32)


def _head_norm_rope(t, hmean, gain, cos, sin_a, sin_b):
    t = t * lax.rsqrt(_mm(t * t, hmean) + EPS) * gain
    return t * cos + pltpu.roll(t, ROT_DIM // 2, axis=1) * sin_a + pltpu.roll(t, LANES - ROT_DIM // 2, axis=1) * sin_b


def _softmax_pv(s, sink, v, ones):
    m = jnp.maximum(jnp.max(s, axis=-1, keepdims=True), sink)
    e = jnp.exp(s - m).astype(v.dtype)
    den = _mm(e, ones) + jnp.exp(sink - m)
    return _mm(e, v) / den


def _mixer_prompt_kernel(sinks_ref, x_ref, cos_ref, sa_ref, sb_ref, g_ref, win_ref, qn_ref, kn_ref, hm_ref,
                         wpool_ref, pscale_ref, wout_ref,
                         x1_ref, klast_ref, vlast_ref, ulast_ref,
                         kprev, vprev, uext, mix, *, row_offset):
    i = pl.program_id(0)
    rows = x_ref.shape[0]
    n_sub = rows // WINDOW
    cdt = win_ref.dtype
    row0 = row_offset + i * rows

    @pl.when(i == 0)
    def _():
        kprev[...] = jnp.zeros_like(kprev)
        vprev[...] = jnp.zeros_like(vprev)
        uext[0:POOL_HALO, :] = jnp.zeros((POOL_HALO, POOL_WIDTH), F32)

    x = x_ref[...]
    proj = _mm(_rms(x, g_ref[...]), win_ref[...])
    cos, sin_a, sin_b = cos_ref[...], sa_ref[...], sb_ref[...]
    hmean = hm_ref[...]

    k = _head_norm_rope(proj[:, ATTN_WIDTH:ATTN_WIDTH + KV_WIDTH], hmean, kn_ref[...], cos, sin_a, sin_b)
    v = proj[:, ATTN_WIDTH + KV_WIDTH:ATTN_WIDTH + 2 * KV_WIDTH]
    u = proj[:, ATTN_WIDTH + 2 * KV_WIDTH:]
    klast_ref[...] = k[rows - WINDOW:, :]
    vlast_ref[...] = v[rows - WINDOW:, :]
    ulast_ref[...] = u[rows - POOL_HALO:, :]
    k_c = k.astype(cdt)
    v_c = v.astype(cdt)

    lane = lax.broadcasted_iota(I32, (WINDOW, LANES), 1)
    left = lane < HEAD_DIM
    qi = lax.broadcasted_iota(I32, (WINDOW, 2 * WINDOW), 0)
    kj = lax.broadcasted_iota(I32, (WINDOW, 2 * WINDOW), 1)
    band = (kj - qi >= 1) & (kj - qi <= WINDOW)
    ones = jnp.ones((2 * WINDOW, LANES), cdt)
    scale = HEAD_DIM ** -0.5

    for c in range(n_sub):
        r0 = c * WINDOW
        if c == 0:
            k_cat = jnp.concatenate([kprev[...], k_c[0:WINDOW]], axis=0)
            v_cat = jnp.concatenate([vprev[...], v_c[0:WINDOW]], axis=0)
            mask = band & (kj + (row0 - WINDOW) >= 0)
        else:
            k_cat = k_c[r0 - WINDOW:r0 + WINDOW]
            v_cat = v_c[r0 - WINDOW:r0 + WINDOW]
            mask = band
        mask2 = jnp.concatenate([mask, mask], axis=0)
        for j in range(Q_TILES):
            qt = _head_norm_rope(proj[r0:r0 + WINDOW, j * LANES:(j + 1) * LANES], hmean, qn_ref[...],
                                 cos[r0:r0 + WINDOW], sin_a[r0:r0 + WINDOW], sin_b[r0:r0 + WINDOW]) * scale
            q2 = jnp.concatenate([jnp.where(left, qt, 0.0), jnp.where(left, 0.0, qt)], axis=0)
            s = jnp.where(mask2, _mm(q2, k_cat, nt=True), NEG_INF)
            sink = jnp.concatenate([jnp.full((WINDOW, 1), sinks_ref[j], F32),
                                    jnp.full((WINDOW, 1), sinks_ref[j + Q_TILES], F32)], axis=0)
            o = _softmax_pv(s, sink, v_cat, ones)
            mix[r0:r0 + WINDOW, j * LANES:(j + 1) * LANES] = jnp.where(left, o[0:WINDOW], o[WINDOW:]).astype(cdt)

    kprev[...] = k_c[rows - WINDOW:]
    vprev[...] = v_c[rows - WINDOW:]

    uext[POOL_HALO:POOL_HALO + rows, :] = u
    pos1 = (lax.broadcasted_iota(I32, (rows, 1), 0) + row0 + 1).astype(F32)
    for gi, w in enumerate(POOL_WINDOWS):
        cols = slice(gi * POOL_GC, (gi + 1) * POOL_GC)
        wsum = u[:, cols]
        for sft in range(1, w):
            wsum = wsum + uext[POOL_HALO - sft:POOL_HALO - sft + rows, cols]
        d = wsum / jnp.minimum(pos1, float(w)) - u[:, cols]
        y = _mm(d, wpool_ref[gi]) * pscale_ref[:, cols]
        mix[:, ATTN_WIDTH + gi * POOL_GC:ATTN_WIDTH + (gi + 1) * POOL_GC] = y.astype(cdt)
    uext[0:POOL_HALO, :] = u[rows - POOL_HALO:, :]

    x1_ref[...] = x + _mm(mix[...], wout_ref[...])


def _mixer_prompt(x, row_offset, sinks, tabs, g_attn, w_in, qn, kn, hmean, w_pool, pscale, w_out):
    t = x.shape[0]
    rows = MIX_ROWS
    cdt = w_in.dtype
    assert t % rows == 0 and rows % WINDOW == 0 and rows >= POOL_HALO
    const = lambda shape: pl.BlockSpec(shape, lambda i, *_: (0,) * len(shape))
    row_blk = lambda width: pl.BlockSpec((rows, width), lambda i, *_: (i, 0))
    grid_spec = pltpu.PrefetchScalarGridSpec(
        num_scalar_prefetch=1,
        grid=(t // rows,),
        in_specs=[row_blk(D_MODEL), row_blk(LANES), row_blk(LANES), row_blk(LANES),
                  const((1, D_MODEL)), const((D_MODEL, IN_WIDTH)), const((1, LANES)), const((1, LANES)),
                  const((LANES, LANES)), const((len(POOL_WINDOWS), POOL_GC, POOL_GC)), const((1, POOL_WIDTH)),
                  const((D_MODEL, D_MODEL))],
        out_specs=[row_blk(D_MODEL), const((WINDOW, KV_WIDTH)), const((WINDOW, KV_WIDTH)),
                   const((POOL_HALO, POOL_WIDTH))],
        scratch_shapes=[pltpu.VMEM((WINDOW, KV_WIDTH), cdt), pltpu.VMEM((WINDOW, KV_WIDTH), cdt),
                        pltpu.VMEM((POOL_HALO + rows, POOL_WIDTH), F32), pltpu.VMEM((rows, D_MODEL), cdt)],
    )
    return pl.pallas_call(
        functools.partial(_mixer_prompt_kernel, row_offset=row_offset),
        grid_spec=grid_spec,
        out_shape=[jax.ShapeDtypeStruct((t, D_MODEL), F32), jax.ShapeDtypeStruct((WINDOW, KV_WIDTH), F32),
                   jax.ShapeDtypeStruct((WINDOW, KV_WIDTH), F32), jax.ShapeDtypeStruct((POOL_HALO, POOL_WIDTH), F32)],
        compiler_params=pltpu.CompilerParams(dimension_semantics=("arbitrary",), vmem_limit_bytes=VMEM_LIMIT),
        name="mixer_prompt",
    )(sinks, x, *tabs, g_attn, w_in, qn, kn, hmean, w_pool, pscale, w_out)


def _mixer_sample_kernel(x_ref, ck_ref, cv_ref, st_ref, cos_ref, sa_ref, sb_ref, sink8_ref, g_ref, win_ref, qn_ref,
                         kn_ref, hm_ref, wpool_ref, pscale_ref, wout_ref,
                         x1_ref, nk_ref, nv_ref, nst_ref, o8, *, pos):
    nb = x_ref.shape[0]
    wb = ck_ref.shape[1]
    x = x_ref[...]
    h = _rms(x, g_ref[...]).astype(BF16)
    proj = jnp.dot(h, win_ref[...], preferred_element_type=F32)
    cos, sin_a, sin_b = cos_ref[...], sa_ref[...], sb_ref[...]
    hmean = hm_ref[...]
    k = _head_norm_rope(proj[:, ATTN_WIDTH:ATTN_WIDTH + KV_WIDTH], hmean, kn_ref[...], cos, sin_a, sin_b)
    v = proj[:, ATTN_WIDTH + KV_WIDTH:ATTN_WIDTH + 2 * KV_WIDTH]
    u = proj[:, ATTN_WIDTH + 2 * KV_WIDTH:]

    nk_ref[:, 0:wb - 1, :] = ck_ref[:, 1:wb, :]
    nv_ref[:, 0:wb - 1, :] = cv_ref[:, 1:wb, :]
    for b in range(nb):
        nk_ref[b, wb - 1:wb, :] = k[b:b + 1, :]
        nv_ref[b, wb - 1:wb, :] = v[b:b + 1, :]

    r8 = lax.broadcasted_iota(I32, (nb * 8, LANES), 0)
    lane8 = lax.broadcasted_iota(I32, (nb * 8, LANES), 1)
    keep = (lane8 < HEAD_DIM) == (r8 % 2 == 0)
    rep = (lax.broadcasted_iota(I32, (nb * 8, nb), 0) // 8 == lax.broadcasted_iota(I32, (nb * 8, nb), 1)).astype(BF16)
    q8 = jnp.zeros((nb * 8, LANES), F32)
    scale = HEAD_DIM ** -0.5
    for j in range(Q_TILES):
        qt = _head_norm_rope(proj[:, j * LANES:(j + 1) * LANES], hmean, qn_ref[...], cos, sin_a, sin_b) * scale
        qrep = jnp.dot(rep, qt.astype(BF16), preferred_element_type=F32)
        q8 = jnp.where(keep & ((r8 % 8) // 2 == j), qrep, q8)
    q8 = q8.astype(BF16)

    sink8 = sink8_ref[:, 0:1]
    ones_bf = jnp.ones((wb, LANES), BF16)
    assert pos >= wb - 1 and wb <= WINDOW
    for b in range(nb):
        kb = nk_ref[b].astype(BF16)
        vb = nv_ref[b].astype(BF16)
        s = lax.dot_general(q8[b * 8:(b + 1) * 8], kb, (((1,), (1,)), ((), ())), preferred_element_type=F32)
        o8[b * 8:(b + 1) * 8, :] = _softmax_pv(s, sink8, vb, ones_bf)
    o8m = jnp.where(keep, o8[...], 0.0).astype(BF16)

    a_tiles = []
    sel_r = lax.broadcasted_iota(I32, (nb, nb * 8), 1)
    sel_b = lax.broadcasted_iota(I32, (nb, nb * 8), 0)
    for j in range(Q_TILES):
        sel = ((sel_r // 8 == sel_b) & ((sel_r % 8) // 2 == j)).astype(BF16)
        a_tiles.append(jnp.dot(sel, o8m, preferred_element_type=F32))

    z_tiles = []
    for gi, w in enumerate(POOL_WINDOWS):
        cols = slice(gi * POOL_GC, (gi + 1) * POOL_GC)
        wsum = u[:, cols]
        for sft in range(1, w):
            wsum = wsum + st_ref[POOL_PREFIX - sft, :, cols]
        d = wsum / float(min(pos + 1, w)) - u[:, cols]
        z_tiles.append(jnp.dot(d.astype(BF16), wpool_ref[gi], preferred_element_type=F32) * pscale_ref[:, cols])
    nst_ref[0:POOL_PREFIX - 1] = st_ref[1:POOL_PREFIX]
    nst_ref[POOL_PREFIX - 1] = u

    mixv = jnp.concatenate(a_tiles + z_tiles, axis=1).astype(BF16)
    x1_ref[...] = x + jnp.dot(mixv, wout_ref[...], preferred_element_type=F32)


def _mixer_sample(x, ck, cv, st, pos, tabs, sink8, g_attn, w_in, qn, kn, hmean, w_pool, pscale, w_out):
    n, wb = ck.shape[0], ck.shape[1]
    nb = SAMPLE_CHUNK
    assert n % nb == 0
    const = lambda shape: pl.BlockSpec(shape, lambda i: (0,) * len(shape))
    cache_blk = pl.BlockSpec((nb, wb, KV_WIDTH), lambda i: (i, 0, 0))
    st_blk = pl.BlockSpec((POOL_PREFIX, nb, POOL_WIDTH), lambda i: (0, i, 0))
    x_blk = pl.BlockSpec((nb, D_MODEL), lambda i: (i, 0))
    return pl.pallas_call(
        functools.partial(_mixer_sample_kernel, pos=pos),
        grid=(n // nb,),
        in_specs=[x_blk, cache_blk, cache_blk, st_blk, const((1, LANES)), const((1, LANES)), const((1, LANES)),
                  const((8, LANES)), const((1, D_MODEL)), const((D_MODEL, IN_WIDTH)), const((1, LANES)),
                  const((1, LANES)), const((LANES, LANES)), const((len(POOL_WINDOWS), POOL_GC, POOL_GC)),
                  const((1, POOL_WIDTH)), const((D_MODEL, D_MODEL))],
        out_specs=[x_blk, cache_blk, cache_blk, st_blk],
        out_shape=[jax.ShapeDtypeStruct((n, D_MODEL), F32), jax.ShapeDtypeStruct(ck.shape, F32),
                   jax.ShapeDtypeStruct(cv.shape, F32), jax.ShapeDtypeStruct(st.shape, F32)],
        scratch_shapes=[pltpu.VMEM((nb * 8, LANES), F32)],
        compiler_params=pltpu.CompilerParams(dimension_semantics=("arbitrary",), vmem_limit_bytes=VMEM_LIMIT),
        name="mixer_sample",
    )(x, ck, cv, st, *tabs, sink8, g_attn, w_in, qn, kn, hmean, w_pool, pscale, w_out)


def _route_kernel(x1_ref, g_ref, wr_ref, br_ref, cnt_in_ref, idx_ref, gate_ref, pos_ref, cnt_ref, counts):
    i = pl.program_id(0)
    rows = x1_ref.shape[0]

    @pl.when(i == 0)
    def _():
        counts[...] = cnt_in_ref[...]

    h = _rms(x1_ref[...], g_ref[...])

    logits = lax.dot_general(wr_ref[...], h, (((1,), (1,)), ((), ())), preferred_element_type=F32,
                             precision=lax.Precision.HIGHEST) + br_ref[...]
    eid = lax.broadcasted_iota(I32, (N_EXPERTS, rows), 0).astype(F32)
    work = logits
    vals, hots = [], []
    for kk in range(TOP_K):
        m = jnp.max(work, axis=0, keepdims=True)
        first = jnp.min(jnp.where(work == m, eid, float(N_EXPERTS)), axis=0, keepdims=True)
        hot = eid == first
        work = jnp.where(hot, -jnp.inf, work)
        vals.append(m)
        hots.append(hot)
        idx_ref[kk:kk + 1, :] = first.astype(I32)
    es = [jnp.exp(vv - vals[0]) for vv in vals]
    den = es[0] + es[1] + es[2] + es[3]
    for kk in range(TOP_K):
        gate_ref[kk:kk + 1, :] = es[kk] / den

    chosen = hots[0] | hots[1] | hots[2] | hots[3]
    before = (lax.broadcasted_iota(I32, (rows, rows), 0) < lax.broadcasted_iota(I32, (rows, rows), 1)).astype(BF16)
    rank = jnp.dot(chosen.astype(BF16), before, preferred_element_type=F32) + counts[...]
    for kk in range(TOP_K):
        pos_ref[kk:kk + 1, :] = jnp.sum(jnp.where(hots[kk], rank, 0.0), axis=0, keepdims=True).astype(I32)
    counts[...] = counts[...] + jnp.sum(chosen.astype(F32), axis=1, keepdims=True)
    cnt_ref[...] = counts[...]


def _route(x1, g_ffn, wr_t, br, cnt_in, rows):
    n = x1.shape[0]
    assert n % rows == 0
    const = lambda shape: pl.BlockSpec(shape, lambda i: (0,) * len(shape))
    tok_blk = pl.BlockSpec((TOP_K, rows), lambda i: (0, i))
    return pl.pallas_call(
        _route_kernel,
        grid=(n // rows,),
        in_specs=[pl.BlockSpec((rows, D_MODEL), lambda i: (i, 0)), const((1, D_MODEL)), const((N_EXPERTS, D_MODEL)),
                  const((N_EXPERTS, 1)), const((N_EXPERTS, 1))],
        out_specs=[tok_blk, tok_blk, tok_blk, const((N_EXPERTS, 1))],
        out_shape=[jax.ShapeDtypeStruct((TOP_K, n), I32),
                   jax.ShapeDtypeStruct((TOP_K, n), F32), jax.ShapeDtypeStruct((TOP_K, n), I32),
                   jax.ShapeDtypeStruct((N_EXPERTS, 1), F32)],
        scratch_shapes=[pltpu.VMEM((N_EXPERTS, 1), F32)],
        compiler_params=pltpu.CompilerParams(dimension_semantics=("arbitrary",), vmem_limit_bytes=VMEM_LIMIT),
        name="route",
    )(x1, g_ffn, wr_t, br, cnt_in)


def _sc_mesh():
    return plsc.VectorSubcoreMesh(core_axis_name="core", subcore_axis_name="subcore")


def _sc_worker_id():
    return lax.axis_index("core") * SC_SUBCORES + lax.axis_index("subcore")


def _scatter_rows(x, dest, n_rows):
    ch = SC_CHUNK
    n, w = x.shape
    nk = dest.shape[0]
    assert n % ch == 0 and dest.shape[1] == n
    n_chunks = n // ch

    @pl.kernel(out_type=jax.ShapeDtypeStruct((n_rows, w), x.dtype), mesh=_sc_mesh(),
               scratch_types=[pltpu.VMEM((ch,), I32), pltpu.VMEM((ch, w), x.dtype)])
    def scatter_kernel(x_hbm, d_hbm, o_hbm, idx_v, buf):
        wid = _sc_worker_id()

        @pl.loop(0, -(-n_chunks // SC_WORKERS))
        def _(j):
            c = j * SC_WORKERS + wid

            @pl.when(c < n_chunks)
            def _():
                pltpu.sync_copy(x_hbm.at[pl.ds(c * ch, ch)], buf)
                for kk in range(nk):
                    pltpu.sync_copy(d_hbm.at[pl.ds(kk * n + c * ch, ch)], idx_v)
                    pltpu.sync_copy(buf, o_hbm.at[idx_v])

    return scatter_kernel(x, dest.reshape(-1))


def _gather_rows(src, idx):
    ch = SC_CHUNK
    m = idx.shape[0]
    w = src.shape[1]
    per = m // SC_WORKERS
    assert m % SC_WORKERS == 0 and per % ch == 0

    @pl.kernel(out_type=jax.ShapeDtypeStruct((m, w), src.dtype), mesh=_sc_mesh(),
               scratch_types=[pltpu.VMEM((ch,), I32), pltpu.VMEM((ch, w), src.dtype)])
    def gather_kernel(s_hbm, i_hbm, o_hbm, idx_v, buf):
        base = _sc_worker_id() * per

        @pl.loop(0, per // ch)
        def _(j):
            r0 = base + j * ch
            pltpu.sync_copy(i_hbm.at[pl.ds(r0, ch)], idx_v)
            pltpu.sync_copy(s_hbm.at[idx_v], buf)
            pltpu.sync_copy(buf, o_hbm.at[pl.ds(r0, ch)])

    return gather_kernel(src, idx)


def _moe_kernel(blk_e_ref, blk_valid_ref, xs_ref, g_ref, wgu_ref, bgu_ref, wd_ref, bd_ref, y_ref, wgu_bf, wd_bf):
    i = pl.program_id(0)
    e = blk_e_ref[i]
    n_valid = blk_valid_ref[i]
    used = n_valid > 0
    new_expert = (i == 0) | (blk_e_ref[jnp.maximum(i - 1, 0)] != e)

    @pl.when(used & new_expert)
    def _():
        wgu_bf[...] = wgu_ref[0].astype(BF16)
        wd_bf[...] = wd_ref[0].astype(BF16)

    @pl.when(used)
    def _():
        valid = lax.broadcasted_iota(I32, xs_ref.shape, 0) < n_valid
        xb = _rms(jnp.where(valid, xs_ref[...], 0.0), g_ref[...]).astype(BF16)
        gu = jnp.dot(xb, wgu_bf[...], preferred_element_type=F32) + bgu_ref[0]
        g = jnp.minimum(gu[:, :D_FF], SWIGLU_LIMIT)
        up = jnp.clip(gu[:, D_FF:], -SWIGLU_LIMIT, SWIGLU_LIMIT)
        act = (up + 1.0) * (g * jax.nn.sigmoid(SWIGLU_ALPHA * g))
        y_ref[...] = jnp.dot(act.astype(BF16), wd_bf[...], preferred_element_type=F32) + bd_ref[0]

    @pl.when(jnp.logical_not(used))
    def _():
        y_ref[...] = jnp.zeros_like(y_ref)


def _moe(xs, blk_e, blk_valid, g_ffn, w_gu, b_gu, w_d, b_d):
    n_rows = xs.shape[0]
    bm = MOE_BM
    assert n_rows % bm == 0
    grid_spec = pltpu.PrefetchScalarGridSpec(
        num_scalar_prefetch=2,
        grid=(n_rows // bm,),
        in_specs=[pl.BlockSpec((bm, D_MODEL), lambda i, be, nb: (i, 0)),
                  pl.BlockSpec((1, D_MODEL), lambda i, be, nb: (0, 0)),
                  pl.BlockSpec((1, D_MODEL, 2 * D_FF), lambda i, be, nb: (be[i], 0, 0)),
                  pl.BlockSpec((1, 1, 2 * D_FF), lambda i, be, nb: (be[i], 0, 0)),
                  pl.BlockSpec((1, D_FF, D_MODEL), lambda i, be, nb: (be[i], 0, 0)),
                  pl.BlockSpec((1, 1, D_MODEL), lambda i, be, nb: (be[i], 0, 0))],
        out_specs=pl.BlockSpec((bm, D_MODEL), lambda i, be, nb: (i, 0)),
        scratch_shapes=[pltpu.VMEM((D_MODEL, 2 * D_FF), BF16), pltpu.VMEM((D_FF, D_MODEL), BF16)],
    )
    return pl.pallas_call(
        _moe_kernel,
        grid_spec=grid_spec,
        out_shape=jax.ShapeDtypeStruct((n_rows, D_MODEL), F32),
        compiler_params=pltpu.CompilerParams(dimension_semantics=("arbitrary",), vmem_limit_bytes=VMEM_LIMIT),
        name="moe_experts",
    )(blk_e, blk_valid, xs, g_ffn, w_gu, b_gu.reshape(N_EXPERTS, 1, 2 * D_FF), w_d, b_d.reshape(N_EXPERTS, 1, D_MODEL))


def _ple_kernel(x1_ref, y0_ref, y1_ref, y2_ref, y3_ref, gates_ref, p_ref, g_ref, wg_ref, wp_ref, out_ref):
    x2 = x1_ref[...]
    gates = gates_ref[...]
    for kk, y_ref in enumerate((y0_ref, y1_ref, y2_ref, y3_ref)):
        x2 = x2 + y_ref[...] * gates[:, kk:kk + 1]
    hp = _rms(x2, g_ref[...]).astype(BF16)
    gate = jax.nn.sigmoid(jnp.dot(hp, wg_ref[...], preferred_element_type=F32))
    pp = jnp.dot(p_ref[...].astype(BF16), wp_ref[...], preferred_element_type=F32)
    out_ref[...] = x2 + gate * pp


def _ple(x1_all, tok0, y_tok, y0, gates_t, p, g_ple, w_gate, w_proj, rows):
    n = p.shape[0]
    assert n % rows == 0 and tok0 % rows == 0 and y0 % rows == 0
    const = lambda shape: pl.BlockSpec(shape, lambda i: (0,) * len(shape))
    tok_blk = lambda width: pl.BlockSpec((rows, width), lambda i: (tok0 // rows + i, 0))
    y_blk = lambda kk: pl.BlockSpec((rows, D_MODEL), lambda i: ((y0 + kk * n) // rows + i, 0))
    return pl.pallas_call(
        _ple_kernel,
        grid=(n // rows,),
        in_specs=[tok_blk(D_MODEL), y_blk(0), y_blk(1), y_blk(2), y_blk(3), tok_blk(TOP_K),
                  pl.BlockSpec((rows, PLE_DIM), lambda i: (i, 0)),
                  const((1, D_MODEL)), const((D_MODEL, D_MODEL)), const((PLE_DIM, D_MODEL))],
        out_specs=pl.BlockSpec((rows, D_MODEL), lambda i: (i, 0)),
        out_shape=jax.ShapeDtypeStruct((n, D_MODEL), F32),
        compiler_params=pltpu.CompilerParams(dimension_semantics=("arbitrary",), vmem_limit_bytes=VMEM_LIMIT),
        name="combine_ple",
    )(x1_all, y_tok, y_tok, y_tok, y_tok, gates_t, p, g_ple, w_gate, w_proj)


def _q_perm():
    perm = np.empty((ATTN_WIDTH,), np.int32)
    for j in range(Q_TILES):
        for s in range(2):
            for d in range(HEAD_DIM):
                perm[j * LANES + s * HEAD_DIM + d] = (j + Q_TILES * s) * HEAD_DIM + d
    return perm


def _rope_tables(pos):
    half = ROT_DIM // 2
    inv = ROPE_THETA ** (-jnp.arange(half, dtype=F32) / half)
    ang = pos.astype(F32)[:, None] * inv
    cos, sin = jnp.cos(ang), jnp.sin(ang)
    n = pos.shape[0]
    zeros = jnp.zeros((n, HEAD_DIM - ROT_DIM), F32)
    cos_h = jnp.concatenate([cos, cos, jnp.ones_like(zeros)], axis=1)
    sa_h = jnp.concatenate([jnp.zeros_like(sin), sin, zeros], axis=1)
    sb_h = jnp.concatenate([-sin, jnp.zeros_like(sin), zeros], axis=1)
    return tuple(jnp.concatenate([t, t], axis=1) for t in (cos_h, sa_h, sb_h))


def _layer(xp, xs, ck, cv, st, pp, ps, past_len, refine_tail,
           norm_attn, w_in, q_norm, k_norm, attn_sinks, w_pool, pool_scale, w_out,
           norm_ffn, w_router, b_router, w_gate_up, b_gate_up, w_down, b_down,
           norm_ple, w_ple_gate, w_ple_proj):
    t, ns = xp.shape[0], xs.shape[0]
    perm = _q_perm()
    w_in_p = jnp.concatenate([w_in[:, perm], w_in[:, ATTN_WIDTH:]], axis=1)
    w_out_p = jnp.concatenate([w_out[perm], w_out[ATTN_WIDTH:]], axis=0)
    g_attn = norm_attn.reshape(1, D_MODEL)
    qn = jnp.tile(q_norm, 2).reshape(1, LANES)
    kn = jnp.tile(k_norm, 2).reshape(1, LANES)
    lane = np.arange(LANES)
    hmean = jnp.asarray((lane[:, None] // HEAD_DIM == lane[None, :] // HEAD_DIM) / HEAD_DIM, F32)
    pscale = pool_scale.reshape(1, POOL_WIDTH)
    sink8 = jnp.broadcast_to(attn_sinks[np.array([j + Q_TILES * s for j in range(Q_TILES) for s in range(2)])][:, None],
                             (8, LANES))

    mix_args = (g_attn, w_in_p.astype(BF16), qn, kn, hmean.astype(BF16), w_pool.astype(BF16), pscale,
                w_out_p.astype(BF16))
    tabs = _rope_tables(jnp.arange(t))
    x1p, nk_p, nv_p, nu_p = _mixer_prompt(xp, 0, attn_sinks, tabs, *mix_args)
    if refine_tail:
        tail = 2 * MIX_ROWS
        hi_args = (g_attn, w_in_p, qn, kn, hmean, w_pool, pscale, w_out_p)
        x1_hi, nk_p, nv_p, nu_p = _mixer_prompt(xp[t - tail:], t - tail, attn_sinks,
                                                tuple(tb[t - tail:] for tb in tabs), *hi_args)
        x1p = lax.dynamic_update_slice(x1p, x1_hi[tail - MIX_ROWS:], (t - MIX_ROWS, 0))
    st_t = jnp.transpose(st, (1, 0, 2))
    x1s, nk_s, nv_s, nst_t = _mixer_sample(xs, ck.reshape(ns, -1, KV_WIDTH), cv.reshape(ns, -1, KV_WIDTH), st_t,
                                           past_len, _rope_tables(jnp.full((1,), past_len)), sink8, *mix_args)

    g_ffn = norm_ffn.reshape(1, D_MODEL)
    wr_t = w_router.T
    br = b_router.reshape(N_EXPERTS, 1)
    x1_all = jnp.concatenate([x1p, x1s], axis=0)
    n_tok = t + ns
    idx, gate, pos, cnt = _route(x1_all, g_ffn, wr_t, br, jnp.zeros((N_EXPERTS, 1), F32), ROUTE_ROWS)

    bm = MOE_BM
    n_blocks = -(-(n_tok * TOP_K + N_EXPERTS * (bm - 1)) // bm)
    counts = cnt.reshape(N_EXPERTS).astype(I32)
    padded = (counts + bm - 1) // bm * bm
    pad_end = jnp.cumsum(padded)
    pad_start = pad_end - padded
    expert_ids = jnp.arange(N_EXPERTS, dtype=I32)
    start_of = jnp.sum(jnp.where(idx[None] == expert_ids[:, None, None], pad_start[:, None, None], 0), axis=0)
    dest = start_of + pos
    blk_start = jnp.arange(n_blocks, dtype=I32) * bm
    blk_e = jnp.minimum(jnp.sum((pad_end[None, :] <= blk_start[:, None]).astype(I32), axis=1), N_EXPERTS - 1)
    blk_valid = jnp.clip(pad_start[blk_e] + counts[blk_e] - blk_start, 0, bm).astype(I32)

    xs_rows = _scatter_rows(x1_all, dest, n_blocks * bm)
    y = _moe(xs_rows, blk_e, blk_valid, g_ffn, w_gate_up, b_gate_up, w_down, b_down)

    back = jnp.concatenate([dest[:, :t].reshape(-1), dest[:, t:].reshape(-1)])
    unit = SC_WORKERS * SC_CHUNK
    back = jnp.concatenate([back, jnp.zeros((-back.shape[0] % unit,), I32)])
    y_tok = _gather_rows(y, back)

    g_ple = norm_ple.reshape(1, D_MODEL)
    wg_bf = w_ple_gate.astype(BF16)
    wp_bf = w_ple_proj.astype(BF16)
    gates_t = gate.T
    yp = _ple(x1_all, 0, y_tok, 0, gates_t, pp, g_ple, wg_bf, wp_bf, PLE_ROWS)
    ys = _ple(x1_all, t, y_tok, TOP_K * t, gates_t, ps, g_ple, wg_bf, wp_bf, ns)
    new_pool_s = jnp.transpose(nst_t, (1, 0, 2))
    return yp, ys, nk_p, nv_p, nu_p[POOL_HALO - POOL_PREFIX:], nk_s, nv_s, new_pool_s


def kernel(x_prompt, x_sample, cache_k, cache_v, state_pool, p_prompt, p_sample, norm_attn, w_in, q_norm, k_norm,
           attn_sinks, w_pool, pool_scale, w_out, norm_ffn, w_router, b_router, w_gate_up, b_gate_up, w_down, b_down,
           norm_ple, w_ple_gate, w_ple_proj):
    depth = norm_attn.shape[0]
    batch, seq, d = x_prompt.shape
    ns, dec_seq, _ = x_sample.shape
    wb = cache_k.shape[2]
    assert batch == 1 and dec_seq == 1 and d == D_MODEL and wb == WINDOW
    assert cache_k.shape[3:] == (N_KV_HEADS, HEAD_DIM) and state_pool.shape[2:] == (POOL_PREFIX, POOL_WIDTH)
    past_len = PAST_LEN
    yp = x_prompt.reshape(seq, d)
    ys = x_sample.reshape(ns, d)
    outs = [[] for _ in range(6)]
    for i in range(depth):
        res = _layer(yp, ys, cache_k[i], cache_v[i], state_pool[i], p_prompt[i, 0], p_sample[i, :, 0], past_len,
                     i < depth - 1,
                     norm_attn[i], w_in[i], q_norm[i], k_norm[i], attn_sinks[i], w_pool[i], pool_scale[i], w_out[i],
                     norm_ffn[i], w_router[i], b_router[i], w_gate_up[i], b_gate_up[i], w_down[i], b_down[i],
                     norm_ple[i], w_ple_gate[i], w_ple_proj[i])
        yp, ys = res[0], res[1]
        kv_shape = (1, WINDOW, N_KV_HEADS, HEAD_DIM)
        outs[0].append(res[2].reshape(kv_shape))
        outs[1].append(res[3].reshape(kv_shape))
        outs[2].append(res[4].reshape(1, POOL_PREFIX, POOL_WIDTH))
        outs[3].append(res[5].reshape(ns, wb, N_KV_HEADS, HEAD_DIM))
        outs[4].append(res[6].reshape(ns, wb, N_KV_HEADS, HEAD_DIM))
        outs[5].append(res[7])
    return (yp.reshape(batch, seq, d), ys.reshape(ns, dec_seq, d)) + tuple(jnp.stack(o) for o in outs)
```

```python
import functools

import jax
import jax.numpy as jnp
import numpy as np
from jax import lax
from jax.experimental import pallas as pl
from jax.experimental.pallas import tpu as pltpu
from jax.experimental.pallas import tpu_sc as plsc

F32 = jnp.float32
BF16 = jnp.bfloat16
U32 = jnp.uint32
I32 = jnp.int32

D_MODEL = 1024
HEAD_DIM = 64
N_HEADS = 8
N_KV_HEADS = 2
GROUP = N_HEADS // N_KV_HEADS
ATTN_WIDTH = N_HEADS * HEAD_DIM
KV_WIDTH = N_KV_HEADS * HEAD_DIM
POOL_WIDTH = 512
POOL_WINDOWS = (2, 4, 8, 16)
POOL_GC = POOL_WIDTH // len(POOL_WINDOWS)
POOL_PREFIX = max(POOL_WINDOWS) - 1
POOL_HALO = POOL_PREFIX + 1
IN_WIDTH = ATTN_WIDTH + 2 * KV_WIDTH + POOL_WIDTH
WINDOW = 128
ROPE_THETA = 500000.0
ROT_DIM = HEAD_DIM // 4
N_EXPERTS = 32
TOP_K = 4
D_FF = 1024
SWIGLU_ALPHA = 1.702
SWIGLU_LIMIT = 7.0
PLE_DIM = 256
PAST_LEN = 16384
EPS = 1e-5
NEG_INF = -1e30

LANES = 128
Q_TILES = ATTN_WIDTH // LANES

MIX_ROWS = 256
SAMPLE_CHUNK = 16
ROUTE_ROWS = 512
MOE_BM = 256
SC_CORES = 2
SC_SUBCORES = 16
SC_WORKERS = SC_CORES * SC_SUBCORES
SC_CHUNK = 32
PLE_ROWS = 256
VMEM_LIMIT = 56 * 1024 * 1024


def _rms(x, g):
    return x * lax.rsqrt(jnp.mean(x * x, axis=-1, keepdims=True) + EPS) * g


def _mm(a, b, nt=False):
    dims = (((1,), (1 if nt else 0,)), ((), ()))
    if b.dtype == F32:
        return lax.dot_general(a.astype(F32), b, dims, preferred_element_type=F32, precision=lax.Precision.HIGHEST)
    return lax.dot_general(a.astype(BF16), b, dims, preferred_element_type=F32)


def _head_norm_rope(t, hmean, gain, cos, sin_a, sin_b):
    t = t * lax.rsqrt(_mm(t * t, hmean) + EPS) * gain
    return t * cos + pltpu.roll(t, ROT_DIM // 2, axis=1) * sin_a + pltpu.roll(t, LANES - ROT_DIM // 2, axis=1) * sin_b


def _softmax_pv(s, sink, v, ones):
    m = jnp.maximum(jnp.max(s, axis=-1, keepdims=True), sink)
    e = jnp.exp(s - m).astype(v.dtype)
    den = _mm(e, ones) + jnp.exp(sink - m)
    return _mm(e, v) / den


def _mixer_prompt_kernel(sinks_ref, x_ref, cos_ref, sa_ref, sb_ref, g_ref, win_ref, qn_ref, kn_ref, hm_ref,
                         wpool_ref, pscale_ref, wout_ref, *rest, row_offset, aliased):
    x1_ref, klast_ref, vlast_ref, ulast_ref, kprev, vprev, uext, mix = rest[1:] if aliased else rest
    i = pl.program_id(0)
    rows = x_ref.shape[0]
    n_sub = rows // WINDOW
    cdt = win_ref.dtype
    row0 = row_offset + i * rows

    @pl.when(i == 0)
    def _():
        kprev[...] = jnp.zeros_like(kprev)
        vprev[...] = jnp.zeros_like(vprev)
        uext[0:POOL_HALO, :] = jnp.zeros((POOL_HALO, POOL_WIDTH), F32)

    x = x_ref[...]
    proj = _mm(_rms(x, g_ref[...]), win_ref[...])
    cos, sin_a, sin_b = cos_ref[...], sa_ref[...], sb_ref[...]
    hmean = hm_ref[...]

    k = _head_norm_rope(proj[:, ATTN_WIDTH:ATTN_WIDTH + KV_WIDTH], hmean, kn_ref[...], cos, sin_a, sin_b)
    v = proj[:, ATTN_WIDTH + KV_WIDTH:ATTN_WIDTH + 2 * KV_WIDTH]
    u = proj[:, ATTN_WIDTH + 2 * KV_WIDTH:]
    klast_ref[...] = k[rows - WINDOW:, :]
    vlast_ref[...] = v[rows - WINDOW:, :]
    ulast_ref[...] = u[rows - POOL_HALO:, :]
    k_c = k.astype(cdt)
    v_c = v.astype(cdt)

    lane = lax.broadcasted_iota(I32, (WINDOW, LANES), 1)
    left = lane < HEAD_DIM
    qi = lax.broadcasted_iota(I32, (WINDOW, 2 * WINDOW), 0)
    kj = lax.broadcasted_iota(I32, (WINDOW, 2 * WINDOW), 1)
    band = (kj - qi >= 1) & (kj - qi <= WINDOW)
    ones = jnp.ones((2 * WINDOW, LANES), cdt)
    scale = HEAD_DIM ** -0.5

    for c in range(n_sub):
        r0 = c * WINDOW
        if c == 0:
            k_cat = jnp.concatenate([kprev[...], k_c[0:WINDOW]], axis=0)
            v_cat = jnp.concatenate([vprev[...], v_c[0:WINDOW]], axis=0)
            mask = band & (kj + (row0 - WINDOW) >= 0)
        else:
            k_cat = k_c[r0 - WINDOW:r0 + WINDOW]
            v_cat = v_c[r0 - WINDOW:r0 + WINDOW]
            mask = band
        mask2 = jnp.concatenate([mask, mask], axis=0)
        for j in range(Q_TILES):
            qt = _head_norm_rope(proj[r0:r0 + WINDOW, j * LANES:(j + 1) * LANES], hmean, qn_ref[...],
                                 cos[r0:r0 + WINDOW], sin_a[r0:r0 + WINDOW], sin_b[r0:r0 + WINDOW]) * scale
            q2 = jnp.concatenate([jnp.where(left, qt, 0.0), jnp.where(left, 0.0, qt)], axis=0)
            s = jnp.where(mask2, _mm(q2, k_cat, nt=True), NEG_INF)
            sink = jnp.concatenate([jnp.full((WINDOW, 1), sinks_ref[j], F32),
                                    jnp.full((WINDOW, 1), sinks_ref[j + Q_TILES], F32)], axis=0)
            o = _softmax_pv(s, sink, v_cat, ones)
            mix[r0:r0 + WINDOW, j * LANES:(j + 1) * LANES] = jnp.where(left, o[0:WINDOW], o[WINDOW:]).astype(cdt)

    kprev[...] = k_c[rows - WINDOW:]
    vprev[...] = v_c[rows - WINDOW:]

    uext[POOL_HALO:POOL_HALO + rows, :] = u
    pos1 = (lax.broadcasted_iota(I32, (rows, 1), 0) + row0 + 1).astype(F32)
    for gi, w in enumerate(POOL_WINDOWS):
        cols = slice(gi * POOL_GC, (gi + 1) * POOL_GC)
        wsum = u[:, cols]
        for sft in range(1, w):
            wsum = wsum + uext[POOL_HALO - sft:POOL_HALO - sft + rows, cols]
        d = wsum / jnp.minimum(pos1, float(w)) - u[:, cols]
        y = _mm(d, wpool_ref[gi]) * pscale_ref[:, cols]
        mix[:, ATTN_WIDTH + gi * POOL_GC:ATTN_WIDTH + (gi + 1) * POOL_GC] = y.astype(cdt)
    uext[0:POOL_HALO, :] = u[rows - POOL_HALO:, :]

    x1_ref[...] = x + _mm(mix[...], wout_ref[...])


def _mixer_prompt(x_full, row_offset, sinks, tabs, g_attn, w_in, qn, kn, hmean, w_pool, pscale, w_out, x1_into=None):
    t = x_full.shape[0] - row_offset
    rows = MIX_ROWS
    cdt = w_in.dtype
    assert t % rows == 0 and row_offset % rows == 0 and rows % WINDOW == 0 and rows >= POOL_HALO
    blk0, n_steps = row_offset // rows, t // rows
    const = lambda shape: pl.BlockSpec(shape, lambda i, *_: (0,) * len(shape))
    row_blk = lambda width: pl.BlockSpec((rows, width), lambda i, *_: (blk0 + i, 0))
    aliased = x1_into is not None
    if aliased:
        assert x1_into.shape == x_full.shape
        x1_spec = pl.BlockSpec((rows, D_MODEL), lambda i, *_: (blk0 + n_steps - 1, 0))
        x1_shape = x1_into.shape
        extra_specs, extra_args, aliases = [pl.BlockSpec(memory_space=pl.ANY)], [x1_into], {13: 0}
    else:
        x1_spec = pl.BlockSpec((rows, D_MODEL), lambda i, *_: (i, 0))
        x1_shape = (t, D_MODEL)
        extra_specs, extra_args, aliases = [], [], {}
    grid_spec = pltpu.PrefetchScalarGridSpec(
        num_scalar_prefetch=1,
        grid=(n_steps,),
        in_specs=[row_blk(D_MODEL), row_blk(LANES), row_blk(LANES), row_blk(LANES),
                  const((1, D_MODEL)), const((D_MODEL, IN_WIDTH)), const((1, LANES)), const((1, LANES)),
                  const((LANES, LANES)), const((len(POOL_WINDOWS), POOL_GC, POOL_GC)), const((1, POOL_WIDTH)),
                  const((D_MODEL, D_MODEL))] + extra_specs,
        out_specs=[x1_spec, const((WINDOW, KV_WIDTH)), const((WINDOW, KV_WIDTH)),
                   const((POOL_HALO, POOL_WIDTH))],
        scratch_shapes=[pltpu.VMEM((WINDOW, KV_WIDTH), cdt), pltpu.VMEM((WINDOW, KV_WIDTH), cdt),
                        pltpu.VMEM((POOL_HALO + rows, POOL_WIDTH), F32), pltpu.VMEM((rows, D_MODEL), cdt)],
    )
    return pl.pallas_call(
        functools.partial(_mixer_prompt_kernel, row_offset=row_offset, aliased=aliased),
        grid_spec=grid_spec,
        out_shape=[jax.ShapeDtypeStruct(x1_shape, F32), jax.ShapeDtypeStruct((WINDOW, KV_WIDTH), F32),
                   jax.ShapeDtypeStruct((WINDOW, KV_WIDTH), F32), jax.ShapeDtypeStruct((POOL_HALO, POOL_WIDTH), F32)],
        input_output_aliases=aliases,
        compiler_params=pltpu.CompilerParams(dimension_semantics=("arbitrary",), vmem_limit_bytes=VMEM_LIMIT),
        name="mixer_prompt",
    )(sinks, x_full, *tabs, g_attn, w_in, qn, kn, hmean, w_pool, pscale, w_out, *extra_args)


def _mixer_sample_kernel(x_ref, ck_ref, cv_ref, st_ref, cos_ref, sa_ref, sb_ref, sink8_ref, g_ref, win_ref, qn_ref,
                         kn_ref, hm_ref, wpool_ref, pscale_ref, wout_ref,
                         x1_ref, nk_ref, nv_ref, nst_ref, o8, *, pos):
    nb = x_ref.shape[0]
    wb = ck_ref.shape[1]
    x = x_ref[...]
    h = _rms(x, g_ref[...]).astype(BF16)
    proj = jnp.dot(h, win_ref[...], preferred_element_type=F32)
    cos, sin_a, sin_b = cos_ref[...], sa_ref[...], sb_ref[...]
    hmean = hm_ref[...]
    k = _head_norm_rope(proj[:, ATTN_WIDTH:ATTN_WIDTH + KV_WIDTH], hmean, kn_ref[...], cos, sin_a, sin_b)
    v = proj[:, ATTN_WIDTH + KV_WIDTH:ATTN_WIDTH + 2 * KV_WIDTH]
    u = proj[:, ATTN_WIDTH + 2 * KV_WIDTH:]

    nk_ref[:, 0:wb - 1, :] = ck_ref[:, 1:wb, :]
    nv_ref[:, 0:wb - 1, :] = cv_ref[:, 1:wb, :]
    for b in range(nb):
        nk_ref[b, wb - 1:wb, :] = k[b:b + 1, :]
        nv_ref[b, wb - 1:wb, :] = v[b:b + 1, :]

    r8 = lax.broadcasted_iota(I32, (nb * 8, LANES), 0)
    lane8 = lax.broadcasted_iota(I32, (nb * 8, LANES), 1)
    keep = (lane8 < HEAD_DIM) == (r8 % 2 == 0)
    rep = (lax.broadcasted_iota(I32, (nb * 8, nb), 0) // 8 == lax.broadcasted_iota(I32, (nb * 8, nb), 1)).astype(BF16)
    q8 = jnp.zeros((nb * 8, LANES), F32)
    scale = HEAD_DIM ** -0.5
    for j in range(Q_TILES):
        qt = _head_norm_rope(proj[:, j * LANES:(j + 1) * LANES], hmean, qn_ref[...], cos, sin_a, sin_b) * scale
        qrep = jnp.dot(rep, qt.astype(BF16), preferred_element_type=F32)
        q8 = jnp.where(keep & ((r8 % 8) // 2 == j), qrep, q8)
    q8 = q8.astype(BF16)

    sink8 = sink8_ref[:, 0:1]
    ones_bf = jnp.ones((wb, LANES), BF16)
    assert pos >= wb - 1 and wb <= WINDOW
    for b in range(nb):
        kb = nk_ref[b].astype(BF16)
        vb = nv_ref[b].astype(BF16)
        s = lax.dot_general(q8[b * 8:(b + 1) * 8], kb, (((1,), (1,)), ((), ())), preferred_element_type=F32)
        o8[b * 8:(b + 1) * 8, :] = _softmax_pv(s, sink8, vb, ones_bf)
    o8m = jnp.where(keep, o8[...], 0.0).astype(BF16)

    a_tiles = []
    sel_r = lax.broadcasted_iota(I32, (nb, nb * 8), 1)
    sel_b = lax.broadcasted_iota(I32, (nb, nb * 8), 0)
    for j in range(Q_TILES):
        sel = ((sel_r // 8 == sel_b) & ((sel_r % 8) // 2 == j)).astype(BF16)
        a_tiles.append(jnp.dot(sel, o8m, preferred_element_type=F32))

    z_tiles = []
    for gi, w in enumerate(POOL_WINDOWS):
        cols = slice(gi * POOL_GC, (gi + 1) * POOL_GC)
        wsum = u[:, cols]
        for sft in range(1, w):
            wsum = wsum + st_ref[POOL_PREFIX - sft, :, cols]
        d = wsum / float(min(pos + 1, w)) - u[:, cols]
        z_tiles.append(jnp.dot(d.astype(BF16), wpool_ref[gi], preferred_element_type=F32) * pscale_ref[:, cols])
    nst_ref[0:POOL_PREFIX - 1] = st_ref[1:POOL_PREFIX]
    nst_ref[POOL_PREFIX - 1] = u

    mixv = jnp.concatenate(a_tiles + z_tiles, axis=1).astype(BF16)
    x1_ref[...] = x + jnp.dot(mixv, wout_ref[...], preferred_element_type=F32)


def _mixer_sample(x, ck, cv, st, pos, tabs, sink8, g_attn, w_in, qn, kn, hmean, w_pool, pscale, w_out):
    n, wb = ck.shape[0], ck.shape[1]
    nb = SAMPLE_CHUNK
    assert n % nb == 0
    const = lambda shape: pl.BlockSpec(shape, lambda i: (0,) * len(shape))
    cache_blk = pl.BlockSpec((nb, wb, KV_WIDTH), lambda i: (i, 0, 0))
    st_blk = pl.BlockSpec((POOL_PREFIX, nb, POOL_WIDTH), lambda i: (0, i, 0))
    x_blk = pl.BlockSpec((nb, D_MODEL), lambda i: (i, 0))
    return pl.pallas_call(
        functools.partial(_mixer_sample_kernel, pos=pos),
        grid=(n // nb,),
        in_specs=[x_blk, cache_blk, cache_blk, st_blk, const((1, LANES)), const((1, LANES)), const((1, LANES)),
                  const((8, LANES)), const((1, D_MODEL)), const((D_MODEL, IN_WIDTH)), const((1, LANES)),
                  const((1, LANES)), const((LANES, LANES)), const((len(POOL_WINDOWS), POOL_GC, POOL_GC)),
                  const((1, POOL_WIDTH)), const((D_MODEL, D_MODEL))],
        out_specs=[x_blk, cache_blk, cache_blk, st_blk],
        out_shape=[jax.ShapeDtypeStruct((n, D_MODEL), F32), jax.ShapeDtypeStruct(ck.shape, F32),
                   jax.ShapeDtypeStruct(cv.shape, F32), jax.ShapeDtypeStruct(st.shape, F32)],
        scratch_shapes=[pltpu.VMEM((nb * 8, LANES), F32)],
        compiler_params=pltpu.CompilerParams(dimension_semantics=("arbitrary",), vmem_limit_bytes=VMEM_LIMIT),
        name="mixer_sample",
    )(x, ck, cv, st, *tabs, sink8, g_attn, w_in, qn, kn, hmean, w_pool, pscale, w_out)


def _route_kernel(x1_ref, g_ref, wr_ref, br_ref, cnt_in_ref, idx_ref, gate_ref, pos_ref, cnt_ref, counts):
    i = pl.program_id(0)
    rows = x1_ref.shape[0]

    @pl.when(i == 0)
    def _():
        counts[...] = cnt_in_ref[...]

    h = _rms(x1_ref[...], g_ref[...])

    logits = lax.dot_general(wr_ref[...], h, (((1,), (1,)), ((), ())), preferred_element_type=F32,
                             precision=lax.Precision.HIGHEST) + br_ref[...]
    eid = lax.broadcasted_iota(I32, (N_EXPERTS, rows), 0).astype(F32)
    work = logits
    vals, hots = [], []
    for kk in range(TOP_K):
        m = jnp.max(work, axis=0, keepdims=True)
        first = jnp.min(jnp.where(work == m, eid, float(N_EXPERTS)), axis=0, keepdims=True)
        hot = eid == first
        work = jnp.where(hot, -jnp.inf, work)
        vals.append(m)
        hots.append(hot)
        idx_ref[kk:kk + 1, :] = first.astype(I32)
    es = [jnp.exp(vv - vals[0]) for vv in vals]
    den = es[0] + es[1] + es[2] + es[3]
    for kk in range(TOP_K):
        gate_ref[kk:kk + 1, :] = es[kk] / den

    chosen = hots[0] | hots[1] | hots[2] | hots[3]
    before = (lax.broadcasted_iota(I32, (rows, rows), 0) < lax.broadcasted_iota(I32, (rows, rows), 1)).astype(BF16)
    rank = jnp.dot(chosen.astype(BF16), before, preferred_element_type=F32) + counts[...]
    for kk in range(TOP_K):
        pos_ref[kk:kk + 1, :] = jnp.sum(jnp.where(hots[kk], rank, 0.0), axis=0, keepdims=True).astype(I32)
    counts[...] = counts[...] + jnp.sum(chosen.astype(F32), axis=1, keepdims=True)
    cnt_ref[...] = counts[...]


def _route(x1, g_ffn, wr_t, br, cnt_in, rows):
    n = x1.shape[0]
    assert n % rows == 0
    const = lambda shape: pl.BlockSpec(shape, lambda i: (0,) * len(shape))
    tok_blk = pl.BlockSpec((TOP_K, rows), lambda i: (0, i))
    return pl.pallas_call(
        _route_kernel,
        grid=(n // rows,),
        in_specs=[pl.BlockSpec((rows, D_MODEL), lambda i: (i, 0)), const((1, D_MODEL)), const((N_EXPERTS, D_MODEL)),
                  const((N_EXPERTS, 1)), const((N_EXPERTS, 1))],
        out_specs=[tok_blk, tok_blk, tok_blk, const((N_EXPERTS, 1))],
        out_shape=[jax.ShapeDtypeStruct((TOP_K, n), I32),
                   jax.ShapeDtypeStruct((TOP_K, n), F32), jax.ShapeDtypeStruct((TOP_K, n), I32),
                   jax.ShapeDtypeStruct((N_EXPERTS, 1), F32)],
        scratch_shapes=[pltpu.VMEM((N_EXPERTS, 1), F32)],
        compiler_params=pltpu.CompilerParams(dimension_semantics=("arbitrary",), vmem_limit_bytes=VMEM_LIMIT),
        name="route",
    )(x1, g_ffn, wr_t, br, cnt_in)


def _sc_mesh():
    return plsc.VectorSubcoreMesh(core_axis_name="core", subcore_axis_name="subcore")


def _sc_worker_id():
    return lax.axis_index("core") * SC_SUBCORES + lax.axis_index("subcore")


def _scatter_rows(xa, xb, dest, n_rows):
    ch = SC_CHUNK
    na, w = xa.shape
    n = na + xb.shape[0]
    nk = dest.shape[0]
    assert na % ch == 0 and n % ch == 0 and dest.shape[1] == n and xb.shape[1] == w and xa.dtype == xb.dtype
    n_chunks = n // ch

    @pl.kernel(out_type=jax.ShapeDtypeStruct((n_rows, w), xa.dtype), mesh=_sc_mesh(),
               scratch_types=[pltpu.VMEM((ch,), I32), pltpu.VMEM((ch, w), xa.dtype)])
    def scatter_kernel(xa_hbm, xb_hbm, d_hbm, o_hbm, idx_v, buf):
        wid = _sc_worker_id()

        @pl.loop(0, -(-n_chunks // SC_WORKERS))
        def _(j):
            c = j * SC_WORKERS + wid

            @pl.when(c < na // ch)
            def _():
                pltpu.sync_copy(xa_hbm.at[pl.ds(c * ch, ch)], buf)

            @pl.when((c >= na // ch) & (c < n_chunks))
            def _():
                pltpu.sync_copy(xb_hbm.at[pl.ds(c * ch - na, ch)], buf)

            @pl.when(c < n_chunks)
            def _():
                for kk in range(nk):
                    pltpu.sync_copy(d_hbm.at[pl.ds(kk * n + c * ch, ch)], idx_v)
                    pltpu.sync_copy(buf, o_hbm.at[idx_v])

    return scatter_kernel(xa, xb, dest.reshape(-1))


def _gather_rows(src, idx):
    ch = SC_CHUNK
    m = idx.shape[0]
    w = src.shape[1]
    per = m // SC_WORKERS
    assert m % SC_WORKERS == 0 and per % ch == 0

    @pl.kernel(out_type=jax.ShapeDtypeStruct((m, w), src.dtype), mesh=_sc_mesh(),
               scratch_types=[pltpu.VMEM((ch,), I32), pltpu.VMEM((ch, w), src.dtype)])
    def gather_kernel(s_hbm, i_hbm, o_hbm, idx_v, buf):
        base = _sc_worker_id() * per

        @pl.loop(0, per // ch)
        def _(j):
            r0 = base + j * ch
            pltpu.sync_copy(i_hbm.at[pl.ds(r0, ch)], idx_v)
            pltpu.sync_copy(s_hbm.at[idx_v], buf)
            pltpu.sync_copy(buf, o_hbm.at[pl.ds(r0, ch)])

    return gather_kernel(src, idx)


def _moe_kernel(blk_e_ref, blk_valid_ref, xs_ref, g_ref, wgu_ref, bgu_ref, wd_ref, bd_ref, y_ref, wgu_bf, wd_bf):
    i = pl.program_id(0)
    e = blk_e_ref[i]
    n_valid = blk_valid_ref[i]
    used = n_valid > 0
    new_expert = (i == 0) | (blk_e_ref[jnp.maximum(i - 1, 0)] != e)

    @pl.when(used & new_expert)
    def _():
        wgu_bf[...] = wgu_ref[0].astype(BF16)
        wd_bf[...] = wd_ref[0].astype(BF16)

    @pl.when(used)
    def _():
        valid = lax.broadcasted_iota(I32, xs_ref.shape, 0) < n_valid
        xb = _rms(jnp.where(valid, xs_ref[...], 0.0), g_ref[...]).astype(BF16)
        gu = jnp.dot(xb, wgu_bf[...], preferred_element_type=F32) + bgu_ref[0]
        g = jnp.minimum(gu[:, :D_FF], SWIGLU_LIMIT)
        up = jnp.clip(gu[:, D_FF:], -SWIGLU_LIMIT, SWIGLU_LIMIT)
        act = (up + 1.0) * (g * jax.nn.sigmoid(SWIGLU_ALPHA * g))
        y_ref[...] = jnp.dot(act.astype(BF16), wd_bf[...], preferred_element_type=F32) + bd_ref[0]

    @pl.when(jnp.logical_not(used))
    def _():
        y_ref[...] = jnp.zeros_like(y_ref)


def _moe(xs, blk_e, blk_valid, g_ffn, layer, w_gu, b_gu, w_d, b_d):
    n_rows = xs.shape[0]
    bm = MOE_BM
    assert n_rows % bm == 0
    e0 = layer * N_EXPERTS
    n_we = w_gu.shape[0] * w_gu.shape[1]
    grid_spec = pltpu.PrefetchScalarGridSpec(
        num_scalar_prefetch=2,
        grid=(n_rows // bm,),
        in_specs=[pl.BlockSpec((bm, D_MODEL), lambda i, be, nb: (i, 0)),
                  pl.BlockSpec((1, D_MODEL), lambda i, be, nb: (0, 0)),
                  pl.BlockSpec((1, D_MODEL, 2 * D_FF), lambda i, be, nb: (e0 + be[i], 0, 0)),
                  pl.BlockSpec((1, 1, 2 * D_FF), lambda i, be, nb: (e0 + be[i], 0, 0)),
                  pl.BlockSpec((1, D_FF, D_MODEL), lambda i, be, nb: (e0 + be[i], 0, 0)),
                  pl.BlockSpec((1, 1, D_MODEL), lambda i, be, nb: (e0 + be[i], 0, 0))],
        out_specs=pl.BlockSpec((bm, D_MODEL), lambda i, be, nb: (i, 0)),
        scratch_shapes=[pltpu.VMEM((D_MODEL, 2 * D_FF), BF16), pltpu.VMEM((D_FF, D_MODEL), BF16)],
    )
    return pl.pallas_call(
        _moe_kernel,
        grid_spec=grid_spec,
        out_shape=jax.ShapeDtypeStruct((n_rows, D_MODEL), F32),
        compiler_params=pltpu.CompilerParams(dimension_semantics=("arbitrary",), vmem_limit_bytes=VMEM_LIMIT),
        name="moe_experts",
    )(blk_e, blk_valid, xs, g_ffn, w_gu.reshape(n_we, D_MODEL, 2 * D_FF), b_gu.reshape(n_we, 1, 2 * D_FF),
      w_d.reshape(n_we, D_FF, D_MODEL), b_d.reshape(n_we, 1, D_MODEL))


def _ple_kernel(x1_ref, y0_ref, y1_ref, y2_ref, y3_ref, gates_ref, p_ref, g_ref, wg_ref, wp_ref, out_ref):
    x2 = x1_ref[...]
    gates = gates_ref[...]
    for kk, y_ref in enumerate((y0_ref, y1_ref, y2_ref, y3_ref)):
        x2 = x2 + y_ref[...] * gates[:, kk:kk + 1]
    hp = _rms(x2, g_ref[...]).astype(BF16)
    gate = jax.nn.sigmoid(jnp.dot(hp, wg_ref[...], preferred_element_type=F32))
    pp = jnp.dot(p_ref[...].astype(BF16), wp_ref[...], preferred_element_type=F32)
    out_ref[...] = x2 + gate * pp


def _ple(x1, y_tok, y0, gates_t, tok0, p_all, p0, g_ple, w_gate, w_proj, rows):
    n = x1.shape[0]
    assert n % rows == 0 and tok0 % rows == 0 and y0 % rows == 0 and p0 % rows == 0
    const = lambda shape: pl.BlockSpec(shape, lambda i: (0,) * len(shape))
    y_blk = lambda kk: pl.BlockSpec((rows, D_MODEL), lambda i: ((y0 + kk * n) // rows + i, 0))
    return pl.pallas_call(
        _ple_kernel,
        grid=(n // rows,),
        in_specs=[pl.BlockSpec((rows, D_MODEL), lambda i: (i, 0)), y_blk(0), y_blk(1), y_blk(2), y_blk(3),
                  pl.BlockSpec((rows, TOP_K), lambda i: (tok0 // rows + i, 0)),
                  pl.BlockSpec((rows, PLE_DIM), lambda i: (p0 // rows + i, 0)),
                  const((1, D_MODEL)), const((D_MODEL, D_MODEL)), const((PLE_DIM, D_MODEL))],
        out_specs=pl.BlockSpec((rows, D_MODEL), lambda i: (i, 0)),
        out_shape=jax.ShapeDtypeStruct((n, D_MODEL), F32),
        compiler_params=pltpu.CompilerParams(dimension_semantics=("arbitrary",), vmem_limit_bytes=VMEM_LIMIT),
        name="combine_ple",
    )(x1, y_tok, y_tok, y_tok, y_tok, gates_t, p_all, g_ple, w_gate, w_proj)


def _rope_tables(pos):
    half = ROT_DIM // 2
    inv = ROPE_THETA ** (-jnp.arange(half, dtype=F32) / half)
    ang = pos.astype(F32)[:, None] * inv
    cos, sin = jnp.cos(ang), jnp.sin(ang)
    n = pos.shape[0]
    zeros = jnp.zeros((n, HEAD_DIM - ROT_DIM), F32)
    cos_h = jnp.concatenate([cos, cos, jnp.ones_like(zeros)], axis=1)
    sa_h = jnp.concatenate([jnp.zeros_like(sin), sin, zeros], axis=1)
    sb_h = jnp.concatenate([-sin, jnp.zeros_like(sin), zeros], axis=1)
    return tuple(jnp.concatenate([t, t], axis=1) for t in (cos_h, sa_h, sb_h))


def _layer(layer, xp, xs, ck, cv, st, p_prompt_all, p_sample_all, past_len, refine_tail, tabs_p, tabs_s,
           norm_attn, w_in, q_norm, k_norm, attn_sinks, w_pool, pool_scale, w_out,
           norm_ffn, w_router, b_router, w_gate_up_all, b_gate_up_all, w_down_all, b_down_all,
           norm_ple, w_ple_gate, w_ple_proj):
    t, ns = xp.shape[0], xs.shape[0]
    w_q = w_in[:, :ATTN_WIDTH].reshape(D_MODEL, 2, Q_TILES, HEAD_DIM).transpose(0, 2, 1, 3).reshape(D_MODEL, ATTN_WIDTH)
    w_in_p = jnp.concatenate([w_q, w_in[:, ATTN_WIDTH:]], axis=1)
    w_oa = w_out[:ATTN_WIDTH].reshape(2, Q_TILES, HEAD_DIM, D_MODEL).transpose(1, 0, 2, 3).reshape(ATTN_WIDTH, D_MODEL)
    w_out_p = jnp.concatenate([w_oa, w_out[ATTN_WIDTH:]], axis=0)
    g_attn = norm_attn.reshape(1, D_MODEL)
    qn = jnp.tile(q_norm, 2).reshape(1, LANES)
    kn = jnp.tile(k_norm, 2).reshape(1, LANES)
    lane = np.arange(LANES)
    hmean = jnp.asarray((lane[:, None] // HEAD_DIM == lane[None, :] // HEAD_DIM) / HEAD_DIM, F32)
    pscale = pool_scale.reshape(1, POOL_WIDTH)
    sink8 = jnp.broadcast_to(attn_sinks.reshape(2, Q_TILES).T.reshape(8, 1), (8, LANES))

    mix_args = (g_attn, w_in_p.astype(BF16), qn, kn, hmean.astype(BF16), w_pool.astype(BF16), pscale,
                w_out_p.astype(BF16))
    x1p, nk_p, nv_p, nu_p = _mixer_prompt(xp, 0, attn_sinks, tabs_p, *mix_args)
    if refine_tail:
        hi_args = (g_attn, w_in_p, qn, kn, hmean, w_pool, pscale, w_out_p)
        x1p, nk_p, nv_p, nu_p = _mixer_prompt(xp, t - 2 * MIX_ROWS, attn_sinks, tabs_p, *hi_args, x1_into=x1p)
    st_t = jnp.transpose(st, (1, 0, 2))
    x1s, nk_s, nv_s, nst_t = _mixer_sample(xs, ck.reshape(ns, -1, KV_WIDTH), cv.reshape(ns, -1, KV_WIDTH), st_t,
                                           past_len, tabs_s, sink8, *mix_args)

    g_ffn = norm_ffn.reshape(1, D_MODEL)
    wr_t = w_router.T
    br = b_router.reshape(N_EXPERTS, 1)
    idx_p, gate_p, pos_p, cnt_p = _route(x1p, g_ffn, wr_t, br, jnp.zeros((N_EXPERTS, 1), F32), ROUTE_ROWS)
    idx_s, gate_s, pos_s, cnt = _route(x1s, g_ffn, wr_t, br, cnt_p, ns)
    idx = jnp.concatenate([idx_p, idx_s], axis=1)
    pos = jnp.concatenate([pos_p, pos_s], axis=1)
    n_tok = t + ns

    bm = MOE_BM
    n_blocks = -(-(n_tok * TOP_K + N_EXPERTS * (bm - 1)) // bm)
    counts = cnt.reshape(N_EXPERTS).astype(I32)
    padded = (counts + bm - 1) // bm * bm
    pad_end = jnp.cumsum(padded)
    pad_start = pad_end - padded
    expert_ids = jnp.arange(N_EXPERTS, dtype=I32)
    start_of = jnp.sum(jnp.where(idx[None] == expert_ids[:, None, None], pad_start[:, None, None], 0), axis=0)
    dest = start_of + pos
    blk_start = jnp.arange(n_blocks, dtype=I32) * bm
    blk_e = jnp.minimum(jnp.sum((pad_end[None, :] <= blk_start[:, None]).astype(I32), axis=1), N_EXPERTS - 1)
    blk_valid = jnp.clip(pad_start[blk_e] + counts[blk_e] - blk_start, 0, bm).astype(I32)

    xs_rows = _scatter_rows(x1p, x1s, dest, n_blocks * bm)
    y = _moe(xs_rows, blk_e, blk_valid, g_ffn, layer, w_gate_up_all, b_gate_up_all, w_down_all, b_down_all)

    back = jnp.concatenate([dest[:, :t].reshape(-1), dest[:, t:].reshape(-1)])
    unit = SC_WORKERS * SC_CHUNK
    back = jnp.concatenate([back, jnp.zeros((-back.shape[0] % unit,), I32)])
    y_tok = _gather_rows(y, back)

    g_ple = norm_ple.reshape(1, D_MODEL)
    wg_bf = w_ple_gate.astype(BF16)
    wp_bf = w_ple_proj.astype(BF16)
    yp = _ple(x1p, y_tok, 0, gate_p.T, 0, p_prompt_all, layer * t, g_ple, wg_bf, wp_bf, PLE_ROWS)
    ys = _ple(x1s, y_tok, TOP_K * t, gate_s.T, 0, p_sample_all, layer * ns, g_ple, wg_bf, wp_bf, ns)
    new_pool_s = jnp.transpose(nst_t, (1, 0, 2))
    return yp, ys, nk_p, nv_p, nu_p[POOL_HALO - POOL_PREFIX:], nk_s, nv_s, new_pool_s


def kernel(x_prompt, x_sample, cache_k, cache_v, state_pool, p_prompt, p_sample, norm_attn, w_in, q_norm, k_norm,
           attn_sinks, w_pool, pool_scale, w_out, norm_ffn, w_router, b_router, w_gate_up, b_gate_up, w_down, b_down,
           norm_ple, w_ple_gate, w_ple_proj):
    depth = norm_attn.shape[0]
    batch, seq, d = x_prompt.shape
    ns, dec_seq, _ = x_sample.shape
    wb = cache_k.shape[2]
    assert batch == 1 and dec_seq == 1 and d == D_MODEL and wb == WINDOW
    assert cache_k.shape[3:] == (N_KV_HEADS, HEAD_DIM) and state_pool.shape[2:] == (POOL_PREFIX, POOL_WIDTH)
    past_len = PAST_LEN
    yp = x_prompt.reshape(seq, d)
    ys = x_sample.reshape(ns, d)
    p_prompt_all = p_prompt.reshape(depth * seq, PLE_DIM)
    p_sample_all = p_sample.reshape(depth * ns, PLE_DIM)
    tabs_p = _rope_tables(jnp.arange(seq))
    tabs_s = _rope_tables(jnp.full((1,), past_len))
    outs = [[] for _ in range(6)]
    for i in range(depth):
        res = _layer(i, yp, ys, cache_k[i], cache_v[i], state_pool[i], p_prompt_all, p_sample_all, past_len,
                     i < depth - 1, tabs_p, tabs_s,
                     norm_attn[i], w_in[i], q_norm[i], k_norm[i], attn_sinks[i], w_pool[i], pool_scale[i], w_out[i],
                     norm_ffn[i], w_router[i], b_router[i], w_gate_up, b_gate_up, w_down, b_down,
                     norm_ple[i], w_ple_gate[i], w_ple_proj[i])
        yp, ys = res[0], res[1]
        kv_shape = (1, WINDOW, N_KV_HEADS, HEAD_DIM)
        outs[0].append(res[2].reshape(kv_shape))
        outs[1].append(res[3].reshape(kv_shape))
        outs[2].append(res[4].reshape(1, POOL_PREFIX, POOL_WIDTH))
        outs[3].append(res[5].reshape(ns, wb, N_KV_HEADS, HEAD_DIM))
        outs[4].append(res[6].reshape(ns, wb, N_KV_HEADS, HEAD_DIM))
        outs[5].append(res[7])
    return (yp.reshape(batch, seq, d), ys.reshape(ns, dec_seq, d)) + tuple(jnp.stack(o) for o in outs)
```

```python
import functools

import jax
import jax.numpy as jnp
import numpy as np
from jax import lax
from jax.experimental import pallas as pl
from jax.experimental.pallas import tpu as pltpu
from jax.experimental.pallas import tpu_sc as plsc

F32 = jnp.float32
BF16 = jnp.bfloat16
U32 = jnp.uint32
I32 = jnp.int32

D_MODEL = 1024
HEAD_DIM = 64
N_HEADS = 8
N_KV_HEADS = 2
GROUP = N_HEADS // N_KV_HEADS
ATTN_WIDTH = N_HEADS * HEAD_DIM
KV_WIDTH = N_KV_HEADS * HEAD_DIM
POOL_WIDTH = 512
POOL_WINDOWS = (2, 4, 8, 16)
POOL_GC = POOL_WIDTH // len(POOL_WINDOWS)
POOL_PREFIX = max(POOL_WINDOWS) - 1
POOL_HALO = POOL_PREFIX + 1
IN_WIDTH = ATTN_WIDTH + 2 * KV_WIDTH + POOL_WIDTH
WINDOW = 128
ROPE_THETA = 500000.0
ROT_DIM = HEAD_DIM // 4
N_EXPERTS = 32
TOP_K = 4
D_FF = 1024
SWIGLU_ALPHA = 1.702
SWIGLU_LIMIT = 7.0
PLE_DIM = 256
PAST_LEN = 16384
EPS = 1e-5
NEG_INF = -1e30

LANES = 128
Q_TILES = ATTN_WIDTH // LANES

MIX_ROWS = 256
SAMPLE_CHUNK = 16
ROUTE_ROWS = 512
MOE_BM = 512
SC_CORES = 2
SC_SUBCORES = 16
SC_WORKERS = SC_CORES * SC_SUBCORES
SC_CHUNK = 64
PLE_ROWS = 256
VMEM_LIMIT = 56 * 1024 * 1024


def _rms(x, g):
    return x * lax.rsqrt(jnp.mean(x * x, axis=-1, keepdims=True) + EPS) * g


def _pack_halves(x):
    w = x.shape[1] // 2
    lo = lax.bitcast_convert_type(x[:, :w].astype(BF16).astype(F32), U32) >> 16
    hi = lax.bitcast_convert_type(x[:, w:].astype(BF16).astype(F32), U32) & jnp.uint32(0xFFFF0000)
    return lo | hi


def _unpack_halves(packed):
    lo = lax.bitcast_convert_type(packed << 16, F32)
    hi = lax.bitcast_convert_type(packed & jnp.uint32(0xFFFF0000), F32)
    return jnp.concatenate([lo, hi], axis=1)


def _mm(a, b, nt=False):
    dims = (((1,), (1 if nt else 0,)), ((), ()))
    if b.dtype == F32:
        return lax.dot_general(a.astype(F32), b, dims, preferred_element_type=F32, precision=lax.Precision.HIGHEST)
    return lax.dot_general(a.astype(BF16), b, dims, preferred_element_type=F32)


def _head_norm_rope(t, hmean, gain, cos, sin_a, sin_b):
    t = t * lax.rsqrt(_mm(t * t, hmean) + EPS) * gain
    return t * cos + pltpu.roll(t, ROT_DIM // 2, axis=1) * sin_a + pltpu.roll(t, LANES - ROT_DIM // 2, axis=1) * sin_b


def _softmax_pv(s, sink, v, ones):
    m = jnp.maximum(jnp.max(s, axis=-1, keepdims=True), sink)
    e = jnp.exp(s - m).astype(v.dtype)
    den = _mm(e, ones) + jnp.exp(sink - m)
    return _mm(e, v) / den


def _mixer_prompt_kernel(sinks_ref, x_ref, cos_ref, sa_ref, sb_ref, g_ref, win_ref, qn_ref, kn_ref, hm_ref,
                         wpool_ref, pscale_ref, wout_ref, *rest, row_offset, aliased):
    x1_ref, klast_ref, vlast_ref, ulast_ref, kprev, vprev, uext, mix = rest[1:] if aliased else rest
    i = pl.program_id(0)
    rows = x_ref.shape[0]
    n_sub = rows // WINDOW
    cdt = win_ref.dtype
    row0 = row_offset + i * rows

    @pl.when(i == 0)
    def _():
        kprev[...] = jnp.zeros_like(kprev)
        vprev[...] = jnp.zeros_like(vprev)
        uext[0:POOL_HALO, :] = jnp.zeros((POOL_HALO, POOL_WIDTH), F32)

    x = x_ref[...]
    proj = _mm(_rms(x, g_ref[...]), win_ref[...])
    cos, sin_a, sin_b = cos_ref[...], sa_ref[...], sb_ref[...]
    hmean = hm_ref[...]

    k = _head_norm_rope(proj[:, ATTN_WIDTH:ATTN_WIDTH + KV_WIDTH], hmean, kn_ref[...], cos, sin_a, sin_b)
    v = proj[:, ATTN_WIDTH + KV_WIDTH:ATTN_WIDTH + 2 * KV_WIDTH]
    u = proj[:, ATTN_WIDTH + 2 * KV_WIDTH:]
    klast_ref[...] = k[rows - WINDOW:, :]
    vlast_ref[...] = v[rows - WINDOW:, :]
    ulast_ref[...] = u[rows - POOL_HALO:, :]
    k_c = k.astype(cdt)
    v_c = v.astype(cdt)

    lane = lax.broadcasted_iota(I32, (WINDOW, LANES), 1)
    left = lane < HEAD_DIM
    qi = lax.broadcasted_iota(I32, (WINDOW, 2 * WINDOW), 0)
    kj = lax.broadcasted_iota(I32, (WINDOW, 2 * WINDOW), 1)
    band = (kj - qi >= 1) & (kj - qi <= WINDOW)
    ones = jnp.ones((2 * WINDOW, LANES), cdt)
    scale = HEAD_DIM ** -0.5

    for c in range(n_sub):
        r0 = c * WINDOW
        if c == 0:
            k_cat = jnp.concatenate([kprev[...], k_c[0:WINDOW]], axis=0)
            v_cat = jnp.concatenate([vprev[...], v_c[0:WINDOW]], axis=0)
            mask = band & (kj + (row0 - WINDOW) >= 0)
        else:
            k_cat = k_c[r0 - WINDOW:r0 + WINDOW]
            v_cat = v_c[r0 - WINDOW:r0 + WINDOW]
            mask = band
        mask2 = jnp.concatenate([mask, mask], axis=0)
        for j in range(Q_TILES):
            qt = _head_norm_rope(proj[r0:r0 + WINDOW, j * LANES:(j + 1) * LANES], hmean, qn_ref[...],
                                 cos[r0:r0 + WINDOW], sin_a[r0:r0 + WINDOW], sin_b[r0:r0 + WINDOW]) * scale
            q2 = jnp.concatenate([jnp.where(left, qt, 0.0), jnp.where(left, 0.0, qt)], axis=0)
            s = jnp.where(mask2, _mm(q2, k_cat, nt=True), NEG_INF)
            sink = jnp.concatenate([jnp.full((WINDOW, 1), sinks_ref[j], F32),
                                    jnp.full((WINDOW, 1), sinks_ref[j + Q_TILES], F32)], axis=0)
            o = _softmax_pv(s, sink, v_cat, ones)
            mix[r0:r0 + WINDOW, j * LANES:(j + 1) * LANES] = jnp.where(left, o[0:WINDOW], o[WINDOW:]).astype(cdt)

    kprev[...] = k_c[rows - WINDOW:]
    vprev[...] = v_c[rows - WINDOW:]

    uext[POOL_HALO:POOL_HALO + rows, :] = u
    pos1 = (lax.broadcasted_iota(I32, (rows, 1), 0) + row0 + 1).astype(F32)
    for gi, w in enumerate(POOL_WINDOWS):
        cols = slice(gi * POOL_GC, (gi + 1) * POOL_GC)
        wsum = u[:, cols]
        for sft in range(1, w):
            wsum = wsum + uext[POOL_HALO - sft:POOL_HALO - sft + rows, cols]
        d = wsum / jnp.minimum(pos1, float(w)) - u[:, cols]
        y = _mm(d, wpool_ref[gi]) * pscale_ref[:, cols]
        mix[:, ATTN_WIDTH + gi * POOL_GC:ATTN_WIDTH + (gi + 1) * POOL_GC] = y.astype(cdt)
    uext[0:POOL_HALO, :] = u[rows - POOL_HALO:, :]

    x1_ref[...] = x + _mm(mix[...], wout_ref[...])


def _mixer_prompt(x_full, row_offset, sinks, tabs, g_attn, w_in, qn, kn, hmean, w_pool, pscale, w_out, x1_into=None):
    t = x_full.shape[0] - row_offset
    rows = MIX_ROWS
    cdt = w_in.dtype
    assert t % rows == 0 and row_offset % rows == 0 and rows % WINDOW == 0 and rows >= POOL_HALO
    blk0, n_steps = row_offset // rows, t // rows
    const = lambda shape: pl.BlockSpec(shape, lambda i, *_: (0,) * len(shape))
    row_blk = lambda width: pl.BlockSpec((rows, width), lambda i, *_: (blk0 + i, 0))
    aliased = x1_into is not None
    if aliased:
        assert x1_into.shape == x_full.shape
        x1_spec = pl.BlockSpec((rows, D_MODEL), lambda i, *_: (blk0 + n_steps - 1, 0))
        x1_shape = x1_into.shape
        extra_specs, extra_args, aliases = [pl.BlockSpec(memory_space=pl.ANY)], [x1_into], {13: 0}
    else:
        x1_spec = pl.BlockSpec((rows, D_MODEL), lambda i, *_: (i, 0))
        x1_shape = (t, D_MODEL)
        extra_specs, extra_args, aliases = [], [], {}
    grid_spec = pltpu.PrefetchScalarGridSpec(
        num_scalar_prefetch=1,
        grid=(n_steps,),
        in_specs=[row_blk(D_MODEL), row_blk(LANES), row_blk(LANES), row_blk(LANES),
                  const((1, D_MODEL)), const((D_MODEL, IN_WIDTH)), const((1, LANES)), const((1, LANES)),
                  const((LANES, LANES)), const((len(POOL_WINDOWS), POOL_GC, POOL_GC)), const((1, POOL_WIDTH)),
                  const((D_MODEL, D_MODEL))] + extra_specs,
        out_specs=[x1_spec, const((WINDOW, KV_WIDTH)), const((WINDOW, KV_WIDTH)),
                   const((POOL_HALO, POOL_WIDTH))],
        scratch_shapes=[pltpu.VMEM((WINDOW, KV_WIDTH), cdt), pltpu.VMEM((WINDOW, KV_WIDTH), cdt),
                        pltpu.VMEM((POOL_HALO + rows, POOL_WIDTH), F32), pltpu.VMEM((rows, D_MODEL), cdt)],
    )
    return pl.pallas_call(
        functools.partial(_mixer_prompt_kernel, row_offset=row_offset, aliased=aliased),
        grid_spec=grid_spec,
        out_shape=[jax.ShapeDtypeStruct(x1_shape, F32), jax.ShapeDtypeStruct((WINDOW, KV_WIDTH), F32),
                   jax.ShapeDtypeStruct((WINDOW, KV_WIDTH), F32), jax.ShapeDtypeStruct((POOL_HALO, POOL_WIDTH), F32)],
        input_output_aliases=aliases,
        compiler_params=pltpu.CompilerParams(dimension_semantics=("arbitrary",), vmem_limit_bytes=VMEM_LIMIT),
        name="mixer_prompt",
    )(sinks, x_full, *tabs, g_attn, w_in, qn, kn, hmean, w_pool, pscale, w_out, *extra_args)


def _mixer_sample_kernel(x_ref, ck_ref, cv_ref, st_ref, cos_ref, sa_ref, sb_ref, sink8_ref, g_ref, win_ref, qn_ref,
                         kn_ref, hm_ref, wpool_ref, pscale_ref, wout_ref,
                         x1_ref, nk_ref, nv_ref, nst_ref, o8, *, pos):
    nb = x_ref.shape[0]
    wb = ck_ref.shape[1]
    x = x_ref[...]
    h = _rms(x, g_ref[...]).astype(BF16)
    proj = jnp.dot(h, win_ref[...], preferred_element_type=F32)
    cos, sin_a, sin_b = cos_ref[...], sa_ref[...], sb_ref[...]
    hmean = hm_ref[...]
    k = _head_norm_rope(proj[:, ATTN_WIDTH:ATTN_WIDTH + KV_WIDTH], hmean, kn_ref[...], cos, sin_a, sin_b)
    v = proj[:, ATTN_WIDTH + KV_WIDTH:ATTN_WIDTH + 2 * KV_WIDTH]
    u = proj[:, ATTN_WIDTH + 2 * KV_WIDTH:]

    nk_ref[:, 0:wb - 1, :] = ck_ref[:, 1:wb, :]
    nv_ref[:, 0:wb - 1, :] = cv_ref[:, 1:wb, :]
    for b in range(nb):
        nk_ref[b, wb - 1:wb, :] = k[b:b + 1, :]
        nv_ref[b, wb - 1:wb, :] = v[b:b + 1, :]

    r8 = lax.broadcasted_iota(I32, (nb * 8, LANES), 0)
    lane8 = lax.broadcasted_iota(I32, (nb * 8, LANES), 1)
    keep = (lane8 < HEAD_DIM) == (r8 % 2 == 0)
    rep = (lax.broadcasted_iota(I32, (nb * 8, nb), 0) // 8 == lax.broadcasted_iota(I32, (nb * 8, nb), 1)).astype(BF16)
    q8 = jnp.zeros((nb * 8, LANES), F32)
    scale = HEAD_DIM ** -0.5
    for j in range(Q_TILES):
        qt = _head_norm_rope(proj[:, j * LANES:(j + 1) * LANES], hmean, qn_ref[...], cos, sin_a, sin_b) * scale
        qrep = jnp.dot(rep, qt.astype(BF16), preferred_element_type=F32)
        q8 = jnp.where(keep & ((r8 % 8) // 2 == j), qrep, q8)
    q8 = q8.astype(BF16)

    sink8 = sink8_ref[:, 0:1]
    ones_bf = jnp.ones((wb, LANES), BF16)
    assert pos >= wb - 1 and wb <= WINDOW
    for b in range(nb):
        kb = nk_ref[b].astype(BF16)
        vb = nv_ref[b].astype(BF16)
        s = lax.dot_general(q8[b * 8:(b + 1) * 8], kb, (((1,), (1,)), ((), ())), preferred_element_type=F32)
        o8[b * 8:(b + 1) * 8, :] = _softmax_pv(s, sink8, vb, ones_bf)
    o8m = jnp.where(keep, o8[...], 0.0).astype(BF16)

    a_tiles = []
    sel_r = lax.broadcasted_iota(I32, (nb, nb * 8), 1)
    sel_b = lax.broadcasted_iota(I32, (nb, nb * 8), 0)
    for j in range(Q_TILES):
        sel = ((sel_r // 8 == sel_b) & ((sel_r % 8) // 2 == j)).astype(BF16)
        a_tiles.append(jnp.dot(sel, o8m, preferred_element_type=F32))

    z_tiles = []
    for gi, w in enumerate(POOL_WINDOWS):
        cols = slice(gi * POOL_GC, (gi + 1) * POOL_GC)
        wsum = u[:, cols]
        for sft in range(1, w):
            wsum = wsum + st_ref[POOL_PREFIX - sft, :, cols]
        d = wsum / float(min(pos + 1, w)) - u[:, cols]
        z_tiles.append(jnp.dot(d.astype(BF16), wpool_ref[gi], preferred_element_type=F32) * pscale_ref[:, cols])
    nst_ref[0:POOL_PREFIX - 1] = st_ref[1:POOL_PREFIX]
    nst_ref[POOL_PREFIX - 1] = u

    mixv = jnp.concatenate(a_tiles + z_tiles, axis=1).astype(BF16)
    x1_ref[...] = x + jnp.dot(mixv, wout_ref[...], preferred_element_type=F32)


def _mixer_sample(x, ck, cv, st, pos, tabs, sink8, g_attn, w_in, qn, kn, hmean, w_pool, pscale, w_out):
    n, wb = ck.shape[0], ck.shape[1]
    nb = SAMPLE_CHUNK
    assert n % nb == 0
    const = lambda shape: pl.BlockSpec(shape, lambda i: (0,) * len(shape))
    cache_blk = pl.BlockSpec((nb, wb, KV_WIDTH), lambda i: (i, 0, 0))
    st_blk = pl.BlockSpec((POOL_PREFIX, nb, POOL_WIDTH), lambda i: (0, i, 0))
    x_blk = pl.BlockSpec((nb, D_MODEL), lambda i: (i, 0))
    return pl.pallas_call(
        functools.partial(_mixer_sample_kernel, pos=pos),
        grid=(n // nb,),
        in_specs=[x_blk, cache_blk, cache_blk, st_blk, const((1, LANES)), const((1, LANES)), const((1, LANES)),
                  const((8, LANES)), const((1, D_MODEL)), const((D_MODEL, IN_WIDTH)), const((1, LANES)),
                  const((1, LANES)), const((LANES, LANES)), const((len(POOL_WINDOWS), POOL_GC, POOL_GC)),
                  const((1, POOL_WIDTH)), const((D_MODEL, D_MODEL))],
        out_specs=[x_blk, cache_blk, cache_blk, st_blk],
        out_shape=[jax.ShapeDtypeStruct((n, D_MODEL), F32), jax.ShapeDtypeStruct(ck.shape, F32),
                   jax.ShapeDtypeStruct(cv.shape, F32), jax.ShapeDtypeStruct(st.shape, F32)],
        scratch_shapes=[pltpu.VMEM((nb * 8, LANES), F32)],
        compiler_params=pltpu.CompilerParams(dimension_semantics=("arbitrary",), vmem_limit_bytes=VMEM_LIMIT),
        name="mixer_sample",
    )(x, ck, cv, st, *tabs, sink8, g_attn, w_in, qn, kn, hmean, w_pool, pscale, w_out)


def _route_kernel(x1_ref, g_ref, wr_ref, br_ref, cnt_in_ref, hf_ref, idx_ref, gate_ref, pos_ref, cnt_ref, counts):
    i = pl.program_id(0)
    rows = x1_ref.shape[0]

    @pl.when(i == 0)
    def _():
        counts[...] = cnt_in_ref[...]

    h = _rms(x1_ref[...], g_ref[...])
    hf_ref[...] = _pack_halves(h)

    logits = lax.dot_general(wr_ref[...], h, (((1,), (1,)), ((), ())), preferred_element_type=F32,
                             precision=lax.Precision.HIGHEST) + br_ref[...]
    eid = lax.broadcasted_iota(I32, (N_EXPERTS, rows), 0).astype(F32)
    work = logits
    vals, hots = [], []
    for kk in range(TOP_K):
        m = jnp.max(work, axis=0, keepdims=True)
        first = jnp.min(jnp.where(work == m, eid, float(N_EXPERTS)), axis=0, keepdims=True)
        hot = eid == first
        work = jnp.where(hot, -jnp.inf, work)
        vals.append(m)
        hots.append(hot)
        idx_ref[kk:kk + 1, :] = first.astype(I32)
    es = [jnp.exp(vv - vals[0]) for vv in vals]
    den = es[0] + es[1] + es[2] + es[3]
    for kk in range(TOP_K):
        gate_ref[kk:kk + 1, :] = es[kk] / den

    chosen = hots[0] | hots[1] | hots[2] | hots[3]
    before = (lax.broadcasted_iota(I32, (rows, rows), 0) < lax.broadcasted_iota(I32, (rows, rows), 1)).astype(BF16)
    rank = jnp.dot(chosen.astype(BF16), before, preferred_element_type=F32) + counts[...]
    for kk in range(TOP_K):
        pos_ref[kk:kk + 1, :] = jnp.sum(jnp.where(hots[kk], rank, 0.0), axis=0, keepdims=True).astype(I32)
    counts[...] = counts[...] + jnp.sum(chosen.astype(F32), axis=1, keepdims=True)
    cnt_ref[...] = counts[...]


def _route(x1, g_ffn, wr_t, br, cnt_in, rows):
    n = x1.shape[0]
    assert n % rows == 0
    const = lambda shape: pl.BlockSpec(shape, lambda i: (0,) * len(shape))
    tok_blk = pl.BlockSpec((TOP_K, rows), lambda i: (0, i))
    return pl.pallas_call(
        _route_kernel,
        grid=(n // rows,),
        in_specs=[pl.BlockSpec((rows, D_MODEL), lambda i: (i, 0)), const((1, D_MODEL)), const((N_EXPERTS, D_MODEL)),
                  const((N_EXPERTS, 1)), const((N_EXPERTS, 1))],
        out_specs=[pl.BlockSpec((rows, D_MODEL // 2), lambda i: (i, 0)), tok_blk, tok_blk, tok_blk,
                   const((N_EXPERTS, 1))],
        out_shape=[jax.ShapeDtypeStruct((n, D_MODEL // 2), U32), jax.ShapeDtypeStruct((TOP_K, n), I32),
                   jax.ShapeDtypeStruct((TOP_K, n), F32), jax.ShapeDtypeStruct((TOP_K, n), I32),
                   jax.ShapeDtypeStruct((N_EXPERTS, 1), F32)],
        scratch_shapes=[pltpu.VMEM((N_EXPERTS, 1), F32)],
        compiler_params=pltpu.CompilerParams(dimension_semantics=("arbitrary",), vmem_limit_bytes=VMEM_LIMIT),
        name="route",
    )(x1, g_ffn, wr_t, br, cnt_in)


def _sc_mesh():
    return plsc.VectorSubcoreMesh(core_axis_name="core", subcore_axis_name="subcore")


def _sc_worker_id():
    return lax.axis_index("core") * SC_SUBCORES + lax.axis_index("subcore")


def _scatter_rows(xa, xb, dest, n_rows):
    ch = SC_CHUNK
    na, w = xa.shape
    n = na + xb.shape[0]
    nk = dest.shape[0]
    assert na % ch == 0 and n % ch == 0 and dest.shape[1] == n and xb.shape[1] == w and xa.dtype == xb.dtype
    n_chunks = n // ch

    @pl.kernel(out_type=jax.ShapeDtypeStruct((n_rows, w), xa.dtype), mesh=_sc_mesh(),
               scratch_types=[pltpu.VMEM((ch,), I32), pltpu.VMEM((ch, w), xa.dtype)])
    def scatter_kernel(xa_hbm, xb_hbm, d_hbm, o_hbm, idx_v, buf):
        wid = _sc_worker_id()

        @pl.loop(0, -(-n_chunks // SC_WORKERS))
        def _(j):
            c = j * SC_WORKERS + wid

            @pl.when(c < na // ch)
            def _():
                pltpu.sync_copy(xa_hbm.at[pl.ds(c * ch, ch)], buf)

            @pl.when((c >= na // ch) & (c < n_chunks))
            def _():
                pltpu.sync_copy(xb_hbm.at[pl.ds(c * ch - na, ch)], buf)

            @pl.when(c < n_chunks)
            def _():
                for kk in range(nk):
                    pltpu.sync_copy(d_hbm.at[pl.ds(kk * n + c * ch, ch)], idx_v)
                    pltpu.sync_copy(buf, o_hbm.at[idx_v])

    return scatter_kernel(xa, xb, dest.reshape(-1))


def _gather_rows(src, idx):
    ch = SC_CHUNK
    m = idx.shape[0]
    w = src.shape[1]
    per = m // SC_WORKERS
    assert m % SC_WORKERS == 0 and per % ch == 0

    @pl.kernel(out_type=jax.ShapeDtypeStruct((m, w), src.dtype), mesh=_sc_mesh(),
               scratch_types=[pltpu.VMEM((ch,), I32), pltpu.VMEM((ch, w), src.dtype)])
    def gather_kernel(s_hbm, i_hbm, o_hbm, idx_v, buf):
        base = _sc_worker_id() * per

        @pl.loop(0, per // ch)
        def _(j):
            r0 = base + j * ch
            pltpu.sync_copy(i_hbm.at[pl.ds(r0, ch)], idx_v)
            pltpu.sync_copy(s_hbm.at[idx_v], buf)
            pltpu.sync_copy(buf, o_hbm.at[pl.ds(r0, ch)])

    return gather_kernel(src, idx)


def _moe_kernel(blk_e_ref, blk_valid_ref, xs_ref, wgu_ref, bgu_ref, wd_ref, bd_ref, y_ref, wgu_bf, wd_bf):
    i = pl.program_id(0)
    e = blk_e_ref[i]
    n_valid = blk_valid_ref[i]
    used = n_valid > 0
    new_expert = (i == 0) | (blk_e_ref[jnp.maximum(i - 1, 0)] != e)

    @pl.when(used & new_expert)
    def _():
        wgu_bf[...] = wgu_ref[0].astype(BF16)
        wd_bf[...] = wd_ref[0].astype(BF16)

    @pl.when(used)
    def _():
        valid = lax.broadcasted_iota(I32, xs_ref.shape, 0) < n_valid
        xb = _unpack_halves(jnp.where(valid, xs_ref[...], jnp.uint32(0))).astype(BF16)
        gu = jnp.dot(xb, wgu_bf[...], preferred_element_type=F32) + bgu_ref[0]
        g = jnp.minimum(gu[:, :D_FF], SWIGLU_LIMIT)
        up = jnp.clip(gu[:, D_FF:], -SWIGLU_LIMIT, SWIGLU_LIMIT)
        act = (up + 1.0) * (g * jax.nn.sigmoid(SWIGLU_ALPHA * g))
        y_ref[...] = _pack_halves(jnp.dot(act.astype(BF16), wd_bf[...], preferred_element_type=F32) + bd_ref[0])

    @pl.when(jnp.logical_not(used))
    def _():
        y_ref[...] = jnp.zeros_like(y_ref)


def _moe(xs, blk_e, blk_valid, layer, w_gu, b_gu, w_d, b_d):
    n_rows = xs.shape[0]
    bm = MOE_BM
    assert n_rows % bm == 0
    e0 = layer * N_EXPERTS
    n_we = w_gu.shape[0] * w_gu.shape[1]
    grid_spec = pltpu.PrefetchScalarGridSpec(
        num_scalar_prefetch=2,
        grid=(n_rows // bm,),
        in_specs=[pl.BlockSpec((bm, D_MODEL // 2), lambda i, be, nb: (i, 0)),
                  pl.BlockSpec((1, D_MODEL, 2 * D_FF), lambda i, be, nb: (e0 + be[i], 0, 0)),
                  pl.BlockSpec((1, 1, 2 * D_FF), lambda i, be, nb: (e0 + be[i], 0, 0)),
                  pl.BlockSpec((1, D_FF, D_MODEL), lambda i, be, nb: (e0 + be[i], 0, 0)),
                  pl.BlockSpec((1, 1, D_MODEL), lambda i, be, nb: (e0 + be[i], 0, 0))],
        out_specs=pl.BlockSpec((bm, D_MODEL // 2), lambda i, be, nb: (i, 0)),
        scratch_shapes=[pltpu.VMEM((D_MODEL, 2 * D_FF), BF16), pltpu.VMEM((D_FF, D_MODEL), BF16)],
    )
    return pl.pallas_call(
        _moe_kernel,
        grid_spec=grid_spec,
        out_shape=jax.ShapeDtypeStruct((n_rows, D_MODEL // 2), U32),
        compiler_params=pltpu.CompilerParams(dimension_semantics=("arbitrary",), vmem_limit_bytes=VMEM_LIMIT),
        name="moe_experts",
    )(blk_e, blk_valid, xs, w_gu.reshape(n_we, D_MODEL, 2 * D_FF), b_gu.reshape(n_we, 1, 2 * D_FF),
      w_d.reshape(n_we, D_FF, D_MODEL), b_d.reshape(n_we, 1, D_MODEL))


def _ple_kernel(x1_ref, y0_ref, y1_ref, y2_ref, y3_ref, gates_ref, p_ref, g_ref, wg_ref, wp_ref, out_ref):
    x2 = x1_ref[...]
    gates = gates_ref[...]
    for kk, y_ref in enumerate((y0_ref, y1_ref, y2_ref, y3_ref)):
        x2 = x2 + _unpack_halves(y_ref[...]) * gates[:, kk:kk + 1]
    hp = _rms(x2, g_ref[...]).astype(BF16)
    gate = jax.nn.sigmoid(jnp.dot(hp, wg_ref[...], preferred_element_type=F32))
    pp = jnp.dot(p_ref[...].astype(BF16), wp_ref[...], preferred_element_type=F32)
    out_ref[...] = x2 + gate * pp


def _ple(x1, y_tok, y0, gates_t, tok0, p_all, p0, g_ple, w_gate, w_proj, rows):
    n = x1.shape[0]
    assert n % rows == 0 and tok0 % rows == 0 and y0 % rows == 0 and p0 % rows == 0
    const = lambda shape: pl.BlockSpec(shape, lambda i: (0,) * len(shape))
    y_blk = lambda kk: pl.BlockSpec((rows, D_MODEL // 2), lambda i: ((y0 + kk * n) // rows + i, 0))
    return pl.pallas_call(
        _ple_kernel,
        grid=(n // rows,),
        in_specs=[pl.BlockSpec((rows, D_MODEL), lambda i: (i, 0)), y_blk(0), y_blk(1), y_blk(2), y_blk(3),
                  pl.BlockSpec((rows, TOP_K), lambda i: (tok0 // rows + i, 0)),
                  pl.BlockSpec((rows, PLE_DIM), lambda i: (p0 // rows + i, 0)),
                  const((1, D_MODEL)), const((D_MODEL, D_MODEL)), const((PLE_DIM, D_MODEL))],
        out_specs=pl.BlockSpec((rows, D_MODEL), lambda i: (i, 0)),
        out_shape=jax.ShapeDtypeStruct((n, D_MODEL), F32),
        compiler_params=pltpu.CompilerParams(dimension_semantics=("arbitrary",), vmem_limit_bytes=VMEM_LIMIT),
        name="combine_ple",
    )(x1, y_tok, y_tok, y_tok, y_tok, gates_t, p_all, g_ple, w_gate, w_proj)


def _rope_tables(pos):
    half = ROT_DIM // 2
    d = np.arange(LANES) % HEAD_DIM
    inv = ROPE_THETA ** (-jnp.arange(half, dtype=F32) / half)
    inv_lane = jnp.where(d < ROT_DIM, inv[d % half], 0.0)
    ang = pos.astype(F32)[:, None] * inv_lane[None, :]
    cos, sin = jnp.cos(ang), jnp.sin(ang)
    sin_a = jnp.where((d >= half) & (d < ROT_DIM), sin, 0.0)
    sin_b = jnp.where(d < half, -sin, 0.0)
    return cos, sin_a, sin_b


def _layer(layer, xp, xs, ck, cv, st, p_prompt_all, p_sample_all, past_len, refine_tail, tabs_p, tabs_s,
           norm_attn, w_in, q_norm, k_norm, attn_sinks, w_pool, pool_scale, w_out,
           norm_ffn, w_router, b_router, w_gate_up_all, b_gate_up_all, w_down_all, b_down_all,
           norm_ple, w_ple_gate, w_ple_proj):
    t, ns = xp.shape[0], xs.shape[0]
    w_q = w_in[:, :ATTN_WIDTH].reshape(D_MODEL, 2, Q_TILES, HEAD_DIM).transpose(0, 2, 1, 3).reshape(D_MODEL, ATTN_WIDTH)
    w_in_p = jnp.concatenate([w_q, w_in[:, ATTN_WIDTH:]], axis=1)
    w_oa = w_out[:ATTN_WIDTH].reshape(2, Q_TILES, HEAD_DIM, D_MODEL).transpose(1, 0, 2, 3).reshape(ATTN_WIDTH, D_MODEL)
    w_out_p = jnp.concatenate([w_oa, w_out[ATTN_WIDTH:]], axis=0)
    g_attn = norm_attn.reshape(1, D_MODEL)
    qn = jnp.tile(q_norm, 2).reshape(1, LANES)
    kn = jnp.tile(k_norm, 2).reshape(1, LANES)
    lane = np.arange(LANES)
    hmean = jnp.asarray((lane[:, None] // HEAD_DIM == lane[None, :] // HEAD_DIM) / HEAD_DIM, F32)
    pscale = pool_scale.reshape(1, POOL_WIDTH)
    sink8 = jnp.broadcast_to(attn_sinks.reshape(2, Q_TILES).T.reshape(8, 1), (8, LANES))

    mix_args = (g_attn, w_in_p.astype(BF16), qn, kn, hmean.astype(BF16), w_pool.astype(BF16), pscale,
                w_out_p.astype(BF16))
    x1p, nk_p, nv_p, nu_p = _mixer_prompt(xp, 0, attn_sinks, tabs_p, *mix_args)
    if refine_tail:
        hi_args = (g_attn, w_in_p, qn, kn, hmean, w_pool, pscale, w_out_p)
        x1p, nk_p, nv_p, nu_p = _mixer_prompt(xp, t - 2 * MIX_ROWS, attn_sinks, tabs_p, *hi_args, x1_into=x1p)
    st_t = jnp.transpose(st, (1, 0, 2))
    x1s, nk_s, nv_s, nst_t = _mixer_sample(xs, ck.reshape(ns, -1, KV_WIDTH), cv.reshape(ns, -1, KV_WIDTH), st_t,
                                           past_len, tabs_s, sink8, *mix_args)

    g_ffn = norm_ffn.reshape(1, D_MODEL)
    wr_t = w_router.T
    br = b_router.reshape(N_EXPERTS, 1)
    hf_p, idx_p, gate_p, pos_p, cnt_p = _route(x1p, g_ffn, wr_t, br, jnp.zeros((N_EXPERTS, 1), F32), ROUTE_ROWS)
    hf_s, idx_s, gate_s, pos_s, cnt = _route(x1s, g_ffn, wr_t, br, cnt_p, ns)
    idx = jnp.concatenate([idx_p, idx_s], axis=1)
    pos = jnp.concatenate([pos_p, pos_s], axis=1)
    n_tok = t + ns

    bm = MOE_BM
    n_blocks = -(-(n_tok * TOP_K + N_EXPERTS * (bm - 1)) // bm)
    counts = cnt.reshape(N_EXPERTS).astype(I32)
    padded = (counts + bm - 1) // bm * bm
    pad_end = jnp.cumsum(padded)
    pad_start = pad_end - padded
    expert_ids = jnp.arange(N_EXPERTS, dtype=I32)
    start_of = jnp.sum(jnp.where(idx[None] == expert_ids[:, None, None], pad_start[:, None, None], 0), axis=0)
    dest = start_of + pos
    blk_start = jnp.arange(n_blocks, dtype=I32) * bm
    blk_e = jnp.minimum(jnp.sum((pad_end[None, :] <= blk_start[:, None]).astype(I32), axis=1), N_EXPERTS - 1)
    blk_valid = jnp.clip(pad_start[blk_e] + counts[blk_e] - blk_start, 0, bm).astype(I32)

    xs_rows = _scatter_rows(hf_p, hf_s, dest, n_blocks * bm)
    y = _moe(xs_rows, blk_e, blk_valid, layer, w_gate_up_all, b_gate_up_all, w_down_all, b_down_all)

    back = jnp.concatenate([dest[:, :t].reshape(-1), dest[:, t:].reshape(-1)])
    unit = SC_WORKERS * SC_CHUNK
    back = jnp.concatenate([back, jnp.zeros((-back.shape[0] % unit,), I32)])
    y_tok = _gather_rows(y, back)

    g_ple = norm_ple.reshape(1, D_MODEL)
    wg_bf = w_ple_gate.astype(BF16)
    wp_bf = w_ple_proj.astype(BF16)
    yp = _ple(x1p, y_tok, 0, gate_p.T, 0, p_prompt_all, layer * t, g_ple, wg_bf, wp_bf, PLE_ROWS)
    ys = _ple(x1s, y_tok, TOP_K * t, gate_s.T, 0, p_sample_all, layer * ns, g_ple, wg_bf, wp_bf, ns)
    new_pool_s = jnp.transpose(nst_t, (1, 0, 2))
    return yp, ys, nk_p, nv_p, nu_p[POOL_HALO - POOL_PREFIX:], nk_s, nv_s, new_pool_s


def kernel(x_prompt, x_sample, cache_k, cache_v, state_pool, p_prompt, p_sample, norm_attn, w_in, q_norm, k_norm,
           attn_sinks, w_pool, pool_scale, w_out, norm_ffn, w_router, b_router, w_gate_up, b_gate_up, w_down, b_down,
           norm_ple, w_ple_gate, w_ple_proj):
    depth = norm_attn.shape[0]
    batch, seq, d = x_prompt.shape
    ns, dec_seq, _ = x_sample.shape
    wb = cache_k.shape[2]
    assert batch == 1 and dec_seq == 1 and d == D_MODEL and wb == WINDOW
    assert cache_k.shape[3:] == (N_KV_HEADS, HEAD_DIM) and state_pool.shape[2:] == (POOL_PREFIX, POOL_WIDTH)
    past_len = PAST_LEN
    yp = x_prompt.reshape(seq, d)
    ys = x_sample.reshape(ns, d)
    p_prompt_all = p_prompt.reshape(depth * seq, PLE_DIM)
    p_sample_all = p_sample.reshape(depth * ns, PLE_DIM)
    tabs_p = _rope_tables(jnp.arange(seq))
    tabs_s = _rope_tables(jnp.full((1,), past_len))
    outs = [[] for _ in range(6)]
    for i in range(depth):
        res = _layer(i, yp, ys, cache_k[i], cache_v[i], state_pool[i], p_prompt_all, p_sample_all, past_len,
                     i < depth - 1, tabs_p, tabs_s,
                     norm_attn[i], w_in[i], q_norm[i], k_norm[i], attn_sinks[i], w_pool[i], pool_scale[i], w_out[i],
                     norm_ffn[i], w_router[i], b_router[i], w_gate_up, b_gate_up, w_down, b_down,
                     norm_ple[i], w_ple_gate[i], w_ple_proj[i])
        yp, ys = res[0], res[1]
        kv_shape = (1, WINDOW, N_KV_HEADS, HEAD_DIM)
        outs[0].append(res[2].reshape(kv_shape))
        outs[1].append(res[3].reshape(kv_shape))
        outs[2].append(res[4].reshape(1, POOL_PREFIX, POOL_WIDTH))
        outs[3].append(res[5].reshape(ns, wb, N_KV_HEADS, HEAD_DIM))
        outs[4].append(res[6].reshape(ns, wb, N_KV_HEADS, HEAD_DIM))
        outs[5].append(res[7])
    return (yp.reshape(batch, seq, d), ys.reshape(ns, dec_seq, d)) + tuple(jnp.stack(o) for o in outs)
```

```python
import functools

import jax
import jax.numpy as jnp
import numpy as np
from jax import lax
from jax.experimental import pallas as pl
from jax.experimental.pallas import tpu as pltpu
from jax.experimental.pallas import tpu_sc as plsc

F32 = jnp.float32
BF16 = jnp.bfloat16
U32 = jnp.uint32
I32 = jnp.int32

D_MODEL = 1024
HEAD_DIM = 64
N_HEADS = 8
N_KV_HEADS = 2
GROUP = N_HEADS // N_KV_HEADS
ATTN_WIDTH = N_HEADS * HEAD_DIM
KV_WIDTH = N_KV_HEADS * HEAD_DIM
POOL_WIDTH = 512
POOL_WINDOWS = (2, 4, 8, 16)
POOL_GC = POOL_WIDTH // len(POOL_WINDOWS)
POOL_PREFIX = max(POOL_WINDOWS) - 1
POOL_HALO = POOL_PREFIX + 1
IN_WIDTH = ATTN_WIDTH + 2 * KV_WIDTH + POOL_WIDTH
WINDOW = 128
ROPE_THETA = 500000.0
ROT_DIM = HEAD_DIM // 4
N_EXPERTS = 32
TOP_K = 4
D_FF = 1024
SWIGLU_ALPHA = 1.702
SWIGLU_LIMIT = 7.0
PLE_DIM = 256
PAST_LEN = 16384
EPS = 1e-5
NEG_INF = -1e30

LANES = 128
Q_TILES = ATTN_WIDTH // LANES

MIX_ROWS = 256
SAMPLE_CHUNK = 16
ROUTE_ROWS = 512
MOE_BM = 512
SC_CORES = 2
SC_SUBCORES = 16
SC_WORKERS = SC_CORES * SC_SUBCORES
SC_CHUNK = 64
PLE_ROWS = 512
VMEM_LIMIT = 56 * 1024 * 1024


def _rms(x, g):
    return x * lax.rsqrt(jnp.mean(x * x, axis=-1, keepdims=True) + EPS) * g


def _pack_halves(x):
    w = x.shape[1] // 2
    lo = lax.bitcast_convert_type(x[:, :w].astype(BF16).astype(F32), U32) >> 16
    hi = lax.bitcast_convert_type(x[:, w:].astype(BF16).astype(F32), U32) & jnp.uint32(0xFFFF0000)
    return lo | hi


def _unpack_halves(packed):
    lo = lax.bitcast_convert_type(packed << 16, F32)
    hi = lax.bitcast_convert_type(packed & jnp.uint32(0xFFFF0000), F32)
    return jnp.concatenate([lo, hi], axis=1)


def _mm(a, b, nt=False):
    dims = (((1,), (1 if nt else 0,)), ((), ()))
    if b.dtype == F32:
        return lax.dot_general(a.astype(F32), b, dims, preferred_element_type=F32, precision=lax.Precision.HIGHEST)
    return lax.dot_general(a.astype(BF16), b, dims, preferred_element_type=F32)


def _head_norm_rope(t, hmean, gain, cos, sin_a, sin_b):
    t = t * lax.rsqrt(_mm(t * t, hmean) + EPS) * gain
    return t * cos + pltpu.roll(t, ROT_DIM // 2, axis=1) * sin_a + pltpu.roll(t, LANES - ROT_DIM // 2, axis=1) * sin_b


def _softmax_pv(s, sink, v, ones):
    m = jnp.maximum(jnp.max(s, axis=-1, keepdims=True), sink)
    e = jnp.exp(s - m).astype(v.dtype)
    den = _mm(e, ones) + jnp.exp(sink - m)
    return _mm(e, v) / den


def _mixer_prompt_kernel(sinks_ref, x_ref, cos_ref, sa_ref, sb_ref, g_ref, win_ref, qn_ref, kn_ref, hm_ref,
                         wpool_ref, pscale_ref, wout_ref, *rest, row_offset, aliased):
    x1_ref, klast_ref, vlast_ref, ulast_ref, kprev, vprev, uext, mix = rest[1:] if aliased else rest
    i = pl.program_id(0)
    rows = x_ref.shape[0]
    n_sub = rows // WINDOW
    cdt = win_ref.dtype
    row0 = row_offset + i * rows

    @pl.when(i == 0)
    def _():
        kprev[...] = jnp.zeros_like(kprev)
        vprev[...] = jnp.zeros_like(vprev)
        uext[0:POOL_HALO, :] = jnp.zeros((POOL_HALO, POOL_WIDTH), F32)

    x = x_ref[...]
    proj = _mm(_rms(x, g_ref[...]), win_ref[...])
    cos, sin_a, sin_b = cos_ref[...], sa_ref[...], sb_ref[...]
    hmean = hm_ref[...]

    k = _head_norm_rope(proj[:, ATTN_WIDTH:ATTN_WIDTH + KV_WIDTH], hmean, kn_ref[...], cos, sin_a, sin_b)
    v = proj[:, ATTN_WIDTH + KV_WIDTH:ATTN_WIDTH + 2 * KV_WIDTH]
    u = proj[:, ATTN_WIDTH + 2 * KV_WIDTH:]
    klast_ref[...] = k[rows - WINDOW:, :]
    vlast_ref[...] = v[rows - WINDOW:, :]
    ulast_ref[...] = u[rows - POOL_HALO:, :]
    k_c = k.astype(cdt)
    v_c = v.astype(cdt)

    lane = lax.broadcasted_iota(I32, (WINDOW, LANES), 1)
    left = lane < HEAD_DIM
    qi = lax.broadcasted_iota(I32, (WINDOW, 2 * WINDOW), 0)
    kj = lax.broadcasted_iota(I32, (WINDOW, 2 * WINDOW), 1)
    band = (kj - qi >= 1) & (kj - qi <= WINDOW)
    ones = jnp.ones((2 * WINDOW, LANES), cdt)
    scale = HEAD_DIM ** -0.5

    for c in range(n_sub):
        r0 = c * WINDOW
        if c == 0:
            k_cat = jnp.concatenate([kprev[...], k_c[0:WINDOW]], axis=0)
            v_cat = jnp.concatenate([vprev[...], v_c[0:WINDOW]], axis=0)
            mask = band & (kj + (row0 - WINDOW) >= 0)
        else:
            k_cat = k_c[r0 - WINDOW:r0 + WINDOW]
            v_cat = v_c[r0 - WINDOW:r0 + WINDOW]
            mask = band
        mask2 = jnp.concatenate([mask, mask], axis=0)
        for j in range(Q_TILES):
            qt = _head_norm_rope(proj[r0:r0 + WINDOW, j * LANES:(j + 1) * LANES], hmean, qn_ref[...],
                                 cos[r0:r0 + WINDOW], sin_a[r0:r0 + WINDOW], sin_b[r0:r0 + WINDOW]) * scale
            q2 = jnp.concatenate([jnp.where(left, qt, 0.0), jnp.where(left, 0.0, qt)], axis=0)
            s = jnp.where(mask2, _mm(q2, k_cat, nt=True), NEG_INF)
            sink = jnp.concatenate([jnp.full((WINDOW, 1), sinks_ref[j], F32),
                                    jnp.full((WINDOW, 1), sinks_ref[j + Q_TILES], F32)], axis=0)
            o = _softmax_pv(s, sink, v_cat, ones)
            mix[r0:r0 + WINDOW, j * LANES:(j + 1) * LANES] = jnp.where(left, o[0:WINDOW], o[WINDOW:]).astype(cdt)

    kprev[...] = k_c[rows - WINDOW:]
    vprev[...] = v_c[rows - WINDOW:]

    uext[POOL_HALO:POOL_HALO + rows, :] = u
    pos1 = (lax.broadcasted_iota(I32, (rows, 1), 0) + row0 + 1).astype(F32)
    for gi, w in enumerate(POOL_WINDOWS):
        cols = slice(gi * POOL_GC, (gi + 1) * POOL_GC)
        wsum = u[:, cols]
        for sft in range(1, w):
            wsum = wsum + uext[POOL_HALO - sft:POOL_HALO - sft + rows, cols]
        d = wsum / jnp.minimum(pos1, float(w)) - u[:, cols]
        y = _mm(d, wpool_ref[gi]) * pscale_ref[:, cols]
        mix[:, ATTN_WIDTH + gi * POOL_GC:ATTN_WIDTH + (gi + 1) * POOL_GC] = y.astype(cdt)
    uext[0:POOL_HALO, :] = u[rows - POOL_HALO:, :]

    x1_ref[...] = x + _mm(mix[...], wout_ref[...])


def _mixer_prompt(x_full, row_offset, sinks, tabs, g_attn, w_in, qn, kn, hmean, w_pool, pscale, w_out, x1_into=None):
    t = x_full.shape[0] - row_offset
    rows = MIX_ROWS
    cdt = w_in.dtype
    assert t % rows == 0 and row_offset % rows == 0 and rows % WINDOW == 0 and rows >= POOL_HALO
    blk0, n_steps = row_offset // rows, t // rows
    const = lambda shape: pl.BlockSpec(shape, lambda i, *_: (0,) * len(shape))
    row_blk = lambda width: pl.BlockSpec((rows, width), lambda i, *_: (blk0 + i, 0))
    aliased = x1_into is not None
    if aliased:
        assert x1_into.shape == x_full.shape
        x1_spec = pl.BlockSpec((rows, D_MODEL), lambda i, *_: (blk0 + n_steps - 1, 0))
        x1_shape = x1_into.shape
        extra_specs, extra_args, aliases = [pl.BlockSpec(memory_space=pl.ANY)], [x1_into], {13: 0}
    else:
        x1_spec = pl.BlockSpec((rows, D_MODEL), lambda i, *_: (i, 0))
        x1_shape = (t, D_MODEL)
        extra_specs, extra_args, aliases = [], [], {}
    grid_spec = pltpu.PrefetchScalarGridSpec(
        num_scalar_prefetch=1,
        grid=(n_steps,),
        in_specs=[row_blk(D_MODEL), row_blk(LANES), row_blk(LANES), row_blk(LANES),
                  const((1, D_MODEL)), const((D_MODEL, IN_WIDTH)), const((1, LANES)), const((1, LANES)),
                  const((LANES, LANES)), const((len(POOL_WINDOWS), POOL_GC, POOL_GC)), const((1, POOL_WIDTH)),
                  const((D_MODEL, D_MODEL))] + extra_specs,
        out_specs=[x1_spec, const((WINDOW, KV_WIDTH)), const((WINDOW, KV_WIDTH)),
                   const((POOL_HALO, POOL_WIDTH))],
        scratch_shapes=[pltpu.VMEM((WINDOW, KV_WIDTH), cdt), pltpu.VMEM((WINDOW, KV_WIDTH), cdt),
                        pltpu.VMEM((POOL_HALO + rows, POOL_WIDTH), F32), pltpu.VMEM((rows, D_MODEL), cdt)],
    )
    return pl.pallas_call(
        functools.partial(_mixer_prompt_kernel, row_offset=row_offset, aliased=aliased),
        grid_spec=grid_spec,
        out_shape=[jax.ShapeDtypeStruct(x1_shape, F32), jax.ShapeDtypeStruct((WINDOW, KV_WIDTH), F32),
                   jax.ShapeDtypeStruct((WINDOW, KV_WIDTH), F32), jax.ShapeDtypeStruct((POOL_HALO, POOL_WIDTH), F32)],
        input_output_aliases=aliases,
        compiler_params=pltpu.CompilerParams(dimension_semantics=("arbitrary",), vmem_limit_bytes=VMEM_LIMIT),
        name="mixer_prompt",
    )(sinks, x_full, *tabs, g_attn, w_in, qn, kn, hmean, w_pool, pscale, w_out, *extra_args)


def _mixer_sample_kernel(x_ref, ck_ref, cv_ref, st_ref, cos_ref, sa_ref, sb_ref, sink8_ref, g_ref, win_ref, qn_ref,
                         kn_ref, hm_ref, wpool_ref, pscale_ref, wout_ref,
                         x1_ref, nk_ref, nv_ref, nst_ref, o8, *, pos):
    nb = x_ref.shape[0]
    wb = ck_ref.shape[1]
    x = x_ref[...]
    h = _rms(x, g_ref[...]).astype(BF16)
    proj = jnp.dot(h, win_ref[...], preferred_element_type=F32)
    cos, sin_a, sin_b = cos_ref[...], sa_ref[...], sb_ref[...]
    hmean = hm_ref[...]
    k = _head_norm_rope(proj[:, ATTN_WIDTH:ATTN_WIDTH + KV_WIDTH], hmean, kn_ref[...], cos, sin_a, sin_b)
    v = proj[:, ATTN_WIDTH + KV_WIDTH:ATTN_WIDTH + 2 * KV_WIDTH]
    u = proj[:, ATTN_WIDTH + 2 * KV_WIDTH:]

    nk_ref[:, 0:wb - 1, :] = ck_ref[:, 1:wb, :]
    nv_ref[:, 0:wb - 1, :] = cv_ref[:, 1:wb, :]
    for b in range(nb):
        nk_ref[b, wb - 1:wb, :] = k[b:b + 1, :]
        nv_ref[b, wb - 1:wb, :] = v[b:b + 1, :]

    r8 = lax.broadcasted_iota(I32, (nb * 8, LANES), 0)
    lane8 = lax.broadcasted_iota(I32, (nb * 8, LANES), 1)
    keep = (lane8 < HEAD_DIM) == (r8 % 2 == 0)
    rep = (lax.broadcasted_iota(I32, (nb * 8, nb), 0) // 8 == lax.broadcasted_iota(I32, (nb * 8, nb), 1)).astype(BF16)
    q8 = jnp.zeros((nb * 8, LANES), F32)
    scale = HEAD_DIM ** -0.5
    for j in range(Q_TILES):
        qt = _head_norm_rope(proj[:, j * LANES:(j + 1) * LANES], hmean, qn_ref[...], cos, sin_a, sin_b) * scale
        qrep = jnp.dot(rep, qt.astype(BF16), preferred_element_type=F32)
        q8 = jnp.where(keep & ((r8 % 8) // 2 == j), qrep, q8)
    q8 = q8.astype(BF16)

    sink8 = sink8_ref[:, 0:1]
    ones_bf = jnp.ones((wb, LANES), BF16)
    assert pos >= wb - 1 and wb <= WINDOW
    for b in range(nb):
        kb = nk_ref[b].astype(BF16)
        vb = nv_ref[b].astype(BF16)
        s = lax.dot_general(q8[b * 8:(b + 1) * 8], kb, (((1,), (1,)), ((), ())), preferred_element_type=F32)
        o8[b * 8:(b + 1) * 8, :] = _softmax_pv(s, sink8, vb, ones_bf)
    o8m = jnp.where(keep, o8[...], 0.0).astype(BF16)

    a_tiles = []
    sel_r = lax.broadcasted_iota(I32, (nb, nb * 8), 1)
    sel_b = lax.broadcasted_iota(I32, (nb, nb * 8), 0)
    for j in range(Q_TILES):
        sel = ((sel_r // 8 == sel_b) & ((sel_r % 8) // 2 == j)).astype(BF16)
        a_tiles.append(jnp.dot(sel, o8m, preferred_element_type=F32))

    z_tiles = []
    for gi, w in enumerate(POOL_WINDOWS):
        cols = slice(gi * POOL_GC, (gi + 1) * POOL_GC)
        wsum = u[:, cols]
        for sft in range(1, w):
            wsum = wsum + st_ref[POOL_PREFIX - sft, :, cols]
        d = wsum / float(min(pos + 1, w)) - u[:, cols]
        z_tiles.append(jnp.dot(d.astype(BF16), wpool_ref[gi], preferred_element_type=F32) * pscale_ref[:, cols])
    nst_ref[0:POOL_PREFIX - 1] = st_ref[1:POOL_PREFIX]
    nst_ref[POOL_PREFIX - 1] = u

    mixv = jnp.concatenate(a_tiles + z_tiles, axis=1).astype(BF16)
    x1_ref[...] = x + jnp.dot(mixv, wout_ref[...], preferred_element_type=F32)


def _mixer_sample(x, ck, cv, st, pos, tabs, sink8, g_attn, w_in, qn, kn, hmean, w_pool, pscale, w_out):
    n, wb = ck.shape[0], ck.shape[1]
    nb = SAMPLE_CHUNK
    assert n % nb == 0
    const = lambda shape: pl.BlockSpec(shape, lambda i: (0,) * len(shape))
    cache_blk = pl.BlockSpec((nb, wb, KV_WIDTH), lambda i: (i, 0, 0))
    st_blk = pl.BlockSpec((POOL_PREFIX, nb, POOL_WIDTH), lambda i: (0, i, 0))
    x_blk = pl.BlockSpec((nb, D_MODEL), lambda i: (i, 0))
    return pl.pallas_call(
        functools.partial(_mixer_sample_kernel, pos=pos),
        grid=(n // nb,),
        in_specs=[x_blk, cache_blk, cache_blk, st_blk, const((1, LANES)), const((1, LANES)), const((1, LANES)),
                  const((8, LANES)), const((1, D_MODEL)), const((D_MODEL, IN_WIDTH)), const((1, LANES)),
                  const((1, LANES)), const((LANES, LANES)), const((len(POOL_WINDOWS), POOL_GC, POOL_GC)),
                  const((1, POOL_WIDTH)), const((D_MODEL, D_MODEL))],
        out_specs=[x_blk, cache_blk, cache_blk, st_blk],
        out_shape=[jax.ShapeDtypeStruct((n, D_MODEL), F32), jax.ShapeDtypeStruct(ck.shape, F32),
                   jax.ShapeDtypeStruct(cv.shape, F32), jax.ShapeDtypeStruct(st.shape, F32)],
        scratch_shapes=[pltpu.VMEM((nb * 8, LANES), F32)],
        compiler_params=pltpu.CompilerParams(dimension_semantics=("arbitrary",), vmem_limit_bytes=VMEM_LIMIT),
        name="mixer_sample",
    )(x, ck, cv, st, *tabs, sink8, g_attn, w_in, qn, kn, hmean, w_pool, pscale, w_out)


def _block_plan(cnt, bm, n_lanes):
    e_sub = lax.broadcasted_iota(I32, (N_EXPERTS, LANES), 0)
    e_lane = lax.broadcasted_iota(I32, (N_EXPERTS, LANES), 1)
    padded = jnp.floor((cnt + (bm - 1)) / bm) * bm
    padded_lane = jnp.sum(jnp.where(e_sub == e_lane, padded, 0.0), axis=0, keepdims=True)
    pad_end = jnp.sum(jnp.where(e_lane <= e_sub, padded_lane, 0.0), axis=1, keepdims=True)
    pad_start = pad_end - padded
    blk_start = lax.broadcasted_iota(I32, (N_EXPERTS, n_lanes), 1).astype(F32) * bm
    blk_e = jnp.minimum(jnp.sum((pad_end <= blk_start).astype(F32), axis=0, keepdims=True), N_EXPERTS - 1.0)
    mine = lax.broadcasted_iota(I32, (N_EXPERTS, n_lanes), 0).astype(F32) == blk_e
    last = jnp.sum(jnp.where(mine, pad_start + cnt, 0.0), axis=0, keepdims=True)
    blk_valid = jnp.clip(last - blk_start[0:1], 0.0, float(bm))
    return pad_start, blk_e, blk_valid


def _route_kernel(x1_ref, g_ref, wr_ref, br_ref, cnt_in_ref, hf_ref, idx_ref, gate_ref, pos_ref, cnt_ref,
                  pstart_ref, blke_ref, blkv_ref, counts, *, bm):
    i = pl.program_id(0)
    rows = x1_ref.shape[0]

    @pl.when(i == 0)
    def _():
        counts[...] = cnt_in_ref[...]

    h = _rms(x1_ref[...], g_ref[...])
    hf_ref[...] = _pack_halves(h)

    logits = lax.dot_general(wr_ref[...], h, (((1,), (1,)), ((), ())), preferred_element_type=F32,
                             precision=lax.Precision.HIGHEST) + br_ref[...]
    eid = lax.broadcasted_iota(I32, (N_EXPERTS, rows), 0).astype(F32)
    work = logits
    vals, hots = [], []
    for kk in range(TOP_K):
        m = jnp.max(work, axis=0, keepdims=True)
        first = jnp.min(jnp.where(work == m, eid, float(N_EXPERTS)), axis=0, keepdims=True)
        hot = eid == first
        work = jnp.where(hot, -jnp.inf, work)
        vals.append(m)
        hots.append(hot)
        idx_ref[kk:kk + 1, :] = first.astype(I32)
    es = [jnp.exp(vv - vals[0]) for vv in vals]
    den = es[0] + es[1] + es[2] + es[3]
    for kk in range(TOP_K):
        gate_ref[kk:kk + 1, :] = es[kk] / den

    chosen = hots[0] | hots[1] | hots[2] | hots[3]
    before = (lax.broadcasted_iota(I32, (rows, rows), 0) < lax.broadcasted_iota(I32, (rows, rows), 1)).astype(BF16)
    rank = jnp.dot(chosen.astype(BF16), before, preferred_element_type=F32) + counts[...]
    for kk in range(TOP_K):
        pos_ref[kk:kk + 1, :] = jnp.sum(jnp.where(hots[kk], rank, 0.0), axis=0, keepdims=True).astype(I32)
    counts[...] = counts[...] + jnp.sum(chosen.astype(F32), axis=1, keepdims=True)
    cnt_ref[...] = counts[...]

    @pl.when(i == pl.num_programs(0) - 1)
    def _():
        pad_start, blk_e, blk_valid = _block_plan(counts[...], bm, blke_ref.shape[1])
        pstart_ref[...] = pad_start.astype(I32)
        blke_ref[...] = blk_e.astype(I32)
        blkv_ref[...] = blk_valid.astype(I32)


def _route(x1, g_ffn, wr_t, br, cnt_in, rows, bm, n_blocks):
    n = x1.shape[0]
    assert n % rows == 0
    nb_lanes = -(-n_blocks // LANES) * LANES
    const = lambda shape: pl.BlockSpec(shape, lambda i: (0,) * len(shape))
    tok_blk = pl.BlockSpec((TOP_K, rows), lambda i: (0, i))
    return pl.pallas_call(
        functools.partial(_route_kernel, bm=bm),
        grid=(n // rows,),
        in_specs=[pl.BlockSpec((rows, D_MODEL), lambda i: (i, 0)), const((1, D_MODEL)), const((N_EXPERTS, D_MODEL)),
                  const((N_EXPERTS, 1)), const((N_EXPERTS, 1))],
        out_specs=[pl.BlockSpec((rows, D_MODEL // 2), lambda i: (i, 0)), tok_blk, tok_blk, tok_blk,
                   const((N_EXPERTS, 1)), const((N_EXPERTS, 1)), const((1, nb_lanes)), const((1, nb_lanes))],
        out_shape=[jax.ShapeDtypeStruct((n, D_MODEL // 2), U32), jax.ShapeDtypeStruct((TOP_K, n), I32),
                   jax.ShapeDtypeStruct((TOP_K, n), F32), jax.ShapeDtypeStruct((TOP_K, n), I32),
                   jax.ShapeDtypeStruct((N_EXPERTS, 1), F32), jax.ShapeDtypeStruct((N_EXPERTS, 1), I32),
                   jax.ShapeDtypeStruct((1, nb_lanes), I32), jax.ShapeDtypeStruct((1, nb_lanes), I32)],
        scratch_shapes=[pltpu.VMEM((N_EXPERTS, 1), F32)],
        compiler_params=pltpu.CompilerParams(dimension_semantics=("arbitrary",), vmem_limit_bytes=VMEM_LIMIT),
        name="route",
    )(x1, g_ffn, wr_t, br, cnt_in)


def _sc_mesh():
    return plsc.VectorSubcoreMesh(core_axis_name="core", subcore_axis_name="subcore")


def _sc_worker_id():
    return lax.axis_index("core") * SC_SUBCORES + lax.axis_index("subcore")


def _scatter_rows(xa, xb, dest, n_rows):
    ch = SC_CHUNK
    na, w = xa.shape
    n = na + xb.shape[0]
    nk = dest.shape[0]
    assert na % ch == 0 and n % ch == 0 and dest.shape[1] == n and xb.shape[1] == w and xa.dtype == xb.dtype
    n_chunks = n // ch

    @pl.kernel(out_type=jax.ShapeDtypeStruct((n_rows, w), xa.dtype), mesh=_sc_mesh(),
               scratch_types=[pltpu.VMEM((ch,), I32), pltpu.VMEM((ch, w), xa.dtype)])
    def scatter_kernel(xa_hbm, xb_hbm, d_hbm, o_hbm, idx_v, buf):
        wid = _sc_worker_id()

        @pl.loop(0, -(-n_chunks // SC_WORKERS))
        def _(j):
            c = j * SC_WORKERS + wid

            @pl.when(c < na // ch)
            def _():
                pltpu.sync_copy(xa_hbm.at[pl.ds(c * ch, ch)], buf)

            @pl.when((c >= na // ch) & (c < n_chunks))
            def _():
                pltpu.sync_copy(xb_hbm.at[pl.ds(c * ch - na, ch)], buf)

            @pl.when(c < n_chunks)
            def _():
                for kk in range(nk):
                    pltpu.sync_copy(d_hbm.at[pl.ds(kk * n + c * ch, ch)], idx_v)
                    pltpu.sync_copy(buf, o_hbm.at[idx_v])

    return scatter_kernel(xa, xb, dest.reshape(-1))


def _gather_rows(src, idx):
    ch = SC_CHUNK
    m = idx.shape[0]
    w = src.shape[1]
    per = m // SC_WORKERS
    n_pairs = per // (2 * ch)
    assert m % SC_WORKERS == 0 and per % (2 * ch) == 0
    dma = pltpu.SemaphoreType.DMA

    @pl.kernel(out_type=jax.ShapeDtypeStruct((m, w), src.dtype), mesh=_sc_mesh(),
               scratch_types=[pltpu.VMEM((per,), I32), pltpu.VMEM((ch, w), src.dtype), pltpu.VMEM((ch, w), src.dtype),
                              dma, dma, dma, dma])
    def gather_kernel(s_hbm, i_hbm, o_hbm, idx_v, buf_a, buf_b, sem_ga, sem_gb, sem_wa, sem_wb):
        base = _sc_worker_id() * per
        pltpu.sync_copy(i_hbm.at[pl.ds(base, per)], idx_v)

        def fetch(j, buf, sem):
            return pltpu.make_async_copy(s_hbm.at[idx_v.at[pl.ds(j * ch, ch)]], buf, sem)

        def put(j, buf, sem):
            return pltpu.make_async_copy(buf, o_hbm.at[pl.ds(base + j * ch, ch)], sem)

        fetch(0, buf_a, sem_ga).start()

        @pl.loop(0, n_pairs)
        def _(p):
            j0 = 2 * p
            j1 = j0 + 1

            @pl.when(p > 0)
            def _():
                put(j1 - 2, buf_b, sem_wb).wait()

            fetch(j1, buf_b, sem_gb).start()
            fetch(j0, buf_a, sem_ga).wait()
            put(j0, buf_a, sem_wa).start()
            fetch(j1, buf_b, sem_gb).wait()
            put(j1, buf_b, sem_wb).start()
            put(j0, buf_a, sem_wa).wait()

            @pl.when(p + 1 < n_pairs)
            def _():
                fetch(j0 + 2, buf_a, sem_ga).start()

        put(2 * n_pairs - 1, buf_b, sem_wb).wait()

    return gather_kernel(src, idx)


def _moe_kernel(blk_e_ref, blk_valid_ref, xs_ref, wgu_ref, bgu_ref, wd_ref, bd_ref, y_ref, wgu_bf, wd_bf):
    i = pl.program_id(0)
    e = blk_e_ref[i]
    n_valid = blk_valid_ref[i]
    used = n_valid > 0
    new_expert = (i == 0) | (blk_e_ref[jnp.maximum(i - 1, 0)] != e)

    @pl.when(used & new_expert)
    def _():
        wgu_bf[...] = wgu_ref[0].astype(BF16)
        wd_bf[...] = wd_ref[0].astype(BF16)

    @pl.when(used)
    def _():
        valid = lax.broadcasted_iota(I32, xs_ref.shape, 0) < n_valid
        xb = _unpack_halves(jnp.where(valid, xs_ref[...], jnp.uint32(0))).astype(BF16)
        gu = jnp.dot(xb, wgu_bf[...], preferred_element_type=F32) + bgu_ref[0]
        g = jnp.minimum(gu[:, :D_FF], SWIGLU_LIMIT)
        up = jnp.clip(gu[:, D_FF:], -SWIGLU_LIMIT, SWIGLU_LIMIT)
        act = (up + 1.0) * (g * jax.nn.sigmoid(SWIGLU_ALPHA * g))
        y_ref[...] = _pack_halves(jnp.dot(act.astype(BF16), wd_bf[...], preferred_element_type=F32) + bd_ref[0])

    @pl.when(jnp.logical_not(used))
    def _():
        y_ref[...] = jnp.zeros_like(y_ref)


def _moe(xs, blk_e, blk_valid, layer, w_gu, b_gu, w_d, b_d):
    n_rows = xs.shape[0]
    bm = MOE_BM
    assert n_rows % bm == 0
    e0 = layer * N_EXPERTS
    n_we = w_gu.shape[0] * w_gu.shape[1]
    grid_spec = pltpu.PrefetchScalarGridSpec(
        num_scalar_prefetch=2,
        grid=(n_rows // bm,),
        in_specs=[pl.BlockSpec((bm, D_MODEL // 2), lambda i, be, nb: (i, 0)),
                  pl.BlockSpec((1, D_MODEL, 2 * D_FF), lambda i, be, nb: (e0 + be[i], 0, 0)),
                  pl.BlockSpec((1, 1, 2 * D_FF), lambda i, be, nb: (e0 + be[i], 0, 0)),
                  pl.BlockSpec((1, D_FF, D_MODEL), lambda i, be, nb: (e0 + be[i], 0, 0)),
                  pl.BlockSpec((1, 1, D_MODEL), lambda i, be, nb: (e0 + be[i], 0, 0))],
        out_specs=pl.BlockSpec((bm, D_MODEL // 2), lambda i, be, nb: (i, 0)),
        scratch_shapes=[pltpu.VMEM((D_MODEL, 2 * D_FF), BF16), pltpu.VMEM((D_FF, D_MODEL), BF16)],
    )
    return pl.pallas_call(
        _moe_kernel,
        grid_spec=grid_spec,
        out_shape=jax.ShapeDtypeStruct((n_rows, D_MODEL // 2), U32),
        compiler_params=pltpu.CompilerParams(dimension_semantics=("arbitrary",), vmem_limit_bytes=VMEM_LIMIT),
        name="moe_experts",
    )(blk_e, blk_valid, xs, w_gu.reshape(n_we, D_MODEL, 2 * D_FF), b_gu.reshape(n_we, 1, 2 * D_FF),
      w_d.reshape(n_we, D_FF, D_MODEL), b_d.reshape(n_we, 1, D_MODEL))


def _ple_kernel(x1_ref, y0_ref, y1_ref, y2_ref, y3_ref, gates_ref, p_ref, g_ref, wg_ref, wp_ref, out_ref):
    x2 = x1_ref[...]
    gates = gates_ref[...]
    for kk, y_ref in enumerate((y0_ref, y1_ref, y2_ref, y3_ref)):
        x2 = x2 + _unpack_halves(y_ref[...]) * gates[:, kk:kk + 1]
    hp = _rms(x2, g_ref[...]).astype(BF16)
    gate = jax.nn.sigmoid(jnp.dot(hp, wg_ref[...], preferred_element_type=F32))
    pp = jnp.dot(p_ref[...].astype(BF16), wp_ref[...], preferred_element_type=F32)
    out_ref[...] = x2 + gate * pp


def _ple(x1, y_tok, y0, gates_t, tok0, p_all, p0, g_ple, w_gate, w_proj, rows):
    n = x1.shape[0]
    assert n % rows == 0 and tok0 % rows == 0 and y0 % rows == 0 and p0 % rows == 0
    const = lambda shape: pl.BlockSpec(shape, lambda i: (0,) * len(shape))
    y_blk = lambda kk: pl.BlockSpec((rows, D_MODEL // 2), lambda i: ((y0 + kk * n) // rows + i, 0))
    return pl.pallas_call(
        _ple_kernel,
        grid=(n // rows,),
        in_specs=[pl.BlockSpec((rows, D_MODEL), lambda i: (i, 0)), y_blk(0), y_blk(1), y_blk(2), y_blk(3),
                  pl.BlockSpec((rows, TOP_K), lambda i: (tok0 // rows + i, 0)),
                  pl.BlockSpec((rows, PLE_DIM), lambda i: (p0 // rows + i, 0)),
                  const((1, D_MODEL)), const((D_MODEL, D_MODEL)), const((PLE_DIM, D_MODEL))],
        out_specs=pl.BlockSpec((rows, D_MODEL), lambda i: (i, 0)),
        out_shape=jax.ShapeDtypeStruct((n, D_MODEL), F32),
        compiler_params=pltpu.CompilerParams(dimension_semantics=("arbitrary",), vmem_limit_bytes=VMEM_LIMIT),
        name="combine_ple",
    )(x1, y_tok, y_tok, y_tok, y_tok, gates_t, p_all, g_ple, w_gate, w_proj)


def _rope_tables(pos):
    half = ROT_DIM // 2
    d = np.arange(LANES) % HEAD_DIM
    inv = ROPE_THETA ** (-jnp.arange(half, dtype=F32) / half)
    inv_lane = jnp.where(d < ROT_DIM, inv[d % half], 0.0)
    ang = pos.astype(F32)[:, None] * inv_lane[None, :]
    cos, sin = jnp.cos(ang), jnp.sin(ang)
    sin_a = jnp.where((d >= half) & (d < ROT_DIM), sin, 0.0)
    sin_b = jnp.where(d < half, -sin, 0.0)
    return cos, sin_a, sin_b


def _layer(layer, xp, xs, ck, cv, st, p_prompt_all, p_sample_all, past_len, refine_tail, tabs_p, tabs_s,
           norm_attn, w_in, q_norm, k_norm, attn_sinks, w_pool, pool_scale, w_out,
           norm_ffn, w_router, b_router, w_gate_up_all, b_gate_up_all, w_down_all, b_down_all,
           norm_ple, w_ple_gate, w_ple_proj):
    t, ns = xp.shape[0], xs.shape[0]
    w_q = w_in[:, :ATTN_WIDTH].reshape(D_MODEL, 2, Q_TILES, HEAD_DIM).transpose(0, 2, 1, 3).reshape(D_MODEL, ATTN_WIDTH)
    w_in_p = jnp.concatenate([w_q, w_in[:, ATTN_WIDTH:]], axis=1)
    w_oa = w_out[:ATTN_WIDTH].reshape(2, Q_TILES, HEAD_DIM, D_MODEL).transpose(1, 0, 2, 3).reshape(ATTN_WIDTH, D_MODEL)
    w_out_p = jnp.concatenate([w_oa, w_out[ATTN_WIDTH:]], axis=0)
    g_attn = norm_attn.reshape(1, D_MODEL)
    qn = jnp.tile(q_norm, 2).reshape(1, LANES)
    kn = jnp.tile(k_norm, 2).reshape(1, LANES)
    lane = np.arange(LANES)
    hmean = jnp.asarray((lane[:, None] // HEAD_DIM == lane[None, :] // HEAD_DIM) / HEAD_DIM, F32)
    pscale = pool_scale.reshape(1, POOL_WIDTH)
    sink8 = jnp.broadcast_to(attn_sinks.reshape(2, Q_TILES).T.reshape(8, 1), (8, LANES))

    mix_args = (g_attn, w_in_p.astype(BF16), qn, kn, hmean.astype(BF16), w_pool.astype(BF16), pscale,
                w_out_p.astype(BF16))
    x1p, nk_p, nv_p, nu_p = _mixer_prompt(xp, 0, attn_sinks, tabs_p, *mix_args)
    if refine_tail:
        hi_args = (g_attn, w_in_p, qn, kn, hmean, w_pool, pscale, w_out_p)
        x1p, nk_p, nv_p, nu_p = _mixer_prompt(xp, t - 2 * MIX_ROWS, attn_sinks, tabs_p, *hi_args, x1_into=x1p)
    st_t = jnp.transpose(st, (1, 0, 2))
    x1s, nk_s, nv_s, nst_t = _mixer_sample(xs, ck.reshape(ns, -1, KV_WIDTH), cv.reshape(ns, -1, KV_WIDTH), st_t,
                                           past_len, tabs_s, sink8, *mix_args)

    g_ffn = norm_ffn.reshape(1, D_MODEL)
    wr_t = w_router.T
    br = b_router.reshape(N_EXPERTS, 1)
    n_tok = t + ns
    bm = MOE_BM
    n_blocks = -(-(n_tok * TOP_K + N_EXPERTS * (bm - 1)) // bm)
    hf_p, idx_p, gate_p, pos_p, cnt_p, _, _, _ = _route(x1p, g_ffn, wr_t, br, jnp.zeros((N_EXPERTS, 1), F32),
                                                        ROUTE_ROWS, bm, n_blocks)
    hf_s, idx_s, gate_s, pos_s, _, pad_start, blk_e, blk_valid = _route(x1s, g_ffn, wr_t, br, cnt_p, ns, bm, n_blocks)
    idx = jnp.concatenate([idx_p, idx_s], axis=1)
    pos = jnp.concatenate([pos_p, pos_s], axis=1)
    blk_e = blk_e[0, :n_blocks]
    blk_valid = blk_valid[0, :n_blocks]
    start_of = jnp.sum(jnp.where(idx[None] == jnp.arange(N_EXPERTS, dtype=I32)[:, None, None],
                                 pad_start.reshape(N_EXPERTS, 1, 1), 0), axis=0)
    dest = start_of + pos

    xs_rows = _scatter_rows(hf_p, hf_s, dest, n_blocks * bm)
    y = _moe(xs_rows, blk_e, blk_valid, layer, w_gate_up_all, b_gate_up_all, w_down_all, b_down_all)

    back = jnp.concatenate([dest[:, :t].reshape(-1), dest[:, t:].reshape(-1)])
    unit = SC_WORKERS * SC_CHUNK * 2
    back = jnp.concatenate([back, jnp.zeros((-back.shape[0] % unit,), I32)])
    y_tok = _gather_rows(y, back)

    g_ple = norm_ple.reshape(1, D_MODEL)
    wg_bf = w_ple_gate.astype(BF16)
    wp_bf = w_ple_proj.astype(BF16)
    yp = _ple(x1p, y_tok, 0, gate_p.T, 0, p_prompt_all, layer * t, g_ple, wg_bf, wp_bf, PLE_ROWS)
    ys = _ple(x1s, y_tok, TOP_K * t, gate_s.T, 0, p_sample_all, layer * ns, g_ple, wg_bf, wp_bf, ns)
    new_pool_s = jnp.transpose(nst_t, (1, 0, 2))
    return yp, ys, nk_p, nv_p, nu_p[POOL_HALO - POOL_PREFIX:], nk_s, nv_s, new_pool_s


def kernel(x_prompt, x_sample, cache_k, cache_v, state_pool, p_prompt, p_sample, norm_attn, w_in, q_norm, k_norm,
           attn_sinks, w_pool, pool_scale, w_out, norm_ffn, w_router, b_router, w_gate_up, b_gate_up, w_down, b_down,
           norm_ple, w_ple_gate, w_ple_proj):
    depth = norm_attn.shape[0]
    batch, seq, d = x_prompt.shape
    ns, dec_seq, _ = x_sample.shape
    wb = cache_k.shape[2]
    assert batch == 1 and dec_seq == 1 and d == D_MODEL and wb == WINDOW
    assert cache_k.shape[3:] == (N_KV_HEADS, HEAD_DIM) and state_pool.shape[2:] == (POOL_PREFIX, POOL_WIDTH)
    past_len = PAST_LEN
    yp = x_prompt.reshape(seq, d)
    ys = x_sample.reshape(ns, d)
    p_prompt_all = p_prompt.reshape(depth * seq, PLE_DIM)
    p_sample_all = p_sample.reshape(depth * ns, PLE_DIM)
    tabs_p = _rope_tables(jnp.arange(seq))
    tabs_s = _rope_tables(jnp.full((1,), past_len))
    outs = [[] for _ in range(6)]
    for i in range(depth):
        res = _layer(i, yp, ys, cache_k[i], cache_v[i], state_pool[i], p_prompt_all, p_sample_all, past_len,
                     i < depth - 1, tabs_p, tabs_s,
                     norm_attn[i], w_in[i], q_norm[i], k_norm[i], attn_sinks[i], w_pool[i], pool_scale[i], w_out[i],
                     norm_ffn[i], w_router[i], b_router[i], w_gate_up, b_gate_up, w_down, b_down,
                     norm_ple[i], w_ple_gate[i], w_ple_proj[i])
        yp, ys = res[0], res[1]
        kv_shape = (1, WINDOW, N_KV_HEADS, HEAD_DIM)
        outs[0].append(res[2].reshape(kv_shape))
        outs[1].append(res[3].reshape(kv_shape))
        outs[2].append(res[4].reshape(1, POOL_PREFIX, POOL_WIDTH))
        outs[3].append(res[5].reshape(ns, wb, N_KV_HEADS, HEAD_DIM))
        outs[4].append(res[6].reshape(ns, wb, N_KV_HEADS, HEAD_DIM))
        outs[5].append(res[7])
    return (yp.reshape(batch, seq, d), ys.reshape(ns, dec_seq, d)) + tuple(jnp.stack(o) for o in outs)
```

```python
import functools

import jax
import jax.numpy as jnp
import numpy as np
from jax import lax
from jax.experimental import pallas as pl
from jax.experimental.pallas import tpu as pltpu
from jax.experimental.pallas import tpu_sc as plsc

F32 = jnp.float32
BF16 = jnp.bfloat16
U32 = jnp.uint32
I32 = jnp.int32

D_MODEL = 1024
HEAD_DIM = 64
N_HEADS = 8
N_KV_HEADS = 2
GROUP = N_HEADS // N_KV_HEADS
ATTN_WIDTH = N_HEADS * HEAD_DIM
KV_WIDTH = N_KV_HEADS * HEAD_DIM
POOL_WIDTH = 512
POOL_WINDOWS = (2, 4, 8, 16)
POOL_GC = POOL_WIDTH // len(POOL_WINDOWS)
POOL_PREFIX = max(POOL_WINDOWS) - 1
POOL_HALO = POOL_PREFIX + 1
IN_WIDTH = ATTN_WIDTH + 2 * KV_WIDTH + POOL_WIDTH
WINDOW = 128
ROPE_THETA = 500000.0
ROT_DIM = HEAD_DIM // 4
N_EXPERTS = 32
TOP_K = 4
D_FF = 1024
SWIGLU_ALPHA = 1.702
SWIGLU_LIMIT = 7.0
PLE_DIM = 256
PAST_LEN = 16384
EPS = 1e-5
NEG_INF = -1e30

LANES = 128
Q_TILES = ATTN_WIDTH // LANES

MIX_ROWS = 256
SAMPLE_CHUNK = 16
ROUTE_ROWS = 512
MOE_BM = 512
SC_CORES = 2
SC_SUBCORES = 16
SC_WORKERS = SC_CORES * SC_SUBCORES
SC_CHUNK = 128
PLE_ROWS = 512
VMEM_LIMIT = 56 * 1024 * 1024


def _rms(x, g):
    return x * lax.rsqrt(jnp.mean(x * x, axis=-1, keepdims=True) + EPS) * g


def _pack_halves(x):
    w = x.shape[1] // 2
    lo = lax.bitcast_convert_type(x[:, :w].astype(BF16).astype(F32), U32) >> 16
    hi = lax.bitcast_convert_type(x[:, w:].astype(BF16).astype(F32), U32) & jnp.uint32(0xFFFF0000)
    return lo | hi


def _unpack_halves(packed):
    lo = lax.bitcast_convert_type(packed << 16, F32)
    hi = lax.bitcast_convert_type(packed & jnp.uint32(0xFFFF0000), F32)
    return jnp.concatenate([lo, hi], axis=1)


def _mm(a, b, nt=False):
    dims = (((1,), (1 if nt else 0,)), ((), ()))
    if b.dtype == F32:
        return lax.dot_general(a.astype(F32), b, dims, preferred_element_type=F32, precision=lax.Precision.HIGHEST)
    return lax.dot_general(a.astype(BF16), b, dims, preferred_element_type=F32)


def _head_norm_rope(t, hmean, gain, cos, sin_a, sin_b):
    t = t * lax.rsqrt(_mm(t * t, hmean) + EPS) * gain
    return t * cos + pltpu.roll(t, ROT_DIM // 2, axis=1) * sin_a + pltpu.roll(t, LANES - ROT_DIM // 2, axis=1) * sin_b


def _softmax_pv(s, sink, v, ones):
    m = jnp.maximum(jnp.max(s, axis=-1, keepdims=True), sink)
    e = jnp.exp(s - m).astype(v.dtype)
    den = _mm(e, ones) + jnp.exp(sink - m)
    return _mm(e, v) / den


def _mixer_prompt_kernel(sinks_ref, x_ref, cos_ref, sa_ref, sb_ref, g_ref, win_ref, qn_ref, kn_ref, hm_ref,
                         wpool_ref, pscale_ref, wout_ref, *rest, row_offset, aliased):
    x1_ref, klast_ref, vlast_ref, ulast_ref, kprev, vprev, uext, mix = rest[1:] if aliased else rest
    i = pl.program_id(0)
    rows = x_ref.shape[0]
    n_sub = rows // WINDOW
    cdt = win_ref.dtype
    row0 = row_offset + i * rows

    @pl.when(i == 0)
    def _():
        kprev[...] = jnp.zeros_like(kprev)
        vprev[...] = jnp.zeros_like(vprev)
        uext[0:POOL_HALO, :] = jnp.zeros((POOL_HALO, POOL_WIDTH), F32)

    x = x_ref[...]
    proj = _mm(_rms(x, g_ref[...]), win_ref[...])
    cos, sin_a, sin_b = cos_ref[...], sa_ref[...], sb_ref[...]
    hmean = hm_ref[...]

    k = _head_norm_rope(proj[:, ATTN_WIDTH:ATTN_WIDTH + KV_WIDTH], hmean, kn_ref[...], cos, sin_a, sin_b)
    v = proj[:, ATTN_WIDTH + KV_WIDTH:ATTN_WIDTH + 2 * KV_WIDTH]
    u = proj[:, ATTN_WIDTH + 2 * KV_WIDTH:]
    klast_ref[...] = k[rows - WINDOW:, :]
    vlast_ref[...] = v[rows - WINDOW:, :]
    ulast_ref[...] = u[rows - POOL_HALO:, :]
    k_c = k.astype(cdt)
    v_c = v.astype(cdt)

    lane = lax.broadcasted_iota(I32, (WINDOW, LANES), 1)
    left = lane < HEAD_DIM
    qi = lax.broadcasted_iota(I32, (WINDOW, 2 * WINDOW), 0)
    kj = lax.broadcasted_iota(I32, (WINDOW, 2 * WINDOW), 1)
    band = (kj - qi >= 1) & (kj - qi <= WINDOW)
    ones = jnp.ones((2 * WINDOW, LANES), cdt)
    scale = HEAD_DIM ** -0.5

    for c in range(n_sub):
        r0 = c * WINDOW
        if c == 0:
            k_cat = jnp.concatenate([kprev[...], k_c[0:WINDOW]], axis=0)
            v_cat = jnp.concatenate([vprev[...], v_c[0:WINDOW]], axis=0)
            mask = band & (kj + (row0 - WINDOW) >= 0)
        else:
            k_cat = k_c[r0 - WINDOW:r0 + WINDOW]
            v_cat = v_c[r0 - WINDOW:r0 + WINDOW]
            mask = band
        mask2 = jnp.concatenate([mask, mask], axis=0)
        for j in range(Q_TILES):
            qt = _head_norm_rope(proj[r0:r0 + WINDOW, j * LANES:(j + 1) * LANES], hmean, qn_ref[...],
                                 cos[r0:r0 + WINDOW], sin_a[r0:r0 + WINDOW], sin_b[r0:r0 + WINDOW]) * scale
            q2 = jnp.concatenate([jnp.where(left, qt, 0.0), jnp.where(left, 0.0, qt)], axis=0)
            s = jnp.where(mask2, _mm(q2, k_cat, nt=True), NEG_INF)
            sink = jnp.concatenate([jnp.full((WINDOW, 1), sinks_ref[j], F32),
                                    jnp.full((WINDOW, 1), sinks_ref[j + Q_TILES], F32)], axis=0)
            o = _softmax_pv(s, sink, v_cat, ones)
            mix[r0:r0 + WINDOW, j * LANES:(j + 1) * LANES] = jnp.where(left, o[0:WINDOW], o[WINDOW:]).astype(cdt)

    kprev[...] = k_c[rows - WINDOW:]
    vprev[...] = v_c[rows - WINDOW:]

    uext[POOL_HALO:POOL_HALO + rows, :] = u
    pos1 = (lax.broadcasted_iota(I32, (rows, 1), 0) + row0 + 1).astype(F32)
    for gi, w in enumerate(POOL_WINDOWS):
        cols = slice(gi * POOL_GC, (gi + 1) * POOL_GC)
        wsum = u[:, cols]
        for sft in range(1, w):
            wsum = wsum + uext[POOL_HALO - sft:POOL_HALO - sft + rows, cols]
        d = wsum / jnp.minimum(pos1, float(w)) - u[:, cols]
        y = _mm(d, wpool_ref[gi]) * pscale_ref[:, cols]
        mix[:, ATTN_WIDTH + gi * POOL_GC:ATTN_WIDTH + (gi + 1) * POOL_GC] = y.astype(cdt)
    uext[0:POOL_HALO, :] = u[rows - POOL_HALO:, :]

    x1_ref[...] = x + _mm(mix[...], wout_ref[...])


def _mixer_prompt(x_full, row_offset, sinks, tabs, g_attn, w_in, qn, kn, hmean, w_pool, pscale, w_out, x1_into=None):
    t = x_full.shape[0] - row_offset
    rows = MIX_ROWS
    cdt = w_in.dtype
    assert t % rows == 0 and row_offset % rows == 0 and rows % WINDOW == 0 and rows >= POOL_HALO
    blk0, n_steps = row_offset // rows, t // rows
    const = lambda shape: pl.BlockSpec(shape, lambda i, *_: (0,) * len(shape))
    row_blk = lambda width: pl.BlockSpec((rows, width), lambda i, *_: (blk0 + i, 0))
    aliased = x1_into is not None
    if aliased:
        assert x1_into.shape == x_full.shape
        x1_spec = pl.BlockSpec((rows, D_MODEL), lambda i, *_: (blk0 + n_steps - 1, 0))
        x1_shape = x1_into.shape
        extra_specs, extra_args, aliases = [pl.BlockSpec(memory_space=pl.ANY)], [x1_into], {13: 0}
    else:
        x1_spec = pl.BlockSpec((rows, D_MODEL), lambda i, *_: (i, 0))
        x1_shape = (t, D_MODEL)
        extra_specs, extra_args, aliases = [], [], {}
    grid_spec = pltpu.PrefetchScalarGridSpec(
        num_scalar_prefetch=1,
        grid=(n_steps,),
        in_specs=[row_blk(D_MODEL), row_blk(LANES), row_blk(LANES), row_blk(LANES),
                  const((1, D_MODEL)), const((D_MODEL, IN_WIDTH)), const((1, LANES)), const((1, LANES)),
                  const((LANES, LANES)), const((len(POOL_WINDOWS), POOL_GC, POOL_GC)), const((1, POOL_WIDTH)),
                  const((D_MODEL, D_MODEL))] + extra_specs,
        out_specs=[x1_spec, const((WINDOW, KV_WIDTH)), const((WINDOW, KV_WIDTH)),
                   const((POOL_HALO, POOL_WIDTH))],
        scratch_shapes=[pltpu.VMEM((WINDOW, KV_WIDTH), cdt), pltpu.VMEM((WINDOW, KV_WIDTH), cdt),
                        pltpu.VMEM((POOL_HALO + rows, POOL_WIDTH), F32), pltpu.VMEM((rows, D_MODEL), cdt)],
    )
    return pl.pallas_call(
        functools.partial(_mixer_prompt_kernel, row_offset=row_offset, aliased=aliased),
        grid_spec=grid_spec,
        out_shape=[jax.ShapeDtypeStruct(x1_shape, F32), jax.ShapeDtypeStruct((WINDOW, KV_WIDTH), F32),
                   jax.ShapeDtypeStruct((WINDOW, KV_WIDTH), F32), jax.ShapeDtypeStruct((POOL_HALO, POOL_WIDTH), F32)],
        input_output_aliases=aliases,
        compiler_params=pltpu.CompilerParams(dimension_semantics=("arbitrary",), vmem_limit_bytes=VMEM_LIMIT),
        name="mixer_prompt",
    )(sinks, x_full, *tabs, g_attn, w_in, qn, kn, hmean, w_pool, pscale, w_out, *extra_args)


def _mixer_sample_kernel(x_ref, ck_ref, cv_ref, st_ref, cos_ref, sa_ref, sb_ref, sink8_ref, g_ref, win_ref, qn_ref,
                         kn_ref, hm_ref, wpool_ref, pscale_ref, wout_ref,
                         x1_ref, nk_ref, nv_ref, nst_ref, o8, *, pos):
    nb = x_ref.shape[0]
    wb = ck_ref.shape[1]
    x = x_ref[...]
    h = _rms(x, g_ref[...]).astype(BF16)
    proj = jnp.dot(h, win_ref[...], preferred_element_type=F32)
    cos, sin_a, sin_b = cos_ref[...], sa_ref[...], sb_ref[...]
    hmean = hm_ref[...]
    k = _head_norm_rope(proj[:, ATTN_WIDTH:ATTN_WIDTH + KV_WIDTH], hmean, kn_ref[...], cos, sin_a, sin_b)
    v = proj[:, ATTN_WIDTH + KV_WIDTH:ATTN_WIDTH + 2 * KV_WIDTH]
    u = proj[:, ATTN_WIDTH + 2 * KV_WIDTH:]

    nk_ref[:, 0:wb - 1, :] = ck_ref[:, 1:wb, :]
    nv_ref[:, 0:wb - 1, :] = cv_ref[:, 1:wb, :]
    for b in range(nb):
        nk_ref[b, wb - 1:wb, :] = k[b:b + 1, :]
        nv_ref[b, wb - 1:wb, :] = v[b:b + 1, :]

    r8 = lax.broadcasted_iota(I32, (nb * 8, LANES), 0)
    lane8 = lax.broadcasted_iota(I32, (nb * 8, LANES), 1)
    keep = (lane8 < HEAD_DIM) == (r8 % 2 == 0)
    rep = (lax.broadcasted_iota(I32, (nb * 8, nb), 0) // 8 == lax.broadcasted_iota(I32, (nb * 8, nb), 1)).astype(BF16)
    q8 = jnp.zeros((nb * 8, LANES), F32)
    scale = HEAD_DIM ** -0.5
    for j in range(Q_TILES):
        qt = _head_norm_rope(proj[:, j * LANES:(j + 1) * LANES], hmean, qn_ref[...], cos, sin_a, sin_b) * scale
        qrep = jnp.dot(rep, qt.astype(BF16), preferred_element_type=F32)
        q8 = jnp.where(keep & ((r8 % 8) // 2 == j), qrep, q8)
    q8 = q8.astype(BF16)

    sink8 = sink8_ref[:, 0:1]
    ones_bf = jnp.ones((wb, LANES), BF16)
    assert pos >= wb - 1 and wb <= WINDOW
    for b in range(nb):
        kb = nk_ref[b].astype(BF16)
        vb = nv_ref[b].astype(BF16)
        s = lax.dot_general(q8[b * 8:(b + 1) * 8], kb, (((1,), (1,)), ((), ())), preferred_element_type=F32)
        o8[b * 8:(b + 1) * 8, :] = _softmax_pv(s, sink8, vb, ones_bf)
    o8m = jnp.where(keep, o8[...], 0.0).astype(BF16)

    a_tiles = []
    sel_r = lax.broadcasted_iota(I32, (nb, nb * 8), 1)
    sel_b = lax.broadcasted_iota(I32, (nb, nb * 8), 0)
    for j in range(Q_TILES):
        sel = ((sel_r // 8 == sel_b) & ((sel_r % 8) // 2 == j)).astype(BF16)
        a_tiles.append(jnp.dot(sel, o8m, preferred_element_type=F32))

    z_tiles = []
    for gi, w in enumerate(POOL_WINDOWS):
        cols = slice(gi * POOL_GC, (gi + 1) * POOL_GC)
        wsum = u[:, cols]
        for sft in range(1, w):
            wsum = wsum + st_ref[POOL_PREFIX - sft, :, cols]
        d = wsum / float(min(pos + 1, w)) - u[:, cols]
        z_tiles.append(jnp.dot(d.astype(BF16), wpool_ref[gi], preferred_element_type=F32) * pscale_ref[:, cols])
    nst_ref[0:POOL_PREFIX - 1] = st_ref[1:POOL_PREFIX]
    nst_ref[POOL_PREFIX - 1] = u

    mixv = jnp.concatenate(a_tiles + z_tiles, axis=1).astype(BF16)
    x1_ref[...] = x + jnp.dot(mixv, wout_ref[...], preferred_element_type=F32)


def _mixer_sample(x, ck, cv, st, pos, tabs, sink8, g_attn, w_in, qn, kn, hmean, w_pool, pscale, w_out):
    n, wb = ck.shape[0], ck.shape[1]
    nb = SAMPLE_CHUNK
    assert n % nb == 0
    const = lambda shape: pl.BlockSpec(shape, lambda i: (0,) * len(shape))
    cache_blk = pl.BlockSpec((nb, wb, KV_WIDTH), lambda i: (i, 0, 0))
    st_blk = pl.BlockSpec((POOL_PREFIX, nb, POOL_WIDTH), lambda i: (0, i, 0))
    x_blk = pl.BlockSpec((nb, D_MODEL), lambda i: (i, 0))
    return pl.pallas_call(
        functools.partial(_mixer_sample_kernel, pos=pos),
        grid=(n // nb,),
        in_specs=[x_blk, cache_blk, cache_blk, st_blk, const((1, LANES)), const((1, LANES)), const((1, LANES)),
                  const((8, LANES)), const((1, D_MODEL)), const((D_MODEL, IN_WIDTH)), const((1, LANES)),
                  const((1, LANES)), const((LANES, LANES)), const((len(POOL_WINDOWS), POOL_GC, POOL_GC)),
                  const((1, POOL_WIDTH)), const((D_MODEL, D_MODEL))],
        out_specs=[x_blk, cache_blk, cache_blk, st_blk],
        out_shape=[jax.ShapeDtypeStruct((n, D_MODEL), F32), jax.ShapeDtypeStruct(ck.shape, F32),
                   jax.ShapeDtypeStruct(cv.shape, F32), jax.ShapeDtypeStruct(st.shape, F32)],
        scratch_shapes=[pltpu.VMEM((nb * 8, LANES), F32)],
        compiler_params=pltpu.CompilerParams(dimension_semantics=("arbitrary",), vmem_limit_bytes=VMEM_LIMIT),
        name="mixer_sample",
    )(x, ck, cv, st, *tabs, sink8, g_attn, w_in, qn, kn, hmean, w_pool, pscale, w_out)


def _block_plan(cnt, bm, n_lanes):
    e_sub = lax.broadcasted_iota(I32, (N_EXPERTS, LANES), 0)
    e_lane = lax.broadcasted_iota(I32, (N_EXPERTS, LANES), 1)
    padded = jnp.floor((cnt + (bm - 1)) / bm) * bm
    padded_lane = jnp.sum(jnp.where(e_sub == e_lane, padded, 0.0), axis=0, keepdims=True)
    pad_end = jnp.sum(jnp.where(e_lane <= e_sub, padded_lane, 0.0), axis=1, keepdims=True)
    pad_start = pad_end - padded
    blk_start = lax.broadcasted_iota(I32, (N_EXPERTS, n_lanes), 1).astype(F32) * bm
    blk_e = jnp.minimum(jnp.sum((pad_end <= blk_start).astype(F32), axis=0, keepdims=True), N_EXPERTS - 1.0)
    mine = lax.broadcasted_iota(I32, (N_EXPERTS, n_lanes), 0).astype(F32) == blk_e
    last = jnp.sum(jnp.where(mine, pad_start + cnt, 0.0), axis=0, keepdims=True)
    blk_valid = jnp.clip(last - blk_start[0:1], 0.0, float(bm))
    e_blk = lax.broadcasted_iota(I32, (N_EXPERTS, n_lanes), 0).astype(F32)
    later = jnp.min(jnp.where((e_blk > blk_e) & (cnt > 0.0), e_blk, float(N_EXPERTS)), axis=0, keepdims=True)
    blk_next = jnp.where(later < N_EXPERTS, later, -1.0)
    return pad_start, blk_e, blk_valid, blk_next


def _route_kernel(x1_ref, g_ref, wr_ref, br_ref, cnt_in_ref, hf_ref, idx_ref, gate_ref, pos_ref, cnt_ref,
                  pstart_ref, blke_ref, blkv_ref, blkn_ref, counts, *, bm):
    i = pl.program_id(0)
    rows = x1_ref.shape[0]

    @pl.when(i == 0)
    def _():
        counts[...] = cnt_in_ref[...]

    h = _rms(x1_ref[...], g_ref[...])
    hf_ref[...] = _pack_halves(h)

    logits = lax.dot_general(wr_ref[...], h, (((1,), (1,)), ((), ())), preferred_element_type=F32,
                             precision=lax.Precision.HIGHEST) + br_ref[...]
    eid = lax.broadcasted_iota(I32, (N_EXPERTS, rows), 0).astype(F32)
    work = logits
    vals, hots = [], []
    for kk in range(TOP_K):
        m = jnp.max(work, axis=0, keepdims=True)
        first = jnp.min(jnp.where(work == m, eid, float(N_EXPERTS)), axis=0, keepdims=True)
        hot = eid == first
        work = jnp.where(hot, -jnp.inf, work)
        vals.append(m)
        hots.append(hot)
        idx_ref[kk:kk + 1, :] = first.astype(I32)
    es = [jnp.exp(vv - vals[0]) for vv in vals]
    den = es[0] + es[1] + es[2] + es[3]
    for kk in range(TOP_K):
        gate_ref[kk:kk + 1, :] = es[kk] / den

    chosen = hots[0] | hots[1] | hots[2] | hots[3]
    before = (lax.broadcasted_iota(I32, (rows, rows), 0) < lax.broadcasted_iota(I32, (rows, rows), 1)).astype(BF16)
    rank = jnp.dot(chosen.astype(BF16), before, preferred_element_type=F32) + counts[...]
    for kk in range(TOP_K):
        pos_ref[kk:kk + 1, :] = jnp.sum(jnp.where(hots[kk], rank, 0.0), axis=0, keepdims=True).astype(I32)
    counts[...] = counts[...] + jnp.sum(chosen.astype(F32), axis=1, keepdims=True)
    cnt_ref[...] = counts[...]

    @pl.when(i == pl.num_programs(0) - 1)
    def _():
        pad_start, blk_e, blk_valid, blk_next = _block_plan(counts[...], bm, blke_ref.shape[1])
        pstart_ref[...] = pad_start.astype(I32)
        blke_ref[...] = blk_e.astype(I32)
        blkv_ref[...] = blk_valid.astype(I32)
        blkn_ref[...] = blk_next.astype(I32)


def _route(x1, g_ffn, wr_t, br, cnt_in, rows, bm, n_blocks):
    n = x1.shape[0]
    assert n % rows == 0
    nb_lanes = -(-n_blocks // LANES) * LANES
    const = lambda shape: pl.BlockSpec(shape, lambda i: (0,) * len(shape))
    tok_blk = pl.BlockSpec((TOP_K, rows), lambda i: (0, i))
    return pl.pallas_call(
        functools.partial(_route_kernel, bm=bm),
        grid=(n // rows,),
        in_specs=[pl.BlockSpec((rows, D_MODEL), lambda i: (i, 0)), const((1, D_MODEL)), const((N_EXPERTS, D_MODEL)),
                  const((N_EXPERTS, 1)), const((N_EXPERTS, 1))],
        out_specs=[pl.BlockSpec((rows, D_MODEL // 2), lambda i: (i, 0)), tok_blk, tok_blk, tok_blk,
                   const((N_EXPERTS, 1)), const((N_EXPERTS, 1)), const((1, nb_lanes)), const((1, nb_lanes)),
                   const((1, nb_lanes))],
        out_shape=[jax.ShapeDtypeStruct((n, D_MODEL // 2), U32), jax.ShapeDtypeStruct((TOP_K, n), I32),
                   jax.ShapeDtypeStruct((TOP_K, n), F32), jax.ShapeDtypeStruct((TOP_K, n), I32),
                   jax.ShapeDtypeStruct((N_EXPERTS, 1), F32), jax.ShapeDtypeStruct((N_EXPERTS, 1), I32),
                   jax.ShapeDtypeStruct((1, nb_lanes), I32), jax.ShapeDtypeStruct((1, nb_lanes), I32),
                   jax.ShapeDtypeStruct((1, nb_lanes), I32)],
        scratch_shapes=[pltpu.VMEM((N_EXPERTS, 1), F32)],
        compiler_params=pltpu.CompilerParams(dimension_semantics=("arbitrary",), vmem_limit_bytes=VMEM_LIMIT),
        name="route",
    )(x1, g_ffn, wr_t, br, cnt_in)


def _sc_mesh():
    return plsc.VectorSubcoreMesh(core_axis_name="core", subcore_axis_name="subcore")


def _sc_worker_id():
    return lax.axis_index("core") * SC_SUBCORES + lax.axis_index("subcore")


def _scatter_rows(xa, xb, dest, n_rows):
    ch = SC_CHUNK
    na, w = xa.shape
    n = na + xb.shape[0]
    nk = dest.shape[0]
    assert na % ch == 0 and n % ch == 0 and dest.shape[1] == n and xb.shape[1] == w and xa.dtype == xb.dtype
    n_chunks = n // ch

    @pl.kernel(out_type=jax.ShapeDtypeStruct((n_rows, w), xa.dtype), mesh=_sc_mesh(),
               scratch_types=[pltpu.VMEM((ch,), I32), pltpu.VMEM((ch, w), xa.dtype)])
    def scatter_kernel(xa_hbm, xb_hbm, d_hbm, o_hbm, idx_v, buf):
        wid = _sc_worker_id()

        @pl.loop(0, -(-n_chunks // SC_WORKERS))
        def _(j):
            c = j * SC_WORKERS + wid

            @pl.when(c < na // ch)
            def _():
                pltpu.sync_copy(xa_hbm.at[pl.ds(c * ch, ch)], buf)

            @pl.when((c >= na // ch) & (c < n_chunks))
            def _():
                pltpu.sync_copy(xb_hbm.at[pl.ds(c * ch - na, ch)], buf)

            @pl.when(c < n_chunks)
            def _():
                for kk in range(nk):
                    pltpu.sync_copy(d_hbm.at[pl.ds(kk * n + c * ch, ch)], idx_v)
                    pltpu.sync_copy(buf, o_hbm.at[idx_v])

    return scatter_kernel(xa, xb, dest.reshape(-1))


def _gather_rows(src, idx):
    ch = SC_CHUNK
    m = idx.shape[0]
    w = src.shape[1]
    per = m // SC_WORKERS
    assert m % SC_WORKERS == 0 and per % ch == 0

    @pl.kernel(out_type=jax.ShapeDtypeStruct((m, w), src.dtype), mesh=_sc_mesh(),
               scratch_types=[pltpu.VMEM((ch,), I32), pltpu.VMEM((ch, w), src.dtype)])
    def gather_kernel(s_hbm, i_hbm, o_hbm, idx_v, buf):
        base = _sc_worker_id() * per

        @pl.loop(0, per // ch)
        def _(j):
            r0 = base + j * ch
            pltpu.sync_copy(i_hbm.at[pl.ds(r0, ch)], idx_v)
            pltpu.sync_copy(s_hbm.at[idx_v], buf)
            pltpu.sync_copy(buf, o_hbm.at[pl.ds(r0, ch)])

    return gather_kernel(src, idx)


def _moe_kernel(blk_e_ref, blk_valid_ref, blk_next_ref, xs_ref, wgu_hbm, bgu_ref, wd_hbm, bd_ref, y_ref,
                wgu_f32, wd_f32, wgu_bf, wd_bf, sems, *, e0):
    i = pl.program_id(0)
    e = blk_e_ref[i]
    n_valid = blk_valid_ref[i]
    used = n_valid > 0
    new_expert = (i == 0) | (blk_e_ref[jnp.maximum(i - 1, 0)] != e)

    def weight_copies(expert):
        return (pltpu.make_async_copy(wgu_hbm.at[e0 + expert], wgu_f32, sems.at[0]),
                pltpu.make_async_copy(wd_hbm.at[e0 + expert], wd_f32, sems.at[1]))

    @pl.when(used & (i == 0))
    def _():
        for cp in weight_copies(e):
            cp.start()

    @pl.when(used & new_expert)
    def _():
        for cp in weight_copies(e):
            cp.wait()
        wgu_bf[...] = wgu_f32[...].astype(BF16)
        wd_bf[...] = wd_f32[...].astype(BF16)
        nxt = blk_next_ref[i]

        @pl.when(nxt >= 0)
        def _():
            for cp in weight_copies(nxt):
                cp.start()

    @pl.when(used)
    def _():
        valid = lax.broadcasted_iota(I32, xs_ref.shape, 0) < n_valid
        xb = _unpack_halves(jnp.where(valid, xs_ref[...], jnp.uint32(0))).astype(BF16)
        gu = jnp.dot(xb, wgu_bf[...], preferred_element_type=F32) + bgu_ref[0]
        g = jnp.minimum(gu[:, :D_FF], SWIGLU_LIMIT)
        up = jnp.clip(gu[:, D_FF:], -SWIGLU_LIMIT, SWIGLU_LIMIT)
        act = (up + 1.0) * (g * jax.nn.sigmoid(SWIGLU_ALPHA * g))
        y_ref[...] = _pack_halves(jnp.dot(act.astype(BF16), wd_bf[...], preferred_element_type=F32) + bd_ref[0])

    @pl.when(jnp.logical_not(used))
    def _():
        y_ref[...] = jnp.zeros_like(y_ref)


def _moe(xs, blk_e, blk_valid, blk_next, layer, w_gu, b_gu, w_d, b_d):
    n_rows = xs.shape[0]
    bm = MOE_BM
    assert n_rows % bm == 0
    e0 = layer * N_EXPERTS
    n_we = w_gu.shape[0] * w_gu.shape[1]
    any_spec = pl.BlockSpec(memory_space=pl.ANY)
    grid_spec = pltpu.PrefetchScalarGridSpec(
        num_scalar_prefetch=3,
        grid=(n_rows // bm,),
        in_specs=[pl.BlockSpec((bm, D_MODEL // 2), lambda i, be, bv, bn: (i, 0)),
                  any_spec,
                  pl.BlockSpec((1, 1, 2 * D_FF), lambda i, be, bv, bn: (e0 + be[i], 0, 0)),
                  any_spec,
                  pl.BlockSpec((1, 1, D_MODEL), lambda i, be, bv, bn: (e0 + be[i], 0, 0))],
        out_specs=pl.BlockSpec((bm, D_MODEL // 2), lambda i, be, bv, bn: (i, 0)),
        scratch_shapes=[pltpu.VMEM((D_MODEL, 2 * D_FF), F32), pltpu.VMEM((D_FF, D_MODEL), F32),
                        pltpu.VMEM((D_MODEL, 2 * D_FF), BF16), pltpu.VMEM((D_FF, D_MODEL), BF16),
                        pltpu.SemaphoreType.DMA((2,))],
    )
    return pl.pallas_call(
        functools.partial(_moe_kernel, e0=e0),
        grid_spec=grid_spec,
        out_shape=jax.ShapeDtypeStruct((n_rows, D_MODEL // 2), U32),
        compiler_params=pltpu.CompilerParams(dimension_semantics=("arbitrary",), vmem_limit_bytes=VMEM_LIMIT),
        name="moe_experts",
    )(blk_e, blk_valid, blk_next, xs, w_gu.reshape(n_we, D_MODEL, 2 * D_FF), b_gu.reshape(n_we, 1, 2 * D_FF),
      w_d.reshape(n_we, D_FF, D_MODEL), b_d.reshape(n_we, 1, D_MODEL))


def _ple_kernel(x1_ref, y0_ref, y1_ref, y2_ref, y3_ref, gates_ref, p_ref, g_ref, wg_ref, wp_ref, out_ref):
    x2 = x1_ref[...]
    gates = gates_ref[...]
    for kk, y_ref in enumerate((y0_ref, y1_ref, y2_ref, y3_ref)):
        x2 = x2 + _unpack_halves(y_ref[...]) * gates[:, kk:kk + 1]
    hp = _rms(x2, g_ref[...]).astype(BF16)
    gate = jax.nn.sigmoid(jnp.dot(hp, wg_ref[...], preferred_element_type=F32))
    pp = jnp.dot(p_ref[...].astype(BF16), wp_ref[...], preferred_element_type=F32)
    out_ref[...] = x2 + gate * pp


def _ple(x1, y_tok, y0, gates_t, tok0, p_all, p0, g_ple, w_gate, w_proj, rows):
    n = x1.shape[0]
    assert n % rows == 0 and tok0 % rows == 0 and y0 % rows == 0 and p0 % rows == 0
    const = lambda shape: pl.BlockSpec(shape, lambda i: (0,) * len(shape))
    y_blk = lambda kk: pl.BlockSpec((rows, D_MODEL // 2), lambda i: ((y0 + kk * n) // rows + i, 0))
    return pl.pallas_call(
        _ple_kernel,
        grid=(n // rows,),
        in_specs=[pl.BlockSpec((rows, D_MODEL), lambda i: (i, 0)), y_blk(0), y_blk(1), y_blk(2), y_blk(3),
                  pl.BlockSpec((rows, TOP_K), lambda i: (tok0 // rows + i, 0)),
                  pl.BlockSpec((rows, PLE_DIM), lambda i: (p0 // rows + i, 0)),
                  const((1, D_MODEL)), const((D_MODEL, D_MODEL)), const((PLE_DIM, D_MODEL))],
        out_specs=pl.BlockSpec((rows, D_MODEL), lambda i: (i, 0)),
        out_shape=jax.ShapeDtypeStruct((n, D_MODEL), F32),
        compiler_params=pltpu.CompilerParams(dimension_semantics=("arbitrary",), vmem_limit_bytes=VMEM_LIMIT),
        name="combine_ple",
    )(x1, y_tok, y_tok, y_tok, y_tok, gates_t, p_all, g_ple, w_gate, w_proj)


def _rope_tables(pos):
    half = ROT_DIM // 2
    d = np.arange(LANES) % HEAD_DIM
    inv = ROPE_THETA ** (-jnp.arange(half, dtype=F32) / half)
    inv_lane = jnp.where(d < ROT_DIM, inv[d % half], 0.0)
    ang = pos.astype(F32)[:, None] * inv_lane[None, :]
    cos, sin = jnp.cos(ang), jnp.sin(ang)
    sin_a = jnp.where((d >= half) & (d < ROT_DIM), sin, 0.0)
    sin_b = jnp.where(d < half, -sin, 0.0)
    return cos, sin_a, sin_b


def _layer(layer, xp, xs, ck, cv, st, p_prompt_all, p_sample_all, past_len, refine_tail, tabs_p, tabs_s,
           norm_attn, w_in, q_norm, k_norm, attn_sinks, w_pool, pool_scale, w_out,
           norm_ffn, w_router, b_router, w_gate_up_all, b_gate_up_all, w_down_all, b_down_all,
           norm_ple, w_ple_gate, w_ple_proj):
    t, ns = xp.shape[0], xs.shape[0]
    w_q = w_in[:, :ATTN_WIDTH].reshape(D_MODEL, 2, Q_TILES, HEAD_DIM).transpose(0, 2, 1, 3).reshape(D_MODEL, ATTN_WIDTH)
    w_in_p = jnp.concatenate([w_q, w_in[:, ATTN_WIDTH:]], axis=1)
    w_oa = w_out[:ATTN_WIDTH].reshape(2, Q_TILES, HEAD_DIM, D_MODEL).transpose(1, 0, 2, 3).reshape(ATTN_WIDTH, D_MODEL)
    w_out_p = jnp.concatenate([w_oa, w_out[ATTN_WIDTH:]], axis=0)
    g_attn = norm_attn.reshape(1, D_MODEL)
    qn = jnp.tile(q_norm, 2).reshape(1, LANES)
    kn = jnp.tile(k_norm, 2).reshape(1, LANES)
    lane = np.arange(LANES)
    hmean = jnp.asarray((lane[:, None] // HEAD_DIM == lane[None, :] // HEAD_DIM) / HEAD_DIM, F32)
    pscale = pool_scale.reshape(1, POOL_WIDTH)
    sink8 = jnp.broadcast_to(attn_sinks.reshape(2, Q_TILES).T.reshape(8, 1), (8, LANES))

    mix_args = (g_attn, w_in_p.astype(BF16), qn, kn, hmean.astype(BF16), w_pool.astype(BF16), pscale,
                w_out_p.astype(BF16))
    x1p, nk_p, nv_p, nu_p = _mixer_prompt(xp, 0, attn_sinks, tabs_p, *mix_args)
    if refine_tail:
        hi_args = (g_attn, w_in_p, qn, kn, hmean, w_pool, pscale, w_out_p)
        x1p, nk_p, nv_p, nu_p = _mixer_prompt(xp, t - 2 * MIX_ROWS, attn_sinks, tabs_p, *hi_args, x1_into=x1p)
    st_t = jnp.transpose(st, (1, 0, 2))
    x1s, nk_s, nv_s, nst_t = _mixer_sample(xs, ck.reshape(ns, -1, KV_WIDTH), cv.reshape(ns, -1, KV_WIDTH), st_t,
                                           past_len, tabs_s, sink8, *mix_args)

    g_ffn = norm_ffn.reshape(1, D_MODEL)
    wr_t = w_router.T
    br = b_router.reshape(N_EXPERTS, 1)
    n_tok = t + ns
    bm = MOE_BM
    n_blocks = -(-(n_tok * TOP_K + N_EXPERTS * (bm - 1)) // bm)
    hf_p, idx_p, gate_p, pos_p, cnt_p = _route(x1p, g_ffn, wr_t, br, jnp.zeros((N_EXPERTS, 1), F32),
                                               ROUTE_ROWS, bm, n_blocks)[:5]
    hf_s, idx_s, gate_s, pos_s, _, pad_start, blk_e, blk_valid, blk_next = _route(x1s, g_ffn, wr_t, br, cnt_p, ns,
                                                                                  bm, n_blocks)
    idx = jnp.concatenate([idx_p, idx_s], axis=1)
    pos = jnp.concatenate([pos_p, pos_s], axis=1)
    blk_e, blk_valid, blk_next = (a[0, :n_blocks] for a in (blk_e, blk_valid, blk_next))
    start_of = jnp.sum(jnp.where(idx[None] == jnp.arange(N_EXPERTS, dtype=I32)[:, None, None],
                                 pad_start.reshape(N_EXPERTS, 1, 1), 0), axis=0)
    dest = start_of + pos

    xs_rows = _scatter_rows(hf_p, hf_s, dest, n_blocks * bm)
    y = _moe(xs_rows, blk_e, blk_valid, blk_next, layer, w_gate_up_all, b_gate_up_all, w_down_all, b_down_all)

    back = jnp.concatenate([dest[:, :t].reshape(-1), dest[:, t:].reshape(-1)])
    unit = SC_WORKERS * SC_CHUNK
    back = jnp.concatenate([back, jnp.zeros((-back.shape[0] % unit,), I32)])
    y_tok = _gather_rows(y, back)

    g_ple = norm_ple.reshape(1, D_MODEL)
    wg_bf = w_ple_gate.astype(BF16)
    wp_bf = w_ple_proj.astype(BF16)
    yp = _ple(x1p, y_tok, 0, gate_p.T, 0, p_prompt_all, layer * t, g_ple, wg_bf, wp_bf, PLE_ROWS)
    ys = _ple(x1s, y_tok, TOP_K * t, gate_s.T, 0, p_sample_all, layer * ns, g_ple, wg_bf, wp_bf, ns)
    new_pool_s = jnp.transpose(nst_t, (1, 0, 2))
    return yp, ys, nk_p, nv_p, nu_p[POOL_HALO - POOL_PREFIX:], nk_s, nv_s, new_pool_s


def kernel(x_prompt, x_sample, cache_k, cache_v, state_pool, p_prompt, p_sample, norm_attn, w_in, q_norm, k_norm,
           attn_sinks, w_pool, pool_scale, w_out, norm_ffn, w_router, b_router, w_gate_up, b_gate_up, w_down, b_down,
           norm_ple, w_ple_gate, w_ple_proj):
    depth = norm_attn.shape[0]
    batch, seq, d = x_prompt.shape
    ns, dec_seq, _ = x_sample.shape
    wb = cache_k.shape[2]
    assert batch == 1 and dec_seq == 1 and d == D_MODEL and wb == WINDOW
    assert cache_k.shape[3:] == (N_KV_HEADS, HEAD_DIM) and state_pool.shape[2:] == (POOL_PREFIX, POOL_WIDTH)
    past_len = PAST_LEN
    yp = x_prompt.reshape(seq, d)
    ys = x_sample.reshape(ns, d)
    p_prompt_all = p_prompt.reshape(depth * seq, PLE_DIM)
    p_sample_all = p_sample.reshape(depth * ns, PLE_DIM)
    tabs_p = _rope_tables(jnp.arange(seq))
    tabs_s = _rope_tables(jnp.full((1,), past_len))
    outs = [[] for _ in range(6)]
    for i in range(depth):
        res = _layer(i, yp, ys, cache_k[i], cache_v[i], state_pool[i], p_prompt_all, p_sample_all, past_len,
                     i < depth - 1, tabs_p, tabs_s,
                     norm_attn[i], w_in[i], q_norm[i], k_norm[i], attn_sinks[i], w_pool[i], pool_scale[i], w_out[i],
                     norm_ffn[i], w_router[i], b_router[i], w_gate_up, b_gate_up, w_down, b_down,
                     norm_ple[i], w_ple_gate[i], w_ple_proj[i])
        yp, ys = res[0], res[1]
        kv_shape = (1, WINDOW, N_KV_HEADS, HEAD_DIM)
        outs[0].append(res[2].reshape(kv_shape))
        outs[1].append(res[3].reshape(kv_shape))
        outs[2].append(res[4].reshape(1, POOL_PREFIX, POOL_WIDTH))
        outs[3].append(res[5].reshape(ns, wb, N_KV_HEADS, HEAD_DIM))
        outs[4].append(res[6].reshape(ns, wb, N_KV_HEADS, HEAD_DIM))
        outs[5].append(res[7])
    return (yp.reshape(batch, seq, d), ys.reshape(ns, dec_seq, d)) + tuple(jnp.stack(o) for o in outs)
```

```python
import functools

import jax
import jax.numpy as jnp
import numpy as np
from jax import lax
from jax.experimental import pallas as pl
from jax.experimental.pallas import tpu as pltpu
from jax.experimental.pallas import tpu_sc as plsc

F32 = jnp.float32
BF16 = jnp.bfloat16
U32 = jnp.uint32
I32 = jnp.int32

D_MODEL = 1024
HEAD_DIM = 64
N_HEADS = 8
N_KV_HEADS = 2
GROUP = N_HEADS // N_KV_HEADS
ATTN_WIDTH = N_HEADS * HEAD_DIM
KV_WIDTH = N_KV_HEADS * HEAD_DIM
POOL_WIDTH = 512
POOL_WINDOWS = (2, 4, 8, 16)
POOL_GC = POOL_WIDTH // len(POOL_WINDOWS)
POOL_PREFIX = max(POOL_WINDOWS) - 1
POOL_HALO = POOL_PREFIX + 1
IN_WIDTH = ATTN_WIDTH + 2 * KV_WIDTH + POOL_WIDTH
WINDOW = 128
ROPE_THETA = 500000.0
ROT_DIM = HEAD_DIM // 4
N_EXPERTS = 32
TOP_K = 4
D_FF = 1024
SWIGLU_ALPHA = 1.702
SWIGLU_LIMIT = 7.0
PLE_DIM = 256
PAST_LEN = 16384
EPS = 1e-5
NEG_INF = -1e30

LANES = 128
Q_TILES = ATTN_WIDTH // LANES

MIX_ROWS = 256
SAMPLE_CHUNK = 16
ROUTE_ROWS = 512
MOE_BM = 512
SC_CORES = 2
SC_SUBCORES = 16
SC_WORKERS = SC_CORES * SC_SUBCORES
SC_CHUNK = 64
PLE_ROWS = 512
VMEM_LIMIT = 56 * 1024 * 1024


def _rms(x, g):
    return x * lax.rsqrt(jnp.mean(x * x, axis=-1, keepdims=True) + EPS) * g


def _pack_halves(x):
    w = x.shape[1] // 2
    lo = lax.bitcast_convert_type(x[:, :w].astype(BF16).astype(F32), U32) >> 16
    hi = lax.bitcast_convert_type(x[:, w:].astype(BF16).astype(F32), U32) & jnp.uint32(0xFFFF0000)
    return lo | hi


def _unpack_halves(packed):
    lo = lax.bitcast_convert_type(packed << 16, F32)
    hi = lax.bitcast_convert_type(packed & jnp.uint32(0xFFFF0000), F32)
    return jnp.concatenate([lo, hi], axis=1)


def _mm(a, b, nt=False):
    dims = (((1,), (1 if nt else 0,)), ((), ()))
    if b.dtype == F32:
        return lax.dot_general(a.astype(F32), b, dims, preferred_element_type=F32, precision=lax.Precision.HIGHEST)
    return lax.dot_general(a.astype(BF16), b, dims, preferred_element_type=F32)


def _head_norm_rope(t, hmean, gain, cos, sin_a, sin_b):
    t = t * lax.rsqrt(_mm(t * t, hmean) + EPS) * gain
    return t * cos + pltpu.roll(t, ROT_DIM // 2, axis=1) * sin_a + pltpu.roll(t, LANES - ROT_DIM // 2, axis=1) * sin_b


def _softmax_pv(s, sink, v, ones):
    m = jnp.maximum(jnp.max(s, axis=-1, keepdims=True), sink)
    e = jnp.exp(s - m).astype(v.dtype)
    den = _mm(e, ones) + jnp.exp(sink - m)
    return _mm(e, v) / den


def _mixer_prompt_kernel(sinks_ref, x_ref, cos_ref, sa_ref, sb_ref, g_ref, win_ref, qn_ref, kn_ref, hm_ref,
                         wpool_ref, pscale_ref, wout_ref, *rest, row_offset, aliased):
    x1_ref, klast_ref, vlast_ref, ulast_ref, kprev, vprev, uext, mix = rest[1:] if aliased else rest
    i = pl.program_id(0)
    rows = x_ref.shape[0]
    n_sub = rows // WINDOW
    cdt = win_ref.dtype
    row0 = row_offset + i * rows

    @pl.when(i == 0)
    def _():
        kprev[...] = jnp.zeros_like(kprev)
        vprev[...] = jnp.zeros_like(vprev)
        uext[0:POOL_HALO, :] = jnp.zeros((POOL_HALO, POOL_WIDTH), F32)

    x = x_ref[...]
    proj = _mm(_rms(x, g_ref[...]), win_ref[...])
    cos, sin_a, sin_b = cos_ref[...], sa_ref[...], sb_ref[...]
    hmean = hm_ref[...]

    k = _head_norm_rope(proj[:, ATTN_WIDTH:ATTN_WIDTH + KV_WIDTH], hmean, kn_ref[...], cos, sin_a, sin_b)
    v = proj[:, ATTN_WIDTH + KV_WIDTH:ATTN_WIDTH + 2 * KV_WIDTH]
    u = proj[:, ATTN_WIDTH + 2 * KV_WIDTH:]
    klast_ref[...] = k[rows - WINDOW:, :]
    vlast_ref[...] = v[rows - WINDOW:, :]
    ulast_ref[...] = u[rows - POOL_HALO:, :]
    k_c = k.astype(cdt)
    v_c = v.astype(cdt)

    lane = lax.broadcasted_iota(I32, (WINDOW, LANES), 1)
    left = lane < HEAD_DIM
    qi = lax.broadcasted_iota(I32, (WINDOW, 2 * WINDOW), 0)
    kj = lax.broadcasted_iota(I32, (WINDOW, 2 * WINDOW), 1)
    band = (kj - qi >= 1) & (kj - qi <= WINDOW)
    ones = jnp.ones((2 * WINDOW, LANES), cdt)
    scale = HEAD_DIM ** -0.5

    for c in range(n_sub):
        r0 = c * WINDOW
        if c == 0:
            k_cat = jnp.concatenate([kprev[...], k_c[0:WINDOW]], axis=0)
            v_cat = jnp.concatenate([vprev[...], v_c[0:WINDOW]], axis=0)
            mask = band & (kj + (row0 - WINDOW) >= 0)
        else:
            k_cat = k_c[r0 - WINDOW:r0 + WINDOW]
            v_cat = v_c[r0 - WINDOW:r0 + WINDOW]
            mask = band
        mask2 = jnp.concatenate([mask, mask], axis=0)
        for j in range(Q_TILES):
            qt = _head_norm_rope(proj[r0:r0 + WINDOW, j * LANES:(j + 1) * LANES], hmean, qn_ref[...],
                                 cos[r0:r0 + WINDOW], sin_a[r0:r0 + WINDOW], sin_b[r0:r0 + WINDOW]) * scale
            q2 = jnp.concatenate([jnp.where(left, qt, 0.0), jnp.where(left, 0.0, qt)], axis=0)
            s = jnp.where(mask2, _mm(q2, k_cat, nt=True), NEG_INF)
            sink = jnp.concatenate([jnp.full((WINDOW, 1), sinks_ref[j], F32),
                                    jnp.full((WINDOW, 1), sinks_ref[j + Q_TILES], F32)], axis=0)
            o = _softmax_pv(s, sink, v_cat, ones)
            mix[r0:r0 + WINDOW, j * LANES:(j + 1) * LANES] = jnp.where(left, o[0:WINDOW], o[WINDOW:]).astype(cdt)

    kprev[...] = k_c[rows - WINDOW:]
    vprev[...] = v_c[rows - WINDOW:]

    uext[POOL_HALO:POOL_HALO + rows, :] = u
    pos1 = (lax.broadcasted_iota(I32, (rows, 1), 0) + row0 + 1).astype(F32)
    for gi, w in enumerate(POOL_WINDOWS):
        cols = slice(gi * POOL_GC, (gi + 1) * POOL_GC)
        wsum = u[:, cols]
        for sft in range(1, w):
            wsum = wsum + uext[POOL_HALO - sft:POOL_HALO - sft + rows, cols]
        d = wsum / jnp.minimum(pos1, float(w)) - u[:, cols]
        y = _mm(d, wpool_ref[gi]) * pscale_ref[:, cols]
        mix[:, ATTN_WIDTH + gi * POOL_GC:ATTN_WIDTH + (gi + 1) * POOL_GC] = y.astype(cdt)
    uext[0:POOL_HALO, :] = u[rows - POOL_HALO:, :]

    x1_ref[...] = x + _mm(mix[...], wout_ref[...])


def _mixer_prompt(x_full, row_offset, sinks, tabs, g_attn, w_in, qn, kn, hmean, w_pool, pscale, w_out, x1_into=None):
    t = x_full.shape[0] - row_offset
    rows = MIX_ROWS
    cdt = w_in.dtype
    assert t % rows == 0 and row_offset % rows == 0 and rows % WINDOW == 0 and rows >= POOL_HALO
    blk0, n_steps = row_offset // rows, t // rows
    const = lambda shape: pl.BlockSpec(shape, lambda i, *_: (0,) * len(shape))
    row_blk = lambda width: pl.BlockSpec((rows, width), lambda i, *_: (blk0 + i, 0))
    aliased = x1_into is not None
    if aliased:
        assert x1_into.shape == x_full.shape
        x1_spec = pl.BlockSpec((rows, D_MODEL), lambda i, *_: (blk0 + n_steps - 1, 0))
        x1_shape = x1_into.shape
        extra_specs, extra_args, aliases = [pl.BlockSpec(memory_space=pl.ANY)], [x1_into], {13: 0}
    else:
        x1_spec = pl.BlockSpec((rows, D_MODEL), lambda i, *_: (i, 0))
        x1_shape = (t, D_MODEL)
        extra_specs, extra_args, aliases = [], [], {}
    grid_spec = pltpu.PrefetchScalarGridSpec(
        num_scalar_prefetch=1,
        grid=(n_steps,),
        in_specs=[row_blk(D_MODEL), row_blk(LANES), row_blk(LANES), row_blk(LANES),
                  const((1, D_MODEL)), const((D_MODEL, IN_WIDTH)), const((1, LANES)), const((1, LANES)),
                  const((LANES, LANES)), const((len(POOL_WINDOWS), POOL_GC, POOL_GC)), const((1, POOL_WIDTH)),
                  const((D_MODEL, D_MODEL))] + extra_specs,
        out_specs=[x1_spec, const((WINDOW, KV_WIDTH)), const((WINDOW, KV_WIDTH)),
                   const((POOL_HALO, POOL_WIDTH))],
        scratch_shapes=[pltpu.VMEM((WINDOW, KV_WIDTH), cdt), pltpu.VMEM((WINDOW, KV_WIDTH), cdt),
                        pltpu.VMEM((POOL_HALO + rows, POOL_WIDTH), F32), pltpu.VMEM((rows, D_MODEL), cdt)],
    )
    return pl.pallas_call(
        functools.partial(_mixer_prompt_kernel, row_offset=row_offset, aliased=aliased),
        grid_spec=grid_spec,
        out_shape=[jax.ShapeDtypeStruct(x1_shape, F32), jax.ShapeDtypeStruct((WINDOW, KV_WIDTH), F32),
                   jax.ShapeDtypeStruct((WINDOW, KV_WIDTH), F32), jax.ShapeDtypeStruct((POOL_HALO, POOL_WIDTH), F32)],
        input_output_aliases=aliases,
        compiler_params=pltpu.CompilerParams(dimension_semantics=("arbitrary",), vmem_limit_bytes=VMEM_LIMIT),
        name="mixer_prompt",
    )(sinks, x_full, *tabs, g_attn, w_in, qn, kn, hmean, w_pool, pscale, w_out, *extra_args)


def _mixer_sample_kernel(x_ref, ck_ref, cv_ref, st_ref, cos_ref, sa_ref, sb_ref, sink8_ref, g_ref, win_ref, qn_ref,
                         kn_ref, hm_ref, wpool_ref, pscale_ref, wout_ref,
                         x1_ref, nk_ref, nv_ref, nst_ref, o8, *, pos):
    nb = x_ref.shape[0]
    wb = ck_ref.shape[1]
    x = x_ref[...]
    h = _rms(x, g_ref[...]).astype(BF16)
    proj = jnp.dot(h, win_ref[...], preferred_element_type=F32)
    cos, sin_a, sin_b = cos_ref[...], sa_ref[...], sb_ref[...]
    hmean = hm_ref[...]
    k = _head_norm_rope(proj[:, ATTN_WIDTH:ATTN_WIDTH + KV_WIDTH], hmean, kn_ref[...], cos, sin_a, sin_b)
    v = proj[:, ATTN_WIDTH + KV_WIDTH:ATTN_WIDTH + 2 * KV_WIDTH]
    u = proj[:, ATTN_WIDTH + 2 * KV_WIDTH:]

    nk_ref[:, 0:wb - 1, :] = ck_ref[:, 1:wb, :]
    nv_ref[:, 0:wb - 1, :] = cv_ref[:, 1:wb, :]
    for b in range(nb):
        nk_ref[b, wb - 1:wb, :] = k[b:b + 1, :]
        nv_ref[b, wb - 1:wb, :] = v[b:b + 1, :]

    r8 = lax.broadcasted_iota(I32, (nb * 8, LANES), 0)
    lane8 = lax.broadcasted_iota(I32, (nb * 8, LANES), 1)
    keep = (lane8 < HEAD_DIM) == (r8 % 2 == 0)
    rep = (lax.broadcasted_iota(I32, (nb * 8, nb), 0) // 8 == lax.broadcasted_iota(I32, (nb * 8, nb), 1)).astype(BF16)
    q8 = jnp.zeros((nb * 8, LANES), F32)
    scale = HEAD_DIM ** -0.5
    for j in range(Q_TILES):
        qt = _head_norm_rope(proj[:, j * LANES:(j + 1) * LANES], hmean, qn_ref[...], cos, sin_a, sin_b) * scale
        qrep = jnp.dot(rep, qt.astype(BF16), preferred_element_type=F32)
        q8 = jnp.where(keep & ((r8 % 8) // 2 == j), qrep, q8)
    q8 = q8.astype(BF16)

    sink8 = sink8_ref[:, 0:1]
    ones_bf = jnp.ones((wb, LANES), BF16)
    assert pos >= wb - 1 and wb <= WINDOW
    for b in range(nb):
        kb = nk_ref[b].astype(BF16)
        vb = nv_ref[b].astype(BF16)
        s = lax.dot_general(q8[b * 8:(b + 1) * 8], kb, (((1,), (1,)), ((), ())), preferred_element_type=F32)
        o8[b * 8:(b + 1) * 8, :] = _softmax_pv(s, sink8, vb, ones_bf)
    o8m = jnp.where(keep, o8[...], 0.0).astype(BF16)

    a_tiles = []
    sel_r = lax.broadcasted_iota(I32, (nb, nb * 8), 1)
    sel_b = lax.broadcasted_iota(I32, (nb, nb * 8), 0)
    for j in range(Q_TILES):
        sel = ((sel_r // 8 == sel_b) & ((sel_r % 8) // 2 == j)).astype(BF16)
        a_tiles.append(jnp.dot(sel, o8m, preferred_element_type=F32))

    z_tiles = []
    for gi, w in enumerate(POOL_WINDOWS):
        cols = slice(gi * POOL_GC, (gi + 1) * POOL_GC)
        wsum = u[:, cols]
        for sft in range(1, w):
            wsum = wsum + st_ref[POOL_PREFIX - sft, :, cols]
        d = wsum / float(min(pos + 1, w)) - u[:, cols]
        z_tiles.append(jnp.dot(d.astype(BF16), wpool_ref[gi], preferred_element_type=F32) * pscale_ref[:, cols])
    nst_ref[0:POOL_PREFIX - 1] = st_ref[1:POOL_PREFIX]
    nst_ref[POOL_PREFIX - 1] = u

    mixv = jnp.concatenate(a_tiles + z_tiles, axis=1).astype(BF16)
    x1_ref[...] = x + jnp.dot(mixv, wout_ref[...], preferred_element_type=F32)


def _mixer_sample(x, ck, cv, st, pos, tabs, sink8, g_attn, w_in, qn, kn, hmean, w_pool, pscale, w_out):
    n, wb = ck.shape[0], ck.shape[1]
    nb = SAMPLE_CHUNK
    assert n % nb == 0
    const = lambda shape: pl.BlockSpec(shape, lambda i: (0,) * len(shape))
    cache_blk = pl.BlockSpec((nb, wb, KV_WIDTH), lambda i: (i, 0, 0))
    st_blk = pl.BlockSpec((POOL_PREFIX, nb, POOL_WIDTH), lambda i: (0, i, 0))
    x_blk = pl.BlockSpec((nb, D_MODEL), lambda i: (i, 0))
    return pl.pallas_call(
        functools.partial(_mixer_sample_kernel, pos=pos),
        grid=(n // nb,),
        in_specs=[x_blk, cache_blk, cache_blk, st_blk, const((1, LANES)), const((1, LANES)), const((1, LANES)),
                  const((8, LANES)), const((1, D_MODEL)), const((D_MODEL, IN_WIDTH)), const((1, LANES)),
                  const((1, LANES)), const((LANES, LANES)), const((len(POOL_WINDOWS), POOL_GC, POOL_GC)),
                  const((1, POOL_WIDTH)), const((D_MODEL, D_MODEL))],
        out_specs=[x_blk, cache_blk, cache_blk, st_blk],
        out_shape=[jax.ShapeDtypeStruct((n, D_MODEL), F32), jax.ShapeDtypeStruct(ck.shape, F32),
                   jax.ShapeDtypeStruct(cv.shape, F32), jax.ShapeDtypeStruct(st.shape, F32)],
        scratch_shapes=[pltpu.VMEM((nb * 8, LANES), F32)],
        compiler_params=pltpu.CompilerParams(dimension_semantics=("arbitrary",), vmem_limit_bytes=VMEM_LIMIT),
        name="mixer_sample",
    )(x, ck, cv, st, *tabs, sink8, g_attn, w_in, qn, kn, hmean, w_pool, pscale, w_out)


def _block_plan(cnt, bm, n_lanes):
    e_sub = lax.broadcasted_iota(I32, (N_EXPERTS, LANES), 0)
    e_lane = lax.broadcasted_iota(I32, (N_EXPERTS, LANES), 1)
    padded = jnp.floor((cnt + (bm - 1)) / bm) * bm
    padded_lane = jnp.sum(jnp.where(e_sub == e_lane, padded, 0.0), axis=0, keepdims=True)
    pad_end = jnp.sum(jnp.where(e_lane <= e_sub, padded_lane, 0.0), axis=1, keepdims=True)
    pad_start = pad_end - padded
    blk_start = lax.broadcasted_iota(I32, (N_EXPERTS, n_lanes), 1).astype(F32) * bm
    blk_e = jnp.minimum(jnp.sum((pad_end <= blk_start).astype(F32), axis=0, keepdims=True), N_EXPERTS - 1.0)
    mine = lax.broadcasted_iota(I32, (N_EXPERTS, n_lanes), 0).astype(F32) == blk_e
    last = jnp.sum(jnp.where(mine, pad_start + cnt, 0.0), axis=0, keepdims=True)
    blk_valid = jnp.clip(last - blk_start[0:1], 0.0, float(bm))
    e_blk = lax.broadcasted_iota(I32, (N_EXPERTS, n_lanes), 0).astype(F32)
    later = jnp.min(jnp.where((e_blk > blk_e) & (cnt > 0.0), e_blk, float(N_EXPERTS)), axis=0, keepdims=True)
    blk_next = jnp.where(later < N_EXPERTS, later, -1.0)
    return pad_start, blk_e, blk_valid, blk_next


def _route_kernel(x1_ref, g_ref, wr_ref, br_ref, cnt_in_ref, hf_ref, idx_ref, gate_ref, pos_ref, cnt_ref,
                  pstart_ref, blke_ref, blkv_ref, blkn_ref, counts, *, bm):
    i = pl.program_id(0)
    rows = x1_ref.shape[0]

    @pl.when(i == 0)
    def _():
        counts[...] = cnt_in_ref[...]

    h = _rms(x1_ref[...], g_ref[...])
    hf_ref[...] = _pack_halves(h)

    logits = lax.dot_general(wr_ref[...], h, (((1,), (1,)), ((), ())), preferred_element_type=F32,
                             precision=lax.Precision.HIGHEST) + br_ref[...]
    eid = lax.broadcasted_iota(I32, (N_EXPERTS, rows), 0).astype(F32)
    work = logits
    vals, hots = [], []
    for kk in range(TOP_K):
        m = jnp.max(work, axis=0, keepdims=True)
        first = jnp.min(jnp.where(work == m, eid, float(N_EXPERTS)), axis=0, keepdims=True)
        hot = eid == first
        work = jnp.where(hot, -jnp.inf, work)
        vals.append(m)
        hots.append(hot)
        idx_ref[kk:kk + 1, :] = first.astype(I32)
    es = [jnp.exp(vv - vals[0]) for vv in vals]
    den = es[0] + es[1] + es[2] + es[3]
    for kk in range(TOP_K):
        gate_ref[kk:kk + 1, :] = es[kk] / den

    chosen = hots[0] | hots[1] | hots[2] | hots[3]
    before = (lax.broadcasted_iota(I32, (rows, rows), 0) < lax.broadcasted_iota(I32, (rows, rows), 1)).astype(BF16)
    rank = jnp.dot(chosen.astype(BF16), before, preferred_element_type=F32) + counts[...]
    for kk in range(TOP_K):
        pos_ref[kk:kk + 1, :] = jnp.sum(jnp.where(hots[kk], rank, 0.0), axis=0, keepdims=True).astype(I32)
    counts[...] = counts[...] + jnp.sum(chosen.astype(F32), axis=1, keepdims=True)
    cnt_ref[...] = counts[...]

    @pl.when(i == pl.num_programs(0) - 1)
    def _():
        pad_start, blk_e, blk_valid, blk_next = _block_plan(counts[...], bm, blke_ref.shape[1])
        pstart_ref[...] = pad_start.astype(I32)
        blke_ref[...] = blk_e.astype(I32)
        blkv_ref[...] = blk_valid.astype(I32)
        blkn_ref[...] = blk_next.astype(I32)


def _route(x1, g_ffn, wr_t, br, cnt_in, rows, bm, n_blocks):
    n = x1.shape[0]
    assert n % rows == 0
    nb_lanes = -(-n_blocks // LANES) * LANES
    const = lambda shape: pl.BlockSpec(shape, lambda i: (0,) * len(shape))
    tok_blk = pl.BlockSpec((TOP_K, rows), lambda i: (0, i))
    return pl.pallas_call(
        functools.partial(_route_kernel, bm=bm),
        grid=(n // rows,),
        in_specs=[pl.BlockSpec((rows, D_MODEL), lambda i: (i, 0)), const((1, D_MODEL)), const((N_EXPERTS, D_MODEL)),
                  const((N_EXPERTS, 1)), const((N_EXPERTS, 1))],
        out_specs=[pl.BlockSpec((rows, D_MODEL // 2), lambda i: (i, 0)), tok_blk, tok_blk, tok_blk,
                   const((N_EXPERTS, 1)), const((N_EXPERTS, 1)), const((1, nb_lanes)), const((1, nb_lanes)),
                   const((1, nb_lanes))],
        out_shape=[jax.ShapeDtypeStruct((n, D_MODEL // 2), U32), jax.ShapeDtypeStruct((TOP_K, n), I32),
                   jax.ShapeDtypeStruct((TOP_K, n), F32), jax.ShapeDtypeStruct((TOP_K, n), I32),
                   jax.ShapeDtypeStruct((N_EXPERTS, 1), F32), jax.ShapeDtypeStruct((N_EXPERTS, 1), I32),
                   jax.ShapeDtypeStruct((1, nb_lanes), I32), jax.ShapeDtypeStruct((1, nb_lanes), I32),
                   jax.ShapeDtypeStruct((1, nb_lanes), I32)],
        scratch_shapes=[pltpu.VMEM((N_EXPERTS, 1), F32)],
        compiler_params=pltpu.CompilerParams(dimension_semantics=("arbitrary",), vmem_limit_bytes=VMEM_LIMIT),
        name="route",
    )(x1, g_ffn, wr_t, br, cnt_in)


def _sc_mesh():
    return plsc.VectorSubcoreMesh(core_axis_name="core", subcore_axis_name="subcore")


def _sc_worker_id():
    return lax.axis_index("core") * SC_SUBCORES + lax.axis_index("subcore")


def _scatter_rows(xa, xb, dest, n_rows):
    ch = SC_CHUNK
    na, w = xa.shape
    n = na + xb.shape[0]
    nk = dest.shape[0]
    assert na % ch == 0 and n % ch == 0 and dest.shape[1] == n and xb.shape[1] == w and xa.dtype == xb.dtype
    n_chunks = n // ch

    @pl.kernel(out_type=jax.ShapeDtypeStruct((n_rows, w), xa.dtype), mesh=_sc_mesh(),
               scratch_types=[pltpu.VMEM((ch,), I32), pltpu.VMEM((ch, w), xa.dtype)])
    def scatter_kernel(xa_hbm, xb_hbm, d_hbm, o_hbm, idx_v, buf):
        wid = _sc_worker_id()

        @pl.loop(0, -(-n_chunks // SC_WORKERS))
        def _(j):
            c = j * SC_WORKERS + wid

            @pl.when(c < na // ch)
            def _():
                pltpu.sync_copy(xa_hbm.at[pl.ds(c * ch, ch)], buf)

            @pl.when((c >= na // ch) & (c < n_chunks))
            def _():
                pltpu.sync_copy(xb_hbm.at[pl.ds(c * ch - na, ch)], buf)

            @pl.when(c < n_chunks)
            def _():
                for kk in range(nk):
                    pltpu.sync_copy(d_hbm.at[pl.ds(kk * n + c * ch, ch)], idx_v)
                    pltpu.sync_copy(buf, o_hbm.at[idx_v])

    return scatter_kernel(xa, xb, dest.reshape(-1))


def _gather_rows(src, idx):
    ch = SC_CHUNK
    m = idx.shape[0]
    w = src.shape[1]
    per = m // SC_WORKERS
    n_pairs = per // (2 * ch)
    assert m % SC_WORKERS == 0 and per % (2 * ch) == 0
    dma = pltpu.SemaphoreType.DMA

    @pl.kernel(out_type=jax.ShapeDtypeStruct((m, w), src.dtype), mesh=_sc_mesh(),
               scratch_types=[pltpu.VMEM((per,), I32), pltpu.VMEM((ch, w), src.dtype), pltpu.VMEM((ch, w), src.dtype),
                              dma, dma, dma, dma])
    def gather_kernel(s_hbm, i_hbm, o_hbm, idx_v, buf_a, buf_b, sem_ga, sem_gb, sem_wa, sem_wb):
        base = _sc_worker_id() * per
        pltpu.sync_copy(i_hbm.at[pl.ds(base, per)], idx_v)

        def fetch(j, buf, sem):
            return pltpu.make_async_copy(s_hbm.at[idx_v.at[pl.ds(j * ch, ch)]], buf, sem)

        def put(j, buf, sem):
            return pltpu.make_async_copy(buf, o_hbm.at[pl.ds(base + j * ch, ch)], sem)

        fetch(0, buf_a, sem_ga).start()

        @pl.loop(0, n_pairs)
        def _(p):
            j0 = 2 * p
            j1 = j0 + 1

            @pl.when(p > 0)
            def _():
                put(j1 - 2, buf_b, sem_wb).wait()

            fetch(j1, buf_b, sem_gb).start()
            fetch(j0, buf_a, sem_ga).wait()
            put(j0, buf_a, sem_wa).start()
            fetch(j1, buf_b, sem_gb).wait()
            put(j1, buf_b, sem_wb).start()
            put(j0, buf_a, sem_wa).wait()

            @pl.when(p + 1 < n_pairs)
            def _():
                fetch(j0 + 2, buf_a, sem_ga).start()

        put(2 * n_pairs - 1, buf_b, sem_wb).wait()

    return gather_kernel(src, idx)


def _moe_kernel(blk_e_ref, blk_valid_ref, blk_next_ref, xs_ref, wgu_hbm, bgu_ref, wd_hbm, bd_ref, y_ref,
                wgu_f32, wd_f32, wgu_bf, wd_bf, sems, *, e0):
    i = pl.program_id(0)
    e = blk_e_ref[i]
    n_valid = blk_valid_ref[i]
    used = n_valid > 0
    new_expert = (i == 0) | (blk_e_ref[jnp.maximum(i - 1, 0)] != e)

    def weight_copies(expert):
        return (pltpu.make_async_copy(wgu_hbm.at[e0 + expert], wgu_f32, sems.at[0]),
                pltpu.make_async_copy(wd_hbm.at[e0 + expert], wd_f32, sems.at[1]))

    @pl.when(used & (i == 0))
    def _():
        for cp in weight_copies(e):
            cp.start()

    @pl.when(used & new_expert)
    def _():
        for cp in weight_copies(e):
            cp.wait()
        wgu_bf[...] = wgu_f32[...].astype(BF16)
        wd_bf[...] = wd_f32[...].astype(BF16)
        nxt = blk_next_ref[i]

        @pl.when(nxt >= 0)
        def _():
            for cp in weight_copies(nxt):
                cp.start()

    @pl.when(used)
    def _():
        valid = lax.broadcasted_iota(I32, xs_ref.shape, 0) < n_valid
        xb = _unpack_halves(jnp.where(valid, xs_ref[...], jnp.uint32(0))).astype(BF16)
        gu = jnp.dot(xb, wgu_bf[...], preferred_element_type=F32) + bgu_ref[0]
        g = jnp.minimum(gu[:, :D_FF], SWIGLU_LIMIT)
        up = jnp.clip(gu[:, D_FF:], -SWIGLU_LIMIT, SWIGLU_LIMIT)
        act = (up + 1.0) * (g * jax.nn.sigmoid(SWIGLU_ALPHA * g))
        y_ref[...] = _pack_halves(jnp.dot(act.astype(BF16), wd_bf[...], preferred_element_type=F32) + bd_ref[0])

    @pl.when(jnp.logical_not(used))
    def _():
        y_ref[...] = jnp.zeros_like(y_ref)


def _moe(xs, blk_e, blk_valid, blk_next, layer, w_gu, b_gu, w_d, b_d):
    n_rows = xs.shape[0]
    bm = MOE_BM
    assert n_rows % bm == 0
    e0 = layer * N_EXPERTS
    n_we = w_gu.shape[0] * w_gu.shape[1]
    any_spec = pl.BlockSpec(memory_space=pl.ANY)
    grid_spec = pltpu.PrefetchScalarGridSpec(
        num_scalar_prefetch=3,
        grid=(n_rows // bm,),
        in_specs=[pl.BlockSpec((bm, D_MODEL // 2), lambda i, be, bv, bn: (i, 0)),
                  any_spec,
                  pl.BlockSpec((1, 1, 2 * D_FF), lambda i, be, bv, bn: (e0 + be[i], 0, 0)),
                  any_spec,
                  pl.BlockSpec((1, 1, D_MODEL), lambda i, be, bv, bn: (e0 + be[i], 0, 0))],
        out_specs=pl.BlockSpec((bm, D_MODEL // 2), lambda i, be, bv, bn: (i, 0)),
        scratch_shapes=[pltpu.VMEM((D_MODEL, 2 * D_FF), F32), pltpu.VMEM((D_FF, D_MODEL), F32),
                        pltpu.VMEM((D_MODEL, 2 * D_FF), BF16), pltpu.VMEM((D_FF, D_MODEL), BF16),
                        pltpu.SemaphoreType.DMA((2,))],
    )
    return pl.pallas_call(
        functools.partial(_moe_kernel, e0=e0),
        grid_spec=grid_spec,
        out_shape=jax.ShapeDtypeStruct((n_rows, D_MODEL // 2), U32),
        compiler_params=pltpu.CompilerParams(dimension_semantics=("arbitrary",), vmem_limit_bytes=VMEM_LIMIT),
        name="moe_experts",
    )(blk_e, blk_valid, blk_next, xs, w_gu.reshape(n_we, D_MODEL, 2 * D_FF), b_gu.reshape(n_we, 1, 2 * D_FF),
      w_d.reshape(n_we, D_FF, D_MODEL), b_d.reshape(n_we, 1, D_MODEL))


def _ple_kernel(x1_ref, y0_ref, y1_ref, y2_ref, y3_ref, gates_ref, p_ref, g_ref, wg_ref, wp_ref, out_ref):
    x2 = x1_ref[...]
    gates = gates_ref[...]
    for kk, y_ref in enumerate((y0_ref, y1_ref, y2_ref, y3_ref)):
        x2 = x2 + _unpack_halves(y_ref[...]) * gates[:, kk:kk + 1]
    hp = _rms(x2, g_ref[...]).astype(BF16)
    gate = jax.nn.sigmoid(jnp.dot(hp, wg_ref[...], preferred_element_type=F32))
    pp = jnp.dot(p_ref[...].astype(BF16), wp_ref[...], preferred_element_type=F32)
    out_ref[...] = x2 + gate * pp


def _ple(x1, y_tok, y0, gates_t, tok0, p_all, p0, g_ple, w_gate, w_proj, rows):
    n = x1.shape[0]
    assert n % rows == 0 and tok0 % rows == 0 and y0 % rows == 0 and p0 % rows == 0
    const = lambda shape: pl.BlockSpec(shape, lambda i: (0,) * len(shape))
    y_blk = lambda kk: pl.BlockSpec((rows, D_MODEL // 2), lambda i: ((y0 + kk * n) // rows + i, 0))
    return pl.pallas_call(
        _ple_kernel,
        grid=(n // rows,),
        in_specs=[pl.BlockSpec((rows, D_MODEL), lambda i: (i, 0)), y_blk(0), y_blk(1), y_blk(2), y_blk(3),
                  pl.BlockSpec((rows, TOP_K), lambda i: (tok0 // rows + i, 0)),
                  pl.BlockSpec((rows, PLE_DIM), lambda i: (p0 // rows + i, 0)),
                  const((1, D_MODEL)), const((D_MODEL, D_MODEL)), const((PLE_DIM, D_MODEL))],
        out_specs=pl.BlockSpec((rows, D_MODEL), lambda i: (i, 0)),
        out_shape=jax.ShapeDtypeStruct((n, D_MODEL), F32),
        compiler_params=pltpu.CompilerParams(dimension_semantics=("arbitrary",), vmem_limit_bytes=VMEM_LIMIT),
        name="combine_ple",
    )(x1, y_tok, y_tok, y_tok, y_tok, gates_t, p_all, g_ple, w_gate, w_proj)


def _rope_tables(pos):
    half = ROT_DIM // 2
    d = np.arange(LANES) % HEAD_DIM
    inv = ROPE_THETA ** (-jnp.arange(half, dtype=F32) / half)
    inv_lane = jnp.where(d < ROT_DIM, inv[d % half], 0.0)
    ang = pos.astype(F32)[:, None] * inv_lane[None, :]
    cos, sin = jnp.cos(ang), jnp.sin(ang)
    sin_a = jnp.where((d >= half) & (d < ROT_DIM), sin, 0.0)
    sin_b = jnp.where(d < half, -sin, 0.0)
    return cos, sin_a, sin_b


def _layer(layer, xp, xs, ck, cv, st, p_prompt_all, p_sample_all, past_len, refine_tail, tabs_p, tabs_s,
           norm_attn, w_in, q_norm, k_norm, attn_sinks, w_pool, pool_scale, w_out,
           norm_ffn, w_router, b_router, w_gate_up_all, b_gate_up_all, w_down_all, b_down_all,
           norm_ple, w_ple_gate, w_ple_proj):
    t, ns = xp.shape[0], xs.shape[0]
    w_q = w_in[:, :ATTN_WIDTH].reshape(D_MODEL, 2, Q_TILES, HEAD_DIM).transpose(0, 2, 1, 3).reshape(D_MODEL, ATTN_WIDTH)
    w_in_p = jnp.concatenate([w_q, w_in[:, ATTN_WIDTH:]], axis=1)
    w_oa = w_out[:ATTN_WIDTH].reshape(2, Q_TILES, HEAD_DIM, D_MODEL).transpose(1, 0, 2, 3).reshape(ATTN_WIDTH, D_MODEL)
    w_out_p = jnp.concatenate([w_oa, w_out[ATTN_WIDTH:]], axis=0)
    g_attn = norm_attn.reshape(1, D_MODEL)
    qn = jnp.tile(q_norm, 2).reshape(1, LANES)
    kn = jnp.tile(k_norm, 2).reshape(1, LANES)
    lane = np.arange(LANES)
    hmean = jnp.asarray((lane[:, None] // HEAD_DIM == lane[None, :] // HEAD_DIM) / HEAD_DIM, F32)
    pscale = pool_scale.reshape(1, POOL_WIDTH)
    sink8 = jnp.broadcast_to(attn_sinks.reshape(2, Q_TILES).T.reshape(8, 1), (8, LANES))

    mix_args = (g_attn, w_in_p.astype(BF16), qn, kn, hmean.astype(BF16), w_pool.astype(BF16), pscale,
                w_out_p.astype(BF16))
    x1p, nk_p, nv_p, nu_p = _mixer_prompt(xp, 0, attn_sinks, tabs_p, *mix_args)
    if refine_tail:
        hi_args = (g_attn, w_in_p, qn, kn, hmean, w_pool, pscale, w_out_p)
        x1p, nk_p, nv_p, nu_p = _mixer_prompt(xp, t - 2 * MIX_ROWS, attn_sinks, tabs_p, *hi_args, x1_into=x1p)
    st_t = jnp.transpose(st, (1, 0, 2))
    x1s, nk_s, nv_s, nst_t = _mixer_sample(xs, ck.reshape(ns, -1, KV_WIDTH), cv.reshape(ns, -1, KV_WIDTH), st_t,
                                           past_len, tabs_s, sink8, *mix_args)

    g_ffn = norm_ffn.reshape(1, D_MODEL)
    wr_t = w_router.T
    br = b_router.reshape(N_EXPERTS, 1)
    n_tok = t + ns
    bm = MOE_BM
    n_blocks = -(-(n_tok * TOP_K + N_EXPERTS * (bm - 1)) // bm)
    hf_p, idx_p, gate_p, pos_p, cnt_p = _route(x1p, g_ffn, wr_t, br, jnp.zeros((N_EXPERTS, 1), F32),
                                               ROUTE_ROWS, bm, n_blocks)[:5]
    hf_s, idx_s, gate_s, pos_s, _, pad_start, blk_e, blk_valid, blk_next = _route(x1s, g_ffn, wr_t, br, cnt_p, ns,
                                                                                  bm, n_blocks)
    idx = jnp.concatenate([idx_p, idx_s], axis=1)
    pos = jnp.concatenate([pos_p, pos_s], axis=1)
    blk_e, blk_valid, blk_next = (a[0, :n_blocks] for a in (blk_e, blk_valid, blk_next))
    start_of = jnp.sum(jnp.where(idx[None] == jnp.arange(N_EXPERTS, dtype=I32)[:, None, None],
                                 pad_start.reshape(N_EXPERTS, 1, 1), 0), axis=0)
    dest = start_of + pos

    xs_rows = _scatter_rows(hf_p, hf_s, dest, n_blocks * bm)
    y = _moe(xs_rows, blk_e, blk_valid, blk_next, layer, w_gate_up_all, b_gate_up_all, w_down_all, b_down_all)

    back = jnp.concatenate([dest[:, :t].reshape(-1), dest[:, t:].reshape(-1)])
    unit = SC_WORKERS * SC_CHUNK * 2
    back = jnp.concatenate([back, jnp.arange(-back.shape[0] % unit, dtype=I32)])
    y_tok = _gather_rows(y, back)

    g_ple = norm_ple.reshape(1, D_MODEL)
    wg_bf = w_ple_gate.astype(BF16)
    wp_bf = w_ple_proj.astype(BF16)
    yp = _ple(x1p, y_tok, 0, gate_p.T, 0, p_prompt_all, layer * t, g_ple, wg_bf, wp_bf, PLE_ROWS)
    ys = _ple(x1s, y_tok, TOP_K * t, gate_s.T, 0, p_sample_all, layer * ns, g_ple, wg_bf, wp_bf, ns)
    new_pool_s = jnp.transpose(nst_t, (1, 0, 2))
    return yp, ys, nk_p, nv_p, nu_p[POOL_HALO - POOL_PREFIX:], nk_s, nv_s, new_pool_s


def kernel(x_prompt, x_sample, cache_k, cache_v, state_pool, p_prompt, p_sample, norm_attn, w_in, q_norm, k_norm,
           attn_sinks, w_pool, pool_scale, w_out, norm_ffn, w_router, b_router, w_gate_up, b_gate_up, w_down, b_down,
           norm_ple, w_ple_gate, w_ple_proj):
    depth = norm_attn.shape[0]
    batch, seq, d = x_prompt.shape
    ns, dec_seq, _ = x_sample.shape
    wb = cache_k.shape[2]
    assert batch == 1 and dec_seq == 1 and d == D_MODEL and wb == WINDOW
    assert cache_k.shape[3:] == (N_KV_HEADS, HEAD_DIM) and state_pool.shape[2:] == (POOL_PREFIX, POOL_WIDTH)
    past_len = PAST_LEN
    yp = x_prompt.reshape(seq, d)
    ys = x_sample.reshape(ns, d)
    p_prompt_all = p_prompt.reshape(depth * seq, PLE_DIM)
    p_sample_all = p_sample.reshape(depth * ns, PLE_DIM)
    tabs_p = _rope_tables(jnp.arange(seq))
    tabs_s = _rope_tables(jnp.full((1,), past_len))
    outs = [[] for _ in range(6)]
    for i in range(depth):
        res = _layer(i, yp, ys, cache_k[i], cache_v[i], state_pool[i], p_prompt_all, p_sample_all, past_len,
                     i < depth - 1, tabs_p, tabs_s,
                     norm_attn[i], w_in[i], q_norm[i], k_norm[i], attn_sinks[i], w_pool[i], pool_scale[i], w_out[i],
                     norm_ffn[i], w_router[i], b_router[i], w_gate_up, b_gate_up, w_down, b_down,
                     norm_ple[i], w_ple_gate[i], w_ple_proj[i])
        yp, ys = res[0], res[1]
        kv_shape = (1, WINDOW, N_KV_HEADS, HEAD_DIM)
        outs[0].append(res[2].reshape(kv_shape))
        outs[1].append(res[3].reshape(kv_shape))
        outs[2].append(res[4].reshape(1, POOL_PREFIX, POOL_WIDTH))
        outs[3].append(res[5].reshape(ns, wb, N_KV_HEADS, HEAD_DIM))
        outs[4].append(res[6].reshape(ns, wb, N_KV_HEADS, HEAD_DIM))
        outs[5].append(res[7])
    return (yp.reshape(batch, seq, d), ys.reshape(ns, dec_seq, d)) + tuple(jnp.stack(o) for o in outs)
```

```python
import functools

import jax
import jax.numpy as jnp
import numpy as np
from jax import lax
from jax.experimental import pallas as pl
from jax.experimental.pallas import tpu as pltpu
from jax.experimental.pallas import tpu_sc as plsc

F32 = jnp.float32
BF16 = jnp.bfloat16
U32 = jnp.uint32
I32 = jnp.int32

D_MODEL = 1024
HEAD_DIM = 64
N_HEADS = 8
N_KV_HEADS = 2
GROUP = N_HEADS // N_KV_HEADS
ATTN_WIDTH = N_HEADS * HEAD_DIM
KV_WIDTH = N_KV_HEADS * HEAD_DIM
POOL_WIDTH = 512
POOL_WINDOWS = (2, 4, 8, 16)
POOL_GC = POOL_WIDTH // len(POOL_WINDOWS)
POOL_PREFIX = max(POOL_WINDOWS) - 1
POOL_HALO = POOL_PREFIX + 1
IN_WIDTH = ATTN_WIDTH + 2 * KV_WIDTH + POOL_WIDTH
WINDOW = 128
ROPE_THETA = 500000.0
ROT_DIM = HEAD_DIM // 4
N_EXPERTS = 32
TOP_K = 4
D_FF = 1024
SWIGLU_ALPHA = 1.702
SWIGLU_LIMIT = 7.0
PLE_DIM = 256
PAST_LEN = 16384
EPS = 1e-5
NEG_INF = -1e30

LANES = 128
Q_TILES = ATTN_WIDTH // LANES

MIX_ROWS = 256
SAMPLE_CHUNK = 16
ROUTE_ROWS = 512
MOE_BM = 512
SC_CORES = 2
SC_SUBCORES = 16
SC_WORKERS = SC_CORES * SC_SUBCORES
SC_CHUNK = 64
PLE_ROWS = 512
VMEM_LIMIT = 56 * 1024 * 1024


def _rms(x, g):
    return x * lax.rsqrt(jnp.mean(x * x, axis=-1, keepdims=True) + EPS) * g


def _pack_halves(x):
    w = x.shape[1] // 2
    lo = lax.bitcast_convert_type(x[:, :w].astype(BF16).astype(F32), U32) >> 16
    hi = lax.bitcast_convert_type(x[:, w:].astype(BF16).astype(F32), U32) & jnp.uint32(0xFFFF0000)
    return lo | hi


def _unpack_halves(packed):
    lo = lax.bitcast_convert_type(packed << 16, F32)
    hi = lax.bitcast_convert_type(packed & jnp.uint32(0xFFFF0000), F32)
    return jnp.concatenate([lo, hi], axis=1)


def _mm(a, b, nt=False):
    dims = (((1,), (1 if nt else 0,)), ((), ()))
    if b.dtype == F32:
        return lax.dot_general(a.astype(F32), b, dims, preferred_element_type=F32, precision=lax.Precision.HIGHEST)
    return lax.dot_general(a.astype(BF16), b, dims, preferred_element_type=F32)


def _head_norm_rope(t, hmean, gain, cos, sin_a, sin_b):
    t = t * lax.rsqrt(_mm(t * t, hmean) + EPS) * gain
    return t * cos + pltpu.roll(t, ROT_DIM // 2, axis=1) * sin_a + pltpu.roll(t, LANES - ROT_DIM // 2, axis=1) * sin_b


def _softmax_pv(s, sink, v, ones):
    m = jnp.maximum(jnp.max(s, axis=-1, keepdims=True), sink)
    e = jnp.exp(s - m).astype(v.dtype)
    den = _mm(e, ones) + jnp.exp(sink - m)
    return _mm(e, v) / den


def _mixer_prompt_kernel(sinks_ref, x_ref, cos_ref, sa_ref, sb_ref, g_ref, win_ref, qn_ref, kn_ref, hm_ref,
                         wpool_ref, pscale_ref, wout_ref, *rest, row_offset, aliased):
    x1_ref, klast_ref, vlast_ref, ulast_ref, kprev, vprev, uext, mix = rest[1:] if aliased else rest
    i = pl.program_id(0)
    rows = x_ref.shape[0]
    n_sub = rows // WINDOW
    cdt = win_ref.dtype
    row0 = row_offset + i * rows

    @pl.when(i == 0)
    def _():
        kprev[...] = jnp.zeros_like(kprev)
        vprev[...] = jnp.zeros_like(vprev)
        uext[0:POOL_HALO, :] = jnp.zeros((POOL_HALO, POOL_WIDTH), F32)

    x = x_ref[...]
    proj = _mm(_rms(x, g_ref[...]), win_ref[...])
    cos, sin_a, sin_b = cos_ref[...], sa_ref[...], sb_ref[...]

    n_t = Q_TILES + 1
    t_all = jnp.concatenate([proj[:, j * LANES:(j + 1) * LANES] for j in range(n_t)], axis=0)
    t3 = (t_all * lax.rsqrt(_mm(t_all * t_all, hm_ref[...]) + EPS)).reshape(n_t, rows, LANES)
    t3 = jnp.concatenate([t3[:Q_TILES] * (qn_ref[...] * HEAD_DIM ** -0.5), t3[Q_TILES:] * kn_ref[...]], axis=0)
    t2 = t3.reshape(n_t * rows, LANES)
    t3 = (t3 * cos + pltpu.roll(t2, ROT_DIM // 2, axis=1).reshape(n_t, rows, LANES) * sin_a
          + pltpu.roll(t2, LANES - ROT_DIM // 2, axis=1).reshape(n_t, rows, LANES) * sin_b)
    q3 = t3[:Q_TILES]
    k = t3[Q_TILES]
    v = proj[:, ATTN_WIDTH + KV_WIDTH:ATTN_WIDTH + 2 * KV_WIDTH]
    u = proj[:, ATTN_WIDTH + 2 * KV_WIDTH:]
    klast_ref[...] = k[rows - WINDOW:, :]
    vlast_ref[...] = v[rows - WINDOW:, :]
    ulast_ref[...] = u[rows - POOL_HALO:, :]
    k_c = k.astype(cdt)
    v_c = jnp.concatenate([v.astype(cdt), jnp.ones((rows, LANES), cdt)], axis=1)
    v_first = jnp.concatenate([vprev[...], jnp.ones((WINDOW, LANES), cdt)], axis=1)

    lane = lax.broadcasted_iota(I32, (WINDOW, LANES), 1)
    left = (lane < HEAD_DIM)[None]
    qi = lax.broadcasted_iota(I32, (WINDOW, 2 * WINDOW), 0)
    kj = lax.broadcasted_iota(I32, (WINDOW, 2 * WINDOW), 1)
    band = (kj - qi >= 1) & (kj - qi <= WINDOW)
    sink3 = jnp.concatenate([jnp.full((1, 1, 1), sinks_ref[j + Q_TILES * s], F32)
                             for j in range(Q_TILES) for s in range(2)], axis=0)
    n_g = 2 * Q_TILES

    for c in range(n_sub):
        r0 = c * WINDOW
        if c == 0:
            k_cat = jnp.concatenate([kprev[...], k_c[0:WINDOW]], axis=0)
            v_cat = jnp.concatenate([v_first, v_c[0:WINDOW]], axis=0)
            mask = band & (kj + (row0 - WINDOW) >= 0)
        else:
            k_cat = k_c[r0 - WINDOW:r0 + WINDOW]
            v_cat = v_c[r0 - WINDOW:r0 + WINDOW]
            mask = band
        q_c = q3[:, r0:r0 + WINDOW, :]
        q_all = jnp.concatenate([jnp.where(left, q_c, 0.0), jnp.where(left, 0.0, q_c)], axis=1)
        s = _mm(q_all.reshape(n_g * WINDOW, LANES), k_cat, nt=True).reshape(n_g, WINDOW, 2 * WINDOW)
        s = jnp.where(mask[None], s, NEG_INF)
        m = jnp.maximum(jnp.max(s, axis=-1, keepdims=True), sink3)
        e = jnp.exp(s - m).astype(cdt).reshape(n_g * WINDOW, 2 * WINDOW)
        ov = _mm(e, v_cat)
        den = ov[:, LANES:] + jnp.exp(sink3 - m).reshape(n_g * WINDOW, 1)
        o = (ov[:, :LANES] / den).reshape(Q_TILES, 2 * WINDOW, LANES)
        a = jnp.where(left, o[:, :WINDOW], o[:, WINDOW:])
        for j in range(Q_TILES):
            mix[r0:r0 + WINDOW, j * LANES:(j + 1) * LANES] = a[j].astype(cdt)

    kprev[...] = k_c[rows - WINDOW:]
    vprev[...] = v[rows - WINDOW:].astype(cdt)

    uext[POOL_HALO:POOL_HALO + rows, :] = u
    pos1 = (lax.broadcasted_iota(I32, (rows, 1), 0) + row0 + 1).astype(F32)
    for gi, w in enumerate(POOL_WINDOWS):
        cols = slice(gi * POOL_GC, (gi + 1) * POOL_GC)
        wsum = u[:, cols]
        for sft in range(1, w):
            wsum = wsum + uext[POOL_HALO - sft:POOL_HALO - sft + rows, cols]
        d = wsum / jnp.minimum(pos1, float(w)) - u[:, cols]
        y = _mm(d, wpool_ref[gi]) * pscale_ref[:, cols]
        mix[:, ATTN_WIDTH + gi * POOL_GC:ATTN_WIDTH + (gi + 1) * POOL_GC] = y.astype(cdt)
    uext[0:POOL_HALO, :] = u[rows - POOL_HALO:, :]

    x1_ref[...] = x + _mm(mix[...], wout_ref[...])


def _mixer_prompt(x_full, row_offset, sinks, tabs, g_attn, w_in, qn, kn, hmean, w_pool, pscale, w_out, x1_into=None):
    t = x_full.shape[0] - row_offset
    rows = MIX_ROWS
    cdt = w_in.dtype
    assert t % rows == 0 and row_offset % rows == 0 and rows % WINDOW == 0 and rows >= POOL_HALO
    blk0, n_steps = row_offset // rows, t // rows
    const = lambda shape: pl.BlockSpec(shape, lambda i, *_: (0,) * len(shape))
    row_blk = lambda width: pl.BlockSpec((rows, width), lambda i, *_: (blk0 + i, 0))
    aliased = x1_into is not None
    if aliased:
        assert x1_into.shape == x_full.shape
        x1_spec = pl.BlockSpec((rows, D_MODEL), lambda i, *_: (blk0 + n_steps - 1, 0))
        x1_shape = x1_into.shape
        extra_specs, extra_args, aliases = [pl.BlockSpec(memory_space=pl.ANY)], [x1_into], {13: 0}
    else:
        x1_spec = pl.BlockSpec((rows, D_MODEL), lambda i, *_: (i, 0))
        x1_shape = (t, D_MODEL)
        extra_specs, extra_args, aliases = [], [], {}
    grid_spec = pltpu.PrefetchScalarGridSpec(
        num_scalar_prefetch=1,
        grid=(n_steps,),
        in_specs=[row_blk(D_MODEL), row_blk(LANES), row_blk(LANES), row_blk(LANES),
                  const((1, D_MODEL)), const((D_MODEL, IN_WIDTH)), const((1, LANES)), const((1, LANES)),
                  const((LANES, LANES)), const((len(POOL_WINDOWS), POOL_GC, POOL_GC)), const((1, POOL_WIDTH)),
                  const((D_MODEL, D_MODEL))] + extra_specs,
        out_specs=[x1_spec, const((WINDOW, KV_WIDTH)), const((WINDOW, KV_WIDTH)),
                   const((POOL_HALO, POOL_WIDTH))],
        scratch_shapes=[pltpu.VMEM((WINDOW, KV_WIDTH), cdt), pltpu.VMEM((WINDOW, KV_WIDTH), cdt),
                        pltpu.VMEM((POOL_HALO + rows, POOL_WIDTH), F32), pltpu.VMEM((rows, D_MODEL), cdt)],
    )
    return pl.pallas_call(
        functools.partial(_mixer_prompt_kernel, row_offset=row_offset, aliased=aliased),
        grid_spec=grid_spec,
        out_shape=[jax.ShapeDtypeStruct(x1_shape, F32), jax.ShapeDtypeStruct((WINDOW, KV_WIDTH), F32),
                   jax.ShapeDtypeStruct((WINDOW, KV_WIDTH), F32), jax.ShapeDtypeStruct((POOL_HALO, POOL_WIDTH), F32)],
        input_output_aliases=aliases,
        compiler_params=pltpu.CompilerParams(dimension_semantics=("arbitrary",), vmem_limit_bytes=VMEM_LIMIT),
        name="mixer_prompt",
    )(sinks, x_full, *tabs, g_attn, w_in, qn, kn, hmean, w_pool, pscale, w_out, *extra_args)


def _mixer_sample_kernel(x_ref, ck_ref, cv_ref, st_ref, cos_ref, sa_ref, sb_ref, sink8_ref, g_ref, win_ref, qn_ref,
                         kn_ref, hm_ref, wpool_ref, pscale_ref, wout_ref,
                         x1_ref, nk_ref, nv_ref, nst_ref, o8, *, pos):
    nb = x_ref.shape[0]
    wb = ck_ref.shape[1]
    x = x_ref[...]
    h = _rms(x, g_ref[...]).astype(BF16)
    proj = jnp.dot(h, win_ref[...], preferred_element_type=F32)
    cos, sin_a, sin_b = cos_ref[...], sa_ref[...], sb_ref[...]
    hmean = hm_ref[...]
    k = _head_norm_rope(proj[:, ATTN_WIDTH:ATTN_WIDTH + KV_WIDTH], hmean, kn_ref[...], cos, sin_a, sin_b)
    v = proj[:, ATTN_WIDTH + KV_WIDTH:ATTN_WIDTH + 2 * KV_WIDTH]
    u = proj[:, ATTN_WIDTH + 2 * KV_WIDTH:]

    nk_ref[:, 0:wb - 1, :] = ck_ref[:, 1:wb, :]
    nv_ref[:, 0:wb - 1, :] = cv_ref[:, 1:wb, :]
    for b in range(nb):
        nk_ref[b, wb - 1:wb, :] = k[b:b + 1, :]
        nv_ref[b, wb - 1:wb, :] = v[b:b + 1, :]

    r8 = lax.broadcasted_iota(I32, (nb * 8, LANES), 0)
    lane8 = lax.broadcasted_iota(I32, (nb * 8, LANES), 1)
    keep = (lane8 < HEAD_DIM) == (r8 % 2 == 0)
    rep = (lax.broadcasted_iota(I32, (nb * 8, nb), 0) // 8 == lax.broadcasted_iota(I32, (nb * 8, nb), 1)).astype(BF16)
    q8 = jnp.zeros((nb * 8, LANES), F32)
    scale = HEAD_DIM ** -0.5
    for j in range(Q_TILES):
        qt = _head_norm_rope(proj[:, j * LANES:(j + 1) * LANES], hmean, qn_ref[...], cos, sin_a, sin_b) * scale
        qrep = jnp.dot(rep, qt.astype(BF16), preferred_element_type=F32)
        q8 = jnp.where(keep & ((r8 % 8) // 2 == j), qrep, q8)
    q8 = q8.astype(BF16)

    sink8 = sink8_ref[:, 0:1]
    ones_bf = jnp.ones((wb, LANES), BF16)
    assert pos >= wb - 1 and wb <= WINDOW
    for b in range(nb):
        kb = nk_ref[b].astype(BF16)
        vb = nv_ref[b].astype(BF16)
        s = lax.dot_general(q8[b * 8:(b + 1) * 8], kb, (((1,), (1,)), ((), ())), preferred_element_type=F32)
        o8[b * 8:(b + 1) * 8, :] = _softmax_pv(s, sink8, vb, ones_bf)
    o8m = jnp.where(keep, o8[...], 0.0).astype(BF16)

    a_tiles = []
    sel_r = lax.broadcasted_iota(I32, (nb, nb * 8), 1)
    sel_b = lax.broadcasted_iota(I32, (nb, nb * 8), 0)
    for j in range(Q_TILES):
        sel = ((sel_r // 8 == sel_b) & ((sel_r % 8) // 2 == j)).astype(BF16)
        a_tiles.append(jnp.dot(sel, o8m, preferred_element_type=F32))

    z_tiles = []
    for gi, w in enumerate(POOL_WINDOWS):
        cols = slice(gi * POOL_GC, (gi + 1) * POOL_GC)
        wsum = u[:, cols]
        for sft in range(1, w):
            wsum = wsum + st_ref[POOL_PREFIX - sft, :, cols]
        d = wsum / float(min(pos + 1, w)) - u[:, cols]
        z_tiles.append(jnp.dot(d.astype(BF16), wpool_ref[gi], preferred_element_type=F32) * pscale_ref[:, cols])
    nst_ref[0:POOL_PREFIX - 1] = st_ref[1:POOL_PREFIX]
    nst_ref[POOL_PREFIX - 1] = u

    mixv = jnp.concatenate(a_tiles + z_tiles, axis=1).astype(BF16)
    x1_ref[...] = x + jnp.dot(mixv, wout_ref[...], preferred_element_type=F32)


def _mixer_sample(x, ck, cv, st, pos, tabs, sink8, g_attn, w_in, qn, kn, hmean, w_pool, pscale, w_out):
    n, wb = ck.shape[0], ck.shape[1]
    nb = SAMPLE_CHUNK
    assert n % nb == 0
    const = lambda shape: pl.BlockSpec(shape, lambda i: (0,) * len(shape))
    cache_blk = pl.BlockSpec((nb, wb, KV_WIDTH), lambda i: (i, 0, 0))
    st_blk = pl.BlockSpec((POOL_PREFIX, nb, POOL_WIDTH), lambda i: (0, i, 0))
    x_blk = pl.BlockSpec((nb, D_MODEL), lambda i: (i, 0))
    return pl.pallas_call(
        functools.partial(_mixer_sample_kernel, pos=pos),
        grid=(n // nb,),
        in_specs=[x_blk, cache_blk, cache_blk, st_blk, const((1, LANES)), const((1, LANES)), const((1, LANES)),
                  const((8, LANES)), const((1, D_MODEL)), const((D_MODEL, IN_WIDTH)), const((1, LANES)),
                  const((1, LANES)), const((LANES, LANES)), const((len(POOL_WINDOWS), POOL_GC, POOL_GC)),
                  const((1, POOL_WIDTH)), const((D_MODEL, D_MODEL))],
        out_specs=[x_blk, cache_blk, cache_blk, st_blk],
        out_shape=[jax.ShapeDtypeStruct((n, D_MODEL), F32), jax.ShapeDtypeStruct(ck.shape, F32),
                   jax.ShapeDtypeStruct(cv.shape, F32), jax.ShapeDtypeStruct(st.shape, F32)],
        scratch_shapes=[pltpu.VMEM((nb * 8, LANES), F32)],
        compiler_params=pltpu.CompilerParams(dimension_semantics=("arbitrary",), vmem_limit_bytes=VMEM_LIMIT),
        name="mixer_sample",
    )(x, ck, cv, st, *tabs, sink8, g_attn, w_in, qn, kn, hmean, w_pool, pscale, w_out)


def _block_plan(cnt, bm, n_lanes):
    e_sub = lax.broadcasted_iota(I32, (N_EXPERTS, LANES), 0)
    e_lane = lax.broadcasted_iota(I32, (N_EXPERTS, LANES), 1)
    padded = jnp.floor((cnt + (bm - 1)) / bm) * bm
    padded_lane = jnp.sum(jnp.where(e_sub == e_lane, padded, 0.0), axis=0, keepdims=True)
    pad_end = jnp.sum(jnp.where(e_lane <= e_sub, padded_lane, 0.0), axis=1, keepdims=True)
    pad_start = pad_end - padded
    blk_start = lax.broadcasted_iota(I32, (N_EXPERTS, n_lanes), 1).astype(F32) * bm
    blk_e = jnp.minimum(jnp.sum((pad_end <= blk_start).astype(F32), axis=0, keepdims=True), N_EXPERTS - 1.0)
    mine = lax.broadcasted_iota(I32, (N_EXPERTS, n_lanes), 0).astype(F32) == blk_e
    last = jnp.sum(jnp.where(mine, pad_start + cnt, 0.0), axis=0, keepdims=True)
    blk_valid = jnp.clip(last - blk_start[0:1], 0.0, float(bm))
    e_blk = lax.broadcasted_iota(I32, (N_EXPERTS, n_lanes), 0).astype(F32)
    later = jnp.min(jnp.where((e_blk > blk_e) & (cnt > 0.0), e_blk, float(N_EXPERTS)), axis=0, keepdims=True)
    blk_next = jnp.where(later < N_EXPERTS, later, -1.0)
    return pad_start, blk_e, blk_valid, blk_next


def _route_kernel(x1_ref, g_ref, wr_ref, br_ref, cnt_in_ref, hf_ref, idx_ref, gate_ref, pos_ref, cnt_ref,
                  pstart_ref, blke_ref, blkv_ref, blkn_ref, counts, *, bm):
    i = pl.program_id(0)
    rows = x1_ref.shape[0]

    @pl.when(i == 0)
    def _():
        counts[...] = cnt_in_ref[...]

    h = _rms(x1_ref[...], g_ref[...])
    hf_ref[...] = _pack_halves(h)

    logits = lax.dot_general(wr_ref[...], h, (((1,), (1,)), ((), ())), preferred_element_type=F32,
                             precision=lax.Precision.HIGHEST) + br_ref[...]
    eid = lax.broadcasted_iota(I32, (N_EXPERTS, rows), 0).astype(F32)
    work = logits
    vals, hots = [], []
    for kk in range(TOP_K):
        m = jnp.max(work, axis=0, keepdims=True)
        first = jnp.min(jnp.where(work == m, eid, float(N_EXPERTS)), axis=0, keepdims=True)
        hot = eid == first
        work = jnp.where(hot, -jnp.inf, work)
        vals.append(m)
        hots.append(hot)
        idx_ref[kk:kk + 1, :] = first.astype(I32)
    es = [jnp.exp(vv - vals[0]) for vv in vals]
    den = es[0] + es[1] + es[2] + es[3]
    for kk in range(TOP_K):
        gate_ref[kk:kk + 1, :] = es[kk] / den

    chosen = hots[0] | hots[1] | hots[2] | hots[3]
    before = (lax.broadcasted_iota(I32, (rows, rows), 0) < lax.broadcasted_iota(I32, (rows, rows), 1)).astype(BF16)
    rank = jnp.dot(chosen.astype(BF16), before, preferred_element_type=F32) + counts[...]
    for kk in range(TOP_K):
        pos_ref[kk:kk + 1, :] = jnp.sum(jnp.where(hots[kk], rank, 0.0), axis=0, keepdims=True).astype(I32)
    counts[...] = counts[...] + jnp.sum(chosen.astype(F32), axis=1, keepdims=True)
    cnt_ref[...] = counts[...]

    @pl.when(i == pl.num_programs(0) - 1)
    def _():
        pad_start, blk_e, blk_valid, blk_next = _block_plan(counts[...], bm, blke_ref.shape[1])
        pstart_ref[...] = pad_start.astype(I32)
        blke_ref[...] = blk_e.astype(I32)
        blkv_ref[...] = blk_valid.astype(I32)
        blkn_ref[...] = blk_next.astype(I32)


def _route(x1, g_ffn, wr_t, br, cnt_in, rows, bm, n_blocks):
    n = x1.shape[0]
    assert n % rows == 0
    nb_lanes = -(-n_blocks // LANES) * LANES
    const = lambda shape: pl.BlockSpec(shape, lambda i: (0,) * len(shape))
    tok_blk = pl.BlockSpec((TOP_K, rows), lambda i: (0, i))
    return pl.pallas_call(
        functools.partial(_route_kernel, bm=bm),
        grid=(n // rows,),
        in_specs=[pl.BlockSpec((rows, D_MODEL), lambda i: (i, 0)), const((1, D_MODEL)), const((N_EXPERTS, D_MODEL)),
                  const((N_EXPERTS, 1)), const((N_EXPERTS, 1))],
        out_specs=[pl.BlockSpec((rows, D_MODEL // 2), lambda i: (i, 0)), tok_blk, tok_blk, tok_blk,
                   const((N_EXPERTS, 1)), const((N_EXPERTS, 1)), const((1, nb_lanes)), const((1, nb_lanes)),
                   const((1, nb_lanes))],
        out_shape=[jax.ShapeDtypeStruct((n, D_MODEL // 2), U32), jax.ShapeDtypeStruct((TOP_K, n), I32),
                   jax.ShapeDtypeStruct((TOP_K, n), F32), jax.ShapeDtypeStruct((TOP_K, n), I32),
                   jax.ShapeDtypeStruct((N_EXPERTS, 1), F32), jax.ShapeDtypeStruct((N_EXPERTS, 1), I32),
                   jax.ShapeDtypeStruct((1, nb_lanes), I32), jax.ShapeDtypeStruct((1, nb_lanes), I32),
                   jax.ShapeDtypeStruct((1, nb_lanes), I32)],
        scratch_shapes=[pltpu.VMEM((N_EXPERTS, 1), F32)],
        compiler_params=pltpu.CompilerParams(dimension_semantics=("arbitrary",), vmem_limit_bytes=VMEM_LIMIT),
        name="route",
    )(x1, g_ffn, wr_t, br, cnt_in)


def _sc_mesh():
    return plsc.VectorSubcoreMesh(core_axis_name="core", subcore_axis_name="subcore")


def _sc_worker_id():
    return lax.axis_index("core") * SC_SUBCORES + lax.axis_index("subcore")


def _scatter_rows(xa, xb, dest, n_rows):
    ch = SC_CHUNK
    na, w = xa.shape
    n = na + xb.shape[0]
    nk = dest.shape[0]
    assert na % ch == 0 and n % ch == 0 and dest.shape[1] == n and xb.shape[1] == w and xa.dtype == xb.dtype
    n_chunks = n // ch

    @pl.kernel(out_type=jax.ShapeDtypeStruct((n_rows, w), xa.dtype), mesh=_sc_mesh(),
               scratch_types=[pltpu.VMEM((ch,), I32), pltpu.VMEM((ch, w), xa.dtype)])
    def scatter_kernel(xa_hbm, xb_hbm, d_hbm, o_hbm, idx_v, buf):
        wid = _sc_worker_id()

        @pl.loop(0, -(-n_chunks // SC_WORKERS))
        def _(j):
            c = j * SC_WORKERS + wid

            @pl.when(c < na // ch)
            def _():
                pltpu.sync_copy(xa_hbm.at[pl.ds(c * ch, ch)], buf)

            @pl.when((c >= na // ch) & (c < n_chunks))
            def _():
                pltpu.sync_copy(xb_hbm.at[pl.ds(c * ch - na, ch)], buf)

            @pl.when(c < n_chunks)
            def _():
                for kk in range(nk):
                    pltpu.sync_copy(d_hbm.at[pl.ds(kk * n + c * ch, ch)], idx_v)
                    pltpu.sync_copy(buf, o_hbm.at[idx_v])

    return scatter_kernel(xa, xb, dest.reshape(-1))


def _gather_rows(src, idx):
    ch = SC_CHUNK
    m = idx.shape[0]
    w = src.shape[1]
    per = m // SC_WORKERS
    n_pairs = per // (2 * ch)
    assert m % SC_WORKERS == 0 and per % (2 * ch) == 0
    dma = pltpu.SemaphoreType.DMA

    @pl.kernel(out_type=jax.ShapeDtypeStruct((m, w), src.dtype), mesh=_sc_mesh(),
               scratch_types=[pltpu.VMEM((per,), I32), pltpu.VMEM((ch, w), src.dtype), pltpu.VMEM((ch, w), src.dtype),
                              dma, dma, dma, dma])
    def gather_kernel(s_hbm, i_hbm, o_hbm, idx_v, buf_a, buf_b, sem_ga, sem_gb, sem_wa, sem_wb):
        base = _sc_worker_id() * per
        pltpu.sync_copy(i_hbm.at[pl.ds(base, per)], idx_v)

        def fetch(j, buf, sem):
            return pltpu.make_async_copy(s_hbm.at[idx_v.at[pl.ds(j * ch, ch)]], buf, sem)

        def put(j, buf, sem):
            return pltpu.make_async_copy(buf, o_hbm.at[pl.ds(base + j * ch, ch)], sem)

        fetch(0, buf_a, sem_ga).start()

        @pl.loop(0, n_pairs)
        def _(p):
            j0 = 2 * p
            j1 = j0 + 1

            @pl.when(p > 0)
            def _():
                put(j1 - 2, buf_b, sem_wb).wait()

            fetch(j1, buf_b, sem_gb).start()
            fetch(j0, buf_a, sem_ga).wait()
            put(j0, buf_a, sem_wa).start()
            fetch(j1, buf_b, sem_gb).wait()
            put(j1, buf_b, sem_wb).start()
            put(j0, buf_a, sem_wa).wait()

            @pl.when(p + 1 < n_pairs)
            def _():
                fetch(j0 + 2, buf_a, sem_ga).start()

        put(2 * n_pairs - 1, buf_b, sem_wb).wait()

    return gather_kernel(src, idx)


def _moe_kernel(blk_e_ref, blk_valid_ref, blk_next_ref, xs_ref, wgu_hbm, bgu_ref, wd_hbm, bd_ref, y_ref,
                wgu_f32, wd_f32, wgu_bf, wd_bf, sems, *, e0):
    i = pl.program_id(0)
    e = blk_e_ref[i]
    n_valid = blk_valid_ref[i]
    used = n_valid > 0
    new_expert = (i == 0) | (blk_e_ref[jnp.maximum(i - 1, 0)] != e)

    def weight_copies(expert):
        return (pltpu.make_async_copy(wgu_hbm.at[e0 + expert], wgu_f32, sems.at[0]),
                pltpu.make_async_copy(wd_hbm.at[e0 + expert], wd_f32, sems.at[1]))

    @pl.when(used & (i == 0))
    def _():
        for cp in weight_copies(e):
            cp.start()

    @pl.when(used & new_expert)
    def _():
        for cp in weight_copies(e):
            cp.wait()
        wgu_bf[...] = wgu_f32[...].astype(BF16)
        wd_bf[...] = wd_f32[...].astype(BF16)
        nxt = blk_next_ref[i]

        @pl.when(nxt >= 0)
        def _():
            for cp in weight_copies(nxt):
                cp.start()

    @pl.when(used)
    def _():
        valid = lax.broadcasted_iota(I32, xs_ref.shape, 0) < n_valid
        xb = _unpack_halves(jnp.where(valid, xs_ref[...], jnp.uint32(0))).astype(BF16)
        gu = jnp.dot(xb, wgu_bf[...], preferred_element_type=F32) + bgu_ref[0]
        g = jnp.minimum(gu[:, :D_FF], SWIGLU_LIMIT)
        up = jnp.clip(gu[:, D_FF:], -SWIGLU_LIMIT, SWIGLU_LIMIT)
        act = (up + 1.0) * (g * jax.nn.sigmoid(SWIGLU_ALPHA * g))
        y_ref[...] = _pack_halves(jnp.dot(act.astype(BF16), wd_bf[...], preferred_element_type=F32) + bd_ref[0])

    @pl.when(jnp.logical_not(used))
    def _():
        y_ref[...] = jnp.zeros_like(y_ref)


def _moe(xs, blk_e, blk_valid, blk_next, layer, w_gu, b_gu, w_d, b_d):
    n_rows = xs.shape[0]
    bm = MOE_BM
    assert n_rows % bm == 0
    e0 = layer * N_EXPERTS
    n_we = w_gu.shape[0] * w_gu.shape[1]
    any_spec = pl.BlockSpec(memory_space=pl.ANY)
    grid_spec = pltpu.PrefetchScalarGridSpec(
        num_scalar_prefetch=3,
        grid=(n_rows // bm,),
        in_specs=[pl.BlockSpec((bm, D_MODEL // 2), lambda i, be, bv, bn: (i, 0)),
                  any_spec,
                  pl.BlockSpec((1, 1, 2 * D_FF), lambda i, be, bv, bn: (e0 + be[i], 0, 0)),
                  any_spec,
                  pl.BlockSpec((1, 1, D_MODEL), lambda i, be, bv, bn: (e0 + be[i], 0, 0))],
        out_specs=pl.BlockSpec((bm, D_MODEL // 2), lambda i, be, bv, bn: (i, 0)),
        scratch_shapes=[pltpu.VMEM((D_MODEL, 2 * D_FF), F32), pltpu.VMEM((D_FF, D_MODEL), F32),
                        pltpu.VMEM((D_MODEL, 2 * D_FF), BF16), pltpu.VMEM((D_FF, D_MODEL), BF16),
                        pltpu.SemaphoreType.DMA((2,))],
    )
    return pl.pallas_call(
        functools.partial(_moe_kernel, e0=e0),
        grid_spec=grid_spec,
        out_shape=jax.ShapeDtypeStruct((n_rows, D_MODEL // 2), U32),
        compiler_params=pltpu.CompilerParams(dimension_semantics=("arbitrary",), vmem_limit_bytes=VMEM_LIMIT),
        name="moe_experts",
    )(blk_e, blk_valid, blk_next, xs, w_gu.reshape(n_we, D_MODEL, 2 * D_FF), b_gu.reshape(n_we, 1, 2 * D_FF),
      w_d.reshape(n_we, D_FF, D_MODEL), b_d.reshape(n_we, 1, D_MODEL))


def _ple_kernel(x1_ref, y0_ref, y1_ref, y2_ref, y3_ref, gates_ref, p_ref, g_ref, wg_ref, wp_ref, out_ref):
    x2 = x1_ref[...]
    gates = gates_ref[...]
    for kk, y_ref in enumerate((y0_ref, y1_ref, y2_ref, y3_ref)):
        x2 = x2 + _unpack_halves(y_ref[...]) * gates[:, kk:kk + 1]
    hp = _rms(x2, g_ref[...]).astype(BF16)
    gate = jax.nn.sigmoid(jnp.dot(hp, wg_ref[...], preferred_element_type=F32))
    pp = jnp.dot(p_ref[...].astype(BF16), wp_ref[...], preferred_element_type=F32)
    out_ref[...] = x2 + gate * pp


def _ple(x1, y_tok, y0, gates_t, tok0, p_all, p0, g_ple, w_gate, w_proj, rows):
    n = x1.shape[0]
    assert n % rows == 0 and tok0 % rows == 0 and y0 % rows == 0 and p0 % rows == 0
    const = lambda shape: pl.BlockSpec(shape, lambda i: (0,) * len(shape))
    y_blk = lambda kk: pl.BlockSpec((rows, D_MODEL // 2), lambda i: ((y0 + kk * n) // rows + i, 0))
    return pl.pallas_call(
        _ple_kernel,
        grid=(n // rows,),
        in_specs=[pl.BlockSpec((rows, D_MODEL), lambda i: (i, 0)), y_blk(0), y_blk(1), y_blk(2), y_blk(3),
                  pl.BlockSpec((rows, TOP_K), lambda i: (tok0 // rows + i, 0)),
                  pl.BlockSpec((rows, PLE_DIM), lambda i: (p0 // rows + i, 0)),
                  const((1, D_MODEL)), const((D_MODEL, D_MODEL)), const((PLE_DIM, D_MODEL))],
        out_specs=pl.BlockSpec((rows, D_MODEL), lambda i: (i, 0)),
        out_shape=jax.ShapeDtypeStruct((n, D_MODEL), F32),
        compiler_params=pltpu.CompilerParams(dimension_semantics=("arbitrary",), vmem_limit_bytes=VMEM_LIMIT),
        name="combine_ple",
    )(x1, y_tok, y_tok, y_tok, y_tok, gates_t, p_all, g_ple, w_gate, w_proj)


def _rope_tables(pos):
    half = ROT_DIM // 2
    d = np.arange(LANES) % HEAD_DIM
    inv = ROPE_THETA ** (-jnp.arange(half, dtype=F32) / half)
    inv_lane = jnp.where(d < ROT_DIM, inv[d % half], 0.0)
    ang = pos.astype(F32)[:, None] * inv_lane[None, :]
    cos, sin = jnp.cos(ang), jnp.sin(ang)
    sin_a = jnp.where((d >= half) & (d < ROT_DIM), sin, 0.0)
    sin_b = jnp.where(d < half, -sin, 0.0)
    return cos, sin_a, sin_b


def _layer(layer, xp, xs, ck, cv, st, p_prompt_all, p_sample_all, past_len, refine_tail, tabs_p, tabs_s,
           norm_attn, w_in, q_norm, k_norm, attn_sinks, w_pool, pool_scale, w_out,
           norm_ffn, w_router, b_router, w_gate_up_all, b_gate_up_all, w_down_all, b_down_all,
           norm_ple, w_ple_gate, w_ple_proj):
    t, ns = xp.shape[0], xs.shape[0]
    w_q = w_in[:, :ATTN_WIDTH].reshape(D_MODEL, 2, Q_TILES, HEAD_DIM).transpose(0, 2, 1, 3).reshape(D_MODEL, ATTN_WIDTH)
    w_in_p = jnp.concatenate([w_q, w_in[:, ATTN_WIDTH:]], axis=1)
    w_oa = w_out[:ATTN_WIDTH].reshape(2, Q_TILES, HEAD_DIM, D_MODEL).transpose(1, 0, 2, 3).reshape(ATTN_WIDTH, D_MODEL)
    w_out_p = jnp.concatenate([w_oa, w_out[ATTN_WIDTH:]], axis=0)
    g_attn = norm_attn.reshape(1, D_MODEL)
    qn = jnp.tile(q_norm, 2).reshape(1, LANES)
    kn = jnp.tile(k_norm, 2).reshape(1, LANES)
    lane = np.arange(LANES)
    hmean = jnp.asarray((lane[:, None] // HEAD_DIM == lane[None, :] // HEAD_DIM) / HEAD_DIM, F32)
    pscale = pool_scale.reshape(1, POOL_WIDTH)
    sink8 = jnp.broadcast_to(attn_sinks.reshape(2, Q_TILES).T.reshape(8, 1), (8, LANES))

    mix_args = (g_attn, w_in_p.astype(BF16), qn, kn, hmean.astype(BF16), w_pool.astype(BF16), pscale,
                w_out_p.astype(BF16))
    x1p, nk_p, nv_p, nu_p = _mixer_prompt(xp, 0, attn_sinks, tabs_p, *mix_args)
    if refine_tail:
        hi_args = (g_attn, w_in_p, qn, kn, hmean, w_pool, pscale, w_out_p)
        x1p, nk_p, nv_p, nu_p = _mixer_prompt(xp, t - 2 * MIX_ROWS, attn_sinks, tabs_p, *hi_args, x1_into=x1p)
    st_t = jnp.transpose(st, (1, 0, 2))
    x1s, nk_s, nv_s, nst_t = _mixer_sample(xs, ck.reshape(ns, -1, KV_WIDTH), cv.reshape(ns, -1, KV_WIDTH), st_t,
                                           past_len, tabs_s, sink8, *mix_args)

    g_ffn = norm_ffn.reshape(1, D_MODEL)
    wr_t = w_router.T
    br = b_router.reshape(N_EXPERTS, 1)
    n_tok = t + ns
    bm = MOE_BM
    n_blocks = -(-(n_tok * TOP_K + N_EXPERTS * (bm - 1)) // bm)
    hf_p, idx_p, gate_p, pos_p, cnt_p = _route(x1p, g_ffn, wr_t, br, jnp.zeros((N_EXPERTS, 1), F32),
                                               ROUTE_ROWS, bm, n_blocks)[:5]
    hf_s, idx_s, gate_s, pos_s, _, pad_start, blk_e, blk_valid, blk_next = _route(x1s, g_ffn, wr_t, br, cnt_p, ns,
                                                                                  bm, n_blocks)
    idx = jnp.concatenate([idx_p, idx_s], axis=1)
    pos = jnp.concatenate([pos_p, pos_s], axis=1)
    blk_e, blk_valid, blk_next = (a[0, :n_blocks] for a in (blk_e, blk_valid, blk_next))
    start_of = jnp.sum(jnp.where(idx[None] == jnp.arange(N_EXPERTS, dtype=I32)[:, None, None],
                                 pad_start.reshape(N_EXPERTS, 1, 1), 0), axis=0)
    dest = start_of + pos

    xs_rows = _scatter_rows(hf_p, hf_s, dest, n_blocks * bm)
    y = _moe(xs_rows, blk_e, blk_valid, blk_next, layer, w_gate_up_all, b_gate_up_all, w_down_all, b_down_all)

    back = jnp.concatenate([dest[:, :t].reshape(-1), dest[:, t:].reshape(-1)])
    unit = SC_WORKERS * SC_CHUNK * 2
    back = jnp.concatenate([back, jnp.arange(-back.shape[0] % unit, dtype=I32)])
    y_tok = _gather_rows(y, back)

    g_ple = norm_ple.reshape(1, D_MODEL)
    wg_bf = w_ple_gate.astype(BF16)
    wp_bf = w_ple_proj.astype(BF16)
    yp = _ple(x1p, y_tok, 0, gate_p.T, 0, p_prompt_all, layer * t, g_ple, wg_bf, wp_bf, PLE_ROWS)
    ys = _ple(x1s, y_tok, TOP_K * t, gate_s.T, 0, p_sample_all, layer * ns, g_ple, wg_bf, wp_bf, ns)
    new_pool_s = jnp.transpose(nst_t, (1, 0, 2))
    return yp, ys, nk_p, nv_p, nu_p[POOL_HALO - POOL_PREFIX:], nk_s, nv_s, new_pool_s


def kernel(x_prompt, x_sample, cache_k, cache_v, state_pool, p_prompt, p_sample, norm_attn, w_in, q_norm, k_norm,
           attn_sinks, w_pool, pool_scale, w_out, norm_ffn, w_router, b_router, w_gate_up, b_gate_up, w_down, b_down,
           norm_ple, w_ple_gate, w_ple_proj):
    depth = norm_attn.shape[0]
    batch, seq, d = x_prompt.shape
    ns, dec_seq, _ = x_sample.shape
    wb = cache_k.shape[2]
    assert batch == 1 and dec_seq == 1 and d == D_MODEL and wb == WINDOW
    assert cache_k.shape[3:] == (N_KV_HEADS, HEAD_DIM) and state_pool.shape[2:] == (POOL_PREFIX, POOL_WIDTH)
    past_len = PAST_LEN
    yp = x_prompt.reshape(seq, d)
    ys = x_sample.reshape(ns, d)
    p_prompt_all = p_prompt.reshape(depth * seq, PLE_DIM)
    p_sample_all = p_sample.reshape(depth * ns, PLE_DIM)
    tabs_p = _rope_tables(jnp.arange(seq))
    tabs_s = _rope_tables(jnp.full((1,), past_len))
    outs = [[] for _ in range(6)]
    for i in range(depth):
        res = _layer(i, yp, ys, cache_k[i], cache_v[i], state_pool[i], p_prompt_all, p_sample_all, past_len,
                     i < depth - 1, tabs_p, tabs_s,
                     norm_attn[i], w_in[i], q_norm[i], k_norm[i], attn_sinks[i], w_pool[i], pool_scale[i], w_out[i],
                     norm_ffn[i], w_router[i], b_router[i], w_gate_up, b_gate_up, w_down, b_down,
                     norm_ple[i], w_ple_gate[i], w_ple_proj[i])
        yp, ys = res[0], res[1]
        kv_shape = (1, WINDOW, N_KV_HEADS, HEAD_DIM)
        outs[0].append(res[2].reshape(kv_shape))
        outs[1].append(res[3].reshape(kv_shape))
        outs[2].append(res[4].reshape(1, POOL_PREFIX, POOL_WIDTH))
        outs[3].append(res[5].reshape(ns, wb, N_KV_HEADS, HEAD_DIM))
        outs[4].append(res[6].reshape(ns, wb, N_KV_HEADS, HEAD_DIM))
        outs[5].append(res[7])
    return (yp.reshape(batch, seq, d), ys.reshape(ns, dec_seq, d)) + tuple(jnp.stack(o) for o in outs)
```

```python
import functools

import jax
import jax.numpy as jnp
import numpy as np
from jax import lax
from jax.experimental import pallas as pl
from jax.experimental.pallas import tpu as pltpu
from jax.experimental.pallas import tpu_sc as plsc

F32 = jnp.float32
BF16 = jnp.bfloat16
U32 = jnp.uint32
I32 = jnp.int32

D_MODEL = 1024
HEAD_DIM = 64
N_HEADS = 8
N_KV_HEADS = 2
GROUP = N_HEADS // N_KV_HEADS
ATTN_WIDTH = N_HEADS * HEAD_DIM
KV_WIDTH = N_KV_HEADS * HEAD_DIM
POOL_WIDTH = 512
POOL_WINDOWS = (2, 4, 8, 16)
POOL_GC = POOL_WIDTH // len(POOL_WINDOWS)
POOL_PREFIX = max(POOL_WINDOWS) - 1
POOL_HALO = POOL_PREFIX + 1
IN_WIDTH = ATTN_WIDTH + 2 * KV_WIDTH + POOL_WIDTH
WINDOW = 128
ROPE_THETA = 500000.0
ROT_DIM = HEAD_DIM // 4
N_EXPERTS = 32
TOP_K = 4
D_FF = 1024
SWIGLU_ALPHA = 1.702
SWIGLU_LIMIT = 7.0
PLE_DIM = 256
PAST_LEN = 16384
EPS = 1e-5
NEG_INF = -1e30

LANES = 128
Q_TILES = ATTN_WIDTH // LANES

MIX_ROWS = 512
MIX_TAIL_ROWS = 256
SAMPLE_CHUNK = 16
ROUTE_ROWS = 512
MOE_BM = 512
SC_CORES = 2
SC_SUBCORES = 16
SC_WORKERS = SC_CORES * SC_SUBCORES
SC_CHUNK = 64
PLE_ROWS = 512
VMEM_LIMIT = 56 * 1024 * 1024


def _rms(x, g):
    return x * lax.rsqrt(jnp.mean(x * x, axis=-1, keepdims=True) + EPS) * g


def _pack_halves(x):
    w = x.shape[1] // 2
    lo = lax.bitcast_convert_type(x[:, :w].astype(BF16).astype(F32), U32) >> 16
    hi = lax.bitcast_convert_type(x[:, w:].astype(BF16).astype(F32), U32) & jnp.uint32(0xFFFF0000)
    return lo | hi


def _unpack_halves(packed):
    lo = lax.bitcast_convert_type(packed << 16, F32)
    hi = lax.bitcast_convert_type(packed & jnp.uint32(0xFFFF0000), F32)
    return jnp.concatenate([lo, hi], axis=1)


def _mm(a, b, nt=False):
    dims = (((1,), (1 if nt else 0,)), ((), ()))
    if b.dtype == F32:
        return lax.dot_general(a.astype(F32), b, dims, preferred_element_type=F32, precision=lax.Precision.HIGHEST)
    return lax.dot_general(a.astype(BF16), b, dims, preferred_element_type=F32)


def _head_norm_rope(t, hmean, gain, cos, sin_a, sin_b):
    t = t * lax.rsqrt(_mm(t * t, hmean) + EPS) * gain
    return t * cos + pltpu.roll(t, ROT_DIM // 2, axis=1) * sin_a + pltpu.roll(t, LANES - ROT_DIM // 2, axis=1) * sin_b


def _softmax_pv(s, sink, v, ones):
    m = jnp.maximum(jnp.max(s, axis=-1, keepdims=True), sink)
    e = jnp.exp(s - m).astype(v.dtype)
    den = _mm(e, ones) + jnp.exp(sink - m)
    return _mm(e, v) / den


def _mixer_prompt_kernel(sinks_ref, x_ref, cos_ref, sa_ref, sb_ref, g_ref, win_ref, qn_ref, kn_ref, hm_ref,
                         wpool_ref, pscale_ref, wout_ref, *rest, row_offset, aliased):
    x1_ref, klast_ref, vlast_ref, ulast_ref, kprev, vprev, uext, mix = rest[1:] if aliased else rest
    i = pl.program_id(0)
    rows = x_ref.shape[0]
    n_sub = rows // WINDOW
    cdt = win_ref.dtype
    row0 = row_offset + i * rows

    @pl.when(i == 0)
    def _():
        kprev[...] = jnp.zeros_like(kprev)
        vprev[...] = jnp.zeros_like(vprev)
        uext[0:POOL_HALO, :] = jnp.zeros((POOL_HALO, POOL_WIDTH), F32)

    x = x_ref[...]
    proj = _mm(_rms(x, g_ref[...]), win_ref[...])
    cos, sin_a, sin_b = cos_ref[...], sa_ref[...], sb_ref[...]

    n_t = Q_TILES + 1
    t_all = jnp.concatenate([proj[:, j * LANES:(j + 1) * LANES] for j in range(n_t)], axis=0)
    t3 = (t_all * lax.rsqrt(_mm(t_all * t_all, hm_ref[...]) + EPS)).reshape(n_t, rows, LANES)
    t3 = jnp.concatenate([t3[:Q_TILES] * (qn_ref[...] * HEAD_DIM ** -0.5), t3[Q_TILES:] * kn_ref[...]], axis=0)
    t2 = t3.reshape(n_t * rows, LANES)
    t3 = (t3 * cos + pltpu.roll(t2, ROT_DIM // 2, axis=1).reshape(n_t, rows, LANES) * sin_a
          + pltpu.roll(t2, LANES - ROT_DIM // 2, axis=1).reshape(n_t, rows, LANES) * sin_b)
    q3 = t3[:Q_TILES]
    k = t3[Q_TILES]
    v = proj[:, ATTN_WIDTH + KV_WIDTH:ATTN_WIDTH + 2 * KV_WIDTH]
    u = proj[:, ATTN_WIDTH + 2 * KV_WIDTH:]
    klast_ref[...] = k[rows - WINDOW:, :]
    vlast_ref[...] = v[rows - WINDOW:, :]
    ulast_ref[...] = u[rows - POOL_HALO:, :]
    k_c = k.astype(cdt)
    v_c = jnp.concatenate([v.astype(cdt), jnp.ones((rows, LANES), cdt)], axis=1)
    v_first = jnp.concatenate([vprev[...], jnp.ones((WINDOW, LANES), cdt)], axis=1)

    lane = lax.broadcasted_iota(I32, (WINDOW, LANES), 1)
    left = (lane < HEAD_DIM)[None]
    qi = lax.broadcasted_iota(I32, (WINDOW, 2 * WINDOW), 0)
    kj = lax.broadcasted_iota(I32, (WINDOW, 2 * WINDOW), 1)
    band = (kj - qi >= 1) & (kj - qi <= WINDOW)
    sink3 = jnp.concatenate([jnp.full((1, 1, 1), sinks_ref[j + Q_TILES * s], F32)
                             for j in range(Q_TILES) for s in range(2)], axis=0)
    n_g = 2 * Q_TILES

    v_cats, masks, scores = [], [], []
    for c in range(n_sub):
        r0 = c * WINDOW
        if c == 0:
            k_cat = jnp.concatenate([kprev[...], k_c[0:WINDOW]], axis=0)
            v_cats.append(jnp.concatenate([v_first, v_c[0:WINDOW]], axis=0))
            masks.append(band & (kj + (row0 - WINDOW) >= 0))
        else:
            k_cat = k_c[r0 - WINDOW:r0 + WINDOW]
            v_cats.append(v_c[r0 - WINDOW:r0 + WINDOW])
            masks.append(band)
        q_c = q3[:, r0:r0 + WINDOW, :]
        q_all = jnp.concatenate([jnp.where(left, q_c, 0.0), jnp.where(left, 0.0, q_c)], axis=1)
        scores.append(_mm(q_all.reshape(n_g * WINDOW, LANES), k_cat, nt=True))
    kprev[...] = k_c[rows - WINDOW:]
    vprev[...] = v[rows - WINDOW:].astype(cdt)

    probs, maxes = [], []
    for c in range(n_sub):
        s = jnp.where(masks[c][None], scores[c].reshape(n_g, WINDOW, 2 * WINDOW), NEG_INF)
        m = jnp.maximum(jnp.max(s, axis=-1, keepdims=True), sink3)
        probs.append(jnp.exp(s - m).astype(cdt).reshape(n_g * WINDOW, 2 * WINDOW))
        maxes.append(m)
    applied = [_mm(probs[c], v_cats[c]) for c in range(n_sub)]
    for c in range(n_sub):
        r0 = c * WINDOW
        den = applied[c][:, LANES:] + jnp.exp(sink3 - maxes[c]).reshape(n_g * WINDOW, 1)
        o = (applied[c][:, :LANES] / den).reshape(Q_TILES, 2 * WINDOW, LANES)
        a = jnp.where(left, o[:, :WINDOW], o[:, WINDOW:])
        for j in range(Q_TILES):
            mix[r0:r0 + WINDOW, j * LANES:(j + 1) * LANES] = a[j].astype(cdt)

    uext[POOL_HALO:POOL_HALO + rows, :] = u
    pos1 = (lax.broadcasted_iota(I32, (rows, 1), 0) + row0 + 1).astype(F32)
    for gi, w in enumerate(POOL_WINDOWS):
        cols = slice(gi * POOL_GC, (gi + 1) * POOL_GC)
        wsum = u[:, cols]
        for sft in range(1, w):
            wsum = wsum + uext[POOL_HALO - sft:POOL_HALO - sft + rows, cols]
        d = wsum / jnp.minimum(pos1, float(w)) - u[:, cols]
        y = _mm(d, wpool_ref[gi]) * pscale_ref[:, cols]
        mix[:, ATTN_WIDTH + gi * POOL_GC:ATTN_WIDTH + (gi + 1) * POOL_GC] = y.astype(cdt)
    uext[0:POOL_HALO, :] = u[rows - POOL_HALO:, :]

    x1_ref[...] = x + _mm(mix[...], wout_ref[...])


def _mixer_prompt(x_full, row_offset, rows, sinks, tabs, g_attn, w_in, qn, kn, hmean, w_pool, pscale, w_out,
                  x1_into=None):
    t = x_full.shape[0] - row_offset
    cdt = w_in.dtype
    assert t % rows == 0 and row_offset % rows == 0 and rows % WINDOW == 0 and rows >= POOL_HALO
    blk0, n_steps = row_offset // rows, t // rows
    const = lambda shape: pl.BlockSpec(shape, lambda i, *_: (0,) * len(shape))
    row_blk = lambda width: pl.BlockSpec((rows, width), lambda i, *_: (blk0 + i, 0))
    aliased = x1_into is not None
    if aliased:
        assert x1_into.shape == x_full.shape
        x1_spec = pl.BlockSpec((rows, D_MODEL), lambda i, *_: (blk0 + n_steps - 1, 0))
        x1_shape = x1_into.shape
        extra_specs, extra_args, aliases = [pl.BlockSpec(memory_space=pl.ANY)], [x1_into], {13: 0}
    else:
        x1_spec = pl.BlockSpec((rows, D_MODEL), lambda i, *_: (i, 0))
        x1_shape = (t, D_MODEL)
        extra_specs, extra_args, aliases = [], [], {}
    grid_spec = pltpu.PrefetchScalarGridSpec(
        num_scalar_prefetch=1,
        grid=(n_steps,),
        in_specs=[row_blk(D_MODEL), row_blk(LANES), row_blk(LANES), row_blk(LANES),
                  const((1, D_MODEL)), const((D_MODEL, IN_WIDTH)), const((1, LANES)), const((1, LANES)),
                  const((LANES, LANES)), const((len(POOL_WINDOWS), POOL_GC, POOL_GC)), const((1, POOL_WIDTH)),
                  const((D_MODEL, D_MODEL))] + extra_specs,
        out_specs=[x1_spec, const((WINDOW, KV_WIDTH)), const((WINDOW, KV_WIDTH)),
                   const((POOL_HALO, POOL_WIDTH))],
        scratch_shapes=[pltpu.VMEM((WINDOW, KV_WIDTH), cdt), pltpu.VMEM((WINDOW, KV_WIDTH), cdt),
                        pltpu.VMEM((POOL_HALO + rows, POOL_WIDTH), F32), pltpu.VMEM((rows, D_MODEL), cdt)],
    )
    return pl.pallas_call(
        functools.partial(_mixer_prompt_kernel, row_offset=row_offset, aliased=aliased),
        grid_spec=grid_spec,
        out_shape=[jax.ShapeDtypeStruct(x1_shape, F32), jax.ShapeDtypeStruct((WINDOW, KV_WIDTH), F32),
                   jax.ShapeDtypeStruct((WINDOW, KV_WIDTH), F32), jax.ShapeDtypeStruct((POOL_HALO, POOL_WIDTH), F32)],
        input_output_aliases=aliases,
        compiler_params=pltpu.CompilerParams(dimension_semantics=("arbitrary",), vmem_limit_bytes=VMEM_LIMIT),
        name="mixer_prompt",
    )(sinks, x_full, *tabs, g_attn, w_in, qn, kn, hmean, w_pool, pscale, w_out, *extra_args)


def _mixer_sample_kernel(x_ref, ck_ref, cv_ref, st_ref, cos_ref, sa_ref, sb_ref, sink8_ref, g_ref, win_ref, qn_ref,
                         kn_ref, hm_ref, wpool_ref, pscale_ref, wout_ref,
                         x1_ref, nk_ref, nv_ref, nst_ref, o8, *, pos):
    nb = x_ref.shape[0]
    wb = ck_ref.shape[1]
    x = x_ref[...]
    h = _rms(x, g_ref[...]).astype(BF16)
    proj = jnp.dot(h, win_ref[...], preferred_element_type=F32)
    cos, sin_a, sin_b = cos_ref[...], sa_ref[...], sb_ref[...]
    hmean = hm_ref[...]
    k = _head_norm_rope(proj[:, ATTN_WIDTH:ATTN_WIDTH + KV_WIDTH], hmean, kn_ref[...], cos, sin_a, sin_b)
    v = proj[:, ATTN_WIDTH + KV_WIDTH:ATTN_WIDTH + 2 * KV_WIDTH]
    u = proj[:, ATTN_WIDTH + 2 * KV_WIDTH:]

    nk_ref[:, 0:wb - 1, :] = ck_ref[:, 1:wb, :]
    nv_ref[:, 0:wb - 1, :] = cv_ref[:, 1:wb, :]
    for b in range(nb):
        nk_ref[b, wb - 1:wb, :] = k[b:b + 1, :]
        nv_ref[b, wb - 1:wb, :] = v[b:b + 1, :]

    r8 = lax.broadcasted_iota(I32, (nb * 8, LANES), 0)
    lane8 = lax.broadcasted_iota(I32, (nb * 8, LANES), 1)
    keep = (lane8 < HEAD_DIM) == (r8 % 2 == 0)
    rep = (lax.broadcasted_iota(I32, (nb * 8, nb), 0) // 8 == lax.broadcasted_iota(I32, (nb * 8, nb), 1)).astype(BF16)
    q8 = jnp.zeros((nb * 8, LANES), F32)
    scale = HEAD_DIM ** -0.5
    for j in range(Q_TILES):
        qt = _head_norm_rope(proj[:, j * LANES:(j + 1) * LANES], hmean, qn_ref[...], cos, sin_a, sin_b) * scale
        qrep = jnp.dot(rep, qt.astype(BF16), preferred_element_type=F32)
        q8 = jnp.where(keep & ((r8 % 8) // 2 == j), qrep, q8)
    q8 = q8.astype(BF16)

    sink8 = sink8_ref[:, 0:1]
    ones_bf = jnp.ones((wb, LANES), BF16)
    assert pos >= wb - 1 and wb <= WINDOW
    for b in range(nb):
        kb = nk_ref[b].astype(BF16)
        vb = nv_ref[b].astype(BF16)
        s = lax.dot_general(q8[b * 8:(b + 1) * 8], kb, (((1,), (1,)), ((), ())), preferred_element_type=F32)
        o8[b * 8:(b + 1) * 8, :] = _softmax_pv(s, sink8, vb, ones_bf)
    o8m = jnp.where(keep, o8[...], 0.0).astype(BF16)

    a_tiles = []
    sel_r = lax.broadcasted_iota(I32, (nb, nb * 8), 1)
    sel_b = lax.broadcasted_iota(I32, (nb, nb * 8), 0)
    for j in range(Q_TILES):
        sel = ((sel_r // 8 == sel_b) & ((sel_r % 8) // 2 == j)).astype(BF16)
        a_tiles.append(jnp.dot(sel, o8m, preferred_element_type=F32))

    z_tiles = []
    for gi, w in enumerate(POOL_WINDOWS):
        cols = slice(gi * POOL_GC, (gi + 1) * POOL_GC)
        wsum = u[:, cols]
        for sft in range(1, w):
            wsum = wsum + st_ref[POOL_PREFIX - sft, :, cols]
        d = wsum / float(min(pos + 1, w)) - u[:, cols]
        z_tiles.append(jnp.dot(d.astype(BF16), wpool_ref[gi], preferred_element_type=F32) * pscale_ref[:, cols])
    nst_ref[0:POOL_PREFIX - 1] = st_ref[1:POOL_PREFIX]
    nst_ref[POOL_PREFIX - 1] = u

    mixv = jnp.concatenate(a_tiles + z_tiles, axis=1).astype(BF16)
    x1_ref[...] = x + jnp.dot(mixv, wout_ref[...], preferred_element_type=F32)


def _mixer_sample(x, ck, cv, st, pos, tabs, sink8, g_attn, w_in, qn, kn, hmean, w_pool, pscale, w_out):
    n, wb = ck.shape[0], ck.shape[1]
    nb = SAMPLE_CHUNK
    assert n % nb == 0
    const = lambda shape: pl.BlockSpec(shape, lambda i: (0,) * len(shape))
    cache_blk = pl.BlockSpec((nb, wb, KV_WIDTH), lambda i: (i, 0, 0))
    st_blk = pl.BlockSpec((POOL_PREFIX, nb, POOL_WIDTH), lambda i: (0, i, 0))
    x_blk = pl.BlockSpec((nb, D_MODEL), lambda i: (i, 0))
    return pl.pallas_call(
        functools.partial(_mixer_sample_kernel, pos=pos),
        grid=(n // nb,),
        in_specs=[x_blk, cache_blk, cache_blk, st_blk, const((1, LANES)), const((1, LANES)), const((1, LANES)),
                  const((8, LANES)), const((1, D_MODEL)), const((D_MODEL, IN_WIDTH)), const((1, LANES)),
                  const((1, LANES)), const((LANES, LANES)), const((len(POOL_WINDOWS), POOL_GC, POOL_GC)),
                  const((1, POOL_WIDTH)), const((D_MODEL, D_MODEL))],
        out_specs=[x_blk, cache_blk, cache_blk, st_blk],
        out_shape=[jax.ShapeDtypeStruct((n, D_MODEL), F32), jax.ShapeDtypeStruct(ck.shape, F32),
                   jax.ShapeDtypeStruct(cv.shape, F32), jax.ShapeDtypeStruct(st.shape, F32)],
        scratch_shapes=[pltpu.VMEM((nb * 8, LANES), F32)],
        compiler_params=pltpu.CompilerParams(dimension_semantics=("arbitrary",), vmem_limit_bytes=VMEM_LIMIT),
        name="mixer_sample",
    )(x, ck, cv, st, *tabs, sink8, g_attn, w_in, qn, kn, hmean, w_pool, pscale, w_out)


def _block_plan(cnt, bm, n_lanes):
    e_sub = lax.broadcasted_iota(I32, (N_EXPERTS, LANES), 0)
    e_lane = lax.broadcasted_iota(I32, (N_EXPERTS, LANES), 1)
    padded = jnp.floor((cnt + (bm - 1)) / bm) * bm
    padded_lane = jnp.sum(jnp.where(e_sub == e_lane, padded, 0.0), axis=0, keepdims=True)
    pad_end = jnp.sum(jnp.where(e_lane <= e_sub, padded_lane, 0.0), axis=1, keepdims=True)
    pad_start = pad_end - padded
    blk_start = lax.broadcasted_iota(I32, (N_EXPERTS, n_lanes), 1).astype(F32) * bm
    blk_e = jnp.minimum(jnp.sum((pad_end <= blk_start).astype(F32), axis=0, keepdims=True), N_EXPERTS - 1.0)
    mine = lax.broadcasted_iota(I32, (N_EXPERTS, n_lanes), 0).astype(F32) == blk_e
    last = jnp.sum(jnp.where(mine, pad_start + cnt, 0.0), axis=0, keepdims=True)
    blk_valid = jnp.clip(last - blk_start[0:1], 0.0, float(bm))
    e_blk = lax.broadcasted_iota(I32, (N_EXPERTS, n_lanes), 0).astype(F32)
    later = jnp.min(jnp.where((e_blk > blk_e) & (cnt > 0.0), e_blk, float(N_EXPERTS)), axis=0, keepdims=True)
    blk_next = jnp.where(later < N_EXPERTS, later, -1.0)
    return pad_start, blk_e, blk_valid, blk_next


def _route_kernel(x1_ref, g_ref, wr_ref, br_ref, cnt_in_ref, hf_ref, idx_ref, gate_ref, pos_ref, cnt_ref,
                  pstart_ref, blke_ref, blkv_ref, blkn_ref, counts, *, bm):
    i = pl.program_id(0)
    rows = x1_ref.shape[0]

    @pl.when(i == 0)
    def _():
        counts[...] = cnt_in_ref[...]

    h = _rms(x1_ref[...], g_ref[...])
    hf_ref[...] = _pack_halves(h)

    logits = lax.dot_general(wr_ref[...], h, (((1,), (1,)), ((), ())), preferred_element_type=F32,
                             precision=lax.Precision.HIGHEST) + br_ref[...]
    eid = lax.broadcasted_iota(I32, (N_EXPERTS, rows), 0).astype(F32)
    work = logits
    vals, hots = [], []
    for kk in range(TOP_K):
        m = jnp.max(work, axis=0, keepdims=True)
        first = jnp.min(jnp.where(work == m, eid, float(N_EXPERTS)), axis=0, keepdims=True)
        hot = eid == first
        work = jnp.where(hot, -jnp.inf, work)
        vals.append(m)
        hots.append(hot)
        idx_ref[kk:kk + 1, :] = first.astype(I32)
    es = [jnp.exp(vv - vals[0]) for vv in vals]
    den = es[0] + es[1] + es[2] + es[3]
    for kk in range(TOP_K):
        gate_ref[kk:kk + 1, :] = es[kk] / den

    chosen = hots[0] | hots[1] | hots[2] | hots[3]
    before = (lax.broadcasted_iota(I32, (rows, rows), 0) < lax.broadcasted_iota(I32, (rows, rows), 1)).astype(BF16)
    rank = jnp.dot(chosen.astype(BF16), before, preferred_element_type=F32) + counts[...]
    for kk in range(TOP_K):
        pos_ref[kk:kk + 1, :] = jnp.sum(jnp.where(hots[kk], rank, 0.0), axis=0, keepdims=True).astype(I32)
    counts[...] = counts[...] + jnp.sum(chosen.astype(F32), axis=1, keepdims=True)
    cnt_ref[...] = counts[...]

    @pl.when(i == pl.num_programs(0) - 1)
    def _():
        pad_start, blk_e, blk_valid, blk_next = _block_plan(counts[...], bm, blke_ref.shape[1])
        pstart_ref[...] = pad_start.astype(I32)
        blke_ref[...] = blk_e.astype(I32)
        blkv_ref[...] = blk_valid.astype(I32)
        blkn_ref[...] = blk_next.astype(I32)


def _route(x1, g_ffn, wr_t, br, cnt_in, rows, bm, n_blocks):
    n = x1.shape[0]
    assert n % rows == 0
    nb_lanes = -(-n_blocks // LANES) * LANES
    const = lambda shape: pl.BlockSpec(shape, lambda i: (0,) * len(shape))
    tok_blk = pl.BlockSpec((TOP_K, rows), lambda i: (0, i))
    return pl.pallas_call(
        functools.partial(_route_kernel, bm=bm),
        grid=(n // rows,),
        in_specs=[pl.BlockSpec((rows, D_MODEL), lambda i: (i, 0)), const((1, D_MODEL)), const((N_EXPERTS, D_MODEL)),
                  const((N_EXPERTS, 1)), const((N_EXPERTS, 1))],
        out_specs=[pl.BlockSpec((rows, D_MODEL // 2), lambda i: (i, 0)), tok_blk, tok_blk, tok_blk,
                   const((N_EXPERTS, 1)), const((N_EXPERTS, 1)), const((1, nb_lanes)), const((1, nb_lanes)),
                   const((1, nb_lanes))],
        out_shape=[jax.ShapeDtypeStruct((n, D_MODEL // 2), U32), jax.ShapeDtypeStruct((TOP_K, n), I32),
                   jax.ShapeDtypeStruct((TOP_K, n), F32), jax.ShapeDtypeStruct((TOP_K, n), I32),
                   jax.ShapeDtypeStruct((N_EXPERTS, 1), F32), jax.ShapeDtypeStruct((N_EXPERTS, 1), I32),
                   jax.ShapeDtypeStruct((1, nb_lanes), I32), jax.ShapeDtypeStruct((1, nb_lanes), I32),
                   jax.ShapeDtypeStruct((1, nb_lanes), I32)],
        scratch_shapes=[pltpu.VMEM((N_EXPERTS, 1), F32)],
        compiler_params=pltpu.CompilerParams(dimension_semantics=("arbitrary",), vmem_limit_bytes=VMEM_LIMIT),
        name="route",
    )(x1, g_ffn, wr_t, br, cnt_in)


def _sc_mesh():
    return plsc.VectorSubcoreMesh(core_axis_name="core", subcore_axis_name="subcore")


def _sc_worker_id():
    return lax.axis_index("core") * SC_SUBCORES + lax.axis_index("subcore")


def _scatter_rows(xa, xb, dest, n_rows):
    ch = SC_CHUNK
    na, w = xa.shape
    n = na + xb.shape[0]
    nk = dest.shape[0]
    assert na % ch == 0 and n % ch == 0 and dest.shape[1] == n and xb.shape[1] == w and xa.dtype == xb.dtype
    n_chunks = n // ch

    @pl.kernel(out_type=jax.ShapeDtypeStruct((n_rows, w), xa.dtype), mesh=_sc_mesh(),
               scratch_types=[pltpu.VMEM((ch,), I32), pltpu.VMEM((ch, w), xa.dtype)])
    def scatter_kernel(xa_hbm, xb_hbm, d_hbm, o_hbm, idx_v, buf):
        wid = _sc_worker_id()

        @pl.loop(0, -(-n_chunks // SC_WORKERS))
        def _(j):
            c = j * SC_WORKERS + wid

            @pl.when(c < na // ch)
            def _():
                pltpu.sync_copy(xa_hbm.at[pl.ds(c * ch, ch)], buf)

            @pl.when((c >= na // ch) & (c < n_chunks))
            def _():
                pltpu.sync_copy(xb_hbm.at[pl.ds(c * ch - na, ch)], buf)

            @pl.when(c < n_chunks)
            def _():
                for kk in range(nk):
                    pltpu.sync_copy(d_hbm.at[pl.ds(kk * n + c * ch, ch)], idx_v)
                    pltpu.sync_copy(buf, o_hbm.at[idx_v])

    return scatter_kernel(xa, xb, dest.reshape(-1))


def _gather_rows(src, idx):
    ch = SC_CHUNK
    m = idx.shape[0]
    w = src.shape[1]
    per = m // SC_WORKERS
    n_pairs = per // (2 * ch)
    assert m % SC_WORKERS == 0 and per % (2 * ch) == 0
    dma = pltpu.SemaphoreType.DMA

    @pl.kernel(out_type=jax.ShapeDtypeStruct((m, w), src.dtype), mesh=_sc_mesh(),
               scratch_types=[pltpu.VMEM((per,), I32), pltpu.VMEM((ch, w), src.dtype), pltpu.VMEM((ch, w), src.dtype),
                              dma, dma, dma, dma])
    def gather_kernel(s_hbm, i_hbm, o_hbm, idx_v, buf_a, buf_b, sem_ga, sem_gb, sem_wa, sem_wb):
        base = _sc_worker_id() * per
        pltpu.sync_copy(i_hbm.at[pl.ds(base, per)], idx_v)

        def fetch(j, buf, sem):
            return pltpu.make_async_copy(s_hbm.at[idx_v.at[pl.ds(j * ch, ch)]], buf, sem)

        def put(j, buf, sem):
            return pltpu.make_async_copy(buf, o_hbm.at[pl.ds(base + j * ch, ch)], sem)

        fetch(0, buf_a, sem_ga).start()

        @pl.loop(0, n_pairs)
        def _(p):
            j0 = 2 * p
            j1 = j0 + 1

            @pl.when(p > 0)
            def _():
                put(j1 - 2, buf_b, sem_wb).wait()

            fetch(j1, buf_b, sem_gb).start()
            fetch(j0, buf_a, sem_ga).wait()
            put(j0, buf_a, sem_wa).start()
            fetch(j1, buf_b, sem_gb).wait()
            put(j1, buf_b, sem_wb).start()
            put(j0, buf_a, sem_wa).wait()

            @pl.when(p + 1 < n_pairs)
            def _():
                fetch(j0 + 2, buf_a, sem_ga).start()

        put(2 * n_pairs - 1, buf_b, sem_wb).wait()

    return gather_kernel(src, idx)


def _moe_kernel(blk_e_ref, blk_valid_ref, blk_next_ref, xs_ref, wgu_hbm, bgu_ref, wd_hbm, bd_ref, y_ref,
                wgu_f32, wd_f32, wgu_bf, wd_bf, sems, *, e0):
    i = pl.program_id(0)
    e = blk_e_ref[i]
    n_valid = blk_valid_ref[i]
    used = n_valid > 0
    new_expert = (i == 0) | (blk_e_ref[jnp.maximum(i - 1, 0)] != e)

    def weight_copies(expert):
        return (pltpu.make_async_copy(wgu_hbm.at[e0 + expert], wgu_f32, sems.at[0]),
                pltpu.make_async_copy(wd_hbm.at[e0 + expert], wd_f32, sems.at[1]))

    @pl.when(used & (i == 0))
    def _():
        for cp in weight_copies(e):
            cp.start()

    @pl.when(used & new_expert)
    def _():
        for cp in weight_copies(e):
            cp.wait()
        wgu_bf[...] = wgu_f32[...].astype(BF16)
        wd_bf[...] = wd_f32[...].astype(BF16)
        nxt = blk_next_ref[i]

        @pl.when(nxt >= 0)
        def _():
            for cp in weight_copies(nxt):
                cp.start()

    @pl.when(used)
    def _():
        valid = lax.broadcasted_iota(I32, xs_ref.shape, 0) < n_valid
        xb = _unpack_halves(jnp.where(valid, xs_ref[...], jnp.uint32(0))).astype(BF16)
        gu = jnp.dot(xb, wgu_bf[...], preferred_element_type=F32) + bgu_ref[0]
        g = jnp.minimum(gu[:, :D_FF], SWIGLU_LIMIT)
        up = jnp.clip(gu[:, D_FF:], -SWIGLU_LIMIT, SWIGLU_LIMIT)
        act = (up + 1.0) * (g * jax.nn.sigmoid(SWIGLU_ALPHA * g))
        y_ref[...] = _pack_halves(jnp.dot(act.astype(BF16), wd_bf[...], preferred_element_type=F32) + bd_ref[0])

    @pl.when(jnp.logical_not(used))
    def _():
        y_ref[...] = jnp.zeros_like(y_ref)


def _moe(xs, blk_e, blk_valid, blk_next, layer, w_gu, b_gu, w_d, b_d):
    n_rows = xs.shape[0]
    bm = MOE_BM
    assert n_rows % bm == 0
    e0 = layer * N_EXPERTS
    n_we = w_gu.shape[0] * w_gu.shape[1]
    any_spec = pl.BlockSpec(memory_space=pl.ANY)
    grid_spec = pltpu.PrefetchScalarGridSpec(
        num_scalar_prefetch=3,
        grid=(n_rows // bm,),
        in_specs=[pl.BlockSpec((bm, D_MODEL // 2), lambda i, be, bv, bn: (i, 0)),
                  any_spec,
                  pl.BlockSpec((1, 1, 2 * D_FF), lambda i, be, bv, bn: (e0 + be[i], 0, 0)),
                  any_spec,
                  pl.BlockSpec((1, 1, D_MODEL), lambda i, be, bv, bn: (e0 + be[i], 0, 0))],
        out_specs=pl.BlockSpec((bm, D_MODEL // 2), lambda i, be, bv, bn: (i, 0)),
        scratch_shapes=[pltpu.VMEM((D_MODEL, 2 * D_FF), F32), pltpu.VMEM((D_FF, D_MODEL), F32),
                        pltpu.VMEM((D_MODEL, 2 * D_FF), BF16), pltpu.VMEM((D_FF, D_MODEL), BF16),
                        pltpu.SemaphoreType.DMA((2,))],
    )
    return pl.pallas_call(
        functools.partial(_moe_kernel, e0=e0),
        grid_spec=grid_spec,
        out_shape=jax.ShapeDtypeStruct((n_rows, D_MODEL // 2), U32),
        compiler_params=pltpu.CompilerParams(dimension_semantics=("arbitrary",), vmem_limit_bytes=VMEM_LIMIT),
        name="moe_experts",
    )(blk_e, blk_valid, blk_next, xs, w_gu.reshape(n_we, D_MODEL, 2 * D_FF), b_gu.reshape(n_we, 1, 2 * D_FF),
      w_d.reshape(n_we, D_FF, D_MODEL), b_d.reshape(n_we, 1, D_MODEL))


def _ple_kernel(x1_ref, y0_ref, y1_ref, y2_ref, y3_ref, gates_ref, p_ref, g_ref, wg_ref, wp_ref, out_ref):
    x2 = x1_ref[...]
    gates = gates_ref[...]
    for kk, y_ref in enumerate((y0_ref, y1_ref, y2_ref, y3_ref)):
        x2 = x2 + _unpack_halves(y_ref[...]) * gates[:, kk:kk + 1]
    hp = _rms(x2, g_ref[...]).astype(BF16)
    gate = jax.nn.sigmoid(jnp.dot(hp, wg_ref[...], preferred_element_type=F32))
    pp = jnp.dot(p_ref[...].astype(BF16), wp_ref[...], preferred_element_type=F32)
    out_ref[...] = x2 + gate * pp


def _ple(x1, y_tok, y0, gates_t, tok0, p_all, p0, g_ple, w_gate, w_proj, rows):
    n = x1.shape[0]
    assert n % rows == 0 and tok0 % rows == 0 and y0 % rows == 0 and p0 % rows == 0
    const = lambda shape: pl.BlockSpec(shape, lambda i: (0,) * len(shape))
    y_blk = lambda kk: pl.BlockSpec((rows, D_MODEL // 2), lambda i: ((y0 + kk * n) // rows + i, 0))
    return pl.pallas_call(
        _ple_kernel,
        grid=(n // rows,),
        in_specs=[pl.BlockSpec((rows, D_MODEL), lambda i: (i, 0)), y_blk(0), y_blk(1), y_blk(2), y_blk(3),
                  pl.BlockSpec((rows, TOP_K), lambda i: (tok0 // rows + i, 0)),
                  pl.BlockSpec((rows, PLE_DIM), lambda i: (p0 // rows + i, 0)),
                  const((1, D_MODEL)), const((D_MODEL, D_MODEL)), const((PLE_DIM, D_MODEL))],
        out_specs=pl.BlockSpec((rows, D_MODEL), lambda i: (i, 0)),
        out_shape=jax.ShapeDtypeStruct((n, D_MODEL), F32),
        compiler_params=pltpu.CompilerParams(dimension_semantics=("arbitrary",), vmem_limit_bytes=VMEM_LIMIT),
        name="combine_ple",
    )(x1, y_tok, y_tok, y_tok, y_tok, gates_t, p_all, g_ple, w_gate, w_proj)


def _rope_tables(pos):
    half = ROT_DIM // 2
    d = np.arange(LANES) % HEAD_DIM
    inv = ROPE_THETA ** (-jnp.arange(half, dtype=F32) / half)
    inv_lane = jnp.where(d < ROT_DIM, inv[d % half], 0.0)
    ang = pos.astype(F32)[:, None] * inv_lane[None, :]
    cos, sin = jnp.cos(ang), jnp.sin(ang)
    sin_a = jnp.where((d >= half) & (d < ROT_DIM), sin, 0.0)
    sin_b = jnp.where(d < half, -sin, 0.0)
    return cos, sin_a, sin_b


def _layer(layer, xp, xs, ck, cv, st, p_prompt_all, p_sample_all, past_len, refine_tail, tabs_p, tabs_s,
           norm_attn, w_in, q_norm, k_norm, attn_sinks, w_pool, pool_scale, w_out,
           norm_ffn, w_router, b_router, w_gate_up_all, b_gate_up_all, w_down_all, b_down_all,
           norm_ple, w_ple_gate, w_ple_proj):
    t, ns = xp.shape[0], xs.shape[0]
    w_q = w_in[:, :ATTN_WIDTH].reshape(D_MODEL, 2, Q_TILES, HEAD_DIM).transpose(0, 2, 1, 3).reshape(D_MODEL, ATTN_WIDTH)
    w_in_p = jnp.concatenate([w_q, w_in[:, ATTN_WIDTH:]], axis=1)
    w_oa = w_out[:ATTN_WIDTH].reshape(2, Q_TILES, HEAD_DIM, D_MODEL).transpose(1, 0, 2, 3).reshape(ATTN_WIDTH, D_MODEL)
    w_out_p = jnp.concatenate([w_oa, w_out[ATTN_WIDTH:]], axis=0)
    g_attn = norm_attn.reshape(1, D_MODEL)
    qn = jnp.tile(q_norm, 2).reshape(1, LANES)
    kn = jnp.tile(k_norm, 2).reshape(1, LANES)
    lane = np.arange(LANES)
    hmean = jnp.asarray((lane[:, None] // HEAD_DIM == lane[None, :] // HEAD_DIM) / HEAD_DIM, F32)
    pscale = pool_scale.reshape(1, POOL_WIDTH)
    sink8 = jnp.broadcast_to(attn_sinks.reshape(2, Q_TILES).T.reshape(8, 1), (8, LANES))

    mix_args = (g_attn, w_in_p.astype(BF16), qn, kn, hmean.astype(BF16), w_pool.astype(BF16), pscale,
                w_out_p.astype(BF16))
    x1p, nk_p, nv_p, nu_p = _mixer_prompt(xp, 0, MIX_ROWS, attn_sinks, tabs_p, *mix_args)
    if refine_tail:
        hi_args = (g_attn, w_in_p, qn, kn, hmean, w_pool, pscale, w_out_p)
        x1p, nk_p, nv_p, nu_p = _mixer_prompt(xp, t - 2 * MIX_TAIL_ROWS, MIX_TAIL_ROWS, attn_sinks, tabs_p, *hi_args,
                                              x1_into=x1p)
    st_t = jnp.transpose(st, (1, 0, 2))
    x1s, nk_s, nv_s, nst_t = _mixer_sample(xs, ck.reshape(ns, -1, KV_WIDTH), cv.reshape(ns, -1, KV_WIDTH), st_t,
                                           past_len, tabs_s, sink8, *mix_args)

    g_ffn = norm_ffn.reshape(1, D_MODEL)
    wr_t = w_router.T
    br = b_router.reshape(N_EXPERTS, 1)
    n_tok = t + ns
    bm = MOE_BM
    n_blocks = -(-(n_tok * TOP_K + N_EXPERTS * (bm - 1)) // bm)
    hf_p, idx_p, gate_p, pos_p, cnt_p = _route(x1p, g_ffn, wr_t, br, jnp.zeros((N_EXPERTS, 1), F32),
                                               ROUTE_ROWS, bm, n_blocks)[:5]
    hf_s, idx_s, gate_s, pos_s, _, pad_start, blk_e, blk_valid, blk_next = _route(x1s, g_ffn, wr_t, br, cnt_p, ns,
                                                                                  bm, n_blocks)
    idx = jnp.concatenate([idx_p, idx_s], axis=1)
    pos = jnp.concatenate([pos_p, pos_s], axis=1)
    blk_e, blk_valid, blk_next = (a[0, :n_blocks] for a in (blk_e, blk_valid, blk_next))
    start_of = jnp.sum(jnp.where(idx[None] == jnp.arange(N_EXPERTS, dtype=I32)[:, None, None],
                                 pad_start.reshape(N_EXPERTS, 1, 1), 0), axis=0)
    dest = start_of + pos

    xs_rows = _scatter_rows(hf_p, hf_s, dest, n_blocks * bm)
    y = _moe(xs_rows, blk_e, blk_valid, blk_next, layer, w_gate_up_all, b_gate_up_all, w_down_all, b_down_all)

    back = jnp.concatenate([dest[:, :t].reshape(-1), dest[:, t:].reshape(-1)])
    unit = SC_WORKERS * SC_CHUNK * 2
    back = jnp.concatenate([back, jnp.arange(-back.shape[0] % unit, dtype=I32)])
    y_tok = _gather_rows(y, back)

    g_ple = norm_ple.reshape(1, D_MODEL)
    wg_bf = w_ple_gate.astype(BF16)
    wp_bf = w_ple_proj.astype(BF16)
    yp = _ple(x1p, y_tok, 0, gate_p.T, 0, p_prompt_all, layer * t, g_ple, wg_bf, wp_bf, PLE_ROWS)
    ys = _ple(x1s, y_tok, TOP_K * t, gate_s.T, 0, p_sample_all, layer * ns, g_ple, wg_bf, wp_bf, ns)
    new_pool_s = jnp.transpose(nst_t, (1, 0, 2))
    return yp, ys, nk_p, nv_p, nu_p[POOL_HALO - POOL_PREFIX:], nk_s, nv_s, new_pool_s


def kernel(x_prompt, x_sample, cache_k, cache_v, state_pool, p_prompt, p_sample, norm_attn, w_in, q_norm, k_norm,
           attn_sinks, w_pool, pool_scale, w_out, norm_ffn, w_router, b_router, w_gate_up, b_gate_up, w_down, b_down,
           norm_ple, w_ple_gate, w_ple_proj):
    depth = norm_attn.shape[0]
    batch, seq, d = x_prompt.shape
    ns, dec_seq, _ = x_sample.shape
    wb = cache_k.shape[2]
    assert batch == 1 and dec_seq == 1 and d == D_MODEL and wb == WINDOW
    assert cache_k.shape[3:] == (N_KV_HEADS, HEAD_DIM) and state_pool.shape[2:] == (POOL_PREFIX, POOL_WIDTH)
    past_len = PAST_LEN
    yp = x_prompt.reshape(seq, d)
    ys = x_sample.reshape(ns, d)
    p_prompt_all = p_prompt.reshape(depth * seq, PLE_DIM)
    p_sample_all = p_sample.reshape(depth * ns, PLE_DIM)
    tabs_p = _rope_tables(jnp.arange(seq))
    tabs_s = _rope_tables(jnp.full((1,), past_len))
    outs = [[] for _ in range(6)]
    for i in range(depth):
        res = _layer(i, yp, ys, cache_k[i], cache_v[i], state_pool[i], p_prompt_all, p_sample_all, past_len,
                     i < depth - 1, tabs_p, tabs_s,
                     norm_attn[i], w_in[i], q_norm[i], k_norm[i], attn_sinks[i], w_pool[i], pool_scale[i], w_out[i],
                     norm_ffn[i], w_router[i], b_router[i], w_gate_up, b_gate_up, w_down, b_down,
                     norm_ple[i], w_ple_gate[i], w_ple_proj[i])
        yp, ys = res[0], res[1]
        kv_shape = (1, WINDOW, N_KV_HEADS, HEAD_DIM)
        outs[0].append(res[2].reshape(kv_shape))
        outs[1].append(res[3].reshape(kv_shape))
        outs[2].append(res[4].reshape(1, POOL_PREFIX, POOL_WIDTH))
        outs[3].append(res[5].reshape(ns, wb, N_KV_HEADS, HEAD_DIM))
        outs[4].append(res[6].reshape(ns, wb, N_KV_HEADS, HEAD_DIM))
        outs[5].append(res[7])
    return (yp.reshape(batch, seq, d), ys.reshape(ns, dec_seq, d)) + tuple(jnp.stack(o) for o in outs)
```

```python
import functools

import jax
import jax.numpy as jnp
import numpy as np
from jax import lax
from jax.experimental import pallas as pl
from jax.experimental.pallas import tpu as pltpu
from jax.experimental.pallas import tpu_sc as plsc

F32 = jnp.float32
BF16 = jnp.bfloat16
U32 = jnp.uint32
I32 = jnp.int32

D_MODEL = 1024
HEAD_DIM = 64
N_HEADS = 8
N_KV_HEADS = 2
GROUP = N_HEADS // N_KV_HEADS
ATTN_WIDTH = N_HEADS * HEAD_DIM
KV_WIDTH = N_KV_HEADS * HEAD_DIM
POOL_WIDTH = 512
POOL_WINDOWS = (2, 4, 8, 16)
POOL_GC = POOL_WIDTH // len(POOL_WINDOWS)
POOL_PREFIX = max(POOL_WINDOWS) - 1
POOL_HALO = POOL_PREFIX + 1
IN_WIDTH = ATTN_WIDTH + 2 * KV_WIDTH + POOL_WIDTH
WINDOW = 128
ROPE_THETA = 500000.0
ROT_DIM = HEAD_DIM // 4
N_EXPERTS = 32
TOP_K = 4
D_FF = 1024
SWIGLU_ALPHA = 1.702
SWIGLU_LIMIT = 7.0
PLE_DIM = 256
PAST_LEN = 16384
EPS = 1e-5
NEG_INF = -1e30

LANES = 128
Q_TILES = ATTN_WIDTH // LANES

MIX_ROWS = 512
MIX_TAIL_ROWS = 256
SAMPLE_CHUNK = 16
ROUTE_ROWS = 512
MOE_BM = 512
SC_CORES = 2
SC_SUBCORES = 16
SC_WORKERS = SC_CORES * SC_SUBCORES
SC_CHUNK = 64
PLE_ROWS = 512
VMEM_LIMIT = 56 * 1024 * 1024


def _rms(x, g):
    return x * lax.rsqrt(jnp.mean(x * x, axis=-1, keepdims=True) + EPS) * g


def _pack_halves(x):
    w = x.shape[1] // 2
    lo = lax.bitcast_convert_type(x[:, :w].astype(BF16).astype(F32), U32) >> 16
    hi = lax.bitcast_convert_type(x[:, w:].astype(BF16).astype(F32), U32) & jnp.uint32(0xFFFF0000)
    return lo | hi


def _unpack_halves(packed):
    lo = lax.bitcast_convert_type(packed << 16, F32)
    hi = lax.bitcast_convert_type(packed & jnp.uint32(0xFFFF0000), F32)
    return jnp.concatenate([lo, hi], axis=1)


def _mm(a, b, nt=False):
    dims = (((1,), (1 if nt else 0,)), ((), ()))
    if b.dtype == F32:
        return lax.dot_general(a.astype(F32), b, dims, preferred_element_type=F32, precision=lax.Precision.HIGHEST)
    return lax.dot_general(a.astype(BF16), b, dims, preferred_element_type=F32)


def _head_norm_rope(t, hmean, gain, cos, sin_a, sin_b):
    t = t * lax.rsqrt(_mm(t * t, hmean) + EPS) * gain
    return t * cos + pltpu.roll(t, ROT_DIM // 2, axis=1) * sin_a + pltpu.roll(t, LANES - ROT_DIM // 2, axis=1) * sin_b


def _softmax_pv(s, sink, v, ones):
    m = jnp.maximum(jnp.max(s, axis=-1, keepdims=True), sink)
    e = jnp.exp(s - m).astype(v.dtype)
    den = _mm(e, ones) + jnp.exp(sink - m)
    return _mm(e, v) / den


def _mixer_prompt_kernel(sinks_ref, x_ref, cos_ref, sa_ref, sb_ref, g_ref, win_ref, qn_ref, kn_ref, hm_ref,
                         wpool_ref, pscale_ref, wout_ref, *rest, row_offset, aliased):
    x1_ref, klast_ref, vlast_ref, ulast_ref, kprev, vprev, uext, mix = rest[1:] if aliased else rest
    i = pl.program_id(0)
    rows = x_ref.shape[0]
    n_sub = rows // WINDOW
    cdt = win_ref.dtype
    row0 = row_offset + i * rows

    @pl.when(i == 0)
    def _():
        kprev[...] = jnp.zeros_like(kprev)
        vprev[...] = jnp.zeros_like(vprev)
        uext[0:POOL_HALO, :] = jnp.zeros((POOL_HALO, POOL_WIDTH), F32)

    x = x_ref[...]
    proj = _mm(_rms(x, g_ref[...]), win_ref[...])
    cos, sin_a, sin_b = cos_ref[...], sa_ref[...], sb_ref[...]

    n_t = Q_TILES + 1
    t_all = jnp.concatenate([proj[:, j * LANES:(j + 1) * LANES] for j in range(n_t)], axis=0)
    t3 = (t_all * lax.rsqrt(_mm(t_all * t_all, hm_ref[...]) + EPS)).reshape(n_t, rows, LANES)
    t3 = jnp.concatenate([t3[:Q_TILES] * (qn_ref[...] * HEAD_DIM ** -0.5), t3[Q_TILES:] * kn_ref[...]], axis=0)
    t2 = t3.reshape(n_t * rows, LANES)
    t3 = (t3 * cos + pltpu.roll(t2, ROT_DIM // 2, axis=1).reshape(n_t, rows, LANES) * sin_a
          + pltpu.roll(t2, LANES - ROT_DIM // 2, axis=1).reshape(n_t, rows, LANES) * sin_b)
    q3 = t3[:Q_TILES]
    k = t3[Q_TILES]
    v = proj[:, ATTN_WIDTH + KV_WIDTH:ATTN_WIDTH + 2 * KV_WIDTH]
    u = proj[:, ATTN_WIDTH + 2 * KV_WIDTH:]
    klast_ref[...] = k[rows - WINDOW:, :]
    vlast_ref[...] = v[rows - WINDOW:, :]
    ulast_ref[...] = u[rows - POOL_HALO:, :]
    k_c = k.astype(cdt)
    v_c = jnp.concatenate([v.astype(cdt), jnp.ones((rows, LANES), cdt)], axis=1)
    v_first = jnp.concatenate([vprev[...], jnp.ones((WINDOW, LANES), cdt)], axis=1)

    lane = lax.broadcasted_iota(I32, (WINDOW, LANES), 1)
    left = (lane < HEAD_DIM)[None]
    qi = lax.broadcasted_iota(I32, (WINDOW, 2 * WINDOW), 0)
    kj = lax.broadcasted_iota(I32, (WINDOW, 2 * WINDOW), 1)
    band = (kj - qi >= 1) & (kj - qi <= WINDOW)
    sink3 = jnp.concatenate([jnp.full((1, 1, 1), sinks_ref[j + Q_TILES * s], F32)
                             for j in range(Q_TILES) for s in range(2)], axis=0)
    n_g = 2 * Q_TILES

    v_cats, masks, scores = [], [], []
    for c in range(n_sub):
        r0 = c * WINDOW
        if c == 0:
            k_cat = jnp.concatenate([kprev[...], k_c[0:WINDOW]], axis=0)
            v_cats.append(jnp.concatenate([v_first, v_c[0:WINDOW]], axis=0))
            masks.append(band & (kj + (row0 - WINDOW) >= 0))
        else:
            k_cat = k_c[r0 - WINDOW:r0 + WINDOW]
            v_cats.append(v_c[r0 - WINDOW:r0 + WINDOW])
            masks.append(band)
        q_c = q3[:, r0:r0 + WINDOW, :]
        q_all = jnp.concatenate([jnp.where(left, q_c, 0.0), jnp.where(left, 0.0, q_c)], axis=1)
        scores.append(_mm(q_all.reshape(n_g * WINDOW, LANES), k_cat, nt=True))
    kprev[...] = k_c[rows - WINDOW:]
    vprev[...] = v[rows - WINDOW:].astype(cdt)

    probs, maxes = [], []
    for c in range(n_sub):
        s = jnp.where(masks[c][None], scores[c].reshape(n_g, WINDOW, 2 * WINDOW), NEG_INF)
        m = jnp.maximum(jnp.max(s, axis=-1, keepdims=True), sink3)
        probs.append(jnp.exp(s - m).astype(cdt).reshape(n_g * WINDOW, 2 * WINDOW))
        maxes.append(m)
    applied = [_mm(probs[c], v_cats[c]) for c in range(n_sub)]
    for c in range(n_sub):
        r0 = c * WINDOW
        den = applied[c][:, LANES:] + jnp.exp(sink3 - maxes[c]).reshape(n_g * WINDOW, 1)
        o = (applied[c][:, :LANES] / den).reshape(Q_TILES, 2 * WINDOW, LANES)
        a = jnp.where(left, o[:, :WINDOW], o[:, WINDOW:])
        for j in range(Q_TILES):
            mix[r0:r0 + WINDOW, j * LANES:(j + 1) * LANES] = a[j].astype(cdt)

    uext[POOL_HALO:POOL_HALO + rows, :] = u
    pos1 = (lax.broadcasted_iota(I32, (rows, 1), 0) + row0 + 1).astype(F32)
    for gi, w in enumerate(POOL_WINDOWS):
        cols = slice(gi * POOL_GC, (gi + 1) * POOL_GC)
        wsum = u[:, cols]
        for sft in range(1, w):
            wsum = wsum + uext[POOL_HALO - sft:POOL_HALO - sft + rows, cols]
        d = wsum / jnp.minimum(pos1, float(w)) - u[:, cols]
        y = _mm(d, wpool_ref[gi]) * pscale_ref[:, cols]
        mix[:, ATTN_WIDTH + gi * POOL_GC:ATTN_WIDTH + (gi + 1) * POOL_GC] = y.astype(cdt)
    uext[0:POOL_HALO, :] = u[rows - POOL_HALO:, :]

    x1_ref[...] = x + _mm(mix[...], wout_ref[...])


def _mixer_prompt(x_full, row_offset, rows, sinks, tabs, g_attn, w_in, qn, kn, hmean, w_pool, pscale, w_out,
                  x1_into=None):
    t = x_full.shape[0] - row_offset
    cdt = w_in.dtype
    assert t % rows == 0 and row_offset % rows == 0 and rows % WINDOW == 0 and rows >= POOL_HALO
    blk0, n_steps = row_offset // rows, t // rows
    const = lambda shape: pl.BlockSpec(shape, lambda i, *_: (0,) * len(shape))
    row_blk = lambda width: pl.BlockSpec((rows, width), lambda i, *_: (blk0 + i, 0))
    aliased = x1_into is not None
    if aliased:
        assert x1_into.shape == x_full.shape
        x1_spec = pl.BlockSpec((rows, D_MODEL), lambda i, *_: (blk0 + n_steps - 1, 0))
        x1_shape = x1_into.shape
        extra_specs, extra_args, aliases = [pl.BlockSpec(memory_space=pl.ANY)], [x1_into], {13: 0}
    else:
        x1_spec = pl.BlockSpec((rows, D_MODEL), lambda i, *_: (i, 0))
        x1_shape = (t, D_MODEL)
        extra_specs, extra_args, aliases = [], [], {}
    grid_spec = pltpu.PrefetchScalarGridSpec(
        num_scalar_prefetch=1,
        grid=(n_steps,),
        in_specs=[row_blk(D_MODEL), row_blk(LANES), row_blk(LANES), row_blk(LANES),
                  const((1, D_MODEL)), const((D_MODEL, IN_WIDTH)), const((1, LANES)), const((1, LANES)),
                  const((LANES, LANES)), const((len(POOL_WINDOWS), POOL_GC, POOL_GC)), const((1, POOL_WIDTH)),
                  const((D_MODEL, D_MODEL))] + extra_specs,
        out_specs=[x1_spec, const((WINDOW, KV_WIDTH)), const((WINDOW, KV_WIDTH)),
                   const((POOL_HALO, POOL_WIDTH))],
        scratch_shapes=[pltpu.VMEM((WINDOW, KV_WIDTH), cdt), pltpu.VMEM((WINDOW, KV_WIDTH), cdt),
                        pltpu.VMEM((POOL_HALO + rows, POOL_WIDTH), F32), pltpu.VMEM((rows, D_MODEL), cdt)],
    )
    return pl.pallas_call(
        functools.partial(_mixer_prompt_kernel, row_offset=row_offset, aliased=aliased),
        grid_spec=grid_spec,
        out_shape=[jax.ShapeDtypeStruct(x1_shape, F32), jax.ShapeDtypeStruct((WINDOW, KV_WIDTH), F32),
                   jax.ShapeDtypeStruct((WINDOW, KV_WIDTH), F32), jax.ShapeDtypeStruct((POOL_HALO, POOL_WIDTH), F32)],
        input_output_aliases=aliases,
        compiler_params=pltpu.CompilerParams(dimension_semantics=("arbitrary",), vmem_limit_bytes=VMEM_LIMIT),
        name="mixer_prompt",
    )(sinks, x_full, *tabs, g_attn, w_in, qn, kn, hmean, w_pool, pscale, w_out, *extra_args)


def _mixer_sample_kernel(x_ref, ck_ref, cv_ref, st_ref, cos_ref, sa_ref, sb_ref, sink8_ref, g_ref, win_ref, qn_ref,
                         kn_ref, hm_ref, wpool_ref, pscale_ref, wout_ref,
                         x1_ref, nk_ref, nv_ref, nst_ref, o8, *, pos):
    nb = x_ref.shape[0]
    wb = ck_ref.shape[1]
    x = x_ref[...]
    h = _rms(x, g_ref[...]).astype(BF16)
    proj = jnp.dot(h, win_ref[...], preferred_element_type=F32)
    cos, sin_a, sin_b = cos_ref[...], sa_ref[...], sb_ref[...]
    hmean = hm_ref[...]
    k = _head_norm_rope(proj[:, ATTN_WIDTH:ATTN_WIDTH + KV_WIDTH], hmean, kn_ref[...], cos, sin_a, sin_b)
    v = proj[:, ATTN_WIDTH + KV_WIDTH:ATTN_WIDTH + 2 * KV_WIDTH]
    u = proj[:, ATTN_WIDTH + 2 * KV_WIDTH:]

    nk_ref[:, 0:wb - 1, :] = ck_ref[:, 1:wb, :]
    nv_ref[:, 0:wb - 1, :] = cv_ref[:, 1:wb, :]
    for b in range(nb):
        nk_ref[b, wb - 1:wb, :] = k[b:b + 1, :]
        nv_ref[b, wb - 1:wb, :] = v[b:b + 1, :]

    r8 = lax.broadcasted_iota(I32, (nb * 8, LANES), 0)
    lane8 = lax.broadcasted_iota(I32, (nb * 8, LANES), 1)
    keep = (lane8 < HEAD_DIM) == (r8 % 2 == 0)
    rep = (lax.broadcasted_iota(I32, (nb * 8, nb), 0) // 8 == lax.broadcasted_iota(I32, (nb * 8, nb), 1)).astype(BF16)
    q8 = jnp.zeros((nb * 8, LANES), F32)
    scale = HEAD_DIM ** -0.5
    for j in range(Q_TILES):
        qt = _head_norm_rope(proj[:, j * LANES:(j + 1) * LANES], hmean, qn_ref[...], cos, sin_a, sin_b) * scale
        qrep = jnp.dot(rep, qt.astype(BF16), preferred_element_type=F32)
        q8 = jnp.where(keep & ((r8 % 8) // 2 == j), qrep, q8)
    q8 = q8.astype(BF16)

    sink8 = sink8_ref[:, 0:1]
    ones_bf = jnp.ones((wb, LANES), BF16)
    assert pos >= wb - 1 and wb <= WINDOW
    for b in range(nb):
        kb = nk_ref[b].astype(BF16)
        vb = nv_ref[b].astype(BF16)
        s = lax.dot_general(q8[b * 8:(b + 1) * 8], kb, (((1,), (1,)), ((), ())), preferred_element_type=F32)
        o8[b * 8:(b + 1) * 8, :] = _softmax_pv(s, sink8, vb, ones_bf)
    o8m = jnp.where(keep, o8[...], 0.0).astype(BF16)

    a_tiles = []
    sel_r = lax.broadcasted_iota(I32, (nb, nb * 8), 1)
    sel_b = lax.broadcasted_iota(I32, (nb, nb * 8), 0)
    for j in range(Q_TILES):
        sel = ((sel_r // 8 == sel_b) & ((sel_r % 8) // 2 == j)).astype(BF16)
        a_tiles.append(jnp.dot(sel, o8m, preferred_element_type=F32))

    z_tiles = []
    for gi, w in enumerate(POOL_WINDOWS):
        cols = slice(gi * POOL_GC, (gi + 1) * POOL_GC)
        wsum = u[:, cols]
        for sft in range(1, w):
            wsum = wsum + st_ref[POOL_PREFIX - sft, :, cols]
        d = wsum / float(min(pos + 1, w)) - u[:, cols]
        z_tiles.append(jnp.dot(d.astype(BF16), wpool_ref[gi], preferred_element_type=F32) * pscale_ref[:, cols])
    nst_ref[0:POOL_PREFIX - 1] = st_ref[1:POOL_PREFIX]
    nst_ref[POOL_PREFIX - 1] = u

    mixv = jnp.concatenate(a_tiles + z_tiles, axis=1).astype(BF16)
    x1_ref[...] = x + jnp.dot(mixv, wout_ref[...], preferred_element_type=F32)


def _mixer_sample(x, ck, cv, st, pos, tabs, sink8, g_attn, w_in, qn, kn, hmean, w_pool, pscale, w_out):
    n, wb = ck.shape[0], ck.shape[1]
    nb = SAMPLE_CHUNK
    assert n % nb == 0
    const = lambda shape: pl.BlockSpec(shape, lambda i: (0,) * len(shape))
    cache_blk = pl.BlockSpec((nb, wb, KV_WIDTH), lambda i: (i, 0, 0))
    st_blk = pl.BlockSpec((POOL_PREFIX, nb, POOL_WIDTH), lambda i: (0, i, 0))
    x_blk = pl.BlockSpec((nb, D_MODEL), lambda i: (i, 0))
    return pl.pallas_call(
        functools.partial(_mixer_sample_kernel, pos=pos),
        grid=(n // nb,),
        in_specs=[x_blk, cache_blk, cache_blk, st_blk, const((1, LANES)), const((1, LANES)), const((1, LANES)),
                  const((8, LANES)), const((1, D_MODEL)), const((D_MODEL, IN_WIDTH)), const((1, LANES)),
                  const((1, LANES)), const((LANES, LANES)), const((len(POOL_WINDOWS), POOL_GC, POOL_GC)),
                  const((1, POOL_WIDTH)), const((D_MODEL, D_MODEL))],
        out_specs=[x_blk, cache_blk, cache_blk, st_blk],
        out_shape=[jax.ShapeDtypeStruct((n, D_MODEL), F32), jax.ShapeDtypeStruct(ck.shape, F32),
                   jax.ShapeDtypeStruct(cv.shape, F32), jax.ShapeDtypeStruct(st.shape, F32)],
        scratch_shapes=[pltpu.VMEM((nb * 8, LANES), F32)],
        compiler_params=pltpu.CompilerParams(dimension_semantics=("arbitrary",), vmem_limit_bytes=VMEM_LIMIT),
        name="mixer_sample",
    )(x, ck, cv, st, *tabs, sink8, g_attn, w_in, qn, kn, hmean, w_pool, pscale, w_out)


def _block_plan(cnt, bm, n_lanes):
    e_sub = lax.broadcasted_iota(I32, (N_EXPERTS, LANES), 0)
    e_lane = lax.broadcasted_iota(I32, (N_EXPERTS, LANES), 1)
    padded = jnp.floor((cnt + (bm - 1)) / bm) * bm
    padded_lane = jnp.sum(jnp.where(e_sub == e_lane, padded, 0.0), axis=0, keepdims=True)
    pad_end = jnp.sum(jnp.where(e_lane <= e_sub, padded_lane, 0.0), axis=1, keepdims=True)
    pad_start = pad_end - padded
    blk_start = lax.broadcasted_iota(I32, (N_EXPERTS, n_lanes), 1).astype(F32) * bm
    blk_e = jnp.minimum(jnp.sum((pad_end <= blk_start).astype(F32), axis=0, keepdims=True), N_EXPERTS - 1.0)
    mine = lax.broadcasted_iota(I32, (N_EXPERTS, n_lanes), 0).astype(F32) == blk_e
    last = jnp.sum(jnp.where(mine, pad_start + cnt, 0.0), axis=0, keepdims=True)
    blk_valid = jnp.clip(last - blk_start[0:1], 0.0, float(bm))
    e_blk = lax.broadcasted_iota(I32, (N_EXPERTS, n_lanes), 0).astype(F32)
    later = jnp.min(jnp.where((e_blk > blk_e) & (cnt > 0.0), e_blk, float(N_EXPERTS)), axis=0, keepdims=True)
    blk_next = jnp.where(later < N_EXPERTS, later, -1.0)
    return pad_start, blk_e, blk_valid, blk_next


def _route_kernel(x1_ref, g_ref, wr_ref, br_ref, cnt_in_ref, hf_ref, idx_ref, gate_ref, pos_ref, cnt_ref,
                  pstart_ref, blke_ref, blkv_ref, blkn_ref, counts, *, bm):
    i = pl.program_id(0)
    rows = x1_ref.shape[0]

    @pl.when(i == 0)
    def _():
        counts[...] = cnt_in_ref[...]

    h = _rms(x1_ref[...], g_ref[...])
    hf_ref[...] = _pack_halves(h)

    logits = lax.dot_general(wr_ref[...], h, (((1,), (1,)), ((), ())), preferred_element_type=F32,
                             precision=lax.Precision.HIGHEST) + br_ref[...]
    eid = lax.broadcasted_iota(I32, (N_EXPERTS, rows), 0).astype(F32)
    work = logits
    vals, hots = [], []
    for kk in range(TOP_K):
        m = jnp.max(work, axis=0, keepdims=True)
        first = jnp.min(jnp.where(work == m, eid, float(N_EXPERTS)), axis=0, keepdims=True)
        hot = eid == first
        work = jnp.where(hot, -jnp.inf, work)
        vals.append(m)
        hots.append(hot)
        idx_ref[kk:kk + 1, :] = first.astype(I32)
    es = [jnp.exp(vv - vals[0]) for vv in vals]
    den = es[0] + es[1] + es[2] + es[3]
    for kk in range(TOP_K):
        gate_ref[kk:kk + 1, :] = es[kk] / den

    chosen = hots[0] | hots[1] | hots[2] | hots[3]
    before = (lax.broadcasted_iota(I32, (rows, rows), 0) < lax.broadcasted_iota(I32, (rows, rows), 1)).astype(BF16)
    rank = jnp.dot(chosen.astype(BF16), before, preferred_element_type=F32) + counts[...]
    for kk in range(TOP_K):
        pos_ref[kk:kk + 1, :] = jnp.sum(jnp.where(hots[kk], rank, 0.0), axis=0, keepdims=True).astype(I32)
    counts[...] = counts[...] + jnp.sum(chosen.astype(F32), axis=1, keepdims=True)
    cnt_ref[...] = counts[...]

    @pl.when(i == pl.num_programs(0) - 1)
    def _():
        pad_start, blk_e, blk_valid, blk_next = _block_plan(counts[...], bm, blke_ref.shape[1])
        pstart_ref[...] = pad_start.astype(I32)
        blke_ref[...] = blk_e.astype(I32)
        blkv_ref[...] = blk_valid.astype(I32)
        blkn_ref[...] = blk_next.astype(I32)


def _route(x1, g_ffn, wr_t, br, cnt_in, rows, bm, n_blocks):
    n = x1.shape[0]
    assert n % rows == 0
    nb_lanes = -(-n_blocks // LANES) * LANES
    const = lambda shape: pl.BlockSpec(shape, lambda i: (0,) * len(shape))
    tok_blk = pl.BlockSpec((TOP_K, rows), lambda i: (0, i))
    return pl.pallas_call(
        functools.partial(_route_kernel, bm=bm),
        grid=(n // rows,),
        in_specs=[pl.BlockSpec((rows, D_MODEL), lambda i: (i, 0)), const((1, D_MODEL)), const((N_EXPERTS, D_MODEL)),
                  const((N_EXPERTS, 1)), const((N_EXPERTS, 1))],
        out_specs=[pl.BlockSpec((rows, D_MODEL // 2), lambda i: (i, 0)), tok_blk, tok_blk, tok_blk,
                   const((N_EXPERTS, 1)), const((N_EXPERTS, 1)), const((1, nb_lanes)), const((1, nb_lanes)),
                   const((1, nb_lanes))],
        out_shape=[jax.ShapeDtypeStruct((n, D_MODEL // 2), U32), jax.ShapeDtypeStruct((TOP_K, n), I32),
                   jax.ShapeDtypeStruct((TOP_K, n), F32), jax.ShapeDtypeStruct((TOP_K, n), I32),
                   jax.ShapeDtypeStruct((N_EXPERTS, 1), F32), jax.ShapeDtypeStruct((N_EXPERTS, 1), I32),
                   jax.ShapeDtypeStruct((1, nb_lanes), I32), jax.ShapeDtypeStruct((1, nb_lanes), I32),
                   jax.ShapeDtypeStruct((1, nb_lanes), I32)],
        scratch_shapes=[pltpu.VMEM((N_EXPERTS, 1), F32)],
        compiler_params=pltpu.CompilerParams(dimension_semantics=("arbitrary",), vmem_limit_bytes=VMEM_LIMIT),
        name="route",
    )(x1, g_ffn, wr_t, br, cnt_in)


def _sc_mesh():
    return plsc.VectorSubcoreMesh(core_axis_name="core", subcore_axis_name="subcore")


def _sc_worker_id():
    return lax.axis_index("core") * SC_SUBCORES + lax.axis_index("subcore")


def _scatter_rows(xa, xb, dest, n_rows):
    ch = SC_CHUNK
    na, w = xa.shape
    n = na + xb.shape[0]
    nk = dest.shape[0]
    assert na % ch == 0 and n % ch == 0 and dest.shape[1] == n and xb.shape[1] == w and xa.dtype == xb.dtype
    n_chunks = n // ch

    @pl.kernel(out_type=jax.ShapeDtypeStruct((n_rows, w), xa.dtype), mesh=_sc_mesh(),
               scratch_types=[pltpu.VMEM((ch,), I32), pltpu.VMEM((ch, w), xa.dtype)])
    def scatter_kernel(xa_hbm, xb_hbm, d_hbm, o_hbm, idx_v, buf):
        wid = _sc_worker_id()

        @pl.loop(0, -(-n_chunks // SC_WORKERS))
        def _(j):
            c = j * SC_WORKERS + wid

            @pl.when(c < na // ch)
            def _():
                pltpu.sync_copy(xa_hbm.at[pl.ds(c * ch, ch)], buf)

            @pl.when((c >= na // ch) & (c < n_chunks))
            def _():
                pltpu.sync_copy(xb_hbm.at[pl.ds(c * ch - na, ch)], buf)

            @pl.when(c < n_chunks)
            def _():
                for kk in range(nk):
                    pltpu.sync_copy(d_hbm.at[pl.ds(kk * n + c * ch, ch)], idx_v)
                    pltpu.sync_copy(buf, o_hbm.at[idx_v])

    return scatter_kernel(xa, xb, dest.reshape(-1))


def _gather_rows(src, idx):
    ch = SC_CHUNK
    m = idx.shape[0]
    w = src.shape[1]
    per = m // SC_WORKERS
    n_pairs = per // (2 * ch)
    assert m % SC_WORKERS == 0 and per % (2 * ch) == 0
    dma = pltpu.SemaphoreType.DMA

    @pl.kernel(out_type=jax.ShapeDtypeStruct((m, w), src.dtype), mesh=_sc_mesh(),
               scratch_types=[pltpu.VMEM((per,), I32), pltpu.VMEM((ch, w), src.dtype), pltpu.VMEM((ch, w), src.dtype),
                              dma, dma, dma, dma])
    def gather_kernel(s_hbm, i_hbm, o_hbm, idx_v, buf_a, buf_b, sem_ga, sem_gb, sem_wa, sem_wb):
        base = _sc_worker_id() * per
        pltpu.sync_copy(i_hbm.at[pl.ds(base, per)], idx_v)

        def fetch(j, buf, sem):
            return pltpu.make_async_copy(s_hbm.at[idx_v.at[pl.ds(j * ch, ch)]], buf, sem)

        def put(j, buf, sem):
            return pltpu.make_async_copy(buf, o_hbm.at[pl.ds(base + j * ch, ch)], sem)

        fetch(0, buf_a, sem_ga).start()

        @pl.loop(0, n_pairs)
        def _(p):
            j0 = 2 * p
            j1 = j0 + 1

            @pl.when(p > 0)
            def _():
                put(j1 - 2, buf_b, sem_wb).wait()

            fetch(j1, buf_b, sem_gb).start()
            fetch(j0, buf_a, sem_ga).wait()
            put(j0, buf_a, sem_wa).start()
            fetch(j1, buf_b, sem_gb).wait()
            put(j1, buf_b, sem_wb).start()
            put(j0, buf_a, sem_wa).wait()

            @pl.when(p + 1 < n_pairs)
            def _():
                fetch(j0 + 2, buf_a, sem_ga).start()

        put(2 * n_pairs - 1, buf_b, sem_wb).wait()

    return gather_kernel(src, idx)


def _moe_kernel(blk_e_ref, blk_valid_ref, blk_next_ref, xs_ref, wgu_hbm, bgu_ref, wd_hbm, bd_ref, y_ref,
                wgu_f32, wd_f32, wgu_bf, wd_bf, sems, *, e0):
    i = pl.program_id(0)
    e = blk_e_ref[i]
    n_valid = blk_valid_ref[i]
    used = n_valid > 0
    new_expert = (i == 0) | (blk_e_ref[jnp.maximum(i - 1, 0)] != e)

    def weight_copies(expert):
        return (pltpu.make_async_copy(wgu_hbm.at[e0 + expert], wgu_f32, sems.at[0]),
                pltpu.make_async_copy(wd_hbm.at[e0 + expert], wd_f32, sems.at[1]))

    @pl.when(used & (i == 0))
    def _():
        for cp in weight_copies(e):
            cp.start()

    @pl.when(used & new_expert)
    def _():
        for cp in weight_copies(e):
            cp.wait()
        wgu_bf[...] = wgu_f32[...].astype(BF16)
        wd_bf[...] = wd_f32[...].astype(BF16)
        nxt = blk_next_ref[i]

        @pl.when(nxt >= 0)
        def _():
            for cp in weight_copies(nxt):
                cp.start()

    @pl.when(used)
    def _():
        valid = lax.broadcasted_iota(I32, xs_ref.shape, 0) < n_valid
        xb = _unpack_halves(jnp.where(valid, xs_ref[...], jnp.uint32(0))).astype(BF16)
        gu = jnp.dot(xb, wgu_bf[...], preferred_element_type=F32) + bgu_ref[0]
        g = jnp.minimum(gu[:, :D_FF], SWIGLU_LIMIT)
        up = jnp.clip(gu[:, D_FF:], -SWIGLU_LIMIT, SWIGLU_LIMIT)
        act = (up + 1.0) * (g * jax.nn.sigmoid(SWIGLU_ALPHA * g))
        y_ref[...] = _pack_halves(jnp.dot(act.astype(BF16), wd_bf[...], preferred_element_type=F32) + bd_ref[0])

    @pl.when(jnp.logical_not(used))
    def _():
        y_ref[...] = jnp.zeros_like(y_ref)


def _moe(xs, blk_e, blk_valid, blk_next, layer, w_gu, b_gu, w_d, b_d):
    n_rows = xs.shape[0]
    bm = MOE_BM
    assert n_rows % bm == 0
    e0 = layer * N_EXPERTS
    n_we = w_gu.shape[0] * w_gu.shape[1]
    any_spec = pl.BlockSpec(memory_space=pl.ANY)
    grid_spec = pltpu.PrefetchScalarGridSpec(
        num_scalar_prefetch=3,
        grid=(n_rows // bm,),
        in_specs=[pl.BlockSpec((bm, D_MODEL // 2), lambda i, be, bv, bn: (i, 0)),
                  any_spec,
                  pl.BlockSpec((1, 1, 2 * D_FF), lambda i, be, bv, bn: (e0 + be[i], 0, 0)),
                  any_spec,
                  pl.BlockSpec((1, 1, D_MODEL), lambda i, be, bv, bn: (e0 + be[i], 0, 0))],
        out_specs=pl.BlockSpec((bm, D_MODEL // 2), lambda i, be, bv, bn: (i, 0)),
        scratch_shapes=[pltpu.VMEM((D_MODEL, 2 * D_FF), F32), pltpu.VMEM((D_FF, D_MODEL), F32),
                        pltpu.VMEM((D_MODEL, 2 * D_FF), BF16), pltpu.VMEM((D_FF, D_MODEL), BF16),
                        pltpu.SemaphoreType.DMA((2,))],
    )
    return pl.pallas_call(
        functools.partial(_moe_kernel, e0=e0),
        grid_spec=grid_spec,
        out_shape=jax.ShapeDtypeStruct((n_rows, D_MODEL // 2), U32),
        compiler_params=pltpu.CompilerParams(dimension_semantics=("arbitrary",), vmem_limit_bytes=VMEM_LIMIT),
        name="moe_experts",
    )(blk_e, blk_valid, blk_next, xs, w_gu.reshape(n_we, D_MODEL, 2 * D_FF), b_gu.reshape(n_we, 1, 2 * D_FF),
      w_d.reshape(n_we, D_FF, D_MODEL), b_d.reshape(n_we, 1, D_MODEL))


def _ple_kernel(x1_ref, y0_ref, y1_ref, y2_ref, y3_ref, gates_ref, p_ref, g_ref, wg_ref, wp_ref, *rest):
    out_ref = rest[-1]
    x2 = x1_ref[...]
    gates = gates_ref[...]
    for kk, y_ref in enumerate((y0_ref, y1_ref, y2_ref, y3_ref)):
        x2 = x2 + _unpack_halves(y_ref[...]) * gates[:, kk:kk + 1]
    hp = _rms(x2, g_ref[...]).astype(BF16)
    gate = jax.nn.sigmoid(jnp.dot(hp, wg_ref[...], preferred_element_type=F32))
    pp = jnp.dot(p_ref[...].astype(BF16), wp_ref[...], preferred_element_type=F32)
    out_ref[...] = x2 + gate * pp


def _ple(x1, tok0, n, y_tok, y0, gates_t, p_all, p0, g_ple, w_gate, w_proj, rows, out_into=None):
    assert n % rows == 0 and tok0 % rows == 0 and y0 % rows == 0 and p0 % rows == 0
    const = lambda shape: pl.BlockSpec(shape, lambda i: (0,) * len(shape))
    tok_blk = lambda width: pl.BlockSpec((rows, width), lambda i: (tok0 // rows + i, 0))
    y_blk = lambda kk: pl.BlockSpec((rows, D_MODEL // 2), lambda i: ((y0 + kk * n) // rows + i, 0))
    extra_specs, extra_args, aliases = [], [], {}
    if out_into is not None:
        assert out_into.shape == x1.shape
        extra_specs, extra_args, aliases = [pl.BlockSpec(memory_space=pl.ANY)], [out_into], {10: 0}
    return pl.pallas_call(
        _ple_kernel,
        grid=(n // rows,),
        in_specs=[tok_blk(D_MODEL), y_blk(0), y_blk(1), y_blk(2), y_blk(3), tok_blk(TOP_K),
                  pl.BlockSpec((rows, PLE_DIM), lambda i: (p0 // rows + i, 0)),
                  const((1, D_MODEL)), const((D_MODEL, D_MODEL)), const((PLE_DIM, D_MODEL))] + extra_specs,
        out_specs=tok_blk(D_MODEL),
        out_shape=jax.ShapeDtypeStruct(x1.shape, F32),
        input_output_aliases=aliases,
        compiler_params=pltpu.CompilerParams(dimension_semantics=("arbitrary",), vmem_limit_bytes=VMEM_LIMIT),
        name="combine_ple",
    )(x1, y_tok, y_tok, y_tok, y_tok, gates_t, p_all, g_ple, w_gate, w_proj, *extra_args)


def _rope_tables(pos):
    half = ROT_DIM // 2
    d = np.arange(LANES) % HEAD_DIM
    inv = ROPE_THETA ** (-jnp.arange(half, dtype=F32) / half)
    inv_lane = jnp.where(d < ROT_DIM, inv[d % half], 0.0)
    ang = pos.astype(F32)[:, None] * inv_lane[None, :]
    cos, sin = jnp.cos(ang), jnp.sin(ang)
    sin_a = jnp.where((d >= half) & (d < ROT_DIM), sin, 0.0)
    sin_b = jnp.where(d < half, -sin, 0.0)
    return cos, sin_a, sin_b


def _layer(layer, xp, xs, ck, cv, st, p_prompt_all, p_sample_all, past_len, refine_tail, tabs_p, tabs_s,
           norm_attn, w_in, q_norm, k_norm, attn_sinks, w_pool, pool_scale, w_out,
           norm_ffn, w_router, b_router, w_gate_up_all, b_gate_up_all, w_down_all, b_down_all,
           norm_ple, w_ple_gate, w_ple_proj):
    t, ns = xp.shape[0], xs.shape[0]
    w_q = w_in[:, :ATTN_WIDTH].reshape(D_MODEL, 2, Q_TILES, HEAD_DIM).transpose(0, 2, 1, 3).reshape(D_MODEL, ATTN_WIDTH)
    w_in_p = jnp.concatenate([w_q, w_in[:, ATTN_WIDTH:]], axis=1)
    w_oa = w_out[:ATTN_WIDTH].reshape(2, Q_TILES, HEAD_DIM, D_MODEL).transpose(1, 0, 2, 3).reshape(ATTN_WIDTH, D_MODEL)
    w_out_p = jnp.concatenate([w_oa, w_out[ATTN_WIDTH:]], axis=0)
    g_attn = norm_attn.reshape(1, D_MODEL)
    qn = jnp.tile(q_norm, 2).reshape(1, LANES)
    kn = jnp.tile(k_norm, 2).reshape(1, LANES)
    lane = np.arange(LANES)
    hmean = jnp.asarray((lane[:, None] // HEAD_DIM == lane[None, :] // HEAD_DIM) / HEAD_DIM, F32)
    pscale = pool_scale.reshape(1, POOL_WIDTH)
    sink8 = jnp.broadcast_to(attn_sinks.reshape(2, Q_TILES).T.reshape(8, 1), (8, LANES))

    mix_args = (g_attn, w_in_p.astype(BF16), qn, kn, hmean.astype(BF16), w_pool.astype(BF16), pscale,
                w_out_p.astype(BF16))
    x1p, nk_p, nv_p, nu_p = _mixer_prompt(xp, 0, MIX_ROWS, attn_sinks, tabs_p, *mix_args)
    if refine_tail:
        hi_args = (g_attn, w_in_p, qn, kn, hmean, w_pool, pscale, w_out_p)
        x1p, nk_p, nv_p, nu_p = _mixer_prompt(xp, t - 2 * MIX_TAIL_ROWS, MIX_TAIL_ROWS, attn_sinks, tabs_p, *hi_args,
                                              x1_into=x1p)
    st_t = jnp.transpose(st, (1, 0, 2))
    x1s, nk_s, nv_s, nst_t = _mixer_sample(xs, ck.reshape(ns, -1, KV_WIDTH), cv.reshape(ns, -1, KV_WIDTH), st_t,
                                           past_len, tabs_s, sink8, *mix_args)

    g_ffn = norm_ffn.reshape(1, D_MODEL)
    wr_t = w_router.T
    br = b_router.reshape(N_EXPERTS, 1)
    n_tok = t + ns
    bm = MOE_BM
    n_blocks = -(-(n_tok * TOP_K + N_EXPERTS * (bm - 1)) // bm)
    hf_p, idx_p, gate_p, pos_p, cnt_p = _route(x1p, g_ffn, wr_t, br, jnp.zeros((N_EXPERTS, 1), F32),
                                               ROUTE_ROWS, bm, n_blocks)[:5]
    hf_s, idx_s, gate_s, pos_s, _, pad_start, blk_e, blk_valid, blk_next = _route(x1s, g_ffn, wr_t, br, cnt_p, ns,
                                                                                  bm, n_blocks)
    idx = jnp.concatenate([idx_p, idx_s], axis=1)
    pos = jnp.concatenate([pos_p, pos_s], axis=1)
    blk_e, blk_valid, blk_next = (a[0, :n_blocks] for a in (blk_e, blk_valid, blk_next))
    start_of = jnp.sum(jnp.where(idx[None] == jnp.arange(N_EXPERTS, dtype=I32)[:, None, None],
                                 pad_start.reshape(N_EXPERTS, 1, 1), 0), axis=0)
    dest = start_of + pos

    xs_rows = _scatter_rows(hf_p, hf_s, dest, n_blocks * bm)
    y = _moe(xs_rows, blk_e, blk_valid, blk_next, layer, w_gate_up_all, b_gate_up_all, w_down_all, b_down_all)

    th = t // 2
    unit = SC_WORKERS * SC_CHUNK * 2
    assert (TOP_K * th) % unit == 0
    back_b = jnp.concatenate([dest[:, th:t].reshape(-1), dest[:, t:].reshape(-1)])
    back_b = jnp.concatenate([back_b, jnp.arange(-back_b.shape[0] % unit, dtype=I32)])
    y_a = _gather_rows(y, dest[:, :th].reshape(-1))
    y_b = _gather_rows(y, back_b)

    g_ple = norm_ple.reshape(1, D_MODEL)
    wg_bf = w_ple_gate.astype(BF16)
    wp_bf = w_ple_proj.astype(BF16)
    gates_p = gate_p.T
    yp = _ple(x1p, 0, th, y_a, 0, gates_p, p_prompt_all, layer * t, g_ple, wg_bf, wp_bf, PLE_ROWS)
    yp = _ple(x1p, th, t - th, y_b, 0, gates_p, p_prompt_all, layer * t + th, g_ple, wg_bf, wp_bf, PLE_ROWS,
              out_into=yp)
    ys = _ple(x1s, 0, ns, y_b, TOP_K * (t - th), gate_s.T, p_sample_all, layer * ns, g_ple, wg_bf, wp_bf, ns)
    new_pool_s = jnp.transpose(nst_t, (1, 0, 2))
    return yp, ys, nk_p, nv_p, nu_p[POOL_HALO - POOL_PREFIX:], nk_s, nv_s, new_pool_s


def kernel(x_prompt, x_sample, cache_k, cache_v, state_pool, p_prompt, p_sample, norm_attn, w_in, q_norm, k_norm,
           attn_sinks, w_pool, pool_scale, w_out, norm_ffn, w_router, b_router, w_gate_up, b_gate_up, w_down, b_down,
           norm_ple, w_ple_gate, w_ple_proj):
    depth = norm_attn.shape[0]
    batch, seq, d = x_prompt.shape
    ns, dec_seq, _ = x_sample.shape
    wb = cache_k.shape[2]
    assert batch == 1 and dec_seq == 1 and d == D_MODEL and wb == WINDOW
    assert cache_k.shape[3:] == (N_KV_HEADS, HEAD_DIM) and state_pool.shape[2:] == (POOL_PREFIX, POOL_WIDTH)
    past_len = PAST_LEN
    yp = x_prompt.reshape(seq, d)
    ys = x_sample.reshape(ns, d)
    p_prompt_all = p_prompt.reshape(depth * seq, PLE_DIM)
    p_sample_all = p_sample.reshape(depth * ns, PLE_DIM)
    tabs_p = _rope_tables(jnp.arange(seq))
    tabs_s = _rope_tables(jnp.full((1,), past_len))
    outs = [[] for _ in range(6)]
    for i in range(depth):
        res = _layer(i, yp, ys, cache_k[i], cache_v[i], state_pool[i], p_prompt_all, p_sample_all, past_len,
                     i < depth - 1, tabs_p, tabs_s,
                     norm_attn[i], w_in[i], q_norm[i], k_norm[i], attn_sinks[i], w_pool[i], pool_scale[i], w_out[i],
                     norm_ffn[i], w_router[i], b_router[i], w_gate_up, b_gate_up, w_down, b_down,
                     norm_ple[i], w_ple_gate[i], w_ple_proj[i])
        yp, ys = res[0], res[1]
        kv_shape = (1, WINDOW, N_KV_HEADS, HEAD_DIM)
        outs[0].append(res[2].reshape(kv_shape))
        outs[1].append(res[3].reshape(kv_shape))
        outs[2].append(res[4].reshape(1, POOL_PREFIX, POOL_WIDTH))
        outs[3].append(res[5].reshape(ns, wb, N_KV_HEADS, HEAD_DIM))
        outs[4].append(res[6].reshape(ns, wb, N_KV_HEADS, HEAD_DIM))
        outs[5].append(res[7])
    return (yp.reshape(batch, seq, d), ys.reshape(ns, dec_seq, d)) + tuple(jnp.stack(o) for o in outs)
```

```python
import functools

import jax
import jax.numpy as jnp
import numpy as np
from jax import lax
from jax.experimental import pallas as pl
from jax.experimental.pallas import tpu as pltpu
from jax.experimental.pallas import tpu_sc as plsc

F32 = jnp.float32
BF16 = jnp.bfloat16
U32 = jnp.uint32
I32 = jnp.int32

D_MODEL = 1024
HEAD_DIM = 64
N_HEADS = 8
N_KV_HEADS = 2
GROUP = N_HEADS // N_KV_HEADS
ATTN_WIDTH = N_HEADS * HEAD_DIM
KV_WIDTH = N_KV_HEADS * HEAD_DIM
POOL_WIDTH = 512
POOL_WINDOWS = (2, 4, 8, 16)
POOL_GC = POOL_WIDTH // len(POOL_WINDOWS)
POOL_PREFIX = max(POOL_WINDOWS) - 1
POOL_HALO = POOL_PREFIX + 1
IN_WIDTH = ATTN_WIDTH + 2 * KV_WIDTH + POOL_WIDTH
WINDOW = 128
ROPE_THETA = 500000.0
ROT_DIM = HEAD_DIM // 4
N_EXPERTS = 32
TOP_K = 4
D_FF = 1024
SWIGLU_ALPHA = 1.702
SWIGLU_LIMIT = 7.0
PLE_DIM = 256
PAST_LEN = 16384
EPS = 1e-5
NEG_INF = -1e30

LANES = 128
Q_TILES = ATTN_WIDTH // LANES

MIX_ROWS = 512
MIX_TAIL_ROWS = 256
SAMPLE_CHUNK = 16
ROUTE_ROWS = 512
MOE_BM = 512
SC_CORES = 2
SC_SUBCORES = 16
SC_WORKERS = SC_CORES * SC_SUBCORES
SC_CHUNK = 64
PLE_ROWS = 512
VMEM_LIMIT = 56 * 1024 * 1024


def _rms(x, g):
    return x * lax.rsqrt(jnp.mean(x * x, axis=-1, keepdims=True) + EPS) * g


def _pack_halves(x):
    w = x.shape[1] // 2
    lo = lax.bitcast_convert_type(x[:, :w].astype(BF16).astype(F32), U32) >> 16
    hi = lax.bitcast_convert_type(x[:, w:].astype(BF16).astype(F32), U32) & jnp.uint32(0xFFFF0000)
    return lo | hi


def _unpack_halves(packed):
    lo = lax.bitcast_convert_type(packed << 16, F32)
    hi = lax.bitcast_convert_type(packed & jnp.uint32(0xFFFF0000), F32)
    return jnp.concatenate([lo, hi], axis=1)


def _mm(a, b, nt=False):
    dims = (((1,), (1 if nt else 0,)), ((), ()))
    if b.dtype == F32:
        return lax.dot_general(a.astype(F32), b, dims, preferred_element_type=F32, precision=lax.Precision.HIGHEST)
    return lax.dot_general(a.astype(BF16), b, dims, preferred_element_type=F32)


def _head_norm_rope(t, hmean, gain, cos, sin_a, sin_b):
    t = t * lax.rsqrt(_mm(t * t, hmean) + EPS) * gain
    return t * cos + pltpu.roll(t, ROT_DIM // 2, axis=1) * sin_a + pltpu.roll(t, LANES - ROT_DIM // 2, axis=1) * sin_b


def _softmax_pv(s, sink, v, ones):
    m = jnp.maximum(jnp.max(s, axis=-1, keepdims=True), sink)
    e = jnp.exp(s - m).astype(v.dtype)
    den = _mm(e, ones) + jnp.exp(sink - m)
    return _mm(e, v) / den


def _mixer_prompt_kernel(sinks_ref, x_ref, cos_ref, sa_ref, sb_ref, g_ref, win_ref, qn_ref, kn_ref, hm_ref,
                         wpool_ref, pscale_ref, wout_ref, *rest, row_offset, aliased):
    x1_ref, klast_ref, vlast_ref, ulast_ref, kprev, vprev, uext, mix = rest[1:] if aliased else rest
    i = pl.program_id(0)
    rows = x_ref.shape[0]
    n_sub = rows // WINDOW
    cdt = win_ref.dtype
    row0 = row_offset + i * rows

    @pl.when(i == 0)
    def _():
        kprev[...] = jnp.zeros_like(kprev)
        vprev[...] = jnp.zeros_like(vprev)
        uext[0:POOL_HALO, :] = jnp.zeros((POOL_HALO, POOL_WIDTH), F32)

    x = x_ref[...]
    proj = _mm(_rms(x, g_ref[...]), win_ref[...])
    cos, sin_a, sin_b = cos_ref[...], sa_ref[...], sb_ref[...]

    n_t = Q_TILES + 1
    t_all = jnp.concatenate([proj[:, j * LANES:(j + 1) * LANES] for j in range(n_t)], axis=0)
    t3 = (t_all * lax.rsqrt(_mm(t_all * t_all, hm_ref[...]) + EPS)).reshape(n_t, rows, LANES)
    t3 = jnp.concatenate([t3[:Q_TILES] * (qn_ref[...] * HEAD_DIM ** -0.5), t3[Q_TILES:] * kn_ref[...]], axis=0)
    t2 = t3.reshape(n_t * rows, LANES)
    t3 = (t3 * cos + pltpu.roll(t2, ROT_DIM // 2, axis=1).reshape(n_t, rows, LANES) * sin_a
          + pltpu.roll(t2, LANES - ROT_DIM // 2, axis=1).reshape(n_t, rows, LANES) * sin_b)
    q3 = t3[:Q_TILES]
    k = t3[Q_TILES]
    v = proj[:, ATTN_WIDTH + KV_WIDTH:ATTN_WIDTH + 2 * KV_WIDTH]
    u = proj[:, ATTN_WIDTH + 2 * KV_WIDTH:]
    klast_ref[...] = k[rows - WINDOW:, :]
    vlast_ref[...] = v[rows - WINDOW:, :]
    ulast_ref[...] = u[rows - POOL_HALO:, :]
    k_c = k.astype(cdt)
    v_c = jnp.concatenate([v.astype(cdt), jnp.ones((rows, LANES), cdt)], axis=1)
    v_first = jnp.concatenate([vprev[...], jnp.ones((WINDOW, LANES), cdt)], axis=1)

    lane = lax.broadcasted_iota(I32, (WINDOW, LANES), 1)
    left = (lane < HEAD_DIM)[None]
    qi = lax.broadcasted_iota(I32, (WINDOW, 2 * WINDOW), 0)
    kj = lax.broadcasted_iota(I32, (WINDOW, 2 * WINDOW), 1)
    band = (kj - qi >= 1) & (kj - qi <= WINDOW)
    sink3 = jnp.concatenate([jnp.full((1, 1, 1), sinks_ref[j + Q_TILES * s], F32)
                             for j in range(Q_TILES) for s in range(2)], axis=0)
    n_g = 2 * Q_TILES

    v_cats, masks, scores = [], [], []
    for c in range(n_sub):
        r0 = c * WINDOW
        if c == 0:
            k_cat = jnp.concatenate([kprev[...], k_c[0:WINDOW]], axis=0)
            v_cats.append(jnp.concatenate([v_first, v_c[0:WINDOW]], axis=0))
            masks.append(band & (kj + (row0 - WINDOW) >= 0))
        else:
            k_cat = k_c[r0 - WINDOW:r0 + WINDOW]
            v_cats.append(v_c[r0 - WINDOW:r0 + WINDOW])
            masks.append(band)
        q_c = q3[:, r0:r0 + WINDOW, :]
        q_all = jnp.concatenate([jnp.where(left, q_c, 0.0), jnp.where(left, 0.0, q_c)], axis=1)
        scores.append(_mm(q_all.reshape(n_g * WINDOW, LANES), k_cat, nt=True))
    kprev[...] = k_c[rows - WINDOW:]
    vprev[...] = v[rows - WINDOW:].astype(cdt)

    probs, maxes = [], []
    for c in range(n_sub):
        s = jnp.where(masks[c][None], scores[c].reshape(n_g, WINDOW, 2 * WINDOW), NEG_INF)
        m = jnp.maximum(jnp.max(s, axis=-1, keepdims=True), sink3)
        probs.append(jnp.exp(s - m).astype(cdt).reshape(n_g * WINDOW, 2 * WINDOW))
        maxes.append(m)
    applied = [_mm(probs[c], v_cats[c]) for c in range(n_sub)]
    for c in range(n_sub):
        r0 = c * WINDOW
        den = applied[c][:, LANES:] + jnp.exp(sink3 - maxes[c]).reshape(n_g * WINDOW, 1)
        o = (applied[c][:, :LANES] / den).reshape(Q_TILES, 2 * WINDOW, LANES)
        a = jnp.where(left, o[:, :WINDOW], o[:, WINDOW:])
        for j in range(Q_TILES):
            mix[r0:r0 + WINDOW, j * LANES:(j + 1) * LANES] = a[j].astype(cdt)

    uext[POOL_HALO:POOL_HALO + rows, :] = u
    pos1 = (lax.broadcasted_iota(I32, (rows, 1), 0) + row0 + 1).astype(F32)
    for gi, w in enumerate(POOL_WINDOWS):
        cols = slice(gi * POOL_GC, (gi + 1) * POOL_GC)
        wsum = u[:, cols]
        for sft in range(1, w):
            wsum = wsum + uext[POOL_HALO - sft:POOL_HALO - sft + rows, cols]
        d = wsum / jnp.minimum(pos1, float(w)) - u[:, cols]
        y = _mm(d, wpool_ref[gi]) * pscale_ref[:, cols]
        mix[:, ATTN_WIDTH + gi * POOL_GC:ATTN_WIDTH + (gi + 1) * POOL_GC] = y.astype(cdt)
    uext[0:POOL_HALO, :] = u[rows - POOL_HALO:, :]

    x1_ref[...] = x + _mm(mix[...], wout_ref[...])


def _mixer_prompt(x_full, row_offset, rows, sinks, tabs, g_attn, w_in, qn, kn, hmean, w_pool, pscale, w_out,
                  x1_into=None):
    t = x_full.shape[0] - row_offset
    cdt = w_in.dtype
    assert t % rows == 0 and row_offset % rows == 0 and rows % WINDOW == 0 and rows >= POOL_HALO
    blk0, n_steps = row_offset // rows, t // rows
    const = lambda shape: pl.BlockSpec(shape, lambda i, *_: (0,) * len(shape))
    row_blk = lambda width: pl.BlockSpec((rows, width), lambda i, *_: (blk0 + i, 0))
    aliased = x1_into is not None
    if aliased:
        assert x1_into.shape == x_full.shape
        x1_spec = pl.BlockSpec((rows, D_MODEL), lambda i, *_: (blk0 + n_steps - 1, 0))
        x1_shape = x1_into.shape
        extra_specs, extra_args, aliases = [pl.BlockSpec(memory_space=pl.ANY)], [x1_into], {13: 0}
    else:
        x1_spec = pl.BlockSpec((rows, D_MODEL), lambda i, *_: (i, 0))
        x1_shape = (t, D_MODEL)
        extra_specs, extra_args, aliases = [], [], {}
    grid_spec = pltpu.PrefetchScalarGridSpec(
        num_scalar_prefetch=1,
        grid=(n_steps,),
        in_specs=[row_blk(D_MODEL), row_blk(LANES), row_blk(LANES), row_blk(LANES),
                  const((1, D_MODEL)), const((D_MODEL, IN_WIDTH)), const((1, LANES)), const((1, LANES)),
                  const((LANES, LANES)), const((len(POOL_WINDOWS), POOL_GC, POOL_GC)), const((1, POOL_WIDTH)),
                  const((D_MODEL, D_MODEL))] + extra_specs,
        out_specs=[x1_spec, const((WINDOW, KV_WIDTH)), const((WINDOW, KV_WIDTH)),
                   const((POOL_HALO, POOL_WIDTH))],
        scratch_shapes=[pltpu.VMEM((WINDOW, KV_WIDTH), cdt), pltpu.VMEM((WINDOW, KV_WIDTH), cdt),
                        pltpu.VMEM((POOL_HALO + rows, POOL_WIDTH), F32), pltpu.VMEM((rows, D_MODEL), cdt)],
    )
    return pl.pallas_call(
        functools.partial(_mixer_prompt_kernel, row_offset=row_offset, aliased=aliased),
        grid_spec=grid_spec,
        out_shape=[jax.ShapeDtypeStruct(x1_shape, F32), jax.ShapeDtypeStruct((WINDOW, KV_WIDTH), F32),
                   jax.ShapeDtypeStruct((WINDOW, KV_WIDTH), F32), jax.ShapeDtypeStruct((POOL_HALO, POOL_WIDTH), F32)],
        input_output_aliases=aliases,
        compiler_params=pltpu.CompilerParams(dimension_semantics=("arbitrary",), vmem_limit_bytes=VMEM_LIMIT),
        name="mixer_prompt",
    )(sinks, x_full, *tabs, g_attn, w_in, qn, kn, hmean, w_pool, pscale, w_out, *extra_args)


def _mixer_sample_kernel(x_ref, ck_ref, cv_ref, st_ref, cos_ref, sa_ref, sb_ref, sink8_ref, g_ref, win_ref, qn_ref,
                         kn_ref, hm_ref, wpool_ref, pscale_ref, wout_ref,
                         x1_ref, nk_ref, nv_ref, nst_ref, o8, *, pos):
    nb = x_ref.shape[0]
    wb = ck_ref.shape[1]
    x = x_ref[...]
    h = _rms(x, g_ref[...]).astype(BF16)
    proj = jnp.dot(h, win_ref[...], preferred_element_type=F32)
    cos, sin_a, sin_b = cos_ref[...], sa_ref[...], sb_ref[...]
    hmean = hm_ref[...]
    k = _head_norm_rope(proj[:, ATTN_WIDTH:ATTN_WIDTH + KV_WIDTH], hmean, kn_ref[...], cos, sin_a, sin_b)
    v = proj[:, ATTN_WIDTH + KV_WIDTH:ATTN_WIDTH + 2 * KV_WIDTH]
    u = proj[:, ATTN_WIDTH + 2 * KV_WIDTH:]

    nk_ref[:, 0:wb - 1, :] = ck_ref[:, 1:wb, :]
    nv_ref[:, 0:wb - 1, :] = cv_ref[:, 1:wb, :]
    for b in range(nb):
        nk_ref[b, wb - 1:wb, :] = k[b:b + 1, :]
        nv_ref[b, wb - 1:wb, :] = v[b:b + 1, :]

    r8 = lax.broadcasted_iota(I32, (nb * 8, LANES), 0)
    lane8 = lax.broadcasted_iota(I32, (nb * 8, LANES), 1)
    keep = (lane8 < HEAD_DIM) == (r8 % 2 == 0)
    rep = (lax.broadcasted_iota(I32, (nb * 8, nb), 0) // 8 == lax.broadcasted_iota(I32, (nb * 8, nb), 1)).astype(BF16)
    q8 = jnp.zeros((nb * 8, LANES), F32)
    scale = HEAD_DIM ** -0.5
    for j in range(Q_TILES):
        qt = _head_norm_rope(proj[:, j * LANES:(j + 1) * LANES], hmean, qn_ref[...], cos, sin_a, sin_b) * scale
        qrep = jnp.dot(rep, qt.astype(BF16), preferred_element_type=F32)
        q8 = jnp.where(keep & ((r8 % 8) // 2 == j), qrep, q8)
    q8 = q8.astype(BF16)

    sink8 = sink8_ref[:, 0:1]
    ones_bf = jnp.ones((wb, LANES), BF16)
    assert pos >= wb - 1 and wb <= WINDOW
    for b in range(nb):
        kb = nk_ref[b].astype(BF16)
        vb = nv_ref[b].astype(BF16)
        s = lax.dot_general(q8[b * 8:(b + 1) * 8], kb, (((1,), (1,)), ((), ())), preferred_element_type=F32)
        o8[b * 8:(b + 1) * 8, :] = _softmax_pv(s, sink8, vb, ones_bf)
    o8m = jnp.where(keep, o8[...], 0.0).astype(BF16)

    a_tiles = []
    sel_r = lax.broadcasted_iota(I32, (nb, nb * 8), 1)
    sel_b = lax.broadcasted_iota(I32, (nb, nb * 8), 0)
    for j in range(Q_TILES):
        sel = ((sel_r // 8 == sel_b) & ((sel_r % 8) // 2 == j)).astype(BF16)
        a_tiles.append(jnp.dot(sel, o8m, preferred_element_type=F32))

    z_tiles = []
    for gi, w in enumerate(POOL_WINDOWS):
        cols = slice(gi * POOL_GC, (gi + 1) * POOL_GC)
        wsum = u[:, cols]
        for sft in range(1, w):
            wsum = wsum + st_ref[POOL_PREFIX - sft, :, cols]
        d = wsum / float(min(pos + 1, w)) - u[:, cols]
        z_tiles.append(jnp.dot(d.astype(BF16), wpool_ref[gi], preferred_element_type=F32) * pscale_ref[:, cols])
    nst_ref[0:POOL_PREFIX - 1] = st_ref[1:POOL_PREFIX]
    nst_ref[POOL_PREFIX - 1] = u

    mixv = jnp.concatenate(a_tiles + z_tiles, axis=1).astype(BF16)
    x1_ref[...] = x + jnp.dot(mixv, wout_ref[...], preferred_element_type=F32)


def _mixer_sample(x, ck, cv, st, pos, tabs, sink8, g_attn, w_in, qn, kn, hmean, w_pool, pscale, w_out):
    n, wb = ck.shape[0], ck.shape[1]
    nb = SAMPLE_CHUNK
    assert n % nb == 0
    const = lambda shape: pl.BlockSpec(shape, lambda i: (0,) * len(shape))
    cache_blk = pl.BlockSpec((nb, wb, KV_WIDTH), lambda i: (i, 0, 0))
    st_blk = pl.BlockSpec((POOL_PREFIX, nb, POOL_WIDTH), lambda i: (0, i, 0))
    x_blk = pl.BlockSpec((nb, D_MODEL), lambda i: (i, 0))
    return pl.pallas_call(
        functools.partial(_mixer_sample_kernel, pos=pos),
        grid=(n // nb,),
        in_specs=[x_blk, cache_blk, cache_blk, st_blk, const((1, LANES)), const((1, LANES)), const((1, LANES)),
                  const((8, LANES)), const((1, D_MODEL)), const((D_MODEL, IN_WIDTH)), const((1, LANES)),
                  const((1, LANES)), const((LANES, LANES)), const((len(POOL_WINDOWS), POOL_GC, POOL_GC)),
                  const((1, POOL_WIDTH)), const((D_MODEL, D_MODEL))],
        out_specs=[x_blk, cache_blk, cache_blk, st_blk],
        out_shape=[jax.ShapeDtypeStruct((n, D_MODEL), F32), jax.ShapeDtypeStruct(ck.shape, F32),
                   jax.ShapeDtypeStruct(cv.shape, F32), jax.ShapeDtypeStruct(st.shape, F32)],
        scratch_shapes=[pltpu.VMEM((nb * 8, LANES), F32)],
        compiler_params=pltpu.CompilerParams(dimension_semantics=("arbitrary",), vmem_limit_bytes=VMEM_LIMIT),
        name="mixer_sample",
    )(x, ck, cv, st, *tabs, sink8, g_attn, w_in, qn, kn, hmean, w_pool, pscale, w_out)


def _block_plan(cnt, bm, n_lanes):
    e_sub = lax.broadcasted_iota(I32, (N_EXPERTS, LANES), 0)
    e_lane = lax.broadcasted_iota(I32, (N_EXPERTS, LANES), 1)
    padded = jnp.floor((cnt + (bm - 1)) / bm) * bm
    padded_lane = jnp.sum(jnp.where(e_sub == e_lane, padded, 0.0), axis=0, keepdims=True)
    pad_end = jnp.sum(jnp.where(e_lane <= e_sub, padded_lane, 0.0), axis=1, keepdims=True)
    pad_start = pad_end - padded
    blk_start = lax.broadcasted_iota(I32, (N_EXPERTS, n_lanes), 1).astype(F32) * bm
    blk_e = jnp.minimum(jnp.sum((pad_end <= blk_start).astype(F32), axis=0, keepdims=True), N_EXPERTS - 1.0)
    mine = lax.broadcasted_iota(I32, (N_EXPERTS, n_lanes), 0).astype(F32) == blk_e
    last = jnp.sum(jnp.where(mine, pad_start + cnt, 0.0), axis=0, keepdims=True)
    blk_valid = jnp.clip(last - blk_start[0:1], 0.0, float(bm))
    e_blk = lax.broadcasted_iota(I32, (N_EXPERTS, n_lanes), 0).astype(F32)
    later = jnp.min(jnp.where((e_blk > blk_e) & (cnt > 0.0), e_blk, float(N_EXPERTS)), axis=0, keepdims=True)
    blk_next = jnp.where(later < N_EXPERTS, later, -1.0)
    return pad_start, blk_e, blk_valid, blk_next


def _route_kernel(x1_ref, g_ref, wr_ref, br_ref, cnt_in_ref, hf_ref, idx_ref, gate_ref, pos_ref, cnt_ref,
                  pstart_ref, blke_ref, blkv_ref, blkn_ref, counts, *, bm):
    i = pl.program_id(0)
    rows = x1_ref.shape[0]

    @pl.when(i == 0)
    def _():
        counts[...] = cnt_in_ref[...]

    h = _rms(x1_ref[...], g_ref[...])
    hf_ref[...] = _pack_halves(h)

    logits = lax.dot_general(wr_ref[...], h, (((1,), (1,)), ((), ())), preferred_element_type=F32,
                             precision=lax.Precision.HIGHEST) + br_ref[...]
    eid = lax.broadcasted_iota(I32, (N_EXPERTS, rows), 0).astype(F32)
    work = logits
    vals, hots = [], []
    for kk in range(TOP_K):
        m = jnp.max(work, axis=0, keepdims=True)
        first = jnp.min(jnp.where(work == m, eid, float(N_EXPERTS)), axis=0, keepdims=True)
        hot = eid == first
        work = jnp.where(hot, -jnp.inf, work)
        vals.append(m)
        hots.append(hot)
        idx_ref[kk:kk + 1, :] = first.astype(I32)
    es = [jnp.exp(vv - vals[0]) for vv in vals]
    den = es[0] + es[1] + es[2] + es[3]
    for kk in range(TOP_K):
        gate_ref[kk:kk + 1, :] = es[kk] / den

    chosen = hots[0] | hots[1] | hots[2] | hots[3]
    before = (lax.broadcasted_iota(I32, (rows, rows), 0) < lax.broadcasted_iota(I32, (rows, rows), 1)).astype(BF16)
    rank = jnp.dot(chosen.astype(BF16), before, preferred_element_type=F32) + counts[...]
    for kk in range(TOP_K):
        pos_ref[kk:kk + 1, :] = jnp.sum(jnp.where(hots[kk], rank, 0.0), axis=0, keepdims=True).astype(I32)
    counts[...] = counts[...] + jnp.sum(chosen.astype(F32), axis=1, keepdims=True)
    cnt_ref[...] = counts[...]

    @pl.when(i == pl.num_programs(0) - 1)
    def _():
        pad_start, blk_e, blk_valid, blk_next = _block_plan(counts[...], bm, blke_ref.shape[1])
        pstart_ref[...] = pad_start.astype(I32)
        blke_ref[...] = blk_e.astype(I32)
        blkv_ref[...] = blk_valid.astype(I32)
        blkn_ref[...] = blk_next.astype(I32)


def _route(x1, g_ffn, wr_t, br, cnt_in, rows, bm, n_blocks):
    n = x1.shape[0]
    assert n % rows == 0
    nb_lanes = -(-n_blocks // LANES) * LANES
    const = lambda shape: pl.BlockSpec(shape, lambda i: (0,) * len(shape))
    tok_blk = pl.BlockSpec((TOP_K, rows), lambda i: (0, i))
    return pl.pallas_call(
        functools.partial(_route_kernel, bm=bm),
        grid=(n // rows,),
        in_specs=[pl.BlockSpec((rows, D_MODEL), lambda i: (i, 0)), const((1, D_MODEL)), const((N_EXPERTS, D_MODEL)),
                  const((N_EXPERTS, 1)), const((N_EXPERTS, 1))],
        out_specs=[pl.BlockSpec((rows, D_MODEL // 2), lambda i: (i, 0)), tok_blk, tok_blk, tok_blk,
                   const((N_EXPERTS, 1)), const((N_EXPERTS, 1)), const((1, nb_lanes)), const((1, nb_lanes)),
                   const((1, nb_lanes))],
        out_shape=[jax.ShapeDtypeStruct((n, D_MODEL // 2), U32), jax.ShapeDtypeStruct((TOP_K, n), I32),
                   jax.ShapeDtypeStruct((TOP_K, n), F32), jax.ShapeDtypeStruct((TOP_K, n), I32),
                   jax.ShapeDtypeStruct((N_EXPERTS, 1), F32), jax.ShapeDtypeStruct((N_EXPERTS, 1), I32),
                   jax.ShapeDtypeStruct((1, nb_lanes), I32), jax.ShapeDtypeStruct((1, nb_lanes), I32),
                   jax.ShapeDtypeStruct((1, nb_lanes), I32)],
        scratch_shapes=[pltpu.VMEM((N_EXPERTS, 1), F32)],
        compiler_params=pltpu.CompilerParams(dimension_semantics=("arbitrary",), vmem_limit_bytes=VMEM_LIMIT),
        name="route",
    )(x1, g_ffn, wr_t, br, cnt_in)


def _sc_mesh():
    return plsc.VectorSubcoreMesh(core_axis_name="core", subcore_axis_name="subcore")


def _sc_worker_id():
    return lax.axis_index("core") * SC_SUBCORES + lax.axis_index("subcore")


def _scatter_rows(xa, xb, dest, n_rows):
    ch = SC_CHUNK
    na, w = xa.shape
    n = na + xb.shape[0]
    nk = dest.shape[0]
    assert na % ch == 0 and n % ch == 0 and dest.shape[1] == n and xb.shape[1] == w and xa.dtype == xb.dtype
    n_chunks = n // ch
    dest_c = dest.reshape(nk, n_chunks, ch).transpose(1, 0, 2).reshape(-1)
    dma = pltpu.SemaphoreType.DMA

    @pl.kernel(out_type=jax.ShapeDtypeStruct((n_rows, w), xa.dtype), mesh=_sc_mesh(),
               scratch_types=[pltpu.VMEM((nk * ch,), I32), pltpu.VMEM((ch, w), xa.dtype), dma] + [dma] * nk)
    def scatter_kernel(xa_hbm, xb_hbm, d_hbm, o_hbm, idx_v, buf, sem_i, *sem_s):
        wid = _sc_worker_id()

        @pl.loop(0, -(-n_chunks // SC_WORKERS))
        def _(j):
            c = j * SC_WORKERS + wid

            @pl.when(c < n_chunks)
            def _():
                load_idx = pltpu.make_async_copy(d_hbm.at[pl.ds(c * (nk * ch), nk * ch)], idx_v, sem_i)
                load_idx.start()

                @pl.when(c < na // ch)
                def _():
                    pltpu.sync_copy(xa_hbm.at[pl.ds(c * ch, ch)], buf)

                @pl.when(c >= na // ch)
                def _():
                    pltpu.sync_copy(xb_hbm.at[pl.ds(c * ch - na, ch)], buf)

                load_idx.wait()
                puts = [pltpu.make_async_copy(buf, o_hbm.at[idx_v.at[pl.ds(kk * ch, ch)]], sem_s[kk])
                        for kk in range(nk)]
                for cp in puts:
                    cp.start()
                for cp in puts:
                    cp.wait()

    return scatter_kernel(xa, xb, dest_c)


def _gather_rows(src, idx):
    ch = SC_CHUNK
    m = idx.shape[0]
    w = src.shape[1]
    per = m // SC_WORKERS
    n_pairs = per // (2 * ch)
    assert m % SC_WORKERS == 0 and per % (2 * ch) == 0
    dma = pltpu.SemaphoreType.DMA

    @pl.kernel(out_type=jax.ShapeDtypeStruct((m, w), src.dtype), mesh=_sc_mesh(),
               scratch_types=[pltpu.VMEM((per,), I32), pltpu.VMEM((ch, w), src.dtype), pltpu.VMEM((ch, w), src.dtype),
                              dma, dma, dma, dma])
    def gather_kernel(s_hbm, i_hbm, o_hbm, idx_v, buf_a, buf_b, sem_ga, sem_gb, sem_wa, sem_wb):
        base = _sc_worker_id() * per
        pltpu.sync_copy(i_hbm.at[pl.ds(base, per)], idx_v)

        def fetch(j, buf, sem):
            return pltpu.make_async_copy(s_hbm.at[idx_v.at[pl.ds(j * ch, ch)]], buf, sem)

        def put(j, buf, sem):
            return pltpu.make_async_copy(buf, o_hbm.at[pl.ds(base + j * ch, ch)], sem)

        fetch(0, buf_a, sem_ga).start()

        @pl.loop(0, n_pairs)
        def _(p):
            j0 = 2 * p
            j1 = j0 + 1

            @pl.when(p > 0)
            def _():
                put(j1 - 2, buf_b, sem_wb).wait()

            fetch(j1, buf_b, sem_gb).start()
            fetch(j0, buf_a, sem_ga).wait()
            put(j0, buf_a, sem_wa).start()
            fetch(j1, buf_b, sem_gb).wait()
            put(j1, buf_b, sem_wb).start()
            put(j0, buf_a, sem_wa).wait()

            @pl.when(p + 1 < n_pairs)
            def _():
                fetch(j0 + 2, buf_a, sem_ga).start()

        put(2 * n_pairs - 1, buf_b, sem_wb).wait()

    return gather_kernel(src, idx)


def _moe_kernel(blk_e_ref, blk_valid_ref, blk_next_ref, xs_ref, wgu_hbm, bgu_ref, wd_hbm, bd_ref, y_ref,
                wgu_f32, wd_f32, wgu_bf, wd_bf, sems, *, e0):
    i = pl.program_id(0)
    e = blk_e_ref[i]
    n_valid = blk_valid_ref[i]
    used = n_valid > 0
    new_expert = (i == 0) | (blk_e_ref[jnp.maximum(i - 1, 0)] != e)

    def weight_copies(expert):
        return (pltpu.make_async_copy(wgu_hbm.at[e0 + expert], wgu_f32, sems.at[0]),
                pltpu.make_async_copy(wd_hbm.at[e0 + expert], wd_f32, sems.at[1]))

    @pl.when(used & (i == 0))
    def _():
        for cp in weight_copies(e):
            cp.start()

    @pl.when(used & new_expert)
    def _():
        for cp in weight_copies(e):
            cp.wait()
        wgu_bf[...] = wgu_f32[...].astype(BF16)
        wd_bf[...] = wd_f32[...].astype(BF16)
        nxt = blk_next_ref[i]

        @pl.when(nxt >= 0)
        def _():
            for cp in weight_copies(nxt):
                cp.start()

    @pl.when(used)
    def _():
        valid = lax.broadcasted_iota(I32, xs_ref.shape, 0) < n_valid
        xb = _unpack_halves(jnp.where(valid, xs_ref[...], jnp.uint32(0))).astype(BF16)
        gu = jnp.dot(xb, wgu_bf[...], preferred_element_type=F32) + bgu_ref[0]
        g = jnp.minimum(gu[:, :D_FF], SWIGLU_LIMIT)
        up = jnp.clip(gu[:, D_FF:], -SWIGLU_LIMIT, SWIGLU_LIMIT)
        act = (up + 1.0) * (g * jax.nn.sigmoid(SWIGLU_ALPHA * g))
        y_ref[...] = _pack_halves(jnp.dot(act.astype(BF16), wd_bf[...], preferred_element_type=F32) + bd_ref[0])

    @pl.when(jnp.logical_not(used))
    def _():
        y_ref[...] = jnp.zeros_like(y_ref)


def _moe(xs, blk_e, blk_valid, blk_next, layer, w_gu, b_gu, w_d, b_d):
    n_rows = xs.shape[0]
    bm = MOE_BM
    assert n_rows % bm == 0
    e0 = layer * N_EXPERTS
    n_we = w_gu.shape[0] * w_gu.shape[1]
    any_spec = pl.BlockSpec(memory_space=pl.ANY)
    grid_spec = pltpu.PrefetchScalarGridSpec(
        num_scalar_prefetch=3,
        grid=(n_rows // bm,),
        in_specs=[pl.BlockSpec((bm, D_MODEL // 2), lambda i, be, bv, bn: (i, 0)),
                  any_spec,
                  pl.BlockSpec((1, 1, 2 * D_FF), lambda i, be, bv, bn: (e0 + be[i], 0, 0)),
                  any_spec,
                  pl.BlockSpec((1, 1, D_MODEL), lambda i, be, bv, bn: (e0 + be[i], 0, 0))],
        out_specs=pl.BlockSpec((bm, D_MODEL // 2), lambda i, be, bv, bn: (i, 0)),
        scratch_shapes=[pltpu.VMEM((D_MODEL, 2 * D_FF), F32), pltpu.VMEM((D_FF, D_MODEL), F32),
                        pltpu.VMEM((D_MODEL, 2 * D_FF), BF16), pltpu.VMEM((D_FF, D_MODEL), BF16),
                        pltpu.SemaphoreType.DMA((2,))],
    )
    return pl.pallas_call(
        functools.partial(_moe_kernel, e0=e0),
        grid_spec=grid_spec,
        out_shape=jax.ShapeDtypeStruct((n_rows, D_MODEL // 2), U32),
        compiler_params=pltpu.CompilerParams(dimension_semantics=("arbitrary",), vmem_limit_bytes=VMEM_LIMIT),
        name="moe_experts",
    )(blk_e, blk_valid, blk_next, xs, w_gu.reshape(n_we, D_MODEL, 2 * D_FF), b_gu.reshape(n_we, 1, 2 * D_FF),
      w_d.reshape(n_we, D_FF, D_MODEL), b_d.reshape(n_we, 1, D_MODEL))


def _ple_kernel(x1_ref, y0_ref, y1_ref, y2_ref, y3_ref, gates_ref, p_ref, g_ref, wg_ref, wp_ref, *rest):
    out_ref = rest[-1]
    x2 = x1_ref[...]
    gates = gates_ref[...]
    for kk, y_ref in enumerate((y0_ref, y1_ref, y2_ref, y3_ref)):
        x2 = x2 + _unpack_halves(y_ref[...]) * gates[:, kk:kk + 1]
    hp = _rms(x2, g_ref[...]).astype(BF16)
    gate = jax.nn.sigmoid(jnp.dot(hp, wg_ref[...], preferred_element_type=F32))
    pp = jnp.dot(p_ref[...].astype(BF16), wp_ref[...], preferred_element_type=F32)
    out_ref[...] = x2 + gate * pp


def _ple(x1, tok0, n, y_tok, y0, gates_t, p_all, p0, g_ple, w_gate, w_proj, rows, out_into=None):
    assert n % rows == 0 and tok0 % rows == 0 and y0 % rows == 0 and p0 % rows == 0
    const = lambda shape: pl.BlockSpec(shape, lambda i: (0,) * len(shape))
    tok_blk = lambda width: pl.BlockSpec((rows, width), lambda i: (tok0 // rows + i, 0))
    y_blk = lambda kk: pl.BlockSpec((rows, D_MODEL // 2), lambda i: ((y0 + kk * n) // rows + i, 0))
    extra_specs, extra_args, aliases = [], [], {}
    if out_into is not None:
        assert out_into.shape == x1.shape
        extra_specs, extra_args, aliases = [pl.BlockSpec(memory_space=pl.ANY)], [out_into], {10: 0}
    return pl.pallas_call(
        _ple_kernel,
        grid=(n // rows,),
        in_specs=[tok_blk(D_MODEL), y_blk(0), y_blk(1), y_blk(2), y_blk(3), tok_blk(TOP_K),
                  pl.BlockSpec((rows, PLE_DIM), lambda i: (p0 // rows + i, 0)),
                  const((1, D_MODEL)), const((D_MODEL, D_MODEL)), const((PLE_DIM, D_MODEL))] + extra_specs,
        out_specs=tok_blk(D_MODEL),
        out_shape=jax.ShapeDtypeStruct(x1.shape, F32),
        input_output_aliases=aliases,
        compiler_params=pltpu.CompilerParams(dimension_semantics=("arbitrary",), vmem_limit_bytes=VMEM_LIMIT),
        name="combine_ple",
    )(x1, y_tok, y_tok, y_tok, y_tok, gates_t, p_all, g_ple, w_gate, w_proj, *extra_args)


def _rope_tables(pos):
    half = ROT_DIM // 2
    d = np.arange(LANES) % HEAD_DIM
    inv = ROPE_THETA ** (-jnp.arange(half, dtype=F32) / half)
    inv_lane = jnp.where(d < ROT_DIM, inv[d % half], 0.0)
    ang = pos.astype(F32)[:, None] * inv_lane[None, :]
    cos, sin = jnp.cos(ang), jnp.sin(ang)
    sin_a = jnp.where((d >= half) & (d < ROT_DIM), sin, 0.0)
    sin_b = jnp.where(d < half, -sin, 0.0)
    return cos, sin_a, sin_b


def _layer(layer, xp, xs, ck, cv, st, p_prompt_all, p_sample_all, past_len, refine_tail, tabs_p, tabs_s,
           norm_attn, w_in, q_norm, k_norm, attn_sinks, w_pool, pool_scale, w_out,
           norm_ffn, w_router, b_router, w_gate_up_all, b_gate_up_all, w_down_all, b_down_all,
           norm_ple, w_ple_gate, w_ple_proj):
    t, ns = xp.shape[0], xs.shape[0]
    w_q = w_in[:, :ATTN_WIDTH].reshape(D_MODEL, 2, Q_TILES, HEAD_DIM).transpose(0, 2, 1, 3).reshape(D_MODEL, ATTN_WIDTH)
    w_in_p = jnp.concatenate([w_q, w_in[:, ATTN_WIDTH:]], axis=1)
    w_oa = w_out[:ATTN_WIDTH].reshape(2, Q_TILES, HEAD_DIM, D_MODEL).transpose(1, 0, 2, 3).reshape(ATTN_WIDTH, D_MODEL)
    w_out_p = jnp.concatenate([w_oa, w_out[ATTN_WIDTH:]], axis=0)
    g_attn = norm_attn.reshape(1, D_MODEL)
    qn = jnp.tile(q_norm, 2).reshape(1, LANES)
    kn = jnp.tile(k_norm, 2).reshape(1, LANES)
    lane = np.arange(LANES)
    hmean = jnp.asarray((lane[:, None] // HEAD_DIM == lane[None, :] // HEAD_DIM) / HEAD_DIM, F32)
    pscale = pool_scale.reshape(1, POOL_WIDTH)
    sink8 = jnp.broadcast_to(attn_sinks.reshape(2, Q_TILES).T.reshape(8, 1), (8, LANES))

    mix_args = (g_attn, w_in_p.astype(BF16), qn, kn, hmean.astype(BF16), w_pool.astype(BF16), pscale,
                w_out_p.astype(BF16))
    x1p, nk_p, nv_p, nu_p = _mixer_prompt(xp, 0, MIX_ROWS, attn_sinks, tabs_p, *mix_args)
    if refine_tail:
        hi_args = (g_attn, w_in_p, qn, kn, hmean, w_pool, pscale, w_out_p)
        x1p, nk_p, nv_p, nu_p = _mixer_prompt(xp, t - 2 * MIX_TAIL_ROWS, MIX_TAIL_ROWS, attn_sinks, tabs_p, *hi_args,
                                              x1_into=x1p)
    st_t = jnp.transpose(st, (1, 0, 2))
    x1s, nk_s, nv_s, nst_t = _mixer_sample(xs, ck.reshape(ns, -1, KV_WIDTH), cv.reshape(ns, -1, KV_WIDTH), st_t,
                                           past_len, tabs_s, sink8, *mix_args)

    g_ffn = norm_ffn.reshape(1, D_MODEL)
    wr_t = w_router.T
    br = b_router.reshape(N_EXPERTS, 1)
    n_tok = t + ns
    bm = MOE_BM
    n_blocks = -(-(n_tok * TOP_K + N_EXPERTS * (bm - 1)) // bm)
    hf_p, idx_p, gate_p, pos_p, cnt_p = _route(x1p, g_ffn, wr_t, br, jnp.zeros((N_EXPERTS, 1), F32),
                                               ROUTE_ROWS, bm, n_blocks)[:5]
    hf_s, idx_s, gate_s, pos_s, _, pad_start, blk_e, blk_valid, blk_next = _route(x1s, g_ffn, wr_t, br, cnt_p, ns,
                                                                                  bm, n_blocks)
    idx = jnp.concatenate([idx_p, idx_s], axis=1)
    pos = jnp.concatenate([pos_p, pos_s], axis=1)
    blk_e, blk_valid, blk_next = (a[0, :n_blocks] for a in (blk_e, blk_valid, blk_next))
    start_of = jnp.sum(jnp.where(idx[None] == jnp.arange(N_EXPERTS, dtype=I32)[:, None, None],
                                 pad_start.reshape(N_EXPERTS, 1, 1), 0), axis=0)
    dest = start_of + pos

    xs_rows = _scatter_rows(hf_p, hf_s, dest, n_blocks * bm)
    y = _moe(xs_rows, blk_e, blk_valid, blk_next, layer, w_gate_up_all, b_gate_up_all, w_down_all, b_down_all)

    th = t // 2
    unit = SC_WORKERS * SC_CHUNK * 2
    assert (TOP_K * th) % unit == 0
    back_b = jnp.concatenate([dest[:, th:t].reshape(-1), dest[:, t:].reshape(-1)])
    back_b = jnp.concatenate([back_b, jnp.arange(-back_b.shape[0] % unit, dtype=I32)])
    y_a = _gather_rows(y, dest[:, :th].reshape(-1))
    y_a, back_b = lax.optimization_barrier((y_a, back_b))
    y_b = _gather_rows(y, back_b)

    g_ple = norm_ple.reshape(1, D_MODEL)
    wg_bf = w_ple_gate.astype(BF16)
    wp_bf = w_ple_proj.astype(BF16)
    gates_p = gate_p.T
    yp = _ple(x1p, 0, th, y_a, 0, gates_p, p_prompt_all, layer * t, g_ple, wg_bf, wp_bf, PLE_ROWS)
    yp = _ple(x1p, th, t - th, y_b, 0, gates_p, p_prompt_all, layer * t + th, g_ple, wg_bf, wp_bf, PLE_ROWS,
              out_into=yp)
    ys = _ple(x1s, 0, ns, y_b, TOP_K * (t - th), gate_s.T, p_sample_all, layer * ns, g_ple, wg_bf, wp_bf, ns)
    new_pool_s = jnp.transpose(nst_t, (1, 0, 2))
    return yp, ys, nk_p, nv_p, nu_p[POOL_HALO - POOL_PREFIX:], nk_s, nv_s, new_pool_s


def kernel(x_prompt, x_sample, cache_k, cache_v, state_pool, p_prompt, p_sample, norm_attn, w_in, q_norm, k_norm,
           attn_sinks, w_pool, pool_scale, w_out, norm_ffn, w_router, b_router, w_gate_up, b_gate_up, w_down, b_down,
           norm_ple, w_ple_gate, w_ple_proj):
    depth = norm_attn.shape[0]
    batch, seq, d = x_prompt.shape
    ns, dec_seq, _ = x_sample.shape
    wb = cache_k.shape[2]
    assert batch == 1 and dec_seq == 1 and d == D_MODEL and wb == WINDOW
    assert cache_k.shape[3:] == (N_KV_HEADS, HEAD_DIM) and state_pool.shape[2:] == (POOL_PREFIX, POOL_WIDTH)
    past_len = PAST_LEN
    yp = x_prompt.reshape(seq, d)
    ys = x_sample.reshape(ns, d)
    p_prompt_all = p_prompt.reshape(depth * seq, PLE_DIM)
    p_sample_all = p_sample.reshape(depth * ns, PLE_DIM)
    tabs_p = _rope_tables(jnp.arange(seq))
    tabs_s = _rope_tables(jnp.full((1,), past_len))
    outs = [[] for _ in range(6)]
    for i in range(depth):
        res = _layer(i, yp, ys, cache_k[i], cache_v[i], state_pool[i], p_prompt_all, p_sample_all, past_len,
                     i < depth - 1, tabs_p, tabs_s,
                     norm_attn[i], w_in[i], q_norm[i], k_norm[i], attn_sinks[i], w_pool[i], pool_scale[i], w_out[i],
                     norm_ffn[i], w_router[i], b_router[i], w_gate_up, b_gate_up, w_down, b_down,
                     norm_ple[i], w_ple_gate[i], w_ple_proj[i])
        yp, ys = res[0], res[1]
        kv_shape = (1, WINDOW, N_KV_HEADS, HEAD_DIM)
        outs[0].append(res[2].reshape(kv_shape))
        outs[1].append(res[3].reshape(kv_shape))
        outs[2].append(res[4].reshape(1, POOL_PREFIX, POOL_WIDTH))
        outs[3].append(res[5].reshape(ns, wb, N_KV_HEADS, HEAD_DIM))
        outs[4].append(res[6].reshape(ns, wb, N_KV_HEADS, HEAD_DIM))
        outs[5].append(res[7])
    return (yp.reshape(batch, seq, d), ys.reshape(ns, dec_seq, d)) + tuple(jnp.stack(o) for o in outs)
```

```python
import functools

import jax
import jax.numpy as jnp
import numpy as np
from jax import lax
from jax.experimental import pallas as pl
from jax.experimental.pallas import tpu as pltpu
from jax.experimental.pallas import tpu_sc as plsc

F32 = jnp.float32
BF16 = jnp.bfloat16
U32 = jnp.uint32
I32 = jnp.int32

D_MODEL = 1024
HEAD_DIM = 64
N_HEADS = 8
N_KV_HEADS = 2
GROUP = N_HEADS // N_KV_HEADS
ATTN_WIDTH = N_HEADS * HEAD_DIM
KV_WIDTH = N_KV_HEADS * HEAD_DIM
POOL_WIDTH = 512
POOL_WINDOWS = (2, 4, 8, 16)
POOL_GC = POOL_WIDTH // len(POOL_WINDOWS)
POOL_PREFIX = max(POOL_WINDOWS) - 1
POOL_HALO = POOL_PREFIX + 1
IN_WIDTH = ATTN_WIDTH + 2 * KV_WIDTH + POOL_WIDTH
WINDOW = 128
ROPE_THETA = 500000.0
ROT_DIM = HEAD_DIM // 4
N_EXPERTS = 32
TOP_K = 4
D_FF = 1024
SWIGLU_ALPHA = 1.702
SWIGLU_LIMIT = 7.0
PLE_DIM = 256
PAST_LEN = 16384
EPS = 1e-5
NEG_INF = -1e30

LANES = 128
Q_TILES = ATTN_WIDTH // LANES

MIX_ROWS = 512
MIX_TAIL_ROWS = 256
SAMPLE_CHUNK = 16
ROUTE_ROWS = 512
MOE_BM = 512
SC_CORES = 2
SC_SUBCORES = 16
SC_WORKERS = SC_CORES * SC_SUBCORES
SC_CHUNK = 64
PLE_ROWS = 512
VMEM_LIMIT = 56 * 1024 * 1024


def _rms(x, g):
    return x * lax.rsqrt(jnp.mean(x * x, axis=-1, keepdims=True) + EPS) * g


def _pack_halves(x):
    w = x.shape[1] // 2
    lo = lax.bitcast_convert_type(x[:, :w].astype(BF16).astype(F32), U32) >> 16
    hi = lax.bitcast_convert_type(x[:, w:].astype(BF16).astype(F32), U32) & jnp.uint32(0xFFFF0000)
    return lo | hi


def _unpack_halves(packed):
    lo = lax.bitcast_convert_type(packed << 16, F32)
    hi = lax.bitcast_convert_type(packed & jnp.uint32(0xFFFF0000), F32)
    return jnp.concatenate([lo, hi], axis=1)


def _mm(a, b, nt=False):
    dims = (((1,), (1 if nt else 0,)), ((), ()))
    if b.dtype == F32:
        return lax.dot_general(a.astype(F32), b, dims, preferred_element_type=F32, precision=lax.Precision.HIGHEST)
    return lax.dot_general(a.astype(BF16), b, dims, preferred_element_type=F32)


def _head_norm_rope(t, hmean, gain, cos, sin_a, sin_b):
    t = t * lax.rsqrt(_mm(t * t, hmean) + EPS) * gain
    return t * cos + pltpu.roll(t, ROT_DIM // 2, axis=1) * sin_a + pltpu.roll(t, LANES - ROT_DIM // 2, axis=1) * sin_b


def _softmax_pv(s, sink, v, ones):
    m = jnp.maximum(jnp.max(s, axis=-1, keepdims=True), sink)
    e = jnp.exp(s - m).astype(v.dtype)
    den = _mm(e, ones) + jnp.exp(sink - m)
    return _mm(e, v) / den


def _mixer_prompt_kernel(sinks_ref, x_ref, cos_ref, sa_ref, sb_ref, g_ref, win_ref, qn_ref, kn_ref, hm_ref,
                         wpool_ref, pscale_ref, wout_ref, *rest, row_offset, aliased):
    x1_ref, klast_ref, vlast_ref, ulast_ref, kprev, vprev, uext, mix = rest[1:] if aliased else rest
    i = pl.program_id(0)
    rows = x_ref.shape[0]
    n_sub = rows // WINDOW
    cdt = win_ref.dtype
    row0 = row_offset + i * rows

    @pl.when(i == 0)
    def _():
        kprev[...] = jnp.zeros_like(kprev)
        vprev[...] = jnp.zeros_like(vprev)
        uext[0:POOL_HALO, :] = jnp.zeros((POOL_HALO, POOL_WIDTH), F32)

    x = x_ref[...]
    proj = _mm(_rms(x, g_ref[...]), win_ref[...])
    cos, sin_a, sin_b = cos_ref[...], sa_ref[...], sb_ref[...]

    n_t = Q_TILES + 1
    t_all = jnp.concatenate([proj[:, j * LANES:(j + 1) * LANES] for j in range(n_t)], axis=0)
    t3 = (t_all * lax.rsqrt(_mm(t_all * t_all, hm_ref[...]) + EPS)).reshape(n_t, rows, LANES)
    t3 = jnp.concatenate([t3[:Q_TILES] * (qn_ref[...] * HEAD_DIM ** -0.5), t3[Q_TILES:] * kn_ref[...]], axis=0)
    t2 = t3.reshape(n_t * rows, LANES)
    t3 = (t3 * cos + pltpu.roll(t2, ROT_DIM // 2, axis=1).reshape(n_t, rows, LANES) * sin_a
          + pltpu.roll(t2, LANES - ROT_DIM // 2, axis=1).reshape(n_t, rows, LANES) * sin_b)
    q3 = t3[:Q_TILES]
    k = t3[Q_TILES]
    v = proj[:, ATTN_WIDTH + KV_WIDTH:ATTN_WIDTH + 2 * KV_WIDTH]
    u = proj[:, ATTN_WIDTH + 2 * KV_WIDTH:]
    klast_ref[...] = k[rows - WINDOW:, :]
    vlast_ref[...] = v[rows - WINDOW:, :]
    ulast_ref[...] = u[rows - POOL_HALO:, :]
    k_c = k.astype(cdt)
    v_c = jnp.concatenate([v.astype(cdt), jnp.ones((rows, LANES), cdt)], axis=1)
    v_first = jnp.concatenate([vprev[...], jnp.ones((WINDOW, LANES), cdt)], axis=1)

    lane = lax.broadcasted_iota(I32, (WINDOW, LANES), 1)
    left = (lane < HEAD_DIM)[None]
    qi = lax.broadcasted_iota(I32, (WINDOW, 2 * WINDOW), 0)
    kj = lax.broadcasted_iota(I32, (WINDOW, 2 * WINDOW), 1)
    band = (kj - qi >= 1) & (kj - qi <= WINDOW)
    sink3 = jnp.concatenate([jnp.full((1, 1, 1), sinks_ref[j + Q_TILES * s], F32)
                             for j in range(Q_TILES) for s in range(2)], axis=0)
    n_g = 2 * Q_TILES

    v_cats, masks, scores = [], [], []
    for c in range(n_sub):
        r0 = c * WINDOW
        if c == 0:
            k_cat = jnp.concatenate([kprev[...], k_c[0:WINDOW]], axis=0)
            v_cats.append(jnp.concatenate([v_first, v_c[0:WINDOW]], axis=0))
            masks.append(band & (kj + (row0 - WINDOW) >= 0))
        else:
            k_cat = k_c[r0 - WINDOW:r0 + WINDOW]
            v_cats.append(v_c[r0 - WINDOW:r0 + WINDOW])
            masks.append(band)
        q_c = q3[:, r0:r0 + WINDOW, :]
        q_all = jnp.concatenate([jnp.where(left, q_c, 0.0), jnp.where(left, 0.0, q_c)], axis=1)
        scores.append(_mm(q_all.reshape(n_g * WINDOW, LANES), k_cat, nt=True))
    kprev[...] = k_c[rows - WINDOW:]
    vprev[...] = v[rows - WINDOW:].astype(cdt)

    probs, maxes = [], []
    for c in range(n_sub):
        s = jnp.where(masks[c][None], scores[c].reshape(n_g, WINDOW, 2 * WINDOW), NEG_INF)
        m = jnp.maximum(jnp.max(s, axis=-1, keepdims=True), sink3)
        probs.append(jnp.exp(s - m).astype(cdt).reshape(n_g * WINDOW, 2 * WINDOW))
        maxes.append(m)
    applied = [_mm(probs[c], v_cats[c]) for c in range(n_sub)]
    for c in range(n_sub):
        r0 = c * WINDOW
        den = applied[c][:, LANES:] + jnp.exp(sink3 - maxes[c]).reshape(n_g * WINDOW, 1)
        o = (applied[c][:, :LANES] / den).reshape(Q_TILES, 2 * WINDOW, LANES)
        a = jnp.where(left, o[:, :WINDOW], o[:, WINDOW:])
        for j in range(Q_TILES):
            mix[r0:r0 + WINDOW, j * LANES:(j + 1) * LANES] = a[j].astype(cdt)

    uext[POOL_HALO:POOL_HALO + rows, :] = u
    pos1 = (lax.broadcasted_iota(I32, (rows, 1), 0) + row0 + 1).astype(F32)
    for gi, w in enumerate(POOL_WINDOWS):
        cols = slice(gi * POOL_GC, (gi + 1) * POOL_GC)
        wsum = u[:, cols]
        for sft in range(1, w):
            wsum = wsum + uext[POOL_HALO - sft:POOL_HALO - sft + rows, cols]
        d = wsum / jnp.minimum(pos1, float(w)) - u[:, cols]
        y = _mm(d, wpool_ref[gi]) * pscale_ref[:, cols]
        mix[:, ATTN_WIDTH + gi * POOL_GC:ATTN_WIDTH + (gi + 1) * POOL_GC] = y.astype(cdt)
    uext[0:POOL_HALO, :] = u[rows - POOL_HALO:, :]

    x1_ref[...] = x + _mm(mix[...], wout_ref[...])


def _mixer_prompt(x_full, row_offset, rows, sinks, tabs, g_attn, w_in, qn, kn, hmean, w_pool, pscale, w_out,
                  x1_into=None):
    t = x_full.shape[0] - row_offset
    cdt = w_in.dtype
    assert t % rows == 0 and row_offset % rows == 0 and rows % WINDOW == 0 and rows >= POOL_HALO
    blk0, n_steps = row_offset // rows, t // rows
    const = lambda shape: pl.BlockSpec(shape, lambda i, *_: (0,) * len(shape))
    row_blk = lambda width: pl.BlockSpec((rows, width), lambda i, *_: (blk0 + i, 0))
    aliased = x1_into is not None
    if aliased:
        assert x1_into.shape == x_full.shape
        x1_spec = pl.BlockSpec((rows, D_MODEL), lambda i, *_: (blk0 + n_steps - 1, 0))
        x1_shape = x1_into.shape
        extra_specs, extra_args, aliases = [pl.BlockSpec(memory_space=pl.ANY)], [x1_into], {13: 0}
    else:
        x1_spec = pl.BlockSpec((rows, D_MODEL), lambda i, *_: (i, 0))
        x1_shape = (t, D_MODEL)
        extra_specs, extra_args, aliases = [], [], {}
    grid_spec = pltpu.PrefetchScalarGridSpec(
        num_scalar_prefetch=1,
        grid=(n_steps,),
        in_specs=[row_blk(D_MODEL), row_blk(LANES), row_blk(LANES), row_blk(LANES),
                  const((1, D_MODEL)), const((D_MODEL, IN_WIDTH)), const((1, LANES)), const((1, LANES)),
                  const((LANES, LANES)), const((len(POOL_WINDOWS), POOL_GC, POOL_GC)), const((1, POOL_WIDTH)),
                  const((D_MODEL, D_MODEL))] + extra_specs,
        out_specs=[x1_spec, const((WINDOW, KV_WIDTH)), const((WINDOW, KV_WIDTH)),
                   const((POOL_HALO, POOL_WIDTH))],
        scratch_shapes=[pltpu.VMEM((WINDOW, KV_WIDTH), cdt), pltpu.VMEM((WINDOW, KV_WIDTH), cdt),
                        pltpu.VMEM((POOL_HALO + rows, POOL_WIDTH), F32), pltpu.VMEM((rows, D_MODEL), cdt)],
    )
    return pl.pallas_call(
        functools.partial(_mixer_prompt_kernel, row_offset=row_offset, aliased=aliased),
        grid_spec=grid_spec,
        out_shape=[jax.ShapeDtypeStruct(x1_shape, F32), jax.ShapeDtypeStruct((WINDOW, KV_WIDTH), F32),
                   jax.ShapeDtypeStruct((WINDOW, KV_WIDTH), F32), jax.ShapeDtypeStruct((POOL_HALO, POOL_WIDTH), F32)],
        input_output_aliases=aliases,
        compiler_params=pltpu.CompilerParams(dimension_semantics=("arbitrary",), vmem_limit_bytes=VMEM_LIMIT),
        name="mixer_prompt",
    )(sinks, x_full, *tabs, g_attn, w_in, qn, kn, hmean, w_pool, pscale, w_out, *extra_args)


def _mixer_sample_kernel(x_ref, ck_ref, cv_ref, st_ref, cos_ref, sa_ref, sb_ref, sink8_ref, g_ref, win_ref, qn_ref,
                         kn_ref, hm_ref, wpool_ref, pscale_ref, wout_ref,
                         x1_ref, nk_ref, nv_ref, nst_ref, o8, *, pos):
    nb = x_ref.shape[0]
    wb = ck_ref.shape[1]
    x = x_ref[...]
    h = _rms(x, g_ref[...]).astype(BF16)
    proj = jnp.dot(h, win_ref[...], preferred_element_type=F32)
    cos, sin_a, sin_b = cos_ref[...], sa_ref[...], sb_ref[...]
    hmean = hm_ref[...]
    k = _head_norm_rope(proj[:, ATTN_WIDTH:ATTN_WIDTH + KV_WIDTH], hmean, kn_ref[...], cos, sin_a, sin_b)
    v = proj[:, ATTN_WIDTH + KV_WIDTH:ATTN_WIDTH + 2 * KV_WIDTH]
    u = proj[:, ATTN_WIDTH + 2 * KV_WIDTH:]

    nk_ref[:, 0:wb - 1, :] = ck_ref[:, 1:wb, :]
    nv_ref[:, 0:wb - 1, :] = cv_ref[:, 1:wb, :]
    for b in range(nb):
        nk_ref[b, wb - 1:wb, :] = k[b:b + 1, :]
        nv_ref[b, wb - 1:wb, :] = v[b:b + 1, :]

    r8 = lax.broadcasted_iota(I32, (nb * 8, LANES), 0)
    lane8 = lax.broadcasted_iota(I32, (nb * 8, LANES), 1)
    keep = (lane8 < HEAD_DIM) == (r8 % 2 == 0)
    rep = (lax.broadcasted_iota(I32, (nb * 8, nb), 0) // 8 == lax.broadcasted_iota(I32, (nb * 8, nb), 1)).astype(BF16)
    q8 = jnp.zeros((nb * 8, LANES), F32)
    scale = HEAD_DIM ** -0.5
    for j in range(Q_TILES):
        qt = _head_norm_rope(proj[:, j * LANES:(j + 1) * LANES], hmean, qn_ref[...], cos, sin_a, sin_b) * scale
        qrep = jnp.dot(rep, qt.astype(BF16), preferred_element_type=F32)
        q8 = jnp.where(keep & ((r8 % 8) // 2 == j), qrep, q8)
    q8 = q8.astype(BF16)

    sink8 = sink8_ref[:, 0:1]
    ones_bf = jnp.ones((wb, LANES), BF16)
    assert pos >= wb - 1 and wb <= WINDOW
    for b in range(nb):
        kb = nk_ref[b].astype(BF16)
        vb = nv_ref[b].astype(BF16)
        s = lax.dot_general(q8[b * 8:(b + 1) * 8], kb, (((1,), (1,)), ((), ())), preferred_element_type=F32)
        o8[b * 8:(b + 1) * 8, :] = _softmax_pv(s, sink8, vb, ones_bf)
    o8m = jnp.where(keep, o8[...], 0.0).astype(BF16)

    a_tiles = []
    sel_r = lax.broadcasted_iota(I32, (nb, nb * 8), 1)
    sel_b = lax.broadcasted_iota(I32, (nb, nb * 8), 0)
    for j in range(Q_TILES):
        sel = ((sel_r // 8 == sel_b) & ((sel_r % 8) // 2 == j)).astype(BF16)
        a_tiles.append(jnp.dot(sel, o8m, preferred_element_type=F32))

    z_tiles = []
    for gi, w in enumerate(POOL_WINDOWS):
        cols = slice(gi * POOL_GC, (gi + 1) * POOL_GC)
        wsum = u[:, cols]
        for sft in range(1, w):
            wsum = wsum + st_ref[POOL_PREFIX - sft, :, cols]
        d = wsum / float(min(pos + 1, w)) - u[:, cols]
        z_tiles.append(jnp.dot(d.astype(BF16), wpool_ref[gi], preferred_element_type=F32) * pscale_ref[:, cols])
    nst_ref[0:POOL_PREFIX - 1] = st_ref[1:POOL_PREFIX]
    nst_ref[POOL_PREFIX - 1] = u

    mixv = jnp.concatenate(a_tiles + z_tiles, axis=1).astype(BF16)
    x1_ref[...] = x + jnp.dot(mixv, wout_ref[...], preferred_element_type=F32)


def _mixer_sample(x, ck, cv, st, pos, tabs, sink8, g_attn, w_in, qn, kn, hmean, w_pool, pscale, w_out):
    n, wb = ck.shape[0], ck.shape[1]
    nb = SAMPLE_CHUNK
    assert n % nb == 0
    const = lambda shape: pl.BlockSpec(shape, lambda i: (0,) * len(shape))
    cache_blk = pl.BlockSpec((nb, wb, KV_WIDTH), lambda i: (i, 0, 0))
    st_blk = pl.BlockSpec((POOL_PREFIX, nb, POOL_WIDTH), lambda i: (0, i, 0))
    x_blk = pl.BlockSpec((nb, D_MODEL), lambda i: (i, 0))
    return pl.pallas_call(
        functools.partial(_mixer_sample_kernel, pos=pos),
        grid=(n // nb,),
        in_specs=[x_blk, cache_blk, cache_blk, st_blk, const((1, LANES)), const((1, LANES)), const((1, LANES)),
                  const((8, LANES)), const((1, D_MODEL)), const((D_MODEL, IN_WIDTH)), const((1, LANES)),
                  const((1, LANES)), const((LANES, LANES)), const((len(POOL_WINDOWS), POOL_GC, POOL_GC)),
                  const((1, POOL_WIDTH)), const((D_MODEL, D_MODEL))],
        out_specs=[x_blk, cache_blk, cache_blk, st_blk],
        out_shape=[jax.ShapeDtypeStruct((n, D_MODEL), F32), jax.ShapeDtypeStruct(ck.shape, F32),
                   jax.ShapeDtypeStruct(cv.shape, F32), jax.ShapeDtypeStruct(st.shape, F32)],
        scratch_shapes=[pltpu.VMEM((nb * 8, LANES), F32)],
        compiler_params=pltpu.CompilerParams(dimension_semantics=("arbitrary",), vmem_limit_bytes=VMEM_LIMIT),
        name="mixer_sample",
    )(x, ck, cv, st, *tabs, sink8, g_attn, w_in, qn, kn, hmean, w_pool, pscale, w_out)


def _block_plan(cnt, bm, n_lanes):
    e_sub = lax.broadcasted_iota(I32, (N_EXPERTS, LANES), 0)
    e_lane = lax.broadcasted_iota(I32, (N_EXPERTS, LANES), 1)
    padded = jnp.floor((cnt + (bm - 1)) / bm) * bm
    padded_lane = jnp.sum(jnp.where(e_sub == e_lane, padded, 0.0), axis=0, keepdims=True)
    pad_end = jnp.sum(jnp.where(e_lane <= e_sub, padded_lane, 0.0), axis=1, keepdims=True)
    pad_start = pad_end - padded
    blk_start = lax.broadcasted_iota(I32, (N_EXPERTS, n_lanes), 1).astype(F32) * bm
    blk_e = jnp.minimum(jnp.sum((pad_end <= blk_start).astype(F32), axis=0, keepdims=True), N_EXPERTS - 1.0)
    mine = lax.broadcasted_iota(I32, (N_EXPERTS, n_lanes), 0).astype(F32) == blk_e
    last = jnp.sum(jnp.where(mine, pad_start + cnt, 0.0), axis=0, keepdims=True)
    blk_valid = jnp.clip(last - blk_start[0:1], 0.0, float(bm))
    e_blk = lax.broadcasted_iota(I32, (N_EXPERTS, n_lanes), 0).astype(F32)
    later = jnp.min(jnp.where((e_blk > blk_e) & (cnt > 0.0), e_blk, float(N_EXPERTS)), axis=0, keepdims=True)
    blk_next = jnp.where(later < N_EXPERTS, later, -1.0)
    return pad_start, blk_e, blk_valid, blk_next


def _route_kernel(x1_ref, g_ref, wr_ref, br_ref, cnt_in_ref, hf_ref, idx_ref, gate_ref, pos_ref, cnt_ref,
                  pstart_ref, blke_ref, blkv_ref, blkn_ref, counts, *, bm):
    i = pl.program_id(0)
    rows = x1_ref.shape[0]

    @pl.when(i == 0)
    def _():
        counts[...] = cnt_in_ref[...]

    h = _rms(x1_ref[...], g_ref[...])
    hf_ref[...] = _pack_halves(h)

    logits = lax.dot_general(wr_ref[...], h, (((1,), (1,)), ((), ())), preferred_element_type=F32,
                             precision=lax.Precision.HIGHEST) + br_ref[...]
    eid = lax.broadcasted_iota(I32, (N_EXPERTS, rows), 0).astype(F32)
    work = logits
    vals, hots = [], []
    for kk in range(TOP_K):
        m = jnp.max(work, axis=0, keepdims=True)
        first = jnp.min(jnp.where(work == m, eid, float(N_EXPERTS)), axis=0, keepdims=True)
        hot = eid == first
        work = jnp.where(hot, -jnp.inf, work)
        vals.append(m)
        hots.append(hot)
        idx_ref[kk:kk + 1, :] = first.astype(I32)
    es = [jnp.exp(vv - vals[0]) for vv in vals]
    den = es[0] + es[1] + es[2] + es[3]
    for kk in range(TOP_K):
        gate_ref[kk:kk + 1, :] = es[kk] / den

    chosen = hots[0] | hots[1] | hots[2] | hots[3]
    before = (lax.broadcasted_iota(I32, (rows, rows), 0) < lax.broadcasted_iota(I32, (rows, rows), 1)).astype(BF16)
    rank = jnp.dot(chosen.astype(BF16), before, preferred_element_type=F32) + counts[...]
    for kk in range(TOP_K):
        pos_ref[kk:kk + 1, :] = jnp.sum(jnp.where(hots[kk], rank, 0.0), axis=0, keepdims=True).astype(I32)
    counts[...] = counts[...] + jnp.sum(chosen.astype(F32), axis=1, keepdims=True)
    cnt_ref[...] = counts[...]

    @pl.when(i == pl.num_programs(0) - 1)
    def _():
        pad_start, blk_e, blk_valid, blk_next = _block_plan(counts[...], bm, blke_ref.shape[1])
        pstart_ref[...] = pad_start.astype(I32)
        blke_ref[...] = blk_e.astype(I32)
        blkv_ref[...] = blk_valid.astype(I32)
        blkn_ref[...] = blk_next.astype(I32)


def _route(x1, g_ffn, wr_t, br, cnt_in, rows, bm, n_blocks):
    n = x1.shape[0]
    assert n % rows == 0
    nb_lanes = -(-n_blocks // LANES) * LANES
    const = lambda shape: pl.BlockSpec(shape, lambda i: (0,) * len(shape))
    tok_blk = pl.BlockSpec((TOP_K, rows), lambda i: (0, i))
    return pl.pallas_call(
        functools.partial(_route_kernel, bm=bm),
        grid=(n // rows,),
        in_specs=[pl.BlockSpec((rows, D_MODEL), lambda i: (i, 0)), const((1, D_MODEL)), const((N_EXPERTS, D_MODEL)),
                  const((N_EXPERTS, 1)), const((N_EXPERTS, 1))],
        out_specs=[pl.BlockSpec((rows, D_MODEL // 2), lambda i: (i, 0)), tok_blk, tok_blk, tok_blk,
                   const((N_EXPERTS, 1)), const((N_EXPERTS, 1)), const((1, nb_lanes)), const((1, nb_lanes)),
                   const((1, nb_lanes))],
        out_shape=[jax.ShapeDtypeStruct((n, D_MODEL // 2), U32), jax.ShapeDtypeStruct((TOP_K, n), I32),
                   jax.ShapeDtypeStruct((TOP_K, n), F32), jax.ShapeDtypeStruct((TOP_K, n), I32),
                   jax.ShapeDtypeStruct((N_EXPERTS, 1), F32), jax.ShapeDtypeStruct((N_EXPERTS, 1), I32),
                   jax.ShapeDtypeStruct((1, nb_lanes), I32), jax.ShapeDtypeStruct((1, nb_lanes), I32),
                   jax.ShapeDtypeStruct((1, nb_lanes), I32)],
        scratch_shapes=[pltpu.VMEM((N_EXPERTS, 1), F32)],
        compiler_params=pltpu.CompilerParams(dimension_semantics=("arbitrary",), vmem_limit_bytes=VMEM_LIMIT),
        name="route",
    )(x1, g_ffn, wr_t, br, cnt_in)


def _sc_mesh():
    return plsc.VectorSubcoreMesh(core_axis_name="core", subcore_axis_name="subcore")


def _sc_worker_id():
    return lax.axis_index("core") * SC_SUBCORES + lax.axis_index("subcore")


def _scatter_rows(xa, xb, dest, n_rows):
    ch = SC_CHUNK
    na, w = xa.shape
    n = na + xb.shape[0]
    nk = dest.shape[0]
    assert na % ch == 0 and n % ch == 0 and dest.shape[1] == n and xb.shape[1] == w and xa.dtype == xb.dtype
    n_chunks = n // ch
    dest_c = dest.reshape(nk, n_chunks, ch).transpose(1, 0, 2).reshape(-1)
    dma = pltpu.SemaphoreType.DMA

    @pl.kernel(out_type=jax.ShapeDtypeStruct((n_rows, w), xa.dtype), mesh=_sc_mesh(),
               scratch_types=[pltpu.VMEM((nk * ch,), I32), pltpu.VMEM((ch, w), xa.dtype), dma] + [dma] * nk)
    def scatter_kernel(xa_hbm, xb_hbm, d_hbm, o_hbm, idx_v, buf, sem_i, *sem_s):
        wid = _sc_worker_id()

        @pl.loop(0, -(-n_chunks // SC_WORKERS))
        def _(j):
            c = j * SC_WORKERS + wid

            @pl.when(c < n_chunks)
            def _():
                load_idx = pltpu.make_async_copy(d_hbm.at[pl.ds(c * (nk * ch), nk * ch)], idx_v, sem_i)
                load_idx.start()

                @pl.when(c < na // ch)
                def _():
                    pltpu.sync_copy(xa_hbm.at[pl.ds(c * ch, ch)], buf)

                @pl.when(c >= na // ch)
                def _():
                    pltpu.sync_copy(xb_hbm.at[pl.ds(c * ch - na, ch)], buf)

                load_idx.wait()
                puts = [pltpu.make_async_copy(buf, o_hbm.at[idx_v.at[pl.ds(kk * ch, ch)]], sem_s[kk])
                        for kk in range(nk)]
                for cp in puts:
                    cp.start()
                for cp in puts:
                    cp.wait()

    return scatter_kernel(xa, xb, dest_c)


def _gather_rows(src, idx):
    ch = SC_CHUNK
    m = idx.shape[0]
    w = src.shape[1]
    per = m // SC_WORKERS
    n_pairs = per // (2 * ch)
    assert m % SC_WORKERS == 0 and per % (2 * ch) == 0
    dma = pltpu.SemaphoreType.DMA

    @pl.kernel(out_type=jax.ShapeDtypeStruct((m, w), src.dtype), mesh=_sc_mesh(),
               scratch_types=[pltpu.VMEM((per,), I32), pltpu.VMEM((ch, w), src.dtype), pltpu.VMEM((ch, w), src.dtype),
                              dma, dma, dma, dma])
    def gather_kernel(s_hbm, i_hbm, o_hbm, idx_v, buf_a, buf_b, sem_ga, sem_gb, sem_wa, sem_wb):
        base = _sc_worker_id() * per
        pltpu.sync_copy(i_hbm.at[pl.ds(base, per)], idx_v)

        def fetch(j, buf, sem):
            return pltpu.make_async_copy(s_hbm.at[idx_v.at[pl.ds(j * ch, ch)]], buf, sem)

        def put(j, buf, sem):
            return pltpu.make_async_copy(buf, o_hbm.at[pl.ds(base + j * ch, ch)], sem)

        fetch(0, buf_a, sem_ga).start()

        @pl.loop(0, n_pairs)
        def _(p):
            j0 = 2 * p
            j1 = j0 + 1

            @pl.when(p > 0)
            def _():
                put(j1 - 2, buf_b, sem_wb).wait()

            fetch(j1, buf_b, sem_gb).start()
            fetch(j0, buf_a, sem_ga).wait()
            put(j0, buf_a, sem_wa).start()
            fetch(j1, buf_b, sem_gb).wait()
            put(j1, buf_b, sem_wb).start()
            put(j0, buf_a, sem_wa).wait()

            @pl.when(p + 1 < n_pairs)
            def _():
                fetch(j0 + 2, buf_a, sem_ga).start()

        put(2 * n_pairs - 1, buf_b, sem_wb).wait()

    return gather_kernel(src, idx)


def _moe_kernel(blk_e_ref, blk_valid_ref, blk_next_ref, xs_ref, wgu_hbm, bgu_ref, wd_hbm, bd_ref, y_ref,
                wgu_f32, wd_f32, wgu_bf, wd_bf, sems, *, e0):
    i = pl.program_id(0)
    e = blk_e_ref[i]
    n_valid = blk_valid_ref[i]
    used = n_valid > 0
    new_expert = (i == 0) | (blk_e_ref[jnp.maximum(i - 1, 0)] != e)

    def weight_copies(expert):
        return (pltpu.make_async_copy(wgu_hbm.at[e0 + expert], wgu_f32, sems.at[0]),
                pltpu.make_async_copy(wd_hbm.at[e0 + expert], wd_f32, sems.at[1]))

    @pl.when(used & (i == 0))
    def _():
        for cp in weight_copies(e):
            cp.start()

    @pl.when(used & new_expert)
    def _():
        for cp in weight_copies(e):
            cp.wait()
        wgu_bf[...] = wgu_f32[...].astype(BF16)
        wd_bf[...] = wd_f32[...].astype(BF16)
        nxt = blk_next_ref[i]

        @pl.when(nxt >= 0)
        def _():
            for cp in weight_copies(nxt):
                cp.start(priority=1)

    @pl.when(used)
    def _():
        valid = lax.broadcasted_iota(I32, xs_ref.shape, 0) < n_valid
        xb = _unpack_halves(jnp.where(valid, xs_ref[...], jnp.uint32(0))).astype(BF16)
        gu = jnp.dot(xb, wgu_bf[...], preferred_element_type=F32) + bgu_ref[0]
        g = jnp.minimum(gu[:, :D_FF], SWIGLU_LIMIT)
        up = jnp.clip(gu[:, D_FF:], -SWIGLU_LIMIT, SWIGLU_LIMIT)
        act = (up + 1.0) * (g * jax.nn.sigmoid(SWIGLU_ALPHA * g))
        y_ref[...] = _pack_halves(jnp.dot(act.astype(BF16), wd_bf[...], preferred_element_type=F32) + bd_ref[0])

    @pl.when(jnp.logical_not(used))
    def _():
        y_ref[...] = jnp.zeros_like(y_ref)


def _moe(xs, blk_e, blk_valid, blk_next, layer, w_gu, b_gu, w_d, b_d):
    n_rows = xs.shape[0]
    bm = MOE_BM
    assert n_rows % bm == 0
    e0 = layer * N_EXPERTS
    n_we = w_gu.shape[0] * w_gu.shape[1]
    any_spec = pl.BlockSpec(memory_space=pl.ANY)
    grid_spec = pltpu.PrefetchScalarGridSpec(
        num_scalar_prefetch=3,
        grid=(n_rows // bm,),
        in_specs=[pl.BlockSpec((bm, D_MODEL // 2), lambda i, be, bv, bn: (i, 0)),
                  any_spec,
                  pl.BlockSpec((1, 1, 2 * D_FF), lambda i, be, bv, bn: (e0 + be[i], 0, 0)),
                  any_spec,
                  pl.BlockSpec((1, 1, D_MODEL), lambda i, be, bv, bn: (e0 + be[i], 0, 0))],
        out_specs=pl.BlockSpec((bm, D_MODEL // 2), lambda i, be, bv, bn: (i, 0)),
        scratch_shapes=[pltpu.VMEM((D_MODEL, 2 * D_FF), F32), pltpu.VMEM((D_FF, D_MODEL), F32),
                        pltpu.VMEM((D_MODEL, 2 * D_FF), BF16), pltpu.VMEM((D_FF, D_MODEL), BF16),
                        pltpu.SemaphoreType.DMA((2,))],
    )
    return pl.pallas_call(
        functools.partial(_moe_kernel, e0=e0),
        grid_spec=grid_spec,
        out_shape=jax.ShapeDtypeStruct((n_rows, D_MODEL // 2), U32),
        compiler_params=pltpu.CompilerParams(dimension_semantics=("arbitrary",), vmem_limit_bytes=VMEM_LIMIT),
        name="moe_experts",
    )(blk_e, blk_valid, blk_next, xs, w_gu.reshape(n_we, D_MODEL, 2 * D_FF), b_gu.reshape(n_we, 1, 2 * D_FF),
      w_d.reshape(n_we, D_FF, D_MODEL), b_d.reshape(n_we, 1, D_MODEL))


def _ple_kernel(x1_ref, y0_ref, y1_ref, y2_ref, y3_ref, gates_ref, p_ref, g_ref, wg_ref, wp_ref, *rest):
    out_ref = rest[-1]
    x2 = x1_ref[...]
    gates = gates_ref[...]
    for kk, y_ref in enumerate((y0_ref, y1_ref, y2_ref, y3_ref)):
        x2 = x2 + _unpack_halves(y_ref[...]) * gates[:, kk:kk + 1]
    hp = _rms(x2, g_ref[...]).astype(BF16)
    gate = jax.nn.sigmoid(jnp.dot(hp, wg_ref[...], preferred_element_type=F32))
    pp = jnp.dot(p_ref[...].astype(BF16), wp_ref[...], preferred_element_type=F32)
    out_ref[...] = x2 + gate * pp


def _ple(x1, tok0, n, y_tok, y0, gates_t, p_all, p0, g_ple, w_gate, w_proj, rows, out_into=None):
    assert n % rows == 0 and tok0 % rows == 0 and y0 % rows == 0 and p0 % rows == 0
    const = lambda shape: pl.BlockSpec(shape, lambda i: (0,) * len(shape))
    tok_blk = lambda width: pl.BlockSpec((rows, width), lambda i: (tok0 // rows + i, 0))
    y_blk = lambda kk: pl.BlockSpec((rows, D_MODEL // 2), lambda i: ((y0 + kk * n) // rows + i, 0))
    extra_specs, extra_args, aliases = [], [], {}
    if out_into is not None:
        assert out_into.shape == x1.shape
        extra_specs, extra_args, aliases = [pl.BlockSpec(memory_space=pl.ANY)], [out_into], {10: 0}
    return pl.pallas_call(
        _ple_kernel,
        grid=(n // rows,),
        in_specs=[tok_blk(D_MODEL), y_blk(0), y_blk(1), y_blk(2), y_blk(3), tok_blk(TOP_K),
                  pl.BlockSpec((rows, PLE_DIM), lambda i: (p0 // rows + i, 0)),
                  const((1, D_MODEL)), const((D_MODEL, D_MODEL)), const((PLE_DIM, D_MODEL))] + extra_specs,
        out_specs=tok_blk(D_MODEL),
        out_shape=jax.ShapeDtypeStruct(x1.shape, F32),
        input_output_aliases=aliases,
        compiler_params=pltpu.CompilerParams(dimension_semantics=("arbitrary",), vmem_limit_bytes=VMEM_LIMIT),
        name="combine_ple",
    )(x1, y_tok, y_tok, y_tok, y_tok, gates_t, p_all, g_ple, w_gate, w_proj, *extra_args)


def _rope_tables(pos):
    half = ROT_DIM // 2
    d = np.arange(LANES) % HEAD_DIM
    inv = ROPE_THETA ** (-jnp.arange(half, dtype=F32) / half)
    inv_lane = jnp.where(d < ROT_DIM, inv[d % half], 0.0)
    ang = pos.astype(F32)[:, None] * inv_lane[None, :]
    cos, sin = jnp.cos(ang), jnp.sin(ang)
    sin_a = jnp.where((d >= half) & (d < ROT_DIM), sin, 0.0)
    sin_b = jnp.where(d < half, -sin, 0.0)
    return cos, sin_a, sin_b


def _layer(layer, xp, xs, ck, cv, st, p_prompt_all, p_sample_all, past_len, refine_tail, tabs_p, tabs_s,
           norm_attn, w_in, q_norm, k_norm, attn_sinks, w_pool, pool_scale, w_out,
           norm_ffn, w_router, b_router, w_gate_up_all, b_gate_up_all, w_down_all, b_down_all,
           norm_ple, w_ple_gate, w_ple_proj):
    t, ns = xp.shape[0], xs.shape[0]
    w_q = w_in[:, :ATTN_WIDTH].reshape(D_MODEL, 2, Q_TILES, HEAD_DIM).transpose(0, 2, 1, 3).reshape(D_MODEL, ATTN_WIDTH)
    w_in_p = jnp.concatenate([w_q, w_in[:, ATTN_WIDTH:]], axis=1)
    w_oa = w_out[:ATTN_WIDTH].reshape(2, Q_TILES, HEAD_DIM, D_MODEL).transpose(1, 0, 2, 3).reshape(ATTN_WIDTH, D_MODEL)
    w_out_p = jnp.concatenate([w_oa, w_out[ATTN_WIDTH:]], axis=0)
    g_attn = norm_attn.reshape(1, D_MODEL)
    qn = jnp.tile(q_norm, 2).reshape(1, LANES)
    kn = jnp.tile(k_norm, 2).reshape(1, LANES)
    lane = np.arange(LANES)
    hmean = jnp.asarray((lane[:, None] // HEAD_DIM == lane[None, :] // HEAD_DIM) / HEAD_DIM, F32)
    pscale = pool_scale.reshape(1, POOL_WIDTH)
    sink8 = jnp.broadcast_to(attn_sinks.reshape(2, Q_TILES).T.reshape(8, 1), (8, LANES))

    mix_args = (g_attn, w_in_p.astype(BF16), qn, kn, hmean.astype(BF16), w_pool.astype(BF16), pscale,
                w_out_p.astype(BF16))
    x1p, nk_p, nv_p, nu_p = _mixer_prompt(xp, 0, MIX_ROWS, attn_sinks, tabs_p, *mix_args)
    if refine_tail:
        hi_args = (g_attn, w_in_p, qn, kn, hmean, w_pool, pscale, w_out_p)
        x1p, nk_p, nv_p, nu_p = _mixer_prompt(xp, t - 2 * MIX_TAIL_ROWS, MIX_TAIL_ROWS, attn_sinks, tabs_p, *hi_args,
                                              x1_into=x1p)
    st_t = jnp.transpose(st, (1, 0, 2))
    x1s, nk_s, nv_s, nst_t = _mixer_sample(xs, ck.reshape(ns, -1, KV_WIDTH), cv.reshape(ns, -1, KV_WIDTH), st_t,
                                           past_len, tabs_s, sink8, *mix_args)

    g_ffn = norm_ffn.reshape(1, D_MODEL)
    wr_t = w_router.T
    br = b_router.reshape(N_EXPERTS, 1)
    n_tok = t + ns
    bm = MOE_BM
    n_blocks = -(-(n_tok * TOP_K + N_EXPERTS * (bm - 1)) // bm)
    hf_p, idx_p, gate_p, pos_p, cnt_p = _route(x1p, g_ffn, wr_t, br, jnp.zeros((N_EXPERTS, 1), F32),
                                               ROUTE_ROWS, bm, n_blocks)[:5]
    hf_s, idx_s, gate_s, pos_s, _, pad_start, blk_e, blk_valid, blk_next = _route(x1s, g_ffn, wr_t, br, cnt_p, ns,
                                                                                  bm, n_blocks)
    idx = jnp.concatenate([idx_p, idx_s], axis=1)
    pos = jnp.concatenate([pos_p, pos_s], axis=1)
    blk_e, blk_valid, blk_next = (a[0, :n_blocks] for a in (blk_e, blk_valid, blk_next))
    start_of = jnp.sum(jnp.where(idx[None] == jnp.arange(N_EXPERTS, dtype=I32)[:, None, None],
                                 pad_start.reshape(N_EXPERTS, 1, 1), 0), axis=0)
    dest = start_of + pos

    xs_rows = _scatter_rows(hf_p, hf_s, dest, n_blocks * bm)
    y = _moe(xs_rows, blk_e, blk_valid, blk_next, layer, w_gate_up_all, b_gate_up_all, w_down_all, b_down_all)

    th = t // 2
    unit = SC_WORKERS * SC_CHUNK * 2
    assert (TOP_K * th) % unit == 0
    back_b = jnp.concatenate([dest[:, th:t].reshape(-1), dest[:, t:].reshape(-1)])
    back_b = jnp.concatenate([back_b, jnp.arange(-back_b.shape[0] % unit, dtype=I32)])
    y_a = _gather_rows(y, dest[:, :th].reshape(-1))
    y_b = _gather_rows(y, back_b)

    g_ple = norm_ple.reshape(1, D_MODEL)
    wg_bf = w_ple_gate.astype(BF16)
    wp_bf = w_ple_proj.astype(BF16)
    gates_p = gate_p.T
    yp = _ple(x1p, 0, th, y_a, 0, gates_p, p_prompt_all, layer * t, g_ple, wg_bf, wp_bf, PLE_ROWS)
    yp = _ple(x1p, th, t - th, y_b, 0, gates_p, p_prompt_all, layer * t + th, g_ple, wg_bf, wp_bf, PLE_ROWS,
              out_into=yp)
    ys = _ple(x1s, 0, ns, y_b, TOP_K * (t - th), gate_s.T, p_sample_all, layer * ns, g_ple, wg_bf, wp_bf, ns)
    new_pool_s = jnp.transpose(nst_t, (1, 0, 2))
    return yp, ys, nk_p, nv_p, nu_p[POOL_HALO - POOL_PREFIX:], nk_s, nv_s, new_pool_s


def kernel(x_prompt, x_sample, cache_k, cache_v, state_pool, p_prompt, p_sample, norm_attn, w_in, q_norm, k_norm,
           attn_sinks, w_pool, pool_scale, w_out, norm_ffn, w_router, b_router, w_gate_up, b_gate_up, w_down, b_down,
           norm_ple, w_ple_gate, w_ple_proj):
    depth = norm_attn.shape[0]
    batch, seq, d = x_prompt.shape
    ns, dec_seq, _ = x_sample.shape
    wb = cache_k.shape[2]
    assert batch == 1 and dec_seq == 1 and d == D_MODEL and wb == WINDOW
    assert cache_k.shape[3:] == (N_KV_HEADS, HEAD_DIM) and state_pool.shape[2:] == (POOL_PREFIX, POOL_WIDTH)
    past_len = PAST_LEN
    yp = x_prompt.reshape(seq, d)
    ys = x_sample.reshape(ns, d)
    p_prompt_all = p_prompt.reshape(depth * seq, PLE_DIM)
    p_sample_all = p_sample.reshape(depth * ns, PLE_DIM)
    tabs_p = _rope_tables(jnp.arange(seq))
    tabs_s = _rope_tables(jnp.full((1,), past_len))
    outs = [[] for _ in range(6)]
    for i in range(depth):
        res = _layer(i, yp, ys, cache_k[i], cache_v[i], state_pool[i], p_prompt_all, p_sample_all, past_len,
                     i < depth - 1, tabs_p, tabs_s,
                     norm_attn[i], w_in[i], q_norm[i], k_norm[i], attn_sinks[i], w_pool[i], pool_scale[i], w_out[i],
                     norm_ffn[i], w_router[i], b_router[i], w_gate_up, b_gate_up, w_down, b_down,
                     norm_ple[i], w_ple_gate[i], w_ple_proj[i])
        yp, ys = res[0], res[1]
        kv_shape = (1, WINDOW, N_KV_HEADS, HEAD_DIM)
        outs[0].append(res[2].reshape(kv_shape))
        outs[1].append(res[3].reshape(kv_shape))
        outs[2].append(res[4].reshape(1, POOL_PREFIX, POOL_WIDTH))
        outs[3].append(res[5].reshape(ns, wb, N_KV_HEADS, HEAD_DIM))
        outs[4].append(res[6].reshape(ns, wb, N_KV_HEADS, HEAD_DIM))
        outs[5].append(res[7])
    return (yp.reshape(batch, seq, d), ys.reshape(ns, dec_seq, d)) + tuple(jnp.stack(o) for o in outs)
```

```python
import functools

import jax
import jax.numpy as jnp
import numpy as np
from jax import lax
from jax.experimental import pallas as pl
from jax.experimental.pallas import tpu as pltpu
from jax.experimental.pallas import tpu_sc as plsc

F32 = jnp.float32
BF16 = jnp.bfloat16
U32 = jnp.uint32
I32 = jnp.int32

D_MODEL = 1024
HEAD_DIM = 64
N_HEADS = 8
N_KV_HEADS = 2
GROUP = N_HEADS // N_KV_HEADS
ATTN_WIDTH = N_HEADS * HEAD_DIM
KV_WIDTH = N_KV_HEADS * HEAD_DIM
POOL_WIDTH = 512
POOL_WINDOWS = (2, 4, 8, 16)
POOL_GC = POOL_WIDTH // len(POOL_WINDOWS)
POOL_PREFIX = max(POOL_WINDOWS) - 1
POOL_HALO = POOL_PREFIX + 1
POOL_PAD = 8
IN_WIDTH = ATTN_WIDTH + 2 * KV_WIDTH + POOL_WIDTH
WINDOW = 128
ROPE_THETA = 500000.0
ROT_DIM = HEAD_DIM // 4
N_EXPERTS = 32
TOP_K = 4
D_FF = 1024
SWIGLU_ALPHA = 1.702
SWIGLU_LIMIT = 7.0
PLE_DIM = 256
PAST_LEN = 16384
EPS = 1e-5
NEG_INF = -1e30

LANES = 128
Q_TILES = ATTN_WIDTH // LANES

MIX_ROWS = 512
MIX_TAIL_ROWS = 128
SAMPLE_CHUNK = 16
ROUTE_ROWS = 512
MOE_BM = 512
SC_CORES = 2
SC_SUBCORES = 16
SC_WORKERS = SC_CORES * SC_SUBCORES
SC_CHUNK = 64
PLE_ROWS = 512
VMEM_LIMIT = 56 * 1024 * 1024


def _rms(x, g):
    return x * lax.rsqrt(jnp.mean(x * x, axis=-1, keepdims=True) + EPS) * g


def _pack_halves(x):
    w = x.shape[1] // 2
    lo = lax.bitcast_convert_type(x[:, :w].astype(BF16).astype(F32), U32) >> 16
    hi = lax.bitcast_convert_type(x[:, w:].astype(BF16).astype(F32), U32) & jnp.uint32(0xFFFF0000)
    return lo | hi


def _unpack_halves(packed):
    lo = lax.bitcast_convert_type(packed << 16, F32)
    hi = lax.bitcast_convert_type(packed & jnp.uint32(0xFFFF0000), F32)
    return jnp.concatenate([lo, hi], axis=1)


def _mm(a, b, nt=False):
    dims = (((1,), (1 if nt else 0,)), ((), ()))
    if b.dtype == F32:
        return lax.dot_general(a.astype(F32), b, dims, preferred_element_type=F32, precision=lax.Precision.HIGHEST)
    return lax.dot_general(a.astype(BF16), b, dims, preferred_element_type=F32)


def _head_norm_rope(t, hmean, gain, cos, sin_a, sin_b):
    t = t * lax.rsqrt(_mm(t * t, hmean) + EPS) * gain
    return t * cos + pltpu.roll(t, ROT_DIM // 2, axis=1) * sin_a + pltpu.roll(t, LANES - ROT_DIM // 2, axis=1) * sin_b


def _softmax_pv(s, sink, v, ones):
    m = jnp.maximum(jnp.max(s, axis=-1, keepdims=True), sink)
    e = jnp.exp(s - m).astype(v.dtype)
    den = _mm(e, ones) + jnp.exp(sink - m)
    return _mm(e, v) / den


def _mixer_prompt_kernel(sinks_ref, x_ref, cos_ref, sa_ref, sb_ref, g_ref, win_ref, qn_ref, kn_ref, hm_ref,
                         wpool_ref, pscale_ref, wout_ref, *rest, row_offset, aliased):
    x1_ref, klast_ref, vlast_ref, ulast_ref, kprev, vprev, uext, mix, *lvl = rest[1:] if aliased else rest
    i = pl.program_id(0)
    rows = x_ref.shape[0]
    n_sub = rows // WINDOW
    cdt = win_ref.dtype
    row0 = row_offset + i * rows

    @pl.when(i == 0)
    def _():
        kprev[...] = jnp.zeros_like(kprev)
        vprev[...] = jnp.zeros_like(vprev)
        uext[0:POOL_PAD + POOL_HALO, :] = jnp.zeros((POOL_PAD + POOL_HALO, POOL_WIDTH), F32)
        for buf in lvl:
            buf[0:POOL_PAD, :] = jnp.zeros((POOL_PAD, POOL_GC), F32)

    x = x_ref[...]
    proj = _mm(_rms(x, g_ref[...]), win_ref[...])
    cos, sin_a, sin_b = cos_ref[...], sa_ref[...], sb_ref[...]

    n_t = Q_TILES + 1
    t_all = jnp.concatenate([proj[:, j * LANES:(j + 1) * LANES] for j in range(n_t)], axis=0)
    t3 = (t_all * lax.rsqrt(_mm(t_all * t_all, hm_ref[...]) + EPS)).reshape(n_t, rows, LANES)
    t3 = jnp.concatenate([t3[:Q_TILES] * (qn_ref[...] * HEAD_DIM ** -0.5), t3[Q_TILES:] * kn_ref[...]], axis=0)
    t2 = t3.reshape(n_t * rows, LANES)
    t3 = (t3 * cos + pltpu.roll(t2, ROT_DIM // 2, axis=1).reshape(n_t, rows, LANES) * sin_a
          + pltpu.roll(t2, LANES - ROT_DIM // 2, axis=1).reshape(n_t, rows, LANES) * sin_b)
    q3 = t3[:Q_TILES]
    k = t3[Q_TILES]
    v = proj[:, ATTN_WIDTH + KV_WIDTH:ATTN_WIDTH + 2 * KV_WIDTH]
    u = proj[:, ATTN_WIDTH + 2 * KV_WIDTH:]
    klast_ref[...] = k[rows - WINDOW:, :]
    vlast_ref[...] = v[rows - WINDOW:, :]
    ulast_ref[...] = u[rows - POOL_HALO:, :]
    k_c = k.astype(cdt)
    v_c = jnp.concatenate([v.astype(cdt), jnp.ones((rows, LANES), cdt)], axis=1)
    v_first = jnp.concatenate([vprev[...], jnp.ones((WINDOW, LANES), cdt)], axis=1)

    lane = lax.broadcasted_iota(I32, (WINDOW, LANES), 1)
    left = (lane < HEAD_DIM)[None]
    qi = lax.broadcasted_iota(I32, (WINDOW, 2 * WINDOW), 0)
    kj = lax.broadcasted_iota(I32, (WINDOW, 2 * WINDOW), 1)
    band = (kj - qi >= 1) & (kj - qi <= WINDOW)
    sink3 = jnp.concatenate([jnp.full((1, 1, 1), sinks_ref[j + Q_TILES * s], F32)
                             for j in range(Q_TILES) for s in range(2)], axis=0)
    n_g = 2 * Q_TILES

    v_cats, masks, scores = [], [], []
    for c in range(n_sub):
        r0 = c * WINDOW
        if c == 0:
            k_cat = jnp.concatenate([kprev[...], k_c[0:WINDOW]], axis=0)
            v_cats.append(jnp.concatenate([v_first, v_c[0:WINDOW]], axis=0))
            masks.append(band & (kj + (row0 - WINDOW) >= 0))
        else:
            k_cat = k_c[r0 - WINDOW:r0 + WINDOW]
            v_cats.append(v_c[r0 - WINDOW:r0 + WINDOW])
            masks.append(band)
        q_c = q3[:, r0:r0 + WINDOW, :]
        q_all = jnp.concatenate([jnp.where(left, q_c, 0.0), jnp.where(left, 0.0, q_c)], axis=1)
        scores.append(_mm(q_all.reshape(n_g * WINDOW, LANES), k_cat, nt=True))
    kprev[...] = k_c[rows - WINDOW:]
    vprev[...] = v[rows - WINDOW:].astype(cdt)

    probs, maxes = [], []
    for c in range(n_sub):
        s = jnp.where(masks[c][None], scores[c].reshape(n_g, WINDOW, 2 * WINDOW), NEG_INF)
        m = jnp.maximum(jnp.max(s, axis=-1, keepdims=True), sink3)
        probs.append(jnp.exp(s - m).astype(cdt).reshape(n_g * WINDOW, 2 * WINDOW))
        maxes.append(m)
    applied = [_mm(probs[c], v_cats[c]) for c in range(n_sub)]
    for c in range(n_sub):
        r0 = c * WINDOW
        den = applied[c][:, LANES:] + jnp.exp(sink3 - maxes[c]).reshape(n_g * WINDOW, 1)
        o = (applied[c][:, :LANES] / den).reshape(Q_TILES, 2 * WINDOW, LANES)
        a = jnp.where(left, o[:, :WINDOW], o[:, WINDOW:])
        for j in range(Q_TILES):
            mix[r0:r0 + WINDOW, j * LANES:(j + 1) * LANES] = a[j].astype(cdt)

    base = POOL_PAD + POOL_HALO
    ext = base + rows
    uext[base:ext, :] = u
    pos1 = (lax.broadcasted_iota(I32, (rows, 1), 0) + row0 + 1).astype(F32)
    lvl_of = {1: lvl[0:1], 2: lvl[1:3], 3: lvl[3:5]}
    for gi, w in enumerate(POOL_WINDOWS):
        cols = slice(gi * POOL_GC, (gi + 1) * POOL_GC)
        src, src_cols = uext, cols
        for level in range(1, gi + 2):
            sft = 1 << (level - 1)
            if level <= gi:
                dst = lvl_of[gi][(level - 1) % 2]
                dst[POOL_PAD:ext, :] = src[POOL_PAD:ext, src_cols] + src[POOL_PAD - sft:ext - sft, src_cols]
                src, src_cols = dst, slice(None)
            else:
                wsum = src[base:ext, src_cols] + src[base - sft:ext - sft, src_cols]
        d = wsum / jnp.minimum(pos1, float(w)) - u[:, cols]
        y = _mm(d, wpool_ref[gi]) * pscale_ref[:, cols]
        mix[:, ATTN_WIDTH + gi * POOL_GC:ATTN_WIDTH + (gi + 1) * POOL_GC] = y.astype(cdt)
    uext[POOL_PAD:base, :] = u[rows - POOL_HALO:, :]

    x1_ref[...] = x + _mm(mix[...], wout_ref[...])


def _mixer_prompt(x_full, row_offset, rows, sinks, tabs, g_attn, w_in, qn, kn, hmean, w_pool, pscale, w_out,
                  x1_into=None):
    t = x_full.shape[0] - row_offset
    cdt = w_in.dtype
    assert t % rows == 0 and row_offset % rows == 0 and rows % WINDOW == 0 and rows >= POOL_HALO
    blk0, n_steps = row_offset // rows, t // rows
    const = lambda shape: pl.BlockSpec(shape, lambda i, *_: (0,) * len(shape))
    row_blk = lambda width: pl.BlockSpec((rows, width), lambda i, *_: (blk0 + i, 0))
    aliased = x1_into is not None
    if aliased:
        assert x1_into.shape == x_full.shape
        x1_spec = pl.BlockSpec((rows, D_MODEL), lambda i, *_: (blk0 + n_steps - 1, 0))
        x1_shape = x1_into.shape
        extra_specs, extra_args, aliases = [pl.BlockSpec(memory_space=pl.ANY)], [x1_into], {13: 0}
    else:
        x1_spec = pl.BlockSpec((rows, D_MODEL), lambda i, *_: (i, 0))
        x1_shape = (t, D_MODEL)
        extra_specs, extra_args, aliases = [], [], {}
    grid_spec = pltpu.PrefetchScalarGridSpec(
        num_scalar_prefetch=1,
        grid=(n_steps,),
        in_specs=[row_blk(D_MODEL), row_blk(LANES), row_blk(LANES), row_blk(LANES),
                  const((1, D_MODEL)), const((D_MODEL, IN_WIDTH)), const((1, LANES)), const((1, LANES)),
                  const((LANES, LANES)), const((len(POOL_WINDOWS), POOL_GC, POOL_GC)), const((1, POOL_WIDTH)),
                  const((D_MODEL, D_MODEL))] + extra_specs,
        out_specs=[x1_spec, const((WINDOW, KV_WIDTH)), const((WINDOW, KV_WIDTH)),
                   const((POOL_HALO, POOL_WIDTH))],
        scratch_shapes=[pltpu.VMEM((WINDOW, KV_WIDTH), cdt), pltpu.VMEM((WINDOW, KV_WIDTH), cdt),
                        pltpu.VMEM((POOL_PAD + POOL_HALO + rows, POOL_WIDTH), F32), pltpu.VMEM((rows, D_MODEL), cdt)]
        + [pltpu.VMEM((POOL_PAD + POOL_HALO + rows, POOL_GC), F32)] * 5,
    )
    return pl.pallas_call(
        functools.partial(_mixer_prompt_kernel, row_offset=row_offset, aliased=aliased),
        grid_spec=grid_spec,
        out_shape=[jax.ShapeDtypeStruct(x1_shape, F32), jax.ShapeDtypeStruct((WINDOW, KV_WIDTH), F32),
                   jax.ShapeDtypeStruct((WINDOW, KV_WIDTH), F32), jax.ShapeDtypeStruct((POOL_HALO, POOL_WIDTH), F32)],
        input_output_aliases=aliases,
        compiler_params=pltpu.CompilerParams(dimension_semantics=("arbitrary",), vmem_limit_bytes=VMEM_LIMIT),
        name="mixer_prompt",
    )(sinks, x_full, *tabs, g_attn, w_in, qn, kn, hmean, w_pool, pscale, w_out, *extra_args)


def _mixer_sample_kernel(x_ref, ck_ref, cv_ref, st_ref, cos_ref, sa_ref, sb_ref, sink8_ref, g_ref, win_ref, qn_ref,
                         kn_ref, hm_ref, wpool_ref, pscale_ref, wout_ref,
                         x1_ref, nk_ref, nv_ref, nst_ref, o8, *, pos):
    nb = x_ref.shape[0]
    wb = ck_ref.shape[1]
    x = x_ref[...]
    h = _rms(x, g_ref[...]).astype(BF16)
    proj = jnp.dot(h, win_ref[...], preferred_element_type=F32)
    cos, sin_a, sin_b = cos_ref[...], sa_ref[...], sb_ref[...]
    hmean = hm_ref[...]
    k = _head_norm_rope(proj[:, ATTN_WIDTH:ATTN_WIDTH + KV_WIDTH], hmean, kn_ref[...], cos, sin_a, sin_b)
    v = proj[:, ATTN_WIDTH + KV_WIDTH:ATTN_WIDTH + 2 * KV_WIDTH]
    u = proj[:, ATTN_WIDTH + 2 * KV_WIDTH:]

    nk_ref[:, 0:wb - 1, :] = ck_ref[:, 1:wb, :]
    nv_ref[:, 0:wb - 1, :] = cv_ref[:, 1:wb, :]
    for b in range(nb):
        nk_ref[b, wb - 1:wb, :] = k[b:b + 1, :]
        nv_ref[b, wb - 1:wb, :] = v[b:b + 1, :]

    r8 = lax.broadcasted_iota(I32, (nb * 8, LANES), 0)
    lane8 = lax.broadcasted_iota(I32, (nb * 8, LANES), 1)
    keep = (lane8 < HEAD_DIM) == (r8 % 2 == 0)
    rep = (lax.broadcasted_iota(I32, (nb * 8, nb), 0) // 8 == lax.broadcasted_iota(I32, (nb * 8, nb), 1)).astype(BF16)
    q8 = jnp.zeros((nb * 8, LANES), F32)
    scale = HEAD_DIM ** -0.5
    for j in range(Q_TILES):
        qt = _head_norm_rope(proj[:, j * LANES:(j + 1) * LANES], hmean, qn_ref[...], cos, sin_a, sin_b) * scale
        qrep = jnp.dot(rep, qt.astype(BF16), preferred_element_type=F32)
        q8 = jnp.where(keep & ((r8 % 8) // 2 == j), qrep, q8)
    q8 = q8.astype(BF16)

    sink8 = sink8_ref[:, 0:1]
    ones_bf = jnp.ones((wb, LANES), BF16)
    assert pos >= wb - 1 and wb <= WINDOW
    for b in range(nb):
        kb = nk_ref[b].astype(BF16)
        vb = nv_ref[b].astype(BF16)
        s = lax.dot_general(q8[b * 8:(b + 1) * 8], kb, (((1,), (1,)), ((), ())), preferred_element_type=F32)
        o8[b * 8:(b + 1) * 8, :] = _softmax_pv(s, sink8, vb, ones_bf)
    o8m = jnp.where(keep, o8[...], 0.0).astype(BF16)

    a_tiles = []
    sel_r = lax.broadcasted_iota(I32, (nb, nb * 8), 1)
    sel_b = lax.broadcasted_iota(I32, (nb, nb * 8), 0)
    for j in range(Q_TILES):
        sel = ((sel_r // 8 == sel_b) & ((sel_r % 8) // 2 == j)).astype(BF16)
        a_tiles.append(jnp.dot(sel, o8m, preferred_element_type=F32))

    z_tiles = []
    for gi, w in enumerate(POOL_WINDOWS):
        cols = slice(gi * POOL_GC, (gi + 1) * POOL_GC)
        wsum = u[:, cols]
        for sft in range(1, w):
            wsum = wsum + st_ref[POOL_PREFIX - sft, :, cols]
        d = wsum / float(min(pos + 1, w)) - u[:, cols]
        z_tiles.append(jnp.dot(d.astype(BF16), wpool_ref[gi], preferred_element_type=F32) * pscale_ref[:, cols])
    nst_ref[0:POOL_PREFIX - 1] = st_ref[1:POOL_PREFIX]
    nst_ref[POOL_PREFIX - 1] = u

    mixv = jnp.concatenate(a_tiles + z_tiles, axis=1).astype(BF16)
    x1_ref[...] = x + jnp.dot(mixv, wout_ref[...], preferred_element_type=F32)


def _mixer_sample(x, ck, cv, st, pos, tabs, sink8, g_attn, w_in, qn, kn, hmean, w_pool, pscale, w_out):
    n, wb = ck.shape[0], ck.shape[1]
    nb = SAMPLE_CHUNK
    assert n % nb == 0
    const = lambda shape: pl.BlockSpec(shape, lambda i: (0,) * len(shape))
    cache_blk = pl.BlockSpec((nb, wb, KV_WIDTH), lambda i: (i, 0, 0))
    st_blk = pl.BlockSpec((POOL_PREFIX, nb, POOL_WIDTH), lambda i: (0, i, 0))
    x_blk = pl.BlockSpec((nb, D_MODEL), lambda i: (i, 0))
    return pl.pallas_call(
        functools.partial(_mixer_sample_kernel, pos=pos),
        grid=(n // nb,),
        in_specs=[x_blk, cache_blk, cache_blk, st_blk, const((1, LANES)), const((1, LANES)), const((1, LANES)),
                  const((8, LANES)), const((1, D_MODEL)), const((D_MODEL, IN_WIDTH)), const((1, LANES)),
                  const((1, LANES)), const((LANES, LANES)), const((len(POOL_WINDOWS), POOL_GC, POOL_GC)),
                  const((1, POOL_WIDTH)), const((D_MODEL, D_MODEL))],
        out_specs=[x_blk, cache_blk, cache_blk, st_blk],
        out_shape=[jax.ShapeDtypeStruct((n, D_MODEL), F32), jax.ShapeDtypeStruct(ck.shape, F32),
                   jax.ShapeDtypeStruct(cv.shape, F32), jax.ShapeDtypeStruct(st.shape, F32)],
        scratch_shapes=[pltpu.VMEM((nb * 8, LANES), F32)],
        compiler_params=pltpu.CompilerParams(dimension_semantics=("arbitrary",), vmem_limit_bytes=VMEM_LIMIT),
        name="mixer_sample",
    )(x, ck, cv, st, *tabs, sink8, g_attn, w_in, qn, kn, hmean, w_pool, pscale, w_out)


def _block_plan(cnt, bm, n_lanes):
    e_sub = lax.broadcasted_iota(I32, (N_EXPERTS, LANES), 0)
    e_lane = lax.broadcasted_iota(I32, (N_EXPERTS, LANES), 1)
    padded = jnp.floor((cnt + (bm - 1)) / bm) * bm
    padded_lane = jnp.sum(jnp.where(e_sub == e_lane, padded, 0.0), axis=0, keepdims=True)
    pad_end = jnp.sum(jnp.where(e_lane <= e_sub, padded_lane, 0.0), axis=1, keepdims=True)
    pad_start = pad_end - padded
    blk_start = lax.broadcasted_iota(I32, (N_EXPERTS, n_lanes), 1).astype(F32) * bm
    blk_e = jnp.minimum(jnp.sum((pad_end <= blk_start).astype(F32), axis=0, keepdims=True), N_EXPERTS - 1.0)
    mine = lax.broadcasted_iota(I32, (N_EXPERTS, n_lanes), 0).astype(F32) == blk_e
    last = jnp.sum(jnp.where(mine, pad_start + cnt, 0.0), axis=0, keepdims=True)
    blk_valid = jnp.clip(last - blk_start[0:1], 0.0, float(bm))
    e_blk = lax.broadcasted_iota(I32, (N_EXPERTS, n_lanes), 0).astype(F32)
    later = jnp.min(jnp.where((e_blk > blk_e) & (cnt > 0.0), e_blk, float(N_EXPERTS)), axis=0, keepdims=True)
    blk_next = jnp.where(later < N_EXPERTS, later, -1.0)
    return pad_start, blk_e, blk_valid, blk_next


def _route_kernel(x1_ref, g_ref, wr_ref, br_ref, cnt_in_ref, hf_ref, idx_ref, gate_ref, pos_ref, cnt_ref,
                  pstart_ref, blke_ref, blkv_ref, blkn_ref, counts, *, bm):
    i = pl.program_id(0)
    rows = x1_ref.shape[0]

    @pl.when(i == 0)
    def _():
        counts[...] = cnt_in_ref[...]

    h = _rms(x1_ref[...], g_ref[...])
    hf_ref[...] = _pack_halves(h)

    logits = lax.dot_general(wr_ref[...], h, (((1,), (1,)), ((), ())), preferred_element_type=F32,
                             precision=lax.Precision.HIGHEST) + br_ref[...]
    eid = lax.broadcasted_iota(I32, (N_EXPERTS, rows), 0).astype(F32)
    work = logits
    vals, hots = [], []
    for kk in range(TOP_K):
        m = jnp.max(work, axis=0, keepdims=True)
        first = jnp.min(jnp.where(work == m, eid, float(N_EXPERTS)), axis=0, keepdims=True)
        hot = eid == first
        work = jnp.where(hot, -jnp.inf, work)
        vals.append(m)
        hots.append(hot)
        idx_ref[kk:kk + 1, :] = first.astype(I32)
    es = [jnp.exp(vv - vals[0]) for vv in vals]
    den = es[0] + es[1] + es[2] + es[3]
    for kk in range(TOP_K):
        gate_ref[kk:kk + 1, :] = es[kk] / den

    chosen = hots[0] | hots[1] | hots[2] | hots[3]
    before = (lax.broadcasted_iota(I32, (rows, rows), 0) < lax.broadcasted_iota(I32, (rows, rows), 1)).astype(BF16)
    rank = jnp.dot(chosen.astype(BF16), before, preferred_element_type=F32) + counts[...]
    for kk in range(TOP_K):
        pos_ref[kk:kk + 1, :] = jnp.sum(jnp.where(hots[kk], rank, 0.0), axis=0, keepdims=True).astype(I32)
    counts[...] = counts[...] + jnp.sum(chosen.astype(F32), axis=1, keepdims=True)
    cnt_ref[...] = counts[...]

    @pl.when(i == pl.num_programs(0) - 1)
    def _():
        pad_start, blk_e, blk_valid, blk_next = _block_plan(counts[...], bm, blke_ref.shape[1])
        pstart_ref[...] = pad_start.astype(I32)
        blke_ref[...] = blk_e.astype(I32)
        blkv_ref[...] = blk_valid.astype(I32)
        blkn_ref[...] = blk_next.astype(I32)


def _route(x1, g_ffn, wr_t, br, cnt_in, rows, bm, n_blocks):
    n = x1.shape[0]
    assert n % rows == 0
    nb_lanes = -(-n_blocks // LANES) * LANES
    const = lambda shape: pl.BlockSpec(shape, lambda i: (0,) * len(shape))
    tok_blk = pl.BlockSpec((TOP_K, rows), lambda i: (0, i))
    return pl.pallas_call(
        functools.partial(_route_kernel, bm=bm),
        grid=(n // rows,),
        in_specs=[pl.BlockSpec((rows, D_MODEL), lambda i: (i, 0)), const((1, D_MODEL)), const((N_EXPERTS, D_MODEL)),
                  const((N_EXPERTS, 1)), const((N_EXPERTS, 1))],
        out_specs=[pl.BlockSpec((rows, D_MODEL // 2), lambda i: (i, 0)), tok_blk, tok_blk, tok_blk,
                   const((N_EXPERTS, 1)), const((N_EXPERTS, 1)), const((1, nb_lanes)), const((1, nb_lanes)),
                   const((1, nb_lanes))],
        out_shape=[jax.ShapeDtypeStruct((n, D_MODEL // 2), U32), jax.ShapeDtypeStruct((TOP_K, n), I32),
                   jax.ShapeDtypeStruct((TOP_K, n), F32), jax.ShapeDtypeStruct((TOP_K, n), I32),
                   jax.ShapeDtypeStruct((N_EXPERTS, 1), F32), jax.ShapeDtypeStruct((N_EXPERTS, 1), I32),
                   jax.ShapeDtypeStruct((1, nb_lanes), I32), jax.ShapeDtypeStruct((1, nb_lanes), I32),
                   jax.ShapeDtypeStruct((1, nb_lanes), I32)],
        scratch_shapes=[pltpu.VMEM((N_EXPERTS, 1), F32)],
        compiler_params=pltpu.CompilerParams(dimension_semantics=("arbitrary",), vmem_limit_bytes=VMEM_LIMIT),
        name="route",
    )(x1, g_ffn, wr_t, br, cnt_in)


def _sc_mesh():
    return plsc.VectorSubcoreMesh(core_axis_name="core", subcore_axis_name="subcore")


def _sc_worker_id():
    return lax.axis_index("core") * SC_SUBCORES + lax.axis_index("subcore")


def _scatter_rows(xa, xb, dest, n_rows):
    ch = SC_CHUNK
    na, w = xa.shape
    n = na + xb.shape[0]
    nk = dest.shape[0]
    assert na % ch == 0 and n % ch == 0 and dest.shape[1] == n and xb.shape[1] == w and xa.dtype == xb.dtype
    n_chunks = n // ch
    dest_c = dest.reshape(nk, n_chunks, ch).transpose(1, 0, 2).reshape(-1)
    dma = pltpu.SemaphoreType.DMA

    @pl.kernel(out_type=jax.ShapeDtypeStruct((n_rows, w), xa.dtype), mesh=_sc_mesh(),
               scratch_types=[pltpu.VMEM((nk * ch,), I32), pltpu.VMEM((ch, w), xa.dtype), dma] + [dma] * nk)
    def scatter_kernel(xa_hbm, xb_hbm, d_hbm, o_hbm, idx_v, buf, sem_i, *sem_s):
        wid = _sc_worker_id()

        @pl.loop(0, -(-n_chunks // SC_WORKERS))
        def _(j):
            c = j * SC_WORKERS + wid

            @pl.when(c < n_chunks)
            def _():
                load_idx = pltpu.make_async_copy(d_hbm.at[pl.ds(c * (nk * ch), nk * ch)], idx_v, sem_i)
                load_idx.start()

                @pl.when(c < na // ch)
                def _():
                    pltpu.sync_copy(xa_hbm.at[pl.ds(c * ch, ch)], buf)

                @pl.when(c >= na // ch)
                def _():
                    pltpu.sync_copy(xb_hbm.at[pl.ds(c * ch - na, ch)], buf)

                load_idx.wait()
                puts = [pltpu.make_async_copy(buf, o_hbm.at[idx_v.at[pl.ds(kk * ch, ch)]], sem_s[kk])
                        for kk in range(nk)]
                for cp in puts:
                    cp.start()
                for cp in puts:
                    cp.wait()

    return scatter_kernel(xa, xb, dest_c)


def _gather_rows(src, idx):
    ch = SC_CHUNK
    m = idx.shape[0]
    w = src.shape[1]
    per = m // SC_WORKERS
    n_pairs = per // (2 * ch)
    assert m % SC_WORKERS == 0 and per % (2 * ch) == 0
    dma = pltpu.SemaphoreType.DMA

    @pl.kernel(out_type=jax.ShapeDtypeStruct((m, w), src.dtype), mesh=_sc_mesh(),
               scratch_types=[pltpu.VMEM((per,), I32), pltpu.VMEM((ch, w), src.dtype), pltpu.VMEM((ch, w), src.dtype),
                              dma, dma, dma, dma])
    def gather_kernel(s_hbm, i_hbm, o_hbm, idx_v, buf_a, buf_b, sem_ga, sem_gb, sem_wa, sem_wb):
        base = _sc_worker_id() * per
        pltpu.sync_copy(i_hbm.at[pl.ds(base, per)], idx_v)

        def fetch(j, buf, sem):
            return pltpu.make_async_copy(s_hbm.at[idx_v.at[pl.ds(j * ch, ch)]], buf, sem)

        def put(j, buf, sem):
            return pltpu.make_async_copy(buf, o_hbm.at[pl.ds(base + j * ch, ch)], sem)

        fetch(0, buf_a, sem_ga).start()

        @pl.loop(0, n_pairs)
        def _(p):
            j0 = 2 * p
            j1 = j0 + 1

            @pl.when(p > 0)
            def _():
                put(j1 - 2, buf_b, sem_wb).wait()

            fetch(j1, buf_b, sem_gb).start()
            fetch(j0, buf_a, sem_ga).wait()
            put(j0, buf_a, sem_wa).start()
            fetch(j1, buf_b, sem_gb).wait()
            put(j1, buf_b, sem_wb).start()
            put(j0, buf_a, sem_wa).wait()

            @pl.when(p + 1 < n_pairs)
            def _():
                fetch(j0 + 2, buf_a, sem_ga).start()

        put(2 * n_pairs - 1, buf_b, sem_wb).wait()

    return gather_kernel(src, idx)


def _moe_kernel(blk_e_ref, blk_valid_ref, blk_next_ref, xs_ref, wgu_hbm, bgu_ref, wd_hbm, bd_ref, y_ref,
                wgu_f32, wd_f32, wgu_bf, wd_bf, sems, *, e0):
    i = pl.program_id(0)
    e = blk_e_ref[i]
    n_valid = blk_valid_ref[i]
    used = n_valid > 0
    new_expert = (i == 0) | (blk_e_ref[jnp.maximum(i - 1, 0)] != e)

    def weight_copies(expert):
        return (pltpu.make_async_copy(wgu_hbm.at[e0 + expert], wgu_f32, sems.at[0]),
                pltpu.make_async_copy(wd_hbm.at[e0 + expert], wd_f32, sems.at[1]))

    @pl.when(used & (i == 0))
    def _():
        for cp in weight_copies(e):
            cp.start()

    @pl.when(used & new_expert)
    def _():
        for cp in weight_copies(e):
            cp.wait()
        wgu_bf[...] = wgu_f32[...].astype(BF16)
        wd_bf[...] = wd_f32[...].astype(BF16)
        nxt = blk_next_ref[i]

        @pl.when(nxt >= 0)
        def _():
            for cp in weight_copies(nxt):
                cp.start(priority=1)

    @pl.when(used)
    def _():
        valid = lax.broadcasted_iota(I32, xs_ref.shape, 0) < n_valid
        xb = _unpack_halves(jnp.where(valid, xs_ref[...], jnp.uint32(0))).astype(BF16)
        gu = jnp.dot(xb, wgu_bf[...], preferred_element_type=F32) + bgu_ref[0]
        g = jnp.minimum(gu[:, :D_FF], SWIGLU_LIMIT)
        up = jnp.clip(gu[:, D_FF:], -SWIGLU_LIMIT, SWIGLU_LIMIT)
        act = (up + 1.0) * (g * jax.nn.sigmoid(SWIGLU_ALPHA * g))
        y_ref[...] = _pack_halves(jnp.dot(act.astype(BF16), wd_bf[...], preferred_element_type=F32) + bd_ref[0])

    @pl.when(jnp.logical_not(used))
    def _():
        y_ref[...] = jnp.zeros_like(y_ref)


def _moe(xs, blk_e, blk_valid, blk_next, layer, w_gu, b_gu, w_d, b_d):
    n_rows = xs.shape[0]
    bm = MOE_BM
    assert n_rows % bm == 0
    e0 = layer * N_EXPERTS
    n_we = w_gu.shape[0] * w_gu.shape[1]
    any_spec = pl.BlockSpec(memory_space=pl.ANY)
    grid_spec = pltpu.PrefetchScalarGridSpec(
        num_scalar_prefetch=3,
        grid=(n_rows // bm,),
        in_specs=[pl.BlockSpec((bm, D_MODEL // 2), lambda i, be, bv, bn: (i, 0)),
                  any_spec,
                  pl.BlockSpec((1, 1, 2 * D_FF), lambda i, be, bv, bn: (e0 + be[i], 0, 0)),
                  any_spec,
                  pl.BlockSpec((1, 1, D_MODEL), lambda i, be, bv, bn: (e0 + be[i], 0, 0))],
        out_specs=pl.BlockSpec((bm, D_MODEL // 2), lambda i, be, bv, bn: (i, 0)),
        scratch_shapes=[pltpu.VMEM((D_MODEL, 2 * D_FF), F32), pltpu.VMEM((D_FF, D_MODEL), F32),
                        pltpu.VMEM((D_MODEL, 2 * D_FF), BF16), pltpu.VMEM((D_FF, D_MODEL), BF16),
                        pltpu.SemaphoreType.DMA((2,))],
    )
    return pl.pallas_call(
        functools.partial(_moe_kernel, e0=e0),
        grid_spec=grid_spec,
        out_shape=jax.ShapeDtypeStruct((n_rows, D_MODEL // 2), U32),
        compiler_params=pltpu.CompilerParams(dimension_semantics=("arbitrary",), vmem_limit_bytes=VMEM_LIMIT),
        name="moe_experts",
    )(blk_e, blk_valid, blk_next, xs, w_gu.reshape(n_we, D_MODEL, 2 * D_FF), b_gu.reshape(n_we, 1, 2 * D_FF),
      w_d.reshape(n_we, D_FF, D_MODEL), b_d.reshape(n_we, 1, D_MODEL))


def _ple_kernel(x1_ref, y0_ref, y1_ref, y2_ref, y3_ref, gates_ref, p_ref, g_ref, wg_ref, wp_ref, *rest):
    out_ref = rest[-1]
    x2 = x1_ref[...]
    gates = gates_ref[...]
    for kk, y_ref in enumerate((y0_ref, y1_ref, y2_ref, y3_ref)):
        x2 = x2 + _unpack_halves(y_ref[...]) * gates[:, kk:kk + 1]
    hp = _rms(x2, g_ref[...]).astype(BF16)
    gate = jax.nn.sigmoid(jnp.dot(hp, wg_ref[...], preferred_element_type=F32))
    pp = jnp.dot(p_ref[...].astype(BF16), wp_ref[...], preferred_element_type=F32)
    out_ref[...] = x2 + gate * pp


def _ple(x1, tok0, n, y_tok, y0, gates_t, p_all, p0, g_ple, w_gate, w_proj, rows, out_into=None):
    assert n % rows == 0 and tok0 % rows == 0 and y0 % rows == 0 and p0 % rows == 0
    const = lambda shape: pl.BlockSpec(shape, lambda i: (0,) * len(shape))
    tok_blk = lambda width: pl.BlockSpec((rows, width), lambda i: (tok0 // rows + i, 0))
    y_blk = lambda kk: pl.BlockSpec((rows, D_MODEL // 2), lambda i: ((y0 + kk * n) // rows + i, 0))
    extra_specs, extra_args, aliases = [], [], {}
    if out_into is not None:
        assert out_into.shape == x1.shape
        extra_specs, extra_args, aliases = [pl.BlockSpec(memory_space=pl.ANY)], [out_into], {10: 0}
    return pl.pallas_call(
        _ple_kernel,
        grid=(n // rows,),
        in_specs=[tok_blk(D_MODEL), y_blk(0), y_blk(1), y_blk(2), y_blk(3), tok_blk(TOP_K),
                  pl.BlockSpec((rows, PLE_DIM), lambda i: (p0 // rows + i, 0)),
                  const((1, D_MODEL)), const((D_MODEL, D_MODEL)), const((PLE_DIM, D_MODEL))] + extra_specs,
        out_specs=tok_blk(D_MODEL),
        out_shape=jax.ShapeDtypeStruct(x1.shape, F32),
        input_output_aliases=aliases,
        compiler_params=pltpu.CompilerParams(dimension_semantics=("arbitrary",), vmem_limit_bytes=VMEM_LIMIT),
        name="combine_ple",
    )(x1, y_tok, y_tok, y_tok, y_tok, gates_t, p_all, g_ple, w_gate, w_proj, *extra_args)


def _rope_tables(pos):
    half = ROT_DIM // 2
    d = np.arange(LANES) % HEAD_DIM
    inv = ROPE_THETA ** (-jnp.arange(half, dtype=F32) / half)
    inv_lane = jnp.where(d < ROT_DIM, inv[d % half], 0.0)
    ang = pos.astype(F32)[:, None] * inv_lane[None, :]
    cos, sin = jnp.cos(ang), jnp.sin(ang)
    sin_a = jnp.where((d >= half) & (d < ROT_DIM), sin, 0.0)
    sin_b = jnp.where(d < half, -sin, 0.0)
    return cos, sin_a, sin_b


def _layer(layer, xp, xs, ck, cv, st, p_prompt_all, p_sample_all, past_len, refine_tail, tabs_p, tabs_s,
           norm_attn, w_in, q_norm, k_norm, attn_sinks, w_pool, pool_scale, w_out,
           norm_ffn, w_router, b_router, w_gate_up_all, b_gate_up_all, w_down_all, b_down_all,
           norm_ple, w_ple_gate, w_ple_proj):
    t, ns = xp.shape[0], xs.shape[0]
    w_q = w_in[:, :ATTN_WIDTH].reshape(D_MODEL, 2, Q_TILES, HEAD_DIM).transpose(0, 2, 1, 3).reshape(D_MODEL, ATTN_WIDTH)
    w_in_p = jnp.concatenate([w_q, w_in[:, ATTN_WIDTH:]], axis=1)
    w_oa = w_out[:ATTN_WIDTH].reshape(2, Q_TILES, HEAD_DIM, D_MODEL).transpose(1, 0, 2, 3).reshape(ATTN_WIDTH, D_MODEL)
    w_out_p = jnp.concatenate([w_oa, w_out[ATTN_WIDTH:]], axis=0)
    g_attn = norm_attn.reshape(1, D_MODEL)
    qn = jnp.tile(q_norm, 2).reshape(1, LANES)
    kn = jnp.tile(k_norm, 2).reshape(1, LANES)
    lane = np.arange(LANES)
    hmean = jnp.asarray((lane[:, None] // HEAD_DIM == lane[None, :] // HEAD_DIM) / HEAD_DIM, F32)
    pscale = pool_scale.reshape(1, POOL_WIDTH)
    sink8 = jnp.broadcast_to(attn_sinks.reshape(2, Q_TILES).T.reshape(8, 1), (8, LANES))

    mix_args = (g_attn, w_in_p.astype(BF16), qn, kn, hmean.astype(BF16), w_pool.astype(BF16), pscale,
                w_out_p.astype(BF16))
    x1p, nk_p, nv_p, nu_p = _mixer_prompt(xp, 0, MIX_ROWS, attn_sinks, tabs_p, *mix_args)
    if refine_tail:
        hi_args = (g_attn, w_in_p, qn, kn, hmean, w_pool, pscale, w_out_p)
        x1p, nk_p, nv_p, nu_p = _mixer_prompt(xp, t - 2 * MIX_TAIL_ROWS, MIX_TAIL_ROWS, attn_sinks, tabs_p, *hi_args,
                                              x1_into=x1p)
    st_t = jnp.transpose(st, (1, 0, 2))
    x1s, nk_s, nv_s, nst_t = _mixer_sample(xs, ck.reshape(ns, -1, KV_WIDTH), cv.reshape(ns, -1, KV_WIDTH), st_t,
                                           past_len, tabs_s, sink8, *mix_args)

    g_ffn = norm_ffn.reshape(1, D_MODEL)
    wr_t = w_router.T
    br = b_router.reshape(N_EXPERTS, 1)
    n_tok = t + ns
    bm = MOE_BM
    n_blocks = -(-(n_tok * TOP_K + N_EXPERTS * (bm - 1)) // bm)
    hf_p, idx_p, gate_p, pos_p, cnt_p = _route(x1p, g_ffn, wr_t, br, jnp.zeros((N_EXPERTS, 1), F32),
                                               ROUTE_ROWS, bm, n_blocks)[:5]
    hf_s, idx_s, gate_s, pos_s, _, pad_start, blk_e, blk_valid, blk_next = _route(x1s, g_ffn, wr_t, br, cnt_p, ns,
                                                                                  bm, n_blocks)
    idx = jnp.concatenate([idx_p, idx_s], axis=1)
    pos = jnp.concatenate([pos_p, pos_s], axis=1)
    blk_e, blk_valid, blk_next = (a[0, :n_blocks] for a in (blk_e, blk_valid, blk_next))
    start_of = jnp.sum(jnp.where(idx[None] == jnp.arange(N_EXPERTS, dtype=I32)[:, None, None],
                                 pad_start.reshape(N_EXPERTS, 1, 1), 0), axis=0)
    dest = start_of + pos

    xs_rows = _scatter_rows(hf_p, hf_s, dest, n_blocks * bm)
    y = _moe(xs_rows, blk_e, blk_valid, blk_next, layer, w_gate_up_all, b_gate_up_all, w_down_all, b_down_all)

    th = t // 2
    unit = SC_WORKERS * SC_CHUNK * 2
    assert (TOP_K * th) % unit == 0
    back_b = jnp.concatenate([dest[:, th:t].reshape(-1), dest[:, t:].reshape(-1)])
    back_b = jnp.concatenate([back_b, jnp.arange(-back_b.shape[0] % unit, dtype=I32)])
    y_a = _gather_rows(y, dest[:, :th].reshape(-1))
    y_b = _gather_rows(y, back_b)

    g_ple = norm_ple.reshape(1, D_MODEL)
    wg_bf = w_ple_gate.astype(BF16)
    wp_bf = w_ple_proj.astype(BF16)
    gates_p = gate_p.T
    yp = _ple(x1p, 0, th, y_a, 0, gates_p, p_prompt_all, layer * t, g_ple, wg_bf, wp_bf, PLE_ROWS)
    yp = _ple(x1p, th, t - th, y_b, 0, gates_p, p_prompt_all, layer * t + th, g_ple, wg_bf, wp_bf, PLE_ROWS,
              out_into=yp)
    ys = _ple(x1s, 0, ns, y_b, TOP_K * (t - th), gate_s.T, p_sample_all, layer * ns, g_ple, wg_bf, wp_bf, ns)
    new_pool_s = jnp.transpose(nst_t, (1, 0, 2))
    return yp, ys, nk_p, nv_p, nu_p[POOL_HALO - POOL_PREFIX:], nk_s, nv_s, new_pool_s


def kernel(x_prompt, x_sample, cache_k, cache_v, state_pool, p_prompt, p_sample, norm_attn, w_in, q_norm, k_norm,
           attn_sinks, w_pool, pool_scale, w_out, norm_ffn, w_router, b_router, w_gate_up, b_gate_up, w_down, b_down,
           norm_ple, w_ple_gate, w_ple_proj):
    depth = norm_attn.shape[0]
    batch, seq, d = x_prompt.shape
    ns, dec_seq, _ = x_sample.shape
    wb = cache_k.shape[2]
    assert batch == 1 and dec_seq == 1 and d == D_MODEL and wb == WINDOW
    assert cache_k.shape[3:] == (N_KV_HEADS, HEAD_DIM) and state_pool.shape[2:] == (POOL_PREFIX, POOL_WIDTH)
    past_len = PAST_LEN
    yp = x_prompt.reshape(seq, d)
    ys = x_sample.reshape(ns, d)
    p_prompt_all = p_prompt.reshape(depth * seq, PLE_DIM)
    p_sample_all = p_sample.reshape(depth * ns, PLE_DIM)
    tabs_p = _rope_tables(jnp.arange(seq))
    tabs_s = _rope_tables(jnp.full((1,), past_len))
    outs = [[] for _ in range(6)]
    for i in range(depth):
        res = _layer(i, yp, ys, cache_k[i], cache_v[i], state_pool[i], p_prompt_all, p_sample_all, past_len,
                     i < depth - 1, tabs_p, tabs_s,
                     norm_attn[i], w_in[i], q_norm[i], k_norm[i], attn_sinks[i], w_pool[i], pool_scale[i], w_out[i],
                     norm_ffn[i], w_router[i], b_router[i], w_gate_up, b_gate_up, w_down, b_down,
                     norm_ple[i], w_ple_gate[i], w_ple_proj[i])
        yp, ys = res[0], res[1]
        kv_shape = (1, WINDOW, N_KV_HEADS, HEAD_DIM)
        outs[0].append(res[2].reshape(kv_shape))
        outs[1].append(res[3].reshape(kv_shape))
        outs[2].append(res[4].reshape(1, POOL_PREFIX, POOL_WIDTH))
        outs[3].append(res[5].reshape(ns, wb, N_KV_HEADS, HEAD_DIM))
        outs[4].append(res[6].reshape(ns, wb, N_KV_HEADS, HEAD_DIM))
        outs[5].append(res[7])
    return (yp.reshape(batch, seq, d), ys.reshape(ns, dec_seq, d)) + tuple(jnp.stack(o) for o in outs)
```

```python
import functools

import jax
import jax.numpy as jnp
import numpy as np
from jax import lax
from jax.experimental import pallas as pl
from jax.experimental.pallas import tpu as pltpu
from jax.experimental.pallas import tpu_sc as plsc

F32 = jnp.float32
BF16 = jnp.bfloat16
U32 = jnp.uint32
I32 = jnp.int32

D_MODEL = 1024
HEAD_DIM = 64
N_HEADS = 8
N_KV_HEADS = 2
GROUP = N_HEADS // N_KV_HEADS
ATTN_WIDTH = N_HEADS * HEAD_DIM
KV_WIDTH = N_KV_HEADS * HEAD_DIM
POOL_WIDTH = 512
POOL_WINDOWS = (2, 4, 8, 16)
POOL_GC = POOL_WIDTH // len(POOL_WINDOWS)
POOL_PREFIX = max(POOL_WINDOWS) - 1
POOL_HALO = POOL_PREFIX + 1
POOL_PAD = 8
IN_WIDTH = ATTN_WIDTH + 2 * KV_WIDTH + POOL_WIDTH
WINDOW = 128
ROPE_THETA = 500000.0
ROT_DIM = HEAD_DIM // 4
N_EXPERTS = 32
TOP_K = 4
D_FF = 1024
SWIGLU_ALPHA = 1.702
SWIGLU_LIMIT = 7.0
PLE_DIM = 256
PAST_LEN = 16384
EPS = 1e-5
NEG_INF = -1e30

LANES = 128
Q_TILES = ATTN_WIDTH // LANES

MIX_ROWS = 512
MIX_TAIL_ROWS = 128
SAMPLE_CHUNK = 16
ROUTE_ROWS = 512
MOE_BM = 512
SC_CORES = 2
SC_SUBCORES = 16
SC_WORKERS = SC_CORES * SC_SUBCORES
SC_CHUNK = 64
PLE_ROWS = 512
VMEM_LIMIT = 56 * 1024 * 1024


def _rms(x, g):
    return x * lax.rsqrt(jnp.mean(x * x, axis=-1, keepdims=True) + EPS) * g


def _pack_halves(x):
    w = x.shape[1] // 2
    lo = lax.bitcast_convert_type(x[:, :w].astype(BF16).astype(F32), U32) >> 16
    hi = lax.bitcast_convert_type(x[:, w:].astype(BF16).astype(F32), U32) & jnp.uint32(0xFFFF0000)
    return lo | hi


def _unpack_halves(packed):
    lo = lax.bitcast_convert_type(packed << 16, F32)
    hi = lax.bitcast_convert_type(packed & jnp.uint32(0xFFFF0000), F32)
    return jnp.concatenate([lo, hi], axis=1)


def _mm(a, b, nt=False):
    dims = (((1,), (1 if nt else 0,)), ((), ()))
    if b.dtype == F32:
        return lax.dot_general(a.astype(F32), b, dims, preferred_element_type=F32, precision=lax.Precision.HIGHEST)
    return lax.dot_general(a.astype(BF16), b, dims, preferred_element_type=F32)


def _head_norm_rope(t, hmean, gain, cos, sin_a, sin_b):
    t = t * lax.rsqrt(_mm(t * t, hmean) + EPS) * gain
    return t * cos + pltpu.roll(t, ROT_DIM // 2, axis=1) * sin_a + pltpu.roll(t, LANES - ROT_DIM // 2, axis=1) * sin_b


def _softmax_pv(s, sink, v, ones):
    m = jnp.maximum(jnp.max(s, axis=-1, keepdims=True), sink)
    e = jnp.exp(s - m).astype(v.dtype)
    den = _mm(e, ones) + jnp.exp(sink - m)
    return _mm(e, v) / den


def _prep_mixer_weights(win_ref, wout_ref, win_c, wout_c):
    cdt = win_c.dtype
    for j in range(Q_TILES):
        for s in range(2):
            src = (j + Q_TILES * s) * HEAD_DIM
            dst = j * LANES + s * HEAD_DIM
            win_c[:, dst:dst + HEAD_DIM] = win_ref[:, src:src + HEAD_DIM].astype(cdt)
            wout_c[dst:dst + HEAD_DIM, :] = wout_ref[src:src + HEAD_DIM, :].astype(cdt)
    win_c[:, ATTN_WIDTH:] = win_ref[:, ATTN_WIDTH:].astype(cdt)
    wout_c[ATTN_WIDTH:, :] = wout_ref[ATTN_WIDTH:, :].astype(cdt)


def _mixer_prompt_kernel(sinks_ref, x_ref, cos_ref, sa_ref, sb_ref, g_ref, win_ref, qn_ref, kn_ref, hm_ref,
                         wpool_ref, pscale_ref, wout_ref, *rest, row_offset, aliased):
    (x1_ref, klast_ref, vlast_ref, ulast_ref, kprev, vprev, uext, mix, win_c, wout_c,
     *lvl) = rest[1:] if aliased else rest
    i = pl.program_id(0)
    rows = x_ref.shape[0]
    n_sub = rows // WINDOW
    cdt = win_c.dtype
    row0 = row_offset + i * rows

    @pl.when(i == 0)
    def _():
        _prep_mixer_weights(win_ref, wout_ref, win_c, wout_c)
        kprev[...] = jnp.zeros_like(kprev)
        vprev[...] = jnp.zeros_like(vprev)
        uext[0:POOL_PAD + POOL_HALO, :] = jnp.zeros((POOL_PAD + POOL_HALO, POOL_WIDTH), F32)
        for buf in lvl:
            buf[0:POOL_PAD, :] = jnp.zeros((POOL_PAD, POOL_GC), F32)

    x = x_ref[...]
    proj = _mm(_rms(x, g_ref[...]), win_c[...])
    cos, sin_a, sin_b = cos_ref[...], sa_ref[...], sb_ref[...]

    n_t = Q_TILES + 1
    t_all = jnp.concatenate([proj[:, j * LANES:(j + 1) * LANES] for j in range(n_t)], axis=0)
    t3 = (t_all * lax.rsqrt(_mm(t_all * t_all, hm_ref[...].astype(cdt)) + EPS)).reshape(n_t, rows, LANES)
    t3 = jnp.concatenate([t3[:Q_TILES] * (qn_ref[...] * HEAD_DIM ** -0.5), t3[Q_TILES:] * kn_ref[...]], axis=0)
    t2 = t3.reshape(n_t * rows, LANES)
    t3 = (t3 * cos + pltpu.roll(t2, ROT_DIM // 2, axis=1).reshape(n_t, rows, LANES) * sin_a
          + pltpu.roll(t2, LANES - ROT_DIM // 2, axis=1).reshape(n_t, rows, LANES) * sin_b)
    q3 = t3[:Q_TILES]
    k = t3[Q_TILES]
    v = proj[:, ATTN_WIDTH + KV_WIDTH:ATTN_WIDTH + 2 * KV_WIDTH]
    u = proj[:, ATTN_WIDTH + 2 * KV_WIDTH:]
    klast_ref[...] = k[rows - WINDOW:, :]
    vlast_ref[...] = v[rows - WINDOW:, :]
    ulast_ref[...] = u[rows - POOL_HALO:, :]
    k_c = k.astype(cdt)
    v_c = jnp.concatenate([v.astype(cdt), jnp.ones((rows, LANES), cdt)], axis=1)
    v_first = jnp.concatenate([vprev[...], jnp.ones((WINDOW, LANES), cdt)], axis=1)

    lane = lax.broadcasted_iota(I32, (WINDOW, LANES), 1)
    left = (lane < HEAD_DIM)[None]
    qi = lax.broadcasted_iota(I32, (WINDOW, 2 * WINDOW), 0)
    kj = lax.broadcasted_iota(I32, (WINDOW, 2 * WINDOW), 1)
    band = (kj - qi >= 1) & (kj - qi <= WINDOW)
    sink3 = jnp.concatenate([jnp.full((1, 1, 1), sinks_ref[j + Q_TILES * s], F32)
                             for j in range(Q_TILES) for s in range(2)], axis=0)
    n_g = 2 * Q_TILES

    v_cats, masks, scores = [], [], []
    for c in range(n_sub):
        r0 = c * WINDOW
        if c == 0:
            k_cat = jnp.concatenate([kprev[...], k_c[0:WINDOW]], axis=0)
            v_cats.append(jnp.concatenate([v_first, v_c[0:WINDOW]], axis=0))
            masks.append(band & (kj + (row0 - WINDOW) >= 0))
        else:
            k_cat = k_c[r0 - WINDOW:r0 + WINDOW]
            v_cats.append(v_c[r0 - WINDOW:r0 + WINDOW])
            masks.append(band)
        q_c = q3[:, r0:r0 + WINDOW, :]
        q_all = jnp.concatenate([jnp.where(left, q_c, 0.0), jnp.where(left, 0.0, q_c)], axis=1)
        scores.append(_mm(q_all.reshape(n_g * WINDOW, LANES), k_cat, nt=True))
    kprev[...] = k_c[rows - WINDOW:]
    vprev[...] = v[rows - WINDOW:].astype(cdt)

    probs, maxes = [], []
    for c in range(n_sub):
        s = jnp.where(masks[c][None], scores[c].reshape(n_g, WINDOW, 2 * WINDOW), NEG_INF)
        m = jnp.maximum(jnp.max(s, axis=-1, keepdims=True), sink3)
        probs.append(jnp.exp(s - m).astype(cdt).reshape(n_g * WINDOW, 2 * WINDOW))
        maxes.append(m)
    applied = [_mm(probs[c], v_cats[c]) for c in range(n_sub)]
    for c in range(n_sub):
        r0 = c * WINDOW
        den = applied[c][:, LANES:] + jnp.exp(sink3 - maxes[c]).reshape(n_g * WINDOW, 1)
        o = (applied[c][:, :LANES] / den).reshape(Q_TILES, 2 * WINDOW, LANES)
        a = jnp.where(left, o[:, :WINDOW], o[:, WINDOW:])
        for j in range(Q_TILES):
            mix[r0:r0 + WINDOW, j * LANES:(j + 1) * LANES] = a[j].astype(cdt)

    base = POOL_PAD + POOL_HALO
    ext = base + rows
    uext[base:ext, :] = u
    pos1 = (lax.broadcasted_iota(I32, (rows, 1), 0) + row0 + 1).astype(F32)
    lvl_of = {1: lvl[0:1], 2: lvl[1:3], 3: lvl[3:5]}
    for gi, w in enumerate(POOL_WINDOWS):
        cols = slice(gi * POOL_GC, (gi + 1) * POOL_GC)
        src, src_cols = uext, cols
        for level in range(1, gi + 2):
            sft = 1 << (level - 1)
            if level <= gi:
                dst = lvl_of[gi][(level - 1) % 2]
                dst[POOL_PAD:ext, :] = src[POOL_PAD:ext, src_cols] + src[POOL_PAD - sft:ext - sft, src_cols]
                src, src_cols = dst, slice(None)
            else:
                wsum = src[base:ext, src_cols] + src[base - sft:ext - sft, src_cols]
        d = wsum / jnp.minimum(pos1, float(w)) - u[:, cols]
        y = _mm(d, wpool_ref[gi].astype(cdt)) * pscale_ref[:, cols]
        mix[:, ATTN_WIDTH + gi * POOL_GC:ATTN_WIDTH + (gi + 1) * POOL_GC] = y.astype(cdt)
    uext[POOL_PAD:base, :] = u[rows - POOL_HALO:, :]

    x1_ref[...] = x + _mm(mix[...], wout_c[...])


def _mixer_prompt(x_full, row_offset, rows, cdt, sinks, tabs, g_attn, w_in, qn, kn, hmean, w_pool, pscale, w_out,
                  x1_into=None):
    t = x_full.shape[0] - row_offset
    assert t % rows == 0 and row_offset % rows == 0 and rows % WINDOW == 0 and rows >= POOL_HALO
    blk0, n_steps = row_offset // rows, t // rows
    const = lambda shape: pl.BlockSpec(shape, lambda i, *_: (0,) * len(shape))
    row_blk = lambda width: pl.BlockSpec((rows, width), lambda i, *_: (blk0 + i, 0))
    aliased = x1_into is not None
    if aliased:
        assert x1_into.shape == x_full.shape
        x1_spec = pl.BlockSpec((rows, D_MODEL), lambda i, *_: (blk0 + n_steps - 1, 0))
        x1_shape = x1_into.shape
        extra_specs, extra_args, aliases = [pl.BlockSpec(memory_space=pl.ANY)], [x1_into], {13: 0}
    else:
        x1_spec = pl.BlockSpec((rows, D_MODEL), lambda i, *_: (i, 0))
        x1_shape = (t, D_MODEL)
        extra_specs, extra_args, aliases = [], [], {}
    grid_spec = pltpu.PrefetchScalarGridSpec(
        num_scalar_prefetch=1,
        grid=(n_steps,),
        in_specs=[row_blk(D_MODEL), row_blk(LANES), row_blk(LANES), row_blk(LANES),
                  const((1, D_MODEL)), const((D_MODEL, IN_WIDTH)), const((1, LANES)), const((1, LANES)),
                  const((LANES, LANES)), const((len(POOL_WINDOWS), POOL_GC, POOL_GC)), const((1, POOL_WIDTH)),
                  const((D_MODEL, D_MODEL))] + extra_specs,
        out_specs=[x1_spec, const((WINDOW, KV_WIDTH)), const((WINDOW, KV_WIDTH)),
                   const((POOL_HALO, POOL_WIDTH))],
        scratch_shapes=[pltpu.VMEM((WINDOW, KV_WIDTH), cdt), pltpu.VMEM((WINDOW, KV_WIDTH), cdt),
                        pltpu.VMEM((POOL_PAD + POOL_HALO + rows, POOL_WIDTH), F32), pltpu.VMEM((rows, D_MODEL), cdt),
                        pltpu.VMEM((D_MODEL, IN_WIDTH), cdt), pltpu.VMEM((D_MODEL, D_MODEL), cdt)]
        + [pltpu.VMEM((POOL_PAD + POOL_HALO + rows, POOL_GC), F32)] * 5,
    )
    return pl.pallas_call(
        functools.partial(_mixer_prompt_kernel, row_offset=row_offset, aliased=aliased),
        grid_spec=grid_spec,
        out_shape=[jax.ShapeDtypeStruct(x1_shape, F32), jax.ShapeDtypeStruct((WINDOW, KV_WIDTH), F32),
                   jax.ShapeDtypeStruct((WINDOW, KV_WIDTH), F32), jax.ShapeDtypeStruct((POOL_HALO, POOL_WIDTH), F32)],
        input_output_aliases=aliases,
        compiler_params=pltpu.CompilerParams(dimension_semantics=("arbitrary",), vmem_limit_bytes=VMEM_LIMIT),
        name="mixer_prompt",
    )(sinks, x_full, *tabs, g_attn, w_in, qn, kn, hmean, w_pool, pscale, w_out, *extra_args)


def _mixer_sample_kernel(x_ref, ck_ref, cv_ref, st_ref, cos_ref, sa_ref, sb_ref, sink8_ref, g_ref, win_ref, qn_ref,
                         kn_ref, hm_ref, wpool_ref, pscale_ref, wout_ref,
                         x1_ref, nk_ref, nv_ref, nst_ref, o8, win_c, wout_c, *, pos):
    nb = x_ref.shape[0]
    wb = ck_ref.shape[1]

    @pl.when(pl.program_id(0) == 0)
    def _():
        _prep_mixer_weights(win_ref, wout_ref, win_c, wout_c)

    x = x_ref[...]
    h = _rms(x, g_ref[...]).astype(BF16)
    proj = jnp.dot(h, win_c[...], preferred_element_type=F32)
    cos, sin_a, sin_b = cos_ref[...], sa_ref[...], sb_ref[...]
    hmean = hm_ref[...].astype(BF16)
    k = _head_norm_rope(proj[:, ATTN_WIDTH:ATTN_WIDTH + KV_WIDTH], hmean, kn_ref[...], cos, sin_a, sin_b)
    v = proj[:, ATTN_WIDTH + KV_WIDTH:ATTN_WIDTH + 2 * KV_WIDTH]
    u = proj[:, ATTN_WIDTH + 2 * KV_WIDTH:]

    nk_ref[:, 0:wb - 1, :] = ck_ref[:, 1:wb, :]
    nv_ref[:, 0:wb - 1, :] = cv_ref[:, 1:wb, :]
    for b in range(nb):
        nk_ref[b, wb - 1:wb, :] = k[b:b + 1, :]
        nv_ref[b, wb - 1:wb, :] = v[b:b + 1, :]

    r8 = lax.broadcasted_iota(I32, (nb * 8, LANES), 0)
    lane8 = lax.broadcasted_iota(I32, (nb * 8, LANES), 1)
    keep = (lane8 < HEAD_DIM) == (r8 % 2 == 0)
    rep = (lax.broadcasted_iota(I32, (nb * 8, nb), 0) // 8 == lax.broadcasted_iota(I32, (nb * 8, nb), 1)).astype(BF16)
    q8 = jnp.zeros((nb * 8, LANES), F32)
    scale = HEAD_DIM ** -0.5
    for j in range(Q_TILES):
        qt = _head_norm_rope(proj[:, j * LANES:(j + 1) * LANES], hmean, qn_ref[...], cos, sin_a, sin_b) * scale
        qrep = jnp.dot(rep, qt.astype(BF16), preferred_element_type=F32)
        q8 = jnp.where(keep & ((r8 % 8) // 2 == j), qrep, q8)
    q8 = q8.astype(BF16)

    sink8 = sink8_ref[:, 0:1]
    ones_bf = jnp.ones((wb, LANES), BF16)
    assert pos >= wb - 1 and wb <= WINDOW
    for b in range(nb):
        kb = nk_ref[b].astype(BF16)
        vb = nv_ref[b].astype(BF16)
        s = lax.dot_general(q8[b * 8:(b + 1) * 8], kb, (((1,), (1,)), ((), ())), preferred_element_type=F32)
        o8[b * 8:(b + 1) * 8, :] = _softmax_pv(s, sink8, vb, ones_bf)
    o8m = jnp.where(keep, o8[...], 0.0).astype(BF16)

    a_tiles = []
    sel_r = lax.broadcasted_iota(I32, (nb, nb * 8), 1)
    sel_b = lax.broadcasted_iota(I32, (nb, nb * 8), 0)
    for j in range(Q_TILES):
        sel = ((sel_r // 8 == sel_b) & ((sel_r % 8) // 2 == j)).astype(BF16)
        a_tiles.append(jnp.dot(sel, o8m, preferred_element_type=F32))

    z_tiles = []
    for gi, w in enumerate(POOL_WINDOWS):
        cols = slice(gi * POOL_GC, (gi + 1) * POOL_GC)
        wsum = u[:, cols]
        for sft in range(1, w):
            wsum = wsum + st_ref[POOL_PREFIX - sft, :, cols]
        d = wsum / float(min(pos + 1, w)) - u[:, cols]
        z_tiles.append(jnp.dot(d.astype(BF16), wpool_ref[gi].astype(BF16), preferred_element_type=F32)
                       * pscale_ref[:, cols])
    nst_ref[0:POOL_PREFIX - 1] = st_ref[1:POOL_PREFIX]
    nst_ref[POOL_PREFIX - 1] = u

    mixv = jnp.concatenate(a_tiles + z_tiles, axis=1).astype(BF16)
    x1_ref[...] = x + jnp.dot(mixv, wout_c[...], preferred_element_type=F32)


def _mixer_sample(x, ck, cv, st, pos, tabs, sink8, g_attn, w_in, qn, kn, hmean, w_pool, pscale, w_out):
    n, wb = ck.shape[0], ck.shape[1]
    nb = SAMPLE_CHUNK
    assert n % nb == 0
    const = lambda shape: pl.BlockSpec(shape, lambda i: (0,) * len(shape))
    cache_blk = pl.BlockSpec((nb, wb, KV_WIDTH), lambda i: (i, 0, 0))
    st_blk = pl.BlockSpec((POOL_PREFIX, nb, POOL_WIDTH), lambda i: (0, i, 0))
    x_blk = pl.BlockSpec((nb, D_MODEL), lambda i: (i, 0))
    return pl.pallas_call(
        functools.partial(_mixer_sample_kernel, pos=pos),
        grid=(n // nb,),
        in_specs=[x_blk, cache_blk, cache_blk, st_blk, const((1, LANES)), const((1, LANES)), const((1, LANES)),
                  const((8, LANES)), const((1, D_MODEL)), const((D_MODEL, IN_WIDTH)), const((1, LANES)),
                  const((1, LANES)), const((LANES, LANES)), const((len(POOL_WINDOWS), POOL_GC, POOL_GC)),
                  const((1, POOL_WIDTH)), const((D_MODEL, D_MODEL))],
        out_specs=[x_blk, cache_blk, cache_blk, st_blk],
        out_shape=[jax.ShapeDtypeStruct((n, D_MODEL), F32), jax.ShapeDtypeStruct(ck.shape, F32),
                   jax.ShapeDtypeStruct(cv.shape, F32), jax.ShapeDtypeStruct(st.shape, F32)],
        scratch_shapes=[pltpu.VMEM((nb * 8, LANES), F32), pltpu.VMEM((D_MODEL, IN_WIDTH), BF16),
                        pltpu.VMEM((D_MODEL, D_MODEL), BF16)],
        compiler_params=pltpu.CompilerParams(dimension_semantics=("arbitrary",), vmem_limit_bytes=VMEM_LIMIT),
        name="mixer_sample",
    )(x, ck, cv, st, *tabs, sink8, g_attn, w_in, qn, kn, hmean, w_pool, pscale, w_out)


def _block_plan(cnt, bm, n_lanes):
    e_sub = lax.broadcasted_iota(I32, (N_EXPERTS, LANES), 0)
    e_lane = lax.broadcasted_iota(I32, (N_EXPERTS, LANES), 1)
    padded = jnp.floor((cnt + (bm - 1)) / bm) * bm
    padded_lane = jnp.sum(jnp.where(e_sub == e_lane, padded, 0.0), axis=0, keepdims=True)
    pad_end = jnp.sum(jnp.where(e_lane <= e_sub, padded_lane, 0.0), axis=1, keepdims=True)
    pad_start = pad_end - padded
    blk_start = lax.broadcasted_iota(I32, (N_EXPERTS, n_lanes), 1).astype(F32) * bm
    blk_e = jnp.minimum(jnp.sum((pad_end <= blk_start).astype(F32), axis=0, keepdims=True), N_EXPERTS - 1.0)
    mine = lax.broadcasted_iota(I32, (N_EXPERTS, n_lanes), 0).astype(F32) == blk_e
    last = jnp.sum(jnp.where(mine, pad_start + cnt, 0.0), axis=0, keepdims=True)
    blk_valid = jnp.clip(last - blk_start[0:1], 0.0, float(bm))
    e_blk = lax.broadcasted_iota(I32, (N_EXPERTS, n_lanes), 0).astype(F32)
    later = jnp.min(jnp.where((e_blk > blk_e) & (cnt > 0.0), e_blk, float(N_EXPERTS)), axis=0, keepdims=True)
    blk_next = jnp.where(later < N_EXPERTS, later, -1.0)
    return pad_start, blk_e, blk_valid, blk_next


def _route_kernel(x1_ref, g_ref, wr_ref, br_ref, cnt_in_ref, hf_ref, idx_ref, gate_ref, pos_ref, cnt_ref,
                  pstart_ref, blke_ref, blkv_ref, blkn_ref, counts, *, bm):
    i = pl.program_id(0)
    rows = x1_ref.shape[0]

    @pl.when(i == 0)
    def _():
        counts[...] = cnt_in_ref[...]

    h = _rms(x1_ref[...], g_ref[...])
    hf_ref[...] = _pack_halves(h)

    logits = lax.dot_general(wr_ref[...], h, (((1,), (1,)), ((), ())), preferred_element_type=F32,
                             precision=lax.Precision.HIGHEST) + br_ref[...]
    eid = lax.broadcasted_iota(I32, (N_EXPERTS, rows), 0).astype(F32)
    work = logits
    vals, hots = [], []
    for kk in range(TOP_K):
        m = jnp.max(work, axis=0, keepdims=True)
        first = jnp.min(jnp.where(work == m, eid, float(N_EXPERTS)), axis=0, keepdims=True)
        hot = eid == first
        work = jnp.where(hot, -jnp.inf, work)
        vals.append(m)
        hots.append(hot)
        idx_ref[kk:kk + 1, :] = first.astype(I32)
    es = [jnp.exp(vv - vals[0]) for vv in vals]
    den = es[0] + es[1] + es[2] + es[3]
    for kk in range(TOP_K):
        gate_ref[kk:kk + 1, :] = es[kk] / den

    chosen = hots[0] | hots[1] | hots[2] | hots[3]
    before = (lax.broadcasted_iota(I32, (rows, rows), 0) < lax.broadcasted_iota(I32, (rows, rows), 1)).astype(BF16)
    rank = jnp.dot(chosen.astype(BF16), before, preferred_element_type=F32) + counts[...]
    for kk in range(TOP_K):
        pos_ref[kk:kk + 1, :] = jnp.sum(jnp.where(hots[kk], rank, 0.0), axis=0, keepdims=True).astype(I32)
    counts[...] = counts[...] + jnp.sum(chosen.astype(F32), axis=1, keepdims=True)
    cnt_ref[...] = counts[...]

    @pl.when(i == pl.num_programs(0) - 1)
    def _():
        pad_start, blk_e, blk_valid, blk_next = _block_plan(counts[...], bm, blke_ref.shape[1])
        pstart_ref[...] = pad_start.astype(I32)
        blke_ref[...] = blk_e.astype(I32)
        blkv_ref[...] = blk_valid.astype(I32)
        blkn_ref[...] = blk_next.astype(I32)


def _route(x1, g_ffn, wr_t, br, cnt_in, rows, bm, n_blocks):
    n = x1.shape[0]
    assert n % rows == 0
    nb_lanes = -(-n_blocks // LANES) * LANES
    const = lambda shape: pl.BlockSpec(shape, lambda i: (0,) * len(shape))
    tok_blk = pl.BlockSpec((TOP_K, rows), lambda i: (0, i))
    return pl.pallas_call(
        functools.partial(_route_kernel, bm=bm),
        grid=(n // rows,),
        in_specs=[pl.BlockSpec((rows, D_MODEL), lambda i: (i, 0)), const((1, D_MODEL)), const((N_EXPERTS, D_MODEL)),
                  const((N_EXPERTS, 1)), const((N_EXPERTS, 1))],
        out_specs=[pl.BlockSpec((rows, D_MODEL // 2), lambda i: (i, 0)), tok_blk, tok_blk, tok_blk,
                   const((N_EXPERTS, 1)), const((N_EXPERTS, 1)), const((1, nb_lanes)), const((1, nb_lanes)),
                   const((1, nb_lanes))],
        out_shape=[jax.ShapeDtypeStruct((n, D_MODEL // 2), U32), jax.ShapeDtypeStruct((TOP_K, n), I32),
                   jax.ShapeDtypeStruct((TOP_K, n), F32), jax.ShapeDtypeStruct((TOP_K, n), I32),
                   jax.ShapeDtypeStruct((N_EXPERTS, 1), F32), jax.ShapeDtypeStruct((N_EXPERTS, 1), I32),
                   jax.ShapeDtypeStruct((1, nb_lanes), I32), jax.ShapeDtypeStruct((1, nb_lanes), I32),
                   jax.ShapeDtypeStruct((1, nb_lanes), I32)],
        scratch_shapes=[pltpu.VMEM((N_EXPERTS, 1), F32)],
        compiler_params=pltpu.CompilerParams(dimension_semantics=("arbitrary",), vmem_limit_bytes=VMEM_LIMIT),
        name="route",
    )(x1, g_ffn, wr_t, br, cnt_in)


def _sc_mesh():
    return plsc.VectorSubcoreMesh(core_axis_name="core", subcore_axis_name="subcore")


def _sc_worker_id():
    return lax.axis_index("core") * SC_SUBCORES + lax.axis_index("subcore")


def _scatter_rows(xa, xb, dest, n_rows):
    ch = SC_CHUNK
    na, w = xa.shape
    n = na + xb.shape[0]
    nk = dest.shape[0]
    assert na % ch == 0 and n % ch == 0 and dest.shape[1] == n and xb.shape[1] == w and xa.dtype == xb.dtype
    n_chunks = n // ch
    dest_c = dest.reshape(nk, n_chunks, ch).transpose(1, 0, 2).reshape(-1)
    dma = pltpu.SemaphoreType.DMA

    @pl.kernel(out_type=jax.ShapeDtypeStruct((n_rows, w), xa.dtype), mesh=_sc_mesh(),
               scratch_types=[pltpu.VMEM((nk * ch,), I32), pltpu.VMEM((ch, w), xa.dtype), dma] + [dma] * nk)
    def scatter_kernel(xa_hbm, xb_hbm, d_hbm, o_hbm, idx_v, buf, sem_i, *sem_s):
        wid = _sc_worker_id()

        @pl.loop(0, -(-n_chunks // SC_WORKERS))
        def _(j):
            c = j * SC_WORKERS + wid

            @pl.when(c < n_chunks)
            def _():
                load_idx = pltpu.make_async_copy(d_hbm.at[pl.ds(c * (nk * ch), nk * ch)], idx_v, sem_i)
                load_idx.start()

                @pl.when(c < na // ch)
                def _():
                    pltpu.sync_copy(xa_hbm.at[pl.ds(c * ch, ch)], buf)

                @pl.when(c >= na // ch)
                def _():
                    pltpu.sync_copy(xb_hbm.at[pl.ds(c * ch - na, ch)], buf)

                load_idx.wait()
                puts = [pltpu.make_async_copy(buf, o_hbm.at[idx_v.at[pl.ds(kk * ch, ch)]], sem_s[kk])
                        for kk in range(nk)]
                for cp in puts:
                    cp.start()
                for cp in puts:
                    cp.wait()

    return scatter_kernel(xa, xb, dest_c)


def _gather_rows(src, idx):
    ch = SC_CHUNK
    m = idx.shape[0]
    w = src.shape[1]
    per = m // SC_WORKERS
    n_pairs = per // (2 * ch)
    assert m % SC_WORKERS == 0 and per % (2 * ch) == 0
    dma = pltpu.SemaphoreType.DMA

    @pl.kernel(out_type=jax.ShapeDtypeStruct((m, w), src.dtype), mesh=_sc_mesh(),
               scratch_types=[pltpu.VMEM((per,), I32), pltpu.VMEM((ch, w), src.dtype), pltpu.VMEM((ch, w), src.dtype),
                              dma, dma, dma, dma])
    def gather_kernel(s_hbm, i_hbm, o_hbm, idx_v, buf_a, buf_b, sem_ga, sem_gb, sem_wa, sem_wb):
        base = _sc_worker_id() * per
        pltpu.sync_copy(i_hbm.at[pl.ds(base, per)], idx_v)

        def fetch(j, buf, sem):
            return pltpu.make_async_copy(s_hbm.at[idx_v.at[pl.ds(j * ch, ch)]], buf, sem)

        def put(j, buf, sem):
            return pltpu.make_async_copy(buf, o_hbm.at[pl.ds(base + j * ch, ch)], sem)

        fetch(0, buf_a, sem_ga).start()

        @pl.loop(0, n_pairs)
        def _(p):
            j0 = 2 * p
            j1 = j0 + 1

            @pl.when(p > 0)
            def _():
                put(j1 - 2, buf_b, sem_wb).wait()

            fetch(j1, buf_b, sem_gb).start()
            fetch(j0, buf_a, sem_ga).wait()
            put(j0, buf_a, sem_wa).start()
            fetch(j1, buf_b, sem_gb).wait()
            put(j1, buf_b, sem_wb).start()
            put(j0, buf_a, sem_wa).wait()

            @pl.when(p + 1 < n_pairs)
            def _():
                fetch(j0 + 2, buf_a, sem_ga).start()

        put(2 * n_pairs - 1, buf_b, sem_wb).wait()

    return gather_kernel(src, idx)


def _moe_kernel(blk_e_ref, blk_valid_ref, blk_next_ref, xs_ref, wgu_hbm, bgu_ref, wd_hbm, bd_ref, y_ref,
                wgu_f32, wd_f32, wgu_bf, wd_bf, sems, *, e0):
    i = pl.program_id(0)
    e = blk_e_ref[i]
    n_valid = blk_valid_ref[i]
    used = n_valid > 0
    new_expert = (i == 0) | (blk_e_ref[jnp.maximum(i - 1, 0)] != e)

    def weight_copies(expert):
        return (pltpu.make_async_copy(wgu_hbm.at[e0 + expert], wgu_f32, sems.at[0]),
                pltpu.make_async_copy(wd_hbm.at[e0 + expert], wd_f32, sems.at[1]))

    @pl.when(used & (i == 0))
    def _():
        for cp in weight_copies(e):
            cp.start()

    @pl.when(used & new_expert)
    def _():
        for cp in weight_copies(e):
            cp.wait()
        wgu_bf[...] = wgu_f32[...].astype(BF16)
        wd_bf[...] = wd_f32[...].astype(BF16)
        nxt = blk_next_ref[i]

        @pl.when(nxt >= 0)
        def _():
            for cp in weight_copies(nxt):
                cp.start(priority=1)

    @pl.when(used)
    def _():
        valid = lax.broadcasted_iota(I32, xs_ref.shape, 0) < n_valid
        xb = _unpack_halves(jnp.where(valid, xs_ref[...], jnp.uint32(0))).astype(BF16)
        gu = jnp.dot(xb, wgu_bf[...], preferred_element_type=F32) + bgu_ref[0]
        g = jnp.minimum(gu[:, :D_FF], SWIGLU_LIMIT)
        up = jnp.clip(gu[:, D_FF:], -SWIGLU_LIMIT, SWIGLU_LIMIT)
        act = (up + 1.0) * (g * jax.nn.sigmoid(SWIGLU_ALPHA * g))
        y_ref[...] = _pack_halves(jnp.dot(act.astype(BF16), wd_bf[...], preferred_element_type=F32) + bd_ref[0])

    @pl.when(jnp.logical_not(used))
    def _():
        y_ref[...] = jnp.zeros_like(y_ref)


def _moe(xs, blk_e, blk_valid, blk_next, layer, w_gu, b_gu, w_d, b_d):
    n_rows = xs.shape[0]
    bm = MOE_BM
    assert n_rows % bm == 0
    e0 = layer * N_EXPERTS
    n_we = w_gu.shape[0] * w_gu.shape[1]
    any_spec = pl.BlockSpec(memory_space=pl.ANY)
    grid_spec = pltpu.PrefetchScalarGridSpec(
        num_scalar_prefetch=3,
        grid=(n_rows // bm,),
        in_specs=[pl.BlockSpec((bm, D_MODEL // 2), lambda i, be, bv, bn: (i, 0)),
                  any_spec,
                  pl.BlockSpec((1, 1, 2 * D_FF), lambda i, be, bv, bn: (e0 + be[i], 0, 0)),
                  any_spec,
                  pl.BlockSpec((1, 1, D_MODEL), lambda i, be, bv, bn: (e0 + be[i], 0, 0))],
        out_specs=pl.BlockSpec((bm, D_MODEL // 2), lambda i, be, bv, bn: (i, 0)),
        scratch_shapes=[pltpu.VMEM((D_MODEL, 2 * D_FF), F32), pltpu.VMEM((D_FF, D_MODEL), F32),
                        pltpu.VMEM((D_MODEL, 2 * D_FF), BF16), pltpu.VMEM((D_FF, D_MODEL), BF16),
                        pltpu.SemaphoreType.DMA((2,))],
    )
    return pl.pallas_call(
        functools.partial(_moe_kernel, e0=e0),
        grid_spec=grid_spec,
        out_shape=jax.ShapeDtypeStruct((n_rows, D_MODEL // 2), U32),
        compiler_params=pltpu.CompilerParams(dimension_semantics=("arbitrary",), vmem_limit_bytes=VMEM_LIMIT),
        name="moe_experts",
    )(blk_e, blk_valid, blk_next, xs, w_gu.reshape(n_we, D_MODEL, 2 * D_FF), b_gu.reshape(n_we, 1, 2 * D_FF),
      w_d.reshape(n_we, D_FF, D_MODEL), b_d.reshape(n_we, 1, D_MODEL))


def _ple_kernel(x1_ref, y0_ref, y1_ref, y2_ref, y3_ref, gates_ref, p_ref, g_ref, wg_ref, wp_ref, *rest):
    out_ref, wg_bf, wp_bf = rest[-3:]

    @pl.when(pl.program_id(0) == 0)
    def _():
        wg_bf[...] = wg_ref[...].astype(BF16)
        wp_bf[...] = wp_ref[...].astype(BF16)

    x2 = x1_ref[...]
    gates = gates_ref[...]
    for kk, y_ref in enumerate((y0_ref, y1_ref, y2_ref, y3_ref)):
        x2 = x2 + _unpack_halves(y_ref[...]) * gates[:, kk:kk + 1]
    hp = _rms(x2, g_ref[...]).astype(BF16)
    gate = jax.nn.sigmoid(jnp.dot(hp, wg_bf[...], preferred_element_type=F32))
    pp = jnp.dot(p_ref[...].astype(BF16), wp_bf[...], preferred_element_type=F32)
    out_ref[...] = x2 + gate * pp


def _ple(x1, tok0, n, y_tok, y0, gates_t, p_all, p0, g_ple, w_gate, w_proj, rows, out_into=None):
    assert n % rows == 0 and tok0 % rows == 0 and y0 % rows == 0 and p0 % rows == 0
    const = lambda shape: pl.BlockSpec(shape, lambda i: (0,) * len(shape))
    tok_blk = lambda width: pl.BlockSpec((rows, width), lambda i: (tok0 // rows + i, 0))
    y_blk = lambda kk: pl.BlockSpec((rows, D_MODEL // 2), lambda i: ((y0 + kk * n) // rows + i, 0))
    extra_specs, extra_args, aliases = [], [], {}
    if out_into is not None:
        assert out_into.shape == x1.shape
        extra_specs, extra_args, aliases = [pl.BlockSpec(memory_space=pl.ANY)], [out_into], {10: 0}
    return pl.pallas_call(
        _ple_kernel,
        grid=(n // rows,),
        in_specs=[tok_blk(D_MODEL), y_blk(0), y_blk(1), y_blk(2), y_blk(3), tok_blk(TOP_K),
                  pl.BlockSpec((rows, PLE_DIM), lambda i: (p0 // rows + i, 0)),
                  const((1, D_MODEL)), const((D_MODEL, D_MODEL)), const((PLE_DIM, D_MODEL))] + extra_specs,
        out_specs=tok_blk(D_MODEL),
        out_shape=jax.ShapeDtypeStruct(x1.shape, F32),
        scratch_shapes=[pltpu.VMEM((D_MODEL, D_MODEL), BF16), pltpu.VMEM((PLE_DIM, D_MODEL), BF16)],
        input_output_aliases=aliases,
        compiler_params=pltpu.CompilerParams(dimension_semantics=("arbitrary",), vmem_limit_bytes=VMEM_LIMIT),
        name="combine_ple",
    )(x1, y_tok, y_tok, y_tok, y_tok, gates_t, p_all, g_ple, w_gate, w_proj, *extra_args)


def _rope_tables(pos):
    half = ROT_DIM // 2
    d = np.arange(LANES) % HEAD_DIM
    inv = ROPE_THETA ** (-jnp.arange(half, dtype=F32) / half)
    inv_lane = jnp.where(d < ROT_DIM, inv[d % half], 0.0)
    ang = pos.astype(F32)[:, None] * inv_lane[None, :]
    cos, sin = jnp.cos(ang), jnp.sin(ang)
    sin_a = jnp.where((d >= half) & (d < ROT_DIM), sin, 0.0)
    sin_b = jnp.where(d < half, -sin, 0.0)
    return cos, sin_a, sin_b


def _layer(layer, xp, xs, ck, cv, st, p_prompt_all, p_sample_all, past_len, refine_tail, tabs_p, tabs_s,
           norm_attn, w_in, q_norm, k_norm, attn_sinks, w_pool, pool_scale, w_out,
           norm_ffn, w_router, b_router, w_gate_up_all, b_gate_up_all, w_down_all, b_down_all,
           norm_ple, w_ple_gate, w_ple_proj):
    t, ns = xp.shape[0], xs.shape[0]
    g_attn = norm_attn.reshape(1, D_MODEL)
    qn = jnp.tile(q_norm, 2).reshape(1, LANES)
    kn = jnp.tile(k_norm, 2).reshape(1, LANES)
    lane = np.arange(LANES)
    hmean = jnp.asarray((lane[:, None] // HEAD_DIM == lane[None, :] // HEAD_DIM) / HEAD_DIM, F32)
    pscale = pool_scale.reshape(1, POOL_WIDTH)
    sink8 = jnp.broadcast_to(attn_sinks.reshape(2, Q_TILES).T.reshape(8, 1), (8, LANES))

    mix_args = (g_attn, w_in, qn, kn, hmean, w_pool, pscale, w_out)
    x1p, nk_p, nv_p, nu_p = _mixer_prompt(xp, 0, MIX_ROWS, BF16, attn_sinks, tabs_p, *mix_args)
    if refine_tail:
        x1p, nk_p, nv_p, nu_p = _mixer_prompt(xp, t - 2 * MIX_TAIL_ROWS, MIX_TAIL_ROWS, F32, attn_sinks, tabs_p,
                                              *mix_args, x1_into=x1p)
    st_t = jnp.transpose(st, (1, 0, 2))
    x1s, nk_s, nv_s, nst_t = _mixer_sample(xs, ck.reshape(ns, -1, KV_WIDTH), cv.reshape(ns, -1, KV_WIDTH), st_t,
                                           past_len, tabs_s, sink8, *mix_args)

    g_ffn = norm_ffn.reshape(1, D_MODEL)
    wr_t = w_router.T
    br = b_router.reshape(N_EXPERTS, 1)
    n_tok = t + ns
    bm = MOE_BM
    n_blocks = -(-(n_tok * TOP_K + N_EXPERTS * (bm - 1)) // bm)
    hf_p, idx_p, gate_p, pos_p, cnt_p = _route(x1p, g_ffn, wr_t, br, jnp.zeros((N_EXPERTS, 1), F32),
                                               ROUTE_ROWS, bm, n_blocks)[:5]
    hf_s, idx_s, gate_s, pos_s, _, pad_start, blk_e, blk_valid, blk_next = _route(x1s, g_ffn, wr_t, br, cnt_p, ns,
                                                                                  bm, n_blocks)
    idx = jnp.concatenate([idx_p, idx_s], axis=1)
    pos = jnp.concatenate([pos_p, pos_s], axis=1)
    blk_e, blk_valid, blk_next = (a[0, :n_blocks] for a in (blk_e, blk_valid, blk_next))
    start_of = jnp.sum(jnp.where(idx[None] == jnp.arange(N_EXPERTS, dtype=I32)[:, None, None],
                                 pad_start.reshape(N_EXPERTS, 1, 1), 0), axis=0)
    dest = start_of + pos

    xs_rows = _scatter_rows(hf_p, hf_s, dest, n_blocks * bm)
    y = _moe(xs_rows, blk_e, blk_valid, blk_next, layer, w_gate_up_all, b_gate_up_all, w_down_all, b_down_all)

    th = t // 2
    unit = SC_WORKERS * SC_CHUNK * 2
    assert (TOP_K * th) % unit == 0
    back_b = jnp.concatenate([dest[:, th:t].reshape(-1), dest[:, t:].reshape(-1)])
    back_b = jnp.concatenate([back_b, jnp.arange(-back_b.shape[0] % unit, dtype=I32)])
    y_a = _gather_rows(y, dest[:, :th].reshape(-1))
    y_b = _gather_rows(y, back_b)

    g_ple = norm_ple.reshape(1, D_MODEL)
    wg_bf, wp_bf = w_ple_gate, w_ple_proj
    gates_p = gate_p.T
    yp = _ple(x1p, 0, th, y_a, 0, gates_p, p_prompt_all, layer * t, g_ple, wg_bf, wp_bf, PLE_ROWS)
    yp = _ple(x1p, th, t - th, y_b, 0, gates_p, p_prompt_all, layer * t + th, g_ple, wg_bf, wp_bf, PLE_ROWS,
              out_into=yp)
    ys = _ple(x1s, 0, ns, y_b, TOP_K * (t - th), gate_s.T, p_sample_all, layer * ns, g_ple, wg_bf, wp_bf, ns)
    new_pool_s = jnp.transpose(nst_t, (1, 0, 2))
    return yp, ys, nk_p, nv_p, nu_p[POOL_HALO - POOL_PREFIX:], nk_s, nv_s, new_pool_s


def kernel(x_prompt, x_sample, cache_k, cache_v, state_pool, p_prompt, p_sample, norm_attn, w_in, q_norm, k_norm,
           attn_sinks, w_pool, pool_scale, w_out, norm_ffn, w_router, b_router, w_gate_up, b_gate_up, w_down, b_down,
           norm_ple, w_ple_gate, w_ple_proj):
    depth = norm_attn.shape[0]
    batch, seq, d = x_prompt.shape
    ns, dec_seq, _ = x_sample.shape
    wb = cache_k.shape[2]
    assert batch == 1 and dec_seq == 1 and d == D_MODEL and wb == WINDOW
    assert cache_k.shape[3:] == (N_KV_HEADS, HEAD_DIM) and state_pool.shape[2:] == (POOL_PREFIX, POOL_WIDTH)
    past_len = PAST_LEN
    yp = x_prompt.reshape(seq, d)
    ys = x_sample.reshape(ns, d)
    p_prompt_all = p_prompt.reshape(depth * seq, PLE_DIM)
    p_sample_all = p_sample.reshape(depth * ns, PLE_DIM)
    tabs_p = _rope_tables(jnp.arange(seq))
    tabs_s = _rope_tables(jnp.full((1,), past_len))
    outs = [[] for _ in range(6)]
    for i in range(depth):
        res = _layer(i, yp, ys, cache_k[i], cache_v[i], state_pool[i], p_prompt_all, p_sample_all, past_len,
                     i < depth - 1, tabs_p, tabs_s,
                     norm_attn[i], w_in[i], q_norm[i], k_norm[i], attn_sinks[i], w_pool[i], pool_scale[i], w_out[i],
                     norm_ffn[i], w_router[i], b_router[i], w_gate_up, b_gate_up, w_down, b_down,
                     norm_ple[i], w_ple_gate[i], w_ple_proj[i])
        yp, ys = res[0], res[1]
        kv_shape = (1, WINDOW, N_KV_HEADS, HEAD_DIM)
        outs[0].append(res[2].reshape(kv_shape))
        outs[1].append(res[3].reshape(kv_shape))
        outs[2].append(res[4].reshape(1, POOL_PREFIX, POOL_WIDTH))
        outs[3].append(res[5].reshape(ns, wb, N_KV_HEADS, HEAD_DIM))
        outs[4].append(res[6].reshape(ns, wb, N_KV_HEADS, HEAD_DIM))
        outs[5].append(res[7])
    return (yp.reshape(batch, seq, d), ys.reshape(ns, dec_seq, d)) + tuple(jnp.stack(o) for o in outs)
```

```python
import functools

import jax
import jax.numpy as jnp
import numpy as np
from jax import lax
from jax.experimental import pallas as pl
from jax.experimental.pallas import tpu as pltpu
from jax.experimental.pallas import tpu_sc as plsc

F32 = jnp.float32
BF16 = jnp.bfloat16
U32 = jnp.uint32
I32 = jnp.int32

D_MODEL = 1024
HEAD_DIM = 64
N_HEADS = 8
N_KV_HEADS = 2
GROUP = N_HEADS // N_KV_HEADS
ATTN_WIDTH = N_HEADS * HEAD_DIM
KV_WIDTH = N_KV_HEADS * HEAD_DIM
POOL_WIDTH = 512
POOL_WINDOWS = (2, 4, 8, 16)
POOL_GC = POOL_WIDTH // len(POOL_WINDOWS)
POOL_PREFIX = max(POOL_WINDOWS) - 1
POOL_HALO = POOL_PREFIX + 1
POOL_PAD = 8
IN_WIDTH = ATTN_WIDTH + 2 * KV_WIDTH + POOL_WIDTH
WINDOW = 128
ROPE_THETA = 500000.0
ROT_DIM = HEAD_DIM // 4
N_EXPERTS = 32
TOP_K = 4
D_FF = 1024
SWIGLU_ALPHA = 1.702
SWIGLU_LIMIT = 7.0
PLE_DIM = 256
PAST_LEN = 16384
EPS = 1e-5
NEG_INF = -1e30

LANES = 128
Q_TILES = ATTN_WIDTH // LANES

MIX_ROWS = 512
MIX_TAIL_ROWS = 128
SAMPLE_CHUNK = 16
ROUTE_ROWS = 512
MOE_BM = 512
SC_CORES = 2
SC_SUBCORES = 16
SC_WORKERS = SC_CORES * SC_SUBCORES
SC_CHUNK = 64
PLE_ROWS = 512
VMEM_LIMIT = 56 * 1024 * 1024


def _rms(x, g):
    return x * lax.rsqrt(jnp.mean(x * x, axis=-1, keepdims=True) + EPS) * g


def _pack_halves(x):
    w = x.shape[1] // 2
    lo = lax.bitcast_convert_type(x[:, :w].astype(BF16).astype(F32), U32) >> 16
    hi = lax.bitcast_convert_type(x[:, w:].astype(BF16).astype(F32), U32) & jnp.uint32(0xFFFF0000)
    return lo | hi


def _unpack_halves(packed):
    lo = lax.bitcast_convert_type(packed << 16, F32)
    hi = lax.bitcast_convert_type(packed & jnp.uint32(0xFFFF0000), F32)
    return jnp.concatenate([lo, hi], axis=1)


def _mm(a, b, nt=False):
    dims = (((1,), (1 if nt else 0,)), ((), ()))
    if b.dtype == F32:
        return lax.dot_general(a.astype(F32), b, dims, preferred_element_type=F32, precision=lax.Precision.HIGHEST)
    return lax.dot_general(a.astype(BF16), b, dims, preferred_element_type=F32)


def _head_norm_rope(t, hmean, gain, cos, sin_a, sin_b):
    t = t * lax.rsqrt(_mm(t * t, hmean) + EPS) * gain
    return t * cos + pltpu.roll(t, ROT_DIM // 2, axis=1) * sin_a + pltpu.roll(t, LANES - ROT_DIM // 2, axis=1) * sin_b


def _softmax_pv(s, sink, v, ones):
    m = jnp.maximum(jnp.max(s, axis=-1, keepdims=True), sink)
    e = jnp.exp(s - m).astype(v.dtype)
    den = _mm(e, ones) + jnp.exp(sink - m)
    return _mm(e, v) / den


def _prep_mixer_weights(win_ref, wout_ref, win_c, wout_c):
    cdt = win_c.dtype
    for j in range(Q_TILES):
        for s in range(2):
            src = (j + Q_TILES * s) * HEAD_DIM
            dst = j * LANES + s * HEAD_DIM
            win_c[:, dst:dst + HEAD_DIM] = win_ref[:, src:src + HEAD_DIM].astype(cdt)
            wout_c[dst:dst + HEAD_DIM, :] = wout_ref[src:src + HEAD_DIM, :].astype(cdt)
    win_c[:, ATTN_WIDTH:] = win_ref[:, ATTN_WIDTH:].astype(cdt)
    wout_c[ATTN_WIDTH:, :] = wout_ref[ATTN_WIDTH:, :].astype(cdt)


def _mixer_prompt_kernel(sinks_ref, x_ref, cos_ref, sa_ref, sb_ref, g_ref, win_ref, qn_ref, kn_ref, hm_ref,
                         wpool_ref, pscale_ref, wout_ref, *rest, row_offset, aliased):
    (x1_ref, klast_ref, vlast_ref, ulast_ref, kprev, vprev, uext, mix, win_c, wout_c,
     *lvl) = rest[1:] if aliased else rest
    i = pl.program_id(0)
    rows = x_ref.shape[0]
    n_sub = rows // WINDOW
    cdt = win_c.dtype
    row0 = row_offset + i * rows

    @pl.when(i == 0)
    def _():
        _prep_mixer_weights(win_ref, wout_ref, win_c, wout_c)
        kprev[...] = jnp.zeros_like(kprev)
        vprev[...] = jnp.zeros_like(vprev)
        uext[0:POOL_PAD + POOL_HALO, :] = jnp.zeros((POOL_PAD + POOL_HALO, POOL_WIDTH), F32)
        for buf in lvl:
            buf[0:POOL_PAD, :] = jnp.zeros((POOL_PAD, POOL_GC), F32)

    x = x_ref[...]
    proj = _mm(_rms(x, g_ref[...]), win_c[...])
    cos, sin_a, sin_b = cos_ref[...], sa_ref[...], sb_ref[...]

    n_t = Q_TILES + 1
    t_all = jnp.concatenate([proj[:, j * LANES:(j + 1) * LANES] for j in range(n_t)], axis=0)
    t3 = (t_all * lax.rsqrt(_mm(t_all * t_all, hm_ref[...].astype(cdt)) + EPS)).reshape(n_t, rows, LANES)
    t3 = jnp.concatenate([t3[:Q_TILES] * (qn_ref[...] * HEAD_DIM ** -0.5), t3[Q_TILES:] * kn_ref[...]], axis=0)
    t2 = t3.reshape(n_t * rows, LANES)
    t3 = (t3 * cos + pltpu.roll(t2, ROT_DIM // 2, axis=1).reshape(n_t, rows, LANES) * sin_a
          + pltpu.roll(t2, LANES - ROT_DIM // 2, axis=1).reshape(n_t, rows, LANES) * sin_b)
    q3 = t3[:Q_TILES]
    k = t3[Q_TILES]
    v = proj[:, ATTN_WIDTH + KV_WIDTH:ATTN_WIDTH + 2 * KV_WIDTH]
    u = proj[:, ATTN_WIDTH + 2 * KV_WIDTH:]
    klast_ref[...] = k[rows - WINDOW:, :]
    vlast_ref[...] = v[rows - WINDOW:, :]
    ulast_ref[...] = u[rows - POOL_HALO:, :]
    k_c = k.astype(cdt)
    v_c = jnp.concatenate([v.astype(cdt), jnp.ones((rows, LANES), cdt)], axis=1)
    v_first = jnp.concatenate([vprev[...], jnp.ones((WINDOW, LANES), cdt)], axis=1)

    lane = lax.broadcasted_iota(I32, (WINDOW, LANES), 1)
    left = (lane < HEAD_DIM)[None]
    qi = lax.broadcasted_iota(I32, (WINDOW, 2 * WINDOW), 0)
    kj = lax.broadcasted_iota(I32, (WINDOW, 2 * WINDOW), 1)
    band = (kj - qi >= 1) & (kj - qi <= WINDOW)
    sink3 = jnp.concatenate([jnp.full((1, 1, 1), sinks_ref[j + Q_TILES * s], F32)
                             for j in range(Q_TILES) for s in range(2)], axis=0)
    n_g = 2 * Q_TILES

    v_cats, masks, scores = [], [], []
    for c in range(n_sub):
        r0 = c * WINDOW
        if c == 0:
            k_cat = jnp.concatenate([kprev[...], k_c[0:WINDOW]], axis=0)
            v_cats.append(jnp.concatenate([v_first, v_c[0:WINDOW]], axis=0))
            masks.append(band & (kj + (row0 - WINDOW) >= 0))
        else:
            k_cat = k_c[r0 - WINDOW:r0 + WINDOW]
            v_cats.append(v_c[r0 - WINDOW:r0 + WINDOW])
            masks.append(band)
        q_c = q3[:, r0:r0 + WINDOW, :]
        q_all = jnp.concatenate([jnp.where(left, q_c, 0.0), jnp.where(left, 0.0, q_c)], axis=1)
        scores.append(_mm(q_all.reshape(n_g * WINDOW, LANES), k_cat, nt=True))
    kprev[...] = k_c[rows - WINDOW:]
    vprev[...] = v[rows - WINDOW:].astype(cdt)

    probs, maxes = [], []
    for c in range(n_sub):
        s = jnp.where(masks[c][None], scores[c].reshape(n_g, WINDOW, 2 * WINDOW), NEG_INF)
        m = jnp.maximum(jnp.max(s, axis=-1, keepdims=True), sink3)
        probs.append(jnp.exp(s - m).astype(cdt).reshape(n_g * WINDOW, 2 * WINDOW))
        maxes.append(m)
    applied = [_mm(probs[c], v_cats[c]) for c in range(n_sub)]
    for c in range(n_sub):
        r0 = c * WINDOW
        den = applied[c][:, LANES:] + jnp.exp(sink3 - maxes[c]).reshape(n_g * WINDOW, 1)
        o = (applied[c][:, :LANES] / den).reshape(Q_TILES, 2 * WINDOW, LANES)
        a = jnp.where(left, o[:, :WINDOW], o[:, WINDOW:])
        for j in range(Q_TILES):
            mix[r0:r0 + WINDOW, j * LANES:(j + 1) * LANES] = a[j].astype(cdt)

    base = POOL_PAD + POOL_HALO
    ext = base + rows
    uext[base:ext, :] = u
    pos1 = (lax.broadcasted_iota(I32, (rows, 1), 0) + row0 + 1).astype(F32)
    lvl_of = {1: lvl[0:1], 2: lvl[1:3], 3: lvl[3:5]}
    for gi, w in enumerate(POOL_WINDOWS):
        cols = slice(gi * POOL_GC, (gi + 1) * POOL_GC)
        src, src_cols = uext, cols
        for level in range(1, gi + 2):
            sft = 1 << (level - 1)
            if level <= gi:
                dst = lvl_of[gi][(level - 1) % 2]
                dst[POOL_PAD:ext, :] = src[POOL_PAD:ext, src_cols] + src[POOL_PAD - sft:ext - sft, src_cols]
                src, src_cols = dst, slice(None)
            else:
                wsum = src[base:ext, src_cols] + src[base - sft:ext - sft, src_cols]
        d = wsum / jnp.minimum(pos1, float(w)) - u[:, cols]
        y = _mm(d, wpool_ref[gi].astype(cdt)) * pscale_ref[:, cols]
        mix[:, ATTN_WIDTH + gi * POOL_GC:ATTN_WIDTH + (gi + 1) * POOL_GC] = y.astype(cdt)
    uext[POOL_PAD:base, :] = u[rows - POOL_HALO:, :]

    x1_ref[...] = x + _mm(mix[...], wout_c[...])


def _mixer_prompt(x_full, row_offset, rows, cdt, sinks, tabs, g_attn, w_in, qn, kn, hmean, w_pool, pscale, w_out,
                  x1_into=None):
    t = x_full.shape[0] - row_offset
    assert t % rows == 0 and row_offset % rows == 0 and rows % WINDOW == 0 and rows >= POOL_HALO
    blk0, n_steps = row_offset // rows, t // rows
    const = lambda shape: pl.BlockSpec(shape, lambda i, *_: (0,) * len(shape))
    row_blk = lambda width: pl.BlockSpec((rows, width), lambda i, *_: (blk0 + i, 0))
    aliased = x1_into is not None
    if aliased:
        assert x1_into.shape == x_full.shape
        x1_spec = pl.BlockSpec((rows, D_MODEL), lambda i, *_: (blk0 + n_steps - 1, 0))
        x1_shape = x1_into.shape
        extra_specs, extra_args, aliases = [pl.BlockSpec(memory_space=pl.ANY)], [x1_into], {13: 0}
    else:
        x1_spec = pl.BlockSpec((rows, D_MODEL), lambda i, *_: (i, 0))
        x1_shape = (t, D_MODEL)
        extra_specs, extra_args, aliases = [], [], {}
    grid_spec = pltpu.PrefetchScalarGridSpec(
        num_scalar_prefetch=1,
        grid=(n_steps,),
        in_specs=[row_blk(D_MODEL), row_blk(LANES), row_blk(LANES), row_blk(LANES),
                  const((1, D_MODEL)), const((D_MODEL, IN_WIDTH)), const((1, LANES)), const((1, LANES)),
                  const((LANES, LANES)), const((len(POOL_WINDOWS), POOL_GC, POOL_GC)), const((1, POOL_WIDTH)),
                  const((D_MODEL, D_MODEL))] + extra_specs,
        out_specs=[x1_spec, const((WINDOW, KV_WIDTH)), const((WINDOW, KV_WIDTH)),
                   const((POOL_HALO, POOL_WIDTH))],
        scratch_shapes=[pltpu.VMEM((WINDOW, KV_WIDTH), cdt), pltpu.VMEM((WINDOW, KV_WIDTH), cdt),
                        pltpu.VMEM((POOL_PAD + POOL_HALO + rows, POOL_WIDTH), F32), pltpu.VMEM((rows, D_MODEL), cdt),
                        pltpu.VMEM((D_MODEL, IN_WIDTH), cdt), pltpu.VMEM((D_MODEL, D_MODEL), cdt)]
        + [pltpu.VMEM((POOL_PAD + POOL_HALO + rows, POOL_GC), F32)] * 5,
    )
    return pl.pallas_call(
        functools.partial(_mixer_prompt_kernel, row_offset=row_offset, aliased=aliased),
        grid_spec=grid_spec,
        out_shape=[jax.ShapeDtypeStruct(x1_shape, F32), jax.ShapeDtypeStruct((WINDOW, KV_WIDTH), F32),
                   jax.ShapeDtypeStruct((WINDOW, KV_WIDTH), F32), jax.ShapeDtypeStruct((POOL_HALO, POOL_WIDTH), F32)],
        input_output_aliases=aliases,
        compiler_params=pltpu.CompilerParams(dimension_semantics=("arbitrary",), vmem_limit_bytes=VMEM_LIMIT),
        name="mixer_prompt",
    )(sinks, x_full, *tabs, g_attn, w_in, qn, kn, hmean, w_pool, pscale, w_out, *extra_args)


def _mixer_sample_kernel(x_ref, ck_ref, cv_ref, st_ref, cos_ref, sa_ref, sb_ref, sink8_ref, g_ref, win_ref, qn_ref,
                         kn_ref, hm_ref, wpool_ref, pscale_ref, wout_ref,
                         x1_ref, nk_ref, nv_ref, nst_ref, o8, win_c, wout_c, *, pos):
    nb = x_ref.shape[0]
    wb = ck_ref.shape[1]

    @pl.when(pl.program_id(0) == 0)
    def _():
        _prep_mixer_weights(win_ref, wout_ref, win_c, wout_c)

    x = x_ref[...]
    h = _rms(x, g_ref[...]).astype(BF16)
    proj = jnp.dot(h, win_c[...], preferred_element_type=F32)
    cos, sin_a, sin_b = cos_ref[...], sa_ref[...], sb_ref[...]
    hmean = hm_ref[...].astype(BF16)
    k = _head_norm_rope(proj[:, ATTN_WIDTH:ATTN_WIDTH + KV_WIDTH], hmean, kn_ref[...], cos, sin_a, sin_b)
    v = proj[:, ATTN_WIDTH + KV_WIDTH:ATTN_WIDTH + 2 * KV_WIDTH]
    u = proj[:, ATTN_WIDTH + 2 * KV_WIDTH:]

    nk_ref[:, 0:wb - 1, :] = ck_ref[:, 1:wb, :]
    nv_ref[:, 0:wb - 1, :] = cv_ref[:, 1:wb, :]
    for b in range(nb):
        nk_ref[b, wb - 1:wb, :] = k[b:b + 1, :]
        nv_ref[b, wb - 1:wb, :] = v[b:b + 1, :]

    r8 = lax.broadcasted_iota(I32, (nb * 8, LANES), 0)
    lane8 = lax.broadcasted_iota(I32, (nb * 8, LANES), 1)
    keep = (lane8 < HEAD_DIM) == (r8 % 2 == 0)
    rep = (lax.broadcasted_iota(I32, (nb * 8, nb), 0) // 8 == lax.broadcasted_iota(I32, (nb * 8, nb), 1)).astype(BF16)
    q8 = jnp.zeros((nb * 8, LANES), F32)
    scale = HEAD_DIM ** -0.5
    for j in range(Q_TILES):
        qt = _head_norm_rope(proj[:, j * LANES:(j + 1) * LANES], hmean, qn_ref[...], cos, sin_a, sin_b) * scale
        qrep = jnp.dot(rep, qt.astype(BF16), preferred_element_type=F32)
        q8 = jnp.where(keep & ((r8 % 8) // 2 == j), qrep, q8)
    q8 = q8.astype(BF16)

    sink8 = sink8_ref[:, 0:1]
    ones_bf = jnp.ones((wb, LANES), BF16)
    assert pos >= wb - 1 and wb <= WINDOW
    for b in range(nb):
        kb = nk_ref[b].astype(BF16)
        vb = nv_ref[b].astype(BF16)
        s = lax.dot_general(q8[b * 8:(b + 1) * 8], kb, (((1,), (1,)), ((), ())), preferred_element_type=F32)
        o8[b * 8:(b + 1) * 8, :] = _softmax_pv(s, sink8, vb, ones_bf)
    o8m = jnp.where(keep, o8[...], 0.0).astype(BF16)

    a_tiles = []
    sel_r = lax.broadcasted_iota(I32, (nb, nb * 8), 1)
    sel_b = lax.broadcasted_iota(I32, (nb, nb * 8), 0)
    for j in range(Q_TILES):
        sel = ((sel_r // 8 == sel_b) & ((sel_r % 8) // 2 == j)).astype(BF16)
        a_tiles.append(jnp.dot(sel, o8m, preferred_element_type=F32))

    z_tiles = []
    for gi, w in enumerate(POOL_WINDOWS):
        cols = slice(gi * POOL_GC, (gi + 1) * POOL_GC)
        wsum = u[:, cols]
        for sft in range(1, w):
            wsum = wsum + st_ref[POOL_PREFIX - sft, :, cols]
        d = wsum / float(min(pos + 1, w)) - u[:, cols]
        z_tiles.append(jnp.dot(d.astype(BF16), wpool_ref[gi].astype(BF16), preferred_element_type=F32)
                       * pscale_ref[:, cols])
    nst_ref[0:POOL_PREFIX - 1] = st_ref[1:POOL_PREFIX]
    nst_ref[POOL_PREFIX - 1] = u

    mixv = jnp.concatenate(a_tiles + z_tiles, axis=1).astype(BF16)
    x1_ref[...] = x + jnp.dot(mixv, wout_c[...], preferred_element_type=F32)


def _mixer_sample(x, ck, cv, st, pos, tabs, sink8, g_attn, w_in, qn, kn, hmean, w_pool, pscale, w_out):
    n, wb = ck.shape[0], ck.shape[1]
    nb = SAMPLE_CHUNK
    assert n % nb == 0
    const = lambda shape: pl.BlockSpec(shape, lambda i: (0,) * len(shape))
    cache_blk = pl.BlockSpec((nb, wb, KV_WIDTH), lambda i: (i, 0, 0))
    st_blk = pl.BlockSpec((POOL_PREFIX, nb, POOL_WIDTH), lambda i: (0, i, 0))
    x_blk = pl.BlockSpec((nb, D_MODEL), lambda i: (i, 0))
    return pl.pallas_call(
        functools.partial(_mixer_sample_kernel, pos=pos),
        grid=(n // nb,),
        in_specs=[x_blk, cache_blk, cache_blk, st_blk, const((1, LANES)), const((1, LANES)), const((1, LANES)),
                  const((8, LANES)), const((1, D_MODEL)), const((D_MODEL, IN_WIDTH)), const((1, LANES)),
                  const((1, LANES)), const((LANES, LANES)), const((len(POOL_WINDOWS), POOL_GC, POOL_GC)),
                  const((1, POOL_WIDTH)), const((D_MODEL, D_MODEL))],
        out_specs=[x_blk, cache_blk, cache_blk, st_blk],
        out_shape=[jax.ShapeDtypeStruct((n, D_MODEL), F32), jax.ShapeDtypeStruct(ck.shape, F32),
                   jax.ShapeDtypeStruct(cv.shape, F32), jax.ShapeDtypeStruct(st.shape, F32)],
        scratch_shapes=[pltpu.VMEM((nb * 8, LANES), F32), pltpu.VMEM((D_MODEL, IN_WIDTH), BF16),
                        pltpu.VMEM((D_MODEL, D_MODEL), BF16)],
        compiler_params=pltpu.CompilerParams(dimension_semantics=("arbitrary",), vmem_limit_bytes=VMEM_LIMIT),
        name="mixer_sample",
    )(x, ck, cv, st, *tabs, sink8, g_attn, w_in, qn, kn, hmean, w_pool, pscale, w_out)


def _block_plan(cnt, bm, n_lanes):
    e_sub = lax.broadcasted_iota(I32, (N_EXPERTS, LANES), 0)
    e_lane = lax.broadcasted_iota(I32, (N_EXPERTS, LANES), 1)
    padded = jnp.floor((cnt + (bm - 1)) / bm) * bm
    padded_lane = jnp.sum(jnp.where(e_sub == e_lane, padded, 0.0), axis=0, keepdims=True)
    pad_end = jnp.sum(jnp.where(e_lane <= e_sub, padded_lane, 0.0), axis=1, keepdims=True)
    pad_start = pad_end - padded
    blk_start = lax.broadcasted_iota(I32, (N_EXPERTS, n_lanes), 1).astype(F32) * bm
    blk_e = jnp.minimum(jnp.sum((pad_end <= blk_start).astype(F32), axis=0, keepdims=True), N_EXPERTS - 1.0)
    mine = lax.broadcasted_iota(I32, (N_EXPERTS, n_lanes), 0).astype(F32) == blk_e
    last = jnp.sum(jnp.where(mine, pad_start + cnt, 0.0), axis=0, keepdims=True)
    blk_valid = jnp.clip(last - blk_start[0:1], 0.0, float(bm))
    e_blk = lax.broadcasted_iota(I32, (N_EXPERTS, n_lanes), 0).astype(F32)
    later = jnp.min(jnp.where((e_blk > blk_e) & (cnt > 0.0), e_blk, float(N_EXPERTS)), axis=0, keepdims=True)
    blk_next = jnp.where(later < N_EXPERTS, later, -1.0)
    return pad_start, blk_e, blk_valid, blk_next


def _route_kernel(x1_ref, g_ref, wr_ref, br_ref, cnt_in_ref, hf_ref, idx_ref, gate_ref, pos_ref, cnt_ref,
                  pstart_ref, blke_ref, blkv_ref, blkn_ref, counts, *, bm):
    i = pl.program_id(0)
    rows = x1_ref.shape[0]

    @pl.when(i == 0)
    def _():
        counts[...] = cnt_in_ref[...]

    h = _rms(x1_ref[...], g_ref[...])
    hf_ref[...] = _pack_halves(h)

    logits = lax.dot_general(wr_ref[...], h, (((1,), (1,)), ((), ())), preferred_element_type=F32,
                             precision=lax.Precision.HIGHEST) + br_ref[...]
    eid = lax.broadcasted_iota(I32, (N_EXPERTS, rows), 0).astype(F32)
    work = logits
    vals, hots = [], []
    for kk in range(TOP_K):
        m = jnp.max(work, axis=0, keepdims=True)
        first = jnp.min(jnp.where(work == m, eid, float(N_EXPERTS)), axis=0, keepdims=True)
        hot = eid == first
        work = jnp.where(hot, -jnp.inf, work)
        vals.append(m)
        hots.append(hot)
        idx_ref[kk:kk + 1, :] = first.astype(I32)
    es = [jnp.exp(vv - vals[0]) for vv in vals]
    den = es[0] + es[1] + es[2] + es[3]
    for kk in range(TOP_K):
        gate_ref[kk:kk + 1, :] = es[kk] / den

    chosen = hots[0] | hots[1] | hots[2] | hots[3]
    before = (lax.broadcasted_iota(I32, (rows, rows), 0) < lax.broadcasted_iota(I32, (rows, rows), 1)).astype(BF16)
    rank = jnp.dot(chosen.astype(BF16), before, preferred_element_type=F32) + counts[...]
    for kk in range(TOP_K):
        pos_ref[kk:kk + 1, :] = jnp.sum(jnp.where(hots[kk], rank, 0.0), axis=0, keepdims=True).astype(I32)
    counts[...] = counts[...] + jnp.sum(chosen.astype(F32), axis=1, keepdims=True)
    cnt_ref[...] = counts[...]

    @pl.when(i == pl.num_programs(0) - 1)
    def _():
        pad_start, blk_e, blk_valid, blk_next = _block_plan(counts[...], bm, blke_ref.shape[1])
        pstart_ref[...] = pad_start.astype(I32)
        blke_ref[...] = blk_e.astype(I32)
        blkv_ref[...] = blk_valid.astype(I32)
        blkn_ref[...] = blk_next.astype(I32)


def _route(x1, g_ffn, wr_t, br, cnt_in, rows, bm, n_blocks):
    n = x1.shape[0]
    assert n % rows == 0
    nb_lanes = -(-n_blocks // LANES) * LANES
    const = lambda shape: pl.BlockSpec(shape, lambda i: (0,) * len(shape))
    tok_blk = pl.BlockSpec((TOP_K, rows), lambda i: (0, i))
    return pl.pallas_call(
        functools.partial(_route_kernel, bm=bm),
        grid=(n // rows,),
        in_specs=[pl.BlockSpec((rows, D_MODEL), lambda i: (i, 0)), const((1, D_MODEL)), const((N_EXPERTS, D_MODEL)),
                  const((N_EXPERTS, 1)), const((N_EXPERTS, 1))],
        out_specs=[pl.BlockSpec((rows, D_MODEL // 2), lambda i: (i, 0)), tok_blk, tok_blk, tok_blk,
                   const((N_EXPERTS, 1)), const((N_EXPERTS, 1)), const((1, nb_lanes)), const((1, nb_lanes)),
                   const((1, nb_lanes))],
        out_shape=[jax.ShapeDtypeStruct((n, D_MODEL // 2), U32), jax.ShapeDtypeStruct((TOP_K, n), I32),
                   jax.ShapeDtypeStruct((TOP_K, n), F32), jax.ShapeDtypeStruct((TOP_K, n), I32),
                   jax.ShapeDtypeStruct((N_EXPERTS, 1), F32), jax.ShapeDtypeStruct((N_EXPERTS, 1), I32),
                   jax.ShapeDtypeStruct((1, nb_lanes), I32), jax.ShapeDtypeStruct((1, nb_lanes), I32),
                   jax.ShapeDtypeStruct((1, nb_lanes), I32)],
        scratch_shapes=[pltpu.VMEM((N_EXPERTS, 1), F32)],
        compiler_params=pltpu.CompilerParams(dimension_semantics=("arbitrary",), vmem_limit_bytes=VMEM_LIMIT),
        name="route",
    )(x1, g_ffn, wr_t, br, cnt_in)


def _sc_mesh():
    return plsc.VectorSubcoreMesh(core_axis_name="core", subcore_axis_name="subcore")


def _sc_worker_id():
    return lax.axis_index("core") * SC_SUBCORES + lax.axis_index("subcore")


def _scatter_rows(xa, xb, dest, n_rows):
    ch = SC_CHUNK
    na, w = xa.shape
    n = na + xb.shape[0]
    nk = dest.shape[0]
    assert na % ch == 0 and n % ch == 0 and dest.shape[1] == n and xb.shape[1] == w and xa.dtype == xb.dtype
    n_chunks = n // ch
    dest_c = dest.reshape(nk, n_chunks, ch).transpose(1, 0, 2).reshape(-1)
    dma = pltpu.SemaphoreType.DMA

    @pl.kernel(out_type=jax.ShapeDtypeStruct((n_rows, w), xa.dtype), mesh=_sc_mesh(),
               scratch_types=[pltpu.VMEM((nk * ch,), I32), pltpu.VMEM((ch, w), xa.dtype), dma] + [dma] * nk)
    def scatter_kernel(xa_hbm, xb_hbm, d_hbm, o_hbm, idx_v, buf, sem_i, *sem_s):
        wid = _sc_worker_id()

        @pl.loop(0, -(-n_chunks // SC_WORKERS))
        def _(j):
            c = j * SC_WORKERS + wid

            @pl.when(c < n_chunks)
            def _():
                load_idx = pltpu.make_async_copy(d_hbm.at[pl.ds(c * (nk * ch), nk * ch)], idx_v, sem_i)
                load_idx.start()

                @pl.when(c < na // ch)
                def _():
                    pltpu.sync_copy(xa_hbm.at[pl.ds(c * ch, ch)], buf)

                @pl.when(c >= na // ch)
                def _():
                    pltpu.sync_copy(xb_hbm.at[pl.ds(c * ch - na, ch)], buf)

                load_idx.wait()
                puts = [pltpu.make_async_copy(buf, o_hbm.at[idx_v.at[pl.ds(kk * ch, ch)]], sem_s[kk])
                        for kk in range(nk)]
                for cp in puts:
                    cp.start()
                for cp in puts:
                    cp.wait()

    return scatter_kernel(xa, xb, dest_c)


def _gather_rows(src, idx):
    ch = SC_CHUNK
    m = idx.shape[0]
    w = src.shape[1]
    per = m // SC_WORKERS
    n_pairs = per // (2 * ch)
    assert m % SC_WORKERS == 0 and per % (2 * ch) == 0
    dma = pltpu.SemaphoreType.DMA

    @pl.kernel(out_type=jax.ShapeDtypeStruct((m, w), src.dtype), mesh=_sc_mesh(),
               scratch_types=[pltpu.VMEM((per,), I32), pltpu.VMEM((ch, w), src.dtype), pltpu.VMEM((ch, w), src.dtype),
                              dma, dma, dma, dma])
    def gather_kernel(s_hbm, i_hbm, o_hbm, idx_v, buf_a, buf_b, sem_ga, sem_gb, sem_wa, sem_wb):
        base = _sc_worker_id() * per
        pltpu.sync_copy(i_hbm.at[pl.ds(base, per)], idx_v)

        def fetch(j, buf, sem):
            return pltpu.make_async_copy(s_hbm.at[idx_v.at[pl.ds(j * ch, ch)]], buf, sem)

        def put(j, buf, sem):
            return pltpu.make_async_copy(buf, o_hbm.at[pl.ds(base + j * ch, ch)], sem)

        fetch(0, buf_a, sem_ga).start()

        @pl.loop(0, n_pairs)
        def _(p):
            j0 = 2 * p
            j1 = j0 + 1

            @pl.when(p > 0)
            def _():
                put(j1 - 2, buf_b, sem_wb).wait()

            fetch(j1, buf_b, sem_gb).start()
            fetch(j0, buf_a, sem_ga).wait()
            put(j0, buf_a, sem_wa).start()
            fetch(j1, buf_b, sem_gb).wait()
            put(j1, buf_b, sem_wb).start()
            put(j0, buf_a, sem_wa).wait()

            @pl.when(p + 1 < n_pairs)
            def _():
                fetch(j0 + 2, buf_a, sem_ga).start()

        put(2 * n_pairs - 1, buf_b, sem_wb).wait()

    return gather_kernel(src, idx)


def _moe_kernel(blk_e_ref, blk_valid_ref, blk_next_ref, xs_ref, wgu_hbm, bgu_ref, wd_hbm, bd_ref, y_ref,
                wgu_f32, wd_f32, wgu_bf, wd_bf, sems, *, e0):
    i = pl.program_id(0)
    e = blk_e_ref[i]
    n_valid = blk_valid_ref[i]
    used = n_valid > 0
    new_expert = (i == 0) | (blk_e_ref[jnp.maximum(i - 1, 0)] != e)

    def weight_copies(expert):
        return (pltpu.make_async_copy(wgu_hbm.at[e0 + expert], wgu_f32, sems.at[0]),
                pltpu.make_async_copy(wd_hbm.at[e0 + expert], wd_f32, sems.at[1]))

    @pl.when(used & (i == 0))
    def _():
        for cp in weight_copies(e):
            cp.start()

    @pl.when(used & new_expert)
    def _():
        for cp in weight_copies(e):
            cp.wait()
        wgu_bf[...] = wgu_f32[...].astype(BF16)
        wd_bf[...] = wd_f32[...].astype(BF16)
        nxt = blk_next_ref[i]

        @pl.when(nxt >= 0)
        def _():
            for cp in weight_copies(nxt):
                cp.start(priority=1)

    @pl.when(used)
    def _():
        valid = lax.broadcasted_iota(I32, xs_ref.shape, 0) < n_valid
        xb = _unpack_halves(jnp.where(valid, xs_ref[...], jnp.uint32(0))).astype(BF16)
        gu = jnp.dot(xb, wgu_bf[...], preferred_element_type=F32) + bgu_ref[0]
        g = jnp.minimum(gu[:, :D_FF], SWIGLU_LIMIT)
        up = jnp.clip(gu[:, D_FF:], -SWIGLU_LIMIT, SWIGLU_LIMIT)
        act = (up + 1.0) * (g * jax.nn.sigmoid(SWIGLU_ALPHA * g))
        y_ref[...] = _pack_halves(jnp.dot(act.astype(BF16), wd_bf[...], preferred_element_type=F32) + bd_ref[0])

    @pl.when(jnp.logical_not(used))
    def _():
        y_ref[...] = jnp.zeros_like(y_ref)


def _moe(xs, blk_e, blk_valid, blk_next, layer, w_gu, b_gu, w_d, b_d):
    n_rows = xs.shape[0]
    bm = MOE_BM
    assert n_rows % bm == 0
    e0 = layer * N_EXPERTS
    n_we = w_gu.shape[0] * w_gu.shape[1]
    any_spec = pl.BlockSpec(memory_space=pl.ANY)
    grid_spec = pltpu.PrefetchScalarGridSpec(
        num_scalar_prefetch=3,
        grid=(n_rows // bm,),
        in_specs=[pl.BlockSpec((bm, D_MODEL // 2), lambda i, be, bv, bn: (i, 0)),
                  any_spec,
                  pl.BlockSpec((1, 1, 2 * D_FF), lambda i, be, bv, bn: (e0 + be[i], 0, 0)),
                  any_spec,
                  pl.BlockSpec((1, 1, D_MODEL), lambda i, be, bv, bn: (e0 + be[i], 0, 0))],
        out_specs=pl.BlockSpec((bm, D_MODEL // 2), lambda i, be, bv, bn: (i, 0)),
        scratch_shapes=[pltpu.VMEM((D_MODEL, 2 * D_FF), F32), pltpu.VMEM((D_FF, D_MODEL), F32),
                        pltpu.VMEM((D_MODEL, 2 * D_FF), BF16), pltpu.VMEM((D_FF, D_MODEL), BF16),
                        pltpu.SemaphoreType.DMA((2,))],
    )
    return pl.pallas_call(
        functools.partial(_moe_kernel, e0=e0),
        grid_spec=grid_spec,
        out_shape=jax.ShapeDtypeStruct((n_rows, D_MODEL // 2), U32),
        compiler_params=pltpu.CompilerParams(dimension_semantics=("arbitrary",), vmem_limit_bytes=VMEM_LIMIT),
        name="moe_experts",
    )(blk_e, blk_valid, blk_next, xs, w_gu.reshape(n_we, D_MODEL, 2 * D_FF), b_gu.reshape(n_we, 1, 2 * D_FF),
      w_d.reshape(n_we, D_FF, D_MODEL), b_d.reshape(n_we, 1, D_MODEL))


def _ple_kernel(x1_ref, y0_ref, y1_ref, y2_ref, y3_ref, gates_ref, p_ref, g_ref, wg_ref, wp_ref, *rest):
    out_ref, wg_bf, wp_bf = rest[-3:]

    @pl.when(pl.program_id(0) == 0)
    def _():
        wg_bf[...] = wg_ref[...].astype(BF16)
        wp_bf[...] = wp_ref[...].astype(BF16)

    x2 = x1_ref[...]
    gates = gates_ref[...]
    for kk, y_ref in enumerate((y0_ref, y1_ref, y2_ref, y3_ref)):
        x2 = x2 + _unpack_halves(y_ref[...]) * gates[:, kk:kk + 1]
    hp = _rms(x2, g_ref[...]).astype(BF16)
    gate = jax.nn.sigmoid(jnp.dot(hp, wg_bf[...], preferred_element_type=F32))
    pp = jnp.dot(p_ref[...].astype(BF16), wp_bf[...], preferred_element_type=F32)
    out_ref[...] = x2 + gate * pp


def _ple(x1, tok0, n, y_tok, y0, gates_t, p_all, p0, g_ple, w_gate, w_proj, rows, out_into=None):
    assert n % rows == 0 and tok0 % rows == 0 and y0 % rows == 0 and p0 % rows == 0
    const = lambda shape: pl.BlockSpec(shape, lambda i: (0,) * len(shape))
    tok_blk = lambda width: pl.BlockSpec((rows, width), lambda i: (tok0 // rows + i, 0))
    y_blk = lambda kk: pl.BlockSpec((rows, D_MODEL // 2), lambda i: ((y0 + kk * n) // rows + i, 0))
    extra_specs, extra_args, aliases = [], [], {}
    if out_into is not None:
        assert out_into.shape == x1.shape
        extra_specs, extra_args, aliases = [pl.BlockSpec(memory_space=pl.ANY)], [out_into], {10: 0}
    return pl.pallas_call(
        _ple_kernel,
        grid=(n // rows,),
        in_specs=[tok_blk(D_MODEL), y_blk(0), y_blk(1), y_blk(2), y_blk(3), tok_blk(TOP_K),
                  pl.BlockSpec((rows, PLE_DIM), lambda i: (p0 // rows + i, 0)),
                  const((1, D_MODEL)), const((D_MODEL, D_MODEL)), const((PLE_DIM, D_MODEL))] + extra_specs,
        out_specs=tok_blk(D_MODEL),
        out_shape=jax.ShapeDtypeStruct(x1.shape, F32),
        scratch_shapes=[pltpu.VMEM((D_MODEL, D_MODEL), BF16), pltpu.VMEM((PLE_DIM, D_MODEL), BF16)],
        input_output_aliases=aliases,
        compiler_params=pltpu.CompilerParams(dimension_semantics=("arbitrary",), vmem_limit_bytes=VMEM_LIMIT),
        name="combine_ple",
    )(x1, y_tok, y_tok, y_tok, y_tok, gates_t, p_all, g_ple, w_gate, w_proj, *extra_args)


def _rope_tables(pos):
    half = ROT_DIM // 2
    d = np.arange(LANES) % HEAD_DIM
    inv = ROPE_THETA ** (-jnp.arange(half, dtype=F32) / half)
    inv_lane = jnp.where(d < ROT_DIM, inv[d % half], 0.0)
    ang = pos.astype(F32)[:, None] * inv_lane[None, :]
    cos, sin = jnp.cos(ang), jnp.sin(ang)
    sin_a = jnp.where((d >= half) & (d < ROT_DIM), sin, 0.0)
    sin_b = jnp.where(d < half, -sin, 0.0)
    return cos, sin_a, sin_b


def _layer(layer, xp, xs, ck, cv, st, p_prompt_all, p_sample_all, past_len, refine_tail, tabs_p, tabs_s, finish_state,
           norm_attn, w_in, q_norm, k_norm, attn_sinks, w_pool, pool_scale, w_out,
           norm_ffn, w_router, b_router, w_gate_up_all, b_gate_up_all, w_down_all, b_down_all,
           norm_ple, w_ple_gate, w_ple_proj):
    t, ns = xp.shape[0], xs.shape[0]
    g_attn = norm_attn.reshape(1, D_MODEL)
    qn = jnp.tile(q_norm, 2).reshape(1, LANES)
    kn = jnp.tile(k_norm, 2).reshape(1, LANES)
    lane = np.arange(LANES)
    hmean = jnp.asarray((lane[:, None] // HEAD_DIM == lane[None, :] // HEAD_DIM) / HEAD_DIM, F32)
    pscale = pool_scale.reshape(1, POOL_WIDTH)
    sink8 = jnp.broadcast_to(attn_sinks.reshape(2, Q_TILES).T.reshape(8, 1), (8, LANES))

    mix_args = (g_attn, w_in, qn, kn, hmean, w_pool, pscale, w_out)
    x1p, nk_p, nv_p, nu_p = _mixer_prompt(xp, 0, MIX_ROWS, BF16, attn_sinks, tabs_p, *mix_args)
    if refine_tail:
        x1p, nk_p, nv_p, nu_p = _mixer_prompt(xp, t - 2 * MIX_TAIL_ROWS, MIX_TAIL_ROWS, F32, attn_sinks, tabs_p,
                                              *mix_args, x1_into=x1p)
    st_t = jnp.transpose(st, (1, 0, 2))
    x1s, nk_s, nv_s, nst_t = _mixer_sample(xs, ck.reshape(ns, -1, KV_WIDTH), cv.reshape(ns, -1, KV_WIDTH), st_t,
                                           past_len, tabs_s, sink8, *mix_args)

    g_ffn = norm_ffn.reshape(1, D_MODEL)
    wr_t = w_router.T
    br = b_router.reshape(N_EXPERTS, 1)
    n_tok = t + ns
    bm = MOE_BM
    n_blocks = -(-(n_tok * TOP_K + N_EXPERTS * (bm - 1)) // bm)
    hf_p, idx_p, gate_p, pos_p, cnt_p = _route(x1p, g_ffn, wr_t, br, jnp.zeros((N_EXPERTS, 1), F32),
                                               ROUTE_ROWS, bm, n_blocks)[:5]
    hf_s, idx_s, gate_s, pos_s, _, pad_start, blk_e, blk_valid, blk_next = _route(x1s, g_ffn, wr_t, br, cnt_p, ns,
                                                                                  bm, n_blocks)
    idx = jnp.concatenate([idx_p, idx_s], axis=1)
    pos = jnp.concatenate([pos_p, pos_s], axis=1)
    blk_e, blk_valid, blk_next = (a[0, :n_blocks] for a in (blk_e, blk_valid, blk_next))
    state = finish_state(nk_p, nv_p, nu_p[POOL_HALO - POOL_PREFIX:], nk_s, nv_s, jnp.transpose(nst_t, (1, 0, 2)))
    blk_e, state = lax.optimization_barrier((blk_e, state))
    start_of = jnp.sum(jnp.where(idx[None] == jnp.arange(N_EXPERTS, dtype=I32)[:, None, None],
                                 pad_start.reshape(N_EXPERTS, 1, 1), 0), axis=0)
    dest = start_of + pos

    xs_rows = _scatter_rows(hf_p, hf_s, dest, n_blocks * bm)
    y = _moe(xs_rows, blk_e, blk_valid, blk_next, layer, w_gate_up_all, b_gate_up_all, w_down_all, b_down_all)

    th = t // 2
    unit = SC_WORKERS * SC_CHUNK * 2
    assert (TOP_K * th) % unit == 0
    back_b = jnp.concatenate([dest[:, th:t].reshape(-1), dest[:, t:].reshape(-1)])
    back_b = jnp.concatenate([back_b, jnp.arange(-back_b.shape[0] % unit, dtype=I32)])
    y_a = _gather_rows(y, dest[:, :th].reshape(-1))
    y_b = _gather_rows(y, back_b)

    g_ple = norm_ple.reshape(1, D_MODEL)
    wg_bf, wp_bf = w_ple_gate, w_ple_proj
    gates_p = gate_p.T
    yp = _ple(x1p, 0, th, y_a, 0, gates_p, p_prompt_all, layer * t, g_ple, wg_bf, wp_bf, PLE_ROWS)
    yp = _ple(x1p, th, t - th, y_b, 0, gates_p, p_prompt_all, layer * t + th, g_ple, wg_bf, wp_bf, PLE_ROWS,
              out_into=yp)
    ys = _ple(x1s, 0, ns, y_b, TOP_K * (t - th), gate_s.T, p_sample_all, layer * ns, g_ple, wg_bf, wp_bf, ns)
    return yp, ys, state


def kernel(x_prompt, x_sample, cache_k, cache_v, state_pool, p_prompt, p_sample, norm_attn, w_in, q_norm, k_norm,
           attn_sinks, w_pool, pool_scale, w_out, norm_ffn, w_router, b_router, w_gate_up, b_gate_up, w_down, b_down,
           norm_ple, w_ple_gate, w_ple_proj):
    depth = norm_attn.shape[0]
    batch, seq, d = x_prompt.shape
    ns, dec_seq, _ = x_sample.shape
    wb = cache_k.shape[2]
    assert batch == 1 and dec_seq == 1 and d == D_MODEL and wb == WINDOW
    assert cache_k.shape[3:] == (N_KV_HEADS, HEAD_DIM) and state_pool.shape[2:] == (POOL_PREFIX, POOL_WIDTH)
    past_len = PAST_LEN
    yp = x_prompt.reshape(seq, d)
    ys = x_sample.reshape(ns, d)
    p_prompt_all = p_prompt.reshape(depth * seq, PLE_DIM)
    p_sample_all = p_sample.reshape(depth * ns, PLE_DIM)
    tabs_p = _rope_tables(jnp.arange(seq))
    tabs_s = _rope_tables(jnp.full((1,), past_len))
    kv_p, kv_s = (1, WINDOW, N_KV_HEADS, HEAD_DIM), (ns, wb, N_KV_HEADS, HEAD_DIM)
    state_shapes = (kv_p, kv_p, (1, POOL_PREFIX, POOL_WIDTH), kv_s, kv_s, (ns, POOL_PREFIX, POOL_WIDTH))
    per_layer = []

    def shaped(*state):
        return tuple(a.reshape(shp) for a, shp in zip(state, state_shapes))

    def stacked(*state):
        prevs = list(zip(*per_layer)) or [()] * len(state_shapes)
        return tuple(jnp.stack(list(prev) + [cur]) for prev, cur in zip(prevs, shaped(*state)))

    for i in range(depth):
        yp, ys, state = _layer(i, yp, ys, cache_k[i], cache_v[i], state_pool[i], p_prompt_all, p_sample_all, past_len,
                               i < depth - 1, tabs_p, tabs_s, shaped if i < depth - 1 else stacked,
                               norm_attn[i], w_in[i], q_norm[i], k_norm[i], attn_sinks[i], w_pool[i], pool_scale[i],
                               w_out[i], norm_ffn[i], w_router[i], b_router[i], w_gate_up, b_gate_up, w_down, b_down,
                               norm_ple[i], w_ple_gate[i], w_ple_proj[i])
        per_layer.append(state)
    return (yp.reshape(batch, seq, d), ys.reshape(ns, dec_seq, d)) + tuple(state)
```

```python
import functools

import jax
import jax.numpy as jnp
import numpy as np
from jax import lax
from jax.experimental import pallas as pl
from jax.experimental.pallas import tpu as pltpu
from jax.experimental.pallas import tpu_sc as plsc

F32 = jnp.float32
BF16 = jnp.bfloat16
U32 = jnp.uint32
I32 = jnp.int32

D_MODEL = 1024
HEAD_DIM = 64
N_HEADS = 8
N_KV_HEADS = 2
GROUP = N_HEADS // N_KV_HEADS
ATTN_WIDTH = N_HEADS * HEAD_DIM
KV_WIDTH = N_KV_HEADS * HEAD_DIM
POOL_WIDTH = 512
POOL_WINDOWS = (2, 4, 8, 16)
POOL_GC = POOL_WIDTH // len(POOL_WINDOWS)
POOL_PREFIX = max(POOL_WINDOWS) - 1
POOL_HALO = POOL_PREFIX + 1
POOL_PAD = 8
IN_WIDTH = ATTN_WIDTH + 2 * KV_WIDTH + POOL_WIDTH
WINDOW = 128
ROPE_THETA = 500000.0
ROT_DIM = HEAD_DIM // 4
N_EXPERTS = 32
TOP_K = 4
D_FF = 1024
SWIGLU_ALPHA = 1.702
SWIGLU_LIMIT = 7.0
PLE_DIM = 256
PAST_LEN = 16384
EPS = 1e-5
NEG_INF = -1e30

LANES = 128
Q_TILES = ATTN_WIDTH // LANES

MIX_ROWS = 512
MIX_TAIL_ROWS = 128
SAMPLE_CHUNK = 16
ROUTE_ROWS = 512
MOE_BM = 512
SC_CORES = 2
SC_SUBCORES = 16
SC_WORKERS = SC_CORES * SC_SUBCORES
SC_CHUNK = 64
PLE_ROWS = 512
COMBINE_PARTS = 4
VMEM_LIMIT = 56 * 1024 * 1024


def _rms(x, g):
    return x * lax.rsqrt(jnp.mean(x * x, axis=-1, keepdims=True) + EPS) * g


def _pack_halves(x):
    w = x.shape[1] // 2
    lo = lax.bitcast_convert_type(x[:, :w].astype(BF16).astype(F32), U32) >> 16
    hi = lax.bitcast_convert_type(x[:, w:].astype(BF16).astype(F32), U32) & jnp.uint32(0xFFFF0000)
    return lo | hi


def _unpack_halves(packed):
    lo = lax.bitcast_convert_type(packed << 16, F32)
    hi = lax.bitcast_convert_type(packed & jnp.uint32(0xFFFF0000), F32)
    return jnp.concatenate([lo, hi], axis=1)


def _mm(a, b, nt=False):
    dims = (((1,), (1 if nt else 0,)), ((), ()))
    if b.dtype == F32:
        return lax.dot_general(a.astype(F32), b, dims, preferred_element_type=F32, precision=lax.Precision.HIGHEST)
    return lax.dot_general(a.astype(BF16), b, dims, preferred_element_type=F32)


def _head_norm_rope(t, hmean, gain, cos, sin_a, sin_b):
    t = t * lax.rsqrt(_mm(t * t, hmean) + EPS) * gain
    return t * cos + pltpu.roll(t, ROT_DIM // 2, axis=1) * sin_a + pltpu.roll(t, LANES - ROT_DIM // 2, axis=1) * sin_b


def _softmax_pv(s, sink, v, ones):
    m = jnp.maximum(jnp.max(s, axis=-1, keepdims=True), sink)
    e = jnp.exp(s - m).astype(v.dtype)
    den = _mm(e, ones) + jnp.exp(sink - m)
    return _mm(e, v) / den


def _prep_mixer_weights(win_ref, wout_ref, win_c, wout_c):
    cdt = win_c.dtype
    for j in range(Q_TILES):
        for s in range(2):
            src = (j + Q_TILES * s) * HEAD_DIM
            dst = j * LANES + s * HEAD_DIM
            win_c[:, dst:dst + HEAD_DIM] = win_ref[:, src:src + HEAD_DIM].astype(cdt)
            wout_c[dst:dst + HEAD_DIM, :] = wout_ref[src:src + HEAD_DIM, :].astype(cdt)
    win_c[:, ATTN_WIDTH:] = win_ref[:, ATTN_WIDTH:].astype(cdt)
    wout_c[ATTN_WIDTH:, :] = wout_ref[ATTN_WIDTH:, :].astype(cdt)


def _mixer_prompt_kernel(sinks_ref, x_ref, cos_ref, sa_ref, sb_ref, g_ref, win_ref, qn_ref, kn_ref, hm_ref,
                         wpool_ref, pscale_ref, wout_ref, *rest, row_offset, aliased):
    (x1_ref, klast_ref, vlast_ref, ulast_ref, kprev, vprev, uext, mix, win_c, wout_c,
     *lvl) = rest[1:] if aliased else rest
    i = pl.program_id(0)
    rows = x_ref.shape[0]
    n_sub = rows // WINDOW
    cdt = win_c.dtype
    row0 = row_offset + i * rows

    @pl.when(i == 0)
    def _():
        _prep_mixer_weights(win_ref, wout_ref, win_c, wout_c)
        kprev[...] = jnp.zeros_like(kprev)
        vprev[...] = jnp.zeros_like(vprev)
        uext[0:POOL_PAD + POOL_HALO, :] = jnp.zeros((POOL_PAD + POOL_HALO, POOL_WIDTH), F32)
        for buf in lvl:
            buf[0:POOL_PAD, :] = jnp.zeros((POOL_PAD, POOL_GC), F32)

    x = x_ref[...]
    proj = _mm(_rms(x, g_ref[...]), win_c[...])
    cos, sin_a, sin_b = cos_ref[...], sa_ref[...], sb_ref[...]

    n_t = Q_TILES + 1
    t_all = jnp.concatenate([proj[:, j * LANES:(j + 1) * LANES] for j in range(n_t)], axis=0)
    t3 = (t_all * lax.rsqrt(_mm(t_all * t_all, hm_ref[...].astype(cdt)) + EPS)).reshape(n_t, rows, LANES)
    t3 = jnp.concatenate([t3[:Q_TILES] * (qn_ref[...] * HEAD_DIM ** -0.5), t3[Q_TILES:] * kn_ref[...]], axis=0)
    t2 = t3.reshape(n_t * rows, LANES)
    t3 = (t3 * cos + pltpu.roll(t2, ROT_DIM // 2, axis=1).reshape(n_t, rows, LANES) * sin_a
          + pltpu.roll(t2, LANES - ROT_DIM // 2, axis=1).reshape(n_t, rows, LANES) * sin_b)
    q3 = t3[:Q_TILES]
    k = t3[Q_TILES]
    v = proj[:, ATTN_WIDTH + KV_WIDTH:ATTN_WIDTH + 2 * KV_WIDTH]
    u = proj[:, ATTN_WIDTH + 2 * KV_WIDTH:]
    klast_ref[...] = k[rows - WINDOW:, :]
    vlast_ref[...] = v[rows - WINDOW:, :]
    ulast_ref[...] = u[rows - POOL_HALO:, :]
    k_c = k.astype(cdt)
    v_c = jnp.concatenate([v.astype(cdt), jnp.ones((rows, LANES), cdt)], axis=1)
    v_first = jnp.concatenate([vprev[...], jnp.ones((WINDOW, LANES), cdt)], axis=1)

    lane = lax.broadcasted_iota(I32, (WINDOW, LANES), 1)
    left = (lane < HEAD_DIM)[None]
    qi = lax.broadcasted_iota(I32, (WINDOW, 2 * WINDOW), 0)
    kj = lax.broadcasted_iota(I32, (WINDOW, 2 * WINDOW), 1)
    band = (kj - qi >= 1) & (kj - qi <= WINDOW)
    sink3 = jnp.concatenate([jnp.full((1, 1, 1), sinks_ref[j + Q_TILES * s], F32)
                             for j in range(Q_TILES) for s in range(2)], axis=0)
    n_g = 2 * Q_TILES

    v_cats, masks, scores = [], [], []
    for c in range(n_sub):
        r0 = c * WINDOW
        if c == 0:
            k_cat = jnp.concatenate([kprev[...], k_c[0:WINDOW]], axis=0)
            v_cats.append(jnp.concatenate([v_first, v_c[0:WINDOW]], axis=0))
            masks.append(band & (kj + (row0 - WINDOW) >= 0))
        else:
            k_cat = k_c[r0 - WINDOW:r0 + WINDOW]
            v_cats.append(v_c[r0 - WINDOW:r0 + WINDOW])
            masks.append(band)
        q_c = q3[:, r0:r0 + WINDOW, :]
        q_all = jnp.concatenate([jnp.where(left, q_c, 0.0), jnp.where(left, 0.0, q_c)], axis=1)
        scores.append(_mm(q_all.reshape(n_g * WINDOW, LANES), k_cat, nt=True))
    kprev[...] = k_c[rows - WINDOW:]
    vprev[...] = v[rows - WINDOW:].astype(cdt)

    probs, maxes = [], []
    for c in range(n_sub):
        s = jnp.where(masks[c][None], scores[c].reshape(n_g, WINDOW, 2 * WINDOW), NEG_INF)
        m = jnp.maximum(jnp.max(s, axis=-1, keepdims=True), sink3)
        probs.append(jnp.exp(s - m).astype(cdt).reshape(n_g * WINDOW, 2 * WINDOW))
        maxes.append(m)
    applied = [_mm(probs[c], v_cats[c]) for c in range(n_sub)]
    for c in range(n_sub):
        r0 = c * WINDOW
        den = applied[c][:, LANES:] + jnp.exp(sink3 - maxes[c]).reshape(n_g * WINDOW, 1)
        o = (applied[c][:, :LANES] / den).reshape(Q_TILES, 2 * WINDOW, LANES)
        a = jnp.where(left, o[:, :WINDOW], o[:, WINDOW:])
        for j in range(Q_TILES):
            mix[r0:r0 + WINDOW, j * LANES:(j + 1) * LANES] = a[j].astype(cdt)

    base = POOL_PAD + POOL_HALO
    ext = base + rows
    uext[base:ext, :] = u
    pos1 = (lax.broadcasted_iota(I32, (rows, 1), 0) + row0 + 1).astype(F32)
    lvl_of = {1: lvl[0:1], 2: lvl[1:3], 3: lvl[3:5]}
    for gi, w in enumerate(POOL_WINDOWS):
        cols = slice(gi * POOL_GC, (gi + 1) * POOL_GC)
        src, src_cols = uext, cols
        for level in range(1, gi + 2):
            sft = 1 << (level - 1)
            if level <= gi:
                dst = lvl_of[gi][(level - 1) % 2]
                dst[POOL_PAD:ext, :] = src[POOL_PAD:ext, src_cols] + src[POOL_PAD - sft:ext - sft, src_cols]
                src, src_cols = dst, slice(None)
            else:
                wsum = src[base:ext, src_cols] + src[base - sft:ext - sft, src_cols]
        d = wsum / jnp.minimum(pos1, float(w)) - u[:, cols]
        y = _mm(d, wpool_ref[gi].astype(cdt)) * pscale_ref[:, cols]
        mix[:, ATTN_WIDTH + gi * POOL_GC:ATTN_WIDTH + (gi + 1) * POOL_GC] = y.astype(cdt)
    uext[POOL_PAD:base, :] = u[rows - POOL_HALO:, :]

    x1_ref[...] = x + _mm(mix[...], wout_c[...])


def _mixer_prompt(x_full, row_offset, rows, cdt, sinks, tabs, g_attn, w_in, qn, kn, hmean, w_pool, pscale, w_out,
                  x1_into=None):
    t = x_full.shape[0] - row_offset
    assert t % rows == 0 and row_offset % rows == 0 and rows % WINDOW == 0 and rows >= POOL_HALO
    blk0, n_steps = row_offset // rows, t // rows
    const = lambda shape: pl.BlockSpec(shape, lambda i, *_: (0,) * len(shape))
    row_blk = lambda width: pl.BlockSpec((rows, width), lambda i, *_: (blk0 + i, 0))
    aliased = x1_into is not None
    if aliased:
        assert x1_into.shape == x_full.shape
        x1_spec = pl.BlockSpec((rows, D_MODEL), lambda i, *_: (blk0 + n_steps - 1, 0))
        x1_shape = x1_into.shape
        extra_specs, extra_args, aliases = [pl.BlockSpec(memory_space=pl.ANY)], [x1_into], {13: 0}
    else:
        x1_spec = pl.BlockSpec((rows, D_MODEL), lambda i, *_: (i, 0))
        x1_shape = (t, D_MODEL)
        extra_specs, extra_args, aliases = [], [], {}
    grid_spec = pltpu.PrefetchScalarGridSpec(
        num_scalar_prefetch=1,
        grid=(n_steps,),
        in_specs=[row_blk(D_MODEL), row_blk(LANES), row_blk(LANES), row_blk(LANES),
                  const((1, D_MODEL)), const((D_MODEL, IN_WIDTH)), const((1, LANES)), const((1, LANES)),
                  const((LANES, LANES)), const((len(POOL_WINDOWS), POOL_GC, POOL_GC)), const((1, POOL_WIDTH)),
                  const((D_MODEL, D_MODEL))] + extra_specs,
        out_specs=[x1_spec, const((WINDOW, KV_WIDTH)), const((WINDOW, KV_WIDTH)),
                   const((POOL_HALO, POOL_WIDTH))],
        scratch_shapes=[pltpu.VMEM((WINDOW, KV_WIDTH), cdt), pltpu.VMEM((WINDOW, KV_WIDTH), cdt),
                        pltpu.VMEM((POOL_PAD + POOL_HALO + rows, POOL_WIDTH), F32), pltpu.VMEM((rows, D_MODEL), cdt),
                        pltpu.VMEM((D_MODEL, IN_WIDTH), cdt), pltpu.VMEM((D_MODEL, D_MODEL), cdt)]
        + [pltpu.VMEM((POOL_PAD + POOL_HALO + rows, POOL_GC), F32)] * 5,
    )
    return pl.pallas_call(
        functools.partial(_mixer_prompt_kernel, row_offset=row_offset, aliased=aliased),
        grid_spec=grid_spec,
        out_shape=[jax.ShapeDtypeStruct(x1_shape, F32), jax.ShapeDtypeStruct((WINDOW, KV_WIDTH), F32),
                   jax.ShapeDtypeStruct((WINDOW, KV_WIDTH), F32), jax.ShapeDtypeStruct((POOL_HALO, POOL_WIDTH), F32)],
        input_output_aliases=aliases,
        compiler_params=pltpu.CompilerParams(dimension_semantics=("arbitrary",), vmem_limit_bytes=VMEM_LIMIT),
        name="mixer_prompt",
    )(sinks, x_full, *tabs, g_attn, w_in, qn, kn, hmean, w_pool, pscale, w_out, *extra_args)


def _mixer_sample_kernel(x_ref, ck_ref, cv_ref, st_ref, cos_ref, sa_ref, sb_ref, sink8_ref, g_ref, win_ref, qn_ref,
                         kn_ref, hm_ref, wpool_ref, pscale_ref, wout_ref,
                         x1_ref, nk_ref, nv_ref, nst_ref, o8, win_c, wout_c, *, pos):
    nb = x_ref.shape[0]
    wb = ck_ref.shape[1]

    @pl.when(pl.program_id(0) == 0)
    def _():
        _prep_mixer_weights(win_ref, wout_ref, win_c, wout_c)

    x = x_ref[...]
    h = _rms(x, g_ref[...]).astype(BF16)
    proj = jnp.dot(h, win_c[...], preferred_element_type=F32)
    cos, sin_a, sin_b = cos_ref[...], sa_ref[...], sb_ref[...]
    hmean = hm_ref[...].astype(BF16)
    k = _head_norm_rope(proj[:, ATTN_WIDTH:ATTN_WIDTH + KV_WIDTH], hmean, kn_ref[...], cos, sin_a, sin_b)
    v = proj[:, ATTN_WIDTH + KV_WIDTH:ATTN_WIDTH + 2 * KV_WIDTH]
    u = proj[:, ATTN_WIDTH + 2 * KV_WIDTH:]

    nk_ref[:, 0:wb - 1, :] = ck_ref[:, 1:wb, :]
    nv_ref[:, 0:wb - 1, :] = cv_ref[:, 1:wb, :]
    for b in range(nb):
        nk_ref[b, wb - 1:wb, :] = k[b:b + 1, :]
        nv_ref[b, wb - 1:wb, :] = v[b:b + 1, :]

    r8 = lax.broadcasted_iota(I32, (nb * 8, LANES), 0)
    lane8 = lax.broadcasted_iota(I32, (nb * 8, LANES), 1)
    keep = (lane8 < HEAD_DIM) == (r8 % 2 == 0)
    rep = (lax.broadcasted_iota(I32, (nb * 8, nb), 0) // 8 == lax.broadcasted_iota(I32, (nb * 8, nb), 1)).astype(BF16)
    q8 = jnp.zeros((nb * 8, LANES), F32)
    scale = HEAD_DIM ** -0.5
    for j in range(Q_TILES):
        qt = _head_norm_rope(proj[:, j * LANES:(j + 1) * LANES], hmean, qn_ref[...], cos, sin_a, sin_b) * scale
        qrep = jnp.dot(rep, qt.astype(BF16), preferred_element_type=F32)
        q8 = jnp.where(keep & ((r8 % 8) // 2 == j), qrep, q8)
    q8 = q8.astype(BF16)

    sink8 = sink8_ref[:, 0:1]
    ones_bf = jnp.ones((wb, LANES), BF16)
    assert pos >= wb - 1 and wb <= WINDOW
    for b in range(nb):
        kb = nk_ref[b].astype(BF16)
        vb = nv_ref[b].astype(BF16)
        s = lax.dot_general(q8[b * 8:(b + 1) * 8], kb, (((1,), (1,)), ((), ())), preferred_element_type=F32)
        o8[b * 8:(b + 1) * 8, :] = _softmax_pv(s, sink8, vb, ones_bf)
    o8m = jnp.where(keep, o8[...], 0.0).astype(BF16)

    a_tiles = []
    sel_r = lax.broadcasted_iota(I32, (nb, nb * 8), 1)
    sel_b = lax.broadcasted_iota(I32, (nb, nb * 8), 0)
    for j in range(Q_TILES):
        sel = ((sel_r // 8 == sel_b) & ((sel_r % 8) // 2 == j)).astype(BF16)
        a_tiles.append(jnp.dot(sel, o8m, preferred_element_type=F32))

    z_tiles = []
    for gi, w in enumerate(POOL_WINDOWS):
        cols = slice(gi * POOL_GC, (gi + 1) * POOL_GC)
        wsum = u[:, cols]
        for sft in range(1, w):
            wsum = wsum + st_ref[POOL_PREFIX - sft, :, cols]
        d = wsum / float(min(pos + 1, w)) - u[:, cols]
        z_tiles.append(jnp.dot(d.astype(BF16), wpool_ref[gi].astype(BF16), preferred_element_type=F32)
                       * pscale_ref[:, cols])
    nst_ref[0:POOL_PREFIX - 1] = st_ref[1:POOL_PREFIX]
    nst_ref[POOL_PREFIX - 1] = u

    mixv = jnp.concatenate(a_tiles + z_tiles, axis=1).astype(BF16)
    x1_ref[...] = x + jnp.dot(mixv, wout_c[...], preferred_element_type=F32)


def _mixer_sample(x, ck, cv, st, pos, tabs, sink8, g_attn, w_in, qn, kn, hmean, w_pool, pscale, w_out):
    n, wb = ck.shape[0], ck.shape[1]
    nb = SAMPLE_CHUNK
    assert n % nb == 0
    const = lambda shape: pl.BlockSpec(shape, lambda i: (0,) * len(shape))
    cache_blk = pl.BlockSpec((nb, wb, KV_WIDTH), lambda i: (i, 0, 0))
    st_blk = pl.BlockSpec((POOL_PREFIX, nb, POOL_WIDTH), lambda i: (0, i, 0))
    x_blk = pl.BlockSpec((nb, D_MODEL), lambda i: (i, 0))
    return pl.pallas_call(
        functools.partial(_mixer_sample_kernel, pos=pos),
        grid=(n // nb,),
        in_specs=[x_blk, cache_blk, cache_blk, st_blk, const((1, LANES)), const((1, LANES)), const((1, LANES)),
                  const((8, LANES)), const((1, D_MODEL)), const((D_MODEL, IN_WIDTH)), const((1, LANES)),
                  const((1, LANES)), const((LANES, LANES)), const((len(POOL_WINDOWS), POOL_GC, POOL_GC)),
                  const((1, POOL_WIDTH)), const((D_MODEL, D_MODEL))],
        out_specs=[x_blk, cache_blk, cache_blk, st_blk],
        out_shape=[jax.ShapeDtypeStruct((n, D_MODEL), F32), jax.ShapeDtypeStruct(ck.shape, F32),
                   jax.ShapeDtypeStruct(cv.shape, F32), jax.ShapeDtypeStruct(st.shape, F32)],
        scratch_shapes=[pltpu.VMEM((nb * 8, LANES), F32), pltpu.VMEM((D_MODEL, IN_WIDTH), BF16),
                        pltpu.VMEM((D_MODEL, D_MODEL), BF16)],
        compiler_params=pltpu.CompilerParams(dimension_semantics=("arbitrary",), vmem_limit_bytes=VMEM_LIMIT),
        name="mixer_sample",
    )(x, ck, cv, st, *tabs, sink8, g_attn, w_in, qn, kn, hmean, w_pool, pscale, w_out)


def _block_plan(cnt, bm, n_lanes):
    e_sub = lax.broadcasted_iota(I32, (N_EXPERTS, LANES), 0)
    e_lane = lax.broadcasted_iota(I32, (N_EXPERTS, LANES), 1)
    padded = jnp.floor((cnt + (bm - 1)) / bm) * bm
    padded_lane = jnp.sum(jnp.where(e_sub == e_lane, padded, 0.0), axis=0, keepdims=True)
    pad_end = jnp.sum(jnp.where(e_lane <= e_sub, padded_lane, 0.0), axis=1, keepdims=True)
    pad_start = pad_end - padded
    blk_start = lax.broadcasted_iota(I32, (N_EXPERTS, n_lanes), 1).astype(F32) * bm
    blk_e = jnp.minimum(jnp.sum((pad_end <= blk_start).astype(F32), axis=0, keepdims=True), N_EXPERTS - 1.0)
    mine = lax.broadcasted_iota(I32, (N_EXPERTS, n_lanes), 0).astype(F32) == blk_e
    last = jnp.sum(jnp.where(mine, pad_start + cnt, 0.0), axis=0, keepdims=True)
    blk_valid = jnp.clip(last - blk_start[0:1], 0.0, float(bm))
    e_blk = lax.broadcasted_iota(I32, (N_EXPERTS, n_lanes), 0).astype(F32)
    later = jnp.min(jnp.where((e_blk > blk_e) & (cnt > 0.0), e_blk, float(N_EXPERTS)), axis=0, keepdims=True)
    blk_next = jnp.where(later < N_EXPERTS, later, -1.0)
    return pad_start, blk_e, blk_valid, blk_next


def _route_kernel(x1_ref, g_ref, wr_ref, br_ref, cnt_in_ref, hf_ref, idx_ref, gate_ref, pos_ref, cnt_ref,
                  pstart_ref, blke_ref, blkv_ref, blkn_ref, counts, *, bm):
    i = pl.program_id(0)
    rows = x1_ref.shape[0]

    @pl.when(i == 0)
    def _():
        counts[...] = cnt_in_ref[...]

    h = _rms(x1_ref[...], g_ref[...])
    hf_ref[...] = _pack_halves(h)

    logits = lax.dot_general(wr_ref[...], h, (((1,), (1,)), ((), ())), preferred_element_type=F32,
                             precision=lax.Precision.HIGHEST) + br_ref[...]
    eid = lax.broadcasted_iota(I32, (N_EXPERTS, rows), 0).astype(F32)
    work = logits
    vals, hots = [], []
    for kk in range(TOP_K):
        m = jnp.max(work, axis=0, keepdims=True)
        first = jnp.min(jnp.where(work == m, eid, float(N_EXPERTS)), axis=0, keepdims=True)
        hot = eid == first
        work = jnp.where(hot, -jnp.inf, work)
        vals.append(m)
        hots.append(hot)
        idx_ref[kk:kk + 1, :] = first.astype(I32)
    es = [jnp.exp(vv - vals[0]) for vv in vals]
    den = es[0] + es[1] + es[2] + es[3]
    for kk in range(TOP_K):
        gate_ref[kk:kk + 1, :] = es[kk] / den

    chosen = hots[0] | hots[1] | hots[2] | hots[3]
    before = (lax.broadcasted_iota(I32, (rows, rows), 0) < lax.broadcasted_iota(I32, (rows, rows), 1)).astype(BF16)
    rank = jnp.dot(chosen.astype(BF16), before, preferred_element_type=F32) + counts[...]
    for kk in range(TOP_K):
        pos_ref[kk:kk + 1, :] = jnp.sum(jnp.where(hots[kk], rank, 0.0), axis=0, keepdims=True).astype(I32)
    counts[...] = counts[...] + jnp.sum(chosen.astype(F32), axis=1, keepdims=True)
    cnt_ref[...] = counts[...]

    @pl.when(i == pl.num_programs(0) - 1)
    def _():
        pad_start, blk_e, blk_valid, blk_next = _block_plan(counts[...], bm, blke_ref.shape[1])
        pstart_ref[...] = pad_start.astype(I32)
        blke_ref[...] = blk_e.astype(I32)
        blkv_ref[...] = blk_valid.astype(I32)
        blkn_ref[...] = blk_next.astype(I32)


def _route(x1, g_ffn, wr_t, br, cnt_in, rows, bm, n_blocks):
    n = x1.shape[0]
    assert n % rows == 0
    nb_lanes = -(-n_blocks // LANES) * LANES
    const = lambda shape: pl.BlockSpec(shape, lambda i: (0,) * len(shape))
    tok_blk = pl.BlockSpec((TOP_K, rows), lambda i: (0, i))
    return pl.pallas_call(
        functools.partial(_route_kernel, bm=bm),
        grid=(n // rows,),
        in_specs=[pl.BlockSpec((rows, D_MODEL), lambda i: (i, 0)), const((1, D_MODEL)), const((N_EXPERTS, D_MODEL)),
                  const((N_EXPERTS, 1)), const((N_EXPERTS, 1))],
        out_specs=[pl.BlockSpec((rows, D_MODEL // 2), lambda i: (i, 0)), tok_blk, tok_blk, tok_blk,
                   const((N_EXPERTS, 1)), const((N_EXPERTS, 1)), const((1, nb_lanes)), const((1, nb_lanes)),
                   const((1, nb_lanes))],
        out_shape=[jax.ShapeDtypeStruct((n, D_MODEL // 2), U32), jax.ShapeDtypeStruct((TOP_K, n), I32),
                   jax.ShapeDtypeStruct((TOP_K, n), F32), jax.ShapeDtypeStruct((TOP_K, n), I32),
                   jax.ShapeDtypeStruct((N_EXPERTS, 1), F32), jax.ShapeDtypeStruct((N_EXPERTS, 1), I32),
                   jax.ShapeDtypeStruct((1, nb_lanes), I32), jax.ShapeDtypeStruct((1, nb_lanes), I32),
                   jax.ShapeDtypeStruct((1, nb_lanes), I32)],
        scratch_shapes=[pltpu.VMEM((N_EXPERTS, 1), F32)],
        compiler_params=pltpu.CompilerParams(dimension_semantics=("arbitrary",), vmem_limit_bytes=VMEM_LIMIT),
        name="route",
    )(x1, g_ffn, wr_t, br, cnt_in)


def _sc_mesh():
    return plsc.VectorSubcoreMesh(core_axis_name="core", subcore_axis_name="subcore")


def _sc_worker_id():
    return lax.axis_index("core") * SC_SUBCORES + lax.axis_index("subcore")


def _scatter_rows(xa, xb, dest, n_rows):
    ch = SC_CHUNK
    na, w = xa.shape
    n = na + xb.shape[0]
    nk = dest.shape[0]
    assert na % ch == 0 and n % ch == 0 and dest.shape[1] == n and xb.shape[1] == w and xa.dtype == xb.dtype
    n_chunks = n // ch
    dest_c = dest.reshape(nk, n_chunks, ch).transpose(1, 0, 2).reshape(-1)
    dma = pltpu.SemaphoreType.DMA

    @pl.kernel(out_type=jax.ShapeDtypeStruct((n_rows, w), xa.dtype), mesh=_sc_mesh(),
               scratch_types=[pltpu.VMEM((nk * ch,), I32), pltpu.VMEM((ch, w), xa.dtype), dma] + [dma] * nk)
    def scatter_kernel(xa_hbm, xb_hbm, d_hbm, o_hbm, idx_v, buf, sem_i, *sem_s):
        wid = _sc_worker_id()

        @pl.loop(0, -(-n_chunks // SC_WORKERS))
        def _(j):
            c = j * SC_WORKERS + wid

            @pl.when(c < n_chunks)
            def _():
                load_idx = pltpu.make_async_copy(d_hbm.at[pl.ds(c * (nk * ch), nk * ch)], idx_v, sem_i)
                load_idx.start()

                @pl.when(c < na // ch)
                def _():
                    pltpu.sync_copy(xa_hbm.at[pl.ds(c * ch, ch)], buf)

                @pl.when(c >= na // ch)
                def _():
                    pltpu.sync_copy(xb_hbm.at[pl.ds(c * ch - na, ch)], buf)

                load_idx.wait()
                puts = [pltpu.make_async_copy(buf, o_hbm.at[idx_v.at[pl.ds(kk * ch, ch)]], sem_s[kk])
                        for kk in range(nk)]
                for cp in puts:
                    cp.start()
                for cp in puts:
                    cp.wait()

    return scatter_kernel(xa, xb, dest_c)


def _gather_rows(src, idx):
    ch = SC_CHUNK
    m = idx.shape[0]
    w = src.shape[1]
    per = m // SC_WORKERS
    n_pairs = per // (2 * ch)
    assert m % SC_WORKERS == 0 and per % (2 * ch) == 0
    dma = pltpu.SemaphoreType.DMA

    @pl.kernel(out_type=jax.ShapeDtypeStruct((m, w), src.dtype), mesh=_sc_mesh(),
               scratch_types=[pltpu.VMEM((per,), I32), pltpu.VMEM((ch, w), src.dtype), pltpu.VMEM((ch, w), src.dtype),
                              dma, dma, dma, dma])
    def gather_kernel(s_hbm, i_hbm, o_hbm, idx_v, buf_a, buf_b, sem_ga, sem_gb, sem_wa, sem_wb):
        base = _sc_worker_id() * per
        pltpu.sync_copy(i_hbm.at[pl.ds(base, per)], idx_v)

        def fetch(j, buf, sem):
            return pltpu.make_async_copy(s_hbm.at[idx_v.at[pl.ds(j * ch, ch)]], buf, sem)

        def put(j, buf, sem):
            return pltpu.make_async_copy(buf, o_hbm.at[pl.ds(base + j * ch, ch)], sem)

        fetch(0, buf_a, sem_ga).start()

        @pl.loop(0, n_pairs)
        def _(p):
            j0 = 2 * p
            j1 = j0 + 1

            @pl.when(p > 0)
            def _():
                put(j1 - 2, buf_b, sem_wb).wait()

            fetch(j1, buf_b, sem_gb).start()
            fetch(j0, buf_a, sem_ga).wait()
            put(j0, buf_a, sem_wa).start()
            fetch(j1, buf_b, sem_gb).wait()
            put(j1, buf_b, sem_wb).start()
            put(j0, buf_a, sem_wa).wait()

            @pl.when(p + 1 < n_pairs)
            def _():
                fetch(j0 + 2, buf_a, sem_ga).start()

        put(2 * n_pairs - 1, buf_b, sem_wb).wait()

    return gather_kernel(src, idx)


def _moe_kernel(blk_e_ref, blk_valid_ref, blk_next_ref, xs_ref, wgu_hbm, bgu_ref, wd_hbm, bd_ref, y_ref,
                wgu_f32, wd_f32, wgu_bf, wd_bf, sems, *, e0):
    i = pl.program_id(0)
    e = blk_e_ref[i]
    n_valid = blk_valid_ref[i]
    used = n_valid > 0
    new_expert = (i == 0) | (blk_e_ref[jnp.maximum(i - 1, 0)] != e)

    def weight_copies(expert):
        return (pltpu.make_async_copy(wgu_hbm.at[e0 + expert], wgu_f32, sems.at[0]),
                pltpu.make_async_copy(wd_hbm.at[e0 + expert], wd_f32, sems.at[1]))

    @pl.when(used & (i == 0))
    def _():
        for cp in weight_copies(e):
            cp.start()

    @pl.when(used & new_expert)
    def _():
        for cp in weight_copies(e):
            cp.wait()
        wgu_bf[...] = wgu_f32[...].astype(BF16)
        wd_bf[...] = wd_f32[...].astype(BF16)
        nxt = blk_next_ref[i]

        @pl.when(nxt >= 0)
        def _():
            for cp in weight_copies(nxt):
                cp.start(priority=1)

    @pl.when(used)
    def _():
        valid = lax.broadcasted_iota(I32, xs_ref.shape, 0) < n_valid
        xb = _unpack_halves(jnp.where(valid, xs_ref[...], jnp.uint32(0))).astype(BF16)
        gu = jnp.dot(xb, wgu_bf[...], preferred_element_type=F32) + bgu_ref[0]
        g = jnp.minimum(gu[:, :D_FF], SWIGLU_LIMIT)
        up = jnp.clip(gu[:, D_FF:], -SWIGLU_LIMIT, SWIGLU_LIMIT)
        act = (up + 1.0) * (g * jax.nn.sigmoid(SWIGLU_ALPHA * g))
        y_ref[...] = _pack_halves(jnp.dot(act.astype(BF16), wd_bf[...], preferred_element_type=F32) + bd_ref[0])

    @pl.when(jnp.logical_not(used))
    def _():
        y_ref[...] = jnp.zeros_like(y_ref)


def _moe(xs, blk_e, blk_valid, blk_next, layer, w_gu, b_gu, w_d, b_d):
    n_rows = xs.shape[0]
    bm = MOE_BM
    assert n_rows % bm == 0
    e0 = layer * N_EXPERTS
    n_we = w_gu.shape[0] * w_gu.shape[1]
    any_spec = pl.BlockSpec(memory_space=pl.ANY)
    grid_spec = pltpu.PrefetchScalarGridSpec(
        num_scalar_prefetch=3,
        grid=(n_rows // bm,),
        in_specs=[pl.BlockSpec((bm, D_MODEL // 2), lambda i, be, bv, bn: (i, 0)),
                  any_spec,
                  pl.BlockSpec((1, 1, 2 * D_FF), lambda i, be, bv, bn: (e0 + be[i], 0, 0)),
                  any_spec,
                  pl.BlockSpec((1, 1, D_MODEL), lambda i, be, bv, bn: (e0 + be[i], 0, 0))],
        out_specs=pl.BlockSpec((bm, D_MODEL // 2), lambda i, be, bv, bn: (i, 0)),
        scratch_shapes=[pltpu.VMEM((D_MODEL, 2 * D_FF), F32), pltpu.VMEM((D_FF, D_MODEL), F32),
                        pltpu.VMEM((D_MODEL, 2 * D_FF), BF16), pltpu.VMEM((D_FF, D_MODEL), BF16),
                        pltpu.SemaphoreType.DMA((2,))],
    )
    return pl.pallas_call(
        functools.partial(_moe_kernel, e0=e0),
        grid_spec=grid_spec,
        out_shape=jax.ShapeDtypeStruct((n_rows, D_MODEL // 2), U32),
        compiler_params=pltpu.CompilerParams(dimension_semantics=("arbitrary",), vmem_limit_bytes=VMEM_LIMIT),
        name="moe_experts",
    )(blk_e, blk_valid, blk_next, xs, w_gu.reshape(n_we, D_MODEL, 2 * D_FF), b_gu.reshape(n_we, 1, 2 * D_FF),
      w_d.reshape(n_we, D_FF, D_MODEL), b_d.reshape(n_we, 1, D_MODEL))


def _ple_kernel(x1_ref, y0_ref, y1_ref, y2_ref, y3_ref, gates_ref, p_ref, g_ref, wg_ref, wp_ref, *rest):
    out_ref, wg_bf, wp_bf = rest[-3:]

    @pl.when(pl.program_id(0) == 0)
    def _():
        wg_bf[...] = wg_ref[...].astype(BF16)
        wp_bf[...] = wp_ref[...].astype(BF16)

    x2 = x1_ref[...]
    gates = gates_ref[...]
    for kk, y_ref in enumerate((y0_ref, y1_ref, y2_ref, y3_ref)):
        x2 = x2 + _unpack_halves(y_ref[...]) * gates[:, kk:kk + 1]
    hp = _rms(x2, g_ref[...]).astype(BF16)
    gate = jax.nn.sigmoid(jnp.dot(hp, wg_bf[...], preferred_element_type=F32))
    pp = jnp.dot(p_ref[...].astype(BF16), wp_bf[...], preferred_element_type=F32)
    out_ref[...] = x2 + gate * pp


def _ple(x1, tok0, n, y_tok, y0, gates_t, p_all, p0, g_ple, w_gate, w_proj, rows, out_into=None):
    assert n % rows == 0 and tok0 % rows == 0 and y0 % rows == 0 and p0 % rows == 0
    const = lambda shape: pl.BlockSpec(shape, lambda i: (0,) * len(shape))
    tok_blk = lambda width: pl.BlockSpec((rows, width), lambda i: (tok0 // rows + i, 0))
    y_blk = lambda kk: pl.BlockSpec((rows, D_MODEL // 2), lambda i: ((y0 + kk * n) // rows + i, 0))
    extra_specs, extra_args, aliases = [], [], {}
    if out_into is not None:
        assert out_into.shape == x1.shape
        extra_specs, extra_args, aliases = [pl.BlockSpec(memory_space=pl.ANY)], [out_into], {10: 0}
    return pl.pallas_call(
        _ple_kernel,
        grid=(n // rows,),
        in_specs=[tok_blk(D_MODEL), y_blk(0), y_blk(1), y_blk(2), y_blk(3), tok_blk(TOP_K),
                  pl.BlockSpec((rows, PLE_DIM), lambda i: (p0 // rows + i, 0)),
                  const((1, D_MODEL)), const((D_MODEL, D_MODEL)), const((PLE_DIM, D_MODEL))] + extra_specs,
        out_specs=tok_blk(D_MODEL),
        out_shape=jax.ShapeDtypeStruct(x1.shape, F32),
        scratch_shapes=[pltpu.VMEM((D_MODEL, D_MODEL), BF16), pltpu.VMEM((PLE_DIM, D_MODEL), BF16)],
        input_output_aliases=aliases,
        compiler_params=pltpu.CompilerParams(dimension_semantics=("arbitrary",), vmem_limit_bytes=VMEM_LIMIT),
        name="combine_ple",
    )(x1, y_tok, y_tok, y_tok, y_tok, gates_t, p_all, g_ple, w_gate, w_proj, *extra_args)


def _rope_tables(pos):
    half = ROT_DIM // 2
    d = np.arange(LANES) % HEAD_DIM
    inv = ROPE_THETA ** (-jnp.arange(half, dtype=F32) / half)
    inv_lane = jnp.where(d < ROT_DIM, inv[d % half], 0.0)
    ang = pos.astype(F32)[:, None] * inv_lane[None, :]
    cos, sin = jnp.cos(ang), jnp.sin(ang)
    sin_a = jnp.where((d >= half) & (d < ROT_DIM), sin, 0.0)
    sin_b = jnp.where(d < half, -sin, 0.0)
    return cos, sin_a, sin_b


def _layer(layer, xp, xs, ck, cv, st, p_prompt_all, p_sample_all, past_len, refine_tail, tabs_p, tabs_s, finish_state,
           norm_attn, w_in, q_norm, k_norm, attn_sinks, w_pool, pool_scale, w_out,
           norm_ffn, w_router, b_router, w_gate_up_all, b_gate_up_all, w_down_all, b_down_all,
           norm_ple, w_ple_gate, w_ple_proj):
    t, ns = xp.shape[0], xs.shape[0]
    g_attn = norm_attn.reshape(1, D_MODEL)
    qn = jnp.tile(q_norm, 2).reshape(1, LANES)
    kn = jnp.tile(k_norm, 2).reshape(1, LANES)
    lane = np.arange(LANES)
    hmean = jnp.asarray((lane[:, None] // HEAD_DIM == lane[None, :] // HEAD_DIM) / HEAD_DIM, F32)
    pscale = pool_scale.reshape(1, POOL_WIDTH)
    sink8 = jnp.broadcast_to(attn_sinks.reshape(2, Q_TILES).T.reshape(8, 1), (8, LANES))

    mix_args = (g_attn, w_in, qn, kn, hmean, w_pool, pscale, w_out)
    x1p, nk_p, nv_p, nu_p = _mixer_prompt(xp, 0, MIX_ROWS, BF16, attn_sinks, tabs_p, *mix_args)
    if refine_tail:
        x1p, nk_p, nv_p, nu_p = _mixer_prompt(xp, t - 2 * MIX_TAIL_ROWS, MIX_TAIL_ROWS, F32, attn_sinks, tabs_p,
                                              *mix_args, x1_into=x1p)
    st_t = jnp.transpose(st, (1, 0, 2))
    x1s, nk_s, nv_s, nst_t = _mixer_sample(xs, ck.reshape(ns, -1, KV_WIDTH), cv.reshape(ns, -1, KV_WIDTH), st_t,
                                           past_len, tabs_s, sink8, *mix_args)

    g_ffn = norm_ffn.reshape(1, D_MODEL)
    wr_t = w_router.T
    br = b_router.reshape(N_EXPERTS, 1)
    n_tok = t + ns
    bm = MOE_BM
    n_blocks = -(-(n_tok * TOP_K + N_EXPERTS * (bm - 1)) // bm)
    hf_p, idx_p, gate_p, pos_p, cnt_p = _route(x1p, g_ffn, wr_t, br, jnp.zeros((N_EXPERTS, 1), F32),
                                               ROUTE_ROWS, bm, n_blocks)[:5]
    hf_s, idx_s, gate_s, pos_s, _, pad_start, blk_e, blk_valid, blk_next = _route(x1s, g_ffn, wr_t, br, cnt_p, ns,
                                                                                  bm, n_blocks)
    idx = jnp.concatenate([idx_p, idx_s], axis=1)
    pos = jnp.concatenate([pos_p, pos_s], axis=1)
    blk_e, blk_valid, blk_next = (a[0, :n_blocks] for a in (blk_e, blk_valid, blk_next))
    state = finish_state(nk_p, nv_p, nu_p[POOL_HALO - POOL_PREFIX:], nk_s, nv_s, jnp.transpose(nst_t, (1, 0, 2)))
    blk_e, state = lax.optimization_barrier((blk_e, state))
    start_of = jnp.sum(jnp.where(idx[None] == jnp.arange(N_EXPERTS, dtype=I32)[:, None, None],
                                 pad_start.reshape(N_EXPERTS, 1, 1), 0), axis=0)
    dest = start_of + pos

    xs_rows = _scatter_rows(hf_p, hf_s, dest, n_blocks * bm)
    y = _moe(xs_rows, blk_e, blk_valid, blk_next, layer, w_gate_up_all, b_gate_up_all, w_down_all, b_down_all)

    tq = t // COMBINE_PARTS
    unit = SC_WORKERS * SC_CHUNK * 2
    assert t % COMBINE_PARTS == 0 and (TOP_K * tq) % unit == 0 and tq % PLE_ROWS == 0
    g_ple = norm_ple.reshape(1, D_MODEL)
    gates_p = gate_p.T
    yp = None
    for part in range(COMBINE_PARTS):
        back = dest[:, part * tq:(part + 1) * tq].reshape(-1)
        if part == COMBINE_PARTS - 1:
            back = jnp.concatenate([back, dest[:, t:].reshape(-1)])
            back = jnp.concatenate([back, jnp.arange(-back.shape[0] % unit, dtype=I32)])
        y_part = _gather_rows(y, back)
        yp = _ple(x1p, part * tq, tq, y_part, 0, gates_p, p_prompt_all, layer * t + part * tq, g_ple,
                  w_ple_gate, w_ple_proj, PLE_ROWS, out_into=yp)
    ys = _ple(x1s, 0, ns, y_part, TOP_K * tq, gate_s.T, p_sample_all, layer * ns, g_ple, w_ple_gate, w_ple_proj, ns)
    return yp, ys, state


def kernel(x_prompt, x_sample, cache_k, cache_v, state_pool, p_prompt, p_sample, norm_attn, w_in, q_norm, k_norm,
           attn_sinks, w_pool, pool_scale, w_out, norm_ffn, w_router, b_router, w_gate_up, b_gate_up, w_down, b_down,
           norm_ple, w_ple_gate, w_ple_proj):
    depth = norm_attn.shape[0]
    batch, seq, d = x_prompt.shape
    ns, dec_seq, _ = x_sample.shape
    wb = cache_k.shape[2]
    assert batch == 1 and dec_seq == 1 and d == D_MODEL and wb == WINDOW
    assert cache_k.shape[3:] == (N_KV_HEADS, HEAD_DIM) and state_pool.shape[2:] == (POOL_PREFIX, POOL_WIDTH)
    past_len = PAST_LEN
    yp = x_prompt.reshape(seq, d)
    ys = x_sample.reshape(ns, d)
    p_prompt_all = p_prompt.reshape(depth * seq, PLE_DIM)
    p_sample_all = p_sample.reshape(depth * ns, PLE_DIM)
    tabs_p = _rope_tables(jnp.arange(seq))
    tabs_s = _rope_tables(jnp.full((1,), past_len))
    kv_p, kv_s = (1, WINDOW, N_KV_HEADS, HEAD_DIM), (ns, wb, N_KV_HEADS, HEAD_DIM)
    state_shapes = (kv_p, kv_p, (1, POOL_PREFIX, POOL_WIDTH), kv_s, kv_s, (ns, POOL_PREFIX, POOL_WIDTH))
    per_layer = []

    def shaped(*state):
        return tuple(a.reshape(shp) for a, shp in zip(state, state_shapes))

    def stacked(*state):
        prevs = list(zip(*per_layer)) or [()] * len(state_shapes)
        return tuple(jnp.stack(list(prev) + [cur]) for prev, cur in zip(prevs, shaped(*state)))

    for i in range(depth):
        yp, ys, state = _layer(i, yp, ys, cache_k[i], cache_v[i], state_pool[i], p_prompt_all, p_sample_all, past_len,
                               i < depth - 1, tabs_p, tabs_s, shaped if i < depth - 1 else stacked,
                               norm_attn[i], w_in[i], q_norm[i], k_norm[i], attn_sinks[i], w_pool[i], pool_scale[i],
                               w_out[i], norm_ffn[i], w_router[i], b_router[i], w_gate_up, b_gate_up, w_down, b_down,
                               norm_ple[i], w_ple_gate[i], w_ple_proj[i])
        per_layer.append(state)
    return (yp.reshape(batch, seq, d), ys.reshape(ns, dec_seq, d)) + tuple(state)
```

```python
import functools

import jax
import jax.numpy as jnp
import numpy as np
from jax import lax
from jax.experimental import pallas as pl
from jax.experimental.pallas import tpu as pltpu
from jax.experimental.pallas import tpu_sc as plsc

F32 = jnp.float32
BF16 = jnp.bfloat16
U32 = jnp.uint32
I32 = jnp.int32

D_MODEL = 1024
HEAD_DIM = 64
N_HEADS = 8
N_KV_HEADS = 2
GROUP = N_HEADS // N_KV_HEADS
ATTN_WIDTH = N_HEADS * HEAD_DIM
KV_WIDTH = N_KV_HEADS * HEAD_DIM
POOL_WIDTH = 512
POOL_WINDOWS = (2, 4, 8, 16)
POOL_GC = POOL_WIDTH // len(POOL_WINDOWS)
POOL_PREFIX = max(POOL_WINDOWS) - 1
POOL_HALO = POOL_PREFIX + 1
POOL_PAD = 8
IN_WIDTH = ATTN_WIDTH + 2 * KV_WIDTH + POOL_WIDTH
WINDOW = 128
ROPE_THETA = 500000.0
ROT_DIM = HEAD_DIM // 4
N_EXPERTS = 32
TOP_K = 4
D_FF = 1024
SWIGLU_ALPHA = 1.702
SWIGLU_LIMIT = 7.0
PLE_DIM = 256
PAST_LEN = 16384
EPS = 1e-5
NEG_INF = -1e30

LANES = 128
Q_TILES = ATTN_WIDTH // LANES

MIX_ROWS = 512
MIX_TAIL_ROWS = 128
SAMPLE_CHUNK = 16
ROUTE_ROWS = 512
MOE_BM = 512
SC_CORES = 2
SC_SUBCORES = 16
SC_WORKERS = SC_CORES * SC_SUBCORES
SC_CHUNK = 64
PLE_ROWS = 512
COMBINE_PARTS = 2
VMEM_LIMIT = 56 * 1024 * 1024


def _rms(x, g):
    return x * lax.rsqrt(jnp.mean(x * x, axis=-1, keepdims=True) + EPS) * g


def _pack_halves(x):
    w = x.shape[1] // 2
    lo = lax.bitcast_convert_type(x[:, :w].astype(BF16).astype(F32), U32) >> 16
    hi = lax.bitcast_convert_type(x[:, w:].astype(BF16).astype(F32), U32) & jnp.uint32(0xFFFF0000)
    return lo | hi


def _unpack_halves(packed):
    lo = lax.bitcast_convert_type(packed << 16, F32)
    hi = lax.bitcast_convert_type(packed & jnp.uint32(0xFFFF0000), F32)
    return jnp.concatenate([lo, hi], axis=1)


def _mm(a, b, nt=False):
    dims = (((1,), (1 if nt else 0,)), ((), ()))
    if b.dtype == F32:
        return lax.dot_general(a.astype(F32), b, dims, preferred_element_type=F32, precision=lax.Precision.HIGHEST)
    return lax.dot_general(a.astype(BF16), b, dims, preferred_element_type=F32)


def _head_norm_rope(t, hmean, gain, cos, sin_a, sin_b):
    t = t * lax.rsqrt(_mm(t * t, hmean) + EPS) * gain
    return t * cos + pltpu.roll(t, ROT_DIM // 2, axis=1) * sin_a + pltpu.roll(t, LANES - ROT_DIM // 2, axis=1) * sin_b


def _softmax_pv(s, sink, v, ones):
    m = jnp.maximum(jnp.max(s, axis=-1, keepdims=True), sink)
    e = jnp.exp(s - m).astype(v.dtype)
    den = _mm(e, ones) + jnp.exp(sink - m)
    return _mm(e, v) / den


def _prep_mixer_weights(win_ref, wout_ref, win_c, wout_c):
    cdt = win_c.dtype
    for j in range(Q_TILES):
        for s in range(2):
            src = (j + Q_TILES * s) * HEAD_DIM
            dst = j * LANES + s * HEAD_DIM
            win_c[:, dst:dst + HEAD_DIM] = win_ref[:, src:src + HEAD_DIM].astype(cdt)
            wout_c[dst:dst + HEAD_DIM, :] = wout_ref[src:src + HEAD_DIM, :].astype(cdt)
    win_c[:, ATTN_WIDTH:] = win_ref[:, ATTN_WIDTH:].astype(cdt)
    wout_c[ATTN_WIDTH:, :] = wout_ref[ATTN_WIDTH:, :].astype(cdt)


def _mixer_prompt_kernel(sinks_ref, x_ref, cs_ref, ecos_ref, esa_ref, esb_ref, cbias_ref, g_ref, win_ref, qn_ref, kn_ref,
                         hm_ref, wpool_ref, pscale_ref, wout_ref, *rest, row_offset, aliased):
    (x1_ref, klast_ref, vlast_ref, ulast_ref, kprev, vprev, uext, mix, win_c, wout_c,
     *lvl) = rest[1:] if aliased else rest
    i = pl.program_id(0)
    rows = x_ref.shape[0]
    n_sub = rows // WINDOW
    cdt = win_c.dtype
    row0 = row_offset + i * rows

    @pl.when(i == 0)
    def _():
        _prep_mixer_weights(win_ref, wout_ref, win_c, wout_c)
        kprev[...] = jnp.zeros_like(kprev)
        vprev[...] = jnp.zeros_like(vprev)
        uext[0:POOL_PAD + POOL_HALO, :] = jnp.zeros((POOL_PAD + POOL_HALO, POOL_WIDTH), F32)
        for buf in lvl:
            buf[0:POOL_PAD, :] = jnp.zeros((POOL_PAD, POOL_GC), F32)

    x = x_ref[...]
    proj = _mm(_rms(x, g_ref[...]), win_c[...])
    cs = cs_ref[...]
    cos = _mm(cs, ecos_ref[...]) + cbias_ref[...]
    sin_a = _mm(cs, esa_ref[...])
    sin_b = _mm(cs, esb_ref[...])

    n_t = Q_TILES + 1
    t_all = jnp.concatenate([proj[:, j * LANES:(j + 1) * LANES] for j in range(n_t)], axis=0)
    t3 = (t_all * lax.rsqrt(_mm(t_all * t_all, hm_ref[...].astype(cdt)) + EPS)).reshape(n_t, rows, LANES)
    t3 = jnp.concatenate([t3[:Q_TILES] * (qn_ref[...] * HEAD_DIM ** -0.5), t3[Q_TILES:] * kn_ref[...]], axis=0)
    t2 = t3.reshape(n_t * rows, LANES)
    t3 = (t3 * cos + pltpu.roll(t2, ROT_DIM // 2, axis=1).reshape(n_t, rows, LANES) * sin_a
          + pltpu.roll(t2, LANES - ROT_DIM // 2, axis=1).reshape(n_t, rows, LANES) * sin_b)
    q3 = t3[:Q_TILES]
    k = t3[Q_TILES]
    v = proj[:, ATTN_WIDTH + KV_WIDTH:ATTN_WIDTH + 2 * KV_WIDTH]
    u = proj[:, ATTN_WIDTH + 2 * KV_WIDTH:]
    klast_ref[...] = k[rows - WINDOW:, :]
    vlast_ref[...] = v[rows - WINDOW:, :]
    ulast_ref[...] = u[rows - POOL_HALO:, :]
    k_c = k.astype(cdt)
    v_c = jnp.concatenate([v.astype(cdt), jnp.ones((rows, LANES), cdt)], axis=1)
    v_first = jnp.concatenate([vprev[...], jnp.ones((WINDOW, LANES), cdt)], axis=1)

    lane = lax.broadcasted_iota(I32, (WINDOW, LANES), 1)
    left = (lane < HEAD_DIM)[None]
    qi = lax.broadcasted_iota(I32, (WINDOW, 2 * WINDOW), 0)
    kj = lax.broadcasted_iota(I32, (WINDOW, 2 * WINDOW), 1)
    band = (kj - qi >= 1) & (kj - qi <= WINDOW)
    sink3 = jnp.concatenate([jnp.full((1, 1, 1), sinks_ref[j + Q_TILES * s], F32)
                             for j in range(Q_TILES) for s in range(2)], axis=0)
    n_g = 2 * Q_TILES

    v_cats, masks, scores = [], [], []
    for c in range(n_sub):
        r0 = c * WINDOW
        if c == 0:
            k_cat = jnp.concatenate([kprev[...], k_c[0:WINDOW]], axis=0)
            v_cats.append(jnp.concatenate([v_first, v_c[0:WINDOW]], axis=0))
            masks.append(band & (kj + (row0 - WINDOW) >= 0))
        else:
            k_cat = k_c[r0 - WINDOW:r0 + WINDOW]
            v_cats.append(v_c[r0 - WINDOW:r0 + WINDOW])
            masks.append(band)
        q_c = q3[:, r0:r0 + WINDOW, :]
        q_all = jnp.concatenate([jnp.where(left, q_c, 0.0), jnp.where(left, 0.0, q_c)], axis=1)
        scores.append(_mm(q_all.reshape(n_g * WINDOW, LANES), k_cat, nt=True))
    kprev[...] = k_c[rows - WINDOW:]
    vprev[...] = v[rows - WINDOW:].astype(cdt)

    probs, maxes = [], []
    for c in range(n_sub):
        s = jnp.where(masks[c][None], scores[c].reshape(n_g, WINDOW, 2 * WINDOW), NEG_INF)
        m = jnp.maximum(jnp.max(s, axis=-1, keepdims=True), sink3)
        probs.append(jnp.exp(s - m).astype(cdt).reshape(n_g * WINDOW, 2 * WINDOW))
        maxes.append(m)
    applied = [_mm(probs[c], v_cats[c]) for c in range(n_sub)]
    for c in range(n_sub):
        r0 = c * WINDOW
        den = applied[c][:, LANES:] + jnp.exp(sink3 - maxes[c]).reshape(n_g * WINDOW, 1)
        o = (applied[c][:, :LANES] / den).reshape(Q_TILES, 2 * WINDOW, LANES)
        a = jnp.where(left, o[:, :WINDOW], o[:, WINDOW:])
        for j in range(Q_TILES):
            mix[r0:r0 + WINDOW, j * LANES:(j + 1) * LANES] = a[j].astype(cdt)

    base = POOL_PAD + POOL_HALO
    ext = base + rows
    uext[base:ext, :] = u
    pos1 = (lax.broadcasted_iota(I32, (rows, 1), 0) + row0 + 1).astype(F32)
    lvl_of = {1: lvl[0:1], 2: lvl[1:3], 3: lvl[3:5]}
    for gi, w in enumerate(POOL_WINDOWS):
        cols = slice(gi * POOL_GC, (gi + 1) * POOL_GC)
        src, src_cols = uext, cols
        for level in range(1, gi + 2):
            sft = 1 << (level - 1)
            if level <= gi:
                dst = lvl_of[gi][(level - 1) % 2]
                dst[POOL_PAD:ext, :] = src[POOL_PAD:ext, src_cols] + src[POOL_PAD - sft:ext - sft, src_cols]
                src, src_cols = dst, slice(None)
            else:
                wsum = src[base:ext, src_cols] + src[base - sft:ext - sft, src_cols]
        d = wsum / jnp.minimum(pos1, float(w)) - u[:, cols]
        y = _mm(d, wpool_ref[gi].astype(cdt)) * pscale_ref[:, cols]
        mix[:, ATTN_WIDTH + gi * POOL_GC:ATTN_WIDTH + (gi + 1) * POOL_GC] = y.astype(cdt)
    uext[POOL_PAD:base, :] = u[rows - POOL_HALO:, :]

    x1_ref[...] = x + _mm(mix[...], wout_c[...])


def _mixer_prompt(x_full, row_offset, rows, cdt, sinks, tabs, g_attn, w_in, qn, kn, hmean, w_pool, pscale, w_out,
                  x1_into=None):
    t = x_full.shape[0] - row_offset
    assert t % rows == 0 and row_offset % rows == 0 and rows % WINDOW == 0 and rows >= POOL_HALO
    blk0, n_steps = row_offset // rows, t // rows
    const = lambda shape: pl.BlockSpec(shape, lambda i, *_: (0,) * len(shape))
    row_blk = lambda width: pl.BlockSpec((rows, width), lambda i, *_: (blk0 + i, 0))
    aliased = x1_into is not None
    if aliased:
        assert x1_into.shape == x_full.shape
        x1_spec = pl.BlockSpec((rows, D_MODEL), lambda i, *_: (blk0 + n_steps - 1, 0))
        x1_shape = x1_into.shape
        extra_specs, extra_args, aliases = [pl.BlockSpec(memory_space=pl.ANY)], [x1_into], {15: 0}
    else:
        x1_spec = pl.BlockSpec((rows, D_MODEL), lambda i, *_: (i, 0))
        x1_shape = (t, D_MODEL)
        extra_specs, extra_args, aliases = [], [], {}
    grid_spec = pltpu.PrefetchScalarGridSpec(
        num_scalar_prefetch=1,
        grid=(n_steps,),
        in_specs=[row_blk(D_MODEL), row_blk(2 * (ROT_DIM // 2)), const((ROT_DIM, LANES)), const((ROT_DIM, LANES)),
                  const((ROT_DIM, LANES)), const((1, LANES)),
                  const((1, D_MODEL)), const((D_MODEL, IN_WIDTH)), const((1, LANES)), const((1, LANES)),
                  const((LANES, LANES)), const((len(POOL_WINDOWS), POOL_GC, POOL_GC)), const((1, POOL_WIDTH)),
                  const((D_MODEL, D_MODEL))] + extra_specs,
        out_specs=[x1_spec, const((WINDOW, KV_WIDTH)), const((WINDOW, KV_WIDTH)),
                   const((POOL_HALO, POOL_WIDTH))],
        scratch_shapes=[pltpu.VMEM((WINDOW, KV_WIDTH), cdt), pltpu.VMEM((WINDOW, KV_WIDTH), cdt),
                        pltpu.VMEM((POOL_PAD + POOL_HALO + rows, POOL_WIDTH), F32), pltpu.VMEM((rows, D_MODEL), cdt),
                        pltpu.VMEM((D_MODEL, IN_WIDTH), cdt), pltpu.VMEM((D_MODEL, D_MODEL), cdt)]
        + [pltpu.VMEM((POOL_PAD + POOL_HALO + rows, POOL_GC), F32)] * 5,
    )
    return pl.pallas_call(
        functools.partial(_mixer_prompt_kernel, row_offset=row_offset, aliased=aliased),
        grid_spec=grid_spec,
        out_shape=[jax.ShapeDtypeStruct(x1_shape, F32), jax.ShapeDtypeStruct((WINDOW, KV_WIDTH), F32),
                   jax.ShapeDtypeStruct((WINDOW, KV_WIDTH), F32), jax.ShapeDtypeStruct((POOL_HALO, POOL_WIDTH), F32)],
        input_output_aliases=aliases,
        compiler_params=pltpu.CompilerParams(dimension_semantics=("arbitrary",), vmem_limit_bytes=VMEM_LIMIT),
        name="mixer_prompt",
    )(sinks, x_full, *tabs, g_attn, w_in, qn, kn, hmean, w_pool, pscale, w_out, *extra_args)


def _mixer_sample_kernel(x_ref, ck_ref, cv_ref, st_ref, cos_ref, sa_ref, sb_ref, sink8_ref, g_ref, win_ref, qn_ref,
                         kn_ref, hm_ref, wpool_ref, pscale_ref, wout_ref,
                         x1_ref, nk_ref, nv_ref, nst_ref, o8, win_c, wout_c, *, pos):
    nb = x_ref.shape[0]
    wb = ck_ref.shape[1]

    @pl.when(pl.program_id(0) == 0)
    def _():
        _prep_mixer_weights(win_ref, wout_ref, win_c, wout_c)

    x = x_ref[...]
    h = _rms(x, g_ref[...]).astype(BF16)
    proj = jnp.dot(h, win_c[...], preferred_element_type=F32)
    cos, sin_a, sin_b = cos_ref[...], sa_ref[...], sb_ref[...]
    hmean = hm_ref[...].astype(BF16)
    k = _head_norm_rope(proj[:, ATTN_WIDTH:ATTN_WIDTH + KV_WIDTH], hmean, kn_ref[...], cos, sin_a, sin_b)
    v = proj[:, ATTN_WIDTH + KV_WIDTH:ATTN_WIDTH + 2 * KV_WIDTH]
    u = proj[:, ATTN_WIDTH + 2 * KV_WIDTH:]

    nk_ref[:, 0:wb - 1, :] = ck_ref[:, 1:wb, :]
    nv_ref[:, 0:wb - 1, :] = cv_ref[:, 1:wb, :]
    for b in range(nb):
        nk_ref[b, wb - 1:wb, :] = k[b:b + 1, :]
        nv_ref[b, wb - 1:wb, :] = v[b:b + 1, :]

    r8 = lax.broadcasted_iota(I32, (nb * 8, LANES), 0)
    lane8 = lax.broadcasted_iota(I32, (nb * 8, LANES), 1)
    keep = (lane8 < HEAD_DIM) == (r8 % 2 == 0)
    rep = (lax.broadcasted_iota(I32, (nb * 8, nb), 0) // 8 == lax.broadcasted_iota(I32, (nb * 8, nb), 1)).astype(BF16)
    q8 = jnp.zeros((nb * 8, LANES), F32)
    scale = HEAD_DIM ** -0.5
    for j in range(Q_TILES):
        qt = _head_norm_rope(proj[:, j * LANES:(j + 1) * LANES], hmean, qn_ref[...], cos, sin_a, sin_b) * scale
        qrep = jnp.dot(rep, qt.astype(BF16), preferred_element_type=F32)
        q8 = jnp.where(keep & ((r8 % 8) // 2 == j), qrep, q8)
    q8 = q8.astype(BF16)

    sink8 = sink8_ref[:, 0:1]
    ones_bf = jnp.ones((wb, LANES), BF16)
    assert pos >= wb - 1 and wb <= WINDOW
    for b in range(nb):
        kb = nk_ref[b].astype(BF16)
        vb = nv_ref[b].astype(BF16)
        s = lax.dot_general(q8[b * 8:(b + 1) * 8], kb, (((1,), (1,)), ((), ())), preferred_element_type=F32)
        o8[b * 8:(b + 1) * 8, :] = _softmax_pv(s, sink8, vb, ones_bf)
    o8m = jnp.where(keep, o8[...], 0.0).astype(BF16)

    a_tiles = []
    sel_r = lax.broadcasted_iota(I32, (nb, nb * 8), 1)
    sel_b = lax.broadcasted_iota(I32, (nb, nb * 8), 0)
    for j in range(Q_TILES):
        sel = ((sel_r // 8 == sel_b) & ((sel_r % 8) // 2 == j)).astype(BF16)
        a_tiles.append(jnp.dot(sel, o8m, preferred_element_type=F32))

    z_tiles = []
    for gi, w in enumerate(POOL_WINDOWS):
        cols = slice(gi * POOL_GC, (gi + 1) * POOL_GC)
        wsum = u[:, cols]
        for sft in range(1, w):
            wsum = wsum + st_ref[POOL_PREFIX - sft, :, cols]
        d = wsum / float(min(pos + 1, w)) - u[:, cols]
        z_tiles.append(jnp.dot(d.astype(BF16), wpool_ref[gi].astype(BF16), preferred_element_type=F32)
                       * pscale_ref[:, cols])
    nst_ref[0:POOL_PREFIX - 1] = st_ref[1:POOL_PREFIX]
    nst_ref[POOL_PREFIX - 1] = u

    mixv = jnp.concatenate(a_tiles + z_tiles, axis=1).astype(BF16)
    x1_ref[...] = x + jnp.dot(mixv, wout_c[...], preferred_element_type=F32)


def _mixer_sample(x, ck, cv, st, pos, tabs, sink8, g_attn, w_in, qn, kn, hmean, w_pool, pscale, w_out):
    n, wb = ck.shape[0], ck.shape[1]
    nb = SAMPLE_CHUNK
    assert n % nb == 0
    const = lambda shape: pl.BlockSpec(shape, lambda i: (0,) * len(shape))
    cache_blk = pl.BlockSpec((nb, wb, KV_WIDTH), lambda i: (i, 0, 0))
    st_blk = pl.BlockSpec((POOL_PREFIX, nb, POOL_WIDTH), lambda i: (0, i, 0))
    x_blk = pl.BlockSpec((nb, D_MODEL), lambda i: (i, 0))
    return pl.pallas_call(
        functools.partial(_mixer_sample_kernel, pos=pos),
        grid=(n // nb,),
        in_specs=[x_blk, cache_blk, cache_blk, st_blk, const((1, LANES)), const((1, LANES)), const((1, LANES)),
                  const((8, LANES)), const((1, D_MODEL)), const((D_MODEL, IN_WIDTH)), const((1, LANES)),
                  const((1, LANES)), const((LANES, LANES)), const((len(POOL_WINDOWS), POOL_GC, POOL_GC)),
                  const((1, POOL_WIDTH)), const((D_MODEL, D_MODEL))],
        out_specs=[x_blk, cache_blk, cache_blk, st_blk],
        out_shape=[jax.ShapeDtypeStruct((n, D_MODEL), F32), jax.ShapeDtypeStruct(ck.shape, F32),
                   jax.ShapeDtypeStruct(cv.shape, F32), jax.ShapeDtypeStruct(st.shape, F32)],
        scratch_shapes=[pltpu.VMEM((nb * 8, LANES), F32), pltpu.VMEM((D_MODEL, IN_WIDTH), BF16),
                        pltpu.VMEM((D_MODEL, D_MODEL), BF16)],
        compiler_params=pltpu.CompilerParams(dimension_semantics=("arbitrary",), vmem_limit_bytes=VMEM_LIMIT),
        name="mixer_sample",
    )(x, ck, cv, st, *tabs, sink8, g_attn, w_in, qn, kn, hmean, w_pool, pscale, w_out)


def _block_plan(cnt, bm, n_lanes):
    e_sub = lax.broadcasted_iota(I32, (N_EXPERTS, LANES), 0)
    e_lane = lax.broadcasted_iota(I32, (N_EXPERTS, LANES), 1)
    padded = jnp.floor((cnt + (bm - 1)) / bm) * bm
    padded_lane = jnp.sum(jnp.where(e_sub == e_lane, padded, 0.0), axis=0, keepdims=True)
    pad_end = jnp.sum(jnp.where(e_lane <= e_sub, padded_lane, 0.0), axis=1, keepdims=True)
    pad_start = pad_end - padded
    blk_start = lax.broadcasted_iota(I32, (N_EXPERTS, n_lanes), 1).astype(F32) * bm
    blk_e = jnp.minimum(jnp.sum((pad_end <= blk_start).astype(F32), axis=0, keepdims=True), N_EXPERTS - 1.0)
    mine = lax.broadcasted_iota(I32, (N_EXPERTS, n_lanes), 0).astype(F32) == blk_e
    last = jnp.sum(jnp.where(mine, pad_start + cnt, 0.0), axis=0, keepdims=True)
    blk_valid = jnp.clip(last - blk_start[0:1], 0.0, float(bm))
    e_blk = lax.broadcasted_iota(I32, (N_EXPERTS, n_lanes), 0).astype(F32)
    later = jnp.min(jnp.where((e_blk > blk_e) & (cnt > 0.0), e_blk, float(N_EXPERTS)), axis=0, keepdims=True)
    blk_next = jnp.where(later < N_EXPERTS, later, -1.0)
    return pad_start, blk_e, blk_valid, blk_next


def _route_kernel(x1_ref, g_ref, wr_ref, br_ref, cnt_in_ref, hf_ref, idx_ref, gate_ref, pos_ref, cnt_ref,
                  pstart_ref, blke_ref, blkv_ref, blkn_ref, counts, *, bm):
    i = pl.program_id(0)
    rows = x1_ref.shape[0]

    @pl.when(i == 0)
    def _():
        counts[...] = cnt_in_ref[...]

    h = _rms(x1_ref[...], g_ref[...])
    hf_ref[...] = _pack_halves(h)

    logits = lax.dot_general(wr_ref[...], h, (((1,), (1,)), ((), ())), preferred_element_type=F32,
                             precision=lax.Precision.HIGHEST) + br_ref[...]
    eid = lax.broadcasted_iota(I32, (N_EXPERTS, rows), 0).astype(F32)
    work = logits
    vals, hots = [], []
    for kk in range(TOP_K):
        m = jnp.max(work, axis=0, keepdims=True)
        first = jnp.min(jnp.where(work == m, eid, float(N_EXPERTS)), axis=0, keepdims=True)
        hot = eid == first
        work = jnp.where(hot, -jnp.inf, work)
        vals.append(m)
        hots.append(hot)
        idx_ref[kk:kk + 1, :] = first.astype(I32)
    es = [jnp.exp(vv - vals[0]) for vv in vals]
    den = es[0] + es[1] + es[2] + es[3]
    for kk in range(TOP_K):
        gate_ref[kk:kk + 1, :] = es[kk] / den

    chosen = hots[0] | hots[1] | hots[2] | hots[3]
    before = (lax.broadcasted_iota(I32, (rows, rows), 0) < lax.broadcasted_iota(I32, (rows, rows), 1)).astype(BF16)
    rank = jnp.dot(chosen.astype(BF16), before, preferred_element_type=F32) + counts[...]
    for kk in range(TOP_K):
        pos_ref[kk:kk + 1, :] = jnp.sum(jnp.where(hots[kk], rank, 0.0), axis=0, keepdims=True).astype(I32)
    counts[...] = counts[...] + jnp.sum(chosen.astype(F32), axis=1, keepdims=True)
    cnt_ref[...] = counts[...]

    @pl.when(i == pl.num_programs(0) - 1)
    def _():
        pad_start, blk_e, blk_valid, blk_next = _block_plan(counts[...], bm, blke_ref.shape[1])
        pstart_ref[...] = pad_start.astype(I32)
        blke_ref[...] = blk_e.astype(I32)
        blkv_ref[...] = blk_valid.astype(I32)
        blkn_ref[...] = blk_next.astype(I32)


def _route(x1, g_ffn, wr_t, br, cnt_in, rows, bm, n_blocks):
    n = x1.shape[0]
    assert n % rows == 0
    nb_lanes = -(-n_blocks // LANES) * LANES
    const = lambda shape: pl.BlockSpec(shape, lambda i: (0,) * len(shape))
    tok_blk = pl.BlockSpec((TOP_K, rows), lambda i: (0, i))
    return pl.pallas_call(
        functools.partial(_route_kernel, bm=bm),
        grid=(n // rows,),
        in_specs=[pl.BlockSpec((rows, D_MODEL), lambda i: (i, 0)), const((1, D_MODEL)), const((N_EXPERTS, D_MODEL)),
                  const((N_EXPERTS, 1)), const((N_EXPERTS, 1))],
        out_specs=[pl.BlockSpec((rows, D_MODEL // 2), lambda i: (i, 0)), tok_blk, tok_blk, tok_blk,
                   const((N_EXPERTS, 1)), const((N_EXPERTS, 1)), const((1, nb_lanes)), const((1, nb_lanes)),
                   const((1, nb_lanes))],
        out_shape=[jax.ShapeDtypeStruct((n, D_MODEL // 2), U32), jax.ShapeDtypeStruct((TOP_K, n), I32),
                   jax.ShapeDtypeStruct((TOP_K, n), F32), jax.ShapeDtypeStruct((TOP_K, n), I32),
                   jax.ShapeDtypeStruct((N_EXPERTS, 1), F32), jax.ShapeDtypeStruct((N_EXPERTS, 1), I32),
                   jax.ShapeDtypeStruct((1, nb_lanes), I32), jax.ShapeDtypeStruct((1, nb_lanes), I32),
                   jax.ShapeDtypeStruct((1, nb_lanes), I32)],
        scratch_shapes=[pltpu.VMEM((N_EXPERTS, 1), F32)],
        compiler_params=pltpu.CompilerParams(dimension_semantics=("arbitrary",), vmem_limit_bytes=VMEM_LIMIT),
        name="route",
    )(x1, g_ffn, wr_t, br, cnt_in)


def _sc_mesh():
    return plsc.VectorSubcoreMesh(core_axis_name="core", subcore_axis_name="subcore")


def _sc_worker_id():
    return lax.axis_index("core") * SC_SUBCORES + lax.axis_index("subcore")


def _scatter_rows(xa, xb, dest, n_rows):
    ch = SC_CHUNK
    na, w = xa.shape
    n = na + xb.shape[0]
    nk = dest.shape[0]
    assert na % ch == 0 and n % ch == 0 and dest.shape[1] == n and xb.shape[1] == w and xa.dtype == xb.dtype
    n_chunks = n // ch
    dest_c = dest.reshape(nk, n_chunks, ch).transpose(1, 0, 2).reshape(-1)
    dma = pltpu.SemaphoreType.DMA

    @pl.kernel(out_type=jax.ShapeDtypeStruct((n_rows, w), xa.dtype), mesh=_sc_mesh(),
               scratch_types=[pltpu.VMEM((nk * ch,), I32), pltpu.VMEM((ch, w), xa.dtype), dma] + [dma] * nk)
    def scatter_kernel(xa_hbm, xb_hbm, d_hbm, o_hbm, idx_v, buf, sem_i, *sem_s):
        wid = _sc_worker_id()

        @pl.loop(0, -(-n_chunks // SC_WORKERS))
        def _(j):
            c = j * SC_WORKERS + wid

            @pl.when(c < n_chunks)
            def _():
                load_idx = pltpu.make_async_copy(d_hbm.at[pl.ds(c * (nk * ch), nk * ch)], idx_v, sem_i)
                load_idx.start()

                @pl.when(c < na // ch)
                def _():
                    pltpu.sync_copy(xa_hbm.at[pl.ds(c * ch, ch)], buf)

                @pl.when(c >= na // ch)
                def _():
                    pltpu.sync_copy(xb_hbm.at[pl.ds(c * ch - na, ch)], buf)

                load_idx.wait()
                puts = [pltpu.make_async_copy(buf, o_hbm.at[idx_v.at[pl.ds(kk * ch, ch)]], sem_s[kk])
                        for kk in range(nk)]
                for cp in puts:
                    cp.start()
                for cp in puts:
                    cp.wait()

    return scatter_kernel(xa, xb, dest_c)


def _gather_rows(src, idx):
    ch = SC_CHUNK
    m = idx.shape[0]
    w = src.shape[1]
    per = m // SC_WORKERS
    n_pairs = per // (2 * ch)
    assert m % SC_WORKERS == 0 and per % (2 * ch) == 0
    dma = pltpu.SemaphoreType.DMA

    @pl.kernel(out_type=jax.ShapeDtypeStruct((m, w), src.dtype), mesh=_sc_mesh(),
               scratch_types=[pltpu.VMEM((per,), I32), pltpu.VMEM((ch, w), src.dtype), pltpu.VMEM((ch, w), src.dtype),
                              dma, dma, dma, dma])
    def gather_kernel(s_hbm, i_hbm, o_hbm, idx_v, buf_a, buf_b, sem_ga, sem_gb, sem_wa, sem_wb):
        base = _sc_worker_id() * per
        pltpu.sync_copy(i_hbm.at[pl.ds(base, per)], idx_v)

        def fetch(j, buf, sem):
            return pltpu.make_async_copy(s_hbm.at[idx_v.at[pl.ds(j * ch, ch)]], buf, sem)

        def put(j, buf, sem):
            return pltpu.make_async_copy(buf, o_hbm.at[pl.ds(base + j * ch, ch)], sem)

        fetch(0, buf_a, sem_ga).start()

        @pl.loop(0, n_pairs)
        def _(p):
            j0 = 2 * p
            j1 = j0 + 1

            @pl.when(p > 0)
            def _():
                put(j1 - 2, buf_b, sem_wb).wait()

            fetch(j1, buf_b, sem_gb).start()
            fetch(j0, buf_a, sem_ga).wait()
            put(j0, buf_a, sem_wa).start()
            fetch(j1, buf_b, sem_gb).wait()
            put(j1, buf_b, sem_wb).start()
            put(j0, buf_a, sem_wa).wait()

            @pl.when(p + 1 < n_pairs)
            def _():
                fetch(j0 + 2, buf_a, sem_ga).start()

        put(2 * n_pairs - 1, buf_b, sem_wb).wait()

    return gather_kernel(src, idx)


def _moe_kernel(blk_e_ref, blk_valid_ref, blk_next_ref, xs_ref, wgu_hbm, bgu_ref, wd_hbm, bd_ref, y_ref,
                wgu_f32, wd_f32, wgu_bf, wd_bf, sems, *, e0):
    i = pl.program_id(0)
    e = blk_e_ref[i]
    n_valid = blk_valid_ref[i]
    used = n_valid > 0
    new_expert = (i == 0) | (blk_e_ref[jnp.maximum(i - 1, 0)] != e)

    def weight_copies(expert):
        return (pltpu.make_async_copy(wgu_hbm.at[e0 + expert], wgu_f32, sems.at[0]),
                pltpu.make_async_copy(wd_hbm.at[e0 + expert], wd_f32, sems.at[1]))

    @pl.when(used & (i == 0))
    def _():
        for cp in weight_copies(e):
            cp.start()

    @pl.when(used & new_expert)
    def _():
        for cp in weight_copies(e):
            cp.wait()
        wgu_bf[...] = wgu_f32[...].astype(BF16)
        wd_bf[...] = wd_f32[...].astype(BF16)
        nxt = blk_next_ref[i]

        @pl.when(nxt >= 0)
        def _():
            for cp in weight_copies(nxt):
                cp.start(priority=1)

    @pl.when(used)
    def _():
        valid = lax.broadcasted_iota(I32, xs_ref.shape, 0) < n_valid
        xb = _unpack_halves(jnp.where(valid, xs_ref[...], jnp.uint32(0))).astype(BF16)
        gu = jnp.dot(xb, wgu_bf[...], preferred_element_type=F32) + bgu_ref[0]
        g = jnp.minimum(gu[:, :D_FF], SWIGLU_LIMIT)
        up = jnp.clip(gu[:, D_FF:], -SWIGLU_LIMIT, SWIGLU_LIMIT)
        act = (up + 1.0) * (g * jax.nn.sigmoid(SWIGLU_ALPHA * g))
        y_ref[...] = _pack_halves(jnp.dot(act.astype(BF16), wd_bf[...], preferred_element_type=F32) + bd_ref[0])

    @pl.when(jnp.logical_not(used))
    def _():
        y_ref[...] = jnp.zeros_like(y_ref)


def _moe(xs, blk_e, blk_valid, blk_next, layer, w_gu, b_gu, w_d, b_d):
    n_rows = xs.shape[0]
    bm = MOE_BM
    assert n_rows % bm == 0
    e0 = layer * N_EXPERTS
    n_we = w_gu.shape[0] * w_gu.shape[1]
    any_spec = pl.BlockSpec(memory_space=pl.ANY)
    grid_spec = pltpu.PrefetchScalarGridSpec(
        num_scalar_prefetch=3,
        grid=(n_rows // bm,),
        in_specs=[pl.BlockSpec((bm, D_MODEL // 2), lambda i, be, bv, bn: (i, 0)),
                  any_spec,
                  pl.BlockSpec((1, 1, 2 * D_FF), lambda i, be, bv, bn: (e0 + be[i], 0, 0)),
                  any_spec,
                  pl.BlockSpec((1, 1, D_MODEL), lambda i, be, bv, bn: (e0 + be[i], 0, 0))],
        out_specs=pl.BlockSpec((bm, D_MODEL // 2), lambda i, be, bv, bn: (i, 0)),
        scratch_shapes=[pltpu.VMEM((D_MODEL, 2 * D_FF), F32), pltpu.VMEM((D_FF, D_MODEL), F32),
                        pltpu.VMEM((D_MODEL, 2 * D_FF), BF16), pltpu.VMEM((D_FF, D_MODEL), BF16),
                        pltpu.SemaphoreType.DMA((2,))],
    )
    return pl.pallas_call(
        functools.partial(_moe_kernel, e0=e0),
        grid_spec=grid_spec,
        out_shape=jax.ShapeDtypeStruct((n_rows, D_MODEL // 2), U32),
        compiler_params=pltpu.CompilerParams(dimension_semantics=("arbitrary",), vmem_limit_bytes=VMEM_LIMIT),
        name="moe_experts",
    )(blk_e, blk_valid, blk_next, xs, w_gu.reshape(n_we, D_MODEL, 2 * D_FF), b_gu.reshape(n_we, 1, 2 * D_FF),
      w_d.reshape(n_we, D_FF, D_MODEL), b_d.reshape(n_we, 1, D_MODEL))


def _ple_kernel(x1_ref, y0_ref, y1_ref, y2_ref, y3_ref, gates_ref, p_ref, g_ref, wg_ref, wp_ref, *rest):
    out_ref, wg_bf, wp_bf = rest[-3:]

    @pl.when(pl.program_id(0) == 0)
    def _():
        wg_bf[...] = wg_ref[...].astype(BF16)
        wp_bf[...] = wp_ref[...].astype(BF16)

    x2 = x1_ref[...]
    gates = gates_ref[...]
    for kk, y_ref in enumerate((y0_ref, y1_ref, y2_ref, y3_ref)):
        x2 = x2 + _unpack_halves(y_ref[...]) * gates[:, kk:kk + 1]
    hp = _rms(x2, g_ref[...]).astype(BF16)
    gate = jax.nn.sigmoid(jnp.dot(hp, wg_bf[...], preferred_element_type=F32))
    pp = jnp.dot(p_ref[...].astype(BF16), wp_bf[...], preferred_element_type=F32)
    out_ref[...] = x2 + gate * pp


def _ple(x1, tok0, n, y_tok, y0, gates_t, p_all, p0, g_ple, w_gate, w_proj, rows, out_into=None):
    assert n % rows == 0 and tok0 % rows == 0 and y0 % rows == 0 and p0 % rows == 0
    const = lambda shape: pl.BlockSpec(shape, lambda i: (0,) * len(shape))
    tok_blk = lambda width: pl.BlockSpec((rows, width), lambda i: (tok0 // rows + i, 0))
    y_blk = lambda kk: pl.BlockSpec((rows, D_MODEL // 2), lambda i: ((y0 + kk * n) // rows + i, 0))
    extra_specs, extra_args, aliases = [], [], {}
    if out_into is not None:
        assert out_into.shape == x1.shape
        extra_specs, extra_args, aliases = [pl.BlockSpec(memory_space=pl.ANY)], [out_into], {10: 0}
    return pl.pallas_call(
        _ple_kernel,
        grid=(n // rows,),
        in_specs=[tok_blk(D_MODEL), y_blk(0), y_blk(1), y_blk(2), y_blk(3), tok_blk(TOP_K),
                  pl.BlockSpec((rows, PLE_DIM), lambda i: (p0 // rows + i, 0)),
                  const((1, D_MODEL)), const((D_MODEL, D_MODEL)), const((PLE_DIM, D_MODEL))] + extra_specs,
        out_specs=tok_blk(D_MODEL),
        out_shape=jax.ShapeDtypeStruct(x1.shape, F32),
        scratch_shapes=[pltpu.VMEM((D_MODEL, D_MODEL), BF16), pltpu.VMEM((PLE_DIM, D_MODEL), BF16)],
        input_output_aliases=aliases,
        compiler_params=pltpu.CompilerParams(dimension_semantics=("arbitrary",), vmem_limit_bytes=VMEM_LIMIT),
        name="combine_ple",
    )(x1, y_tok, y_tok, y_tok, y_tok, gates_t, p_all, g_ple, w_gate, w_proj, *extra_args)


def _rope_compact(seq):
    half = ROT_DIM // 2
    inv = (np.float64(ROPE_THETA) ** (-np.arange(half, dtype=np.float64) / half)).astype(np.float32)
    ang = np.arange(seq, dtype=np.float32)[:, None] * inv[None, :]
    cs = np.concatenate([np.cos(ang.astype(np.float64)), np.sin(ang.astype(np.float64))], axis=1).astype(np.float32)
    e_cos, e_sa, e_sb = (np.zeros((ROT_DIM, LANES), np.float32) for _ in range(3))
    d = np.arange(LANES) % HEAD_DIM
    for lane, dd in enumerate(d):
        if dd < ROT_DIM:
            e_cos[dd % half, lane] = 1.0
        if half <= dd < ROT_DIM:
            e_sa[half + dd - half, lane] = 1.0
        if dd < half:
            e_sb[half + dd, lane] = -1.0
    cbias = (d >= ROT_DIM).astype(np.float32)[None, :]
    return tuple(jnp.asarray(a) for a in (cs, e_cos, e_sa, e_sb, cbias))


def _rope_tables(pos):
    half = ROT_DIM // 2
    d = np.arange(LANES) % HEAD_DIM
    inv = ROPE_THETA ** (-jnp.arange(half, dtype=F32) / half)
    inv_lane = jnp.where(d < ROT_DIM, inv[d % half], 0.0)
    ang = pos.astype(F32)[:, None] * inv_lane[None, :]
    cos, sin = jnp.cos(ang), jnp.sin(ang)
    sin_a = jnp.where((d >= half) & (d < ROT_DIM), sin, 0.0)
    sin_b = jnp.where(d < half, -sin, 0.0)
    return cos, sin_a, sin_b


def _layer(layer, xp, xs, ck, cv, st, p_prompt_all, p_sample_all, past_len, refine_tail, tabs_p, tabs_s, finish_state,
           norm_attn, w_in, q_norm, k_norm, attn_sinks, w_pool, pool_scale, w_out,
           norm_ffn, w_router, b_router, w_gate_up_all, b_gate_up_all, w_down_all, b_down_all,
           norm_ple, w_ple_gate, w_ple_proj):
    t, ns = xp.shape[0], xs.shape[0]
    g_attn = norm_attn.reshape(1, D_MODEL)
    qn = jnp.tile(q_norm, 2).reshape(1, LANES)
    kn = jnp.tile(k_norm, 2).reshape(1, LANES)
    lane = np.arange(LANES)
    hmean = jnp.asarray((lane[:, None] // HEAD_DIM == lane[None, :] // HEAD_DIM) / HEAD_DIM, F32)
    pscale = pool_scale.reshape(1, POOL_WIDTH)
    sink8 = jnp.broadcast_to(attn_sinks.reshape(2, Q_TILES).T.reshape(8, 1), (8, LANES))

    mix_args = (g_attn, w_in, qn, kn, hmean, w_pool, pscale, w_out)
    x1p, nk_p, nv_p, nu_p = _mixer_prompt(xp, 0, MIX_ROWS, BF16, attn_sinks, tabs_p, *mix_args)
    if refine_tail:
        x1p, nk_p, nv_p, nu_p = _mixer_prompt(xp, t - 2 * MIX_TAIL_ROWS, MIX_TAIL_ROWS, F32, attn_sinks, tabs_p,
                                              *mix_args, x1_into=x1p)
    st_t = jnp.transpose(st, (1, 0, 2))
    x1s, nk_s, nv_s, nst_t = _mixer_sample(xs, ck.reshape(ns, -1, KV_WIDTH), cv.reshape(ns, -1, KV_WIDTH), st_t,
                                           past_len, tabs_s, sink8, *mix_args)

    g_ffn = norm_ffn.reshape(1, D_MODEL)
    wr_t = w_router.T
    br = b_router.reshape(N_EXPERTS, 1)
    n_tok = t + ns
    bm = MOE_BM
    n_blocks = -(-(n_tok * TOP_K + N_EXPERTS * (bm - 1)) // bm)
    hf_p, idx_p, gate_p, pos_p, cnt_p = _route(x1p, g_ffn, wr_t, br, jnp.zeros((N_EXPERTS, 1), F32),
                                               ROUTE_ROWS, bm, n_blocks)[:5]
    hf_s, idx_s, gate_s, pos_s, _, pad_start, blk_e, blk_valid, blk_next = _route(x1s, g_ffn, wr_t, br, cnt_p, ns,
                                                                                  bm, n_blocks)
    idx = jnp.concatenate([idx_p, idx_s], axis=1)
    pos = jnp.concatenate([pos_p, pos_s], axis=1)
    blk_e, blk_valid, blk_next = (a[0, :n_blocks] for a in (blk_e, blk_valid, blk_next))
    state = finish_state(nk_p, nv_p, nu_p[POOL_HALO - POOL_PREFIX:], nk_s, nv_s, jnp.transpose(nst_t, (1, 0, 2)))
    blk_e, state = lax.optimization_barrier((blk_e, state))
    start_of = jnp.sum(jnp.where(idx[None] == jnp.arange(N_EXPERTS, dtype=I32)[:, None, None],
                                 pad_start.reshape(N_EXPERTS, 1, 1), 0), axis=0)
    dest = start_of + pos

    xs_rows = _scatter_rows(hf_p, hf_s, dest, n_blocks * bm)
    y = _moe(xs_rows, blk_e, blk_valid, blk_next, layer, w_gate_up_all, b_gate_up_all, w_down_all, b_down_all)

    tq = t // COMBINE_PARTS
    unit = SC_WORKERS * SC_CHUNK * 2
    assert t % COMBINE_PARTS == 0 and (TOP_K * tq) % unit == 0 and tq % PLE_ROWS == 0
    g_ple = norm_ple.reshape(1, D_MODEL)
    gates_p = gate_p.T
    yp = None
    for part in range(COMBINE_PARTS):
        back = dest[:, part * tq:(part + 1) * tq].reshape(-1)
        if part == COMBINE_PARTS - 1:
            back = jnp.concatenate([back, dest[:, t:].reshape(-1)])
            back = jnp.concatenate([back, jnp.arange(-back.shape[0] % unit, dtype=I32)])
        y_part = _gather_rows(y, back)
        yp = _ple(x1p, part * tq, tq, y_part, 0, gates_p, p_prompt_all, layer * t + part * tq, g_ple,
                  w_ple_gate, w_ple_proj, PLE_ROWS, out_into=yp)
    ys = _ple(x1s, 0, ns, y_part, TOP_K * tq, gate_s.T, p_sample_all, layer * ns, g_ple, w_ple_gate, w_ple_proj, ns)
    return yp, ys, state


def kernel(x_prompt, x_sample, cache_k, cache_v, state_pool, p_prompt, p_sample, norm_attn, w_in, q_norm, k_norm,
           attn_sinks, w_pool, pool_scale, w_out, norm_ffn, w_router, b_router, w_gate_up, b_gate_up, w_down, b_down,
           norm_ple, w_ple_gate, w_ple_proj):
    depth = norm_attn.shape[0]
    batch, seq, d = x_prompt.shape
    ns, dec_seq, _ = x_sample.shape
    wb = cache_k.shape[2]
    assert batch == 1 and dec_seq == 1 and d == D_MODEL and wb == WINDOW
    assert cache_k.shape[3:] == (N_KV_HEADS, HEAD_DIM) and state_pool.shape[2:] == (POOL_PREFIX, POOL_WIDTH)
    past_len = PAST_LEN
    yp = x_prompt.reshape(seq, d)
    ys = x_sample.reshape(ns, d)
    p_prompt_all = p_prompt.reshape(depth * seq, PLE_DIM)
    p_sample_all = p_sample.reshape(depth * ns, PLE_DIM)
    tabs_p = _rope_compact(seq)
    tabs_s = _rope_tables(jnp.full((1,), past_len))
    kv_p, kv_s = (1, WINDOW, N_KV_HEADS, HEAD_DIM), (ns, wb, N_KV_HEADS, HEAD_DIM)
    state_shapes = (kv_p, kv_p, (1, POOL_PREFIX, POOL_WIDTH), kv_s, kv_s, (ns, POOL_PREFIX, POOL_WIDTH))
    per_layer = []

    def shaped(*state):
        return tuple(a.reshape(shp) for a, shp in zip(state, state_shapes))

    def stacked(*state):
        prevs = list(zip(*per_layer)) or [()] * len(state_shapes)
        return tuple(jnp.stack(list(prev) + [cur]) for prev, cur in zip(prevs, shaped(*state)))

    for i in range(depth):
        yp, ys, state = _layer(i, yp, ys, cache_k[i], cache_v[i], state_pool[i], p_prompt_all, p_sample_all, past_len,
                               i < depth - 1, tabs_p, tabs_s, shaped if i < depth - 1 else stacked,
                               norm_attn[i], w_in[i], q_norm[i], k_norm[i], attn_sinks[i], w_pool[i], pool_scale[i],
                               w_out[i], norm_ffn[i], w_router[i], b_router[i], w_gate_up, b_gate_up, w_down, b_down,
                               norm_ple[i], w_ple_gate[i], w_ple_proj[i])
        per_layer.append(state)
    return (yp.reshape(batch, seq, d), ys.reshape(ns, dec_seq, d)) + tuple(state)
```

```python
import functools

import jax
import jax.numpy as jnp
import numpy as np
from jax import lax
from jax.experimental import pallas as pl
from jax.experimental.pallas import tpu as pltpu
from jax.experimental.pallas import tpu_sc as plsc

F32 = jnp.float32
BF16 = jnp.bfloat16
U32 = jnp.uint32
I32 = jnp.int32

D_MODEL = 1024
HEAD_DIM = 64
N_HEADS = 8
N_KV_HEADS = 2
GROUP = N_HEADS // N_KV_HEADS
ATTN_WIDTH = N_HEADS * HEAD_DIM
KV_WIDTH = N_KV_HEADS * HEAD_DIM
POOL_WIDTH = 512
POOL_WINDOWS = (2, 4, 8, 16)
POOL_GC = POOL_WIDTH // len(POOL_WINDOWS)
POOL_PREFIX = max(POOL_WINDOWS) - 1
POOL_HALO = POOL_PREFIX + 1
POOL_PAD = 8
IN_WIDTH = ATTN_WIDTH + 2 * KV_WIDTH + POOL_WIDTH
WINDOW = 128
ROPE_THETA = 500000.0
ROT_DIM = HEAD_DIM // 4
N_EXPERTS = 32
TOP_K = 4
D_FF = 1024
SWIGLU_ALPHA = 1.702
SWIGLU_LIMIT = 7.0
PLE_DIM = 256
PAST_LEN = 16384
EPS = 1e-5
NEG_INF = -1e30

LANES = 128
Q_TILES = ATTN_WIDTH // LANES

MIX_ROWS = 512
MIX_TAIL_ROWS = 128
SAMPLE_CHUNK = 16
ROUTE_ROWS = 512
MOE_BM = 512
SC_CORES = 2
SC_SUBCORES = 16
SC_WORKERS = SC_CORES * SC_SUBCORES
SC_CHUNK = 64
PLE_ROWS = 512
COMBINE_PARTS = 2
VMEM_LIMIT = 56 * 1024 * 1024


def _rms(x, g):
    return x * lax.rsqrt(jnp.mean(x * x, axis=-1, keepdims=True) + EPS) * g


def _pack_halves(x):
    w = x.shape[1] // 2
    lo = lax.bitcast_convert_type(x[:, :w].astype(BF16).astype(F32), U32) >> 16
    hi = lax.bitcast_convert_type(x[:, w:].astype(BF16).astype(F32), U32) & jnp.uint32(0xFFFF0000)
    return lo | hi


def _unpack_halves(packed):
    lo = lax.bitcast_convert_type(packed << 16, F32)
    hi = lax.bitcast_convert_type(packed & jnp.uint32(0xFFFF0000), F32)
    return jnp.concatenate([lo, hi], axis=1)


def _mm(a, b, nt=False):
    dims = (((1,), (1 if nt else 0,)), ((), ()))
    if b.dtype == F32:
        return lax.dot_general(a.astype(F32), b, dims, preferred_element_type=F32, precision=lax.Precision.HIGHEST)
    return lax.dot_general(a.astype(BF16), b, dims, preferred_element_type=F32)


def _head_norm_rope(t, hmean, gain, cos, sin_a, sin_b):
    t = t * lax.rsqrt(_mm(t * t, hmean) + EPS) * gain
    return t * cos + pltpu.roll(t, ROT_DIM // 2, axis=1) * sin_a + pltpu.roll(t, LANES - ROT_DIM // 2, axis=1) * sin_b


def _softmax_pv(s, sink, v, ones):
    m = jnp.maximum(jnp.max(s, axis=-1, keepdims=True), sink)
    e = jnp.exp(s - m).astype(v.dtype)
    den = _mm(e, ones) + jnp.exp(sink - m)
    return _mm(e, v) / den


def _prep_mixer_weights(win_ref, wout_ref, win_c, wout_c):
    cdt = win_c.dtype
    for j in range(Q_TILES):
        for s in range(2):
            src = (j + Q_TILES * s) * HEAD_DIM
            dst = j * LANES + s * HEAD_DIM
            win_c[:, dst:dst + HEAD_DIM] = win_ref[:, src:src + HEAD_DIM].astype(cdt)
            wout_c[dst:dst + HEAD_DIM, :] = wout_ref[src:src + HEAD_DIM, :].astype(cdt)
    win_c[:, ATTN_WIDTH:] = win_ref[:, ATTN_WIDTH:].astype(cdt)
    wout_c[ATTN_WIDTH:, :] = wout_ref[ATTN_WIDTH:, :].astype(cdt)


def _mixer_prompt_kernel(sinks_ref, x_ref, cos_ref, sa_ref, sb_ref, g_ref, win_ref, qn_ref, kn_ref, hm_ref,
                         wpool_ref, pscale_ref, wout_ref, *rest, row_offset, aliased):
    (x1_ref, klast_ref, vlast_ref, ulast_ref, kprev, vprev, uext, mix, win_c, wout_c,
     *lvl) = rest[1:] if aliased else rest
    i = pl.program_id(0)
    rows = x_ref.shape[0]
    n_sub = rows // WINDOW
    cdt = win_c.dtype
    row0 = row_offset + i * rows

    @pl.when(i == 0)
    def _():
        _prep_mixer_weights(win_ref, wout_ref, win_c, wout_c)
        kprev[...] = jnp.zeros_like(kprev)
        vprev[...] = jnp.zeros_like(vprev)
        uext[0:POOL_PAD + POOL_HALO, :] = jnp.zeros((POOL_PAD + POOL_HALO, POOL_WIDTH), F32)
        for buf in lvl:
            buf[0:POOL_PAD, :] = jnp.zeros((POOL_PAD, POOL_GC), F32)

    x = x_ref[...]
    proj = _mm(_rms(x, g_ref[...]), win_c[...])
    cos, sin_a, sin_b = cos_ref[...], sa_ref[...], sb_ref[...]

    n_t = Q_TILES + 1
    t_all = jnp.concatenate([proj[:, j * LANES:(j + 1) * LANES] for j in range(n_t)], axis=0)
    t3 = (t_all * lax.rsqrt(_mm(t_all * t_all, hm_ref[...].astype(cdt)) + EPS)).reshape(n_t, rows, LANES)
    t3 = jnp.concatenate([t3[:Q_TILES] * (qn_ref[...] * HEAD_DIM ** -0.5), t3[Q_TILES:] * kn_ref[...]], axis=0)
    t2 = t3.reshape(n_t * rows, LANES)
    t3 = (t3 * cos + pltpu.roll(t2, ROT_DIM // 2, axis=1).reshape(n_t, rows, LANES) * sin_a
          + pltpu.roll(t2, LANES - ROT_DIM // 2, axis=1).reshape(n_t, rows, LANES) * sin_b)
    q3 = t3[:Q_TILES]
    k = t3[Q_TILES]
    v = proj[:, ATTN_WIDTH + KV_WIDTH:ATTN_WIDTH + 2 * KV_WIDTH]
    u = proj[:, ATTN_WIDTH + 2 * KV_WIDTH:]
    klast_ref[...] = k[rows - WINDOW:, :]
    vlast_ref[...] = v[rows - WINDOW:, :]
    ulast_ref[...] = u[rows - POOL_HALO:, :]
    k_c = k.astype(cdt)
    v_c = jnp.concatenate([v.astype(cdt), jnp.ones((rows, LANES), cdt)], axis=1)
    v_first = jnp.concatenate([vprev[...], jnp.ones((WINDOW, LANES), cdt)], axis=1)

    lane = lax.broadcasted_iota(I32, (WINDOW, LANES), 1)
    left = (lane < HEAD_DIM)[None]
    qi = lax.broadcasted_iota(I32, (WINDOW, 2 * WINDOW), 0)
    kj = lax.broadcasted_iota(I32, (WINDOW, 2 * WINDOW), 1)
    band = (kj - qi >= 1) & (kj - qi <= WINDOW)
    sink3 = jnp.concatenate([jnp.full((1, 1, 1), sinks_ref[j + Q_TILES * s], F32)
                             for j in range(Q_TILES) for s in range(2)], axis=0)
    n_g = 2 * Q_TILES

    v_cats, masks, scores = [], [], []
    for c in range(n_sub):
        r0 = c * WINDOW
        if c == 0:
            k_cat = jnp.concatenate([kprev[...], k_c[0:WINDOW]], axis=0)
            v_cats.append(jnp.concatenate([v_first, v_c[0:WINDOW]], axis=0))
            masks.append(band & (kj + (row0 - WINDOW) >= 0))
        else:
            k_cat = k_c[r0 - WINDOW:r0 + WINDOW]
            v_cats.append(v_c[r0 - WINDOW:r0 + WINDOW])
            masks.append(band)
        q_c = q3[:, r0:r0 + WINDOW, :]
        q_all = jnp.concatenate([jnp.where(left, q_c, 0.0), jnp.where(left, 0.0, q_c)], axis=1)
        scores.append(_mm(q_all.reshape(n_g * WINDOW, LANES), k_cat, nt=True))
    kprev[...] = k_c[rows - WINDOW:]
    vprev[...] = v[rows - WINDOW:].astype(cdt)

    probs, maxes = [], []
    for c in range(n_sub):
        s = jnp.where(masks[c][None], scores[c].reshape(n_g, WINDOW, 2 * WINDOW), NEG_INF)
        m = jnp.maximum(jnp.max(s, axis=-1, keepdims=True), sink3)
        probs.append(jnp.exp(s - m).astype(cdt).reshape(n_g * WINDOW, 2 * WINDOW))
        maxes.append(m)
    applied = [_mm(probs[c], v_cats[c]) for c in range(n_sub)]
    for c in range(n_sub):
        r0 = c * WINDOW
        den = applied[c][:, LANES:] + jnp.exp(sink3 - maxes[c]).reshape(n_g * WINDOW, 1)
        o = (applied[c][:, :LANES] / den).reshape(Q_TILES, 2 * WINDOW, LANES)
        a = jnp.where(left, o[:, :WINDOW], o[:, WINDOW:])
        for j in range(Q_TILES):
            mix[r0:r0 + WINDOW, j * LANES:(j + 1) * LANES] = a[j].astype(cdt)

    base = POOL_PAD + POOL_HALO
    ext = base + rows
    uext[base:ext, :] = u
    pos1 = (lax.broadcasted_iota(I32, (rows, 1), 0) + row0 + 1).astype(F32)
    lvl_of = {1: lvl[0:1], 2: lvl[1:3], 3: lvl[3:5]}
    for gi, w in enumerate(POOL_WINDOWS):
        cols = slice(gi * POOL_GC, (gi + 1) * POOL_GC)
        src, src_cols = uext, cols
        for level in range(1, gi + 2):
            sft = 1 << (level - 1)
            if level <= gi:
                dst = lvl_of[gi][(level - 1) % 2]
                dst[POOL_PAD:ext, :] = src[POOL_PAD:ext, src_cols] + src[POOL_PAD - sft:ext - sft, src_cols]
                src, src_cols = dst, slice(None)
            else:
                wsum = src[base:ext, src_cols] + src[base - sft:ext - sft, src_cols]
        d = wsum / jnp.minimum(pos1, float(w)) - u[:, cols]
        y = _mm(d, wpool_ref[gi].astype(cdt)) * pscale_ref[:, cols]
        mix[:, ATTN_WIDTH + gi * POOL_GC:ATTN_WIDTH + (gi + 1) * POOL_GC] = y.astype(cdt)
    uext[POOL_PAD:base, :] = u[rows - POOL_HALO:, :]

    x1_ref[...] = x + _mm(mix[...], wout_c[...])


def _mixer_prompt(x_full, row_offset, rows, cdt, sinks, tabs, g_attn, w_in, qn, kn, hmean, w_pool, pscale, w_out,
                  x1_into=None):
    t = x_full.shape[0] - row_offset
    assert t % rows == 0 and row_offset % rows == 0 and rows % WINDOW == 0 and rows >= POOL_HALO
    blk0, n_steps = row_offset // rows, t // rows
    const = lambda shape: pl.BlockSpec(shape, lambda i, *_: (0,) * len(shape))
    row_blk = lambda width: pl.BlockSpec((rows, width), lambda i, *_: (blk0 + i, 0))
    aliased = x1_into is not None
    if aliased:
        assert x1_into.shape == x_full.shape
        x1_spec = pl.BlockSpec((rows, D_MODEL), lambda i, *_: (blk0 + n_steps - 1, 0))
        x1_shape = x1_into.shape
        extra_specs, extra_args, aliases = [pl.BlockSpec(memory_space=pl.ANY)], [x1_into], {13: 0}
    else:
        x1_spec = pl.BlockSpec((rows, D_MODEL), lambda i, *_: (i, 0))
        x1_shape = (t, D_MODEL)
        extra_specs, extra_args, aliases = [], [], {}
    grid_spec = pltpu.PrefetchScalarGridSpec(
        num_scalar_prefetch=1,
        grid=(n_steps,),
        in_specs=[row_blk(D_MODEL), row_blk(LANES), row_blk(LANES), row_blk(LANES),
                  const((1, D_MODEL)), const((D_MODEL, IN_WIDTH)), const((1, LANES)), const((1, LANES)),
                  const((LANES, LANES)), const((len(POOL_WINDOWS), POOL_GC, POOL_GC)), const((1, POOL_WIDTH)),
                  const((D_MODEL, D_MODEL))] + extra_specs,
        out_specs=[x1_spec, const((WINDOW, KV_WIDTH)), const((WINDOW, KV_WIDTH)),
                   const((POOL_HALO, POOL_WIDTH))],
        scratch_shapes=[pltpu.VMEM((WINDOW, KV_WIDTH), cdt), pltpu.VMEM((WINDOW, KV_WIDTH), cdt),
                        pltpu.VMEM((POOL_PAD + POOL_HALO + rows, POOL_WIDTH), F32), pltpu.VMEM((rows, D_MODEL), cdt),
                        pltpu.VMEM((D_MODEL, IN_WIDTH), cdt), pltpu.VMEM((D_MODEL, D_MODEL), cdt)]
        + [pltpu.VMEM((POOL_PAD + POOL_HALO + rows, POOL_GC), F32)] * 5,
    )
    return pl.pallas_call(
        functools.partial(_mixer_prompt_kernel, row_offset=row_offset, aliased=aliased),
        grid_spec=grid_spec,
        out_shape=[jax.ShapeDtypeStruct(x1_shape, F32), jax.ShapeDtypeStruct((WINDOW, KV_WIDTH), F32),
                   jax.ShapeDtypeStruct((WINDOW, KV_WIDTH), F32), jax.ShapeDtypeStruct((POOL_HALO, POOL_WIDTH), F32)],
        input_output_aliases=aliases,
        compiler_params=pltpu.CompilerParams(dimension_semantics=("arbitrary",), vmem_limit_bytes=VMEM_LIMIT),
        name="mixer_prompt",
    )(sinks, x_full, *tabs, g_attn, w_in, qn, kn, hmean, w_pool, pscale, w_out, *extra_args)


def _mixer_sample_kernel(x_ref, ck_ref, cv_ref, st_ref, cos_ref, sa_ref, sb_ref, sink8_ref, g_ref, win_ref, qn_ref,
                         kn_ref, hm_ref, wpool_ref, pscale_ref, wout_ref,
                         x1_ref, nk_ref, nv_ref, nst_ref, o8, win_c, wout_c, *, pos):
    nb = x_ref.shape[0]
    wb = ck_ref.shape[1]

    @pl.when(pl.program_id(0) == 0)
    def _():
        _prep_mixer_weights(win_ref, wout_ref, win_c, wout_c)

    x = x_ref[...]
    h = _rms(x, g_ref[...]).astype(BF16)
    proj = jnp.dot(h, win_c[...], preferred_element_type=F32)
    cos, sin_a, sin_b = cos_ref[...], sa_ref[...], sb_ref[...]
    hmean = hm_ref[...].astype(BF16)
    k = _head_norm_rope(proj[:, ATTN_WIDTH:ATTN_WIDTH + KV_WIDTH], hmean, kn_ref[...], cos, sin_a, sin_b)
    v = proj[:, ATTN_WIDTH + KV_WIDTH:ATTN_WIDTH + 2 * KV_WIDTH]
    u = proj[:, ATTN_WIDTH + 2 * KV_WIDTH:]

    nk_ref[:, 0:wb - 1, :] = ck_ref[:, 1:wb, :]
    nv_ref[:, 0:wb - 1, :] = cv_ref[:, 1:wb, :]
    for b in range(nb):
        nk_ref[b, wb - 1:wb, :] = k[b:b + 1, :]
        nv_ref[b, wb - 1:wb, :] = v[b:b + 1, :]

    r8 = lax.broadcasted_iota(I32, (nb * 8, LANES), 0)
    lane8 = lax.broadcasted_iota(I32, (nb * 8, LANES), 1)
    keep = (lane8 < HEAD_DIM) == (r8 % 2 == 0)
    rep = (lax.broadcasted_iota(I32, (nb * 8, nb), 0) // 8 == lax.broadcasted_iota(I32, (nb * 8, nb), 1)).astype(BF16)
    q8 = jnp.zeros((nb * 8, LANES), F32)
    scale = HEAD_DIM ** -0.5
    for j in range(Q_TILES):
        qt = _head_norm_rope(proj[:, j * LANES:(j + 1) * LANES], hmean, qn_ref[...], cos, sin_a, sin_b) * scale
        qrep = jnp.dot(rep, qt.astype(BF16), preferred_element_type=F32)
        q8 = jnp.where(keep & ((r8 % 8) // 2 == j), qrep, q8)
    q8 = q8.astype(BF16)

    sink8 = sink8_ref[:, 0:1]
    ones_bf = jnp.ones((wb, LANES), BF16)
    assert pos >= wb - 1 and wb <= WINDOW
    for b in range(nb):
        kb = nk_ref[b].astype(BF16)
        vb = nv_ref[b].astype(BF16)
        s = lax.dot_general(q8[b * 8:(b + 1) * 8], kb, (((1,), (1,)), ((), ())), preferred_element_type=F32)
        o8[b * 8:(b + 1) * 8, :] = _softmax_pv(s, sink8, vb, ones_bf)
    o8m = jnp.where(keep, o8[...], 0.0).astype(BF16)

    a_tiles = []
    sel_r = lax.broadcasted_iota(I32, (nb, nb * 8), 1)
    sel_b = lax.broadcasted_iota(I32, (nb, nb * 8), 0)
    for j in range(Q_TILES):
        sel = ((sel_r // 8 == sel_b) & ((sel_r % 8) // 2 == j)).astype(BF16)
        a_tiles.append(jnp.dot(sel, o8m, preferred_element_type=F32))

    z_tiles = []
    for gi, w in enumerate(POOL_WINDOWS):
        cols = slice(gi * POOL_GC, (gi + 1) * POOL_GC)
        wsum = u[:, cols]
        for sft in range(1, w):
            wsum = wsum + st_ref[POOL_PREFIX - sft, :, cols]
        d = wsum / float(min(pos + 1, w)) - u[:, cols]
        z_tiles.append(jnp.dot(d.astype(BF16), wpool_ref[gi].astype(BF16), preferred_element_type=F32)
                       * pscale_ref[:, cols])
    nst_ref[0:POOL_PREFIX - 1] = st_ref[1:POOL_PREFIX]
    nst_ref[POOL_PREFIX - 1] = u

    mixv = jnp.concatenate(a_tiles + z_tiles, axis=1).astype(BF16)
    x1_ref[...] = x + jnp.dot(mixv, wout_c[...], preferred_element_type=F32)


def _mixer_sample(x, ck, cv, st, pos, tabs, sink8, g_attn, w_in, qn, kn, hmean, w_pool, pscale, w_out):
    n, wb = ck.shape[0], ck.shape[1]
    nb = SAMPLE_CHUNK
    assert n % nb == 0
    const = lambda shape: pl.BlockSpec(shape, lambda i: (0,) * len(shape))
    cache_blk = pl.BlockSpec((nb, wb, KV_WIDTH), lambda i: (i, 0, 0))
    st_blk = pl.BlockSpec((POOL_PREFIX, nb, POOL_WIDTH), lambda i: (0, i, 0))
    x_blk = pl.BlockSpec((nb, D_MODEL), lambda i: (i, 0))
    return pl.pallas_call(
        functools.partial(_mixer_sample_kernel, pos=pos),
        grid=(n // nb,),
        in_specs=[x_blk, cache_blk, cache_blk, st_blk, const((1, LANES)), const((1, LANES)), const((1, LANES)),
                  const((8, LANES)), const((1, D_MODEL)), const((D_MODEL, IN_WIDTH)), const((1, LANES)),
                  const((1, LANES)), const((LANES, LANES)), const((len(POOL_WINDOWS), POOL_GC, POOL_GC)),
                  const((1, POOL_WIDTH)), const((D_MODEL, D_MODEL))],
        out_specs=[x_blk, cache_blk, cache_blk, st_blk],
        out_shape=[jax.ShapeDtypeStruct((n, D_MODEL), F32), jax.ShapeDtypeStruct(ck.shape, F32),
                   jax.ShapeDtypeStruct(cv.shape, F32), jax.ShapeDtypeStruct(st.shape, F32)],
        scratch_shapes=[pltpu.VMEM((nb * 8, LANES), F32), pltpu.VMEM((D_MODEL, IN_WIDTH), BF16),
                        pltpu.VMEM((D_MODEL, D_MODEL), BF16)],
        compiler_params=pltpu.CompilerParams(dimension_semantics=("arbitrary",), vmem_limit_bytes=VMEM_LIMIT),
        name="mixer_sample",
    )(x, ck, cv, st, *tabs, sink8, g_attn, w_in, qn, kn, hmean, w_pool, pscale, w_out)


def _block_plan(cnt, bm, n_lanes):
    e_sub = lax.broadcasted_iota(I32, (N_EXPERTS, LANES), 0)
    e_lane = lax.broadcasted_iota(I32, (N_EXPERTS, LANES), 1)
    padded = jnp.floor((cnt + (bm - 1)) / bm) * bm
    padded_lane = jnp.sum(jnp.where(e_sub == e_lane, padded, 0.0), axis=0, keepdims=True)
    pad_end = jnp.sum(jnp.where(e_lane <= e_sub, padded_lane, 0.0), axis=1, keepdims=True)
    pad_start = pad_end - padded
    blk_start = lax.broadcasted_iota(I32, (N_EXPERTS, n_lanes), 1).astype(F32) * bm
    blk_e = jnp.minimum(jnp.sum((pad_end <= blk_start).astype(F32), axis=0, keepdims=True), N_EXPERTS - 1.0)
    mine = lax.broadcasted_iota(I32, (N_EXPERTS, n_lanes), 0).astype(F32) == blk_e
    last = jnp.sum(jnp.where(mine, pad_start + cnt, 0.0), axis=0, keepdims=True)
    blk_valid = jnp.clip(last - blk_start[0:1], 0.0, float(bm))
    e_blk = lax.broadcasted_iota(I32, (N_EXPERTS, n_lanes), 0).astype(F32)
    later = jnp.min(jnp.where((e_blk > blk_e) & (cnt > 0.0), e_blk, float(N_EXPERTS)), axis=0, keepdims=True)
    blk_next = jnp.where(later < N_EXPERTS, later, -1.0)
    return pad_start, blk_e, blk_valid, blk_next


def _route_kernel(x1_ref, g_ref, wr_ref, br_ref, cnt_in_ref, hf_ref, idx_ref, gate_ref, pos_ref, cnt_ref,
                  pstart_ref, blke_ref, blkv_ref, blkn_ref, counts, *, bm):
    i = pl.program_id(0)
    rows = x1_ref.shape[0]

    @pl.when(i == 0)
    def _():
        counts[...] = cnt_in_ref[...]

    h = _rms(x1_ref[...], g_ref[...])
    hf_ref[...] = _pack_halves(h)

    logits = lax.dot_general(wr_ref[...], h, (((1,), (1,)), ((), ())), preferred_element_type=F32,
                             precision=lax.Precision.HIGHEST) + br_ref[...]
    eid = lax.broadcasted_iota(I32, (N_EXPERTS, rows), 0).astype(F32)
    work = logits
    vals, hots = [], []
    for kk in range(TOP_K):
        m = jnp.max(work, axis=0, keepdims=True)
        first = jnp.min(jnp.where(work == m, eid, float(N_EXPERTS)), axis=0, keepdims=True)
        hot = eid == first
        work = jnp.where(hot, -jnp.inf, work)
        vals.append(m)
        hots.append(hot)
        idx_ref[kk:kk + 1, :] = first.astype(I32)
    es = [jnp.exp(vv - vals[0]) for vv in vals]
    den = es[0] + es[1] + es[2] + es[3]
    for kk in range(TOP_K):
        gate_ref[kk:kk + 1, :] = es[kk] / den

    chosen = hots[0] | hots[1] | hots[2] | hots[3]
    before = (lax.broadcasted_iota(I32, (rows, rows), 0) < lax.broadcasted_iota(I32, (rows, rows), 1)).astype(BF16)
    rank = jnp.dot(chosen.astype(BF16), before, preferred_element_type=F32) + counts[...]
    for kk in range(TOP_K):
        pos_ref[kk:kk + 1, :] = jnp.sum(jnp.where(hots[kk], rank, 0.0), axis=0, keepdims=True).astype(I32)
    counts[...] = counts[...] + jnp.sum(chosen.astype(F32), axis=1, keepdims=True)
    cnt_ref[...] = counts[...]

    @pl.when(i == pl.num_programs(0) - 1)
    def _():
        pad_start, blk_e, blk_valid, blk_next = _block_plan(counts[...], bm, blke_ref.shape[1])
        pstart_ref[...] = pad_start.astype(I32)
        blke_ref[...] = blk_e.astype(I32)
        blkv_ref[...] = blk_valid.astype(I32)
        blkn_ref[...] = blk_next.astype(I32)


def _route(x1, g_ffn, wr_t, br, cnt_in, rows, bm, n_blocks):
    n = x1.shape[0]
    assert n % rows == 0
    nb_lanes = -(-n_blocks // LANES) * LANES
    const = lambda shape: pl.BlockSpec(shape, lambda i: (0,) * len(shape))
    tok_blk = pl.BlockSpec((TOP_K, rows), lambda i: (0, i))
    return pl.pallas_call(
        functools.partial(_route_kernel, bm=bm),
        grid=(n // rows,),
        in_specs=[pl.BlockSpec((rows, D_MODEL), lambda i: (i, 0)), const((1, D_MODEL)), const((N_EXPERTS, D_MODEL)),
                  const((N_EXPERTS, 1)), const((N_EXPERTS, 1))],
        out_specs=[pl.BlockSpec((rows, D_MODEL // 2), lambda i: (i, 0)), tok_blk, tok_blk, tok_blk,
                   const((N_EXPERTS, 1)), const((N_EXPERTS, 1)), const((1, nb_lanes)), const((1, nb_lanes)),
                   const((1, nb_lanes))],
        out_shape=[jax.ShapeDtypeStruct((n, D_MODEL // 2), U32), jax.ShapeDtypeStruct((TOP_K, n), I32),
                   jax.ShapeDtypeStruct((TOP_K, n), F32), jax.ShapeDtypeStruct((TOP_K, n), I32),
                   jax.ShapeDtypeStruct((N_EXPERTS, 1), F32), jax.ShapeDtypeStruct((N_EXPERTS, 1), I32),
                   jax.ShapeDtypeStruct((1, nb_lanes), I32), jax.ShapeDtypeStruct((1, nb_lanes), I32),
                   jax.ShapeDtypeStruct((1, nb_lanes), I32)],
        scratch_shapes=[pltpu.VMEM((N_EXPERTS, 1), F32)],
        compiler_params=pltpu.CompilerParams(dimension_semantics=("arbitrary",), vmem_limit_bytes=VMEM_LIMIT),
        name="route",
    )(x1, g_ffn, wr_t, br, cnt_in)


def _sc_mesh():
    return plsc.VectorSubcoreMesh(core_axis_name="core", subcore_axis_name="subcore")


def _sc_worker_id():
    return lax.axis_index("core") * SC_SUBCORES + lax.axis_index("subcore")


def _scatter_rows(xa, xb, dest, n_rows):
    ch = SC_CHUNK
    na, w = xa.shape
    n = na + xb.shape[0]
    nk = dest.shape[0]
    assert na % ch == 0 and n % ch == 0 and dest.shape[1] == n and xb.shape[1] == w and xa.dtype == xb.dtype
    n_chunks = n // ch
    dest_c = dest.reshape(nk, n_chunks, ch).transpose(1, 0, 2).reshape(-1)
    dma = pltpu.SemaphoreType.DMA

    @pl.kernel(out_type=jax.ShapeDtypeStruct((n_rows, w), xa.dtype), mesh=_sc_mesh(),
               scratch_types=[pltpu.VMEM((nk * ch,), I32), pltpu.VMEM((ch, w), xa.dtype), dma] + [dma] * nk)
    def scatter_kernel(xa_hbm, xb_hbm, d_hbm, o_hbm, idx_v, buf, sem_i, *sem_s):
        wid = _sc_worker_id()

        @pl.loop(0, -(-n_chunks // SC_WORKERS))
        def _(j):
            c = j * SC_WORKERS + wid

            @pl.when(c < n_chunks)
            def _():
                load_idx = pltpu.make_async_copy(d_hbm.at[pl.ds(c * (nk * ch), nk * ch)], idx_v, sem_i)
                load_idx.start()

                @pl.when(c < na // ch)
                def _():
                    pltpu.sync_copy(xa_hbm.at[pl.ds(c * ch, ch)], buf)

                @pl.when(c >= na // ch)
                def _():
                    pltpu.sync_copy(xb_hbm.at[pl.ds(c * ch - na, ch)], buf)

                load_idx.wait()
                puts = [pltpu.make_async_copy(buf, o_hbm.at[idx_v.at[pl.ds(kk * ch, ch)]], sem_s[kk])
                        for kk in range(nk)]
                for cp in puts:
                    cp.start()
                for cp in puts:
                    cp.wait()

    return scatter_kernel(xa, xb, dest_c)


def _gather_rows(src, idx):
    ch = SC_CHUNK
    m = idx.shape[0]
    w = src.shape[1]
    per = m // SC_WORKERS
    n_pairs = per // (2 * ch)
    assert m % SC_WORKERS == 0 and per % (2 * ch) == 0
    dma = pltpu.SemaphoreType.DMA

    @pl.kernel(out_type=jax.ShapeDtypeStruct((m, w), src.dtype), mesh=_sc_mesh(),
               scratch_types=[pltpu.VMEM((per,), I32), pltpu.VMEM((ch, w), src.dtype), pltpu.VMEM((ch, w), src.dtype),
                              dma, dma, dma, dma])
    def gather_kernel(s_hbm, i_hbm, o_hbm, idx_v, buf_a, buf_b, sem_ga, sem_gb, sem_wa, sem_wb):
        base = _sc_worker_id() * per
        pltpu.sync_copy(i_hbm.at[pl.ds(base, per)], idx_v)

        def fetch(j, buf, sem):
            return pltpu.make_async_copy(s_hbm.at[idx_v.at[pl.ds(j * ch, ch)]], buf, sem)

        def put(j, buf, sem):
            return pltpu.make_async_copy(buf, o_hbm.at[pl.ds(base + j * ch, ch)], sem)

        fetch(0, buf_a, sem_ga).start()

        @pl.loop(0, n_pairs)
        def _(p):
            j0 = 2 * p
            j1 = j0 + 1

            @pl.when(p > 0)
            def _():
                put(j1 - 2, buf_b, sem_wb).wait()

            fetch(j1, buf_b, sem_gb).start()
            fetch(j0, buf_a, sem_ga).wait()
            put(j0, buf_a, sem_wa).start()
            fetch(j1, buf_b, sem_gb).wait()
            put(j1, buf_b, sem_wb).start()
            put(j0, buf_a, sem_wa).wait()

            @pl.when(p + 1 < n_pairs)
            def _():
                fetch(j0 + 2, buf_a, sem_ga).start()

        put(2 * n_pairs - 1, buf_b, sem_wb).wait()

    return gather_kernel(src, idx)


def _moe_kernel(blk_e_ref, blk_valid_ref, blk_next_ref, xs_ref, wgu_hbm, bgu_ref, wd_hbm, bd_ref, y_ref,
                wgu_f32, wd_f32, wgu_bf, wd_bf, sems, *, e0):
    i = pl.program_id(0)
    e = blk_e_ref[i]
    n_valid = blk_valid_ref[i]
    used = n_valid > 0
    new_expert = (i == 0) | (blk_e_ref[jnp.maximum(i - 1, 0)] != e)

    def weight_copies(expert):
        return (pltpu.make_async_copy(wgu_hbm.at[e0 + expert], wgu_f32, sems.at[0]),
                pltpu.make_async_copy(wd_hbm.at[e0 + expert], wd_f32, sems.at[1]))

    @pl.when(used & (i == 0))
    def _():
        for cp in weight_copies(e):
            cp.start()

    @pl.when(used & new_expert)
    def _():
        for cp in weight_copies(e):
            cp.wait()
        wgu_bf[...] = wgu_f32[...].astype(BF16)
        wd_bf[...] = wd_f32[...].astype(BF16)
        nxt = blk_next_ref[i]

        @pl.when(nxt >= 0)
        def _():
            for cp in weight_copies(nxt):
                cp.start(priority=1)

    @pl.when(used)
    def _():
        valid = lax.broadcasted_iota(I32, xs_ref.shape, 0) < n_valid
        xb = _unpack_halves(jnp.where(valid, xs_ref[...], jnp.uint32(0))).astype(BF16)
        gu = jnp.dot(xb, wgu_bf[...], preferred_element_type=F32) + bgu_ref[0]
        g = jnp.minimum(gu[:, :D_FF], SWIGLU_LIMIT)
        up = jnp.clip(gu[:, D_FF:], -SWIGLU_LIMIT, SWIGLU_LIMIT)
        act = (up + 1.0) * (g * jax.nn.sigmoid(SWIGLU_ALPHA * g))
        y_ref[...] = _pack_halves(jnp.dot(act.astype(BF16), wd_bf[...], preferred_element_type=F32) + bd_ref[0])

    @pl.when(jnp.logical_not(used))
    def _():
        y_ref[...] = jnp.zeros_like(y_ref)


def _moe(xs, blk_e, blk_valid, blk_next, layer, w_gu, b_gu, w_d, b_d):
    n_rows = xs.shape[0]
    bm = MOE_BM
    assert n_rows % bm == 0
    e0 = layer * N_EXPERTS
    n_we = w_gu.shape[0] * w_gu.shape[1]
    any_spec = pl.BlockSpec(memory_space=pl.ANY)
    grid_spec = pltpu.PrefetchScalarGridSpec(
        num_scalar_prefetch=3,
        grid=(n_rows // bm,),
        in_specs=[pl.BlockSpec((bm, D_MODEL // 2), lambda i, be, bv, bn: (i, 0)),
                  any_spec,
                  pl.BlockSpec((1, 1, 2 * D_FF), lambda i, be, bv, bn: (e0 + be[i], 0, 0)),
                  any_spec,
                  pl.BlockSpec((1, 1, D_MODEL), lambda i, be, bv, bn: (e0 + be[i], 0, 0))],
        out_specs=pl.BlockSpec((bm, D_MODEL // 2), lambda i, be, bv, bn: (i, 0)),
        scratch_shapes=[pltpu.VMEM((D_MODEL, 2 * D_FF), F32), pltpu.VMEM((D_FF, D_MODEL), F32),
                        pltpu.VMEM((D_MODEL, 2 * D_FF), BF16), pltpu.VMEM((D_FF, D_MODEL), BF16),
                        pltpu.SemaphoreType.DMA((2,))],
    )
    return pl.pallas_call(
        functools.partial(_moe_kernel, e0=e0),
        grid_spec=grid_spec,
        out_shape=jax.ShapeDtypeStruct((n_rows, D_MODEL // 2), U32),
        compiler_params=pltpu.CompilerParams(dimension_semantics=("arbitrary",), vmem_limit_bytes=VMEM_LIMIT),
        name="moe_experts",
    )(blk_e, blk_valid, blk_next, xs, w_gu.reshape(n_we, D_MODEL, 2 * D_FF), b_gu.reshape(n_we, 1, 2 * D_FF),
      w_d.reshape(n_we, D_FF, D_MODEL), b_d.reshape(n_we, 1, D_MODEL))


def _ple_kernel(x1_ref, y0_ref, y1_ref, y2_ref, y3_ref, gates_ref, p_ref, g_ref, wg_ref, wp_ref, *rest):
    out_ref, wg_bf, wp_bf = rest[-3:]

    @pl.when(pl.program_id(0) == 0)
    def _():
        wg_bf[...] = wg_ref[...].astype(BF16)
        wp_bf[...] = wp_ref[...].astype(BF16)

    x2 = x1_ref[...]
    gates = gates_ref[...]
    for kk, y_ref in enumerate((y0_ref, y1_ref, y2_ref, y3_ref)):
        x2 = x2 + _unpack_halves(y_ref[...]) * gates[:, kk:kk + 1]
    hp = _rms(x2, g_ref[...]).astype(BF16)
    gate = jax.nn.sigmoid(jnp.dot(hp, wg_bf[...], preferred_element_type=F32))
    pp = jnp.dot(p_ref[...].astype(BF16), wp_bf[...], preferred_element_type=F32)
    out_ref[...] = x2 + gate * pp


def _ple(x1, tok0, n, y_tok, y0, gates_t, p_all, p0, g_ple, w_gate, w_proj, rows, out_into=None):
    assert n % rows == 0 and tok0 % rows == 0 and y0 % rows == 0 and p0 % rows == 0
    const = lambda shape: pl.BlockSpec(shape, lambda i: (0,) * len(shape))
    tok_blk = lambda width: pl.BlockSpec((rows, width), lambda i: (tok0 // rows + i, 0))
    y_blk = lambda kk: pl.BlockSpec((rows, D_MODEL // 2), lambda i: ((y0 + kk * n) // rows + i, 0))
    extra_specs, extra_args, aliases = [], [], {}
    if out_into is not None:
        assert out_into.shape == x1.shape
        extra_specs, extra_args, aliases = [pl.BlockSpec(memory_space=pl.ANY)], [out_into], {10: 0}
    return pl.pallas_call(
        _ple_kernel,
        grid=(n // rows,),
        in_specs=[tok_blk(D_MODEL), y_blk(0), y_blk(1), y_blk(2), y_blk(3), tok_blk(TOP_K),
                  pl.BlockSpec((rows, PLE_DIM), lambda i: (p0 // rows + i, 0)),
                  const((1, D_MODEL)), const((D_MODEL, D_MODEL)), const((PLE_DIM, D_MODEL))] + extra_specs,
        out_specs=tok_blk(D_MODEL),
        out_shape=jax.ShapeDtypeStruct(x1.shape, F32),
        scratch_shapes=[pltpu.VMEM((D_MODEL, D_MODEL), BF16), pltpu.VMEM((PLE_DIM, D_MODEL), BF16)],
        input_output_aliases=aliases,
        compiler_params=pltpu.CompilerParams(dimension_semantics=("arbitrary",), vmem_limit_bytes=VMEM_LIMIT),
        name="combine_ple",
    )(x1, y_tok, y_tok, y_tok, y_tok, gates_t, p_all, g_ple, w_gate, w_proj, *extra_args)


def _rope_tables_const(seq):
    half = ROT_DIM // 2
    d = np.arange(LANES) % HEAD_DIM
    inv = (np.float64(ROPE_THETA) ** (-np.arange(half, dtype=np.float64) / half)).astype(np.float32)
    inv_lane = np.where(d < ROT_DIM, inv[d % half], np.float32(0.0)).astype(np.float32)
    ang = np.arange(seq, dtype=np.float32)[:, None] * inv_lane[None, :]
    cos = np.cos(ang.astype(np.float64)).astype(np.float32)
    sin = np.sin(ang.astype(np.float64)).astype(np.float32)
    sin_a = np.where((d >= half) & (d < ROT_DIM), sin, np.float32(0.0))
    sin_b = np.where(d < half, -sin, np.float32(0.0))
    return tuple(jnp.asarray(a) for a in (cos, sin_a, sin_b))


def _rope_tables(pos):
    half = ROT_DIM // 2
    d = np.arange(LANES) % HEAD_DIM
    inv = ROPE_THETA ** (-jnp.arange(half, dtype=F32) / half)
    inv_lane = jnp.where(d < ROT_DIM, inv[d % half], 0.0)
    ang = pos.astype(F32)[:, None] * inv_lane[None, :]
    cos, sin = jnp.cos(ang), jnp.sin(ang)
    sin_a = jnp.where((d >= half) & (d < ROT_DIM), sin, 0.0)
    sin_b = jnp.where(d < half, -sin, 0.0)
    return cos, sin_a, sin_b


def _layer(layer, xp, xs, ck, cv, st, p_prompt_all, p_sample_all, past_len, refine_tail, tabs_p, tabs_s, finish_state,
           norm_attn, w_in, q_norm, k_norm, attn_sinks, w_pool, pool_scale, w_out,
           norm_ffn, w_router, b_router, w_gate_up_all, b_gate_up_all, w_down_all, b_down_all,
           norm_ple, w_ple_gate, w_ple_proj):
    t, ns = xp.shape[0], xs.shape[0]
    g_attn = norm_attn.reshape(1, D_MODEL)
    qn = jnp.tile(q_norm, 2).reshape(1, LANES)
    kn = jnp.tile(k_norm, 2).reshape(1, LANES)
    lane = np.arange(LANES)
    hmean = jnp.asarray((lane[:, None] // HEAD_DIM == lane[None, :] // HEAD_DIM) / HEAD_DIM, F32)
    pscale = pool_scale.reshape(1, POOL_WIDTH)
    sink8 = jnp.broadcast_to(attn_sinks.reshape(2, Q_TILES).T.reshape(8, 1), (8, LANES))

    mix_args = (g_attn, w_in, qn, kn, hmean, w_pool, pscale, w_out)
    x1p, nk_p, nv_p, nu_p = _mixer_prompt(xp, 0, MIX_ROWS, BF16, attn_sinks, tabs_p, *mix_args)
    if refine_tail:
        x1p, nk_p, nv_p, nu_p = _mixer_prompt(xp, t - 2 * MIX_TAIL_ROWS, MIX_TAIL_ROWS, F32, attn_sinks, tabs_p,
                                              *mix_args, x1_into=x1p)
    st_t = jnp.transpose(st, (1, 0, 2))
    x1s, nk_s, nv_s, nst_t = _mixer_sample(xs, ck.reshape(ns, -1, KV_WIDTH), cv.reshape(ns, -1, KV_WIDTH), st_t,
                                           past_len, tabs_s, sink8, *mix_args)

    g_ffn = norm_ffn.reshape(1, D_MODEL)
    wr_t = w_router.T
    br = b_router.reshape(N_EXPERTS, 1)
    n_tok = t + ns
    bm = MOE_BM
    n_blocks = -(-(n_tok * TOP_K + N_EXPERTS * (bm - 1)) // bm)
    hf_p, idx_p, gate_p, pos_p, cnt_p = _route(x1p, g_ffn, wr_t, br, jnp.zeros((N_EXPERTS, 1), F32),
                                               ROUTE_ROWS, bm, n_blocks)[:5]
    hf_s, idx_s, gate_s, pos_s, _, pad_start, blk_e, blk_valid, blk_next = _route(x1s, g_ffn, wr_t, br, cnt_p, ns,
                                                                                  bm, n_blocks)
    idx = jnp.concatenate([idx_p, idx_s], axis=1)
    pos = jnp.concatenate([pos_p, pos_s], axis=1)
    blk_e, blk_valid, blk_next = (a[0, :n_blocks] for a in (blk_e, blk_valid, blk_next))
    state = finish_state(nk_p, nv_p, nu_p[POOL_HALO - POOL_PREFIX:], nk_s, nv_s, jnp.transpose(nst_t, (1, 0, 2)))
    blk_e, state = lax.optimization_barrier((blk_e, state))
    start_of = jnp.sum(jnp.where(idx[None] == jnp.arange(N_EXPERTS, dtype=I32)[:, None, None],
                                 pad_start.reshape(N_EXPERTS, 1, 1), 0), axis=0)
    dest = start_of + pos

    xs_rows = _scatter_rows(hf_p, hf_s, dest, n_blocks * bm)
    y = _moe(xs_rows, blk_e, blk_valid, blk_next, layer, w_gate_up_all, b_gate_up_all, w_down_all, b_down_all)

    tq = t // COMBINE_PARTS
    unit = SC_WORKERS * SC_CHUNK * 2
    assert t % COMBINE_PARTS == 0 and (TOP_K * tq) % unit == 0 and tq % PLE_ROWS == 0
    g_ple = norm_ple.reshape(1, D_MODEL)
    gates_p = gate_p.T
    yp = None
    for part in range(COMBINE_PARTS):
        back = dest[:, part * tq:(part + 1) * tq].reshape(-1)
        if part == COMBINE_PARTS - 1:
            back = jnp.concatenate([back, dest[:, t:].reshape(-1)])
            back = jnp.concatenate([back, jnp.arange(-back.shape[0] % unit, dtype=I32)])
        y_part = _gather_rows(y, back)
        yp = _ple(x1p, part * tq, tq, y_part, 0, gates_p, p_prompt_all, layer * t + part * tq, g_ple,
                  w_ple_gate, w_ple_proj, PLE_ROWS, out_into=yp)
    ys = _ple(x1s, 0, ns, y_part, TOP_K * tq, gate_s.T, p_sample_all, layer * ns, g_ple, w_ple_gate, w_ple_proj, ns)
    return yp, ys, state


def kernel(x_prompt, x_sample, cache_k, cache_v, state_pool, p_prompt, p_sample, norm_attn, w_in, q_norm, k_norm,
           attn_sinks, w_pool, pool_scale, w_out, norm_ffn, w_router, b_router, w_gate_up, b_gate_up, w_down, b_down,
           norm_ple, w_ple_gate, w_ple_proj):
    depth = norm_attn.shape[0]
    batch, seq, d = x_prompt.shape
    ns, dec_seq, _ = x_sample.shape
    wb = cache_k.shape[2]
    assert batch == 1 and dec_seq == 1 and d == D_MODEL and wb == WINDOW
    assert cache_k.shape[3:] == (N_KV_HEADS, HEAD_DIM) and state_pool.shape[2:] == (POOL_PREFIX, POOL_WIDTH)
    past_len = PAST_LEN
    yp = x_prompt.reshape(seq, d)
    ys = x_sample.reshape(ns, d)
    p_prompt_all = p_prompt.reshape(depth * seq, PLE_DIM)
    p_sample_all = p_sample.reshape(depth * ns, PLE_DIM)
    tabs_p = _rope_tables_const(seq)
    tabs_s = _rope_tables(jnp.full((1,), past_len))
    kv_p, kv_s = (1, WINDOW, N_KV_HEADS, HEAD_DIM), (ns, wb, N_KV_HEADS, HEAD_DIM)
    state_shapes = (kv_p, kv_p, (1, POOL_PREFIX, POOL_WIDTH), kv_s, kv_s, (ns, POOL_PREFIX, POOL_WIDTH))
    per_layer = []

    def shaped(*state):
        return tuple(a.reshape(shp) for a, shp in zip(state, state_shapes))

    def stacked(*state):
        prevs = list(zip(*per_layer)) or [()] * len(state_shapes)
        return tuple(jnp.stack(list(prev) + [cur]) for prev, cur in zip(prevs, shaped(*state)))

    for i in range(depth):
        yp, ys, state = _layer(i, yp, ys, cache_k[i], cache_v[i], state_pool[i], p_prompt_all, p_sample_all, past_len,
                               i < depth - 1, tabs_p, tabs_s, shaped if i < depth - 1 else stacked,
                               norm_attn[i], w_in[i], q_norm[i], k_norm[i], attn_sinks[i], w_pool[i], pool_scale[i],
                               w_out[i], norm_ffn[i], w_router[i], b_router[i], w_gate_up, b_gate_up, w_down, b_down,
                               norm_ple[i], w_ple_gate[i], w_ple_proj[i])
        per_layer.append(state)
    return (yp.reshape(batch, seq, d), ys.reshape(ns, dec_seq, d)) + tuple(state)
```

```python
import functools

import jax
import jax.numpy as jnp
import numpy as np
from jax import lax
from jax.experimental import pallas as pl
from jax.experimental.pallas import tpu as pltpu
from jax.experimental.pallas import tpu_sc as plsc

F32 = jnp.float32
BF16 = jnp.bfloat16
U32 = jnp.uint32
I32 = jnp.int32

D_MODEL = 1024
HEAD_DIM = 64
N_HEADS = 8
N_KV_HEADS = 2
GROUP = N_HEADS // N_KV_HEADS
ATTN_WIDTH = N_HEADS * HEAD_DIM
KV_WIDTH = N_KV_HEADS * HEAD_DIM
POOL_WIDTH = 512
POOL_WINDOWS = (2, 4, 8, 16)
POOL_GC = POOL_WIDTH // len(POOL_WINDOWS)
POOL_PREFIX = max(POOL_WINDOWS) - 1
POOL_HALO = POOL_PREFIX + 1
POOL_PAD = 8
IN_WIDTH = ATTN_WIDTH + 2 * KV_WIDTH + POOL_WIDTH
WINDOW = 128
ROPE_THETA = 500000.0
ROT_DIM = HEAD_DIM // 4
N_EXPERTS = 32
TOP_K = 4
D_FF = 1024
SWIGLU_ALPHA = 1.702
SWIGLU_LIMIT = 7.0
PLE_DIM = 256
PAST_LEN = 16384
EPS = 1e-5
NEG_INF = -1e30

LANES = 128
Q_TILES = ATTN_WIDTH // LANES

MIX_ROWS = 1024
MIX_TAIL_ROWS = 128
SAMPLE_CHUNK = 16
ROUTE_ROWS = 512
MOE_BM = 512
SC_CORES = 2
SC_SUBCORES = 16
SC_WORKERS = SC_CORES * SC_SUBCORES
SC_CHUNK = 64
PLE_ROWS = 512
COMBINE_PARTS = 2
VMEM_LIMIT = 56 * 1024 * 1024


def _rms(x, g):
    return x * lax.rsqrt(jnp.mean(x * x, axis=-1, keepdims=True) + EPS) * g


def _pack_halves(x):
    w = x.shape[1] // 2
    lo = lax.bitcast_convert_type(x[:, :w].astype(BF16).astype(F32), U32) >> 16
    hi = lax.bitcast_convert_type(x[:, w:].astype(BF16).astype(F32), U32) & jnp.uint32(0xFFFF0000)
    return lo | hi


def _unpack_halves(packed):
    lo = lax.bitcast_convert_type(packed << 16, F32)
    hi = lax.bitcast_convert_type(packed & jnp.uint32(0xFFFF0000), F32)
    return jnp.concatenate([lo, hi], axis=1)


def _mm(a, b, nt=False):
    dims = (((1,), (1 if nt else 0,)), ((), ()))
    if b.dtype == F32:
        return lax.dot_general(a.astype(F32), b, dims, preferred_element_type=F32, precision=lax.Precision.HIGHEST)
    return lax.dot_general(a.astype(BF16), b, dims, preferred_element_type=F32)


def _head_norm_rope(t, hmean, gain, cos, sin_a, sin_b):
    t = t * lax.rsqrt(_mm(t * t, hmean) + EPS) * gain
    return t * cos + pltpu.roll(t, ROT_DIM // 2, axis=1) * sin_a + pltpu.roll(t, LANES - ROT_DIM // 2, axis=1) * sin_b


def _softmax_pv(s, sink, v, ones):
    m = jnp.maximum(jnp.max(s, axis=-1, keepdims=True), sink)
    e = jnp.exp(s - m).astype(v.dtype)
    den = _mm(e, ones) + jnp.exp(sink - m)
    return _mm(e, v) / den


def _prep_mixer_weights(win_ref, wout_ref, win_c, wout_c):
    cdt = win_c.dtype
    for j in range(Q_TILES):
        for s in range(2):
            src = (j + Q_TILES * s) * HEAD_DIM
            dst = j * LANES + s * HEAD_DIM
            win_c[:, dst:dst + HEAD_DIM] = win_ref[:, src:src + HEAD_DIM].astype(cdt)
            wout_c[dst:dst + HEAD_DIM, :] = wout_ref[src:src + HEAD_DIM, :].astype(cdt)
    win_c[:, ATTN_WIDTH:] = win_ref[:, ATTN_WIDTH:].astype(cdt)
    wout_c[ATTN_WIDTH:, :] = wout_ref[ATTN_WIDTH:, :].astype(cdt)


def _mixer_prompt_kernel(sinks_ref, x_ref, cos_ref, sa_ref, sb_ref, g_ref, win_ref, qn_ref, kn_ref, hm_ref,
                         wpool_ref, pscale_ref, wout_ref, *rest, row_offset, aliased):
    (x1_ref, klast_ref, vlast_ref, ulast_ref, kprev, vprev, uext, mix, win_c, wout_c,
     *lvl) = rest[1:] if aliased else rest
    i = pl.program_id(0)
    rows = x_ref.shape[0]
    n_sub = rows // WINDOW
    cdt = win_c.dtype
    row0 = row_offset + i * rows

    @pl.when(i == 0)
    def _():
        _prep_mixer_weights(win_ref, wout_ref, win_c, wout_c)
        kprev[...] = jnp.zeros_like(kprev)
        vprev[...] = jnp.zeros_like(vprev)
        uext[0:POOL_PAD + POOL_HALO, :] = jnp.zeros((POOL_PAD + POOL_HALO, POOL_WIDTH), F32)
        for buf in lvl:
            buf[0:POOL_PAD, :] = jnp.zeros((POOL_PAD, POOL_GC), F32)

    x = x_ref[...]
    proj = _mm(_rms(x, g_ref[...]), win_c[...])
    cos, sin_a, sin_b = cos_ref[...], sa_ref[...], sb_ref[...]

    n_t = Q_TILES + 1
    t_all = jnp.concatenate([proj[:, j * LANES:(j + 1) * LANES] for j in range(n_t)], axis=0)
    t3 = (t_all * lax.rsqrt(_mm(t_all * t_all, hm_ref[...].astype(cdt)) + EPS)).reshape(n_t, rows, LANES)
    t3 = jnp.concatenate([t3[:Q_TILES] * (qn_ref[...] * HEAD_DIM ** -0.5), t3[Q_TILES:] * kn_ref[...]], axis=0)
    t2 = t3.reshape(n_t * rows, LANES)
    t3 = (t3 * cos + pltpu.roll(t2, ROT_DIM // 2, axis=1).reshape(n_t, rows, LANES) * sin_a
          + pltpu.roll(t2, LANES - ROT_DIM // 2, axis=1).reshape(n_t, rows, LANES) * sin_b)
    q3 = t3[:Q_TILES]
    k = t3[Q_TILES]
    v = proj[:, ATTN_WIDTH + KV_WIDTH:ATTN_WIDTH + 2 * KV_WIDTH]
    u = proj[:, ATTN_WIDTH + 2 * KV_WIDTH:]
    klast_ref[...] = k[rows - WINDOW:, :]
    vlast_ref[...] = v[rows - WINDOW:, :]
    ulast_ref[...] = u[rows - POOL_HALO:, :]
    k_c = k.astype(cdt)
    v_c = jnp.concatenate([v.astype(cdt), jnp.ones((rows, LANES), cdt)], axis=1)
    v_first = jnp.concatenate([vprev[...], jnp.ones((WINDOW, LANES), cdt)], axis=1)

    lane = lax.broadcasted_iota(I32, (WINDOW, LANES), 1)
    left = (lane < HEAD_DIM)[None]
    qi = lax.broadcasted_iota(I32, (WINDOW, 2 * WINDOW), 0)
    kj = lax.broadcasted_iota(I32, (WINDOW, 2 * WINDOW), 1)
    band = (kj - qi >= 1) & (kj - qi <= WINDOW)
    sink3 = jnp.concatenate([jnp.full((1, 1, 1), sinks_ref[j + Q_TILES * s], F32)
                             for j in range(Q_TILES) for s in range(2)], axis=0)
    n_g = 2 * Q_TILES

    v_cats, masks, scores = [], [], []
    for c in range(n_sub):
        r0 = c * WINDOW
        if c == 0:
            k_cat = jnp.concatenate([kprev[...], k_c[0:WINDOW]], axis=0)
            v_cats.append(jnp.concatenate([v_first, v_c[0:WINDOW]], axis=0))
            masks.append(band & (kj + (row0 - WINDOW) >= 0))
        else:
            k_cat = k_c[r0 - WINDOW:r0 + WINDOW]
            v_cats.append(v_c[r0 - WINDOW:r0 + WINDOW])
            masks.append(band)
        q_c = q3[:, r0:r0 + WINDOW, :]
        q_all = jnp.concatenate([jnp.where(left, q_c, 0.0), jnp.where(left, 0.0, q_c)], axis=1)
        scores.append(_mm(q_all.reshape(n_g * WINDOW, LANES), k_cat, nt=True))
    kprev[...] = k_c[rows - WINDOW:]
    vprev[...] = v[rows - WINDOW:].astype(cdt)

    probs, maxes = [], []
    for c in range(n_sub):
        s = jnp.where(masks[c][None], scores[c].reshape(n_g, WINDOW, 2 * WINDOW), NEG_INF)
        m = jnp.maximum(jnp.max(s, axis=-1, keepdims=True), sink3)
        probs.append(jnp.exp(s - m).astype(cdt).reshape(n_g * WINDOW, 2 * WINDOW))
        maxes.append(m)
    applied = [_mm(probs[c], v_cats[c]) for c in range(n_sub)]
    for c in range(n_sub):
        r0 = c * WINDOW
        den = applied[c][:, LANES:] + jnp.exp(sink3 - maxes[c]).reshape(n_g * WINDOW, 1)
        o = (applied[c][:, :LANES] / den).reshape(Q_TILES, 2 * WINDOW, LANES)
        a = jnp.where(left, o[:, :WINDOW], o[:, WINDOW:])
        for j in range(Q_TILES):
            mix[r0:r0 + WINDOW, j * LANES:(j + 1) * LANES] = a[j].astype(cdt)

    base = POOL_PAD + POOL_HALO
    ext = base + rows
    uext[base:ext, :] = u
    pos1 = (lax.broadcasted_iota(I32, (rows, 1), 0) + row0 + 1).astype(F32)
    lvl_of = {1: lvl[0:1], 2: lvl[1:3], 3: lvl[3:5]}
    for gi, w in enumerate(POOL_WINDOWS):
        cols = slice(gi * POOL_GC, (gi + 1) * POOL_GC)
        src, src_cols = uext, cols
        for level in range(1, gi + 2):
            sft = 1 << (level - 1)
            if level <= gi:
                dst = lvl_of[gi][(level - 1) % 2]
                dst[POOL_PAD:ext, :] = src[POOL_PAD:ext, src_cols] + src[POOL_PAD - sft:ext - sft, src_cols]
                src, src_cols = dst, slice(None)
            else:
                wsum = src[base:ext, src_cols] + src[base - sft:ext - sft, src_cols]
        d = wsum / jnp.minimum(pos1, float(w)) - u[:, cols]
        y = _mm(d, wpool_ref[gi].astype(cdt)) * pscale_ref[:, cols]
        mix[:, ATTN_WIDTH + gi * POOL_GC:ATTN_WIDTH + (gi + 1) * POOL_GC] = y.astype(cdt)
    uext[POOL_PAD:base, :] = u[rows - POOL_HALO:, :]

    x1_ref[...] = x + _mm(mix[...], wout_c[...])


def _mixer_prompt(x_full, row_offset, rows, cdt, sinks, tabs, g_attn, w_in, qn, kn, hmean, w_pool, pscale, w_out,
                  x1_into=None):
    t = x_full.shape[0] - row_offset
    assert t % rows == 0 and row_offset % rows == 0 and rows % WINDOW == 0 and rows >= POOL_HALO
    blk0, n_steps = row_offset // rows, t // rows
    const = lambda shape: pl.BlockSpec(shape, lambda i, *_: (0,) * len(shape))
    row_blk = lambda width: pl.BlockSpec((rows, width), lambda i, *_: (blk0 + i, 0))
    aliased = x1_into is not None
    if aliased:
        assert x1_into.shape == x_full.shape
        x1_spec = pl.BlockSpec((rows, D_MODEL), lambda i, *_: (blk0 + n_steps - 1, 0))
        x1_shape = x1_into.shape
        extra_specs, extra_args, aliases = [pl.BlockSpec(memory_space=pl.ANY)], [x1_into], {13: 0}
    else:
        x1_spec = pl.BlockSpec((rows, D_MODEL), lambda i, *_: (i, 0))
        x1_shape = (t, D_MODEL)
        extra_specs, extra_args, aliases = [], [], {}
    grid_spec = pltpu.PrefetchScalarGridSpec(
        num_scalar_prefetch=1,
        grid=(n_steps,),
        in_specs=[row_blk(D_MODEL), row_blk(LANES), row_blk(LANES), row_blk(LANES),
                  const((1, D_MODEL)), const((D_MODEL, IN_WIDTH)), const((1, LANES)), const((1, LANES)),
                  const((LANES, LANES)), const((len(POOL_WINDOWS), POOL_GC, POOL_GC)), const((1, POOL_WIDTH)),
                  const((D_MODEL, D_MODEL))] + extra_specs,
        out_specs=[x1_spec, const((WINDOW, KV_WIDTH)), const((WINDOW, KV_WIDTH)),
                   const((POOL_HALO, POOL_WIDTH))],
        scratch_shapes=[pltpu.VMEM((WINDOW, KV_WIDTH), cdt), pltpu.VMEM((WINDOW, KV_WIDTH), cdt),
                        pltpu.VMEM((POOL_PAD + POOL_HALO + rows, POOL_WIDTH), F32), pltpu.VMEM((rows, D_MODEL), cdt),
                        pltpu.VMEM((D_MODEL, IN_WIDTH), cdt), pltpu.VMEM((D_MODEL, D_MODEL), cdt)]
        + [pltpu.VMEM((POOL_PAD + POOL_HALO + rows, POOL_GC), F32)] * 5,
    )
    return pl.pallas_call(
        functools.partial(_mixer_prompt_kernel, row_offset=row_offset, aliased=aliased),
        grid_spec=grid_spec,
        out_shape=[jax.ShapeDtypeStruct(x1_shape, F32), jax.ShapeDtypeStruct((WINDOW, KV_WIDTH), F32),
                   jax.ShapeDtypeStruct((WINDOW, KV_WIDTH), F32), jax.ShapeDtypeStruct((POOL_HALO, POOL_WIDTH), F32)],
        input_output_aliases=aliases,
        compiler_params=pltpu.CompilerParams(dimension_semantics=("arbitrary",), vmem_limit_bytes=VMEM_LIMIT),
        name="mixer_prompt",
    )(sinks, x_full, *tabs, g_attn, w_in, qn, kn, hmean, w_pool, pscale, w_out, *extra_args)


def _mixer_sample_kernel(x_ref, ck_ref, cv_ref, st_ref, cos_ref, sa_ref, sb_ref, sink8_ref, g_ref, win_ref, qn_ref,
                         kn_ref, hm_ref, wpool_ref, pscale_ref, wout_ref,
                         x1_ref, nk_ref, nv_ref, nst_ref, o8, win_c, wout_c, *, pos):
    nb = x_ref.shape[0]
    wb = ck_ref.shape[1]

    @pl.when(pl.program_id(0) == 0)
    def _():
        _prep_mixer_weights(win_ref, wout_ref, win_c, wout_c)

    x = x_ref[...]
    h = _rms(x, g_ref[...]).astype(BF16)
    proj = jnp.dot(h, win_c[...], preferred_element_type=F32)
    cos, sin_a, sin_b = cos_ref[...], sa_ref[...], sb_ref[...]
    hmean = hm_ref[...].astype(BF16)
    k = _head_norm_rope(proj[:, ATTN_WIDTH:ATTN_WIDTH + KV_WIDTH], hmean, kn_ref[...], cos, sin_a, sin_b)
    v = proj[:, ATTN_WIDTH + KV_WIDTH:ATTN_WIDTH + 2 * KV_WIDTH]
    u = proj[:, ATTN_WIDTH + 2 * KV_WIDTH:]

    nk_ref[:, 0:wb - 1, :] = ck_ref[:, 1:wb, :]
    nv_ref[:, 0:wb - 1, :] = cv_ref[:, 1:wb, :]
    for b in range(nb):
        nk_ref[b, wb - 1:wb, :] = k[b:b + 1, :]
        nv_ref[b, wb - 1:wb, :] = v[b:b + 1, :]

    r8 = lax.broadcasted_iota(I32, (nb * 8, LANES), 0)
    lane8 = lax.broadcasted_iota(I32, (nb * 8, LANES), 1)
    keep = (lane8 < HEAD_DIM) == (r8 % 2 == 0)
    rep = (lax.broadcasted_iota(I32, (nb * 8, nb), 0) // 8 == lax.broadcasted_iota(I32, (nb * 8, nb), 1)).astype(BF16)
    q8 = jnp.zeros((nb * 8, LANES), F32)
    scale = HEAD_DIM ** -0.5
    for j in range(Q_TILES):
        qt = _head_norm_rope(proj[:, j * LANES:(j + 1) * LANES], hmean, qn_ref[...], cos, sin_a, sin_b) * scale
        qrep = jnp.dot(rep, qt.astype(BF16), preferred_element_type=F32)
        q8 = jnp.where(keep & ((r8 % 8) // 2 == j), qrep, q8)
    q8 = q8.astype(BF16)

    sink8 = sink8_ref[:, 0:1]
    ones_bf = jnp.ones((wb, LANES), BF16)
    assert pos >= wb - 1 and wb <= WINDOW
    for b in range(nb):
        kb = nk_ref[b].astype(BF16)
        vb = nv_ref[b].astype(BF16)
        s = lax.dot_general(q8[b * 8:(b + 1) * 8], kb, (((1,), (1,)), ((), ())), preferred_element_type=F32)
        o8[b * 8:(b + 1) * 8, :] = _softmax_pv(s, sink8, vb, ones_bf)
    o8m = jnp.where(keep, o8[...], 0.0).astype(BF16)

    a_tiles = []
    sel_r = lax.broadcasted_iota(I32, (nb, nb * 8), 1)
    sel_b = lax.broadcasted_iota(I32, (nb, nb * 8), 0)
    for j in range(Q_TILES):
        sel = ((sel_r // 8 == sel_b) & ((sel_r % 8) // 2 == j)).astype(BF16)
        a_tiles.append(jnp.dot(sel, o8m, preferred_element_type=F32))

    z_tiles = []
    for gi, w in enumerate(POOL_WINDOWS):
        cols = slice(gi * POOL_GC, (gi + 1) * POOL_GC)
        wsum = u[:, cols]
        for sft in range(1, w):
            wsum = wsum + st_ref[POOL_PREFIX - sft, :, cols]
        d = wsum / float(min(pos + 1, w)) - u[:, cols]
        z_tiles.append(jnp.dot(d.astype(BF16), wpool_ref[gi].astype(BF16), preferred_element_type=F32)
                       * pscale_ref[:, cols])
    nst_ref[0:POOL_PREFIX - 1] = st_ref[1:POOL_PREFIX]
    nst_ref[POOL_PREFIX - 1] = u

    mixv = jnp.concatenate(a_tiles + z_tiles, axis=1).astype(BF16)
    x1_ref[...] = x + jnp.dot(mixv, wout_c[...], preferred_element_type=F32)


def _mixer_sample(x, ck, cv, st, pos, tabs, sink8, g_attn, w_in, qn, kn, hmean, w_pool, pscale, w_out):
    n, wb = ck.shape[0], ck.shape[1]
    nb = SAMPLE_CHUNK
    assert n % nb == 0
    const = lambda shape: pl.BlockSpec(shape, lambda i: (0,) * len(shape))
    cache_blk = pl.BlockSpec((nb, wb, KV_WIDTH), lambda i: (i, 0, 0))
    st_blk = pl.BlockSpec((POOL_PREFIX, nb, POOL_WIDTH), lambda i: (0, i, 0))
    x_blk = pl.BlockSpec((nb, D_MODEL), lambda i: (i, 0))
    return pl.pallas_call(
        functools.partial(_mixer_sample_kernel, pos=pos),
        grid=(n // nb,),
        in_specs=[x_blk, cache_blk, cache_blk, st_blk, const((1, LANES)), const((1, LANES)), const((1, LANES)),
                  const((8, LANES)), const((1, D_MODEL)), const((D_MODEL, IN_WIDTH)), const((1, LANES)),
                  const((1, LANES)), const((LANES, LANES)), const((len(POOL_WINDOWS), POOL_GC, POOL_GC)),
                  const((1, POOL_WIDTH)), const((D_MODEL, D_MODEL))],
        out_specs=[x_blk, cache_blk, cache_blk, st_blk],
        out_shape=[jax.ShapeDtypeStruct((n, D_MODEL), F32), jax.ShapeDtypeStruct(ck.shape, F32),
                   jax.ShapeDtypeStruct(cv.shape, F32), jax.ShapeDtypeStruct(st.shape, F32)],
        scratch_shapes=[pltpu.VMEM((nb * 8, LANES), F32), pltpu.VMEM((D_MODEL, IN_WIDTH), BF16),
                        pltpu.VMEM((D_MODEL, D_MODEL), BF16)],
        compiler_params=pltpu.CompilerParams(dimension_semantics=("arbitrary",), vmem_limit_bytes=VMEM_LIMIT),
        name="mixer_sample",
    )(x, ck, cv, st, *tabs, sink8, g_attn, w_in, qn, kn, hmean, w_pool, pscale, w_out)


def _block_plan(cnt, bm, n_lanes):
    e_sub = lax.broadcasted_iota(I32, (N_EXPERTS, LANES), 0)
    e_lane = lax.broadcasted_iota(I32, (N_EXPERTS, LANES), 1)
    padded = jnp.floor((cnt + (bm - 1)) / bm) * bm
    padded_lane = jnp.sum(jnp.where(e_sub == e_lane, padded, 0.0), axis=0, keepdims=True)
    pad_end = jnp.sum(jnp.where(e_lane <= e_sub, padded_lane, 0.0), axis=1, keepdims=True)
    pad_start = pad_end - padded
    blk_start = lax.broadcasted_iota(I32, (N_EXPERTS, n_lanes), 1).astype(F32) * bm
    blk_e = jnp.minimum(jnp.sum((pad_end <= blk_start).astype(F32), axis=0, keepdims=True), N_EXPERTS - 1.0)
    mine = lax.broadcasted_iota(I32, (N_EXPERTS, n_lanes), 0).astype(F32) == blk_e
    last = jnp.sum(jnp.where(mine, pad_start + cnt, 0.0), axis=0, keepdims=True)
    blk_valid = jnp.clip(last - blk_start[0:1], 0.0, float(bm))
    e_blk = lax.broadcasted_iota(I32, (N_EXPERTS, n_lanes), 0).astype(F32)
    later = jnp.min(jnp.where((e_blk > blk_e) & (cnt > 0.0), e_blk, float(N_EXPERTS)), axis=0, keepdims=True)
    blk_next = jnp.where(later < N_EXPERTS, later, -1.0)
    return pad_start, blk_e, blk_valid, blk_next


def _route_kernel(x1_ref, g_ref, wr_ref, br_ref, cnt_in_ref, hf_ref, idx_ref, gate_ref, pos_ref, cnt_ref,
                  pstart_ref, blke_ref, blkv_ref, blkn_ref, counts, *, bm):
    i = pl.program_id(0)
    rows = x1_ref.shape[0]

    @pl.when(i == 0)
    def _():
        counts[...] = cnt_in_ref[...]

    h = _rms(x1_ref[...], g_ref[...])
    hf_ref[...] = _pack_halves(h)

    logits = lax.dot_general(wr_ref[...], h, (((1,), (1,)), ((), ())), preferred_element_type=F32,
                             precision=lax.Precision.HIGHEST) + br_ref[...]
    eid = lax.broadcasted_iota(I32, (N_EXPERTS, rows), 0).astype(F32)
    work = logits
    vals, hots = [], []
    for kk in range(TOP_K):
        m = jnp.max(work, axis=0, keepdims=True)
        first = jnp.min(jnp.where(work == m, eid, float(N_EXPERTS)), axis=0, keepdims=True)
        hot = eid == first
        work = jnp.where(hot, -jnp.inf, work)
        vals.append(m)
        hots.append(hot)
        idx_ref[kk:kk + 1, :] = first.astype(I32)
    es = [jnp.exp(vv - vals[0]) for vv in vals]
    den = es[0] + es[1] + es[2] + es[3]
    for kk in range(TOP_K):
        gate_ref[kk:kk + 1, :] = es[kk] / den

    chosen = hots[0] | hots[1] | hots[2] | hots[3]
    before = (lax.broadcasted_iota(I32, (rows, rows), 0) < lax.broadcasted_iota(I32, (rows, rows), 1)).astype(BF16)
    rank = jnp.dot(chosen.astype(BF16), before, preferred_element_type=F32) + counts[...]
    for kk in range(TOP_K):
        pos_ref[kk:kk + 1, :] = jnp.sum(jnp.where(hots[kk], rank, 0.0), axis=0, keepdims=True).astype(I32)
    counts[...] = counts[...] + jnp.sum(chosen.astype(F32), axis=1, keepdims=True)
    cnt_ref[...] = counts[...]

    @pl.when(i == pl.num_programs(0) - 1)
    def _():
        pad_start, blk_e, blk_valid, blk_next = _block_plan(counts[...], bm, blke_ref.shape[1])
        pstart_ref[...] = pad_start.astype(I32)
        blke_ref[...] = blk_e.astype(I32)
        blkv_ref[...] = blk_valid.astype(I32)
        blkn_ref[...] = blk_next.astype(I32)


def _route(x1, g_ffn, wr_t, br, cnt_in, rows, bm, n_blocks):
    n = x1.shape[0]
    assert n % rows == 0
    nb_lanes = -(-n_blocks // LANES) * LANES
    const = lambda shape: pl.BlockSpec(shape, lambda i: (0,) * len(shape))
    tok_blk = pl.BlockSpec((TOP_K, rows), lambda i: (0, i))
    return pl.pallas_call(
        functools.partial(_route_kernel, bm=bm),
        grid=(n // rows,),
        in_specs=[pl.BlockSpec((rows, D_MODEL), lambda i: (i, 0)), const((1, D_MODEL)), const((N_EXPERTS, D_MODEL)),
                  const((N_EXPERTS, 1)), const((N_EXPERTS, 1))],
        out_specs=[pl.BlockSpec((rows, D_MODEL // 2), lambda i: (i, 0)), tok_blk, tok_blk, tok_blk,
                   const((N_EXPERTS, 1)), const((N_EXPERTS, 1)), const((1, nb_lanes)), const((1, nb_lanes)),
                   const((1, nb_lanes))],
        out_shape=[jax.ShapeDtypeStruct((n, D_MODEL // 2), U32), jax.ShapeDtypeStruct((TOP_K, n), I32),
                   jax.ShapeDtypeStruct((TOP_K, n), F32), jax.ShapeDtypeStruct((TOP_K, n), I32),
                   jax.ShapeDtypeStruct((N_EXPERTS, 1), F32), jax.ShapeDtypeStruct((N_EXPERTS, 1), I32),
                   jax.ShapeDtypeStruct((1, nb_lanes), I32), jax.ShapeDtypeStruct((1, nb_lanes), I32),
                   jax.ShapeDtypeStruct((1, nb_lanes), I32)],
        scratch_shapes=[pltpu.VMEM((N_EXPERTS, 1), F32)],
        compiler_params=pltpu.CompilerParams(dimension_semantics=("arbitrary",), vmem_limit_bytes=VMEM_LIMIT),
        name="route",
    )(x1, g_ffn, wr_t, br, cnt_in)


def _sc_mesh():
    return plsc.VectorSubcoreMesh(core_axis_name="core", subcore_axis_name="subcore")


def _sc_worker_id():
    return lax.axis_index("core") * SC_SUBCORES + lax.axis_index("subcore")


def _scatter_rows(xa, xb, dest, n_rows):
    ch = SC_CHUNK
    na, w = xa.shape
    n = na + xb.shape[0]
    nk = dest.shape[0]
    assert na % ch == 0 and n % ch == 0 and dest.shape[1] == n and xb.shape[1] == w and xa.dtype == xb.dtype
    n_chunks = n // ch
    dest_c = dest.reshape(nk, n_chunks, ch).transpose(1, 0, 2).reshape(-1)
    dma = pltpu.SemaphoreType.DMA

    @pl.kernel(out_type=jax.ShapeDtypeStruct((n_rows, w), xa.dtype), mesh=_sc_mesh(),
               scratch_types=[pltpu.VMEM((nk * ch,), I32), pltpu.VMEM((ch, w), xa.dtype), dma] + [dma] * nk)
    def scatter_kernel(xa_hbm, xb_hbm, d_hbm, o_hbm, idx_v, buf, sem_i, *sem_s):
        wid = _sc_worker_id()

        @pl.loop(0, -(-n_chunks // SC_WORKERS))
        def _(j):
            c = j * SC_WORKERS + wid

            @pl.when(c < n_chunks)
            def _():
                load_idx = pltpu.make_async_copy(d_hbm.at[pl.ds(c * (nk * ch), nk * ch)], idx_v, sem_i)
                load_idx.start()

                @pl.when(c < na // ch)
                def _():
                    pltpu.sync_copy(xa_hbm.at[pl.ds(c * ch, ch)], buf)

                @pl.when(c >= na // ch)
                def _():
                    pltpu.sync_copy(xb_hbm.at[pl.ds(c * ch - na, ch)], buf)

                load_idx.wait()
                puts = [pltpu.make_async_copy(buf, o_hbm.at[idx_v.at[pl.ds(kk * ch, ch)]], sem_s[kk])
                        for kk in range(nk)]
                for cp in puts:
                    cp.start()
                for cp in puts:
                    cp.wait()

    return scatter_kernel(xa, xb, dest_c)


def _gather_rows(src, idx):
    ch = SC_CHUNK
    m = idx.shape[0]
    w = src.shape[1]
    per = m // SC_WORKERS
    n_pairs = per // (2 * ch)
    assert m % SC_WORKERS == 0 and per % (2 * ch) == 0
    dma = pltpu.SemaphoreType.DMA

    @pl.kernel(out_type=jax.ShapeDtypeStruct((m, w), src.dtype), mesh=_sc_mesh(),
               scratch_types=[pltpu.VMEM((per,), I32), pltpu.VMEM((ch, w), src.dtype), pltpu.VMEM((ch, w), src.dtype),
                              dma, dma, dma, dma])
    def gather_kernel(s_hbm, i_hbm, o_hbm, idx_v, buf_a, buf_b, sem_ga, sem_gb, sem_wa, sem_wb):
        base = _sc_worker_id() * per
        pltpu.sync_copy(i_hbm.at[pl.ds(base, per)], idx_v)

        def fetch(j, buf, sem):
            return pltpu.make_async_copy(s_hbm.at[idx_v.at[pl.ds(j * ch, ch)]], buf, sem)

        def put(j, buf, sem):
            return pltpu.make_async_copy(buf, o_hbm.at[pl.ds(base + j * ch, ch)], sem)

        fetch(0, buf_a, sem_ga).start()

        @pl.loop(0, n_pairs)
        def _(p):
            j0 = 2 * p
            j1 = j0 + 1

            @pl.when(p > 0)
            def _():
                put(j1 - 2, buf_b, sem_wb).wait()

            fetch(j1, buf_b, sem_gb).start()
            fetch(j0, buf_a, sem_ga).wait()
            put(j0, buf_a, sem_wa).start()
            fetch(j1, buf_b, sem_gb).wait()
            put(j1, buf_b, sem_wb).start()
            put(j0, buf_a, sem_wa).wait()

            @pl.when(p + 1 < n_pairs)
            def _():
                fetch(j0 + 2, buf_a, sem_ga).start()

        put(2 * n_pairs - 1, buf_b, sem_wb).wait()

    return gather_kernel(src, idx)


def _moe_kernel(blk_e_ref, blk_valid_ref, blk_next_ref, xs_ref, wgu_hbm, bgu_ref, wd_hbm, bd_ref, y_ref,
                wgu_f32, wd_f32, wgu_bf, wd_bf, sems, *, e0):
    i = pl.program_id(0)
    e = blk_e_ref[i]
    n_valid = blk_valid_ref[i]
    used = n_valid > 0
    new_expert = (i == 0) | (blk_e_ref[jnp.maximum(i - 1, 0)] != e)

    def weight_copies(expert):
        return (pltpu.make_async_copy(wgu_hbm.at[e0 + expert], wgu_f32, sems.at[0]),
                pltpu.make_async_copy(wd_hbm.at[e0 + expert], wd_f32, sems.at[1]))

    @pl.when(used & (i == 0))
    def _():
        for cp in weight_copies(e):
            cp.start()

    @pl.when(used & new_expert)
    def _():
        for cp in weight_copies(e):
            cp.wait()
        wgu_bf[...] = wgu_f32[...].astype(BF16)
        wd_bf[...] = wd_f32[...].astype(BF16)
        nxt = blk_next_ref[i]

        @pl.when(nxt >= 0)
        def _():
            for cp in weight_copies(nxt):
                cp.start(priority=1)

    @pl.when(used)
    def _():
        valid = lax.broadcasted_iota(I32, xs_ref.shape, 0) < n_valid
        xb = _unpack_halves(jnp.where(valid, xs_ref[...], jnp.uint32(0))).astype(BF16)
        gu = jnp.dot(xb, wgu_bf[...], preferred_element_type=F32) + bgu_ref[0]
        g = jnp.minimum(gu[:, :D_FF], SWIGLU_LIMIT)
        up = jnp.clip(gu[:, D_FF:], -SWIGLU_LIMIT, SWIGLU_LIMIT)
        act = (up + 1.0) * (g * jax.nn.sigmoid(SWIGLU_ALPHA * g))
        y_ref[...] = _pack_halves(jnp.dot(act.astype(BF16), wd_bf[...], preferred_element_type=F32) + bd_ref[0])

    @pl.when(jnp.logical_not(used))
    def _():
        y_ref[...] = jnp.zeros_like(y_ref)


def _moe(xs, blk_e, blk_valid, blk_next, layer, w_gu, b_gu, w_d, b_d):
    n_rows = xs.shape[0]
    bm = MOE_BM
    assert n_rows % bm == 0
    e0 = layer * N_EXPERTS
    n_we = w_gu.shape[0] * w_gu.shape[1]
    any_spec = pl.BlockSpec(memory_space=pl.ANY)
    grid_spec = pltpu.PrefetchScalarGridSpec(
        num_scalar_prefetch=3,
        grid=(n_rows // bm,),
        in_specs=[pl.BlockSpec((bm, D_MODEL // 2), lambda i, be, bv, bn: (i, 0)),
                  any_spec,
                  pl.BlockSpec((1, 1, 2 * D_FF), lambda i, be, bv, bn: (e0 + be[i], 0, 0)),
                  any_spec,
                  pl.BlockSpec((1, 1, D_MODEL), lambda i, be, bv, bn: (e0 + be[i], 0, 0))],
        out_specs=pl.BlockSpec((bm, D_MODEL // 2), lambda i, be, bv, bn: (i, 0)),
        scratch_shapes=[pltpu.VMEM((D_MODEL, 2 * D_FF), F32), pltpu.VMEM((D_FF, D_MODEL), F32),
                        pltpu.VMEM((D_MODEL, 2 * D_FF), BF16), pltpu.VMEM((D_FF, D_MODEL), BF16),
                        pltpu.SemaphoreType.DMA((2,))],
    )
    return pl.pallas_call(
        functools.partial(_moe_kernel, e0=e0),
        grid_spec=grid_spec,
        out_shape=jax.ShapeDtypeStruct((n_rows, D_MODEL // 2), U32),
        compiler_params=pltpu.CompilerParams(dimension_semantics=("arbitrary",), vmem_limit_bytes=VMEM_LIMIT),
        name="moe_experts",
    )(blk_e, blk_valid, blk_next, xs, w_gu.reshape(n_we, D_MODEL, 2 * D_FF), b_gu.reshape(n_we, 1, 2 * D_FF),
      w_d.reshape(n_we, D_FF, D_MODEL), b_d.reshape(n_we, 1, D_MODEL))


def _ple_kernel(x1_ref, y0_ref, y1_ref, y2_ref, y3_ref, gates_ref, p_ref, g_ref, wg_ref, wp_ref, *rest):
    out_ref, wg_bf, wp_bf = rest[-3:]

    @pl.when(pl.program_id(0) == 0)
    def _():
        wg_bf[...] = wg_ref[...].astype(BF16)
        wp_bf[...] = wp_ref[...].astype(BF16)

    x2 = x1_ref[...]
    gates = gates_ref[...]
    for kk, y_ref in enumerate((y0_ref, y1_ref, y2_ref, y3_ref)):
        x2 = x2 + _unpack_halves(y_ref[...]) * gates[:, kk:kk + 1]
    hp = _rms(x2, g_ref[...]).astype(BF16)
    gate = jax.nn.sigmoid(jnp.dot(hp, wg_bf[...], preferred_element_type=F32))
    pp = jnp.dot(p_ref[...].astype(BF16), wp_bf[...], preferred_element_type=F32)
    out_ref[...] = x2 + gate * pp


def _ple(x1, tok0, n, y_tok, y0, gates_t, p_all, p0, g_ple, w_gate, w_proj, rows, out_into=None):
    assert n % rows == 0 and tok0 % rows == 0 and y0 % rows == 0 and p0 % rows == 0
    const = lambda shape: pl.BlockSpec(shape, lambda i: (0,) * len(shape))
    tok_blk = lambda width: pl.BlockSpec((rows, width), lambda i: (tok0 // rows + i, 0))
    y_blk = lambda kk: pl.BlockSpec((rows, D_MODEL // 2), lambda i: ((y0 + kk * n) // rows + i, 0))
    extra_specs, extra_args, aliases = [], [], {}
    if out_into is not None:
        assert out_into.shape == x1.shape
        extra_specs, extra_args, aliases = [pl.BlockSpec(memory_space=pl.ANY)], [out_into], {10: 0}
    return pl.pallas_call(
        _ple_kernel,
        grid=(n // rows,),
        in_specs=[tok_blk(D_MODEL), y_blk(0), y_blk(1), y_blk(2), y_blk(3), tok_blk(TOP_K),
                  pl.BlockSpec((rows, PLE_DIM), lambda i: (p0 // rows + i, 0)),
                  const((1, D_MODEL)), const((D_MODEL, D_MODEL)), const((PLE_DIM, D_MODEL))] + extra_specs,
        out_specs=tok_blk(D_MODEL),
        out_shape=jax.ShapeDtypeStruct(x1.shape, F32),
        scratch_shapes=[pltpu.VMEM((D_MODEL, D_MODEL), BF16), pltpu.VMEM((PLE_DIM, D_MODEL), BF16)],
        input_output_aliases=aliases,
        compiler_params=pltpu.CompilerParams(dimension_semantics=("arbitrary",), vmem_limit_bytes=VMEM_LIMIT),
        name="combine_ple",
    )(x1, y_tok, y_tok, y_tok, y_tok, gates_t, p_all, g_ple, w_gate, w_proj, *extra_args)


def _rope_tables_const(seq):
    half = ROT_DIM // 2
    d = np.arange(LANES) % HEAD_DIM
    inv = (np.float64(ROPE_THETA) ** (-np.arange(half, dtype=np.float64) / half)).astype(np.float32)
    inv_lane = np.where(d < ROT_DIM, inv[d % half], np.float32(0.0)).astype(np.float32)
    ang = np.arange(seq, dtype=np.float32)[:, None] * inv_lane[None, :]
    cos = np.cos(ang.astype(np.float64)).astype(np.float32)
    sin = np.sin(ang.astype(np.float64)).astype(np.float32)
    sin_a = np.where((d >= half) & (d < ROT_DIM), sin, np.float32(0.0))
    sin_b = np.where(d < half, -sin, np.float32(0.0))
    return tuple(jnp.asarray(a) for a in (cos, sin_a, sin_b))


def _rope_tables(pos):
    half = ROT_DIM // 2
    d = np.arange(LANES) % HEAD_DIM
    inv = ROPE_THETA ** (-jnp.arange(half, dtype=F32) / half)
    inv_lane = jnp.where(d < ROT_DIM, inv[d % half], 0.0)
    ang = pos.astype(F32)[:, None] * inv_lane[None, :]
    cos, sin = jnp.cos(ang), jnp.sin(ang)
    sin_a = jnp.where((d >= half) & (d < ROT_DIM), sin, 0.0)
    sin_b = jnp.where(d < half, -sin, 0.0)
    return cos, sin_a, sin_b


def _layer(layer, xp, xs, ck, cv, st, p_prompt_all, p_sample_all, past_len, refine_tail, tabs_p, tabs_s, finish_state,
           norm_attn, w_in, q_norm, k_norm, attn_sinks, w_pool, pool_scale, w_out,
           norm_ffn, w_router, b_router, w_gate_up_all, b_gate_up_all, w_down_all, b_down_all,
           norm_ple, w_ple_gate, w_ple_proj):
    t, ns = xp.shape[0], xs.shape[0]
    g_attn = norm_attn.reshape(1, D_MODEL)
    qn = jnp.tile(q_norm, 2).reshape(1, LANES)
    kn = jnp.tile(k_norm, 2).reshape(1, LANES)
    lane = np.arange(LANES)
    hmean = jnp.asarray((lane[:, None] // HEAD_DIM == lane[None, :] // HEAD_DIM) / HEAD_DIM, F32)
    pscale = pool_scale.reshape(1, POOL_WIDTH)
    sink8 = jnp.broadcast_to(attn_sinks.reshape(2, Q_TILES).T.reshape(8, 1), (8, LANES))

    mix_args = (g_attn, w_in, qn, kn, hmean, w_pool, pscale, w_out)
    x1p, nk_p, nv_p, nu_p = _mixer_prompt(xp, 0, MIX_ROWS, BF16, attn_sinks, tabs_p, *mix_args)
    if refine_tail:
        x1p, nk_p, nv_p, nu_p = _mixer_prompt(xp, t - 2 * MIX_TAIL_ROWS, MIX_TAIL_ROWS, F32, attn_sinks, tabs_p,
                                              *mix_args, x1_into=x1p)
    st_t = jnp.transpose(st, (1, 0, 2))
    x1s, nk_s, nv_s, nst_t = _mixer_sample(xs, ck.reshape(ns, -1, KV_WIDTH), cv.reshape(ns, -1, KV_WIDTH), st_t,
                                           past_len, tabs_s, sink8, *mix_args)

    g_ffn = norm_ffn.reshape(1, D_MODEL)
    wr_t = w_router.T
    br = b_router.reshape(N_EXPERTS, 1)
    n_tok = t + ns
    bm = MOE_BM
    n_blocks = -(-(n_tok * TOP_K + N_EXPERTS * (bm - 1)) // bm)
    hf_p, idx_p, gate_p, pos_p, cnt_p = _route(x1p, g_ffn, wr_t, br, jnp.zeros((N_EXPERTS, 1), F32),
                                               ROUTE_ROWS, bm, n_blocks)[:5]
    hf_s, idx_s, gate_s, pos_s, _, pad_start, blk_e, blk_valid, blk_next = _route(x1s, g_ffn, wr_t, br, cnt_p, ns,
                                                                                  bm, n_blocks)
    idx = jnp.concatenate([idx_p, idx_s], axis=1)
    pos = jnp.concatenate([pos_p, pos_s], axis=1)
    blk_e, blk_valid, blk_next = (a[0, :n_blocks] for a in (blk_e, blk_valid, blk_next))
    state = finish_state(nk_p, nv_p, nu_p[POOL_HALO - POOL_PREFIX:], nk_s, nv_s, jnp.transpose(nst_t, (1, 0, 2)))
    blk_e, state = lax.optimization_barrier((blk_e, state))
    start_of = jnp.sum(jnp.where(idx[None] == jnp.arange(N_EXPERTS, dtype=I32)[:, None, None],
                                 pad_start.reshape(N_EXPERTS, 1, 1), 0), axis=0)
    dest = start_of + pos

    xs_rows = _scatter_rows(hf_p, hf_s, dest, n_blocks * bm)
    y = _moe(xs_rows, blk_e, blk_valid, blk_next, layer, w_gate_up_all, b_gate_up_all, w_down_all, b_down_all)

    tq = t // COMBINE_PARTS
    unit = SC_WORKERS * SC_CHUNK * 2
    assert t % COMBINE_PARTS == 0 and (TOP_K * tq) % unit == 0 and tq % PLE_ROWS == 0
    g_ple = norm_ple.reshape(1, D_MODEL)
    gates_p = gate_p.T
    yp = None
    for part in range(COMBINE_PARTS):
        back = dest[:, part * tq:(part + 1) * tq].reshape(-1)
        if part == COMBINE_PARTS - 1:
            back = jnp.concatenate([back, dest[:, t:].reshape(-1)])
            back = jnp.concatenate([back, jnp.arange(-back.shape[0] % unit, dtype=I32)])
        y_part = _gather_rows(y, back)
        yp = _ple(x1p, part * tq, tq, y_part, 0, gates_p, p_prompt_all, layer * t + part * tq, g_ple,
                  w_ple_gate, w_ple_proj, PLE_ROWS, out_into=yp)
    ys = _ple(x1s, 0, ns, y_part, TOP_K * tq, gate_s.T, p_sample_all, layer * ns, g_ple, w_ple_gate, w_ple_proj, ns)
    return yp, ys, state


def kernel(x_prompt, x_sample, cache_k, cache_v, state_pool, p_prompt, p_sample, norm_attn, w_in, q_norm, k_norm,
           attn_sinks, w_pool, pool_scale, w_out, norm_ffn, w_router, b_router, w_gate_up, b_gate_up, w_down, b_down,
           norm_ple, w_ple_gate, w_ple_proj):
    depth = norm_attn.shape[0]
    batch, seq, d = x_prompt.shape
    ns, dec_seq, _ = x_sample.shape
    wb = cache_k.shape[2]
    assert batch == 1 and dec_seq == 1 and d == D_MODEL and wb == WINDOW
    assert cache_k.shape[3:] == (N_KV_HEADS, HEAD_DIM) and state_pool.shape[2:] == (POOL_PREFIX, POOL_WIDTH)
    past_len = PAST_LEN
    yp = x_prompt.reshape(seq, d)
    ys = x_sample.reshape(ns, d)
    p_prompt_all = p_prompt.reshape(depth * seq, PLE_DIM)
    p_sample_all = p_sample.reshape(depth * ns, PLE_DIM)
    tabs_p = _rope_tables_const(seq)
    tabs_s = _rope_tables(jnp.full((1,), past_len))
    kv_p, kv_s = (1, WINDOW, N_KV_HEADS, HEAD_DIM), (ns, wb, N_KV_HEADS, HEAD_DIM)
    state_shapes = (kv_p, kv_p, (1, POOL_PREFIX, POOL_WIDTH), kv_s, kv_s, (ns, POOL_PREFIX, POOL_WIDTH))
    per_layer = []

    def shaped(*state):
        return tuple(a.reshape(shp) for a, shp in zip(state, state_shapes))

    def stacked(*state):
        prevs = list(zip(*per_layer)) or [()] * len(state_shapes)
        return tuple(jnp.stack(list(prev) + [cur]) for prev, cur in zip(prevs, shaped(*state)))

    for i in range(depth):
        yp, ys, state = _layer(i, yp, ys, cache_k[i], cache_v[i], state_pool[i], p_prompt_all, p_sample_all, past_len,
                               i < depth - 1, tabs_p, tabs_s, shaped if i < depth - 1 else stacked,
                               norm_attn[i], w_in[i], q_norm[i], k_norm[i], attn_sinks[i], w_pool[i], pool_scale[i],
                               w_out[i], norm_ffn[i], w_router[i], b_router[i], w_gate_up, b_gate_up, w_down, b_down,
                               norm_ple[i], w_ple_gate[i], w_ple_proj[i])
        per_layer.append(state)
    return (yp.reshape(batch, seq, d), ys.reshape(ns, dec_seq, d)) + tuple(state)
```

```python
import functools

import jax
import jax.numpy as jnp
import numpy as np
from jax import lax
from jax.experimental import pallas as pl
from jax.experimental.pallas import tpu as pltpu
from jax.experimental.pallas import tpu_sc as plsc

F32 = jnp.float32
BF16 = jnp.bfloat16
U32 = jnp.uint32
I32 = jnp.int32

D_MODEL = 1024
HEAD_DIM = 64
N_HEADS = 8
N_KV_HEADS = 2
GROUP = N_HEADS // N_KV_HEADS
ATTN_WIDTH = N_HEADS * HEAD_DIM
KV_WIDTH = N_KV_HEADS * HEAD_DIM
POOL_WIDTH = 512
POOL_WINDOWS = (2, 4, 8, 16)
POOL_GC = POOL_WIDTH // len(POOL_WINDOWS)
POOL_PREFIX = max(POOL_WINDOWS) - 1
POOL_HALO = POOL_PREFIX + 1
POOL_PAD = 8
IN_WIDTH = ATTN_WIDTH + 2 * KV_WIDTH + POOL_WIDTH
WINDOW = 128
ROPE_THETA = 500000.0
ROT_DIM = HEAD_DIM // 4
N_EXPERTS = 32
TOP_K = 4
D_FF = 1024
SWIGLU_ALPHA = 1.702
SWIGLU_LIMIT = 7.0
PLE_DIM = 256
PAST_LEN = 16384
EPS = 1e-5
NEG_INF = -1e30

LANES = 128
Q_TILES = ATTN_WIDTH // LANES

MIX_ROWS = 1024
MIX_TAIL_ROWS = 128
SAMPLE_CHUNK = 16
ROUTE_ROWS = 1024
MOE_BM = 512
SC_CORES = 2
SC_SUBCORES = 16
SC_WORKERS = SC_CORES * SC_SUBCORES
SC_CHUNK = 64
PLE_ROWS = 1024
COMBINE_PARTS = 2
VMEM_LIMIT = 56 * 1024 * 1024


def _rms(x, g):
    return x * lax.rsqrt(jnp.mean(x * x, axis=-1, keepdims=True) + EPS) * g


def _pack_halves(x):
    w = x.shape[1] // 2
    lo = lax.bitcast_convert_type(x[:, :w].astype(BF16).astype(F32), U32) >> 16
    hi = lax.bitcast_convert_type(x[:, w:].astype(BF16).astype(F32), U32) & jnp.uint32(0xFFFF0000)
    return lo | hi


def _unpack_halves(packed):
    lo = lax.bitcast_convert_type(packed << 16, F32)
    hi = lax.bitcast_convert_type(packed & jnp.uint32(0xFFFF0000), F32)
    return jnp.concatenate([lo, hi], axis=1)


def _mm(a, b, nt=False):
    dims = (((1,), (1 if nt else 0,)), ((), ()))
    if b.dtype == F32:
        return lax.dot_general(a.astype(F32), b, dims, preferred_element_type=F32, precision=lax.Precision.HIGHEST)
    return lax.dot_general(a.astype(BF16), b, dims, preferred_element_type=F32)


def _head_norm_rope(t, hmean, gain, cos, sin_a, sin_b):
    t = t * lax.rsqrt(_mm(t * t, hmean) + EPS) * gain
    return t * cos + pltpu.roll(t, ROT_DIM // 2, axis=1) * sin_a + pltpu.roll(t, LANES - ROT_DIM // 2, axis=1) * sin_b


def _softmax_pv(s, sink, v, ones):
    m = jnp.maximum(jnp.max(s, axis=-1, keepdims=True), sink)
    e = jnp.exp(s - m).astype(v.dtype)
    den = _mm(e, ones) + jnp.exp(sink - m)
    return _mm(e, v) / den


def _prep_mixer_weights(win_ref, wout_ref, win_c, wout_c):
    cdt = win_c.dtype
    for j in range(Q_TILES):
        for s in range(2):
            src = (j + Q_TILES * s) * HEAD_DIM
            dst = j * LANES + s * HEAD_DIM
            win_c[:, dst:dst + HEAD_DIM] = win_ref[:, src:src + HEAD_DIM].astype(cdt)
            wout_c[dst:dst + HEAD_DIM, :] = wout_ref[src:src + HEAD_DIM, :].astype(cdt)
    win_c[:, ATTN_WIDTH:] = win_ref[:, ATTN_WIDTH:].astype(cdt)
    wout_c[ATTN_WIDTH:, :] = wout_ref[ATTN_WIDTH:, :].astype(cdt)


def _mixer_prompt_kernel(sinks_ref, x_ref, cos_ref, sa_ref, sb_ref, g_ref, win_ref, qn_ref, kn_ref, hm_ref,
                         wpool_ref, pscale_ref, wout_ref, *rest, row_offset, aliased):
    (x1_ref, klast_ref, vlast_ref, ulast_ref, kprev, vprev, uext, mix, win_c, wout_c,
     *lvl) = rest[1:] if aliased else rest
    i = pl.program_id(0)
    rows = x_ref.shape[0]
    n_sub = rows // WINDOW
    cdt = win_c.dtype
    row0 = row_offset + i * rows

    @pl.when(i == 0)
    def _():
        _prep_mixer_weights(win_ref, wout_ref, win_c, wout_c)
        kprev[...] = jnp.zeros_like(kprev)
        vprev[...] = jnp.zeros_like(vprev)
        uext[0:POOL_PAD + POOL_HALO, :] = jnp.zeros((POOL_PAD + POOL_HALO, POOL_WIDTH), F32)
        for buf in lvl:
            buf[0:POOL_PAD, :] = jnp.zeros((POOL_PAD, POOL_GC), F32)

    x = x_ref[...]
    proj = _mm(_rms(x, g_ref[...]), win_c[...])
    cos, sin_a, sin_b = cos_ref[...], sa_ref[...], sb_ref[...]

    n_t = Q_TILES + 1
    t_all = jnp.concatenate([proj[:, j * LANES:(j + 1) * LANES] for j in range(n_t)], axis=0)
    t3 = (t_all * lax.rsqrt(_mm(t_all * t_all, hm_ref[...].astype(cdt)) + EPS)).reshape(n_t, rows, LANES)
    t3 = jnp.concatenate([t3[:Q_TILES] * (qn_ref[...] * HEAD_DIM ** -0.5), t3[Q_TILES:] * kn_ref[...]], axis=0)
    t2 = t3.reshape(n_t * rows, LANES)
    t3 = (t3 * cos + pltpu.roll(t2, ROT_DIM // 2, axis=1).reshape(n_t, rows, LANES) * sin_a
          + pltpu.roll(t2, LANES - ROT_DIM // 2, axis=1).reshape(n_t, rows, LANES) * sin_b)
    q3 = t3[:Q_TILES]
    k = t3[Q_TILES]
    v = proj[:, ATTN_WIDTH + KV_WIDTH:ATTN_WIDTH + 2 * KV_WIDTH]
    u = proj[:, ATTN_WIDTH + 2 * KV_WIDTH:]
    klast_ref[...] = k[rows - WINDOW:, :]
    vlast_ref[...] = v[rows - WINDOW:, :]
    ulast_ref[...] = u[rows - POOL_HALO:, :]
    k_c = k.astype(cdt)
    v_c = jnp.concatenate([v.astype(cdt), jnp.ones((rows, LANES), cdt)], axis=1)
    v_first = jnp.concatenate([vprev[...], jnp.ones((WINDOW, LANES), cdt)], axis=1)

    lane = lax.broadcasted_iota(I32, (WINDOW, LANES), 1)
    left = (lane < HEAD_DIM)[None]
    qi = lax.broadcasted_iota(I32, (WINDOW, 2 * WINDOW), 0)
    kj = lax.broadcasted_iota(I32, (WINDOW, 2 * WINDOW), 1)
    band = (kj - qi >= 1) & (kj - qi <= WINDOW)
    sink3 = jnp.concatenate([jnp.full((1, 1, 1), sinks_ref[j + Q_TILES * s], F32)
                             for j in range(Q_TILES) for s in range(2)], axis=0)
    n_g = 2 * Q_TILES

    v_cats, masks, scores = [], [], []
    for c in range(n_sub):
        r0 = c * WINDOW
        if c == 0:
            k_cat = jnp.concatenate([kprev[...], k_c[0:WINDOW]], axis=0)
            v_cats.append(jnp.concatenate([v_first, v_c[0:WINDOW]], axis=0))
            masks.append(band & (kj + (row0 - WINDOW) >= 0))
        else:
            k_cat = k_c[r0 - WINDOW:r0 + WINDOW]
            v_cats.append(v_c[r0 - WINDOW:r0 + WINDOW])
            masks.append(band)
        q_c = q3[:, r0:r0 + WINDOW, :]
        q_all = jnp.concatenate([jnp.where(left, q_c, 0.0), jnp.where(left, 0.0, q_c)], axis=1)
        scores.append(_mm(q_all.reshape(n_g * WINDOW, LANES), k_cat, nt=True))
    kprev[...] = k_c[rows - WINDOW:]
    vprev[...] = v[rows - WINDOW:].astype(cdt)

    probs, maxes = [], []
    for c in range(n_sub):
        s = jnp.where(masks[c][None], scores[c].reshape(n_g, WINDOW, 2 * WINDOW), NEG_INF)
        m = jnp.maximum(jnp.max(s, axis=-1, keepdims=True), sink3)
        probs.append(jnp.exp(s - m).astype(cdt).reshape(n_g * WINDOW, 2 * WINDOW))
        maxes.append(m)
    applied = [_mm(probs[c], v_cats[c]) for c in range(n_sub)]
    for c in range(n_sub):
        r0 = c * WINDOW
        den = applied[c][:, LANES:] + jnp.exp(sink3 - maxes[c]).reshape(n_g * WINDOW, 1)
        o = (applied[c][:, :LANES] / den).reshape(Q_TILES, 2 * WINDOW, LANES)
        a = jnp.where(left, o[:, :WINDOW], o[:, WINDOW:])
        for j in range(Q_TILES):
            mix[r0:r0 + WINDOW, j * LANES:(j + 1) * LANES] = a[j].astype(cdt)

    base = POOL_PAD + POOL_HALO
    ext = base + rows
    uext[base:ext, :] = u
    pos1 = (lax.broadcasted_iota(I32, (rows, 1), 0) + row0 + 1).astype(F32)
    lvl_of = {1: lvl[0:1], 2: lvl[1:3], 3: lvl[3:5]}
    for gi, w in enumerate(POOL_WINDOWS):
        cols = slice(gi * POOL_GC, (gi + 1) * POOL_GC)
        src, src_cols = uext, cols
        for level in range(1, gi + 2):
            sft = 1 << (level - 1)
            if level <= gi:
                dst = lvl_of[gi][(level - 1) % 2]
                dst[POOL_PAD:ext, :] = src[POOL_PAD:ext, src_cols] + src[POOL_PAD - sft:ext - sft, src_cols]
                src, src_cols = dst, slice(None)
            else:
                wsum = src[base:ext, src_cols] + src[base - sft:ext - sft, src_cols]
        d = wsum / jnp.minimum(pos1, float(w)) - u[:, cols]
        y = _mm(d, wpool_ref[gi].astype(cdt)) * pscale_ref[:, cols]
        mix[:, ATTN_WIDTH + gi * POOL_GC:ATTN_WIDTH + (gi + 1) * POOL_GC] = y.astype(cdt)
    uext[POOL_PAD:base, :] = u[rows - POOL_HALO:, :]

    x1_ref[...] = x + _mm(mix[...], wout_c[...])


def _mixer_prompt(x_full, row_offset, rows, cdt, sinks, tabs, g_attn, w_in, qn, kn, hmean, w_pool, pscale, w_out,
                  x1_into=None):
    t = x_full.shape[0] - row_offset
    assert t % rows == 0 and row_offset % rows == 0 and rows % WINDOW == 0 and rows >= POOL_HALO
    blk0, n_steps = row_offset // rows, t // rows
    const = lambda shape: pl.BlockSpec(shape, lambda i, *_: (0,) * len(shape))
    row_blk = lambda width: pl.BlockSpec((rows, width), lambda i, *_: (blk0 + i, 0))
    aliased = x1_into is not None
    if aliased:
        assert x1_into.shape == x_full.shape
        x1_spec = pl.BlockSpec((rows, D_MODEL), lambda i, *_: (blk0 + n_steps - 1, 0))
        x1_shape = x1_into.shape
        extra_specs, extra_args, aliases = [pl.BlockSpec(memory_space=pl.ANY)], [x1_into], {13: 0}
    else:
        x1_spec = pl.BlockSpec((rows, D_MODEL), lambda i, *_: (i, 0))
        x1_shape = (t, D_MODEL)
        extra_specs, extra_args, aliases = [], [], {}
    grid_spec = pltpu.PrefetchScalarGridSpec(
        num_scalar_prefetch=1,
        grid=(n_steps,),
        in_specs=[row_blk(D_MODEL), row_blk(LANES), row_blk(LANES), row_blk(LANES),
                  const((1, D_MODEL)), const((D_MODEL, IN_WIDTH)), const((1, LANES)), const((1, LANES)),
                  const((LANES, LANES)), const((len(POOL_WINDOWS), POOL_GC, POOL_GC)), const((1, POOL_WIDTH)),
                  const((D_MODEL, D_MODEL))] + extra_specs,
        out_specs=[x1_spec, const((WINDOW, KV_WIDTH)), const((WINDOW, KV_WIDTH)),
                   const((POOL_HALO, POOL_WIDTH))],
        scratch_shapes=[pltpu.VMEM((WINDOW, KV_WIDTH), cdt), pltpu.VMEM((WINDOW, KV_WIDTH), cdt),
                        pltpu.VMEM((POOL_PAD + POOL_HALO + rows, POOL_WIDTH), F32), pltpu.VMEM((rows, D_MODEL), cdt),
                        pltpu.VMEM((D_MODEL, IN_WIDTH), cdt), pltpu.VMEM((D_MODEL, D_MODEL), cdt)]
        + [pltpu.VMEM((POOL_PAD + POOL_HALO + rows, POOL_GC), F32)] * 5,
    )
    return pl.pallas_call(
        functools.partial(_mixer_prompt_kernel, row_offset=row_offset, aliased=aliased),
        grid_spec=grid_spec,
        out_shape=[jax.ShapeDtypeStruct(x1_shape, F32), jax.ShapeDtypeStruct((WINDOW, KV_WIDTH), F32),
                   jax.ShapeDtypeStruct((WINDOW, KV_WIDTH), F32), jax.ShapeDtypeStruct((POOL_HALO, POOL_WIDTH), F32)],
        input_output_aliases=aliases,
        compiler_params=pltpu.CompilerParams(dimension_semantics=("arbitrary",), vmem_limit_bytes=VMEM_LIMIT),
        name="mixer_prompt",
    )(sinks, x_full, *tabs, g_attn, w_in, qn, kn, hmean, w_pool, pscale, w_out, *extra_args)


def _mixer_sample_kernel(x_ref, ck_ref, cv_ref, st_ref, cos_ref, sa_ref, sb_ref, sink8_ref, g_ref, win_ref, qn_ref,
                         kn_ref, hm_ref, wpool_ref, pscale_ref, wout_ref,
                         x1_ref, nk_ref, nv_ref, nst_ref, o8, win_c, wout_c, *, pos):
    nb = x_ref.shape[0]
    wb = ck_ref.shape[1]

    @pl.when(pl.program_id(0) == 0)
    def _():
        _prep_mixer_weights(win_ref, wout_ref, win_c, wout_c)

    x = x_ref[...]
    h = _rms(x, g_ref[...]).astype(BF16)
    proj = jnp.dot(h, win_c[...], preferred_element_type=F32)
    cos, sin_a, sin_b = cos_ref[...], sa_ref[...], sb_ref[...]
    hmean = hm_ref[...].astype(BF16)
    k = _head_norm_rope(proj[:, ATTN_WIDTH:ATTN_WIDTH + KV_WIDTH], hmean, kn_ref[...], cos, sin_a, sin_b)
    v = proj[:, ATTN_WIDTH + KV_WIDTH:ATTN_WIDTH + 2 * KV_WIDTH]
    u = proj[:, ATTN_WIDTH + 2 * KV_WIDTH:]

    nk_ref[:, 0:wb - 1, :] = ck_ref[:, 1:wb, :]
    nv_ref[:, 0:wb - 1, :] = cv_ref[:, 1:wb, :]
    for b in range(nb):
        nk_ref[b, wb - 1:wb, :] = k[b:b + 1, :]
        nv_ref[b, wb - 1:wb, :] = v[b:b + 1, :]

    r8 = lax.broadcasted_iota(I32, (nb * 8, LANES), 0)
    lane8 = lax.broadcasted_iota(I32, (nb * 8, LANES), 1)
    keep = (lane8 < HEAD_DIM) == (r8 % 2 == 0)
    rep = (lax.broadcasted_iota(I32, (nb * 8, nb), 0) // 8 == lax.broadcasted_iota(I32, (nb * 8, nb), 1)).astype(BF16)
    q8 = jnp.zeros((nb * 8, LANES), F32)
    scale = HEAD_DIM ** -0.5
    for j in range(Q_TILES):
        qt = _head_norm_rope(proj[:, j * LANES:(j + 1) * LANES], hmean, qn_ref[...], cos, sin_a, sin_b) * scale
        qrep = jnp.dot(rep, qt.astype(BF16), preferred_element_type=F32)
        q8 = jnp.where(keep & ((r8 % 8) // 2 == j), qrep, q8)
    q8 = q8.astype(BF16)

    sink8 = sink8_ref[:, 0:1]
    ones_bf = jnp.ones((wb, LANES), BF16)
    assert pos >= wb - 1 and wb <= WINDOW
    for b in range(nb):
        kb = nk_ref[b].astype(BF16)
        vb = nv_ref[b].astype(BF16)
        s = lax.dot_general(q8[b * 8:(b + 1) * 8], kb, (((1,), (1,)), ((), ())), preferred_element_type=F32)
        o8[b * 8:(b + 1) * 8, :] = _softmax_pv(s, sink8, vb, ones_bf)
    o8m = jnp.where(keep, o8[...], 0.0).astype(BF16)

    a_tiles = []
    sel_r = lax.broadcasted_iota(I32, (nb, nb * 8), 1)
    sel_b = lax.broadcasted_iota(I32, (nb, nb * 8), 0)
    for j in range(Q_TILES):
        sel = ((sel_r // 8 == sel_b) & ((sel_r % 8) // 2 == j)).astype(BF16)
        a_tiles.append(jnp.dot(sel, o8m, preferred_element_type=F32))

    z_tiles = []
    for gi, w in enumerate(POOL_WINDOWS):
        cols = slice(gi * POOL_GC, (gi + 1) * POOL_GC)
        wsum = u[:, cols]
        for sft in range(1, w):
            wsum = wsum + st_ref[POOL_PREFIX - sft, :, cols]
        d = wsum / float(min(pos + 1, w)) - u[:, cols]
        z_tiles.append(jnp.dot(d.astype(BF16), wpool_ref[gi].astype(BF16), preferred_element_type=F32)
                       * pscale_ref[:, cols])
    nst_ref[0:POOL_PREFIX - 1] = st_ref[1:POOL_PREFIX]
    nst_ref[POOL_PREFIX - 1] = u

    mixv = jnp.concatenate(a_tiles + z_tiles, axis=1).astype(BF16)
    x1_ref[...] = x + jnp.dot(mixv, wout_c[...], preferred_element_type=F32)


def _mixer_sample(x, ck, cv, st, pos, tabs, sink8, g_attn, w_in, qn, kn, hmean, w_pool, pscale, w_out):
    n, wb = ck.shape[0], ck.shape[1]
    nb = SAMPLE_CHUNK
    assert n % nb == 0
    const = lambda shape: pl.BlockSpec(shape, lambda i: (0,) * len(shape))
    cache_blk = pl.BlockSpec((nb, wb, KV_WIDTH), lambda i: (i, 0, 0))
    st_blk = pl.BlockSpec((POOL_PREFIX, nb, POOL_WIDTH), lambda i: (0, i, 0))
    x_blk = pl.BlockSpec((nb, D_MODEL), lambda i: (i, 0))
    return pl.pallas_call(
        functools.partial(_mixer_sample_kernel, pos=pos),
        grid=(n // nb,),
        in_specs=[x_blk, cache_blk, cache_blk, st_blk, const((1, LANES)), const((1, LANES)), const((1, LANES)),
                  const((8, LANES)), const((1, D_MODEL)), const((D_MODEL, IN_WIDTH)), const((1, LANES)),
                  const((1, LANES)), const((LANES, LANES)), const((len(POOL_WINDOWS), POOL_GC, POOL_GC)),
                  const((1, POOL_WIDTH)), const((D_MODEL, D_MODEL))],
        out_specs=[x_blk, cache_blk, cache_blk, st_blk],
        out_shape=[jax.ShapeDtypeStruct((n, D_MODEL), F32), jax.ShapeDtypeStruct(ck.shape, F32),
                   jax.ShapeDtypeStruct(cv.shape, F32), jax.ShapeDtypeStruct(st.shape, F32)],
        scratch_shapes=[pltpu.VMEM((nb * 8, LANES), F32), pltpu.VMEM((D_MODEL, IN_WIDTH), BF16),
                        pltpu.VMEM((D_MODEL, D_MODEL), BF16)],
        compiler_params=pltpu.CompilerParams(dimension_semantics=("arbitrary",), vmem_limit_bytes=VMEM_LIMIT),
        name="mixer_sample",
    )(x, ck, cv, st, *tabs, sink8, g_attn, w_in, qn, kn, hmean, w_pool, pscale, w_out)


def _block_plan(cnt, bm, n_lanes):
    e_sub = lax.broadcasted_iota(I32, (N_EXPERTS, LANES), 0)
    e_lane = lax.broadcasted_iota(I32, (N_EXPERTS, LANES), 1)
    padded = jnp.floor((cnt + (bm - 1)) / bm) * bm
    padded_lane = jnp.sum(jnp.where(e_sub == e_lane, padded, 0.0), axis=0, keepdims=True)
    pad_end = jnp.sum(jnp.where(e_lane <= e_sub, padded_lane, 0.0), axis=1, keepdims=True)
    pad_start = pad_end - padded
    blk_start = lax.broadcasted_iota(I32, (N_EXPERTS, n_lanes), 1).astype(F32) * bm
    blk_e = jnp.minimum(jnp.sum((pad_end <= blk_start).astype(F32), axis=0, keepdims=True), N_EXPERTS - 1.0)
    mine = lax.broadcasted_iota(I32, (N_EXPERTS, n_lanes), 0).astype(F32) == blk_e
    last = jnp.sum(jnp.where(mine, pad_start + cnt, 0.0), axis=0, keepdims=True)
    blk_valid = jnp.clip(last - blk_start[0:1], 0.0, float(bm))
    e_blk = lax.broadcasted_iota(I32, (N_EXPERTS, n_lanes), 0).astype(F32)
    later = jnp.min(jnp.where((e_blk > blk_e) & (cnt > 0.0), e_blk, float(N_EXPERTS)), axis=0, keepdims=True)
    blk_next = jnp.where(later < N_EXPERTS, later, -1.0)
    return pad_start, blk_e, blk_valid, blk_next


def _route_kernel(x1_ref, g_ref, wr_ref, br_ref, cnt_in_ref, hf_ref, idx_ref, gate_ref, pos_ref, cnt_ref,
                  pstart_ref, blke_ref, blkv_ref, blkn_ref, counts, *, bm):
    i = pl.program_id(0)
    rows = x1_ref.shape[0]

    @pl.when(i == 0)
    def _():
        counts[...] = cnt_in_ref[...]

    h = _rms(x1_ref[...], g_ref[...])
    hf_ref[...] = _pack_halves(h)

    logits = lax.dot_general(wr_ref[...], h, (((1,), (1,)), ((), ())), preferred_element_type=F32,
                             precision=lax.Precision.HIGHEST) + br_ref[...]
    eid = lax.broadcasted_iota(I32, (N_EXPERTS, rows), 0).astype(F32)
    work = logits
    vals, hots = [], []
    for kk in range(TOP_K):
        m = jnp.max(work, axis=0, keepdims=True)
        first = jnp.min(jnp.where(work == m, eid, float(N_EXPERTS)), axis=0, keepdims=True)
        hot = eid == first
        work = jnp.where(hot, -jnp.inf, work)
        vals.append(m)
        hots.append(hot)
        idx_ref[kk:kk + 1, :] = first.astype(I32)
    es = [jnp.exp(vv - vals[0]) for vv in vals]
    den = es[0] + es[1] + es[2] + es[3]
    for kk in range(TOP_K):
        gate_ref[kk:kk + 1, :] = es[kk] / den

    chosen = hots[0] | hots[1] | hots[2] | hots[3]
    before = (lax.broadcasted_iota(I32, (rows, rows), 0) < lax.broadcasted_iota(I32, (rows, rows), 1)).astype(BF16)
    rank = jnp.dot(chosen.astype(BF16), before, preferred_element_type=F32) + counts[...]
    for kk in range(TOP_K):
        pos_ref[kk:kk + 1, :] = jnp.sum(jnp.where(hots[kk], rank, 0.0), axis=0, keepdims=True).astype(I32)
    counts[...] = counts[...] + jnp.sum(chosen.astype(F32), axis=1, keepdims=True)
    cnt_ref[...] = counts[...]

    @pl.when(i == pl.num_programs(0) - 1)
    def _():
        pad_start, blk_e, blk_valid, blk_next = _block_plan(counts[...], bm, blke_ref.shape[1])
        pstart_ref[...] = pad_start.astype(I32)
        blke_ref[...] = blk_e.astype(I32)
        blkv_ref[...] = blk_valid.astype(I32)
        blkn_ref[...] = blk_next.astype(I32)


def _route(x1, g_ffn, wr_t, br, cnt_in, rows, bm, n_blocks):
    n = x1.shape[0]
    assert n % rows == 0
    nb_lanes = -(-n_blocks // LANES) * LANES
    const = lambda shape: pl.BlockSpec(shape, lambda i: (0,) * len(shape))
    tok_blk = pl.BlockSpec((TOP_K, rows), lambda i: (0, i))
    return pl.pallas_call(
        functools.partial(_route_kernel, bm=bm),
        grid=(n // rows,),
        in_specs=[pl.BlockSpec((rows, D_MODEL), lambda i: (i, 0)), const((1, D_MODEL)), const((N_EXPERTS, D_MODEL)),
                  const((N_EXPERTS, 1)), const((N_EXPERTS, 1))],
        out_specs=[pl.BlockSpec((rows, D_MODEL // 2), lambda i: (i, 0)), tok_blk, tok_blk, tok_blk,
                   const((N_EXPERTS, 1)), const((N_EXPERTS, 1)), const((1, nb_lanes)), const((1, nb_lanes)),
                   const((1, nb_lanes))],
        out_shape=[jax.ShapeDtypeStruct((n, D_MODEL // 2), U32), jax.ShapeDtypeStruct((TOP_K, n), I32),
                   jax.ShapeDtypeStruct((TOP_K, n), F32), jax.ShapeDtypeStruct((TOP_K, n), I32),
                   jax.ShapeDtypeStruct((N_EXPERTS, 1), F32), jax.ShapeDtypeStruct((N_EXPERTS, 1), I32),
                   jax.ShapeDtypeStruct((1, nb_lanes), I32), jax.ShapeDtypeStruct((1, nb_lanes), I32),
                   jax.ShapeDtypeStruct((1, nb_lanes), I32)],
        scratch_shapes=[pltpu.VMEM((N_EXPERTS, 1), F32)],
        compiler_params=pltpu.CompilerParams(dimension_semantics=("arbitrary",), vmem_limit_bytes=VMEM_LIMIT),
        name="route",
    )(x1, g_ffn, wr_t, br, cnt_in)


def _sc_mesh():
    return plsc.VectorSubcoreMesh(core_axis_name="core", subcore_axis_name="subcore")


def _sc_worker_id():
    return lax.axis_index("core") * SC_SUBCORES + lax.axis_index("subcore")


def _scatter_rows(xa, xb, dest, n_rows):
    ch = SC_CHUNK
    na, w = xa.shape
    n = na + xb.shape[0]
    nk = dest.shape[0]
    assert na % ch == 0 and n % ch == 0 and dest.shape[1] == n and xb.shape[1] == w and xa.dtype == xb.dtype
    n_chunks = n // ch
    dest_c = dest.reshape(nk, n_chunks, ch).transpose(1, 0, 2).reshape(-1)
    dma = pltpu.SemaphoreType.DMA

    @pl.kernel(out_type=jax.ShapeDtypeStruct((n_rows, w), xa.dtype), mesh=_sc_mesh(),
               scratch_types=[pltpu.VMEM((nk * ch,), I32), pltpu.VMEM((ch, w), xa.dtype), dma] + [dma] * nk)
    def scatter_kernel(xa_hbm, xb_hbm, d_hbm, o_hbm, idx_v, buf, sem_i, *sem_s):
        wid = _sc_worker_id()

        @pl.loop(0, -(-n_chunks // SC_WORKERS))
        def _(j):
            c = j * SC_WORKERS + wid

            @pl.when(c < n_chunks)
            def _():
                load_idx = pltpu.make_async_copy(d_hbm.at[pl.ds(c * (nk * ch), nk * ch)], idx_v, sem_i)
                load_idx.start()

                @pl.when(c < na // ch)
                def _():
                    pltpu.sync_copy(xa_hbm.at[pl.ds(c * ch, ch)], buf)

                @pl.when(c >= na // ch)
                def _():
                    pltpu.sync_copy(xb_hbm.at[pl.ds(c * ch - na, ch)], buf)

                load_idx.wait()
                puts = [pltpu.make_async_copy(buf, o_hbm.at[idx_v.at[pl.ds(kk * ch, ch)]], sem_s[kk])
                        for kk in range(nk)]
                for cp in puts:
                    cp.start()
                for cp in puts:
                    cp.wait()

    return scatter_kernel(xa, xb, dest_c)


def _gather_rows(src, idx):
    ch = SC_CHUNK
    m = idx.shape[0]
    w = src.shape[1]
    per = m // SC_WORKERS
    n_pairs = per // (2 * ch)
    assert m % SC_WORKERS == 0 and per % (2 * ch) == 0
    dma = pltpu.SemaphoreType.DMA

    @pl.kernel(out_type=jax.ShapeDtypeStruct((m, w), src.dtype), mesh=_sc_mesh(),
               scratch_types=[pltpu.VMEM((per,), I32), pltpu.VMEM((ch, w), src.dtype), pltpu.VMEM((ch, w), src.dtype),
                              dma, dma, dma, dma])
    def gather_kernel(s_hbm, i_hbm, o_hbm, idx_v, buf_a, buf_b, sem_ga, sem_gb, sem_wa, sem_wb):
        base = _sc_worker_id() * per
        pltpu.sync_copy(i_hbm.at[pl.ds(base, per)], idx_v)

        def fetch(j, buf, sem):
            return pltpu.make_async_copy(s_hbm.at[idx_v.at[pl.ds(j * ch, ch)]], buf, sem)

        def put(j, buf, sem):
            return pltpu.make_async_copy(buf, o_hbm.at[pl.ds(base + j * ch, ch)], sem)

        fetch(0, buf_a, sem_ga).start()

        @pl.loop(0, n_pairs)
        def _(p):
            j0 = 2 * p
            j1 = j0 + 1

            @pl.when(p > 0)
            def _():
                put(j1 - 2, buf_b, sem_wb).wait()

            fetch(j1, buf_b, sem_gb).start()
            fetch(j0, buf_a, sem_ga).wait()
            put(j0, buf_a, sem_wa).start()
            fetch(j1, buf_b, sem_gb).wait()
            put(j1, buf_b, sem_wb).start()
            put(j0, buf_a, sem_wa).wait()

            @pl.when(p + 1 < n_pairs)
            def _():
                fetch(j0 + 2, buf_a, sem_ga).start()

        put(2 * n_pairs - 1, buf_b, sem_wb).wait()

    return gather_kernel(src, idx)


def _moe_kernel(blk_e_ref, blk_valid_ref, blk_next_ref, xs_ref, wgu_hbm, bgu_ref, wd_hbm, bd_ref, y_ref,
                wgu_f32, wd_f32, wgu_bf, wd_bf, sems, *, e0):
    i = pl.program_id(0)
    e = blk_e_ref[i]
    n_valid = blk_valid_ref[i]
    used = n_valid > 0
    new_expert = (i == 0) | (blk_e_ref[jnp.maximum(i - 1, 0)] != e)

    def weight_copies(expert):
        return (pltpu.make_async_copy(wgu_hbm.at[e0 + expert], wgu_f32, sems.at[0]),
                pltpu.make_async_copy(wd_hbm.at[e0 + expert], wd_f32, sems.at[1]))

    @pl.when(used & (i == 0))
    def _():
        for cp in weight_copies(e):
            cp.start()

    @pl.when(used & new_expert)
    def _():
        for cp in weight_copies(e):
            cp.wait()
        wgu_bf[...] = wgu_f32[...].astype(BF16)
        wd_bf[...] = wd_f32[...].astype(BF16)
        nxt = blk_next_ref[i]

        @pl.when(nxt >= 0)
        def _():
            for cp in weight_copies(nxt):
                cp.start(priority=1)

    @pl.when(used)
    def _():
        valid = lax.broadcasted_iota(I32, xs_ref.shape, 0) < n_valid
        xb = _unpack_halves(jnp.where(valid, xs_ref[...], jnp.uint32(0))).astype(BF16)
        gu = jnp.dot(xb, wgu_bf[...], preferred_element_type=F32) + bgu_ref[0]
        g = jnp.minimum(gu[:, :D_FF], SWIGLU_LIMIT)
        up = jnp.clip(gu[:, D_FF:], -SWIGLU_LIMIT, SWIGLU_LIMIT)
        act = (up + 1.0) * (g * jax.nn.sigmoid(SWIGLU_ALPHA * g))
        y_ref[...] = _pack_halves(jnp.dot(act.astype(BF16), wd_bf[...], preferred_element_type=F32) + bd_ref[0])

    @pl.when(jnp.logical_not(used))
    def _():
        y_ref[...] = jnp.zeros_like(y_ref)


def _moe(xs, blk_e, blk_valid, blk_next, layer, w_gu, b_gu, w_d, b_d):
    n_rows = xs.shape[0]
    bm = MOE_BM
    assert n_rows % bm == 0
    e0 = layer * N_EXPERTS
    n_we = w_gu.shape[0] * w_gu.shape[1]
    any_spec = pl.BlockSpec(memory_space=pl.ANY)
    grid_spec = pltpu.PrefetchScalarGridSpec(
        num_scalar_prefetch=3,
        grid=(n_rows // bm,),
        in_specs=[pl.BlockSpec((bm, D_MODEL // 2), lambda i, be, bv, bn: (i, 0)),
                  any_spec,
                  pl.BlockSpec((1, 1, 2 * D_FF), lambda i, be, bv, bn: (e0 + be[i], 0, 0)),
                  any_spec,
                  pl.BlockSpec((1, 1, D_MODEL), lambda i, be, bv, bn: (e0 + be[i], 0, 0))],
        out_specs=pl.BlockSpec((bm, D_MODEL // 2), lambda i, be, bv, bn: (i, 0)),
        scratch_shapes=[pltpu.VMEM((D_MODEL, 2 * D_FF), F32), pltpu.VMEM((D_FF, D_MODEL), F32),
                        pltpu.VMEM((D_MODEL, 2 * D_FF), BF16), pltpu.VMEM((D_FF, D_MODEL), BF16),
                        pltpu.SemaphoreType.DMA((2,))],
    )
    return pl.pallas_call(
        functools.partial(_moe_kernel, e0=e0),
        grid_spec=grid_spec,
        out_shape=jax.ShapeDtypeStruct((n_rows, D_MODEL // 2), U32),
        compiler_params=pltpu.CompilerParams(dimension_semantics=("arbitrary",), vmem_limit_bytes=VMEM_LIMIT),
        name="moe_experts",
    )(blk_e, blk_valid, blk_next, xs, w_gu.reshape(n_we, D_MODEL, 2 * D_FF), b_gu.reshape(n_we, 1, 2 * D_FF),
      w_d.reshape(n_we, D_FF, D_MODEL), b_d.reshape(n_we, 1, D_MODEL))


def _ple_kernel(x1_ref, y0_ref, y1_ref, y2_ref, y3_ref, gates_ref, p_ref, g_ref, wg_ref, wp_ref, *rest):
    out_ref, wg_bf, wp_bf = rest[-3:]

    @pl.when(pl.program_id(0) == 0)
    def _():
        wg_bf[...] = wg_ref[...].astype(BF16)
        wp_bf[...] = wp_ref[...].astype(BF16)

    x2 = x1_ref[...]
    gates = gates_ref[...]
    for kk, y_ref in enumerate((y0_ref, y1_ref, y2_ref, y3_ref)):
        x2 = x2 + _unpack_halves(y_ref[...]) * gates[:, kk:kk + 1]
    hp = _rms(x2, g_ref[...]).astype(BF16)
    gate = jax.nn.sigmoid(jnp.dot(hp, wg_bf[...], preferred_element_type=F32))
    pp = jnp.dot(p_ref[...].astype(BF16), wp_bf[...], preferred_element_type=F32)
    out_ref[...] = x2 + gate * pp


def _ple(x1, tok0, n, y_tok, y0, gates_t, p_all, p0, g_ple, w_gate, w_proj, rows, out_into=None):
    assert n % rows == 0 and tok0 % rows == 0 and y0 % rows == 0 and p0 % rows == 0
    const = lambda shape: pl.BlockSpec(shape, lambda i: (0,) * len(shape))
    tok_blk = lambda width: pl.BlockSpec((rows, width), lambda i: (tok0 // rows + i, 0))
    y_blk = lambda kk: pl.BlockSpec((rows, D_MODEL // 2), lambda i: ((y0 + kk * n) // rows + i, 0))
    extra_specs, extra_args, aliases = [], [], {}
    if out_into is not None:
        assert out_into.shape == x1.shape
        extra_specs, extra_args, aliases = [pl.BlockSpec(memory_space=pl.ANY)], [out_into], {10: 0}
    return pl.pallas_call(
        _ple_kernel,
        grid=(n // rows,),
        in_specs=[tok_blk(D_MODEL), y_blk(0), y_blk(1), y_blk(2), y_blk(3), tok_blk(TOP_K),
                  pl.BlockSpec((rows, PLE_DIM), lambda i: (p0 // rows + i, 0)),
                  const((1, D_MODEL)), const((D_MODEL, D_MODEL)), const((PLE_DIM, D_MODEL))] + extra_specs,
        out_specs=tok_blk(D_MODEL),
        out_shape=jax.ShapeDtypeStruct(x1.shape, F32),
        scratch_shapes=[pltpu.VMEM((D_MODEL, D_MODEL), BF16), pltpu.VMEM((PLE_DIM, D_MODEL), BF16)],
        input_output_aliases=aliases,
        compiler_params=pltpu.CompilerParams(dimension_semantics=("arbitrary",), vmem_limit_bytes=VMEM_LIMIT),
        name="combine_ple",
    )(x1, y_tok, y_tok, y_tok, y_tok, gates_t, p_all, g_ple, w_gate, w_proj, *extra_args)


def _rope_tables_const(seq):
    half = ROT_DIM // 2
    d = np.arange(LANES) % HEAD_DIM
    inv = (np.float64(ROPE_THETA) ** (-np.arange(half, dtype=np.float64) / half)).astype(np.float32)
    inv_lane = np.where(d < ROT_DIM, inv[d % half], np.float32(0.0)).astype(np.float32)
    ang = np.arange(seq, dtype=np.float32)[:, None] * inv_lane[None, :]
    cos = np.cos(ang.astype(np.float64)).astype(np.float32)
    sin = np.sin(ang.astype(np.float64)).astype(np.float32)
    sin_a = np.where((d >= half) & (d < ROT_DIM), sin, np.float32(0.0))
    sin_b = np.where(d < half, -sin, np.float32(0.0))
    return tuple(jnp.asarray(a) for a in (cos, sin_a, sin_b))


def _rope_tables(pos):
    half = ROT_DIM // 2
    d = np.arange(LANES) % HEAD_DIM
    inv = ROPE_THETA ** (-jnp.arange(half, dtype=F32) / half)
    inv_lane = jnp.where(d < ROT_DIM, inv[d % half], 0.0)
    ang = pos.astype(F32)[:, None] * inv_lane[None, :]
    cos, sin = jnp.cos(ang), jnp.sin(ang)
    sin_a = jnp.where((d >= half) & (d < ROT_DIM), sin, 0.0)
    sin_b = jnp.where(d < half, -sin, 0.0)
    return cos, sin_a, sin_b


def _layer(layer, xp, xs, ck, cv, st, p_prompt_all, p_sample_all, past_len, refine_tail, tabs_p, tabs_s, finish_state,
           norm_attn, w_in, q_norm, k_norm, attn_sinks, w_pool, pool_scale, w_out,
           norm_ffn, w_router, b_router, w_gate_up_all, b_gate_up_all, w_down_all, b_down_all,
           norm_ple, w_ple_gate, w_ple_proj):
    t, ns = xp.shape[0], xs.shape[0]
    g_attn = norm_attn.reshape(1, D_MODEL)
    qn = jnp.tile(q_norm, 2).reshape(1, LANES)
    kn = jnp.tile(k_norm, 2).reshape(1, LANES)
    lane = np.arange(LANES)
    hmean = jnp.asarray((lane[:, None] // HEAD_DIM == lane[None, :] // HEAD_DIM) / HEAD_DIM, F32)
    pscale = pool_scale.reshape(1, POOL_WIDTH)
    sink8 = jnp.broadcast_to(attn_sinks.reshape(2, Q_TILES).T.reshape(8, 1), (8, LANES))

    mix_args = (g_attn, w_in, qn, kn, hmean, w_pool, pscale, w_out)
    x1p, nk_p, nv_p, nu_p = _mixer_prompt(xp, 0, MIX_ROWS, BF16, attn_sinks, tabs_p, *mix_args)
    if refine_tail:
        x1p, nk_p, nv_p, nu_p = _mixer_prompt(xp, t - 2 * MIX_TAIL_ROWS, MIX_TAIL_ROWS, F32, attn_sinks, tabs_p,
                                              *mix_args, x1_into=x1p)
    st_t = jnp.transpose(st, (1, 0, 2))
    x1s, nk_s, nv_s, nst_t = _mixer_sample(xs, ck.reshape(ns, -1, KV_WIDTH), cv.reshape(ns, -1, KV_WIDTH), st_t,
                                           past_len, tabs_s, sink8, *mix_args)

    g_ffn = norm_ffn.reshape(1, D_MODEL)
    wr_t = w_router.T
    br = b_router.reshape(N_EXPERTS, 1)
    n_tok = t + ns
    bm = MOE_BM
    n_blocks = -(-(n_tok * TOP_K + N_EXPERTS * (bm - 1)) // bm)
    hf_p, idx_p, gate_p, pos_p, cnt_p = _route(x1p, g_ffn, wr_t, br, jnp.zeros((N_EXPERTS, 1), F32),
                                               ROUTE_ROWS, bm, n_blocks)[:5]
    hf_s, idx_s, gate_s, pos_s, _, pad_start, blk_e, blk_valid, blk_next = _route(x1s, g_ffn, wr_t, br, cnt_p, ns,
                                                                                  bm, n_blocks)
    idx = jnp.concatenate([idx_p, idx_s], axis=1)
    pos = jnp.concatenate([pos_p, pos_s], axis=1)
    blk_e, blk_valid, blk_next = (a[0, :n_blocks] for a in (blk_e, blk_valid, blk_next))
    state = finish_state(nk_p, nv_p, nu_p[POOL_HALO - POOL_PREFIX:], nk_s, nv_s, jnp.transpose(nst_t, (1, 0, 2)))
    blk_e, state = lax.optimization_barrier((blk_e, state))
    start_of = jnp.sum(jnp.where(idx[None] == jnp.arange(N_EXPERTS, dtype=I32)[:, None, None],
                                 pad_start.reshape(N_EXPERTS, 1, 1), 0), axis=0)
    dest = start_of + pos

    xs_rows = _scatter_rows(hf_p, hf_s, dest, n_blocks * bm)
    y = _moe(xs_rows, blk_e, blk_valid, blk_next, layer, w_gate_up_all, b_gate_up_all, w_down_all, b_down_all)

    tq = t // COMBINE_PARTS
    unit = SC_WORKERS * SC_CHUNK * 2
    assert t % COMBINE_PARTS == 0 and (TOP_K * tq) % unit == 0 and tq % PLE_ROWS == 0
    g_ple = norm_ple.reshape(1, D_MODEL)
    gates_p = gate_p.T
    yp = None
    for part in range(COMBINE_PARTS):
        back = dest[:, part * tq:(part + 1) * tq].reshape(-1)
        if part == COMBINE_PARTS - 1:
            back = jnp.concatenate([back, dest[:, t:].reshape(-1)])
            back = jnp.concatenate([back, jnp.arange(-back.shape[0] % unit, dtype=I32)])
        y_part = _gather_rows(y, back)
        yp = _ple(x1p, part * tq, tq, y_part, 0, gates_p, p_prompt_all, layer * t + part * tq, g_ple,
                  w_ple_gate, w_ple_proj, PLE_ROWS, out_into=yp)
    ys = _ple(x1s, 0, ns, y_part, TOP_K * tq, gate_s.T, p_sample_all, layer * ns, g_ple, w_ple_gate, w_ple_proj, ns)
    return yp, ys, state


def kernel(x_prompt, x_sample, cache_k, cache_v, state_pool, p_prompt, p_sample, norm_attn, w_in, q_norm, k_norm,
           attn_sinks, w_pool, pool_scale, w_out, norm_ffn, w_router, b_router, w_gate_up, b_gate_up, w_down, b_down,
           norm_ple, w_ple_gate, w_ple_proj):
    depth = norm_attn.shape[0]
    batch, seq, d = x_prompt.shape
    ns, dec_seq, _ = x_sample.shape
    wb = cache_k.shape[2]
    assert batch == 1 and dec_seq == 1 and d == D_MODEL and wb == WINDOW
    assert cache_k.shape[3:] == (N_KV_HEADS, HEAD_DIM) and state_pool.shape[2:] == (POOL_PREFIX, POOL_WIDTH)
    past_len = PAST_LEN
    yp = x_prompt.reshape(seq, d)
    ys = x_sample.reshape(ns, d)
    p_prompt_all = p_prompt.reshape(depth * seq, PLE_DIM)
    p_sample_all = p_sample.reshape(depth * ns, PLE_DIM)
    tabs_p = _rope_tables_const(seq)
    tabs_s = _rope_tables(jnp.full((1,), past_len))
    kv_p, kv_s = (1, WINDOW, N_KV_HEADS, HEAD_DIM), (ns, wb, N_KV_HEADS, HEAD_DIM)
    state_shapes = (kv_p, kv_p, (1, POOL_PREFIX, POOL_WIDTH), kv_s, kv_s, (ns, POOL_PREFIX, POOL_WIDTH))
    per_layer = []

    def shaped(*state):
        return tuple(a.reshape(shp) for a, shp in zip(state, state_shapes))

    def stacked(*state):
        prevs = list(zip(*per_layer)) or [()] * len(state_shapes)
        return tuple(jnp.stack(list(prev) + [cur]) for prev, cur in zip(prevs, shaped(*state)))

    for i in range(depth):
        yp, ys, state = _layer(i, yp, ys, cache_k[i], cache_v[i], state_pool[i], p_prompt_all, p_sample_all, past_len,
                               i < depth - 1, tabs_p, tabs_s, shaped if i < depth - 1 else stacked,
                               norm_attn[i], w_in[i], q_norm[i], k_norm[i], attn_sinks[i], w_pool[i], pool_scale[i],
                               w_out[i], norm_ffn[i], w_router[i], b_router[i], w_gate_up, b_gate_up, w_down, b_down,
                               norm_ple[i], w_ple_gate[i], w_ple_proj[i])
        per_layer.append(state)
    return (yp.reshape(batch, seq, d), ys.reshape(ns, dec_seq, d)) + tuple(state)
```

```python
import functools

import jax
import jax.numpy as jnp
import numpy as np
from jax import lax
from jax.experimental import pallas as pl
from jax.experimental.pallas import tpu as pltpu
from jax.experimental.pallas import tpu_sc as plsc

F32 = jnp.float32
BF16 = jnp.bfloat16
U32 = jnp.uint32
I32 = jnp.int32

D_MODEL = 1024
HEAD_DIM = 64
N_HEADS = 8
N_KV_HEADS = 2
GROUP = N_HEADS // N_KV_HEADS
ATTN_WIDTH = N_HEADS * HEAD_DIM
KV_WIDTH = N_KV_HEADS * HEAD_DIM
POOL_WIDTH = 512
POOL_WINDOWS = (2, 4, 8, 16)
POOL_GC = POOL_WIDTH // len(POOL_WINDOWS)
POOL_PREFIX = max(POOL_WINDOWS) - 1
POOL_HALO = POOL_PREFIX + 1
POOL_PAD = 8
IN_WIDTH = ATTN_WIDTH + 2 * KV_WIDTH + POOL_WIDTH
WINDOW = 128
ROPE_THETA = 500000.0
ROT_DIM = HEAD_DIM // 4
N_EXPERTS = 32
TOP_K = 4
D_FF = 1024
SWIGLU_ALPHA = 1.702
SWIGLU_LIMIT = 7.0
PLE_DIM = 256
PAST_LEN = 16384
EPS = 1e-5
NEG_INF = -1e30

LANES = 128
Q_TILES = ATTN_WIDTH // LANES

MIX_ROWS = 1024
MIX_TAIL_ROWS = 128
SAMPLE_CHUNK = 16
ROUTE_ROWS = 1024
MOE_BM = 512
SC_CORES = 2
SC_SUBCORES = 16
SC_WORKERS = SC_CORES * SC_SUBCORES
SC_CHUNK = 64
PLE_ROWS = 1024
COMBINE_PARTS = 2
VMEM_LIMIT = 56 * 1024 * 1024


def _rms(x, g):
    return x * lax.rsqrt(jnp.mean(x * x, axis=-1, keepdims=True) + EPS) * g


def _pack_halves(x):
    w = x.shape[1] // 2
    lo = lax.bitcast_convert_type(x[:, :w].astype(BF16).astype(F32), U32) >> 16
    hi = lax.bitcast_convert_type(x[:, w:].astype(BF16).astype(F32), U32) & jnp.uint32(0xFFFF0000)
    return lo | hi


def _unpack_halves(packed):
    lo = lax.bitcast_convert_type(packed << 16, F32)
    hi = lax.bitcast_convert_type(packed & jnp.uint32(0xFFFF0000), F32)
    return jnp.concatenate([lo, hi], axis=1)


def _mm(a, b, nt=False):
    dims = (((1,), (1 if nt else 0,)), ((), ()))
    if b.dtype == F32:
        return lax.dot_general(a.astype(F32), b, dims, preferred_element_type=F32, precision=lax.Precision.HIGHEST)
    return lax.dot_general(a.astype(BF16), b, dims, preferred_element_type=F32)


def _head_norm_rope(t, hmean, gain, cos, sin_a, sin_b):
    t = t * lax.rsqrt(_mm(t * t, hmean) + EPS) * gain
    return t * cos + pltpu.roll(t, ROT_DIM // 2, axis=1) * sin_a + pltpu.roll(t, LANES - ROT_DIM // 2, axis=1) * sin_b


def _softmax_pv(s, sink, v, ones):
    m = jnp.maximum(jnp.max(s, axis=-1, keepdims=True), sink)
    e = jnp.exp(s - m).astype(v.dtype)
    den = _mm(e, ones) + jnp.exp(sink - m)
    return _mm(e, v) / den


def _prep_mixer_weights(win_ref, wout_ref, win_c, wout_c):
    cdt = win_c.dtype
    for j in range(Q_TILES):
        for s in range(2):
            src = (j + Q_TILES * s) * HEAD_DIM
            dst = j * LANES + s * HEAD_DIM
            win_c[:, dst:dst + HEAD_DIM] = win_ref[:, src:src + HEAD_DIM].astype(cdt)
            wout_c[dst:dst + HEAD_DIM, :] = wout_ref[src:src + HEAD_DIM, :].astype(cdt)
    win_c[:, ATTN_WIDTH:] = win_ref[:, ATTN_WIDTH:].astype(cdt)
    wout_c[ATTN_WIDTH:, :] = wout_ref[ATTN_WIDTH:, :].astype(cdt)


def _mixer_prompt_kernel(sinks_ref, x_ref, cos_ref, sa_ref, sb_ref, g_ref, win_ref, qn_ref, kn_ref, hm_ref,
                         wpool_ref, pscale_ref, wout_ref, *rest, row_offset, aliased):
    (x1_ref, klast_ref, vlast_ref, ulast_ref, kprev, vprev, uext, mix, win_c, wout_c,
     *lvl) = rest[1:] if aliased else rest
    i = pl.program_id(0)
    rows = x_ref.shape[0]
    n_sub = rows // WINDOW
    cdt = win_c.dtype
    row0 = row_offset + i * rows

    @pl.when(i == 0)
    def _():
        _prep_mixer_weights(win_ref, wout_ref, win_c, wout_c)
        kprev[...] = jnp.zeros_like(kprev)
        vprev[...] = jnp.zeros_like(vprev)
        uext[0:POOL_PAD + POOL_HALO, :] = jnp.zeros((POOL_PAD + POOL_HALO, POOL_WIDTH), F32)
        for buf in lvl:
            buf[0:POOL_PAD, :] = jnp.zeros((POOL_PAD, POOL_GC), F32)

    x = x_ref[...]
    proj = _mm(_rms(x, g_ref[...]), win_c[...])
    cos, sin_a, sin_b = cos_ref[...], sa_ref[...], sb_ref[...]

    n_t = Q_TILES + 1
    t_all = jnp.concatenate([proj[:, j * LANES:(j + 1) * LANES] for j in range(n_t)], axis=0)
    t3 = (t_all * lax.rsqrt(_mm(t_all * t_all, hm_ref[...].astype(cdt)) + EPS)).reshape(n_t, rows, LANES)
    t3 = jnp.concatenate([t3[:Q_TILES] * (qn_ref[...] * HEAD_DIM ** -0.5), t3[Q_TILES:] * kn_ref[...]], axis=0)
    t2 = t3.reshape(n_t * rows, LANES)
    t3 = (t3 * cos + pltpu.roll(t2, ROT_DIM // 2, axis=1).reshape(n_t, rows, LANES) * sin_a
          + pltpu.roll(t2, LANES - ROT_DIM // 2, axis=1).reshape(n_t, rows, LANES) * sin_b)
    q3 = t3[:Q_TILES]
    k = t3[Q_TILES]
    v = proj[:, ATTN_WIDTH + KV_WIDTH:ATTN_WIDTH + 2 * KV_WIDTH]
    u = proj[:, ATTN_WIDTH + 2 * KV_WIDTH:]
    klast_ref[...] = k[rows - WINDOW:, :]
    vlast_ref[...] = v[rows - WINDOW:, :]
    ulast_ref[...] = u[rows - POOL_HALO:, :]
    k_c = k.astype(cdt)
    v_c = jnp.concatenate([v.astype(cdt), jnp.ones((rows, LANES), cdt)], axis=1)
    v_first = jnp.concatenate([vprev[...], jnp.ones((WINDOW, LANES), cdt)], axis=1)

    lane = lax.broadcasted_iota(I32, (WINDOW, LANES), 1)
    left = (lane < HEAD_DIM)[None]
    qi = lax.broadcasted_iota(I32, (WINDOW, 2 * WINDOW), 0)
    kj = lax.broadcasted_iota(I32, (WINDOW, 2 * WINDOW), 1)
    band = (kj - qi >= 1) & (kj - qi <= WINDOW)
    sink3 = jnp.concatenate([jnp.full((1, 1, 1), sinks_ref[j + Q_TILES * s], F32)
                             for j in range(Q_TILES) for s in range(2)], axis=0)
    n_g = 2 * Q_TILES
    fill = jnp.where(kj[0:1][None] == 0, sink3, NEG_INF)
    slot = lax.broadcasted_iota(I32, (2 * WINDOW, 2 * LANES), 0) == 0
    slot_v = slot & (lax.broadcasted_iota(I32, (2 * WINDOW, 2 * LANES), 1) < LANES)

    v_cats, masks, scores = [], [], []
    for c in range(n_sub):
        r0 = c * WINDOW
        if c == 0:
            k_cat = jnp.concatenate([kprev[...], k_c[0:WINDOW]], axis=0)
            v_cat = jnp.concatenate([v_first, v_c[0:WINDOW]], axis=0)
            masks.append(band & (kj + (row0 - WINDOW) >= 0))
        else:
            k_cat = k_c[r0 - WINDOW:r0 + WINDOW]
            v_cat = v_c[r0 - WINDOW:r0 + WINDOW]
            masks.append(band)
        v_cats.append(jnp.where(slot_v, jnp.zeros_like(v_cat), v_cat))
        q_c = q3[:, r0:r0 + WINDOW, :]
        q_all = jnp.concatenate([jnp.where(left, q_c, 0.0), jnp.where(left, 0.0, q_c)], axis=1)
        scores.append(_mm(q_all.reshape(n_g * WINDOW, LANES), k_cat, nt=True))
    kprev[...] = k_c[rows - WINDOW:]
    vprev[...] = v[rows - WINDOW:].astype(cdt)

    probs = []
    for c in range(n_sub):
        s = jnp.where(masks[c][None], scores[c].reshape(n_g, WINDOW, 2 * WINDOW), fill)
        m = jnp.max(s, axis=-1, keepdims=True)
        probs.append(jnp.exp(s - m).astype(cdt).reshape(n_g * WINDOW, 2 * WINDOW))
    applied = [_mm(probs[c], v_cats[c]) for c in range(n_sub)]
    for c in range(n_sub):
        r0 = c * WINDOW
        o = (applied[c][:, :LANES] / applied[c][:, LANES:]).reshape(Q_TILES, 2 * WINDOW, LANES)
        a = jnp.where(left, o[:, :WINDOW], o[:, WINDOW:])
        for j in range(Q_TILES):
            mix[r0:r0 + WINDOW, j * LANES:(j + 1) * LANES] = a[j].astype(cdt)

    base = POOL_PAD + POOL_HALO
    ext = base + rows
    uext[base:ext, :] = u
    pos1 = (lax.broadcasted_iota(I32, (rows, 1), 0) + row0 + 1).astype(F32)
    lvl_of = {1: lvl[0:1], 2: lvl[1:3], 3: lvl[3:5]}
    for gi, w in enumerate(POOL_WINDOWS):
        cols = slice(gi * POOL_GC, (gi + 1) * POOL_GC)
        src, src_cols = uext, cols
        for level in range(1, gi + 2):
            sft = 1 << (level - 1)
            if level <= gi:
                dst = lvl_of[gi][(level - 1) % 2]
                dst[POOL_PAD:ext, :] = src[POOL_PAD:ext, src_cols] + src[POOL_PAD - sft:ext - sft, src_cols]
                src, src_cols = dst, slice(None)
            else:
                wsum = src[base:ext, src_cols] + src[base - sft:ext - sft, src_cols]
        d = wsum / jnp.minimum(pos1, float(w)) - u[:, cols]
        y = _mm(d, wpool_ref[gi].astype(cdt)) * pscale_ref[:, cols]
        mix[:, ATTN_WIDTH + gi * POOL_GC:ATTN_WIDTH + (gi + 1) * POOL_GC] = y.astype(cdt)
    uext[POOL_PAD:base, :] = u[rows - POOL_HALO:, :]

    x1_ref[...] = x + _mm(mix[...], wout_c[...])


def _mixer_prompt(x_full, row_offset, rows, cdt, sinks, tabs, g_attn, w_in, qn, kn, hmean, w_pool, pscale, w_out,
                  x1_into=None):
    t = x_full.shape[0] - row_offset
    assert t % rows == 0 and row_offset % rows == 0 and rows % WINDOW == 0 and rows >= POOL_HALO
    blk0, n_steps = row_offset // rows, t // rows
    const = lambda shape: pl.BlockSpec(shape, lambda i, *_: (0,) * len(shape))
    row_blk = lambda width: pl.BlockSpec((rows, width), lambda i, *_: (blk0 + i, 0))
    aliased = x1_into is not None
    if aliased:
        assert x1_into.shape == x_full.shape
        x1_spec = pl.BlockSpec((rows, D_MODEL), lambda i, *_: (blk0 + n_steps - 1, 0))
        x1_shape = x1_into.shape
        extra_specs, extra_args, aliases = [pl.BlockSpec(memory_space=pl.ANY)], [x1_into], {13: 0}
    else:
        x1_spec = pl.BlockSpec((rows, D_MODEL), lambda i, *_: (i, 0))
        x1_shape = (t, D_MODEL)
        extra_specs, extra_args, aliases = [], [], {}
    grid_spec = pltpu.PrefetchScalarGridSpec(
        num_scalar_prefetch=1,
        grid=(n_steps,),
        in_specs=[row_blk(D_MODEL), row_blk(LANES), row_blk(LANES), row_blk(LANES),
                  const((1, D_MODEL)), const((D_MODEL, IN_WIDTH)), const((1, LANES)), const((1, LANES)),
                  const((LANES, LANES)), const((len(POOL_WINDOWS), POOL_GC, POOL_GC)), const((1, POOL_WIDTH)),
                  const((D_MODEL, D_MODEL))] + extra_specs,
        out_specs=[x1_spec, const((WINDOW, KV_WIDTH)), const((WINDOW, KV_WIDTH)),
                   const((POOL_HALO, POOL_WIDTH))],
        scratch_shapes=[pltpu.VMEM((WINDOW, KV_WIDTH), cdt), pltpu.VMEM((WINDOW, KV_WIDTH), cdt),
                        pltpu.VMEM((POOL_PAD + POOL_HALO + rows, POOL_WIDTH), F32), pltpu.VMEM((rows, D_MODEL), cdt),
                        pltpu.VMEM((D_MODEL, IN_WIDTH), cdt), pltpu.VMEM((D_MODEL, D_MODEL), cdt)]
        + [pltpu.VMEM((POOL_PAD + POOL_HALO + rows, POOL_GC), F32)] * 5,
    )
    return pl.pallas_call(
        functools.partial(_mixer_prompt_kernel, row_offset=row_offset, aliased=aliased),
        grid_spec=grid_spec,
        out_shape=[jax.ShapeDtypeStruct(x1_shape, F32), jax.ShapeDtypeStruct((WINDOW, KV_WIDTH), F32),
                   jax.ShapeDtypeStruct((WINDOW, KV_WIDTH), F32), jax.ShapeDtypeStruct((POOL_HALO, POOL_WIDTH), F32)],
        input_output_aliases=aliases,
        compiler_params=pltpu.CompilerParams(dimension_semantics=("arbitrary",), vmem_limit_bytes=VMEM_LIMIT),
        name="mixer_prompt",
    )(sinks, x_full, *tabs, g_attn, w_in, qn, kn, hmean, w_pool, pscale, w_out, *extra_args)


def _mixer_sample_kernel(x_ref, ck_ref, cv_ref, st_ref, cos_ref, sa_ref, sb_ref, sink8_ref, g_ref, win_ref, qn_ref,
                         kn_ref, hm_ref, wpool_ref, pscale_ref, wout_ref,
                         x1_ref, nk_ref, nv_ref, nst_ref, o8, win_c, wout_c, *, pos):
    nb = x_ref.shape[0]
    wb = ck_ref.shape[1]

    @pl.when(pl.program_id(0) == 0)
    def _():
        _prep_mixer_weights(win_ref, wout_ref, win_c, wout_c)

    x = x_ref[...]
    h = _rms(x, g_ref[...]).astype(BF16)
    proj = jnp.dot(h, win_c[...], preferred_element_type=F32)
    cos, sin_a, sin_b = cos_ref[...], sa_ref[...], sb_ref[...]
    hmean = hm_ref[...].astype(BF16)
    k = _head_norm_rope(proj[:, ATTN_WIDTH:ATTN_WIDTH + KV_WIDTH], hmean, kn_ref[...], cos, sin_a, sin_b)
    v = proj[:, ATTN_WIDTH + KV_WIDTH:ATTN_WIDTH + 2 * KV_WIDTH]
    u = proj[:, ATTN_WIDTH + 2 * KV_WIDTH:]

    nk_ref[:, 0:wb - 1, :] = ck_ref[:, 1:wb, :]
    nv_ref[:, 0:wb - 1, :] = cv_ref[:, 1:wb, :]
    for b in range(nb):
        nk_ref[b, wb - 1:wb, :] = k[b:b + 1, :]
        nv_ref[b, wb - 1:wb, :] = v[b:b + 1, :]

    r8 = lax.broadcasted_iota(I32, (nb * 8, LANES), 0)
    lane8 = lax.broadcasted_iota(I32, (nb * 8, LANES), 1)
    keep = (lane8 < HEAD_DIM) == (r8 % 2 == 0)
    rep = (lax.broadcasted_iota(I32, (nb * 8, nb), 0) // 8 == lax.broadcasted_iota(I32, (nb * 8, nb), 1)).astype(BF16)
    q8 = jnp.zeros((nb * 8, LANES), F32)
    scale = HEAD_DIM ** -0.5
    for j in range(Q_TILES):
        qt = _head_norm_rope(proj[:, j * LANES:(j + 1) * LANES], hmean, qn_ref[...], cos, sin_a, sin_b) * scale
        qrep = jnp.dot(rep, qt.astype(BF16), preferred_element_type=F32)
        q8 = jnp.where(keep & ((r8 % 8) // 2 == j), qrep, q8)
    q8 = q8.astype(BF16)

    sink8 = sink8_ref[:, 0:1]
    ones_bf = jnp.ones((wb, LANES), BF16)
    assert pos >= wb - 1 and wb <= WINDOW
    for b in range(nb):
        kb = nk_ref[b].astype(BF16)
        vb = nv_ref[b].astype(BF16)
        s = lax.dot_general(q8[b * 8:(b + 1) * 8], kb, (((1,), (1,)), ((), ())), preferred_element_type=F32)
        o8[b * 8:(b + 1) * 8, :] = _softmax_pv(s, sink8, vb, ones_bf)
    o8m = jnp.where(keep, o8[...], 0.0).astype(BF16)

    a_tiles = []
    sel_r = lax.broadcasted_iota(I32, (nb, nb * 8), 1)
    sel_b = lax.broadcasted_iota(I32, (nb, nb * 8), 0)
    for j in range(Q_TILES):
        sel = ((sel_r // 8 == sel_b) & ((sel_r % 8) // 2 == j)).astype(BF16)
        a_tiles.append(jnp.dot(sel, o8m, preferred_element_type=F32))

    z_tiles = []
    for gi, w in enumerate(POOL_WINDOWS):
        cols = slice(gi * POOL_GC, (gi + 1) * POOL_GC)
        wsum = u[:, cols]
        for sft in range(1, w):
            wsum = wsum + st_ref[POOL_PREFIX - sft, :, cols]
        d = wsum / float(min(pos + 1, w)) - u[:, cols]
        z_tiles.append(jnp.dot(d.astype(BF16), wpool_ref[gi].astype(BF16), preferred_element_type=F32)
                       * pscale_ref[:, cols])
    nst_ref[0:POOL_PREFIX - 1] = st_ref[1:POOL_PREFIX]
    nst_ref[POOL_PREFIX - 1] = u

    mixv = jnp.concatenate(a_tiles + z_tiles, axis=1).astype(BF16)
    x1_ref[...] = x + jnp.dot(mixv, wout_c[...], preferred_element_type=F32)


def _mixer_sample(x, ck, cv, st, pos, tabs, sink8, g_attn, w_in, qn, kn, hmean, w_pool, pscale, w_out):
    n, wb = ck.shape[0], ck.shape[1]
    nb = SAMPLE_CHUNK
    assert n % nb == 0
    const = lambda shape: pl.BlockSpec(shape, lambda i: (0,) * len(shape))
    cache_blk = pl.BlockSpec((nb, wb, KV_WIDTH), lambda i: (i, 0, 0))
    st_blk = pl.BlockSpec((POOL_PREFIX, nb, POOL_WIDTH), lambda i: (0, i, 0))
    x_blk = pl.BlockSpec((nb, D_MODEL), lambda i: (i, 0))
    return pl.pallas_call(
        functools.partial(_mixer_sample_kernel, pos=pos),
        grid=(n // nb,),
        in_specs=[x_blk, cache_blk, cache_blk, st_blk, const((1, LANES)), const((1, LANES)), const((1, LANES)),
                  const((8, LANES)), const((1, D_MODEL)), const((D_MODEL, IN_WIDTH)), const((1, LANES)),
                  const((1, LANES)), const((LANES, LANES)), const((len(POOL_WINDOWS), POOL_GC, POOL_GC)),
                  const((1, POOL_WIDTH)), const((D_MODEL, D_MODEL))],
        out_specs=[x_blk, cache_blk, cache_blk, st_blk],
        out_shape=[jax.ShapeDtypeStruct((n, D_MODEL), F32), jax.ShapeDtypeStruct(ck.shape, F32),
                   jax.ShapeDtypeStruct(cv.shape, F32), jax.ShapeDtypeStruct(st.shape, F32)],
        scratch_shapes=[pltpu.VMEM((nb * 8, LANES), F32), pltpu.VMEM((D_MODEL, IN_WIDTH), BF16),
                        pltpu.VMEM((D_MODEL, D_MODEL), BF16)],
        compiler_params=pltpu.CompilerParams(dimension_semantics=("arbitrary",), vmem_limit_bytes=VMEM_LIMIT),
        name="mixer_sample",
    )(x, ck, cv, st, *tabs, sink8, g_attn, w_in, qn, kn, hmean, w_pool, pscale, w_out)


def _block_plan(cnt, bm, n_lanes):
    e_sub = lax.broadcasted_iota(I32, (N_EXPERTS, LANES), 0)
    e_lane = lax.broadcasted_iota(I32, (N_EXPERTS, LANES), 1)
    padded = jnp.floor((cnt + (bm - 1)) / bm) * bm
    padded_lane = jnp.sum(jnp.where(e_sub == e_lane, padded, 0.0), axis=0, keepdims=True)
    pad_end = jnp.sum(jnp.where(e_lane <= e_sub, padded_lane, 0.0), axis=1, keepdims=True)
    pad_start = pad_end - padded
    blk_start = lax.broadcasted_iota(I32, (N_EXPERTS, n_lanes), 1).astype(F32) * bm
    blk_e = jnp.minimum(jnp.sum((pad_end <= blk_start).astype(F32), axis=0, keepdims=True), N_EXPERTS - 1.0)
    mine = lax.broadcasted_iota(I32, (N_EXPERTS, n_lanes), 0).astype(F32) == blk_e
    last = jnp.sum(jnp.where(mine, pad_start + cnt, 0.0), axis=0, keepdims=True)
    blk_valid = jnp.clip(last - blk_start[0:1], 0.0, float(bm))
    e_blk = lax.broadcasted_iota(I32, (N_EXPERTS, n_lanes), 0).astype(F32)
    later = jnp.min(jnp.where((e_blk > blk_e) & (cnt > 0.0), e_blk, float(N_EXPERTS)), axis=0, keepdims=True)
    blk_next = jnp.where(later < N_EXPERTS, later, -1.0)
    return pad_start, blk_e, blk_valid, blk_next


def _route_kernel(x1_ref, g_ref, wr_ref, br_ref, cnt_in_ref, hf_ref, idx_ref, gate_ref, pos_ref, cnt_ref,
                  pstart_ref, blke_ref, blkv_ref, blkn_ref, counts, *, bm):
    i = pl.program_id(0)
    rows = x1_ref.shape[0]

    @pl.when(i == 0)
    def _():
        counts[...] = cnt_in_ref[...]

    h = _rms(x1_ref[...], g_ref[...])
    hf_ref[...] = _pack_halves(h)

    logits = lax.dot_general(wr_ref[...], h, (((1,), (1,)), ((), ())), preferred_element_type=F32,
                             precision=lax.Precision.HIGHEST) + br_ref[...]
    eid = lax.broadcasted_iota(I32, (N_EXPERTS, rows), 0).astype(F32)
    work = logits
    vals, hots = [], []
    for kk in range(TOP_K):
        m = jnp.max(work, axis=0, keepdims=True)
        first = jnp.min(jnp.where(work == m, eid, float(N_EXPERTS)), axis=0, keepdims=True)
        hot = eid == first
        work = jnp.where(hot, -jnp.inf, work)
        vals.append(m)
        hots.append(hot)
        idx_ref[kk:kk + 1, :] = first.astype(I32)
    es = [jnp.exp(vv - vals[0]) for vv in vals]
    den = es[0] + es[1] + es[2] + es[3]
    for kk in range(TOP_K):
        gate_ref[kk:kk + 1, :] = es[kk] / den

    chosen = hots[0] | hots[1] | hots[2] | hots[3]
    before = (lax.broadcasted_iota(I32, (rows, rows), 0) < lax.broadcasted_iota(I32, (rows, rows), 1)).astype(BF16)
    rank = jnp.dot(chosen.astype(BF16), before, preferred_element_type=F32) + counts[...]
    for kk in range(TOP_K):
        pos_ref[kk:kk + 1, :] = jnp.sum(jnp.where(hots[kk], rank, 0.0), axis=0, keepdims=True).astype(I32)
    counts[...] = counts[...] + jnp.sum(chosen.astype(F32), axis=1, keepdims=True)
    cnt_ref[...] = counts[...]

    @pl.when(i == pl.num_programs(0) - 1)
    def _():
        pad_start, blk_e, blk_valid, blk_next = _block_plan(counts[...], bm, blke_ref.shape[1])
        pstart_ref[...] = pad_start.astype(I32)
        blke_ref[...] = blk_e.astype(I32)
        blkv_ref[...] = blk_valid.astype(I32)
        blkn_ref[...] = blk_next.astype(I32)


def _route(x1, g_ffn, wr_t, br, cnt_in, rows, bm, n_blocks):
    n = x1.shape[0]
    assert n % rows == 0
    nb_lanes = -(-n_blocks // LANES) * LANES
    const = lambda shape: pl.BlockSpec(shape, lambda i: (0,) * len(shape))
    tok_blk = pl.BlockSpec((TOP_K, rows), lambda i: (0, i))
    return pl.pallas_call(
        functools.partial(_route_kernel, bm=bm),
        grid=(n // rows,),
        in_specs=[pl.BlockSpec((rows, D_MODEL), lambda i: (i, 0)), const((1, D_MODEL)), const((N_EXPERTS, D_MODEL)),
                  const((N_EXPERTS, 1)), const((N_EXPERTS, 1))],
        out_specs=[pl.BlockSpec((rows, D_MODEL // 2), lambda i: (i, 0)), tok_blk, tok_blk, tok_blk,
                   const((N_EXPERTS, 1)), const((N_EXPERTS, 1)), const((1, nb_lanes)), const((1, nb_lanes)),
                   const((1, nb_lanes))],
        out_shape=[jax.ShapeDtypeStruct((n, D_MODEL // 2), U32), jax.ShapeDtypeStruct((TOP_K, n), I32),
                   jax.ShapeDtypeStruct((TOP_K, n), F32), jax.ShapeDtypeStruct((TOP_K, n), I32),
                   jax.ShapeDtypeStruct((N_EXPERTS, 1), F32), jax.ShapeDtypeStruct((N_EXPERTS, 1), I32),
                   jax.ShapeDtypeStruct((1, nb_lanes), I32), jax.ShapeDtypeStruct((1, nb_lanes), I32),
                   jax.ShapeDtypeStruct((1, nb_lanes), I32)],
        scratch_shapes=[pltpu.VMEM((N_EXPERTS, 1), F32)],
        compiler_params=pltpu.CompilerParams(dimension_semantics=("arbitrary",), vmem_limit_bytes=VMEM_LIMIT),
        name="route",
    )(x1, g_ffn, wr_t, br, cnt_in)


def _sc_mesh():
    return plsc.VectorSubcoreMesh(core_axis_name="core", subcore_axis_name="subcore")


def _sc_worker_id():
    return lax.axis_index("core") * SC_SUBCORES + lax.axis_index("subcore")


def _scatter_rows(xa, xb, dest, n_rows):
    ch = SC_CHUNK
    na, w = xa.shape
    n = na + xb.shape[0]
    nk = dest.shape[0]
    assert na % ch == 0 and n % ch == 0 and dest.shape[1] == n and xb.shape[1] == w and xa.dtype == xb.dtype
    n_chunks = n // ch
    dest_c = dest.reshape(nk, n_chunks, ch).transpose(1, 0, 2).reshape(-1)
    dma = pltpu.SemaphoreType.DMA

    @pl.kernel(out_type=jax.ShapeDtypeStruct((n_rows, w), xa.dtype), mesh=_sc_mesh(),
               scratch_types=[pltpu.VMEM((nk * ch,), I32), pltpu.VMEM((ch, w), xa.dtype), dma] + [dma] * nk)
    def scatter_kernel(xa_hbm, xb_hbm, d_hbm, o_hbm, idx_v, buf, sem_i, *sem_s):
        wid = _sc_worker_id()

        @pl.loop(0, -(-n_chunks // SC_WORKERS))
        def _(j):
            c = j * SC_WORKERS + wid

            @pl.when(c < n_chunks)
            def _():
                load_idx = pltpu.make_async_copy(d_hbm.at[pl.ds(c * (nk * ch), nk * ch)], idx_v, sem_i)
                load_idx.start()

                @pl.when(c < na // ch)
                def _():
                    pltpu.sync_copy(xa_hbm.at[pl.ds(c * ch, ch)], buf)

                @pl.when(c >= na // ch)
                def _():
                    pltpu.sync_copy(xb_hbm.at[pl.ds(c * ch - na, ch)], buf)

                load_idx.wait()
                puts = [pltpu.make_async_copy(buf, o_hbm.at[idx_v.at[pl.ds(kk * ch, ch)]], sem_s[kk])
                        for kk in range(nk)]
                for cp in puts:
                    cp.start()
                for cp in puts:
                    cp.wait()

    return scatter_kernel(xa, xb, dest_c)


def _gather_rows(src, idx):
    ch = SC_CHUNK
    m = idx.shape[0]
    w = src.shape[1]
    per = m // SC_WORKERS
    n_pairs = per // (2 * ch)
    assert m % SC_WORKERS == 0 and per % (2 * ch) == 0
    dma = pltpu.SemaphoreType.DMA

    @pl.kernel(out_type=jax.ShapeDtypeStruct((m, w), src.dtype), mesh=_sc_mesh(),
               scratch_types=[pltpu.VMEM((per,), I32), pltpu.VMEM((ch, w), src.dtype), pltpu.VMEM((ch, w), src.dtype),
                              dma, dma, dma, dma])
    def gather_kernel(s_hbm, i_hbm, o_hbm, idx_v, buf_a, buf_b, sem_ga, sem_gb, sem_wa, sem_wb):
        base = _sc_worker_id() * per
        pltpu.sync_copy(i_hbm.at[pl.ds(base, per)], idx_v)

        def fetch(j, buf, sem):
            return pltpu.make_async_copy(s_hbm.at[idx_v.at[pl.ds(j * ch, ch)]], buf, sem)

        def put(j, buf, sem):
            return pltpu.make_async_copy(buf, o_hbm.at[pl.ds(base + j * ch, ch)], sem)

        fetch(0, buf_a, sem_ga).start()

        @pl.loop(0, n_pairs)
        def _(p):
            j0 = 2 * p
            j1 = j0 + 1

            @pl.when(p > 0)
            def _():
                put(j1 - 2, buf_b, sem_wb).wait()

            fetch(j1, buf_b, sem_gb).start()
            fetch(j0, buf_a, sem_ga).wait()
            put(j0, buf_a, sem_wa).start()
            fetch(j1, buf_b, sem_gb).wait()
            put(j1, buf_b, sem_wb).start()
            put(j0, buf_a, sem_wa).wait()

            @pl.when(p + 1 < n_pairs)
            def _():
                fetch(j0 + 2, buf_a, sem_ga).start()

        put(2 * n_pairs - 1, buf_b, sem_wb).wait()

    return gather_kernel(src, idx)


def _moe_kernel(blk_e_ref, blk_valid_ref, blk_next_ref, xs_ref, wgu_hbm, bgu_ref, wd_hbm, bd_ref, y_ref,
                wgu_f32, wd_f32, wgu_bf, wd_bf, sems, *, e0):
    i = pl.program_id(0)
    e = blk_e_ref[i]
    n_valid = blk_valid_ref[i]
    used = n_valid > 0
    new_expert = (i == 0) | (blk_e_ref[jnp.maximum(i - 1, 0)] != e)

    def weight_copies(expert):
        return (pltpu.make_async_copy(wgu_hbm.at[e0 + expert], wgu_f32, sems.at[0]),
                pltpu.make_async_copy(wd_hbm.at[e0 + expert], wd_f32, sems.at[1]))

    @pl.when(used & (i == 0))
    def _():
        for cp in weight_copies(e):
            cp.start()

    @pl.when(used & new_expert)
    def _():
        for cp in weight_copies(e):
            cp.wait()
        wgu_bf[...] = wgu_f32[...].astype(BF16)
        wd_bf[...] = wd_f32[...].astype(BF16)
        nxt = blk_next_ref[i]

        @pl.when(nxt >= 0)
        def _():
            for cp in weight_copies(nxt):
                cp.start(priority=1)

    @pl.when(used)
    def _():
        valid = lax.broadcasted_iota(I32, xs_ref.shape, 0) < n_valid
        xb = _unpack_halves(jnp.where(valid, xs_ref[...], jnp.uint32(0))).astype(BF16)
        gu = jnp.dot(xb, wgu_bf[...], preferred_element_type=F32) + bgu_ref[0]
        g = jnp.minimum(gu[:, :D_FF], SWIGLU_LIMIT)
        up = jnp.clip(gu[:, D_FF:], -SWIGLU_LIMIT, SWIGLU_LIMIT)
        act = (up + 1.0) * (g * jax.nn.sigmoid(SWIGLU_ALPHA * g))
        y_ref[...] = _pack_halves(jnp.dot(act.astype(BF16), wd_bf[...], preferred_element_type=F32) + bd_ref[0])

    @pl.when(jnp.logical_not(used))
    def _():
        y_ref[...] = jnp.zeros_like(y_ref)


def _moe(xs, blk_e, blk_valid, blk_next, layer, w_gu, b_gu, w_d, b_d):
    n_rows = xs.shape[0]
    bm = MOE_BM
    assert n_rows % bm == 0
    e0 = layer * N_EXPERTS
    n_we = w_gu.shape[0] * w_gu.shape[1]
    any_spec = pl.BlockSpec(memory_space=pl.ANY)
    grid_spec = pltpu.PrefetchScalarGridSpec(
        num_scalar_prefetch=3,
        grid=(n_rows // bm,),
        in_specs=[pl.BlockSpec((bm, D_MODEL // 2), lambda i, be, bv, bn: (i, 0)),
                  any_spec,
                  pl.BlockSpec((1, 1, 2 * D_FF), lambda i, be, bv, bn: (e0 + be[i], 0, 0)),
                  any_spec,
                  pl.BlockSpec((1, 1, D_MODEL), lambda i, be, bv, bn: (e0 + be[i], 0, 0))],
        out_specs=pl.BlockSpec((bm, D_MODEL // 2), lambda i, be, bv, bn: (i, 0)),
        scratch_shapes=[pltpu.VMEM((D_MODEL, 2 * D_FF), F32), pltpu.VMEM((D_FF, D_MODEL), F32),
                        pltpu.VMEM((D_MODEL, 2 * D_FF), BF16), pltpu.VMEM((D_FF, D_MODEL), BF16),
                        pltpu.SemaphoreType.DMA((2,))],
    )
    return pl.pallas_call(
        functools.partial(_moe_kernel, e0=e0),
        grid_spec=grid_spec,
        out_shape=jax.ShapeDtypeStruct((n_rows, D_MODEL // 2), U32),
        compiler_params=pltpu.CompilerParams(dimension_semantics=("arbitrary",), vmem_limit_bytes=VMEM_LIMIT),
        name="moe_experts",
    )(blk_e, blk_valid, blk_next, xs, w_gu.reshape(n_we, D_MODEL, 2 * D_FF), b_gu.reshape(n_we, 1, 2 * D_FF),
      w_d.reshape(n_we, D_FF, D_MODEL), b_d.reshape(n_we, 1, D_MODEL))


def _ple_kernel(x1_ref, y0_ref, y1_ref, y2_ref, y3_ref, gates_ref, p_ref, g_ref, wg_ref, wp_ref, *rest):
    out_ref, wg_bf, wp_bf = rest[-3:]

    @pl.when(pl.program_id(0) == 0)
    def _():
        wg_bf[...] = wg_ref[...].astype(BF16)
        wp_bf[...] = wp_ref[...].astype(BF16)

    x2 = x1_ref[...]
    gates = gates_ref[...]
    for kk, y_ref in enumerate((y0_ref, y1_ref, y2_ref, y3_ref)):
        x2 = x2 + _unpack_halves(y_ref[...]) * gates[:, kk:kk + 1]
    hp = _rms(x2, g_ref[...]).astype(BF16)
    gate = jax.nn.sigmoid(jnp.dot(hp, wg_bf[...], preferred_element_type=F32))
    pp = jnp.dot(p_ref[...].astype(BF16), wp_bf[...], preferred_element_type=F32)
    out_ref[...] = x2 + gate * pp


def _ple(x1, tok0, n, y_tok, y0, gates_t, p_all, p0, g_ple, w_gate, w_proj, rows, out_into=None):
    assert n % rows == 0 and tok0 % rows == 0 and y0 % rows == 0 and p0 % rows == 0
    const = lambda shape: pl.BlockSpec(shape, lambda i: (0,) * len(shape))
    tok_blk = lambda width: pl.BlockSpec((rows, width), lambda i: (tok0 // rows + i, 0))
    y_blk = lambda kk: pl.BlockSpec((rows, D_MODEL // 2), lambda i: ((y0 + kk * n) // rows + i, 0))
    extra_specs, extra_args, aliases = [], [], {}
    if out_into is not None:
        assert out_into.shape == x1.shape
        extra_specs, extra_args, aliases = [pl.BlockSpec(memory_space=pl.ANY)], [out_into], {10: 0}
    return pl.pallas_call(
        _ple_kernel,
        grid=(n // rows,),
        in_specs=[tok_blk(D_MODEL), y_blk(0), y_blk(1), y_blk(2), y_blk(3), tok_blk(TOP_K),
                  pl.BlockSpec((rows, PLE_DIM), lambda i: (p0 // rows + i, 0)),
                  const((1, D_MODEL)), const((D_MODEL, D_MODEL)), const((PLE_DIM, D_MODEL))] + extra_specs,
        out_specs=tok_blk(D_MODEL),
        out_shape=jax.ShapeDtypeStruct(x1.shape, F32),
        scratch_shapes=[pltpu.VMEM((D_MODEL, D_MODEL), BF16), pltpu.VMEM((PLE_DIM, D_MODEL), BF16)],
        input_output_aliases=aliases,
        compiler_params=pltpu.CompilerParams(dimension_semantics=("arbitrary",), vmem_limit_bytes=VMEM_LIMIT),
        name="combine_ple",
    )(x1, y_tok, y_tok, y_tok, y_tok, gates_t, p_all, g_ple, w_gate, w_proj, *extra_args)


def _rope_tables_const(seq):
    half = ROT_DIM // 2
    d = np.arange(LANES) % HEAD_DIM
    inv = (np.float64(ROPE_THETA) ** (-np.arange(half, dtype=np.float64) / half)).astype(np.float32)
    inv_lane = np.where(d < ROT_DIM, inv[d % half], np.float32(0.0)).astype(np.float32)
    ang = np.arange(seq, dtype=np.float32)[:, None] * inv_lane[None, :]
    cos = np.cos(ang.astype(np.float64)).astype(np.float32)
    sin = np.sin(ang.astype(np.float64)).astype(np.float32)
    sin_a = np.where((d >= half) & (d < ROT_DIM), sin, np.float32(0.0))
    sin_b = np.where(d < half, -sin, np.float32(0.0))
    return tuple(jnp.asarray(a) for a in (cos, sin_a, sin_b))


def _rope_tables(pos):
    half = ROT_DIM // 2
    d = np.arange(LANES) % HEAD_DIM
    inv = ROPE_THETA ** (-jnp.arange(half, dtype=F32) / half)
    inv_lane = jnp.where(d < ROT_DIM, inv[d % half], 0.0)
    ang = pos.astype(F32)[:, None] * inv_lane[None, :]
    cos, sin = jnp.cos(ang), jnp.sin(ang)
    sin_a = jnp.where((d >= half) & (d < ROT_DIM), sin, 0.0)
    sin_b = jnp.where(d < half, -sin, 0.0)
    return cos, sin_a, sin_b


def _layer(layer, xp, xs, ck, cv, st, p_prompt_all, p_sample_all, past_len, refine_tail, tabs_p, tabs_s, finish_state,
           norm_attn, w_in, q_norm, k_norm, attn_sinks, w_pool, pool_scale, w_out,
           norm_ffn, w_router, b_router, w_gate_up_all, b_gate_up_all, w_down_all, b_down_all,
           norm_ple, w_ple_gate, w_ple_proj):
    t, ns = xp.shape[0], xs.shape[0]
    g_attn = norm_attn.reshape(1, D_MODEL)
    qn = jnp.tile(q_norm, 2).reshape(1, LANES)
    kn = jnp.tile(k_norm, 2).reshape(1, LANES)
    lane = np.arange(LANES)
    hmean = jnp.asarray((lane[:, None] // HEAD_DIM == lane[None, :] // HEAD_DIM) / HEAD_DIM, F32)
    pscale = pool_scale.reshape(1, POOL_WIDTH)
    sink8 = jnp.broadcast_to(attn_sinks.reshape(2, Q_TILES).T.reshape(8, 1), (8, LANES))

    mix_args = (g_attn, w_in, qn, kn, hmean, w_pool, pscale, w_out)
    x1p, nk_p, nv_p, nu_p = _mixer_prompt(xp, 0, MIX_ROWS, BF16, attn_sinks, tabs_p, *mix_args)
    if refine_tail:
        x1p, nk_p, nv_p, nu_p = _mixer_prompt(xp, t - 2 * MIX_TAIL_ROWS, MIX_TAIL_ROWS, F32, attn_sinks, tabs_p,
                                              *mix_args, x1_into=x1p)
    st_t = jnp.transpose(st, (1, 0, 2))
    x1s, nk_s, nv_s, nst_t = _mixer_sample(xs, ck.reshape(ns, -1, KV_WIDTH), cv.reshape(ns, -1, KV_WIDTH), st_t,
                                           past_len, tabs_s, sink8, *mix_args)

    g_ffn = norm_ffn.reshape(1, D_MODEL)
    wr_t = w_router.T
    br = b_router.reshape(N_EXPERTS, 1)
    n_tok = t + ns
    bm = MOE_BM
    n_blocks = -(-(n_tok * TOP_K + N_EXPERTS * (bm - 1)) // bm)
    hf_p, idx_p, gate_p, pos_p, cnt_p = _route(x1p, g_ffn, wr_t, br, jnp.zeros((N_EXPERTS, 1), F32),
                                               ROUTE_ROWS, bm, n_blocks)[:5]
    hf_s, idx_s, gate_s, pos_s, _, pad_start, blk_e, blk_valid, blk_next = _route(x1s, g_ffn, wr_t, br, cnt_p, ns,
                                                                                  bm, n_blocks)
    idx = jnp.concatenate([idx_p, idx_s], axis=1)
    pos = jnp.concatenate([pos_p, pos_s], axis=1)
    blk_e, blk_valid, blk_next = (a[0, :n_blocks] for a in (blk_e, blk_valid, blk_next))
    state = finish_state(nk_p, nv_p, nu_p[POOL_HALO - POOL_PREFIX:], nk_s, nv_s, jnp.transpose(nst_t, (1, 0, 2)))
    blk_e, state = lax.optimization_barrier((blk_e, state))
    start_of = jnp.sum(jnp.where(idx[None] == jnp.arange(N_EXPERTS, dtype=I32)[:, None, None],
                                 pad_start.reshape(N_EXPERTS, 1, 1), 0), axis=0)
    dest = start_of + pos

    xs_rows = _scatter_rows(hf_p, hf_s, dest, n_blocks * bm)
    y = _moe(xs_rows, blk_e, blk_valid, blk_next, layer, w_gate_up_all, b_gate_up_all, w_down_all, b_down_all)

    tq = t // COMBINE_PARTS
    unit = SC_WORKERS * SC_CHUNK * 2
    assert t % COMBINE_PARTS == 0 and (TOP_K * tq) % unit == 0 and tq % PLE_ROWS == 0
    g_ple = norm_ple.reshape(1, D_MODEL)
    gates_p = gate_p.T
    yp = None
    for part in range(COMBINE_PARTS):
        back = dest[:, part * tq:(part + 1) * tq].reshape(-1)
        if part == COMBINE_PARTS - 1:
            back = jnp.concatenate([back, dest[:, t:].reshape(-1)])
            back = jnp.concatenate([back, jnp.arange(-back.shape[0] % unit, dtype=I32)])
        y_part = _gather_rows(y, back)
        yp = _ple(x1p, part * tq, tq, y_part, 0, gates_p, p_prompt_all, layer * t + part * tq, g_ple,
                  w_ple_gate, w_ple_proj, PLE_ROWS, out_into=yp)
    ys = _ple(x1s, 0, ns, y_part, TOP_K * tq, gate_s.T, p_sample_all, layer * ns, g_ple, w_ple_gate, w_ple_proj, ns)
    return yp, ys, state


def kernel(x_prompt, x_sample, cache_k, cache_v, state_pool, p_prompt, p_sample, norm_attn, w_in, q_norm, k_norm,
           attn_sinks, w_pool, pool_scale, w_out, norm_ffn, w_router, b_router, w_gate_up, b_gate_up, w_down, b_down,
           norm_ple, w_ple_gate, w_ple_proj):
    depth = norm_attn.shape[0]
    batch, seq, d = x_prompt.shape
    ns, dec_seq, _ = x_sample.shape
    wb = cache_k.shape[2]
    assert batch == 1 and dec_seq == 1 and d == D_MODEL and wb == WINDOW
    assert cache_k.shape[3:] == (N_KV_HEADS, HEAD_DIM) and state_pool.shape[2:] == (POOL_PREFIX, POOL_WIDTH)
    past_len = PAST_LEN
    yp = x_prompt.reshape(seq, d)
    ys = x_sample.reshape(ns, d)
    p_prompt_all = p_prompt.reshape(depth * seq, PLE_DIM)
    p_sample_all = p_sample.reshape(depth * ns, PLE_DIM)
    tabs_p = _rope_tables_const(seq)
    tabs_s = _rope_tables(jnp.full((1,), past_len))
    kv_p, kv_s = (1, WINDOW, N_KV_HEADS, HEAD_DIM), (ns, wb, N_KV_HEADS, HEAD_DIM)
    state_shapes = (kv_p, kv_p, (1, POOL_PREFIX, POOL_WIDTH), kv_s, kv_s, (ns, POOL_PREFIX, POOL_WIDTH))
    per_layer = []

    def shaped(*state):
        return tuple(a.reshape(shp) for a, shp in zip(state, state_shapes))

    def stacked(*state):
        prevs = list(zip(*per_layer)) or [()] * len(state_shapes)
        return tuple(jnp.stack(list(prev) + [cur]) for prev, cur in zip(prevs, shaped(*state)))

    for i in range(depth):
        yp, ys, state = _layer(i, yp, ys, cache_k[i], cache_v[i], state_pool[i], p_prompt_all, p_sample_all, past_len,
                               i < depth - 1, tabs_p, tabs_s, shaped if i < depth - 1 else stacked,
                               norm_attn[i], w_in[i], q_norm[i], k_norm[i], attn_sinks[i], w_pool[i], pool_scale[i],
                               w_out[i], norm_ffn[i], w_router[i], b_router[i], w_gate_up, b_gate_up, w_down, b_down,
                               norm_ple[i], w_ple_gate[i], w_ple_proj[i])
        per_layer.append(state)
    return (yp.reshape(batch, seq, d), ys.reshape(ns, dec_seq, d)) + tuple(state)
```

```python
import functools

import jax
import jax.numpy as jnp
import numpy as np
from jax import lax
from jax.experimental import pallas as pl
from jax.experimental.pallas import tpu as pltpu
from jax.experimental.pallas import tpu_sc as plsc

F32 = jnp.float32
BF16 = jnp.bfloat16
U32 = jnp.uint32
I32 = jnp.int32

D_MODEL = 1024
HEAD_DIM = 64
N_HEADS = 8
N_KV_HEADS = 2
GROUP = N_HEADS // N_KV_HEADS
ATTN_WIDTH = N_HEADS * HEAD_DIM
KV_WIDTH = N_KV_HEADS * HEAD_DIM
POOL_WIDTH = 512
POOL_WINDOWS = (2, 4, 8, 16)
POOL_GC = POOL_WIDTH // len(POOL_WINDOWS)
POOL_PREFIX = max(POOL_WINDOWS) - 1
POOL_HALO = POOL_PREFIX + 1
POOL_PAD = 8
IN_WIDTH = ATTN_WIDTH + 2 * KV_WIDTH + POOL_WIDTH
WINDOW = 128
ROPE_THETA = 500000.0
ROT_DIM = HEAD_DIM // 4
N_EXPERTS = 32
TOP_K = 4
D_FF = 1024
SWIGLU_ALPHA = 1.702
SWIGLU_LIMIT = 7.0
PLE_DIM = 256
PAST_LEN = 16384
EPS = 1e-5
NEG_INF = -1e30

LANES = 128
Q_TILES = ATTN_WIDTH // LANES

MIX_ROWS = 1024
MIX_TAIL_ROWS = 128
SAMPLE_CHUNK = 16
ROUTE_ROWS = 1024
MOE_BM = 512
SC_CORES = 2
SC_SUBCORES = 16
SC_WORKERS = SC_CORES * SC_SUBCORES
SC_CHUNK = 64
PLE_ROWS = 1024
COMBINE_PARTS = 2
VMEM_LIMIT = 56 * 1024 * 1024


def _rms(x, g):
    return x * lax.rsqrt(jnp.mean(x * x, axis=-1, keepdims=True) + EPS) * g


def _pack_halves(x):
    w = x.shape[1] // 2
    lo = lax.bitcast_convert_type(x[:, :w].astype(BF16).astype(F32), U32) >> 16
    hi = lax.bitcast_convert_type(x[:, w:].astype(BF16).astype(F32), U32) & jnp.uint32(0xFFFF0000)
    return lo | hi


def _unpack_halves(packed):
    lo = lax.bitcast_convert_type(packed << 16, F32)
    hi = lax.bitcast_convert_type(packed & jnp.uint32(0xFFFF0000), F32)
    return jnp.concatenate([lo, hi], axis=1)


def _mm(a, b, nt=False):
    dims = (((1,), (1 if nt else 0,)), ((), ()))
    if b.dtype == F32:
        return lax.dot_general(a.astype(F32), b, dims, preferred_element_type=F32, precision=lax.Precision.HIGHEST)
    return lax.dot_general(a.astype(BF16), b, dims, preferred_element_type=F32)


def _head_norm_rope(t, hmean, gain, cos, sin_a, sin_b):
    t = t * lax.rsqrt(_mm(t * t, hmean) + EPS) * gain
    return t * cos + pltpu.roll(t, ROT_DIM // 2, axis=1) * sin_a + pltpu.roll(t, LANES - ROT_DIM // 2, axis=1) * sin_b


def _softmax_pv(s, sink, v, ones):
    m = jnp.maximum(jnp.max(s, axis=-1, keepdims=True), sink)
    e = jnp.exp(s - m).astype(v.dtype)
    den = _mm(e, ones) + jnp.exp(sink - m)
    return _mm(e, v) / den


def _prep_mixer_weights(win_ref, wout_ref, win_c, wout_c):
    cdt = win_c.dtype
    for j in range(Q_TILES):
        for s in range(2):
            src = (j + Q_TILES * s) * HEAD_DIM
            dst = j * LANES + s * HEAD_DIM
            win_c[:, dst:dst + HEAD_DIM] = win_ref[:, src:src + HEAD_DIM].astype(cdt)
            wout_c[dst:dst + HEAD_DIM, :] = wout_ref[src:src + HEAD_DIM, :].astype(cdt)
    win_c[:, ATTN_WIDTH:] = win_ref[:, ATTN_WIDTH:].astype(cdt)
    wout_c[ATTN_WIDTH:, :] = wout_ref[ATTN_WIDTH:, :].astype(cdt)


def _mixer_prompt_kernel(sinks_ref, x_ref, cos_ref, sa_ref, sb_ref, g_ref, win_ref, qn_ref, kn_ref, hm_ref,
                         wpool_ref, pscale_ref, wout_ref, *rest, row_offset, aliased):
    (x1_ref, klast_ref, vlast_ref, ulast_ref, kprev, vprev, uext, mix, win_c, wout_c,
     *lvl) = rest[1:] if aliased else rest
    i = pl.program_id(0)
    rows = x_ref.shape[0]
    n_sub = rows // WINDOW
    cdt = win_c.dtype
    row0 = row_offset + i * rows

    @pl.when(i == 0)
    def _():
        _prep_mixer_weights(win_ref, wout_ref, win_c, wout_c)
        kprev[...] = jnp.zeros_like(kprev)
        vprev[...] = jnp.zeros_like(vprev)
        uext[0:POOL_PAD + POOL_HALO, :] = jnp.zeros((POOL_PAD + POOL_HALO, POOL_WIDTH), F32)
        for buf in lvl:
            buf[0:POOL_PAD, :] = jnp.zeros((POOL_PAD, POOL_GC), F32)

    x = x_ref[...]
    proj = _mm(_rms(x, g_ref[...]), win_c[...])
    cos, sin_a, sin_b = cos_ref[...], sa_ref[...], sb_ref[...]

    n_t = Q_TILES + 1
    t_all = jnp.concatenate([proj[:, j * LANES:(j + 1) * LANES] for j in range(n_t)], axis=0)
    t3 = (t_all * lax.rsqrt(_mm(t_all * t_all, hm_ref[...].astype(cdt)) + EPS)).reshape(n_t, rows, LANES)
    t3 = jnp.concatenate([t3[:Q_TILES] * (qn_ref[...] * HEAD_DIM ** -0.5), t3[Q_TILES:] * kn_ref[...]], axis=0)
    t2 = t3.reshape(n_t * rows, LANES)
    t3 = (t3 * cos + pltpu.roll(t2, ROT_DIM // 2, axis=1).reshape(n_t, rows, LANES) * sin_a
          + pltpu.roll(t2, LANES - ROT_DIM // 2, axis=1).reshape(n_t, rows, LANES) * sin_b)
    q3 = t3[:Q_TILES]
    k = t3[Q_TILES]
    v = proj[:, ATTN_WIDTH + KV_WIDTH:ATTN_WIDTH + 2 * KV_WIDTH]
    u = proj[:, ATTN_WIDTH + 2 * KV_WIDTH:]
    klast_ref[...] = k[rows - WINDOW:, :]
    vlast_ref[...] = v[rows - WINDOW:, :]
    ulast_ref[...] = u[rows - POOL_HALO:, :]
    k_c = k.astype(cdt)
    v_c = jnp.concatenate([v.astype(cdt), jnp.ones((rows, LANES), cdt)], axis=1)
    v_first = jnp.concatenate([vprev[...], jnp.ones((WINDOW, LANES), cdt)], axis=1)

    lane = lax.broadcasted_iota(I32, (WINDOW, LANES), 1)
    left = (lane < HEAD_DIM)[None]
    qi = lax.broadcasted_iota(I32, (WINDOW, 2 * WINDOW), 0)
    kj = lax.broadcasted_iota(I32, (WINDOW, 2 * WINDOW), 1)
    band = (kj - qi >= 1) & (kj - qi <= WINDOW)
    sink3 = jnp.concatenate([jnp.full((1, 1, 1), sinks_ref[j + Q_TILES * s], F32)
                             for j in range(Q_TILES) for s in range(2)], axis=0)
    n_g = 2 * Q_TILES
    fill = jnp.where(kj[0:1][None] == 0, sink3, NEG_INF)
    slot = lax.broadcasted_iota(I32, (2 * WINDOW, 2 * LANES), 0) == 0
    slot_v = slot & (lax.broadcasted_iota(I32, (2 * WINDOW, 2 * LANES), 1) < LANES)

    v_cats, masks, scores = [], [], []
    for c in range(n_sub):
        r0 = c * WINDOW
        if c == 0:
            k_cat = jnp.concatenate([kprev[...], k_c[0:WINDOW]], axis=0)
            v_cat = jnp.concatenate([v_first, v_c[0:WINDOW]], axis=0)
            masks.append(band & (kj + (row0 - WINDOW) >= 0))
        else:
            k_cat = k_c[r0 - WINDOW:r0 + WINDOW]
            v_cat = v_c[r0 - WINDOW:r0 + WINDOW]
            masks.append(band)
        v_cats.append(jnp.where(slot_v, jnp.zeros_like(v_cat), v_cat))
        q_c = q3[:, r0:r0 + WINDOW, :]
        q_all = jnp.concatenate([jnp.where(left, q_c, 0.0), jnp.where(left, 0.0, q_c)], axis=1)
        scores.append(_mm(q_all.reshape(n_g * WINDOW, LANES), k_cat, nt=True))
    kprev[...] = k_c[rows - WINDOW:]
    vprev[...] = v[rows - WINDOW:].astype(cdt)

    probs = []
    for c in range(n_sub):
        s = jnp.where(masks[c][None], scores[c].reshape(n_g, WINDOW, 2 * WINDOW), fill)
        m = jnp.max(s, axis=-1, keepdims=True)
        probs.append(jnp.exp(s - m).astype(cdt).reshape(n_g * WINDOW, 2 * WINDOW))
    applied = [_mm(probs[c], v_cats[c]) for c in range(n_sub)]
    for c in range(n_sub):
        r0 = c * WINDOW
        o = (applied[c][:, :LANES] / applied[c][:, LANES:]).reshape(Q_TILES, 2 * WINDOW, LANES)
        a = jnp.where(left, o[:, :WINDOW], o[:, WINDOW:])
        for j in range(Q_TILES):
            mix[r0:r0 + WINDOW, j * LANES:(j + 1) * LANES] = a[j].astype(cdt)

    base = POOL_PAD + POOL_HALO
    ext = base + rows
    uext[base:ext, :] = u
    pos1 = (lax.broadcasted_iota(I32, (rows, 1), 0) + row0 + 1).astype(F32)
    lvl_of = {1: lvl[0:1], 2: lvl[1:3], 3: lvl[3:5]}
    for gi, w in enumerate(POOL_WINDOWS):
        cols = slice(gi * POOL_GC, (gi + 1) * POOL_GC)
        src, src_cols = uext, cols
        for level in range(1, gi + 2):
            sft = 1 << (level - 1)
            if level <= gi:
                dst = lvl_of[gi][(level - 1) % 2]
                dst[POOL_PAD:ext, :] = src[POOL_PAD:ext, src_cols] + src[POOL_PAD - sft:ext - sft, src_cols]
                src, src_cols = dst, slice(None)
            else:
                wsum = src[base:ext, src_cols] + src[base - sft:ext - sft, src_cols]
        d = wsum / jnp.minimum(pos1, float(w)) - u[:, cols]
        y = _mm(d, wpool_ref[gi].astype(cdt)) * pscale_ref[:, cols]
        mix[:, ATTN_WIDTH + gi * POOL_GC:ATTN_WIDTH + (gi + 1) * POOL_GC] = y.astype(cdt)
    uext[POOL_PAD:base, :] = u[rows - POOL_HALO:, :]

    x1_ref[...] = x + _mm(mix[...], wout_c[...])


def _mixer_prompt(x_full, row_offset, rows, cdt, sinks, tabs, g_attn, w_in, qn, kn, hmean, w_pool, pscale, w_out,
                  x1_into=None):
    t = x_full.shape[0] - row_offset
    assert t % rows == 0 and row_offset % rows == 0 and rows % WINDOW == 0 and rows >= POOL_HALO
    blk0, n_steps = row_offset // rows, t // rows
    const = lambda shape: pl.BlockSpec(shape, lambda i, *_: (0,) * len(shape))
    row_blk = lambda width: pl.BlockSpec((rows, width), lambda i, *_: (blk0 + i, 0))
    aliased = x1_into is not None
    if aliased:
        assert x1_into.shape == x_full.shape
        x1_spec = pl.BlockSpec((rows, D_MODEL), lambda i, *_: (blk0 + n_steps - 1, 0))
        x1_shape = x1_into.shape
        extra_specs, extra_args, aliases = [pl.BlockSpec(memory_space=pl.ANY)], [x1_into], {13: 0}
    else:
        x1_spec = pl.BlockSpec((rows, D_MODEL), lambda i, *_: (i, 0))
        x1_shape = (t, D_MODEL)
        extra_specs, extra_args, aliases = [], [], {}
    grid_spec = pltpu.PrefetchScalarGridSpec(
        num_scalar_prefetch=1,
        grid=(n_steps,),
        in_specs=[row_blk(D_MODEL), row_blk(LANES), row_blk(LANES), row_blk(LANES),
                  const((1, D_MODEL)), const((D_MODEL, IN_WIDTH)), const((1, LANES)), const((1, LANES)),
                  const((LANES, LANES)), const((len(POOL_WINDOWS), POOL_GC, POOL_GC)), const((1, POOL_WIDTH)),
                  const((D_MODEL, D_MODEL))] + extra_specs,
        out_specs=[x1_spec, const((WINDOW, KV_WIDTH)), const((WINDOW, KV_WIDTH)),
                   const((POOL_HALO, POOL_WIDTH))],
        scratch_shapes=[pltpu.VMEM((WINDOW, KV_WIDTH), cdt), pltpu.VMEM((WINDOW, KV_WIDTH), cdt),
                        pltpu.VMEM((POOL_PAD + POOL_HALO + rows, POOL_WIDTH), F32), pltpu.VMEM((rows, D_MODEL), cdt),
                        pltpu.VMEM((D_MODEL, IN_WIDTH), cdt), pltpu.VMEM((D_MODEL, D_MODEL), cdt)]
        + [pltpu.VMEM((POOL_PAD + POOL_HALO + rows, POOL_GC), F32)] * 5,
    )
    return pl.pallas_call(
        functools.partial(_mixer_prompt_kernel, row_offset=row_offset, aliased=aliased),
        grid_spec=grid_spec,
        out_shape=[jax.ShapeDtypeStruct(x1_shape, F32), jax.ShapeDtypeStruct((WINDOW, KV_WIDTH), F32),
                   jax.ShapeDtypeStruct((WINDOW, KV_WIDTH), F32), jax.ShapeDtypeStruct((POOL_HALO, POOL_WIDTH), F32)],
        input_output_aliases=aliases,
        compiler_params=pltpu.CompilerParams(dimension_semantics=("arbitrary",), vmem_limit_bytes=VMEM_LIMIT),
        name="mixer_prompt",
    )(sinks, x_full, *tabs, g_attn, w_in, qn, kn, hmean, w_pool, pscale, w_out, *extra_args)


def _mixer_sample_kernel(x_ref, ck_ref, cv_ref, st_ref, cos_ref, sa_ref, sb_ref, sink8_ref, g_ref, win_ref, qn_ref,
                         kn_ref, hm_ref, wpool_ref, pscale_ref, wout_ref,
                         x1_ref, nk_ref, nv_ref, nst_ref, o8, win_c, wout_c, *, pos):
    nb = x_ref.shape[0]
    wb = ck_ref.shape[1]

    @pl.when(pl.program_id(0) == 0)
    def _():
        _prep_mixer_weights(win_ref, wout_ref, win_c, wout_c)

    x = x_ref[...]
    h = _rms(x, g_ref[...]).astype(BF16)
    proj = jnp.dot(h, win_c[...], preferred_element_type=F32)
    cos, sin_a, sin_b = cos_ref[...], sa_ref[...], sb_ref[...]
    hmean = hm_ref[...].astype(BF16)
    k = _head_norm_rope(proj[:, ATTN_WIDTH:ATTN_WIDTH + KV_WIDTH], hmean, kn_ref[...], cos, sin_a, sin_b)
    v = proj[:, ATTN_WIDTH + KV_WIDTH:ATTN_WIDTH + 2 * KV_WIDTH]
    u = proj[:, ATTN_WIDTH + 2 * KV_WIDTH:]

    nk_ref[:, 0:wb - 1, :] = ck_ref[:, 1:wb, :]
    nv_ref[:, 0:wb - 1, :] = cv_ref[:, 1:wb, :]
    for b in range(nb):
        nk_ref[b, wb - 1:wb, :] = k[b:b + 1, :]
        nv_ref[b, wb - 1:wb, :] = v[b:b + 1, :]

    r8 = lax.broadcasted_iota(I32, (nb * 8, LANES), 0)
    lane8 = lax.broadcasted_iota(I32, (nb * 8, LANES), 1)
    keep = (lane8 < HEAD_DIM) == (r8 % 2 == 0)
    rep = (lax.broadcasted_iota(I32, (nb * 8, nb), 0) // 8 == lax.broadcasted_iota(I32, (nb * 8, nb), 1)).astype(BF16)
    q8 = jnp.zeros((nb * 8, LANES), F32)
    scale = HEAD_DIM ** -0.5
    for j in range(Q_TILES):
        qt = _head_norm_rope(proj[:, j * LANES:(j + 1) * LANES], hmean, qn_ref[...], cos, sin_a, sin_b) * scale
        qrep = jnp.dot(rep, qt.astype(BF16), preferred_element_type=F32)
        q8 = jnp.where(keep & ((r8 % 8) // 2 == j), qrep, q8)
    q8 = q8.astype(BF16)

    sink8 = sink8_ref[:, 0:1]
    ones_bf = jnp.ones((wb, LANES), BF16)
    assert pos >= wb - 1 and wb <= WINDOW
    for b in range(nb):
        kb = nk_ref[b].astype(BF16)
        vb = nv_ref[b].astype(BF16)
        s = lax.dot_general(q8[b * 8:(b + 1) * 8], kb, (((1,), (1,)), ((), ())), preferred_element_type=F32)
        o8[b * 8:(b + 1) * 8, :] = _softmax_pv(s, sink8, vb, ones_bf)
    o8m = jnp.where(keep, o8[...], 0.0).astype(BF16)

    a_tiles = []
    sel_r = lax.broadcasted_iota(I32, (nb, nb * 8), 1)
    sel_b = lax.broadcasted_iota(I32, (nb, nb * 8), 0)
    for j in range(Q_TILES):
        sel = ((sel_r // 8 == sel_b) & ((sel_r % 8) // 2 == j)).astype(BF16)
        a_tiles.append(jnp.dot(sel, o8m, preferred_element_type=F32))

    z_tiles = []
    for gi, w in enumerate(POOL_WINDOWS):
        cols = slice(gi * POOL_GC, (gi + 1) * POOL_GC)
        wsum = u[:, cols]
        for sft in range(1, w):
            wsum = wsum + st_ref[POOL_PREFIX - sft, :, cols]
        d = wsum / float(min(pos + 1, w)) - u[:, cols]
        z_tiles.append(jnp.dot(d.astype(BF16), wpool_ref[gi].astype(BF16), preferred_element_type=F32)
                       * pscale_ref[:, cols])
    nst_ref[0:POOL_PREFIX - 1] = st_ref[1:POOL_PREFIX]
    nst_ref[POOL_PREFIX - 1] = u

    mixv = jnp.concatenate(a_tiles + z_tiles, axis=1).astype(BF16)
    x1_ref[...] = x + jnp.dot(mixv, wout_c[...], preferred_element_type=F32)


def _mixer_sample(x, ck, cv, st, pos, tabs, sink8, g_attn, w_in, qn, kn, hmean, w_pool, pscale, w_out):
    n, wb = ck.shape[0], ck.shape[1]
    nb = SAMPLE_CHUNK
    assert n % nb == 0
    const = lambda shape: pl.BlockSpec(shape, lambda i: (0,) * len(shape))
    cache_blk = pl.BlockSpec((nb, wb, KV_WIDTH), lambda i: (i, 0, 0))
    st_blk = pl.BlockSpec((POOL_PREFIX, nb, POOL_WIDTH), lambda i: (0, i, 0))
    x_blk = pl.BlockSpec((nb, D_MODEL), lambda i: (i, 0))
    return pl.pallas_call(
        functools.partial(_mixer_sample_kernel, pos=pos),
        grid=(n // nb,),
        in_specs=[x_blk, cache_blk, cache_blk, st_blk, const((1, LANES)), const((1, LANES)), const((1, LANES)),
                  const((8, LANES)), const((1, D_MODEL)), const((D_MODEL, IN_WIDTH)), const((1, LANES)),
                  const((1, LANES)), const((LANES, LANES)), const((len(POOL_WINDOWS), POOL_GC, POOL_GC)),
                  const((1, POOL_WIDTH)), const((D_MODEL, D_MODEL))],
        out_specs=[x_blk, cache_blk, cache_blk, st_blk],
        out_shape=[jax.ShapeDtypeStruct((n, D_MODEL), F32), jax.ShapeDtypeStruct(ck.shape, F32),
                   jax.ShapeDtypeStruct(cv.shape, F32), jax.ShapeDtypeStruct(st.shape, F32)],
        scratch_shapes=[pltpu.VMEM((nb * 8, LANES), F32), pltpu.VMEM((D_MODEL, IN_WIDTH), BF16),
                        pltpu.VMEM((D_MODEL, D_MODEL), BF16)],
        compiler_params=pltpu.CompilerParams(dimension_semantics=("arbitrary",), vmem_limit_bytes=VMEM_LIMIT),
        name="mixer_sample",
    )(x, ck, cv, st, *tabs, sink8, g_attn, w_in, qn, kn, hmean, w_pool, pscale, w_out)


def _block_plan(cnt, bm, n_lanes):
    e_sub = lax.broadcasted_iota(I32, (N_EXPERTS, LANES), 0)
    e_lane = lax.broadcasted_iota(I32, (N_EXPERTS, LANES), 1)
    padded = jnp.floor((cnt + (bm - 1)) / bm) * bm
    padded_lane = jnp.sum(jnp.where(e_sub == e_lane, padded, 0.0), axis=0, keepdims=True)
    pad_end = jnp.sum(jnp.where(e_lane <= e_sub, padded_lane, 0.0), axis=1, keepdims=True)
    pad_start = pad_end - padded
    blk_start = lax.broadcasted_iota(I32, (N_EXPERTS, n_lanes), 1).astype(F32) * bm
    blk_e = jnp.minimum(jnp.sum((pad_end <= blk_start).astype(F32), axis=0, keepdims=True), N_EXPERTS - 1.0)
    mine = lax.broadcasted_iota(I32, (N_EXPERTS, n_lanes), 0).astype(F32) == blk_e
    last = jnp.sum(jnp.where(mine, pad_start + cnt, 0.0), axis=0, keepdims=True)
    blk_valid = jnp.clip(last - blk_start[0:1], 0.0, float(bm))
    e_blk = lax.broadcasted_iota(I32, (N_EXPERTS, n_lanes), 0).astype(F32)
    later = jnp.min(jnp.where((e_blk > blk_e) & (cnt > 0.0), e_blk, float(N_EXPERTS)), axis=0, keepdims=True)
    blk_next = jnp.where(later < N_EXPERTS, later, -1.0)
    return pad_start, blk_e, blk_valid, blk_next


def _route_kernel(x1_ref, g_ref, wr_ref, br_ref, cnt_in_ref, hf_ref, idx_ref, gate_ref, pos_ref, cnt_ref,
                  pstart_ref, blke_ref, blkv_ref, blkn_ref, counts, *, bm):
    i = pl.program_id(0)
    rows = x1_ref.shape[0]

    @pl.when(i == 0)
    def _():
        counts[...] = cnt_in_ref[...]

    h = _rms(x1_ref[...], g_ref[...])
    hf_ref[...] = _pack_halves(h)

    logits = lax.dot_general(wr_ref[...], h, (((1,), (1,)), ((), ())), preferred_element_type=F32,
                             precision=lax.Precision.HIGHEST) + br_ref[...]
    eid = lax.broadcasted_iota(I32, (N_EXPERTS, rows), 0).astype(F32)
    work = logits
    vals, hots = [], []
    for kk in range(TOP_K):
        m = jnp.max(work, axis=0, keepdims=True)
        first = jnp.min(jnp.where(work == m, eid, float(N_EXPERTS)), axis=0, keepdims=True)
        hot = eid == first
        work = jnp.where(hot, -jnp.inf, work)
        vals.append(m)
        hots.append(hot)
        idx_ref[kk:kk + 1, :] = first.astype(I32)
    es = [jnp.exp(vv - vals[0]) for vv in vals]
    den = es[0] + es[1] + es[2] + es[3]
    for kk in range(TOP_K):
        gate_ref[kk:kk + 1, :] = es[kk] / den

    chosen = hots[0] | hots[1] | hots[2] | hots[3]
    before = (lax.broadcasted_iota(I32, (rows, rows), 0) < lax.broadcasted_iota(I32, (rows, rows), 1)).astype(BF16)
    rank = jnp.dot(chosen.astype(BF16), before, preferred_element_type=F32) + counts[...]
    for kk in range(TOP_K):
        pos_ref[kk:kk + 1, :] = jnp.sum(jnp.where(hots[kk], rank, 0.0), axis=0, keepdims=True).astype(I32)
    counts[...] = counts[...] + jnp.sum(chosen.astype(F32), axis=1, keepdims=True)
    cnt_ref[...] = counts[...]

    @pl.when(i == pl.num_programs(0) - 1)
    def _():
        pad_start, blk_e, blk_valid, blk_next = _block_plan(counts[...], bm, blke_ref.shape[1])
        pstart_ref[...] = pad_start.astype(I32)
        blke_ref[...] = blk_e.astype(I32)
        blkv_ref[...] = blk_valid.astype(I32)
        blkn_ref[...] = blk_next.astype(I32)


def _route(x1, g_ffn, wr_t, br, cnt_in, rows, bm, n_blocks):
    n = x1.shape[0]
    assert n % rows == 0
    nb_lanes = -(-n_blocks // LANES) * LANES
    const = lambda shape: pl.BlockSpec(shape, lambda i: (0,) * len(shape))
    tok_blk = pl.BlockSpec((TOP_K, rows), lambda i: (0, i))
    return pl.pallas_call(
        functools.partial(_route_kernel, bm=bm),
        grid=(n // rows,),
        in_specs=[pl.BlockSpec((rows, D_MODEL), lambda i: (i, 0)), const((1, D_MODEL)), const((N_EXPERTS, D_MODEL)),
                  const((N_EXPERTS, 1)), const((N_EXPERTS, 1))],
        out_specs=[pl.BlockSpec((rows, D_MODEL // 2), lambda i: (i, 0)), tok_blk, tok_blk, tok_blk,
                   const((N_EXPERTS, 1)), const((N_EXPERTS, 1)), const((1, nb_lanes)), const((1, nb_lanes)),
                   const((1, nb_lanes))],
        out_shape=[jax.ShapeDtypeStruct((n, D_MODEL // 2), U32), jax.ShapeDtypeStruct((TOP_K, n), I32),
                   jax.ShapeDtypeStruct((TOP_K, n), F32), jax.ShapeDtypeStruct((TOP_K, n), I32),
                   jax.ShapeDtypeStruct((N_EXPERTS, 1), F32), jax.ShapeDtypeStruct((N_EXPERTS, 1), I32),
                   jax.ShapeDtypeStruct((1, nb_lanes), I32), jax.ShapeDtypeStruct((1, nb_lanes), I32),
                   jax.ShapeDtypeStruct((1, nb_lanes), I32)],
        scratch_shapes=[pltpu.VMEM((N_EXPERTS, 1), F32)],
        compiler_params=pltpu.CompilerParams(dimension_semantics=("arbitrary",), vmem_limit_bytes=VMEM_LIMIT),
        name="route",
    )(x1, g_ffn, wr_t, br, cnt_in)


def _sc_mesh():
    return plsc.VectorSubcoreMesh(core_axis_name="core", subcore_axis_name="subcore")


def _sc_worker_id():
    return lax.axis_index("core") * SC_SUBCORES + lax.axis_index("subcore")


def _scatter_rows(xa, xb, dest, n_rows):
    ch = SC_CHUNK
    na, w = xa.shape
    n = na + xb.shape[0]
    nk = dest.shape[0]
    assert na % ch == 0 and n % ch == 0 and dest.shape[1] == n and xb.shape[1] == w and xa.dtype == xb.dtype
    n_chunks = n // ch
    dest_c = dest.reshape(nk, n_chunks, ch).transpose(1, 0, 2).reshape(-1)
    dma = pltpu.SemaphoreType.DMA

    @pl.kernel(out_type=jax.ShapeDtypeStruct((n_rows, w), xa.dtype), mesh=_sc_mesh(),
               scratch_types=[pltpu.VMEM((nk * ch,), I32), pltpu.VMEM((ch, w), xa.dtype), dma] + [dma] * nk)
    def scatter_kernel(xa_hbm, xb_hbm, d_hbm, o_hbm, idx_v, buf, sem_i, *sem_s):
        wid = _sc_worker_id()

        @pl.loop(0, -(-n_chunks // SC_WORKERS))
        def _(j):
            c = j * SC_WORKERS + wid

            @pl.when(c < n_chunks)
            def _():
                load_idx = pltpu.make_async_copy(d_hbm.at[pl.ds(c * (nk * ch), nk * ch)], idx_v, sem_i)
                load_idx.start()

                @pl.when(c < na // ch)
                def _():
                    pltpu.sync_copy(xa_hbm.at[pl.ds(c * ch, ch)], buf)

                @pl.when(c >= na // ch)
                def _():
                    pltpu.sync_copy(xb_hbm.at[pl.ds(c * ch - na, ch)], buf)

                load_idx.wait()
                puts = [pltpu.make_async_copy(buf, o_hbm.at[idx_v.at[pl.ds(kk * ch, ch)]], sem_s[kk])
                        for kk in range(nk)]
                for cp in puts:
                    cp.start()
                for cp in puts:
                    cp.wait()

    return scatter_kernel(xa, xb, dest_c)


def _gather_rows(src, idx):
    ch = SC_CHUNK
    m = idx.shape[0]
    w = src.shape[1]
    per = m // SC_WORKERS
    n_pairs = per // (2 * ch)
    assert m % SC_WORKERS == 0 and per % (2 * ch) == 0
    dma = pltpu.SemaphoreType.DMA

    @pl.kernel(out_type=jax.ShapeDtypeStruct((m, w), src.dtype), mesh=_sc_mesh(),
               scratch_types=[pltpu.VMEM((per,), I32), pltpu.VMEM((ch, w), src.dtype), pltpu.VMEM((ch, w), src.dtype),
                              dma, dma, dma, dma])
    def gather_kernel(s_hbm, i_hbm, o_hbm, idx_v, buf_a, buf_b, sem_ga, sem_gb, sem_wa, sem_wb):
        base = _sc_worker_id() * per
        pltpu.sync_copy(i_hbm.at[pl.ds(base, per)], idx_v)

        def fetch(j, buf, sem):
            return pltpu.make_async_copy(s_hbm.at[idx_v.at[pl.ds(j * ch, ch)]], buf, sem)

        def put(j, buf, sem):
            return pltpu.make_async_copy(buf, o_hbm.at[pl.ds(base + j * ch, ch)], sem)

        fetch(0, buf_a, sem_ga).start()

        @pl.loop(0, n_pairs)
        def _(p):
            j0 = 2 * p
            j1 = j0 + 1

            @pl.when(p > 0)
            def _():
                put(j1 - 2, buf_b, sem_wb).wait()

            fetch(j1, buf_b, sem_gb).start()
            fetch(j0, buf_a, sem_ga).wait()
            put(j0, buf_a, sem_wa).start()
            fetch(j1, buf_b, sem_gb).wait()
            put(j1, buf_b, sem_wb).start()
            put(j0, buf_a, sem_wa).wait()

            @pl.when(p + 1 < n_pairs)
            def _():
                fetch(j0 + 2, buf_a, sem_ga).start()

        put(2 * n_pairs - 1, buf_b, sem_wb).wait()

    return gather_kernel(src, idx)


def _moe_kernel(blk_e_ref, blk_valid_ref, blk_next_ref, xs_ref, wgu_hbm, bgu_ref, wd_hbm, bd_ref, y_ref,
                wgu_f32, wd_f32, wgu_bf, wd_bf, sems, *, e0):
    i = pl.program_id(0)
    e = blk_e_ref[i]
    n_valid = blk_valid_ref[i]
    used = n_valid > 0
    new_expert = (i == 0) | (blk_e_ref[jnp.maximum(i - 1, 0)] != e)

    def weight_copies(expert):
        return (pltpu.make_async_copy(wgu_hbm.at[e0 + expert], wgu_f32, sems.at[0]),
                pltpu.make_async_copy(wd_hbm.at[e0 + expert], wd_f32, sems.at[1]))

    @pl.when(used & (i == 0))
    def _():
        for cp in weight_copies(e):
            cp.start()

    @pl.when(used & new_expert)
    def _():
        for cp in weight_copies(e):
            cp.wait()
        wgu_bf[...] = wgu_f32[...].astype(BF16)
        wd_bf[...] = wd_f32[...].astype(BF16)
        nxt = blk_next_ref[i]

        @pl.when(nxt >= 0)
        def _():
            for cp in weight_copies(nxt):
                cp.start(priority=1)

    half = xs_ref.shape[0] // 2

    def expert_mlp(n_rows):
        valid = lax.broadcasted_iota(I32, (n_rows, xs_ref.shape[1]), 0) < n_valid
        xb = _unpack_halves(jnp.where(valid, xs_ref[0:n_rows, :], jnp.uint32(0))).astype(BF16)
        gu = jnp.dot(xb, wgu_bf[...], preferred_element_type=F32) + bgu_ref[0]
        g = jnp.minimum(gu[:, :D_FF], SWIGLU_LIMIT)
        up = jnp.clip(gu[:, D_FF:], -SWIGLU_LIMIT, SWIGLU_LIMIT)
        act = (up + 1.0) * (g * jax.nn.sigmoid(SWIGLU_ALPHA * g))
        y = jnp.dot(act.astype(BF16), wd_bf[...], preferred_element_type=F32) + bd_ref[0]
        y_ref[0:n_rows, :] = _pack_halves(y)

    @pl.when(n_valid > half)
    def _():
        expert_mlp(2 * half)

    @pl.when(used & (n_valid <= half))
    def _():
        expert_mlp(half)
        y_ref[half:, :] = jnp.zeros((half, y_ref.shape[1]), y_ref.dtype)

    @pl.when(jnp.logical_not(used))
    def _():
        y_ref[...] = jnp.zeros_like(y_ref)


def _moe(xs, blk_e, blk_valid, blk_next, layer, w_gu, b_gu, w_d, b_d):
    n_rows = xs.shape[0]
    bm = MOE_BM
    assert n_rows % bm == 0
    e0 = layer * N_EXPERTS
    n_we = w_gu.shape[0] * w_gu.shape[1]
    any_spec = pl.BlockSpec(memory_space=pl.ANY)
    grid_spec = pltpu.PrefetchScalarGridSpec(
        num_scalar_prefetch=3,
        grid=(n_rows // bm,),
        in_specs=[pl.BlockSpec((bm, D_MODEL // 2), lambda i, be, bv, bn: (i, 0)),
                  any_spec,
                  pl.BlockSpec((1, 1, 2 * D_FF), lambda i, be, bv, bn: (e0 + be[i], 0, 0)),
                  any_spec,
                  pl.BlockSpec((1, 1, D_MODEL), lambda i, be, bv, bn: (e0 + be[i], 0, 0))],
        out_specs=pl.BlockSpec((bm, D_MODEL // 2), lambda i, be, bv, bn: (i, 0)),
        scratch_shapes=[pltpu.VMEM((D_MODEL, 2 * D_FF), F32), pltpu.VMEM((D_FF, D_MODEL), F32),
                        pltpu.VMEM((D_MODEL, 2 * D_FF), BF16), pltpu.VMEM((D_FF, D_MODEL), BF16),
                        pltpu.SemaphoreType.DMA((2,))],
    )
    return pl.pallas_call(
        functools.partial(_moe_kernel, e0=e0),
        grid_spec=grid_spec,
        out_shape=jax.ShapeDtypeStruct((n_rows, D_MODEL // 2), U32),
        compiler_params=pltpu.CompilerParams(dimension_semantics=("arbitrary",), vmem_limit_bytes=VMEM_LIMIT),
        name="moe_experts",
    )(blk_e, blk_valid, blk_next, xs, w_gu.reshape(n_we, D_MODEL, 2 * D_FF), b_gu.reshape(n_we, 1, 2 * D_FF),
      w_d.reshape(n_we, D_FF, D_MODEL), b_d.reshape(n_we, 1, D_MODEL))


def _ple_kernel(x1_ref, y0_ref, y1_ref, y2_ref, y3_ref, gates_ref, p_ref, g_ref, wg_ref, wp_ref, *rest):
    out_ref, wg_bf, wp_bf = rest[-3:]

    @pl.when(pl.program_id(0) == 0)
    def _():
        wg_bf[...] = wg_ref[...].astype(BF16)
        wp_bf[...] = wp_ref[...].astype(BF16)

    x2 = x1_ref[...]
    gates = gates_ref[...]
    for kk, y_ref in enumerate((y0_ref, y1_ref, y2_ref, y3_ref)):
        x2 = x2 + _unpack_halves(y_ref[...]) * gates[:, kk:kk + 1]
    hp = _rms(x2, g_ref[...]).astype(BF16)
    gate = 0.5 * jnp.tanh(0.5 * jnp.dot(hp, wg_bf[...], preferred_element_type=F32)) + 0.5
    pp = jnp.dot(p_ref[...].astype(BF16), wp_bf[...], preferred_element_type=F32)
    out_ref[...] = x2 + gate * pp


def _ple(x1, tok0, n, y_tok, y0, gates_t, p_all, p0, g_ple, w_gate, w_proj, rows, out_into=None):
    assert n % rows == 0 and tok0 % rows == 0 and y0 % rows == 0 and p0 % rows == 0
    const = lambda shape: pl.BlockSpec(shape, lambda i: (0,) * len(shape))
    tok_blk = lambda width: pl.BlockSpec((rows, width), lambda i: (tok0 // rows + i, 0))
    y_blk = lambda kk: pl.BlockSpec((rows, D_MODEL // 2), lambda i: ((y0 + kk * n) // rows + i, 0))
    extra_specs, extra_args, aliases = [], [], {}
    if out_into is not None:
        assert out_into.shape == x1.shape
        extra_specs, extra_args, aliases = [pl.BlockSpec(memory_space=pl.ANY)], [out_into], {10: 0}
    return pl.pallas_call(
        _ple_kernel,
        grid=(n // rows,),
        in_specs=[tok_blk(D_MODEL), y_blk(0), y_blk(1), y_blk(2), y_blk(3), tok_blk(TOP_K),
                  pl.BlockSpec((rows, PLE_DIM), lambda i: (p0 // rows + i, 0)),
                  const((1, D_MODEL)), const((D_MODEL, D_MODEL)), const((PLE_DIM, D_MODEL))] + extra_specs,
        out_specs=tok_blk(D_MODEL),
        out_shape=jax.ShapeDtypeStruct(x1.shape, F32),
        scratch_shapes=[pltpu.VMEM((D_MODEL, D_MODEL), BF16), pltpu.VMEM((PLE_DIM, D_MODEL), BF16)],
        input_output_aliases=aliases,
        compiler_params=pltpu.CompilerParams(dimension_semantics=("arbitrary",), vmem_limit_bytes=VMEM_LIMIT),
        name="combine_ple",
    )(x1, y_tok, y_tok, y_tok, y_tok, gates_t, p_all, g_ple, w_gate, w_proj, *extra_args)


def _rope_tables_const(seq):
    half = ROT_DIM // 2
    d = np.arange(LANES) % HEAD_DIM
    inv = (np.float64(ROPE_THETA) ** (-np.arange(half, dtype=np.float64) / half)).astype(np.float32)
    inv_lane = np.where(d < ROT_DIM, inv[d % half], np.float32(0.0)).astype(np.float32)
    ang = np.arange(seq, dtype=np.float32)[:, None] * inv_lane[None, :]
    cos = np.cos(ang.astype(np.float64)).astype(np.float32)
    sin = np.sin(ang.astype(np.float64)).astype(np.float32)
    sin_a = np.where((d >= half) & (d < ROT_DIM), sin, np.float32(0.0))
    sin_b = np.where(d < half, -sin, np.float32(0.0))
    return tuple(jnp.asarray(a) for a in (cos, sin_a, sin_b))


def _rope_tables(pos):
    half = ROT_DIM // 2
    d = np.arange(LANES) % HEAD_DIM
    inv = ROPE_THETA ** (-jnp.arange(half, dtype=F32) / half)
    inv_lane = jnp.where(d < ROT_DIM, inv[d % half], 0.0)
    ang = pos.astype(F32)[:, None] * inv_lane[None, :]
    cos, sin = jnp.cos(ang), jnp.sin(ang)
    sin_a = jnp.where((d >= half) & (d < ROT_DIM), sin, 0.0)
    sin_b = jnp.where(d < half, -sin, 0.0)
    return cos, sin_a, sin_b


def _layer(layer, xp, xs, ck, cv, st, p_prompt_all, p_sample_all, past_len, refine_tail, tabs_p, tabs_s, finish_state,
           norm_attn, w_in, q_norm, k_norm, attn_sinks, w_pool, pool_scale, w_out,
           norm_ffn, w_router, b_router, w_gate_up_all, b_gate_up_all, w_down_all, b_down_all,
           norm_ple, w_ple_gate, w_ple_proj):
    t, ns = xp.shape[0], xs.shape[0]
    g_attn = norm_attn.reshape(1, D_MODEL)
    qn = jnp.tile(q_norm, 2).reshape(1, LANES)
    kn = jnp.tile(k_norm, 2).reshape(1, LANES)
    lane = np.arange(LANES)
    hmean = jnp.asarray((lane[:, None] // HEAD_DIM == lane[None, :] // HEAD_DIM) / HEAD_DIM, F32)
    pscale = pool_scale.reshape(1, POOL_WIDTH)
    sink8 = jnp.broadcast_to(attn_sinks.reshape(2, Q_TILES).T.reshape(8, 1), (8, LANES))

    mix_args = (g_attn, w_in, qn, kn, hmean, w_pool, pscale, w_out)
    x1p, nk_p, nv_p, nu_p = _mixer_prompt(xp, 0, MIX_ROWS, BF16, attn_sinks, tabs_p, *mix_args)
    if refine_tail:
        x1p, nk_p, nv_p, nu_p = _mixer_prompt(xp, t - 2 * MIX_TAIL_ROWS, MIX_TAIL_ROWS, F32, attn_sinks, tabs_p,
                                              *mix_args, x1_into=x1p)
    st_t = jnp.transpose(st, (1, 0, 2))
    x1s, nk_s, nv_s, nst_t = _mixer_sample(xs, ck.reshape(ns, -1, KV_WIDTH), cv.reshape(ns, -1, KV_WIDTH), st_t,
                                           past_len, tabs_s, sink8, *mix_args)

    g_ffn = norm_ffn.reshape(1, D_MODEL)
    wr_t = w_router.T
    br = b_router.reshape(N_EXPERTS, 1)
    n_tok = t + ns
    bm = MOE_BM
    n_blocks = -(-(n_tok * TOP_K + N_EXPERTS * (bm - 1)) // bm)
    hf_p, idx_p, gate_p, pos_p, cnt_p = _route(x1p, g_ffn, wr_t, br, jnp.zeros((N_EXPERTS, 1), F32),
                                               ROUTE_ROWS, bm, n_blocks)[:5]
    hf_s, idx_s, gate_s, pos_s, _, pad_start, blk_e, blk_valid, blk_next = _route(x1s, g_ffn, wr_t, br, cnt_p, ns,
                                                                                  bm, n_blocks)
    idx = jnp.concatenate([idx_p, idx_s], axis=1)
    pos = jnp.concatenate([pos_p, pos_s], axis=1)
    blk_e, blk_valid, blk_next = (a[0, :n_blocks] for a in (blk_e, blk_valid, blk_next))
    state = finish_state(nk_p, nv_p, nu_p[POOL_HALO - POOL_PREFIX:], nk_s, nv_s, jnp.transpose(nst_t, (1, 0, 2)))
    blk_e, state = lax.optimization_barrier((blk_e, state))
    start_of = jnp.sum(jnp.where(idx[None] == jnp.arange(N_EXPERTS, dtype=I32)[:, None, None],
                                 pad_start.reshape(N_EXPERTS, 1, 1), 0), axis=0)
    dest = start_of + pos

    xs_rows = _scatter_rows(hf_p, hf_s, dest, n_blocks * bm)
    y = _moe(xs_rows, blk_e, blk_valid, blk_next, layer, w_gate_up_all, b_gate_up_all, w_down_all, b_down_all)

    tq = t // COMBINE_PARTS
    unit = SC_WORKERS * SC_CHUNK * 2
    assert t % COMBINE_PARTS == 0 and (TOP_K * tq) % unit == 0 and tq % PLE_ROWS == 0
    g_ple = norm_ple.reshape(1, D_MODEL)
    gates_p = gate_p.T
    yp = None
    for part in range(COMBINE_PARTS):
        back = dest[:, part * tq:(part + 1) * tq].reshape(-1)
        if part == COMBINE_PARTS - 1:
            back = jnp.concatenate([back, dest[:, t:].reshape(-1)])
            back = jnp.concatenate([back, jnp.arange(-back.shape[0] % unit, dtype=I32)])
        y_part = _gather_rows(y, back)
        yp = _ple(x1p, part * tq, tq, y_part, 0, gates_p, p_prompt_all, layer * t + part * tq, g_ple,
                  w_ple_gate, w_ple_proj, PLE_ROWS, out_into=yp)
    ys = _ple(x1s, 0, ns, y_part, TOP_K * tq, gate_s.T, p_sample_all, layer * ns, g_ple, w_ple_gate, w_ple_proj, ns)
    return yp, ys, state


def kernel(x_prompt, x_sample, cache_k, cache_v, state_pool, p_prompt, p_sample, norm_attn, w_in, q_norm, k_norm,
           attn_sinks, w_pool, pool_scale, w_out, norm_ffn, w_router, b_router, w_gate_up, b_gate_up, w_down, b_down,
           norm_ple, w_ple_gate, w_ple_proj):
    depth = norm_attn.shape[0]
    batch, seq, d = x_prompt.shape
    ns, dec_seq, _ = x_sample.shape
    wb = cache_k.shape[2]
    assert batch == 1 and dec_seq == 1 and d == D_MODEL and wb == WINDOW
    assert cache_k.shape[3:] == (N_KV_HEADS, HEAD_DIM) and state_pool.shape[2:] == (POOL_PREFIX, POOL_WIDTH)
    past_len = PAST_LEN
    yp = x_prompt.reshape(seq, d)
    ys = x_sample.reshape(ns, d)
    p_prompt_all = p_prompt.reshape(depth * seq, PLE_DIM)
    p_sample_all = p_sample.reshape(depth * ns, PLE_DIM)
    tabs_p = _rope_tables_const(seq)
    tabs_s = _rope_tables(jnp.full((1,), past_len))
    kv_p, kv_s = (1, WINDOW, N_KV_HEADS, HEAD_DIM), (ns, wb, N_KV_HEADS, HEAD_DIM)
    state_shapes = (kv_p, kv_p, (1, POOL_PREFIX, POOL_WIDTH), kv_s, kv_s, (ns, POOL_PREFIX, POOL_WIDTH))
    per_layer = []

    def shaped(*state):
        return tuple(a.reshape(shp) for a, shp in zip(state, state_shapes))

    def stacked(*state):
        prevs = list(zip(*per_layer)) or [()] * len(state_shapes)
        return tuple(jnp.stack(list(prev) + [cur]) for prev, cur in zip(prevs, shaped(*state)))

    for i in range(depth):
        yp, ys, state = _layer(i, yp, ys, cache_k[i], cache_v[i], state_pool[i], p_prompt_all, p_sample_all, past_len,
                               i < depth - 1, tabs_p, tabs_s, shaped if i < depth - 1 else stacked,
                               norm_attn[i], w_in[i], q_norm[i], k_norm[i], attn_sinks[i], w_pool[i], pool_scale[i],
                               w_out[i], norm_ffn[i], w_router[i], b_router[i], w_gate_up, b_gate_up, w_down, b_down,
                               norm_ple[i], w_ple_gate[i], w_ple_proj[i])
        per_layer.append(state)
    return (yp.reshape(batch, seq, d), ys.reshape(ns, dec_seq, d)) + tuple(state)
```

```python
import functools

import jax
import jax.numpy as jnp
import numpy as np
from jax import lax
from jax.experimental import pallas as pl
from jax.experimental.pallas import tpu as pltpu
from jax.experimental.pallas import tpu_sc as plsc

F32 = jnp.float32
BF16 = jnp.bfloat16
U32 = jnp.uint32
I32 = jnp.int32

D_MODEL = 1024
HEAD_DIM = 64
N_HEADS = 8
N_KV_HEADS = 2
GROUP = N_HEADS // N_KV_HEADS
ATTN_WIDTH = N_HEADS * HEAD_DIM
KV_WIDTH = N_KV_HEADS * HEAD_DIM
POOL_WIDTH = 512
POOL_WINDOWS = (2, 4, 8, 16)
POOL_GC = POOL_WIDTH // len(POOL_WINDOWS)
POOL_PREFIX = max(POOL_WINDOWS) - 1
POOL_HALO = POOL_PREFIX + 1
POOL_PAD = 8
IN_WIDTH = ATTN_WIDTH + 2 * KV_WIDTH + POOL_WIDTH
WINDOW = 128
ROPE_THETA = 500000.0
ROT_DIM = HEAD_DIM // 4
N_EXPERTS = 32
TOP_K = 4
D_FF = 1024
SWIGLU_ALPHA = 1.702
SWIGLU_LIMIT = 7.0
PLE_DIM = 256
PAST_LEN = 16384
EPS = 1e-5
NEG_INF = -1e30

LANES = 128
Q_TILES = ATTN_WIDTH // LANES

MIX_ROWS = 1024
MIX_TAIL_ROWS = 128
SAMPLE_CHUNK = 16
ROUTE_ROWS = 1024
MOE_BM = 512
MOE_BLOCK_PARTS = 4
SC_CORES = 2
SC_SUBCORES = 16
SC_WORKERS = SC_CORES * SC_SUBCORES
SC_CHUNK = 64
PLE_ROWS = 1024
COMBINE_PARTS = 2
VMEM_LIMIT = 56 * 1024 * 1024


def _rms(x, g):
    return x * lax.rsqrt(jnp.mean(x * x, axis=-1, keepdims=True) + EPS) * g


def _pack_halves(x):
    w = x.shape[1] // 2
    lo = lax.bitcast_convert_type(x[:, :w].astype(BF16).astype(F32), U32) >> 16
    hi = lax.bitcast_convert_type(x[:, w:].astype(BF16).astype(F32), U32) & jnp.uint32(0xFFFF0000)
    return lo | hi


def _unpack_halves(packed):
    lo = lax.bitcast_convert_type(packed << 16, F32)
    hi = lax.bitcast_convert_type(packed & jnp.uint32(0xFFFF0000), F32)
    return jnp.concatenate([lo, hi], axis=1)


def _mm(a, b, nt=False):
    dims = (((1,), (1 if nt else 0,)), ((), ()))
    if b.dtype == F32:
        return lax.dot_general(a.astype(F32), b, dims, preferred_element_type=F32, precision=lax.Precision.HIGHEST)
    return lax.dot_general(a.astype(BF16), b, dims, preferred_element_type=F32)


def _head_norm_rope(t, hmean, gain, cos, sin_a, sin_b):
    t = t * lax.rsqrt(_mm(t * t, hmean) + EPS) * gain
    return t * cos + pltpu.roll(t, ROT_DIM // 2, axis=1) * sin_a + pltpu.roll(t, LANES - ROT_DIM // 2, axis=1) * sin_b


def _softmax_pv(s, sink, v, ones):
    m = jnp.maximum(jnp.max(s, axis=-1, keepdims=True), sink)
    e = jnp.exp(s - m).astype(v.dtype)
    den = _mm(e, ones) + jnp.exp(sink - m)
    return _mm(e, v) / den


def _prep_mixer_weights(win_ref, wout_ref, win_c, wout_c):
    cdt = win_c.dtype
    for j in range(Q_TILES):
        for s in range(2):
            src = (j + Q_TILES * s) * HEAD_DIM
            dst = j * LANES + s * HEAD_DIM
            win_c[:, dst:dst + HEAD_DIM] = win_ref[:, src:src + HEAD_DIM].astype(cdt)
            wout_c[dst:dst + HEAD_DIM, :] = wout_ref[src:src + HEAD_DIM, :].astype(cdt)
    win_c[:, ATTN_WIDTH:] = win_ref[:, ATTN_WIDTH:].astype(cdt)
    wout_c[ATTN_WIDTH:, :] = wout_ref[ATTN_WIDTH:, :].astype(cdt)


def _mixer_prompt_kernel(sinks_ref, x_ref, cos_ref, sa_ref, sb_ref, g_ref, win_ref, qn_ref, kn_ref, hm_ref,
                         wpool_ref, pscale_ref, wout_ref, *rest, row_offset, aliased):
    (x1_ref, klast_ref, vlast_ref, ulast_ref, kprev, vprev, uext, mix, win_c, wout_c,
     *lvl) = rest[1:] if aliased else rest
    i = pl.program_id(0)
    rows = x_ref.shape[0]
    n_sub = rows // WINDOW
    cdt = win_c.dtype
    row0 = row_offset + i * rows

    @pl.when(i == 0)
    def _():
        _prep_mixer_weights(win_ref, wout_ref, win_c, wout_c)
        kprev[...] = jnp.zeros_like(kprev)
        vprev[...] = jnp.zeros_like(vprev)
        uext[0:POOL_PAD + POOL_HALO, :] = jnp.zeros((POOL_PAD + POOL_HALO, POOL_WIDTH), F32)
        for buf in lvl:
            buf[0:POOL_PAD, :] = jnp.zeros((POOL_PAD, POOL_GC), F32)

    x = x_ref[...]
    proj = _mm(_rms(x, g_ref[...]), win_c[...])
    cos, sin_a, sin_b = cos_ref[...], sa_ref[...], sb_ref[...]

    n_t = Q_TILES + 1
    t_all = jnp.concatenate([proj[:, j * LANES:(j + 1) * LANES] for j in range(n_t)], axis=0)
    t3 = (t_all * lax.rsqrt(_mm(t_all * t_all, hm_ref[...].astype(cdt)) + EPS)).reshape(n_t, rows, LANES)
    t3 = jnp.concatenate([t3[:Q_TILES] * (qn_ref[...] * HEAD_DIM ** -0.5), t3[Q_TILES:] * kn_ref[...]], axis=0)
    t2 = t3.reshape(n_t * rows, LANES)
    t3 = (t3 * cos + pltpu.roll(t2, ROT_DIM // 2, axis=1).reshape(n_t, rows, LANES) * sin_a
          + pltpu.roll(t2, LANES - ROT_DIM // 2, axis=1).reshape(n_t, rows, LANES) * sin_b)
    q3 = t3[:Q_TILES]
    k = t3[Q_TILES]
    v = proj[:, ATTN_WIDTH + KV_WIDTH:ATTN_WIDTH + 2 * KV_WIDTH]
    u = proj[:, ATTN_WIDTH + 2 * KV_WIDTH:]
    klast_ref[...] = k[rows - WINDOW:, :]
    vlast_ref[...] = v[rows - WINDOW:, :]
    ulast_ref[...] = u[rows - POOL_HALO:, :]
    k_c = k.astype(cdt)
    v_c = jnp.concatenate([v.astype(cdt), jnp.ones((rows, LANES), cdt)], axis=1)
    v_first = jnp.concatenate([vprev[...], jnp.ones((WINDOW, LANES), cdt)], axis=1)

    lane = lax.broadcasted_iota(I32, (WINDOW, LANES), 1)
    left = (lane < HEAD_DIM)[None]
    qi = lax.broadcasted_iota(I32, (WINDOW, 2 * WINDOW), 0)
    kj = lax.broadcasted_iota(I32, (WINDOW, 2 * WINDOW), 1)
    band = (kj - qi >= 1) & (kj - qi <= WINDOW)
    sink3 = jnp.concatenate([jnp.full((1, 1, 1), sinks_ref[j + Q_TILES * s], F32)
                             for j in range(Q_TILES) for s in range(2)], axis=0)
    n_g = 2 * Q_TILES
    fill = jnp.where(kj[0:1][None] == 0, sink3, NEG_INF)
    slot = lax.broadcasted_iota(I32, (2 * WINDOW, 2 * LANES), 0) == 0
    slot_v = slot & (lax.broadcasted_iota(I32, (2 * WINDOW, 2 * LANES), 1) < LANES)

    v_cats, masks, scores = [], [], []
    for c in range(n_sub):
        r0 = c * WINDOW
        if c == 0:
            k_cat = jnp.concatenate([kprev[...], k_c[0:WINDOW]], axis=0)
            v_cat = jnp.concatenate([v_first, v_c[0:WINDOW]], axis=0)
            masks.append(band & (kj + (row0 - WINDOW) >= 0))
        else:
            k_cat = k_c[r0 - WINDOW:r0 + WINDOW]
            v_cat = v_c[r0 - WINDOW:r0 + WINDOW]
            masks.append(band)
        v_cats.append(jnp.where(slot_v, jnp.zeros_like(v_cat), v_cat))
        q_c = q3[:, r0:r0 + WINDOW, :]
        q_all = jnp.concatenate([jnp.where(left, q_c, 0.0), jnp.where(left, 0.0, q_c)], axis=1)
        scores.append(_mm(q_all.reshape(n_g * WINDOW, LANES), k_cat, nt=True))
    kprev[...] = k_c[rows - WINDOW:]
    vprev[...] = v[rows - WINDOW:].astype(cdt)

    probs = []
    for c in range(n_sub):
        s = jnp.where(masks[c][None], scores[c].reshape(n_g, WINDOW, 2 * WINDOW), fill)
        m = jnp.max(s, axis=-1, keepdims=True)
        probs.append(jnp.exp(s - m).astype(cdt).reshape(n_g * WINDOW, 2 * WINDOW))
    applied = [_mm(probs[c], v_cats[c]) for c in range(n_sub)]
    for c in range(n_sub):
        r0 = c * WINDOW
        o = (applied[c][:, :LANES] / applied[c][:, LANES:]).reshape(Q_TILES, 2 * WINDOW, LANES)
        a = jnp.where(left, o[:, :WINDOW], o[:, WINDOW:])
        for j in range(Q_TILES):
            mix[r0:r0 + WINDOW, j * LANES:(j + 1) * LANES] = a[j].astype(cdt)

    base = POOL_PAD + POOL_HALO
    ext = base + rows
    uext[base:ext, :] = u
    pos1 = (lax.broadcasted_iota(I32, (rows, 1), 0) + row0 + 1).astype(F32)
    lvl_of = {1: lvl[0:1], 2: lvl[1:3], 3: lvl[3:5]}
    for gi, w in enumerate(POOL_WINDOWS):
        cols = slice(gi * POOL_GC, (gi + 1) * POOL_GC)
        src, src_cols = uext, cols
        for level in range(1, gi + 2):
            sft = 1 << (level - 1)
            if level <= gi:
                dst = lvl_of[gi][(level - 1) % 2]
                dst[POOL_PAD:ext, :] = src[POOL_PAD:ext, src_cols] + src[POOL_PAD - sft:ext - sft, src_cols]
                src, src_cols = dst, slice(None)
            else:
                wsum = src[base:ext, src_cols] + src[base - sft:ext - sft, src_cols]
        d = wsum / jnp.minimum(pos1, float(w)) - u[:, cols]
        y = _mm(d, wpool_ref[gi].astype(cdt)) * pscale_ref[:, cols]
        mix[:, ATTN_WIDTH + gi * POOL_GC:ATTN_WIDTH + (gi + 1) * POOL_GC] = y.astype(cdt)
    uext[POOL_PAD:base, :] = u[rows - POOL_HALO:, :]

    x1_ref[...] = x + _mm(mix[...], wout_c[...])


def _mixer_prompt(x_full, row_offset, rows, cdt, sinks, tabs, g_attn, w_in, qn, kn, hmean, w_pool, pscale, w_out,
                  x1_into=None):
    t = x_full.shape[0] - row_offset
    assert t % rows == 0 and row_offset % rows == 0 and rows % WINDOW == 0 and rows >= POOL_HALO
    blk0, n_steps = row_offset // rows, t // rows
    const = lambda shape: pl.BlockSpec(shape, lambda i, *_: (0,) * len(shape))
    row_blk = lambda width: pl.BlockSpec((rows, width), lambda i, *_: (blk0 + i, 0))
    aliased = x1_into is not None
    if aliased:
        assert x1_into.shape == x_full.shape
        x1_spec = pl.BlockSpec((rows, D_MODEL), lambda i, *_: (blk0 + n_steps - 1, 0))
        x1_shape = x1_into.shape
        extra_specs, extra_args, aliases = [pl.BlockSpec(memory_space=pl.ANY)], [x1_into], {13: 0}
    else:
        x1_spec = pl.BlockSpec((rows, D_MODEL), lambda i, *_: (i, 0))
        x1_shape = (t, D_MODEL)
        extra_specs, extra_args, aliases = [], [], {}
    grid_spec = pltpu.PrefetchScalarGridSpec(
        num_scalar_prefetch=1,
        grid=(n_steps,),
        in_specs=[row_blk(D_MODEL), row_blk(LANES), row_blk(LANES), row_blk(LANES),
                  const((1, D_MODEL)), const((D_MODEL, IN_WIDTH)), const((1, LANES)), const((1, LANES)),
                  const((LANES, LANES)), const((len(POOL_WINDOWS), POOL_GC, POOL_GC)), const((1, POOL_WIDTH)),
                  const((D_MODEL, D_MODEL))] + extra_specs,
        out_specs=[x1_spec, const((WINDOW, KV_WIDTH)), const((WINDOW, KV_WIDTH)),
                   const((POOL_HALO, POOL_WIDTH))],
        scratch_shapes=[pltpu.VMEM((WINDOW, KV_WIDTH), cdt), pltpu.VMEM((WINDOW, KV_WIDTH), cdt),
                        pltpu.VMEM((POOL_PAD + POOL_HALO + rows, POOL_WIDTH), F32), pltpu.VMEM((rows, D_MODEL), cdt),
                        pltpu.VMEM((D_MODEL, IN_WIDTH), cdt), pltpu.VMEM((D_MODEL, D_MODEL), cdt)]
        + [pltpu.VMEM((POOL_PAD + POOL_HALO + rows, POOL_GC), F32)] * 5,
    )
    return pl.pallas_call(
        functools.partial(_mixer_prompt_kernel, row_offset=row_offset, aliased=aliased),
        grid_spec=grid_spec,
        out_shape=[jax.ShapeDtypeStruct(x1_shape, F32), jax.ShapeDtypeStruct((WINDOW, KV_WIDTH), F32),
                   jax.ShapeDtypeStruct((WINDOW, KV_WIDTH), F32), jax.ShapeDtypeStruct((POOL_HALO, POOL_WIDTH), F32)],
        input_output_aliases=aliases,
        compiler_params=pltpu.CompilerParams(dimension_semantics=("arbitrary",), vmem_limit_bytes=VMEM_LIMIT),
        name="mixer_prompt",
    )(sinks, x_full, *tabs, g_attn, w_in, qn, kn, hmean, w_pool, pscale, w_out, *extra_args)


def _mixer_sample_kernel(x_ref, ck_ref, cv_ref, st_ref, cos_ref, sa_ref, sb_ref, sink8_ref, g_ref, win_ref, qn_ref,
                         kn_ref, hm_ref, wpool_ref, pscale_ref, wout_ref,
                         x1_ref, nk_ref, nv_ref, nst_ref, o8, win_c, wout_c, *, pos):
    nb = x_ref.shape[0]
    wb = ck_ref.shape[1]

    @pl.when(pl.program_id(0) == 0)
    def _():
        _prep_mixer_weights(win_ref, wout_ref, win_c, wout_c)

    x = x_ref[...]
    h = _rms(x, g_ref[...]).astype(BF16)
    proj = jnp.dot(h, win_c[...], preferred_element_type=F32)
    cos, sin_a, sin_b = cos_ref[...], sa_ref[...], sb_ref[...]
    hmean = hm_ref[...].astype(BF16)
    k = _head_norm_rope(proj[:, ATTN_WIDTH:ATTN_WIDTH + KV_WIDTH], hmean, kn_ref[...], cos, sin_a, sin_b)
    v = proj[:, ATTN_WIDTH + KV_WIDTH:ATTN_WIDTH + 2 * KV_WIDTH]
    u = proj[:, ATTN_WIDTH + 2 * KV_WIDTH:]

    nk_ref[:, 0:wb - 1, :] = ck_ref[:, 1:wb, :]
    nv_ref[:, 0:wb - 1, :] = cv_ref[:, 1:wb, :]
    for b in range(nb):
        nk_ref[b, wb - 1:wb, :] = k[b:b + 1, :]
        nv_ref[b, wb - 1:wb, :] = v[b:b + 1, :]

    r8 = lax.broadcasted_iota(I32, (nb * 8, LANES), 0)
    lane8 = lax.broadcasted_iota(I32, (nb * 8, LANES), 1)
    keep = (lane8 < HEAD_DIM) == (r8 % 2 == 0)
    rep = (lax.broadcasted_iota(I32, (nb * 8, nb), 0) // 8 == lax.broadcasted_iota(I32, (nb * 8, nb), 1)).astype(BF16)
    q8 = jnp.zeros((nb * 8, LANES), F32)
    scale = HEAD_DIM ** -0.5
    for j in range(Q_TILES):
        qt = _head_norm_rope(proj[:, j * LANES:(j + 1) * LANES], hmean, qn_ref[...], cos, sin_a, sin_b) * scale
        qrep = jnp.dot(rep, qt.astype(BF16), preferred_element_type=F32)
        q8 = jnp.where(keep & ((r8 % 8) // 2 == j), qrep, q8)
    q8 = q8.astype(BF16)

    sink8 = sink8_ref[:, 0:1]
    ones_bf = jnp.ones((wb, LANES), BF16)
    assert pos >= wb - 1 and wb <= WINDOW
    for b in range(nb):
        kb = nk_ref[b].astype(BF16)
        vb = nv_ref[b].astype(BF16)
        s = lax.dot_general(q8[b * 8:(b + 1) * 8], kb, (((1,), (1,)), ((), ())), preferred_element_type=F32)
        o8[b * 8:(b + 1) * 8, :] = _softmax_pv(s, sink8, vb, ones_bf)
    o8m = jnp.where(keep, o8[...], 0.0).astype(BF16)

    a_tiles = []
    sel_r = lax.broadcasted_iota(I32, (nb, nb * 8), 1)
    sel_b = lax.broadcasted_iota(I32, (nb, nb * 8), 0)
    for j in range(Q_TILES):
        sel = ((sel_r // 8 == sel_b) & ((sel_r % 8) // 2 == j)).astype(BF16)
        a_tiles.append(jnp.dot(sel, o8m, preferred_element_type=F32))

    z_tiles = []
    for gi, w in enumerate(POOL_WINDOWS):
        cols = slice(gi * POOL_GC, (gi + 1) * POOL_GC)
        wsum = u[:, cols]
        for sft in range(1, w):
            wsum = wsum + st_ref[POOL_PREFIX - sft, :, cols]
        d = wsum / float(min(pos + 1, w)) - u[:, cols]
        z_tiles.append(jnp.dot(d.astype(BF16), wpool_ref[gi].astype(BF16), preferred_element_type=F32)
                       * pscale_ref[:, cols])
    nst_ref[0:POOL_PREFIX - 1] = st_ref[1:POOL_PREFIX]
    nst_ref[POOL_PREFIX - 1] = u

    mixv = jnp.concatenate(a_tiles + z_tiles, axis=1).astype(BF16)
    x1_ref[...] = x + jnp.dot(mixv, wout_c[...], preferred_element_type=F32)


def _mixer_sample(x, ck, cv, st, pos, tabs, sink8, g_attn, w_in, qn, kn, hmean, w_pool, pscale, w_out):
    n, wb = ck.shape[0], ck.shape[1]
    nb = SAMPLE_CHUNK
    assert n % nb == 0
    const = lambda shape: pl.BlockSpec(shape, lambda i: (0,) * len(shape))
    cache_blk = pl.BlockSpec((nb, wb, KV_WIDTH), lambda i: (i, 0, 0))
    st_blk = pl.BlockSpec((POOL_PREFIX, nb, POOL_WIDTH), lambda i: (0, i, 0))
    x_blk = pl.BlockSpec((nb, D_MODEL), lambda i: (i, 0))
    return pl.pallas_call(
        functools.partial(_mixer_sample_kernel, pos=pos),
        grid=(n // nb,),
        in_specs=[x_blk, cache_blk, cache_blk, st_blk, const((1, LANES)), const((1, LANES)), const((1, LANES)),
                  const((8, LANES)), const((1, D_MODEL)), const((D_MODEL, IN_WIDTH)), const((1, LANES)),
                  const((1, LANES)), const((LANES, LANES)), const((len(POOL_WINDOWS), POOL_GC, POOL_GC)),
                  const((1, POOL_WIDTH)), const((D_MODEL, D_MODEL))],
        out_specs=[x_blk, cache_blk, cache_blk, st_blk],
        out_shape=[jax.ShapeDtypeStruct((n, D_MODEL), F32), jax.ShapeDtypeStruct(ck.shape, F32),
                   jax.ShapeDtypeStruct(cv.shape, F32), jax.ShapeDtypeStruct(st.shape, F32)],
        scratch_shapes=[pltpu.VMEM((nb * 8, LANES), F32), pltpu.VMEM((D_MODEL, IN_WIDTH), BF16),
                        pltpu.VMEM((D_MODEL, D_MODEL), BF16)],
        compiler_params=pltpu.CompilerParams(dimension_semantics=("arbitrary",), vmem_limit_bytes=VMEM_LIMIT),
        name="mixer_sample",
    )(x, ck, cv, st, *tabs, sink8, g_attn, w_in, qn, kn, hmean, w_pool, pscale, w_out)


def _block_plan(cnt, bm, n_lanes):
    e_sub = lax.broadcasted_iota(I32, (N_EXPERTS, LANES), 0)
    e_lane = lax.broadcasted_iota(I32, (N_EXPERTS, LANES), 1)
    padded = jnp.floor((cnt + (bm - 1)) / bm) * bm
    padded_lane = jnp.sum(jnp.where(e_sub == e_lane, padded, 0.0), axis=0, keepdims=True)
    pad_end = jnp.sum(jnp.where(e_lane <= e_sub, padded_lane, 0.0), axis=1, keepdims=True)
    pad_start = pad_end - padded
    blk_start = lax.broadcasted_iota(I32, (N_EXPERTS, n_lanes), 1).astype(F32) * bm
    blk_e = jnp.minimum(jnp.sum((pad_end <= blk_start).astype(F32), axis=0, keepdims=True), N_EXPERTS - 1.0)
    mine = lax.broadcasted_iota(I32, (N_EXPERTS, n_lanes), 0).astype(F32) == blk_e
    last = jnp.sum(jnp.where(mine, pad_start + cnt, 0.0), axis=0, keepdims=True)
    blk_valid = jnp.clip(last - blk_start[0:1], 0.0, float(bm))
    e_blk = lax.broadcasted_iota(I32, (N_EXPERTS, n_lanes), 0).astype(F32)
    later = jnp.min(jnp.where((e_blk > blk_e) & (cnt > 0.0), e_blk, float(N_EXPERTS)), axis=0, keepdims=True)
    blk_next = jnp.where(later < N_EXPERTS, later, -1.0)
    return pad_start, blk_e, blk_valid, blk_next


def _route_kernel(x1_ref, g_ref, wr_ref, br_ref, cnt_in_ref, hf_ref, idx_ref, gate_ref, pos_ref, cnt_ref,
                  pstart_ref, blke_ref, blkv_ref, blkn_ref, counts, *, bm):
    i = pl.program_id(0)
    rows = x1_ref.shape[0]

    @pl.when(i == 0)
    def _():
        counts[...] = cnt_in_ref[...]

    h = _rms(x1_ref[...], g_ref[...])
    hf_ref[...] = _pack_halves(h)

    logits = lax.dot_general(wr_ref[...], h, (((1,), (1,)), ((), ())), preferred_element_type=F32,
                             precision=lax.Precision.HIGHEST) + br_ref[...]
    eid = lax.broadcasted_iota(I32, (N_EXPERTS, rows), 0).astype(F32)
    work = logits
    vals, hots = [], []
    for kk in range(TOP_K):
        m = jnp.max(work, axis=0, keepdims=True)
        first = jnp.min(jnp.where(work == m, eid, float(N_EXPERTS)), axis=0, keepdims=True)
        hot = eid == first
        work = jnp.where(hot, -jnp.inf, work)
        vals.append(m)
        hots.append(hot)
        idx_ref[kk:kk + 1, :] = first.astype(I32)
    es = [jnp.exp(vv - vals[0]) for vv in vals]
    den = es[0] + es[1] + es[2] + es[3]
    for kk in range(TOP_K):
        gate_ref[kk:kk + 1, :] = es[kk] / den

    chosen = hots[0] | hots[1] | hots[2] | hots[3]
    before = (lax.broadcasted_iota(I32, (rows, rows), 0) < lax.broadcasted_iota(I32, (rows, rows), 1)).astype(BF16)
    rank = jnp.dot(chosen.astype(BF16), before, preferred_element_type=F32) + counts[...]
    for kk in range(TOP_K):
        pos_ref[kk:kk + 1, :] = jnp.sum(jnp.where(hots[kk], rank, 0.0), axis=0, keepdims=True).astype(I32)
    counts[...] = counts[...] + jnp.sum(chosen.astype(F32), axis=1, keepdims=True)
    cnt_ref[...] = counts[...]

    @pl.when(i == pl.num_programs(0) - 1)
    def _():
        pad_start, blk_e, blk_valid, blk_next = _block_plan(counts[...], bm, blke_ref.shape[1])
        pstart_ref[...] = pad_start.astype(I32)
        blke_ref[...] = blk_e.astype(I32)
        blkv_ref[...] = blk_valid.astype(I32)
        blkn_ref[...] = blk_next.astype(I32)


def _route(x1, g_ffn, wr_t, br, cnt_in, rows, bm, n_blocks):
    n = x1.shape[0]
    assert n % rows == 0
    nb_lanes = -(-n_blocks // LANES) * LANES
    const = lambda shape: pl.BlockSpec(shape, lambda i: (0,) * len(shape))
    tok_blk = pl.BlockSpec((TOP_K, rows), lambda i: (0, i))
    return pl.pallas_call(
        functools.partial(_route_kernel, bm=bm),
        grid=(n // rows,),
        in_specs=[pl.BlockSpec((rows, D_MODEL), lambda i: (i, 0)), const((1, D_MODEL)), const((N_EXPERTS, D_MODEL)),
                  const((N_EXPERTS, 1)), const((N_EXPERTS, 1))],
        out_specs=[pl.BlockSpec((rows, D_MODEL // 2), lambda i: (i, 0)), tok_blk, tok_blk, tok_blk,
                   const((N_EXPERTS, 1)), const((N_EXPERTS, 1)), const((1, nb_lanes)), const((1, nb_lanes)),
                   const((1, nb_lanes))],
        out_shape=[jax.ShapeDtypeStruct((n, D_MODEL // 2), U32), jax.ShapeDtypeStruct((TOP_K, n), I32),
                   jax.ShapeDtypeStruct((TOP_K, n), F32), jax.ShapeDtypeStruct((TOP_K, n), I32),
                   jax.ShapeDtypeStruct((N_EXPERTS, 1), F32), jax.ShapeDtypeStruct((N_EXPERTS, 1), I32),
                   jax.ShapeDtypeStruct((1, nb_lanes), I32), jax.ShapeDtypeStruct((1, nb_lanes), I32),
                   jax.ShapeDtypeStruct((1, nb_lanes), I32)],
        scratch_shapes=[pltpu.VMEM((N_EXPERTS, 1), F32)],
        compiler_params=pltpu.CompilerParams(dimension_semantics=("arbitrary",), vmem_limit_bytes=VMEM_LIMIT),
        name="route",
    )(x1, g_ffn, wr_t, br, cnt_in)


def _sc_mesh():
    return plsc.VectorSubcoreMesh(core_axis_name="core", subcore_axis_name="subcore")


def _sc_worker_id():
    return lax.axis_index("core") * SC_SUBCORES + lax.axis_index("subcore")


def _scatter_rows(xa, xb, dest, n_rows):
    ch = SC_CHUNK
    na, w = xa.shape
    n = na + xb.shape[0]
    nk = dest.shape[0]
    assert na % ch == 0 and n % ch == 0 and dest.shape[1] == n and xb.shape[1] == w and xa.dtype == xb.dtype
    n_chunks = n // ch
    dest_c = dest.reshape(nk, n_chunks, ch).transpose(1, 0, 2).reshape(-1)
    dma = pltpu.SemaphoreType.DMA

    @pl.kernel(out_type=jax.ShapeDtypeStruct((n_rows, w), xa.dtype), mesh=_sc_mesh(),
               scratch_types=[pltpu.VMEM((nk * ch,), I32), pltpu.VMEM((ch, w), xa.dtype), dma] + [dma] * nk)
    def scatter_kernel(xa_hbm, xb_hbm, d_hbm, o_hbm, idx_v, buf, sem_i, *sem_s):
        wid = _sc_worker_id()

        @pl.loop(0, -(-n_chunks // SC_WORKERS))
        def _(j):
            c = j * SC_WORKERS + wid

            @pl.when(c < n_chunks)
            def _():
                load_idx = pltpu.make_async_copy(d_hbm.at[pl.ds(c * (nk * ch), nk * ch)], idx_v, sem_i)
                load_idx.start()

                @pl.when(c < na // ch)
                def _():
                    pltpu.sync_copy(xa_hbm.at[pl.ds(c * ch, ch)], buf)

                @pl.when(c >= na // ch)
                def _():
                    pltpu.sync_copy(xb_hbm.at[pl.ds(c * ch - na, ch)], buf)

                load_idx.wait()
                puts = [pltpu.make_async_copy(buf, o_hbm.at[idx_v.at[pl.ds(kk * ch, ch)]], sem_s[kk])
                        for kk in range(nk)]
                for cp in puts:
                    cp.start()
                for cp in puts:
                    cp.wait()

    return scatter_kernel(xa, xb, dest_c)


def _gather_rows(src, idx):
    ch = SC_CHUNK
    m = idx.shape[0]
    w = src.shape[1]
    per = m // SC_WORKERS
    n_pairs = per // (2 * ch)
    assert m % SC_WORKERS == 0 and per % (2 * ch) == 0
    dma = pltpu.SemaphoreType.DMA

    @pl.kernel(out_type=jax.ShapeDtypeStruct((m, w), src.dtype), mesh=_sc_mesh(),
               scratch_types=[pltpu.VMEM((per,), I32), pltpu.VMEM((ch, w), src.dtype), pltpu.VMEM((ch, w), src.dtype),
                              dma, dma, dma, dma])
    def gather_kernel(s_hbm, i_hbm, o_hbm, idx_v, buf_a, buf_b, sem_ga, sem_gb, sem_wa, sem_wb):
        base = _sc_worker_id() * per
        pltpu.sync_copy(i_hbm.at[pl.ds(base, per)], idx_v)

        def fetch(j, buf, sem):
            return pltpu.make_async_copy(s_hbm.at[idx_v.at[pl.ds(j * ch, ch)]], buf, sem)

        def put(j, buf, sem):
            return pltpu.make_async_copy(buf, o_hbm.at[pl.ds(base + j * ch, ch)], sem)

        fetch(0, buf_a, sem_ga).start()

        @pl.loop(0, n_pairs)
        def _(p):
            j0 = 2 * p
            j1 = j0 + 1

            @pl.when(p > 0)
            def _():
                put(j1 - 2, buf_b, sem_wb).wait()

            fetch(j1, buf_b, sem_gb).start()
            fetch(j0, buf_a, sem_ga).wait()
            put(j0, buf_a, sem_wa).start()
            fetch(j1, buf_b, sem_gb).wait()
            put(j1, buf_b, sem_wb).start()
            put(j0, buf_a, sem_wa).wait()

            @pl.when(p + 1 < n_pairs)
            def _():
                fetch(j0 + 2, buf_a, sem_ga).start()

        put(2 * n_pairs - 1, buf_b, sem_wb).wait()

    return gather_kernel(src, idx)


def _moe_kernel(blk_e_ref, blk_valid_ref, blk_next_ref, xs_ref, wgu_hbm, bgu_ref, wd_hbm, bd_ref, y_ref,
                wgu_f32, wd_f32, wgu_bf, wd_bf, sems, *, e0):
    i = pl.program_id(0)
    e = blk_e_ref[i]
    n_valid = blk_valid_ref[i]
    used = n_valid > 0
    new_expert = (i == 0) | (blk_e_ref[jnp.maximum(i - 1, 0)] != e)

    def weight_copies(expert):
        return (pltpu.make_async_copy(wgu_hbm.at[e0 + expert], wgu_f32, sems.at[0]),
                pltpu.make_async_copy(wd_hbm.at[e0 + expert], wd_f32, sems.at[1]))

    @pl.when(used & (i == 0))
    def _():
        for cp in weight_copies(e):
            cp.start()

    @pl.when(used & new_expert)
    def _():
        for cp in weight_copies(e):
            cp.wait()
        wgu_bf[...] = wgu_f32[...].astype(BF16)
        wd_bf[...] = wd_f32[...].astype(BF16)
        nxt = blk_next_ref[i]

        @pl.when(nxt >= 0)
        def _():
            for cp in weight_copies(nxt):
                cp.start(priority=1)

    def expert_mlp(n_rows):
        valid = lax.broadcasted_iota(I32, (n_rows, xs_ref.shape[1]), 0) < n_valid
        xb = _unpack_halves(jnp.where(valid, xs_ref[0:n_rows, :], jnp.uint32(0))).astype(BF16)
        gu = jnp.dot(xb, wgu_bf[...], preferred_element_type=F32) + bgu_ref[0]
        g = jnp.minimum(gu[:, :D_FF], SWIGLU_LIMIT)
        up = jnp.clip(gu[:, D_FF:], -SWIGLU_LIMIT, SWIGLU_LIMIT)
        act = (up + 1.0) * (g * jax.nn.sigmoid(SWIGLU_ALPHA * g))
        y = jnp.dot(act.astype(BF16), wd_bf[...], preferred_element_type=F32) + bd_ref[0]
        y_ref[0:n_rows, :] = _pack_halves(y)

    quarter = xs_ref.shape[0] // MOE_BLOCK_PARTS
    for part in range(1, MOE_BLOCK_PARTS + 1):
        @pl.when((n_valid > (part - 1) * quarter) & (n_valid <= part * quarter))
        def _(n_rows=part * quarter):
            expert_mlp(n_rows)
            if n_rows < xs_ref.shape[0]:
                y_ref[n_rows:, :] = jnp.zeros((xs_ref.shape[0] - n_rows, y_ref.shape[1]), y_ref.dtype)

    @pl.when(jnp.logical_not(used))
    def _():
        y_ref[...] = jnp.zeros_like(y_ref)


def _moe(xs, blk_e, blk_valid, blk_next, layer, w_gu, b_gu, w_d, b_d):
    n_rows = xs.shape[0]
    bm = MOE_BM
    assert n_rows % bm == 0
    e0 = layer * N_EXPERTS
    n_we = w_gu.shape[0] * w_gu.shape[1]
    any_spec = pl.BlockSpec(memory_space=pl.ANY)
    grid_spec = pltpu.PrefetchScalarGridSpec(
        num_scalar_prefetch=3,
        grid=(n_rows // bm,),
        in_specs=[pl.BlockSpec((bm, D_MODEL // 2), lambda i, be, bv, bn: (i, 0)),
                  any_spec,
                  pl.BlockSpec((1, 1, 2 * D_FF), lambda i, be, bv, bn: (e0 + be[i], 0, 0)),
                  any_spec,
                  pl.BlockSpec((1, 1, D_MODEL), lambda i, be, bv, bn: (e0 + be[i], 0, 0))],
        out_specs=pl.BlockSpec((bm, D_MODEL // 2), lambda i, be, bv, bn: (i, 0)),
        scratch_shapes=[pltpu.VMEM((D_MODEL, 2 * D_FF), F32), pltpu.VMEM((D_FF, D_MODEL), F32),
                        pltpu.VMEM((D_MODEL, 2 * D_FF), BF16), pltpu.VMEM((D_FF, D_MODEL), BF16),
                        pltpu.SemaphoreType.DMA((2,))],
    )
    return pl.pallas_call(
        functools.partial(_moe_kernel, e0=e0),
        grid_spec=grid_spec,
        out_shape=jax.ShapeDtypeStruct((n_rows, D_MODEL // 2), U32),
        compiler_params=pltpu.CompilerParams(dimension_semantics=("arbitrary",), vmem_limit_bytes=VMEM_LIMIT),
        name="moe_experts",
    )(blk_e, blk_valid, blk_next, xs, w_gu.reshape(n_we, D_MODEL, 2 * D_FF), b_gu.reshape(n_we, 1, 2 * D_FF),
      w_d.reshape(n_we, D_FF, D_MODEL), b_d.reshape(n_we, 1, D_MODEL))


def _ple_kernel(x1_ref, y0_ref, y1_ref, y2_ref, y3_ref, gates_ref, p_ref, g_ref, wg_ref, wp_ref, *rest):
    out_ref, wg_bf, wp_bf = rest[-3:]

    @pl.when(pl.program_id(0) == 0)
    def _():
        wg_bf[...] = wg_ref[...].astype(BF16)
        wp_bf[...] = wp_ref[...].astype(BF16)

    x2 = x1_ref[...]
    gates = gates_ref[...]
    for kk, y_ref in enumerate((y0_ref, y1_ref, y2_ref, y3_ref)):
        x2 = x2 + _unpack_halves(y_ref[...]) * gates[:, kk:kk + 1]
    hp = _rms(x2, g_ref[...]).astype(BF16)
    gate = 0.5 * jnp.tanh(0.5 * jnp.dot(hp, wg_bf[...], preferred_element_type=F32)) + 0.5
    pp = jnp.dot(p_ref[...].astype(BF16), wp_bf[...], preferred_element_type=F32)
    out_ref[...] = x2 + gate * pp


def _ple(x1, tok0, n, y_tok, y0, gates_t, p_all, p0, g_ple, w_gate, w_proj, rows, out_into=None):
    assert n % rows == 0 and tok0 % rows == 0 and y0 % rows == 0 and p0 % rows == 0
    const = lambda shape: pl.BlockSpec(shape, lambda i: (0,) * len(shape))
    tok_blk = lambda width: pl.BlockSpec((rows, width), lambda i: (tok0 // rows + i, 0))
    y_blk = lambda kk: pl.BlockSpec((rows, D_MODEL // 2), lambda i: ((y0 + kk * n) // rows + i, 0))
    extra_specs, extra_args, aliases = [], [], {}
    if out_into is not None:
        assert out_into.shape == x1.shape
        extra_specs, extra_args, aliases = [pl.BlockSpec(memory_space=pl.ANY)], [out_into], {10: 0}
    return pl.pallas_call(
        _ple_kernel,
        grid=(n // rows,),
        in_specs=[tok_blk(D_MODEL), y_blk(0), y_blk(1), y_blk(2), y_blk(3), tok_blk(TOP_K),
                  pl.BlockSpec((rows, PLE_DIM), lambda i: (p0 // rows + i, 0)),
                  const((1, D_MODEL)), const((D_MODEL, D_MODEL)), const((PLE_DIM, D_MODEL))] + extra_specs,
        out_specs=tok_blk(D_MODEL),
        out_shape=jax.ShapeDtypeStruct(x1.shape, F32),
        scratch_shapes=[pltpu.VMEM((D_MODEL, D_MODEL), BF16), pltpu.VMEM((PLE_DIM, D_MODEL), BF16)],
        input_output_aliases=aliases,
        compiler_params=pltpu.CompilerParams(dimension_semantics=("arbitrary",), vmem_limit_bytes=VMEM_LIMIT),
        name="combine_ple",
    )(x1, y_tok, y_tok, y_tok, y_tok, gates_t, p_all, g_ple, w_gate, w_proj, *extra_args)


def _rope_tables_const(seq):
    half = ROT_DIM // 2
    d = np.arange(LANES) % HEAD_DIM
    inv = (np.float64(ROPE_THETA) ** (-np.arange(half, dtype=np.float64) / half)).astype(np.float32)
    inv_lane = np.where(d < ROT_DIM, inv[d % half], np.float32(0.0)).astype(np.float32)
    ang = np.arange(seq, dtype=np.float32)[:, None] * inv_lane[None, :]
    cos = np.cos(ang.astype(np.float64)).astype(np.float32)
    sin = np.sin(ang.astype(np.float64)).astype(np.float32)
    sin_a = np.where((d >= half) & (d < ROT_DIM), sin, np.float32(0.0))
    sin_b = np.where(d < half, -sin, np.float32(0.0))
    return tuple(jnp.asarray(a) for a in (cos, sin_a, sin_b))


def _rope_tables(pos):
    half = ROT_DIM // 2
    d = np.arange(LANES) % HEAD_DIM
    inv = ROPE_THETA ** (-jnp.arange(half, dtype=F32) / half)
    inv_lane = jnp.where(d < ROT_DIM, inv[d % half], 0.0)
    ang = pos.astype(F32)[:, None] * inv_lane[None, :]
    cos, sin = jnp.cos(ang), jnp.sin(ang)
    sin_a = jnp.where((d >= half) & (d < ROT_DIM), sin, 0.0)
    sin_b = jnp.where(d < half, -sin, 0.0)
    return cos, sin_a, sin_b


def _layer(layer, xp, xs, ck, cv, st, p_prompt_all, p_sample_all, past_len, refine_tail, tabs_p, tabs_s, finish_state,
           norm_attn, w_in, q_norm, k_norm, attn_sinks, w_pool, pool_scale, w_out,
           norm_ffn, w_router, b_router, w_gate_up_all, b_gate_up_all, w_down_all, b_down_all,
           norm_ple, w_ple_gate, w_ple_proj):
    t, ns = xp.shape[0], xs.shape[0]
    g_attn = norm_attn.reshape(1, D_MODEL)
    qn = jnp.tile(q_norm, 2).reshape(1, LANES)
    kn = jnp.tile(k_norm, 2).reshape(1, LANES)
    lane = np.arange(LANES)
    hmean = jnp.asarray((lane[:, None] // HEAD_DIM == lane[None, :] // HEAD_DIM) / HEAD_DIM, F32)
    pscale = pool_scale.reshape(1, POOL_WIDTH)
    sink8 = jnp.broadcast_to(attn_sinks.reshape(2, Q_TILES).T.reshape(8, 1), (8, LANES))

    mix_args = (g_attn, w_in, qn, kn, hmean, w_pool, pscale, w_out)
    x1p, nk_p, nv_p, nu_p = _mixer_prompt(xp, 0, MIX_ROWS, BF16, attn_sinks, tabs_p, *mix_args)
    if refine_tail:
        x1p, nk_p, nv_p, nu_p = _mixer_prompt(xp, t - 2 * MIX_TAIL_ROWS, MIX_TAIL_ROWS, F32, attn_sinks, tabs_p,
                                              *mix_args, x1_into=x1p)
    st_t = jnp.transpose(st, (1, 0, 2))
    x1s, nk_s, nv_s, nst_t = _mixer_sample(xs, ck.reshape(ns, -1, KV_WIDTH), cv.reshape(ns, -1, KV_WIDTH), st_t,
                                           past_len, tabs_s, sink8, *mix_args)

    g_ffn = norm_ffn.reshape(1, D_MODEL)
    wr_t = w_router.T
    br = b_router.reshape(N_EXPERTS, 1)
    n_tok = t + ns
    bm = MOE_BM
    n_blocks = -(-(n_tok * TOP_K + N_EXPERTS * (bm - 1)) // bm)
    hf_p, idx_p, gate_p, pos_p, cnt_p = _route(x1p, g_ffn, wr_t, br, jnp.zeros((N_EXPERTS, 1), F32),
                                               ROUTE_ROWS, bm, n_blocks)[:5]
    hf_s, idx_s, gate_s, pos_s, _, pad_start, blk_e, blk_valid, blk_next = _route(x1s, g_ffn, wr_t, br, cnt_p, ns,
                                                                                  bm, n_blocks)
    idx = jnp.concatenate([idx_p, idx_s], axis=1)
    pos = jnp.concatenate([pos_p, pos_s], axis=1)
    blk_e, blk_valid, blk_next = (a[0, :n_blocks] for a in (blk_e, blk_valid, blk_next))
    state = finish_state(nk_p, nv_p, nu_p[POOL_HALO - POOL_PREFIX:], nk_s, nv_s, jnp.transpose(nst_t, (1, 0, 2)))
    blk_e, state = lax.optimization_barrier((blk_e, state))
    start_of = jnp.sum(jnp.where(idx[None] == jnp.arange(N_EXPERTS, dtype=I32)[:, None, None],
                                 pad_start.reshape(N_EXPERTS, 1, 1), 0), axis=0)
    dest = start_of + pos

    xs_rows = _scatter_rows(hf_p, hf_s, dest, n_blocks * bm)
    y = _moe(xs_rows, blk_e, blk_valid, blk_next, layer, w_gate_up_all, b_gate_up_all, w_down_all, b_down_all)

    tq = t // COMBINE_PARTS
    unit = SC_WORKERS * SC_CHUNK * 2
    assert t % COMBINE_PARTS == 0 and (TOP_K * tq) % unit == 0 and tq % PLE_ROWS == 0
    g_ple = norm_ple.reshape(1, D_MODEL)
    gates_p = gate_p.T
    yp = None
    for part in range(COMBINE_PARTS):
        back = dest[:, part * tq:(part + 1) * tq].reshape(-1)
        if part == COMBINE_PARTS - 1:
            back = jnp.concatenate([back, dest[:, t:].reshape(-1)])
            back = jnp.concatenate([back, jnp.arange(-back.shape[0] % unit, dtype=I32)])
        y_part = _gather_rows(y, back)
        yp = _ple(x1p, part * tq, tq, y_part, 0, gates_p, p_prompt_all, layer * t + part * tq, g_ple,
                  w_ple_gate, w_ple_proj, PLE_ROWS, out_into=yp)
    ys = _ple(x1s, 0, ns, y_part, TOP_K * tq, gate_s.T, p_sample_all, layer * ns, g_ple, w_ple_gate, w_ple_proj, ns)
    return yp, ys, state


def kernel(x_prompt, x_sample, cache_k, cache_v, state_pool, p_prompt, p_sample, norm_attn, w_in, q_norm, k_norm,
           attn_sinks, w_pool, pool_scale, w_out, norm_ffn, w_router, b_router, w_gate_up, b_gate_up, w_down, b_down,
           norm_ple, w_ple_gate, w_ple_proj):
    depth = norm_attn.shape[0]
    batch, seq, d = x_prompt.shape
    ns, dec_seq, _ = x_sample.shape
    wb = cache_k.shape[2]
    assert batch == 1 and dec_seq == 1 and d == D_MODEL and wb == WINDOW
    assert cache_k.shape[3:] == (N_KV_HEADS, HEAD_DIM) and state_pool.shape[2:] == (POOL_PREFIX, POOL_WIDTH)
    past_len = PAST_LEN
    yp = x_prompt.reshape(seq, d)
    ys = x_sample.reshape(ns, d)
    p_prompt_all = p_prompt.reshape(depth * seq, PLE_DIM)
    p_sample_all = p_sample.reshape(depth * ns, PLE_DIM)
    tabs_p = _rope_tables_const(seq)
    tabs_s = _rope_tables(jnp.full((1,), past_len))
    kv_p, kv_s = (1, WINDOW, N_KV_HEADS, HEAD_DIM), (ns, wb, N_KV_HEADS, HEAD_DIM)
    state_shapes = (kv_p, kv_p, (1, POOL_PREFIX, POOL_WIDTH), kv_s, kv_s, (ns, POOL_PREFIX, POOL_WIDTH))
    per_layer = []

    def shaped(*state):
        return tuple(a.reshape(shp) for a, shp in zip(state, state_shapes))

    def stacked(*state):
        prevs = list(zip(*per_layer)) or [()] * len(state_shapes)
        return tuple(jnp.stack(list(prev) + [cur]) for prev, cur in zip(prevs, shaped(*state)))

    for i in range(depth):
        yp, ys, state = _layer(i, yp, ys, cache_k[i], cache_v[i], state_pool[i], p_prompt_all, p_sample_all, past_len,
                               i < depth - 1, tabs_p, tabs_s, shaped if i < depth - 1 else stacked,
                               norm_attn[i], w_in[i], q_norm[i], k_norm[i], attn_sinks[i], w_pool[i], pool_scale[i],
                               w_out[i], norm_ffn[i], w_router[i], b_router[i], w_gate_up, b_gate_up, w_down, b_down,
                               norm_ple[i], w_ple_gate[i], w_ple_proj[i])
        per_layer.append(state)
    return (yp.reshape(batch, seq, d), ys.reshape(ns, dec_seq, d)) + tuple(state)
```

```python
import functools

import jax
import jax.numpy as jnp
import numpy as np
from jax import lax
from jax.experimental import pallas as pl
from jax.experimental.pallas import tpu as pltpu
from jax.experimental.pallas import tpu_sc as plsc

F32 = jnp.float32
BF16 = jnp.bfloat16
U32 = jnp.uint32
I32 = jnp.int32

D_MODEL = 1024
HEAD_DIM = 64
N_HEADS = 8
N_KV_HEADS = 2
GROUP = N_HEADS // N_KV_HEADS
ATTN_WIDTH = N_HEADS * HEAD_DIM
KV_WIDTH = N_KV_HEADS * HEAD_DIM
POOL_WIDTH = 512
POOL_WINDOWS = (2, 4, 8, 16)
POOL_GC = POOL_WIDTH // len(POOL_WINDOWS)
POOL_PREFIX = max(POOL_WINDOWS) - 1
POOL_HALO = POOL_PREFIX + 1
POOL_PAD = 8
IN_WIDTH = ATTN_WIDTH + 2 * KV_WIDTH + POOL_WIDTH
WINDOW = 128
ROPE_THETA = 500000.0
ROT_DIM = HEAD_DIM // 4
N_EXPERTS = 32
TOP_K = 4
D_FF = 1024
SWIGLU_ALPHA = 1.702
SWIGLU_LIMIT = 7.0
PLE_DIM = 256
PAST_LEN = 16384
EPS = 1e-5
NEG_INF = -1e30

LANES = 128
Q_TILES = ATTN_WIDTH // LANES

MIX_ROWS = 1024
MIX_TAIL_ROWS = 128
SAMPLE_CHUNK = 16
ROUTE_ROWS = 1024
MOE_BM = 512
SC_CORES = 2
SC_SUBCORES = 16
SC_WORKERS = SC_CORES * SC_SUBCORES
SC_CHUNK = 64
PLE_ROWS = 1024
COMBINE_PARTS = 2
VMEM_LIMIT = 56 * 1024 * 1024


def _rms(x, g):
    return x * lax.rsqrt(jnp.mean(x * x, axis=-1, keepdims=True) + EPS) * g


def _pack_halves(x):
    w = x.shape[1] // 2
    lo = lax.bitcast_convert_type(x[:, :w].astype(BF16).astype(F32), U32) >> 16
    hi = lax.bitcast_convert_type(x[:, w:].astype(BF16).astype(F32), U32) & jnp.uint32(0xFFFF0000)
    return lo | hi


def _unpack_halves(packed):
    lo = lax.bitcast_convert_type(packed << 16, F32)
    hi = lax.bitcast_convert_type(packed & jnp.uint32(0xFFFF0000), F32)
    return jnp.concatenate([lo, hi], axis=1)


def _mm(a, b, nt=False):
    dims = (((1,), (1 if nt else 0,)), ((), ()))
    if b.dtype == F32:
        return lax.dot_general(a.astype(F32), b, dims, preferred_element_type=F32, precision=lax.Precision.HIGHEST)
    return lax.dot_general(a.astype(BF16), b, dims, preferred_element_type=F32)


def _head_norm_rope(t, hmean, gain, cos, sin_a, sin_b):
    t = t * lax.rsqrt(_mm(t * t, hmean) + EPS) * gain
    return t * cos + pltpu.roll(t, ROT_DIM // 2, axis=1) * sin_a + pltpu.roll(t, LANES - ROT_DIM // 2, axis=1) * sin_b


def _softmax_pv(s, sink, v, ones):
    m = jnp.maximum(jnp.max(s, axis=-1, keepdims=True), sink)
    e = jnp.exp(s - m).astype(v.dtype)
    den = _mm(e, ones) + jnp.exp(sink - m)
    return _mm(e, v) / den


def _prep_mixer_weights(win_ref, wout_ref, win_c, wout_c):
    cdt = win_c.dtype
    for j in range(Q_TILES):
        for s in range(2):
            src = (j + Q_TILES * s) * HEAD_DIM
            dst = j * LANES + s * HEAD_DIM
            win_c[:, dst:dst + HEAD_DIM] = win_ref[:, src:src + HEAD_DIM].astype(cdt)
            wout_c[dst:dst + HEAD_DIM, :] = wout_ref[src:src + HEAD_DIM, :].astype(cdt)
    win_c[:, ATTN_WIDTH:] = win_ref[:, ATTN_WIDTH:].astype(cdt)
    wout_c[ATTN_WIDTH:, :] = wout_ref[ATTN_WIDTH:, :].astype(cdt)


def _mixer_prompt_kernel(sinks_ref, x_ref, cos_ref, sa_ref, sb_ref, g_ref, win_ref, qn_ref, kn_ref, hm_ref,
                         wpool_ref, pscale_ref, wout_ref, *rest, row_offset, aliased):
    (x1_ref, klast_ref, vlast_ref, ulast_ref, kprev, vprev, uext, mix, win_c, wout_c,
     *lvl) = rest[1:] if aliased else rest
    i = pl.program_id(0)
    rows = x_ref.shape[0]
    n_sub = rows // WINDOW
    cdt = win_c.dtype
    row0 = row_offset + i * rows

    @pl.when(i == 0)
    def _():
        _prep_mixer_weights(win_ref, wout_ref, win_c, wout_c)
        kprev[...] = jnp.zeros_like(kprev)
        vprev[...] = jnp.zeros_like(vprev)
        uext[0:POOL_PAD + POOL_HALO, :] = jnp.zeros((POOL_PAD + POOL_HALO, POOL_WIDTH), F32)
        for buf in lvl:
            buf[0:POOL_PAD, :] = jnp.zeros((POOL_PAD, POOL_GC), F32)

    x = x_ref[...]
    proj = _mm(_rms(x, g_ref[...]), win_c[...])
    cos, sin_a, sin_b = cos_ref[...], sa_ref[...], sb_ref[...]

    n_t = Q_TILES + 1
    t_all = jnp.concatenate([proj[:, j * LANES:(j + 1) * LANES] for j in range(n_t)], axis=0)
    t3 = (t_all * lax.rsqrt(_mm(t_all * t_all, hm_ref[...].astype(cdt)) + EPS)).reshape(n_t, rows, LANES)
    t3 = jnp.concatenate([t3[:Q_TILES] * (qn_ref[...] * HEAD_DIM ** -0.5), t3[Q_TILES:] * kn_ref[...]], axis=0)
    t2 = t3.reshape(n_t * rows, LANES)
    t3 = (t3 * cos + pltpu.roll(t2, ROT_DIM // 2, axis=1).reshape(n_t, rows, LANES) * sin_a
          + pltpu.roll(t2, LANES - ROT_DIM // 2, axis=1).reshape(n_t, rows, LANES) * sin_b)
    q3 = t3[:Q_TILES]
    k = t3[Q_TILES]
    v = proj[:, ATTN_WIDTH + KV_WIDTH:ATTN_WIDTH + 2 * KV_WIDTH]
    u = proj[:, ATTN_WIDTH + 2 * KV_WIDTH:]
    klast_ref[...] = k[rows - WINDOW:, :]
    vlast_ref[...] = v[rows - WINDOW:, :]
    ulast_ref[...] = u[rows - POOL_HALO:, :]
    k_c = k.astype(cdt)
    v_c = jnp.concatenate([v.astype(cdt), jnp.ones((rows, LANES), cdt)], axis=1)
    v_first = jnp.concatenate([vprev[...], jnp.ones((WINDOW, LANES), cdt)], axis=1)

    lane = lax.broadcasted_iota(I32, (WINDOW, LANES), 1)
    left = (lane < HEAD_DIM)[None]
    qi = lax.broadcasted_iota(I32, (WINDOW, 2 * WINDOW), 0)
    kj = lax.broadcasted_iota(I32, (WINDOW, 2 * WINDOW), 1)
    band = (kj - qi >= 1) & (kj - qi <= WINDOW)
    sink3 = jnp.concatenate([jnp.full((1, 1, 1), sinks_ref[j + Q_TILES * s], F32)
                             for j in range(Q_TILES) for s in range(2)], axis=0)
    n_g = 2 * Q_TILES
    fill = jnp.where(kj[0:1][None] == 0, sink3, NEG_INF)
    slot = lax.broadcasted_iota(I32, (2 * WINDOW, 2 * LANES), 0) == 0
    slot_v = slot & (lax.broadcasted_iota(I32, (2 * WINDOW, 2 * LANES), 1) < LANES)

    v_cats, masks, scores = [], [], []
    for c in range(n_sub):
        r0 = c * WINDOW
        if c == 0:
            k_cat = jnp.concatenate([kprev[...], k_c[0:WINDOW]], axis=0)
            v_cat = jnp.concatenate([v_first, v_c[0:WINDOW]], axis=0)
            masks.append(band & (kj + (row0 - WINDOW) >= 0))
        else:
            k_cat = k_c[r0 - WINDOW:r0 + WINDOW]
            v_cat = v_c[r0 - WINDOW:r0 + WINDOW]
            masks.append(band)
        v_cats.append(jnp.where(slot_v, jnp.zeros_like(v_cat), v_cat))
        q_c = q3[:, r0:r0 + WINDOW, :]
        q_all = jnp.concatenate([jnp.where(left, q_c, 0.0), jnp.where(left, 0.0, q_c)], axis=1)
        scores.append(_mm(q_all.reshape(n_g * WINDOW, LANES), k_cat, nt=True))
    kprev[...] = k_c[rows - WINDOW:]
    vprev[...] = v[rows - WINDOW:].astype(cdt)

    probs = []
    for c in range(n_sub):
        s = jnp.where(masks[c][None], scores[c].reshape(n_g, WINDOW, 2 * WINDOW), fill)
        m = jnp.max(s, axis=-1, keepdims=True)
        probs.append(jnp.exp(s - m).astype(cdt).reshape(n_g * WINDOW, 2 * WINDOW))
    applied = [_mm(probs[c], v_cats[c]) for c in range(n_sub)]
    for c in range(n_sub):
        r0 = c * WINDOW
        o = (applied[c][:, :LANES] / applied[c][:, LANES:]).reshape(Q_TILES, 2 * WINDOW, LANES)
        a = jnp.where(left, o[:, :WINDOW], o[:, WINDOW:])
        for j in range(Q_TILES):
            mix[r0:r0 + WINDOW, j * LANES:(j + 1) * LANES] = a[j].astype(cdt)

    base = POOL_PAD + POOL_HALO
    ext = base + rows
    uext[base:ext, :] = u
    pos1 = (lax.broadcasted_iota(I32, (rows, 1), 0) + row0 + 1).astype(F32)
    lvl_of = {1: lvl[0:1], 2: lvl[1:3], 3: lvl[3:5]}
    for gi, w in enumerate(POOL_WINDOWS):
        cols = slice(gi * POOL_GC, (gi + 1) * POOL_GC)
        src, src_cols = uext, cols
        for level in range(1, gi + 2):
            sft = 1 << (level - 1)
            if level <= gi:
                dst = lvl_of[gi][(level - 1) % 2]
                dst[POOL_PAD:ext, :] = src[POOL_PAD:ext, src_cols] + src[POOL_PAD - sft:ext - sft, src_cols]
                src, src_cols = dst, slice(None)
            else:
                wsum = src[base:ext, src_cols] + src[base - sft:ext - sft, src_cols]
        d = wsum / jnp.minimum(pos1, float(w)) - u[:, cols]
        y = _mm(d, wpool_ref[gi].astype(cdt)) * pscale_ref[:, cols]
        mix[:, ATTN_WIDTH + gi * POOL_GC:ATTN_WIDTH + (gi + 1) * POOL_GC] = y.astype(cdt)
    uext[POOL_PAD:base, :] = u[rows - POOL_HALO:, :]

    x1_ref[...] = x + _mm(mix[...], wout_c[...])


def _mixer_prompt(x_full, row_offset, rows, cdt, sinks, tabs, g_attn, w_in, qn, kn, hmean, w_pool, pscale, w_out,
                  x1_into=None):
    t = x_full.shape[0] - row_offset
    assert t % rows == 0 and row_offset % rows == 0 and rows % WINDOW == 0 and rows >= POOL_HALO
    blk0, n_steps = row_offset // rows, t // rows
    const = lambda shape: pl.BlockSpec(shape, lambda i, *_: (0,) * len(shape))
    row_blk = lambda width: pl.BlockSpec((rows, width), lambda i, *_: (blk0 + i, 0))
    aliased = x1_into is not None
    operands = (sinks, x_full, *tabs, g_attn, w_in, qn, kn, hmean, w_pool, pscale, w_out)
    if aliased:
        assert x1_into.shape == x_full.shape
        x1_spec = pl.BlockSpec((rows, D_MODEL), lambda i, *_: (blk0 + n_steps - 1, 0))
        x1_shape = x1_into.shape
        extra_specs, extra_args, aliases = [pl.BlockSpec(memory_space=pl.ANY)], [x1_into], {len(operands): 0}
    else:
        x1_spec = pl.BlockSpec((rows, D_MODEL), lambda i, *_: (i, 0))
        x1_shape = (t, D_MODEL)
        extra_specs, extra_args, aliases = [], [], {}
    grid_spec = pltpu.PrefetchScalarGridSpec(
        num_scalar_prefetch=1,
        grid=(n_steps,),
        in_specs=[row_blk(D_MODEL), row_blk(LANES), row_blk(LANES), row_blk(LANES),
                  const((1, D_MODEL)), const((D_MODEL, IN_WIDTH)), const((1, LANES)), const((1, LANES)),
                  const((LANES, LANES)), const((len(POOL_WINDOWS), POOL_GC, POOL_GC)), const((1, POOL_WIDTH)),
                  const((D_MODEL, D_MODEL))] + extra_specs,
        out_specs=[x1_spec, const((WINDOW, KV_WIDTH)), const((WINDOW, KV_WIDTH)),
                   const((POOL_HALO, POOL_WIDTH))],
        scratch_shapes=[pltpu.VMEM((WINDOW, KV_WIDTH), cdt), pltpu.VMEM((WINDOW, KV_WIDTH), cdt),
                        pltpu.VMEM((POOL_PAD + POOL_HALO + rows, POOL_WIDTH), F32), pltpu.VMEM((rows, D_MODEL), cdt),
                        pltpu.VMEM((D_MODEL, IN_WIDTH), cdt), pltpu.VMEM((D_MODEL, D_MODEL), cdt)]
        + [pltpu.VMEM((POOL_PAD + POOL_HALO + rows, POOL_GC), F32)] * 5,
    )
    return pl.pallas_call(
        functools.partial(_mixer_prompt_kernel, row_offset=row_offset, aliased=aliased),
        grid_spec=grid_spec,
        out_shape=[jax.ShapeDtypeStruct(x1_shape, F32), jax.ShapeDtypeStruct((WINDOW, KV_WIDTH), F32),
                   jax.ShapeDtypeStruct((WINDOW, KV_WIDTH), F32), jax.ShapeDtypeStruct((POOL_HALO, POOL_WIDTH), F32)],
        input_output_aliases=aliases,
        compiler_params=pltpu.CompilerParams(dimension_semantics=("arbitrary",), vmem_limit_bytes=VMEM_LIMIT),
        name="mixer_prompt",
    )(*operands, *extra_args)


def _mixer_sample_kernel(x_ref, ck_ref, cv_ref, st_ref, cos_ref, sa_ref, sb_ref, sink8_ref, g_ref, win_ref, qn_ref,
                         kn_ref, hm_ref, wpool_ref, pscale_ref, wout_ref,
                         x1_ref, nk_ref, nv_ref, nst_ref, o8, win_c, wout_c, *, pos):
    nb = x_ref.shape[0]
    wb = ck_ref.shape[1]

    @pl.when(pl.program_id(0) == 0)
    def _():
        _prep_mixer_weights(win_ref, wout_ref, win_c, wout_c)

    x = x_ref[...]
    h = _rms(x, g_ref[...]).astype(BF16)
    proj = jnp.dot(h, win_c[...], preferred_element_type=F32)
    cos, sin_a, sin_b = cos_ref[...], sa_ref[...], sb_ref[...]
    hmean = hm_ref[...].astype(BF16)
    k = _head_norm_rope(proj[:, ATTN_WIDTH:ATTN_WIDTH + KV_WIDTH], hmean, kn_ref[...], cos, sin_a, sin_b)
    v = proj[:, ATTN_WIDTH + KV_WIDTH:ATTN_WIDTH + 2 * KV_WIDTH]
    u = proj[:, ATTN_WIDTH + 2 * KV_WIDTH:]

    nk_ref[:, 0:wb - 1, :] = ck_ref[:, 1:wb, :]
    nv_ref[:, 0:wb - 1, :] = cv_ref[:, 1:wb, :]
    for b in range(nb):
        nk_ref[b, wb - 1:wb, :] = k[b:b + 1, :]
        nv_ref[b, wb - 1:wb, :] = v[b:b + 1, :]

    r8 = lax.broadcasted_iota(I32, (nb * 8, LANES), 0)
    lane8 = lax.broadcasted_iota(I32, (nb * 8, LANES), 1)
    keep = (lane8 < HEAD_DIM) == (r8 % 2 == 0)
    rep = (lax.broadcasted_iota(I32, (nb * 8, nb), 0) // 8 == lax.broadcasted_iota(I32, (nb * 8, nb), 1)).astype(BF16)
    q8 = jnp.zeros((nb * 8, LANES), F32)
    scale = HEAD_DIM ** -0.5
    for j in range(Q_TILES):
        qt = _head_norm_rope(proj[:, j * LANES:(j + 1) * LANES], hmean, qn_ref[...], cos, sin_a, sin_b) * scale
        qrep = jnp.dot(rep, qt.astype(BF16), preferred_element_type=F32)
        q8 = jnp.where(keep & ((r8 % 8) // 2 == j), qrep, q8)
    q8 = q8.astype(BF16)

    sink8 = sink8_ref[:, 0:1]
    ones_bf = jnp.ones((wb, LANES), BF16)
    assert pos >= wb - 1 and wb <= WINDOW
    for b in range(nb):
        kb = nk_ref[b].astype(BF16)
        vb = nv_ref[b].astype(BF16)
        s = lax.dot_general(q8[b * 8:(b + 1) * 8], kb, (((1,), (1,)), ((), ())), preferred_element_type=F32)
        o8[b * 8:(b + 1) * 8, :] = _softmax_pv(s, sink8, vb, ones_bf)
    o8m = jnp.where(keep, o8[...], 0.0).astype(BF16)

    a_tiles = []
    sel_r = lax.broadcasted_iota(I32, (nb, nb * 8), 1)
    sel_b = lax.broadcasted_iota(I32, (nb, nb * 8), 0)
    for j in range(Q_TILES):
        sel = ((sel_r // 8 == sel_b) & ((sel_r % 8) // 2 == j)).astype(BF16)
        a_tiles.append(jnp.dot(sel, o8m, preferred_element_type=F32))

    z_tiles = []
    for gi, w in enumerate(POOL_WINDOWS):
        cols = slice(gi * POOL_GC, (gi + 1) * POOL_GC)
        wsum = u[:, cols]
        for sft in range(1, w):
            wsum = wsum + st_ref[POOL_PREFIX - sft, :, cols]
        d = wsum / float(min(pos + 1, w)) - u[:, cols]
        z_tiles.append(jnp.dot(d.astype(BF16), wpool_ref[gi].astype(BF16), preferred_element_type=F32)
                       * pscale_ref[:, cols])
    nst_ref[0:POOL_PREFIX - 1] = st_ref[1:POOL_PREFIX]
    nst_ref[POOL_PREFIX - 1] = u

    mixv = jnp.concatenate(a_tiles + z_tiles, axis=1).astype(BF16)
    x1_ref[...] = x + jnp.dot(mixv, wout_c[...], preferred_element_type=F32)


def _mixer_sample(x, ck, cv, st, pos, tabs, sink8, g_attn, w_in, qn, kn, hmean, w_pool, pscale, w_out):
    n, wb = ck.shape[0], ck.shape[1]
    nb = SAMPLE_CHUNK
    assert n % nb == 0
    const = lambda shape: pl.BlockSpec(shape, lambda i: (0,) * len(shape))
    cache_blk = pl.BlockSpec((nb, wb, KV_WIDTH), lambda i: (i, 0, 0))
    st_blk = pl.BlockSpec((POOL_PREFIX, nb, POOL_WIDTH), lambda i: (0, i, 0))
    x_blk = pl.BlockSpec((nb, D_MODEL), lambda i: (i, 0))
    return pl.pallas_call(
        functools.partial(_mixer_sample_kernel, pos=pos),
        grid=(n // nb,),
        in_specs=[x_blk, cache_blk, cache_blk, st_blk, const((1, LANES)), const((1, LANES)), const((1, LANES)),
                  const((8, LANES)), const((1, D_MODEL)), const((D_MODEL, IN_WIDTH)), const((1, LANES)),
                  const((1, LANES)), const((LANES, LANES)), const((len(POOL_WINDOWS), POOL_GC, POOL_GC)),
                  const((1, POOL_WIDTH)), const((D_MODEL, D_MODEL))],
        out_specs=[x_blk, cache_blk, cache_blk, st_blk],
        out_shape=[jax.ShapeDtypeStruct((n, D_MODEL), F32), jax.ShapeDtypeStruct(ck.shape, F32),
                   jax.ShapeDtypeStruct(cv.shape, F32), jax.ShapeDtypeStruct(st.shape, F32)],
        scratch_shapes=[pltpu.VMEM((nb * 8, LANES), F32), pltpu.VMEM((D_MODEL, IN_WIDTH), BF16),
                        pltpu.VMEM((D_MODEL, D_MODEL), BF16)],
        compiler_params=pltpu.CompilerParams(dimension_semantics=("arbitrary",), vmem_limit_bytes=VMEM_LIMIT),
        name="mixer_sample",
    )(x, ck, cv, st, *tabs, sink8, g_attn, w_in, qn, kn, hmean, w_pool, pscale, w_out)


def _block_plan(cnt, bm, n_lanes):
    e_sub = lax.broadcasted_iota(I32, (N_EXPERTS, LANES), 0)
    e_lane = lax.broadcasted_iota(I32, (N_EXPERTS, LANES), 1)
    padded = jnp.floor((cnt + (bm - 1)) / bm) * bm
    padded_lane = jnp.sum(jnp.where(e_sub == e_lane, padded, 0.0), axis=0, keepdims=True)
    pad_end = jnp.sum(jnp.where(e_lane <= e_sub, padded_lane, 0.0), axis=1, keepdims=True)
    pad_start = pad_end - padded
    blk_start = lax.broadcasted_iota(I32, (N_EXPERTS, n_lanes), 1).astype(F32) * bm
    blk_e = jnp.minimum(jnp.sum((pad_end <= blk_start).astype(F32), axis=0, keepdims=True), N_EXPERTS - 1.0)
    mine = lax.broadcasted_iota(I32, (N_EXPERTS, n_lanes), 0).astype(F32) == blk_e
    last = jnp.sum(jnp.where(mine, pad_start + cnt, 0.0), axis=0, keepdims=True)
    blk_valid = jnp.clip(last - blk_start[0:1], 0.0, float(bm))
    e_blk = lax.broadcasted_iota(I32, (N_EXPERTS, n_lanes), 0).astype(F32)
    later = jnp.min(jnp.where((e_blk > blk_e) & (cnt > 0.0), e_blk, float(N_EXPERTS)), axis=0, keepdims=True)
    blk_next = jnp.where(later < N_EXPERTS, later, -1.0)
    return pad_start, blk_e, blk_valid, blk_next


def _route_kernel(x1_ref, g_ref, wr_ref, br_ref, cnt_in_ref, hf_ref, idx_ref, gate_ref, pos_ref, cnt_ref,
                  pstart_ref, blke_ref, blkv_ref, blkn_ref, counts, *, bm):
    i = pl.program_id(0)
    rows = x1_ref.shape[0]

    @pl.when(i == 0)
    def _():
        counts[...] = cnt_in_ref[...]

    h = _rms(x1_ref[...], g_ref[...])
    hf_ref[...] = _pack_halves(h)

    logits = lax.dot_general(wr_ref[...], h, (((1,), (1,)), ((), ())), preferred_element_type=F32,
                             precision=lax.Precision.HIGHEST) + br_ref[...]
    eid = lax.broadcasted_iota(I32, (N_EXPERTS, rows), 0).astype(F32)
    work = logits
    vals, hots = [], []
    for kk in range(TOP_K):
        m = jnp.max(work, axis=0, keepdims=True)
        first = jnp.min(jnp.where(work == m, eid, float(N_EXPERTS)), axis=0, keepdims=True)
        hot = eid == first
        work = jnp.where(hot, -jnp.inf, work)
        vals.append(m)
        hots.append(hot)
        idx_ref[kk:kk + 1, :] = first.astype(I32)
    es = [jnp.exp(vv - vals[0]) for vv in vals]
    den = es[0] + es[1] + es[2] + es[3]
    for kk in range(TOP_K):
        gate_ref[kk:kk + 1, :] = es[kk] / den

    chosen = hots[0] | hots[1] | hots[2] | hots[3]
    before = (lax.broadcasted_iota(I32, (rows, rows), 0) < lax.broadcasted_iota(I32, (rows, rows), 1)).astype(BF16)
    rank = jnp.dot(chosen.astype(BF16), before, preferred_element_type=F32) + counts[...]
    for kk in range(TOP_K):
        pos_ref[kk:kk + 1, :] = jnp.sum(jnp.where(hots[kk], rank, 0.0), axis=0, keepdims=True).astype(I32)
    counts[...] = counts[...] + jnp.sum(chosen.astype(F32), axis=1, keepdims=True)
    cnt_ref[...] = counts[...]

    @pl.when(i == pl.num_programs(0) - 1)
    def _():
        pad_start, blk_e, blk_valid, blk_next = _block_plan(counts[...], bm, blke_ref.shape[1])
        pstart_ref[...] = pad_start.astype(I32)
        blke_ref[...] = blk_e.astype(I32)
        blkv_ref[...] = blk_valid.astype(I32)
        blkn_ref[...] = blk_next.astype(I32)


def _route(x1, g_ffn, wr_t, br, cnt_in, rows, bm, n_blocks):
    n = x1.shape[0]
    assert n % rows == 0
    nb_lanes = -(-n_blocks // LANES) * LANES
    const = lambda shape: pl.BlockSpec(shape, lambda i: (0,) * len(shape))
    tok_blk = pl.BlockSpec((TOP_K, rows), lambda i: (0, i))
    return pl.pallas_call(
        functools.partial(_route_kernel, bm=bm),
        grid=(n // rows,),
        in_specs=[pl.BlockSpec((rows, D_MODEL), lambda i: (i, 0)), const((1, D_MODEL)), const((N_EXPERTS, D_MODEL)),
                  const((N_EXPERTS, 1)), const((N_EXPERTS, 1))],
        out_specs=[pl.BlockSpec((rows, D_MODEL // 2), lambda i: (i, 0)), tok_blk, tok_blk, tok_blk,
                   const((N_EXPERTS, 1)), const((N_EXPERTS, 1)), const((1, nb_lanes)), const((1, nb_lanes)),
                   const((1, nb_lanes))],
        out_shape=[jax.ShapeDtypeStruct((n, D_MODEL // 2), U32), jax.ShapeDtypeStruct((TOP_K, n), I32),
                   jax.ShapeDtypeStruct((TOP_K, n), F32), jax.ShapeDtypeStruct((TOP_K, n), I32),
                   jax.ShapeDtypeStruct((N_EXPERTS, 1), F32), jax.ShapeDtypeStruct((N_EXPERTS, 1), I32),
                   jax.ShapeDtypeStruct((1, nb_lanes), I32), jax.ShapeDtypeStruct((1, nb_lanes), I32),
                   jax.ShapeDtypeStruct((1, nb_lanes), I32)],
        scratch_shapes=[pltpu.VMEM((N_EXPERTS, 1), F32)],
        compiler_params=pltpu.CompilerParams(dimension_semantics=("arbitrary",), vmem_limit_bytes=VMEM_LIMIT),
        name="route",
    )(x1, g_ffn, wr_t, br, cnt_in)


def _sc_mesh():
    return plsc.VectorSubcoreMesh(core_axis_name="core", subcore_axis_name="subcore")


def _sc_worker_id():
    return lax.axis_index("core") * SC_SUBCORES + lax.axis_index("subcore")


def _scatter_rows(xa, xb, dest, n_rows):
    ch = SC_CHUNK
    na, w = xa.shape
    n = na + xb.shape[0]
    nk = dest.shape[0]
    assert na % ch == 0 and n % ch == 0 and dest.shape[1] == n and xb.shape[1] == w and xa.dtype == xb.dtype
    n_chunks = n // ch
    dest_c = dest.reshape(nk, n_chunks, ch).transpose(1, 0, 2).reshape(-1)
    dma = pltpu.SemaphoreType.DMA

    @pl.kernel(out_type=jax.ShapeDtypeStruct((n_rows, w), xa.dtype), mesh=_sc_mesh(),
               scratch_types=[pltpu.VMEM((nk * ch,), I32), pltpu.VMEM((ch, w), xa.dtype), dma] + [dma] * nk)
    def scatter_kernel(xa_hbm, xb_hbm, d_hbm, o_hbm, idx_v, buf, sem_i, *sem_s):
        wid = _sc_worker_id()

        @pl.loop(0, -(-n_chunks // SC_WORKERS))
        def _(j):
            c = j * SC_WORKERS + wid

            @pl.when(c < n_chunks)
            def _():
                load_idx = pltpu.make_async_copy(d_hbm.at[pl.ds(c * (nk * ch), nk * ch)], idx_v, sem_i)
                load_idx.start()

                @pl.when(c < na // ch)
                def _():
                    pltpu.sync_copy(xa_hbm.at[pl.ds(c * ch, ch)], buf)

                @pl.when(c >= na // ch)
                def _():
                    pltpu.sync_copy(xb_hbm.at[pl.ds(c * ch - na, ch)], buf)

                load_idx.wait()
                puts = [pltpu.make_async_copy(buf, o_hbm.at[idx_v.at[pl.ds(kk * ch, ch)]], sem_s[kk])
                        for kk in range(nk)]
                for cp in puts:
                    cp.start()
                for cp in puts:
                    cp.wait()

    return scatter_kernel(xa, xb, dest_c)


def _gather_rows(src, idx):
    ch = SC_CHUNK
    m = idx.shape[0]
    w = src.shape[1]
    per = m // SC_WORKERS
    n_pairs = per // (2 * ch)
    assert m % SC_WORKERS == 0 and per % (2 * ch) == 0
    dma = pltpu.SemaphoreType.DMA

    @pl.kernel(out_type=jax.ShapeDtypeStruct((m, w), src.dtype), mesh=_sc_mesh(),
               scratch_types=[pltpu.VMEM((per,), I32), pltpu.VMEM((ch, w), src.dtype), pltpu.VMEM((ch, w), src.dtype),
                              dma, dma, dma, dma])
    def gather_kernel(s_hbm, i_hbm, o_hbm, idx_v, buf_a, buf_b, sem_ga, sem_gb, sem_wa, sem_wb):
        base = _sc_worker_id() * per
        pltpu.sync_copy(i_hbm.at[pl.ds(base, per)], idx_v)

        def fetch(j, buf, sem):
            return pltpu.make_async_copy(s_hbm.at[idx_v.at[pl.ds(j * ch, ch)]], buf, sem)

        def put(j, buf, sem):
            return pltpu.make_async_copy(buf, o_hbm.at[pl.ds(base + j * ch, ch)], sem)

        fetch(0, buf_a, sem_ga).start()

        @pl.loop(0, n_pairs)
        def _(p):
            j0 = 2 * p
            j1 = j0 + 1

            @pl.when(p > 0)
            def _():
                put(j1 - 2, buf_b, sem_wb).wait()

            fetch(j1, buf_b, sem_gb).start()
            fetch(j0, buf_a, sem_ga).wait()
            put(j0, buf_a, sem_wa).start()
            fetch(j1, buf_b, sem_gb).wait()
            put(j1, buf_b, sem_wb).start()
            put(j0, buf_a, sem_wa).wait()

            @pl.when(p + 1 < n_pairs)
            def _():
                fetch(j0 + 2, buf_a, sem_ga).start()

        put(2 * n_pairs - 1, buf_b, sem_wb).wait()

    return gather_kernel(src, idx)


def _moe_kernel(blk_e_ref, blk_valid_ref, blk_next_ref, xs_ref, wgu_hbm, bgu_ref, wd_hbm, bd_ref, y_ref,
                wgu_f32, wd_f32, wgu_bf, wd_bf, sems, *, e0):
    i = pl.program_id(0)
    e = blk_e_ref[i]
    n_valid = blk_valid_ref[i]
    used = n_valid > 0
    new_expert = (i == 0) | (blk_e_ref[jnp.maximum(i - 1, 0)] != e)

    def weight_copies(expert):
        return (pltpu.make_async_copy(wgu_hbm.at[e0 + expert], wgu_f32, sems.at[0]),
                pltpu.make_async_copy(wd_hbm.at[e0 + expert], wd_f32, sems.at[1]))

    @pl.when(used & (i == 0))
    def _():
        for cp in weight_copies(e):
            cp.start()

    @pl.when(used & new_expert)
    def _():
        for cp in weight_copies(e):
            cp.wait()
        wgu_bf[...] = wgu_f32[...].astype(BF16)
        wd_bf[...] = wd_f32[...].astype(BF16)
        nxt = blk_next_ref[i]

        @pl.when(nxt >= 0)
        def _():
            for cp in weight_copies(nxt):
                cp.start()

    half = xs_ref.shape[0] // 2

    def expert_mlp(n_rows):
        valid = lax.broadcasted_iota(I32, (n_rows, xs_ref.shape[1]), 0) < n_valid
        xb = _unpack_halves(jnp.where(valid, xs_ref[0:n_rows, :], jnp.uint32(0))).astype(BF16)
        gu = jnp.dot(xb, wgu_bf[...], preferred_element_type=F32) + bgu_ref[0]
        g = jnp.minimum(gu[:, :D_FF], SWIGLU_LIMIT)
        up = jnp.clip(gu[:, D_FF:], -SWIGLU_LIMIT, SWIGLU_LIMIT)
        act = (up + 1.0) * (g * jax.nn.sigmoid(SWIGLU_ALPHA * g))
        y = jnp.dot(act.astype(BF16), wd_bf[...], preferred_element_type=F32) + bd_ref[0]
        y_ref[0:n_rows, :] = _pack_halves(y)

    @pl.when(n_valid > half)
    def _():
        expert_mlp(2 * half)

    @pl.when(used & (n_valid <= half))
    def _():
        expert_mlp(half)
        y_ref[half:, :] = jnp.zeros((half, y_ref.shape[1]), y_ref.dtype)

    @pl.when(jnp.logical_not(used))
    def _():
        y_ref[...] = jnp.zeros_like(y_ref)


def _moe(xs, blk_e, blk_valid, blk_next, layer, w_gu, b_gu, w_d, b_d):
    n_rows = xs.shape[0]
    bm = MOE_BM
    assert n_rows % bm == 0
    e0 = layer * N_EXPERTS
    n_we = w_gu.shape[0] * w_gu.shape[1]
    any_spec = pl.BlockSpec(memory_space=pl.ANY)
    grid_spec = pltpu.PrefetchScalarGridSpec(
        num_scalar_prefetch=3,
        grid=(n_rows // bm,),
        in_specs=[pl.BlockSpec((bm, D_MODEL // 2), lambda i, be, bv, bn: (i, 0)),
                  any_spec,
                  pl.BlockSpec((1, 1, 2 * D_FF), lambda i, be, bv, bn: (e0 + be[i], 0, 0)),
                  any_spec,
                  pl.BlockSpec((1, 1, D_MODEL), lambda i, be, bv, bn: (e0 + be[i], 0, 0))],
        out_specs=pl.BlockSpec((bm, D_MODEL // 2), lambda i, be, bv, bn: (i, 0)),
        scratch_shapes=[pltpu.VMEM((D_MODEL, 2 * D_FF), F32), pltpu.VMEM((D_FF, D_MODEL), F32),
                        pltpu.VMEM((D_MODEL, 2 * D_FF), BF16), pltpu.VMEM((D_FF, D_MODEL), BF16),
                        pltpu.SemaphoreType.DMA((2,))],
    )
    return pl.pallas_call(
        functools.partial(_moe_kernel, e0=e0),
        grid_spec=grid_spec,
        out_shape=jax.ShapeDtypeStruct((n_rows, D_MODEL // 2), U32),
        compiler_params=pltpu.CompilerParams(dimension_semantics=("arbitrary",), vmem_limit_bytes=VMEM_LIMIT),
        name="moe_experts",
    )(blk_e, blk_valid, blk_next, xs, w_gu.reshape(n_we, D_MODEL, 2 * D_FF), b_gu.reshape(n_we, 1, 2 * D_FF),
      w_d.reshape(n_we, D_FF, D_MODEL), b_d.reshape(n_we, 1, D_MODEL))


def _ple_kernel(x1_ref, y0_ref, y1_ref, y2_ref, y3_ref, gates_ref, p_ref, g_ref, wg_ref, wp_ref, *rest):
    out_ref, wg_bf, wp_bf = rest[-3:]

    @pl.when(pl.program_id(0) == 0)
    def _():
        wg_bf[...] = wg_ref[...].astype(BF16)
        wp_bf[...] = wp_ref[...].astype(BF16)

    x2 = x1_ref[...]
    gates = gates_ref[...]
    for kk, y_ref in enumerate((y0_ref, y1_ref, y2_ref, y3_ref)):
        x2 = x2 + _unpack_halves(y_ref[...]) * gates[:, kk:kk + 1]
    hp = _rms(x2, g_ref[...]).astype(BF16)
    gate = 0.5 * jnp.tanh(0.5 * jnp.dot(hp, wg_bf[...], preferred_element_type=F32)) + 0.5
    pp = jnp.dot(p_ref[...].astype(BF16), wp_bf[...], preferred_element_type=F32)
    out_ref[...] = x2 + gate * pp


def _ple(x1, tok0, n, y_tok, y0, gates_t, p_all, p0, g_ple, w_gate, w_proj, rows, out_into=None):
    assert n % rows == 0 and tok0 % rows == 0 and y0 % rows == 0 and p0 % rows == 0
    const = lambda shape: pl.BlockSpec(shape, lambda i: (0,) * len(shape))
    tok_blk = lambda width: pl.BlockSpec((rows, width), lambda i: (tok0 // rows + i, 0))
    y_blk = lambda kk: pl.BlockSpec((rows, D_MODEL // 2), lambda i: ((y0 + kk * n) // rows + i, 0))
    operands = (x1, y_tok, y_tok, y_tok, y_tok, gates_t, p_all, g_ple, w_gate, w_proj)
    extra_specs, extra_args, aliases = [], [], {}
    if out_into is not None:
        assert out_into.shape == x1.shape
        extra_specs, extra_args, aliases = [pl.BlockSpec(memory_space=pl.ANY)], [out_into], {len(operands): 0}
    return pl.pallas_call(
        _ple_kernel,
        grid=(n // rows,),
        in_specs=[tok_blk(D_MODEL), y_blk(0), y_blk(1), y_blk(2), y_blk(3), tok_blk(TOP_K),
                  pl.BlockSpec((rows, PLE_DIM), lambda i: (p0 // rows + i, 0)),
                  const((1, D_MODEL)), const((D_MODEL, D_MODEL)), const((PLE_DIM, D_MODEL))] + extra_specs,
        out_specs=tok_blk(D_MODEL),
        out_shape=jax.ShapeDtypeStruct(x1.shape, F32),
        scratch_shapes=[pltpu.VMEM((D_MODEL, D_MODEL), BF16), pltpu.VMEM((PLE_DIM, D_MODEL), BF16)],
        input_output_aliases=aliases,
        compiler_params=pltpu.CompilerParams(dimension_semantics=("arbitrary",), vmem_limit_bytes=VMEM_LIMIT),
        name="combine_ple",
    )(*operands, *extra_args)


def _rope_tables_const(seq):
    half = ROT_DIM // 2
    d = np.arange(LANES) % HEAD_DIM
    inv = (np.float64(ROPE_THETA) ** (-np.arange(half, dtype=np.float64) / half)).astype(np.float32)
    inv_lane = np.where(d < ROT_DIM, inv[d % half], np.float32(0.0)).astype(np.float32)
    ang = np.arange(seq, dtype=np.float32)[:, None] * inv_lane[None, :]
    cos = np.cos(ang.astype(np.float64)).astype(np.float32)
    sin = np.sin(ang.astype(np.float64)).astype(np.float32)
    sin_a = np.where((d >= half) & (d < ROT_DIM), sin, np.float32(0.0))
    sin_b = np.where(d < half, -sin, np.float32(0.0))
    return tuple(jnp.asarray(a) for a in (cos, sin_a, sin_b))


def _rope_tables(pos):
    half = ROT_DIM // 2
    d = np.arange(LANES) % HEAD_DIM
    inv = ROPE_THETA ** (-jnp.arange(half, dtype=F32) / half)
    inv_lane = jnp.where(d < ROT_DIM, inv[d % half], 0.0)
    ang = pos.astype(F32)[:, None] * inv_lane[None, :]
    cos, sin = jnp.cos(ang), jnp.sin(ang)
    sin_a = jnp.where((d >= half) & (d < ROT_DIM), sin, 0.0)
    sin_b = jnp.where(d < half, -sin, 0.0)
    return cos, sin_a, sin_b


def _layer(layer, xp, xs, ck, cv, st, p_prompt_all, p_sample_all, past_len, refine_tail, tabs_p, tabs_s, finish_state,
           norm_attn, w_in, q_norm, k_norm, attn_sinks, w_pool, pool_scale, w_out,
           norm_ffn, w_router, b_router, w_gate_up_all, b_gate_up_all, w_down_all, b_down_all,
           norm_ple, w_ple_gate, w_ple_proj):
    t, ns = xp.shape[0], xs.shape[0]
    g_attn = norm_attn.reshape(1, D_MODEL)
    qn = jnp.tile(q_norm, 2).reshape(1, LANES)
    kn = jnp.tile(k_norm, 2).reshape(1, LANES)
    lane = np.arange(LANES)
    hmean = jnp.asarray((lane[:, None] // HEAD_DIM == lane[None, :] // HEAD_DIM) / HEAD_DIM, F32)
    pscale = pool_scale.reshape(1, POOL_WIDTH)
    sink8 = jnp.broadcast_to(attn_sinks.reshape(2, Q_TILES).T.reshape(8, 1), (8, LANES))

    mix_args = (g_attn, w_in, qn, kn, hmean, w_pool, pscale, w_out)
    x1p, nk_p, nv_p, nu_p = _mixer_prompt(xp, 0, MIX_ROWS, BF16, attn_sinks, tabs_p, *mix_args)
    if refine_tail:
        x1p, nk_p, nv_p, nu_p = _mixer_prompt(xp, t - 2 * MIX_TAIL_ROWS, MIX_TAIL_ROWS, F32, attn_sinks, tabs_p,
                                              *mix_args, x1_into=x1p)
    st_t = jnp.transpose(st, (1, 0, 2))
    x1s, nk_s, nv_s, nst_t = _mixer_sample(xs, ck.reshape(ns, -1, KV_WIDTH), cv.reshape(ns, -1, KV_WIDTH), st_t,
                                           past_len, tabs_s, sink8, *mix_args)

    g_ffn = norm_ffn.reshape(1, D_MODEL)
    wr_t = w_router.T
    br = b_router.reshape(N_EXPERTS, 1)
    n_tok = t + ns
    bm = MOE_BM
    n_blocks = -(-(n_tok * TOP_K + N_EXPERTS * (bm - 1)) // bm)
    hf_p, idx_p, gate_p, pos_p, cnt_p = _route(x1p, g_ffn, wr_t, br, jnp.zeros((N_EXPERTS, 1), F32),
                                               ROUTE_ROWS, bm, n_blocks)[:5]
    hf_s, idx_s, gate_s, pos_s, _, pad_start, blk_e, blk_valid, blk_next = _route(x1s, g_ffn, wr_t, br, cnt_p, ns,
                                                                                  bm, n_blocks)
    idx = jnp.concatenate([idx_p, idx_s], axis=1)
    pos = jnp.concatenate([pos_p, pos_s], axis=1)
    blk_e, blk_valid, blk_next = (a[0, :n_blocks] for a in (blk_e, blk_valid, blk_next))
    state = finish_state(nk_p, nv_p, nu_p[POOL_HALO - POOL_PREFIX:], nk_s, nv_s, jnp.transpose(nst_t, (1, 0, 2)))
    blk_e, state = lax.optimization_barrier((blk_e, state))
    start_of = jnp.sum(jnp.where(idx[None] == jnp.arange(N_EXPERTS, dtype=I32)[:, None, None],
                                 pad_start.reshape(N_EXPERTS, 1, 1), 0), axis=0)
    dest = start_of + pos

    xs_rows = _scatter_rows(hf_p, hf_s, dest, n_blocks * bm)
    y = _moe(xs_rows, blk_e, blk_valid, blk_next, layer, w_gate_up_all, b_gate_up_all, w_down_all, b_down_all)

    tq = t // COMBINE_PARTS
    unit = SC_WORKERS * SC_CHUNK * 2
    assert t % COMBINE_PARTS == 0 and (TOP_K * tq) % unit == 0 and tq % PLE_ROWS == 0
    g_ple = norm_ple.reshape(1, D_MODEL)
    gates_p = gate_p.T
    yp = None
    for part in range(COMBINE_PARTS):
        back = dest[:, part * tq:(part + 1) * tq].reshape(-1)
        if part == COMBINE_PARTS - 1:
            back = jnp.concatenate([back, dest[:, t:].reshape(-1)])
            back = jnp.concatenate([back, jnp.arange(-back.shape[0] % unit, dtype=I32)])
        y_part = _gather_rows(y, back)
        yp = _ple(x1p, part * tq, tq, y_part, 0, gates_p, p_prompt_all, layer * t + part * tq, g_ple,
                  w_ple_gate, w_ple_proj, PLE_ROWS, out_into=yp)
    ys = _ple(x1s, 0, ns, y_part, TOP_K * tq, gate_s.T, p_sample_all, layer * ns, g_ple, w_ple_gate, w_ple_proj, ns)
    return yp, ys, state


def kernel(x_prompt, x_sample, cache_k, cache_v, state_pool, p_prompt, p_sample, norm_attn, w_in, q_norm, k_norm,
           attn_sinks, w_pool, pool_scale, w_out, norm_ffn, w_router, b_router, w_gate_up, b_gate_up, w_down, b_down,
           norm_ple, w_ple_gate, w_ple_proj):
    depth = norm_attn.shape[0]
    batch, seq, d = x_prompt.shape
    ns, dec_seq, _ = x_sample.shape
    wb = cache_k.shape[2]
    assert batch == 1 and dec_seq == 1 and d == D_MODEL and wb == WINDOW
    assert cache_k.shape[3:] == (N_KV_HEADS, HEAD_DIM) and state_pool.shape[2:] == (POOL_PREFIX, POOL_WIDTH)
    past_len = PAST_LEN
    yp = x_prompt.reshape(seq, d)
    ys = x_sample.reshape(ns, d)
    p_prompt_all = p_prompt.reshape(depth * seq, PLE_DIM)
    p_sample_all = p_sample.reshape(depth * ns, PLE_DIM)
    tabs_p = _rope_tables_const(seq)
    tabs_s = _rope_tables(jnp.full((1,), past_len))
    kv_p, kv_s = (1, WINDOW, N_KV_HEADS, HEAD_DIM), (ns, wb, N_KV_HEADS, HEAD_DIM)
    state_shapes = (kv_p, kv_p, (1, POOL_PREFIX, POOL_WIDTH), kv_s, kv_s, (ns, POOL_PREFIX, POOL_WIDTH))
    per_layer = []

    def shaped(*state):
        return tuple(a.reshape(shp) for a, shp in zip(state, state_shapes))

    def stacked(*state):
        prevs = list(zip(*per_layer)) or [()] * len(state_shapes)
        return tuple(jnp.stack(list(prev) + [cur]) for prev, cur in zip(prevs, shaped(*state)))

    for i in range(depth):
        yp, ys, state = _layer(i, yp, ys, cache_k[i], cache_v[i], state_pool[i], p_prompt_all, p_sample_all, past_len,
                               i < depth - 1, tabs_p, tabs_s, shaped if i < depth - 1 else stacked,
                               norm_attn[i], w_in[i], q_norm[i], k_norm[i], attn_sinks[i], w_pool[i], pool_scale[i],
                               w_out[i], norm_ffn[i], w_router[i], b_router[i], w_gate_up, b_gate_up, w_down, b_down,
                               norm_ple[i], w_ple_gate[i], w_ple_proj[i])
        per_layer.append(state)
    return (yp.reshape(batch, seq, d), ys.reshape(ns, dec_seq, d)) + tuple(state)
```

```python
import functools

import jax
import jax.numpy as jnp
import numpy as np
from jax import lax
from jax.experimental import pallas as pl
from jax.experimental.pallas import tpu as pltpu
from jax.experimental.pallas import tpu_sc as plsc

F32 = jnp.float32
BF16 = jnp.bfloat16
U32 = jnp.uint32
I32 = jnp.int32

D_MODEL = 1024
HEAD_DIM = 64
N_HEADS = 8
N_KV_HEADS = 2
GROUP = N_HEADS // N_KV_HEADS
ATTN_WIDTH = N_HEADS * HEAD_DIM
KV_WIDTH = N_KV_HEADS * HEAD_DIM
POOL_WIDTH = 512
POOL_WINDOWS = (2, 4, 8, 16)
POOL_GC = POOL_WIDTH // len(POOL_WINDOWS)
POOL_PREFIX = max(POOL_WINDOWS) - 1
POOL_HALO = POOL_PREFIX + 1
POOL_PAD = 8
IN_WIDTH = ATTN_WIDTH + 2 * KV_WIDTH + POOL_WIDTH
WINDOW = 128
ROPE_THETA = 500000.0
ROT_DIM = HEAD_DIM // 4
N_EXPERTS = 32
TOP_K = 4
D_FF = 1024
SWIGLU_ALPHA = 1.702
SWIGLU_LIMIT = 7.0
PLE_DIM = 256
PAST_LEN = 16384
EPS = 1e-5
NEG_INF = -1e30

LANES = 128
Q_TILES = ATTN_WIDTH // LANES

MIX_ROWS = 1024
MIX_TAIL_ROWS = 128
SAMPLE_CHUNK = 16
ROUTE_ROWS = 1024
MOE_BM = 512
SC_CORES = 2
SC_SUBCORES = 16
SC_WORKERS = SC_CORES * SC_SUBCORES
SC_CHUNK = 64
PLE_ROWS = 1024
COMBINE_PARTS = 2
VMEM_LIMIT = 56 * 1024 * 1024


def _rms(x, g):
    return x * lax.rsqrt(jnp.mean(x * x, axis=-1, keepdims=True) + EPS) * g


def _pack_halves(x):
    w = x.shape[1] // 2
    lo = lax.bitcast_convert_type(x[:, :w].astype(BF16).astype(F32), U32) >> 16
    hi = lax.bitcast_convert_type(x[:, w:].astype(BF16).astype(F32), U32) & jnp.uint32(0xFFFF0000)
    return lo | hi


def _unpack_halves(packed):
    lo = lax.bitcast_convert_type(packed << 16, F32)
    hi = lax.bitcast_convert_type(packed & jnp.uint32(0xFFFF0000), F32)
    return jnp.concatenate([lo, hi], axis=1)


def _mm(a, b, nt=False):
    dims = (((1,), (1 if nt else 0,)), ((), ()))
    if b.dtype == F32:
        return lax.dot_general(a.astype(F32), b, dims, preferred_element_type=F32, precision=lax.Precision.HIGHEST)
    return lax.dot_general(a.astype(BF16), b, dims, preferred_element_type=F32)


def _head_norm_rope(t, hmean, gain, cos, sin_a, sin_b):
    t = t * lax.rsqrt(_mm(t * t, hmean) + EPS) * gain
    return t * cos + pltpu.roll(t, ROT_DIM // 2, axis=1) * sin_a + pltpu.roll(t, LANES - ROT_DIM // 2, axis=1) * sin_b


def _softmax_pv(s, sink, v, ones):
    m = jnp.maximum(jnp.max(s, axis=-1, keepdims=True), sink)
    e = jnp.exp(s - m).astype(v.dtype)
    den = _mm(e, ones) + jnp.exp(sink - m)
    return _mm(e, v) / den


def _prep_mixer_weights(win_ref, wout_ref, win_c, wout_c):
    cdt = win_c.dtype
    for j in range(Q_TILES):
        for s in range(2):
            src = (j + Q_TILES * s) * HEAD_DIM
            dst = j * LANES + s * HEAD_DIM
            win_c[:, dst:dst + HEAD_DIM] = win_ref[:, src:src + HEAD_DIM].astype(cdt)
            wout_c[dst:dst + HEAD_DIM, :] = wout_ref[src:src + HEAD_DIM, :].astype(cdt)
    win_c[:, ATTN_WIDTH:] = win_ref[:, ATTN_WIDTH:].astype(cdt)
    wout_c[ATTN_WIDTH:, :] = wout_ref[ATTN_WIDTH:, :].astype(cdt)


def _mixer_prompt_kernel(sinks_ref, x_ref, cos_ref, sa_ref, sb_ref, g_ref, win_ref, qn_ref, kn_ref, hm_ref,
                         wpool_ref, pscale_ref, wout_ref, *rest, row_offset, aliased):
    (x1_ref, klast_ref, vlast_ref, ulast_ref, kprev, vprev, uext, mix, win_c, wout_c,
     *lvl) = rest[1:] if aliased else rest
    i = pl.program_id(0)
    rows = x_ref.shape[0]
    n_sub = rows // WINDOW
    cdt = win_c.dtype
    row0 = row_offset + i * rows

    @pl.when(i == 0)
    def _():
        _prep_mixer_weights(win_ref, wout_ref, win_c, wout_c)
        kprev[...] = jnp.zeros_like(kprev)
        vprev[...] = jnp.zeros_like(vprev)
        uext[0:POOL_PAD + POOL_HALO, :] = jnp.zeros((POOL_PAD + POOL_HALO, POOL_WIDTH), F32)
        for buf in lvl:
            buf[0:POOL_PAD, :] = jnp.zeros((POOL_PAD, POOL_GC), F32)

    x = x_ref[...]
    proj = _mm(_rms(x, g_ref[...]), win_c[...])
    cos, sin_a, sin_b = cos_ref[...], sa_ref[...], sb_ref[...]

    n_t = Q_TILES + 1
    t_all = jnp.concatenate([proj[:, j * LANES:(j + 1) * LANES] for j in range(n_t)], axis=0)
    t3 = (t_all * lax.rsqrt(_mm(t_all * t_all, hm_ref[...].astype(cdt)) + EPS)).reshape(n_t, rows, LANES)
    t3 = jnp.concatenate([t3[:Q_TILES] * (qn_ref[...] * HEAD_DIM ** -0.5), t3[Q_TILES:] * kn_ref[...]], axis=0)
    t2 = t3.reshape(n_t * rows, LANES)
    t3 = (t3 * cos + pltpu.roll(t2, ROT_DIM // 2, axis=1).reshape(n_t, rows, LANES) * sin_a
          + pltpu.roll(t2, LANES - ROT_DIM // 2, axis=1).reshape(n_t, rows, LANES) * sin_b)
    q3 = t3[:Q_TILES]
    k = t3[Q_TILES]
    v = proj[:, ATTN_WIDTH + KV_WIDTH:ATTN_WIDTH + 2 * KV_WIDTH]
    u = proj[:, ATTN_WIDTH + 2 * KV_WIDTH:]
    klast_ref[...] = k[rows - WINDOW:, :]
    vlast_ref[...] = v[rows - WINDOW:, :]
    ulast_ref[...] = u[rows - POOL_HALO:, :]
    k_c = k.astype(cdt)
    v_c = jnp.concatenate([v.astype(cdt), jnp.ones((rows, LANES), cdt)], axis=1)
    v_first = jnp.concatenate([vprev[...], jnp.ones((WINDOW, LANES), cdt)], axis=1)

    lane = lax.broadcasted_iota(I32, (WINDOW, LANES), 1)
    left = (lane < HEAD_DIM)[None]
    qi = lax.broadcasted_iota(I32, (WINDOW, 2 * WINDOW), 0)
    kj = lax.broadcasted_iota(I32, (WINDOW, 2 * WINDOW), 1)
    band = (kj - qi >= 1) & (kj - qi <= WINDOW)
    sink3 = jnp.concatenate([jnp.full((1, 1, 1), sinks_ref[j + Q_TILES * s], F32)
                             for j in range(Q_TILES) for s in range(2)], axis=0)
    n_g = 2 * Q_TILES
    fill = jnp.where(kj[0:1][None] == 0, sink3, NEG_INF)
    slot = lax.broadcasted_iota(I32, (2 * WINDOW, 2 * LANES), 0) == 0
    slot_v = slot & (lax.broadcasted_iota(I32, (2 * WINDOW, 2 * LANES), 1) < LANES)

    v_cats, masks, scores = [], [], []
    for c in range(n_sub):
        r0 = c * WINDOW
        if c == 0:
            k_cat = jnp.concatenate([kprev[...], k_c[0:WINDOW]], axis=0)
            v_cat = jnp.concatenate([v_first, v_c[0:WINDOW]], axis=0)
            masks.append(band & (kj + (row0 - WINDOW) >= 0))
        else:
            k_cat = k_c[r0 - WINDOW:r0 + WINDOW]
            v_cat = v_c[r0 - WINDOW:r0 + WINDOW]
            masks.append(band)
        v_cats.append(jnp.where(slot_v, jnp.zeros_like(v_cat), v_cat))
        q_c = q3[:, r0:r0 + WINDOW, :]
        q_all = jnp.concatenate([jnp.where(left, q_c, 0.0), jnp.where(left, 0.0, q_c)], axis=1)
        scores.append(_mm(q_all.reshape(n_g * WINDOW, LANES), k_cat, nt=True))
    kprev[...] = k_c[rows - WINDOW:]
    vprev[...] = v[rows - WINDOW:].astype(cdt)

    probs = []
    for c in range(n_sub):
        s = jnp.where(masks[c][None], scores[c].reshape(n_g, WINDOW, 2 * WINDOW), fill)
        m = jnp.max(s, axis=-1, keepdims=True)
        probs.append(jnp.exp(s - m).astype(cdt).reshape(n_g * WINDOW, 2 * WINDOW))
    applied = [_mm(probs[c], v_cats[c]) for c in range(n_sub)]
    for c in range(n_sub):
        r0 = c * WINDOW
        o = (applied[c][:, :LANES] / applied[c][:, LANES:]).reshape(Q_TILES, 2 * WINDOW, LANES)
        a = jnp.where(left, o[:, :WINDOW], o[:, WINDOW:])
        for j in range(Q_TILES):
            mix[r0:r0 + WINDOW, j * LANES:(j + 1) * LANES] = a[j].astype(cdt)

    base = POOL_PAD + POOL_HALO
    ext = base + rows
    uext[base:ext, :] = u
    pos1 = (lax.broadcasted_iota(I32, (rows, 1), 0) + row0 + 1).astype(F32)
    lvl_of = {1: lvl[0:1], 2: lvl[1:3], 3: lvl[3:5]}
    for gi, w in enumerate(POOL_WINDOWS):
        cols = slice(gi * POOL_GC, (gi + 1) * POOL_GC)
        src, src_cols = uext, cols
        for level in range(1, gi + 2):
            sft = 1 << (level - 1)
            if level <= gi:
                dst = lvl_of[gi][(level - 1) % 2]
                dst[POOL_PAD:ext, :] = src[POOL_PAD:ext, src_cols] + src[POOL_PAD - sft:ext - sft, src_cols]
                src, src_cols = dst, slice(None)
            else:
                wsum = src[base:ext, src_cols] + src[base - sft:ext - sft, src_cols]
        d = wsum / jnp.minimum(pos1, float(w)) - u[:, cols]
        y = _mm(d, wpool_ref[gi].astype(cdt)) * pscale_ref[:, cols]
        mix[:, ATTN_WIDTH + gi * POOL_GC:ATTN_WIDTH + (gi + 1) * POOL_GC] = y.astype(cdt)
    uext[POOL_PAD:base, :] = u[rows - POOL_HALO:, :]

    x1_ref[...] = x + _mm(mix[...], wout_c[...])


def _mixer_prompt(x_full, row_offset, rows, cdt, sinks, tabs, g_attn, w_in, qn, kn, hmean, w_pool, pscale, w_out,
                  x1_into=None):
    t = x_full.shape[0] - row_offset
    assert t % rows == 0 and row_offset % rows == 0 and rows % WINDOW == 0 and rows >= POOL_HALO
    blk0, n_steps = row_offset // rows, t // rows
    const = lambda shape: pl.BlockSpec(shape, lambda i, *_: (0,) * len(shape))
    row_blk = lambda width: pl.BlockSpec((rows, width), lambda i, *_: (blk0 + i, 0))
    aliased = x1_into is not None
    operands = (sinks, x_full, *tabs, g_attn, w_in, qn, kn, hmean, w_pool, pscale, w_out)
    if aliased:
        assert x1_into.shape == x_full.shape
        x1_spec = pl.BlockSpec((rows, D_MODEL), lambda i, *_: (blk0 + n_steps - 1, 0))
        x1_shape = x1_into.shape
        extra_specs, extra_args, aliases = [pl.BlockSpec(memory_space=pl.ANY)], [x1_into], {len(operands): 0}
    else:
        x1_spec = pl.BlockSpec((rows, D_MODEL), lambda i, *_: (i, 0))
        x1_shape = (t, D_MODEL)
        extra_specs, extra_args, aliases = [], [], {}
    grid_spec = pltpu.PrefetchScalarGridSpec(
        num_scalar_prefetch=1,
        grid=(n_steps,),
        in_specs=[row_blk(D_MODEL), row_blk(LANES), row_blk(LANES), row_blk(LANES),
                  const((1, D_MODEL)), const((D_MODEL, IN_WIDTH)), const((1, LANES)), const((1, LANES)),
                  const((LANES, LANES)), const((len(POOL_WINDOWS), POOL_GC, POOL_GC)), const((1, POOL_WIDTH)),
                  const((D_MODEL, D_MODEL))] + extra_specs,
        out_specs=[x1_spec, const((WINDOW, KV_WIDTH)), const((WINDOW, KV_WIDTH)),
                   const((POOL_HALO, POOL_WIDTH))],
        scratch_shapes=[pltpu.VMEM((WINDOW, KV_WIDTH), cdt), pltpu.VMEM((WINDOW, KV_WIDTH), cdt),
                        pltpu.VMEM((POOL_PAD + POOL_HALO + rows, POOL_WIDTH), F32), pltpu.VMEM((rows, D_MODEL), cdt),
                        pltpu.VMEM((D_MODEL, IN_WIDTH), cdt), pltpu.VMEM((D_MODEL, D_MODEL), cdt)]
        + [pltpu.VMEM((POOL_PAD + POOL_HALO + rows, POOL_GC), F32)] * 5,
    )
    return pl.pallas_call(
        functools.partial(_mixer_prompt_kernel, row_offset=row_offset, aliased=aliased),
        grid_spec=grid_spec,
        out_shape=[jax.ShapeDtypeStruct(x1_shape, F32), jax.ShapeDtypeStruct((WINDOW, KV_WIDTH), F32),
                   jax.ShapeDtypeStruct((WINDOW, KV_WIDTH), F32), jax.ShapeDtypeStruct((POOL_HALO, POOL_WIDTH), F32)],
        input_output_aliases=aliases,
        compiler_params=pltpu.CompilerParams(dimension_semantics=("arbitrary",), vmem_limit_bytes=VMEM_LIMIT),
        name="mixer_prompt",
    )(*operands, *extra_args)


def _mixer_sample_kernel(x_ref, ck_ref, cv_ref, st_ref, cos_ref, sa_ref, sb_ref, sink8_ref, g_ref, win_ref, qn_ref,
                         kn_ref, hm_ref, wpool_ref, pscale_ref, wout_ref,
                         x1_ref, nk_ref, nv_ref, nst_ref, o8, win_c, wout_c, *, pos):
    nb = x_ref.shape[0]
    wb = ck_ref.shape[1]

    @pl.when(pl.program_id(0) == 0)
    def _():
        _prep_mixer_weights(win_ref, wout_ref, win_c, wout_c)

    x = x_ref[...]
    h = _rms(x, g_ref[...]).astype(BF16)
    proj = jnp.dot(h, win_c[...], preferred_element_type=F32)
    cos, sin_a, sin_b = cos_ref[...], sa_ref[...], sb_ref[...]
    hmean = hm_ref[...].astype(BF16)
    k = _head_norm_rope(proj[:, ATTN_WIDTH:ATTN_WIDTH + KV_WIDTH], hmean, kn_ref[...], cos, sin_a, sin_b)
    v = proj[:, ATTN_WIDTH + KV_WIDTH:ATTN_WIDTH + 2 * KV_WIDTH]
    u = proj[:, ATTN_WIDTH + 2 * KV_WIDTH:]

    nk_ref[:, 0:wb - 1, :] = ck_ref[:, 1:wb, :]
    nv_ref[:, 0:wb - 1, :] = cv_ref[:, 1:wb, :]
    for b in range(nb):
        nk_ref[b, wb - 1:wb, :] = k[b:b + 1, :]
        nv_ref[b, wb - 1:wb, :] = v[b:b + 1, :]

    r8 = lax.broadcasted_iota(I32, (nb * 8, LANES), 0)
    lane8 = lax.broadcasted_iota(I32, (nb * 8, LANES), 1)
    keep = (lane8 < HEAD_DIM) == (r8 % 2 == 0)
    rep = (lax.broadcasted_iota(I32, (nb * 8, nb), 0) // 8 == lax.broadcasted_iota(I32, (nb * 8, nb), 1)).astype(BF16)
    q8 = jnp.zeros((nb * 8, LANES), F32)
    scale = HEAD_DIM ** -0.5
    for j in range(Q_TILES):
        qt = _head_norm_rope(proj[:, j * LANES:(j + 1) * LANES], hmean, qn_ref[...], cos, sin_a, sin_b) * scale
        qrep = jnp.dot(rep, qt.astype(BF16), preferred_element_type=F32)
        q8 = jnp.where(keep & ((r8 % 8) // 2 == j), qrep, q8)
    q8 = q8.astype(BF16)

    sink8 = sink8_ref[:, 0:1]
    ones_bf = jnp.ones((wb, LANES), BF16)
    assert pos >= wb - 1 and wb <= WINDOW
    for b in range(nb):
        kb = nk_ref[b].astype(BF16)
        vb = nv_ref[b].astype(BF16)
        s = lax.dot_general(q8[b * 8:(b + 1) * 8], kb, (((1,), (1,)), ((), ())), preferred_element_type=F32)
        o8[b * 8:(b + 1) * 8, :] = _softmax_pv(s, sink8, vb, ones_bf)
    o8m = jnp.where(keep, o8[...], 0.0).astype(BF16)

    a_tiles = []
    sel_r = lax.broadcasted_iota(I32, (nb, nb * 8), 1)
    sel_b = lax.broadcasted_iota(I32, (nb, nb * 8), 0)
    for j in range(Q_TILES):
        sel = ((sel_r // 8 == sel_b) & ((sel_r % 8) // 2 == j)).astype(BF16)
        a_tiles.append(jnp.dot(sel, o8m, preferred_element_type=F32))

    z_tiles = []
    for gi, w in enumerate(POOL_WINDOWS):
        cols = slice(gi * POOL_GC, (gi + 1) * POOL_GC)
        wsum = u[:, cols]
        for sft in range(1, w):
            wsum = wsum + st_ref[POOL_PREFIX - sft, :, cols]
        d = wsum / float(min(pos + 1, w)) - u[:, cols]
        z_tiles.append(jnp.dot(d.astype(BF16), wpool_ref[gi].astype(BF16), preferred_element_type=F32)
                       * pscale_ref[:, cols])
    nst_ref[0:POOL_PREFIX - 1] = st_ref[1:POOL_PREFIX]
    nst_ref[POOL_PREFIX - 1] = u

    mixv = jnp.concatenate(a_tiles + z_tiles, axis=1).astype(BF16)
    x1_ref[...] = x + jnp.dot(mixv, wout_c[...], preferred_element_type=F32)


def _mixer_sample(x, ck, cv, st, pos, tabs, sink8, g_attn, w_in, qn, kn, hmean, w_pool, pscale, w_out):
    n, wb = ck.shape[0], ck.shape[1]
    nb = SAMPLE_CHUNK
    assert n % nb == 0
    const = lambda shape: pl.BlockSpec(shape, lambda i: (0,) * len(shape))
    cache_blk = pl.BlockSpec((nb, wb, KV_WIDTH), lambda i: (i, 0, 0))
    st_blk = pl.BlockSpec((POOL_PREFIX, nb, POOL_WIDTH), lambda i: (0, i, 0))
    x_blk = pl.BlockSpec((nb, D_MODEL), lambda i: (i, 0))
    return pl.pallas_call(
        functools.partial(_mixer_sample_kernel, pos=pos),
        grid=(n // nb,),
        in_specs=[x_blk, cache_blk, cache_blk, st_blk, const((1, LANES)), const((1, LANES)), const((1, LANES)),
                  const((8, LANES)), const((1, D_MODEL)), const((D_MODEL, IN_WIDTH)), const((1, LANES)),
                  const((1, LANES)), const((LANES, LANES)), const((len(POOL_WINDOWS), POOL_GC, POOL_GC)),
                  const((1, POOL_WIDTH)), const((D_MODEL, D_MODEL))],
        out_specs=[x_blk, cache_blk, cache_blk, st_blk],
        out_shape=[jax.ShapeDtypeStruct((n, D_MODEL), F32), jax.ShapeDtypeStruct(ck.shape, F32),
                   jax.ShapeDtypeStruct(cv.shape, F32), jax.ShapeDtypeStruct(st.shape, F32)],
        scratch_shapes=[pltpu.VMEM((nb * 8, LANES), F32), pltpu.VMEM((D_MODEL, IN_WIDTH), BF16),
                        pltpu.VMEM((D_MODEL, D_MODEL), BF16)],
        compiler_params=pltpu.CompilerParams(dimension_semantics=("arbitrary",), vmem_limit_bytes=VMEM_LIMIT),
        name="mixer_sample",
    )(x, ck, cv, st, *tabs, sink8, g_attn, w_in, qn, kn, hmean, w_pool, pscale, w_out)


def _block_plan(cnt, bm, n_lanes):
    e_sub = lax.broadcasted_iota(I32, (N_EXPERTS, LANES), 0)
    e_lane = lax.broadcasted_iota(I32, (N_EXPERTS, LANES), 1)
    padded = jnp.floor((cnt + (bm - 1)) / bm) * bm
    padded_lane = jnp.sum(jnp.where(e_sub == e_lane, padded, 0.0), axis=0, keepdims=True)
    pad_end = jnp.sum(jnp.where(e_lane <= e_sub, padded_lane, 0.0), axis=1, keepdims=True)
    pad_start = pad_end - padded
    blk_start = lax.broadcasted_iota(I32, (N_EXPERTS, n_lanes), 1).astype(F32) * bm
    blk_e = jnp.minimum(jnp.sum((pad_end <= blk_start).astype(F32), axis=0, keepdims=True), N_EXPERTS - 1.0)
    mine = lax.broadcasted_iota(I32, (N_EXPERTS, n_lanes), 0).astype(F32) == blk_e
    last = jnp.sum(jnp.where(mine, pad_start + cnt, 0.0), axis=0, keepdims=True)
    blk_valid = jnp.clip(last - blk_start[0:1], 0.0, float(bm))
    e_blk = lax.broadcasted_iota(I32, (N_EXPERTS, n_lanes), 0).astype(F32)
    later = jnp.min(jnp.where((e_blk > blk_e) & (cnt > 0.0), e_blk, float(N_EXPERTS)), axis=0, keepdims=True)
    blk_next = jnp.where(later < N_EXPERTS, later, -1.0)
    return pad_start, blk_e, blk_valid, blk_next


def _route_kernel(x1_ref, g_ref, wr_ref, br_ref, cnt_in_ref, hf_ref, idx_ref, gate_ref, pos_ref, cnt_ref,
                  pstart_ref, blke_ref, blkv_ref, blkn_ref, counts, *, bm):
    i = pl.program_id(0)
    rows = x1_ref.shape[0]

    @pl.when(i == 0)
    def _():
        counts[...] = cnt_in_ref[...]

    h = _rms(x1_ref[...], g_ref[...])
    hf_ref[...] = _pack_halves(h)

    logits = lax.dot_general(wr_ref[...], h, (((1,), (1,)), ((), ())), preferred_element_type=F32,
                             precision=lax.Precision.HIGHEST) + br_ref[...]
    eid = lax.broadcasted_iota(I32, (N_EXPERTS, rows), 0).astype(F32)
    work = logits
    vals, hots = [], []
    for kk in range(TOP_K):
        m = jnp.max(work, axis=0, keepdims=True)
        first = jnp.min(jnp.where(work == m, eid, float(N_EXPERTS)), axis=0, keepdims=True)
        hot = eid == first
        work = jnp.where(hot, -jnp.inf, work)
        vals.append(m)
        hots.append(hot)
        idx_ref[kk:kk + 1, :] = first.astype(I32)
    es = [jnp.exp(vv - vals[0]) for vv in vals]
    den = es[0] + es[1] + es[2] + es[3]
    for kk in range(TOP_K):
        gate_ref[kk:kk + 1, :] = es[kk] / den

    chosen = hots[0] | hots[1] | hots[2] | hots[3]
    before = (lax.broadcasted_iota(I32, (rows, rows), 0) < lax.broadcasted_iota(I32, (rows, rows), 1)).astype(BF16)
    rank = jnp.dot(chosen.astype(BF16), before, preferred_element_type=F32) + counts[...]
    for kk in range(TOP_K):
        pos_ref[kk:kk + 1, :] = jnp.sum(jnp.where(hots[kk], rank, 0.0), axis=0, keepdims=True).astype(I32)
    counts[...] = counts[...] + jnp.sum(chosen.astype(F32), axis=1, keepdims=True)
    cnt_ref[...] = counts[...]

    @pl.when(i == pl.num_programs(0) - 1)
    def _():
        pad_start, blk_e, blk_valid, blk_next = _block_plan(counts[...], bm, blke_ref.shape[1])
        pstart_ref[...] = pad_start.astype(I32)
        blke_ref[...] = blk_e.astype(I32)
        blkv_ref[...] = blk_valid.astype(I32)
        blkn_ref[...] = blk_next.astype(I32)


def _route(x1, g_ffn, wr_t, br, cnt_in, rows, bm, n_blocks):
    n = x1.shape[0]
    assert n % rows == 0
    nb_lanes = -(-n_blocks // LANES) * LANES
    const = lambda shape: pl.BlockSpec(shape, lambda i: (0,) * len(shape))
    tok_blk = pl.BlockSpec((TOP_K, rows), lambda i: (0, i))
    return pl.pallas_call(
        functools.partial(_route_kernel, bm=bm),
        grid=(n // rows,),
        in_specs=[pl.BlockSpec((rows, D_MODEL), lambda i: (i, 0)), const((1, D_MODEL)), const((N_EXPERTS, D_MODEL)),
                  const((N_EXPERTS, 1)), const((N_EXPERTS, 1))],
        out_specs=[pl.BlockSpec((rows, D_MODEL // 2), lambda i: (i, 0)), tok_blk, tok_blk, tok_blk,
                   const((N_EXPERTS, 1)), const((N_EXPERTS, 1)), const((1, nb_lanes)), const((1, nb_lanes)),
                   const((1, nb_lanes))],
        out_shape=[jax.ShapeDtypeStruct((n, D_MODEL // 2), U32), jax.ShapeDtypeStruct((TOP_K, n), I32),
                   jax.ShapeDtypeStruct((TOP_K, n), F32), jax.ShapeDtypeStruct((TOP_K, n), I32),
                   jax.ShapeDtypeStruct((N_EXPERTS, 1), F32), jax.ShapeDtypeStruct((N_EXPERTS, 1), I32),
                   jax.ShapeDtypeStruct((1, nb_lanes), I32), jax.ShapeDtypeStruct((1, nb_lanes), I32),
                   jax.ShapeDtypeStruct((1, nb_lanes), I32)],
        scratch_shapes=[pltpu.VMEM((N_EXPERTS, 1), F32)],
        compiler_params=pltpu.CompilerParams(dimension_semantics=("arbitrary",), vmem_limit_bytes=VMEM_LIMIT),
        name="route",
    )(x1, g_ffn, wr_t, br, cnt_in)


def _sc_mesh():
    return plsc.VectorSubcoreMesh(core_axis_name="core", subcore_axis_name="subcore")


def _sc_worker_id():
    return lax.axis_index("core") * SC_SUBCORES + lax.axis_index("subcore")


def _scatter_rows(xa, xb, dest, n_rows):
    ch = SC_CHUNK
    na, w = xa.shape
    n = na + xb.shape[0]
    nk = dest.shape[0]
    assert na % ch == 0 and n % ch == 0 and dest.shape[1] == n and xb.shape[1] == w and xa.dtype == xb.dtype
    n_chunks = n // ch
    dest_c = dest.reshape(nk, n_chunks, ch).transpose(1, 0, 2).reshape(-1)
    dma = pltpu.SemaphoreType.DMA

    @pl.kernel(out_type=jax.ShapeDtypeStruct((n_rows, w), xa.dtype), mesh=_sc_mesh(),
               scratch_types=[pltpu.VMEM((nk * ch,), I32), pltpu.VMEM((ch, w), xa.dtype), dma] + [dma] * nk)
    def scatter_kernel(xa_hbm, xb_hbm, d_hbm, o_hbm, idx_v, buf, sem_i, *sem_s):
        wid = _sc_worker_id()

        @pl.loop(0, -(-n_chunks // SC_WORKERS))
        def _(j):
            c = j * SC_WORKERS + wid

            @pl.when(c < n_chunks)
            def _():
                load_idx = pltpu.make_async_copy(d_hbm.at[pl.ds(c * (nk * ch), nk * ch)], idx_v, sem_i)
                load_idx.start()

                @pl.when(c < na // ch)
                def _():
                    pltpu.sync_copy(xa_hbm.at[pl.ds(c * ch, ch)], buf)

                @pl.when(c >= na // ch)
                def _():
                    pltpu.sync_copy(xb_hbm.at[pl.ds(c * ch - na, ch)], buf)

                load_idx.wait()
                puts = [pltpu.make_async_copy(buf, o_hbm.at[idx_v.at[pl.ds(kk * ch, ch)]], sem_s[kk])
                        for kk in range(nk)]
                for cp in puts:
                    cp.start()
                for cp in puts:
                    cp.wait()

    return scatter_kernel(xa, xb, dest_c)


def _gather_rows(src, idx):
    ch = SC_CHUNK
    m = idx.shape[0]
    w = src.shape[1]
    per = m // SC_WORKERS
    n_pairs = per // (2 * ch)
    assert m % SC_WORKERS == 0 and per % (2 * ch) == 0
    dma = pltpu.SemaphoreType.DMA

    @pl.kernel(out_type=jax.ShapeDtypeStruct((m, w), src.dtype), mesh=_sc_mesh(),
               scratch_types=[pltpu.VMEM((per,), I32), pltpu.VMEM((ch, w), src.dtype), pltpu.VMEM((ch, w), src.dtype),
                              dma, dma, dma, dma])
    def gather_kernel(s_hbm, i_hbm, o_hbm, idx_v, buf_a, buf_b, sem_ga, sem_gb, sem_wa, sem_wb):
        base = _sc_worker_id() * per
        pltpu.sync_copy(i_hbm.at[pl.ds(base, per)], idx_v)

        def fetch(j, buf, sem):
            return pltpu.make_async_copy(s_hbm.at[idx_v.at[pl.ds(j * ch, ch)]], buf, sem)

        def put(j, buf, sem):
            return pltpu.make_async_copy(buf, o_hbm.at[pl.ds(base + j * ch, ch)], sem)

        fetch(0, buf_a, sem_ga).start()

        @pl.loop(0, n_pairs)
        def _(p):
            j0 = 2 * p
            j1 = j0 + 1

            @pl.when(p > 0)
            def _():
                put(j1 - 2, buf_b, sem_wb).wait()

            fetch(j1, buf_b, sem_gb).start()
            fetch(j0, buf_a, sem_ga).wait()
            put(j0, buf_a, sem_wa).start()
            fetch(j1, buf_b, sem_gb).wait()
            put(j1, buf_b, sem_wb).start()
            put(j0, buf_a, sem_wa).wait()

            @pl.when(p + 1 < n_pairs)
            def _():
                fetch(j0 + 2, buf_a, sem_ga).start()

        put(2 * n_pairs - 1, buf_b, sem_wb).wait()

    return gather_kernel(src, idx)


def _moe_kernel(blk_e_ref, blk_valid_ref, blk_next_ref, xs_ref, wgu_hbm, bgu_ref, wd_hbm, bd_ref, y_ref,
                wgu_f32, wd_f32, wgu_bf, wd_bf, sems, *, e0):
    i = pl.program_id(0)
    e = blk_e_ref[i]
    n_valid = blk_valid_ref[i]
    used = n_valid > 0
    new_expert = (i == 0) | (blk_e_ref[jnp.maximum(i - 1, 0)] != e)

    def weight_copies(expert):
        return (pltpu.make_async_copy(wgu_hbm.at[e0 + expert], wgu_f32, sems.at[0]),
                pltpu.make_async_copy(wd_hbm.at[e0 + expert], wd_f32, sems.at[1]))

    @pl.when(used & (i == 0))
    def _():
        for cp in weight_copies(e):
            cp.start()

    @pl.when(used & new_expert)
    def _():
        for cp in weight_copies(e):
            cp.wait()
        wgu_bf[...] = wgu_f32[...].astype(BF16)
        wd_bf[...] = wd_f32[...].astype(BF16)
        nxt = blk_next_ref[i]

        @pl.when(nxt >= 0)
        def _():
            for cp in weight_copies(nxt):
                cp.start(priority=1)

    half = xs_ref.shape[0] // 2

    def expert_mlp(n_rows):
        valid = lax.broadcasted_iota(I32, (n_rows, xs_ref.shape[1]), 0) < n_valid
        xb = _unpack_halves(jnp.where(valid, xs_ref[0:n_rows, :], jnp.uint32(0))).astype(BF16)
        gu = jnp.dot(xb, wgu_bf[...], preferred_element_type=F32) + bgu_ref[0]
        g = jnp.minimum(gu[:, :D_FF], SWIGLU_LIMIT)
        up = jnp.clip(gu[:, D_FF:], -SWIGLU_LIMIT, SWIGLU_LIMIT)
        act = (up + 1.0) * (g * jax.nn.sigmoid(SWIGLU_ALPHA * g))
        y = jnp.dot(act.astype(BF16), wd_bf[...], preferred_element_type=F32) + bd_ref[0]
        y_ref[0:n_rows, :] = _pack_halves(y)

    @pl.when(n_valid > half)
    def _():
        expert_mlp(2 * half)

    @pl.when(used & (n_valid <= half))
    def _():
        expert_mlp(half)
        y_ref[half:, :] = jnp.zeros((half, y_ref.shape[1]), y_ref.dtype)

    @pl.when(jnp.logical_not(used))
    def _():
        y_ref[...] = jnp.zeros_like(y_ref)


def _moe(xs, blk_e, blk_valid, blk_next, layer, w_gu, b_gu, w_d, b_d):
    n_rows = xs.shape[0]
    bm = MOE_BM
    assert n_rows % bm == 0
    e0 = layer * N_EXPERTS
    n_we = w_gu.shape[0] * w_gu.shape[1]
    any_spec = pl.BlockSpec(memory_space=pl.ANY)
    grid_spec = pltpu.PrefetchScalarGridSpec(
        num_scalar_prefetch=3,
        grid=(n_rows // bm,),
        in_specs=[pl.BlockSpec((bm, D_MODEL // 2), lambda i, be, bv, bn: (i, 0)),
                  any_spec,
                  pl.BlockSpec((1, 1, 2 * D_FF), lambda i, be, bv, bn: (e0 + be[i], 0, 0)),
                  any_spec,
                  pl.BlockSpec((1, 1, D_MODEL), lambda i, be, bv, bn: (e0 + be[i], 0, 0))],
        out_specs=pl.BlockSpec((bm, D_MODEL // 2), lambda i, be, bv, bn: (i, 0)),
        scratch_shapes=[pltpu.VMEM((D_MODEL, 2 * D_FF), F32), pltpu.VMEM((D_FF, D_MODEL), F32),
                        pltpu.VMEM((D_MODEL, 2 * D_FF), BF16), pltpu.VMEM((D_FF, D_MODEL), BF16),
                        pltpu.SemaphoreType.DMA((2,))],
    )
    return pl.pallas_call(
        functools.partial(_moe_kernel, e0=e0),
        grid_spec=grid_spec,
        out_shape=jax.ShapeDtypeStruct((n_rows, D_MODEL // 2), U32),
        compiler_params=pltpu.CompilerParams(dimension_semantics=("arbitrary",), vmem_limit_bytes=VMEM_LIMIT),
        name="moe_experts",
    )(blk_e, blk_valid, blk_next, xs, w_gu.reshape(n_we, D_MODEL, 2 * D_FF), b_gu.reshape(n_we, 1, 2 * D_FF),
      w_d.reshape(n_we, D_FF, D_MODEL), b_d.reshape(n_we, 1, D_MODEL))


def _ple_kernel(x1_ref, y0_ref, y1_ref, y2_ref, y3_ref, gates_ref, p_ref, g_ref, wg_ref, wp_ref, *rest):
    out_ref, wg_bf, wp_bf = rest[-3:]

    @pl.when(pl.program_id(0) == 0)
    def _():
        wg_bf[...] = wg_ref[...].astype(BF16)
        wp_bf[...] = wp_ref[...].astype(BF16)

    x2 = x1_ref[...]
    gates = gates_ref[...]
    for kk, y_ref in enumerate((y0_ref, y1_ref, y2_ref, y3_ref)):
        x2 = x2 + _unpack_halves(y_ref[...]) * gates[:, kk:kk + 1]
    hp = _rms(x2, g_ref[...]).astype(BF16)
    gate = 0.5 * jnp.tanh(0.5 * jnp.dot(hp, wg_bf[...], preferred_element_type=F32)) + 0.5
    pp = jnp.dot(p_ref[...].astype(BF16), wp_bf[...], preferred_element_type=F32)
    out_ref[...] = x2 + gate * pp


def _ple(x1, tok0, n, y_tok, y0, gates_t, p_all, p0, g_ple, w_gate, w_proj, rows, out_into=None):
    assert n % rows == 0 and tok0 % rows == 0 and y0 % rows == 0 and p0 % rows == 0
    const = lambda shape: pl.BlockSpec(shape, lambda i: (0,) * len(shape))
    tok_blk = lambda width: pl.BlockSpec((rows, width), lambda i: (tok0 // rows + i, 0))
    y_blk = lambda kk: pl.BlockSpec((rows, D_MODEL // 2), lambda i: ((y0 + kk * n) // rows + i, 0))
    operands = (x1, y_tok, y_tok, y_tok, y_tok, gates_t, p_all, g_ple, w_gate, w_proj)
    extra_specs, extra_args, aliases = [], [], {}
    if out_into is not None:
        assert out_into.shape == x1.shape
        extra_specs, extra_args, aliases = [pl.BlockSpec(memory_space=pl.ANY)], [out_into], {len(operands): 0}
    return pl.pallas_call(
        _ple_kernel,
        grid=(n // rows,),
        in_specs=[tok_blk(D_MODEL), y_blk(0), y_blk(1), y_blk(2), y_blk(3), tok_blk(TOP_K),
                  pl.BlockSpec((rows, PLE_DIM), lambda i: (p0 // rows + i, 0)),
                  const((1, D_MODEL)), const((D_MODEL, D_MODEL)), const((PLE_DIM, D_MODEL))] + extra_specs,
        out_specs=tok_blk(D_MODEL),
        out_shape=jax.ShapeDtypeStruct(x1.shape, F32),
        scratch_shapes=[pltpu.VMEM((D_MODEL, D_MODEL), BF16), pltpu.VMEM((PLE_DIM, D_MODEL), BF16)],
        input_output_aliases=aliases,
        compiler_params=pltpu.CompilerParams(dimension_semantics=("arbitrary",), vmem_limit_bytes=VMEM_LIMIT),
        name="combine_ple",
    )(*operands, *extra_args)


def _rope_tables_const(seq):
    half = ROT_DIM // 2
    d = np.arange(LANES) % HEAD_DIM
    inv = (np.float64(ROPE_THETA) ** (-np.arange(half, dtype=np.float64) / half)).astype(np.float32)
    inv_lane = np.where(d < ROT_DIM, inv[d % half], np.float32(0.0)).astype(np.float32)
    ang = np.arange(seq, dtype=np.float32)[:, None] * inv_lane[None, :]
    cos = np.cos(ang.astype(np.float64)).astype(np.float32)
    sin = np.sin(ang.astype(np.float64)).astype(np.float32)
    sin_a = np.where((d >= half) & (d < ROT_DIM), sin, np.float32(0.0))
    sin_b = np.where(d < half, -sin, np.float32(0.0))
    return tuple(jnp.asarray(a) for a in (cos, sin_a, sin_b))


def _rope_tables(pos):
    half = ROT_DIM // 2
    d = np.arange(LANES) % HEAD_DIM
    inv = ROPE_THETA ** (-jnp.arange(half, dtype=F32) / half)
    inv_lane = jnp.where(d < ROT_DIM, inv[d % half], 0.0)
    ang = pos.astype(F32)[:, None] * inv_lane[None, :]
    cos, sin = jnp.cos(ang), jnp.sin(ang)
    sin_a = jnp.where((d >= half) & (d < ROT_DIM), sin, 0.0)
    sin_b = jnp.where(d < half, -sin, 0.0)
    return cos, sin_a, sin_b


def _layer(layer, xp, xs, ck, cv, st, p_prompt_all, p_sample_all, past_len, refine_tail, tabs_p, tabs_s, finish_state,
           norm_attn, w_in, q_norm, k_norm, attn_sinks, w_pool, pool_scale, w_out,
           norm_ffn, w_router, b_router, w_gate_up_all, b_gate_up_all, w_down_all, b_down_all,
           norm_ple, w_ple_gate, w_ple_proj):
    t, ns = xp.shape[0], xs.shape[0]
    g_attn = norm_attn.reshape(1, D_MODEL)
    qn = jnp.tile(q_norm, 2).reshape(1, LANES)
    kn = jnp.tile(k_norm, 2).reshape(1, LANES)
    lane = np.arange(LANES)
    hmean = jnp.asarray((lane[:, None] // HEAD_DIM == lane[None, :] // HEAD_DIM) / HEAD_DIM, F32)
    pscale = pool_scale.reshape(1, POOL_WIDTH)
    sink8 = jnp.broadcast_to(attn_sinks.reshape(2, Q_TILES).T.reshape(8, 1), (8, LANES))

    mix_args = (g_attn, w_in, qn, kn, hmean, w_pool, pscale, w_out)
    x1p, nk_p, nv_p, nu_p = _mixer_prompt(xp, 0, MIX_ROWS, BF16, attn_sinks, tabs_p, *mix_args)
    if refine_tail:
        x1p, nk_p, nv_p, nu_p = _mixer_prompt(xp, t - 2 * MIX_TAIL_ROWS, MIX_TAIL_ROWS, F32, attn_sinks, tabs_p,
                                              *mix_args, x1_into=x1p)
    st_t = jnp.transpose(st, (1, 0, 2))
    x1s, nk_s, nv_s, nst_t = _mixer_sample(xs, ck.reshape(ns, -1, KV_WIDTH), cv.reshape(ns, -1, KV_WIDTH), st_t,
                                           past_len, tabs_s, sink8, *mix_args)

    g_ffn = norm_ffn.reshape(1, D_MODEL)
    wr_t = w_router.T
    br = b_router.reshape(N_EXPERTS, 1)
    n_tok = t + ns
    bm = MOE_BM
    n_blocks = -(-(n_tok * TOP_K + N_EXPERTS * (bm - 1)) // bm)
    hf_p, idx_p, gate_p, pos_p, cnt_p = _route(x1p, g_ffn, wr_t, br, jnp.zeros((N_EXPERTS, 1), F32),
                                               ROUTE_ROWS, bm, n_blocks)[:5]
    hf_s, idx_s, gate_s, pos_s, _, pad_start, blk_e, blk_valid, blk_next = _route(x1s, g_ffn, wr_t, br, cnt_p, ns,
                                                                                  bm, n_blocks)
    idx = jnp.concatenate([idx_p, idx_s], axis=1)
    pos = jnp.concatenate([pos_p, pos_s], axis=1)
    blk_e, blk_valid, blk_next = (a[0, :n_blocks] for a in (blk_e, blk_valid, blk_next))
    state = finish_state(nk_p, nv_p, nu_p[POOL_HALO - POOL_PREFIX:], nk_s, nv_s, jnp.transpose(nst_t, (1, 0, 2)))
    blk_e, state = lax.optimization_barrier((blk_e, state))
    start_of = jnp.sum(jnp.where(idx[None] == jnp.arange(N_EXPERTS, dtype=I32)[:, None, None],
                                 pad_start.reshape(N_EXPERTS, 1, 1), 0), axis=0)
    dest = start_of + pos

    xs_rows = _scatter_rows(hf_p, hf_s, dest, n_blocks * bm)
    y = _moe(xs_rows, blk_e, blk_valid, blk_next, layer, w_gate_up_all, b_gate_up_all, w_down_all, b_down_all)

    tq = t // COMBINE_PARTS
    unit = SC_WORKERS * SC_CHUNK * 2
    assert t % COMBINE_PARTS == 0 and (TOP_K * tq) % unit == 0 and tq % PLE_ROWS == 0
    g_ple = norm_ple.reshape(1, D_MODEL)
    gates_p = gate_p.T
    yp = None
    for part in range(COMBINE_PARTS):
        back = dest[:, part * tq:(part + 1) * tq].reshape(-1)
        if part == COMBINE_PARTS - 1:
            back = jnp.concatenate([back, dest[:, t:].reshape(-1)])
            back = jnp.concatenate([back, jnp.arange(-back.shape[0] % unit, dtype=I32)])
        y_part = _gather_rows(y, back)
        yp = _ple(x1p, part * tq, tq, y_part, 0, gates_p, p_prompt_all, layer * t + part * tq, g_ple,
                  w_ple_gate, w_ple_proj, PLE_ROWS, out_into=yp)
    ys = _ple(x1s, 0, ns, y_part, TOP_K * tq, gate_s.T, p_sample_all, layer * ns, g_ple, w_ple_gate, w_ple_proj, ns)
    return yp, ys, state


def kernel(x_prompt, x_sample, cache_k, cache_v, state_pool, p_prompt, p_sample, norm_attn, w_in, q_norm, k_norm,
           attn_sinks, w_pool, pool_scale, w_out, norm_ffn, w_router, b_router, w_gate_up, b_gate_up, w_down, b_down,
           norm_ple, w_ple_gate, w_ple_proj):
    depth = norm_attn.shape[0]
    batch, seq, d = x_prompt.shape
    ns, dec_seq, _ = x_sample.shape
    wb = cache_k.shape[2]
    assert batch == 1 and dec_seq == 1 and d == D_MODEL and wb == WINDOW
    assert cache_k.shape[3:] == (N_KV_HEADS, HEAD_DIM) and state_pool.shape[2:] == (POOL_PREFIX, POOL_WIDTH)
    past_len = PAST_LEN
    yp = x_prompt.reshape(seq, d)
    ys = x_sample.reshape(ns, d)
    p_prompt_all = p_prompt.reshape(depth * seq, PLE_DIM)
    p_sample_all = p_sample.reshape(depth * ns, PLE_DIM)
    tabs_p = _rope_tables_const(seq)
    tabs_s = _rope_tables(jnp.full((1,), past_len))
    kv_p, kv_s = (1, WINDOW, N_KV_HEADS, HEAD_DIM), (ns, wb, N_KV_HEADS, HEAD_DIM)
    state_shapes = (kv_p, kv_p, (1, POOL_PREFIX, POOL_WIDTH), kv_s, kv_s, (ns, POOL_PREFIX, POOL_WIDTH))
    per_layer = []

    def shaped(*state):
        return tuple(a.reshape(shp) for a, shp in zip(state, state_shapes))

    def stacked(*state):
        prevs = list(zip(*per_layer)) or [()] * len(state_shapes)
        return tuple(jnp.stack(list(prev) + [cur]) for prev, cur in zip(prevs, shaped(*state)))

    for i in range(depth):
        yp, ys, state = _layer(i, yp, ys, cache_k[i], cache_v[i], state_pool[i], p_prompt_all, p_sample_all, past_len,
                               i < depth - 1, tabs_p, tabs_s, shaped if i < depth - 1 else stacked,
                               norm_attn[i], w_in[i], q_norm[i], k_norm[i], attn_sinks[i], w_pool[i], pool_scale[i],
                               w_out[i], norm_ffn[i], w_router[i], b_router[i], w_gate_up, b_gate_up, w_down, b_down,
                               norm_ple[i], w_ple_gate[i], w_ple_proj[i])
        per_layer.append(state)
    return (yp.reshape(batch, seq, d), ys.reshape(ns, dec_seq, d)) + tuple(state)
```

```python
import functools

import jax
import jax.numpy as jnp
import numpy as np
from jax import lax
from jax.experimental import pallas as pl
from jax.experimental.pallas import tpu as pltpu
from jax.experimental.pallas import tpu_sc as plsc

F32 = jnp.float32
BF16 = jnp.bfloat16
U32 = jnp.uint32
I32 = jnp.int32

D_MODEL = 1024
HEAD_DIM = 64
N_HEADS = 8
N_KV_HEADS = 2
GROUP = N_HEADS // N_KV_HEADS
ATTN_WIDTH = N_HEADS * HEAD_DIM
KV_WIDTH = N_KV_HEADS * HEAD_DIM
POOL_WIDTH = 512
POOL_WINDOWS = (2, 4, 8, 16)
POOL_GC = POOL_WIDTH // len(POOL_WINDOWS)
POOL_PREFIX = max(POOL_WINDOWS) - 1
POOL_HALO = POOL_PREFIX + 1
POOL_PAD = 8
IN_WIDTH = ATTN_WIDTH + 2 * KV_WIDTH + POOL_WIDTH
WINDOW = 128
ROPE_THETA = 500000.0
ROT_DIM = HEAD_DIM // 4
N_EXPERTS = 32
TOP_K = 4
D_FF = 1024
SWIGLU_ALPHA = 1.702
SWIGLU_LIMIT = 7.0
PLE_DIM = 256
PAST_LEN = 16384
EPS = 1e-5
NEG_INF = -1e30

LANES = 128
Q_TILES = ATTN_WIDTH // LANES

MIX_ROWS = 1024
MIX_TAIL_ROWS = 128
SAMPLE_CHUNK = 16
ROUTE_ROWS = 1024
MOE_BM = 512
SC_CORES = 2
SC_SUBCORES = 16
SC_WORKERS = SC_CORES * SC_SUBCORES
SC_CHUNK = 64
PLE_ROWS = 1024
COMBINE_PARTS = 2
VMEM_LIMIT = 56 * 1024 * 1024


def _rms(x, g):
    return x * lax.rsqrt(jnp.mean(x * x, axis=-1, keepdims=True) + EPS) * g


def _pack_halves(x):
    w = x.shape[1] // 2
    lo = lax.bitcast_convert_type(x[:, :w].astype(BF16).astype(F32), U32) >> 16
    hi = lax.bitcast_convert_type(x[:, w:].astype(BF16).astype(F32), U32) & jnp.uint32(0xFFFF0000)
    return lo | hi


def _unpack_halves(packed):
    lo = lax.bitcast_convert_type(packed << 16, F32)
    hi = lax.bitcast_convert_type(packed & jnp.uint32(0xFFFF0000), F32)
    return jnp.concatenate([lo, hi], axis=1)


def _mm(a, b, nt=False):
    dims = (((1,), (1 if nt else 0,)), ((), ()))
    if b.dtype == F32:
        return lax.dot_general(a.astype(F32), b, dims, preferred_element_type=F32, precision=lax.Precision.HIGHEST)
    return lax.dot_general(a.astype(BF16), b, dims, preferred_element_type=F32)


def _head_norm_rope(t, hmean, gain, cos, sin_a, sin_b):
    t = t * lax.rsqrt(_mm(t * t, hmean) + EPS) * gain
    return t * cos + pltpu.roll(t, ROT_DIM // 2, axis=1) * sin_a + pltpu.roll(t, LANES - ROT_DIM // 2, axis=1) * sin_b


def _prep_mixer_weights(win_ref, wout_ref, win_c, wout_c):
    cdt = win_c.dtype
    for j in range(Q_TILES):
        for s in range(2):
            src = (j + Q_TILES * s) * HEAD_DIM
            dst = j * LANES + s * HEAD_DIM
            win_c[:, dst:dst + HEAD_DIM] = win_ref[:, src:src + HEAD_DIM].astype(cdt)
            wout_c[dst:dst + HEAD_DIM, :] = wout_ref[src:src + HEAD_DIM, :].astype(cdt)
    win_c[:, ATTN_WIDTH:] = win_ref[:, ATTN_WIDTH:].astype(cdt)
    wout_c[ATTN_WIDTH:, :] = wout_ref[ATTN_WIDTH:, :].astype(cdt)


def _mixer_prompt_kernel(sinks_ref, x_ref, cos_ref, sa_ref, sb_ref, g_ref, win_ref, qn_ref, kn_ref, hm_ref,
                         wpool_ref, pscale_ref, wout_ref, *rest, row_offset, aliased):
    (x1_ref, klast_ref, vlast_ref, ulast_ref, kprev, vprev, uext, mix, win_c, wout_c,
     *lvl) = rest[1:] if aliased else rest
    i = pl.program_id(0)
    rows = x_ref.shape[0]
    n_sub = rows // WINDOW
    cdt = win_c.dtype
    row0 = row_offset + i * rows

    @pl.when(i == 0)
    def _():
        _prep_mixer_weights(win_ref, wout_ref, win_c, wout_c)
        kprev[...] = jnp.zeros_like(kprev)
        vprev[...] = jnp.zeros_like(vprev)
        uext[0:POOL_PAD + POOL_HALO, :] = jnp.zeros((POOL_PAD + POOL_HALO, POOL_WIDTH), F32)
        for buf in lvl:
            buf[0:POOL_PAD, :] = jnp.zeros((POOL_PAD, POOL_GC), F32)

    x = x_ref[...]
    proj = _mm(_rms(x, g_ref[...]), win_c[...])
    cos, sin_a, sin_b = cos_ref[...], sa_ref[...], sb_ref[...]

    n_t = Q_TILES + 1
    t_all = jnp.concatenate([proj[:, j * LANES:(j + 1) * LANES] for j in range(n_t)], axis=0)
    t3 = (t_all * lax.rsqrt(_mm(t_all * t_all, hm_ref[...].astype(cdt)) + EPS)).reshape(n_t, rows, LANES)
    t3 = jnp.concatenate([t3[:Q_TILES] * (qn_ref[...] * HEAD_DIM ** -0.5), t3[Q_TILES:] * kn_ref[...]], axis=0)
    t2 = t3.reshape(n_t * rows, LANES)
    t3 = (t3 * cos + pltpu.roll(t2, ROT_DIM // 2, axis=1).reshape(n_t, rows, LANES) * sin_a
          + pltpu.roll(t2, LANES - ROT_DIM // 2, axis=1).reshape(n_t, rows, LANES) * sin_b)
    q3 = t3[:Q_TILES]
    k = t3[Q_TILES]
    v = proj[:, ATTN_WIDTH + KV_WIDTH:ATTN_WIDTH + 2 * KV_WIDTH]
    u = proj[:, ATTN_WIDTH + 2 * KV_WIDTH:]
    klast_ref[...] = k[rows - WINDOW:, :]
    vlast_ref[...] = v[rows - WINDOW:, :]
    ulast_ref[...] = u[rows - POOL_HALO:, :]
    k_c = k.astype(cdt)
    v_c = jnp.concatenate([v.astype(cdt), jnp.ones((rows, LANES), cdt)], axis=1)
    v_first = jnp.concatenate([vprev[...], jnp.ones((WINDOW, LANES), cdt)], axis=1)

    lane = lax.broadcasted_iota(I32, (WINDOW, LANES), 1)
    left = (lane < HEAD_DIM)[None]
    qi = lax.broadcasted_iota(I32, (WINDOW, 2 * WINDOW), 0)
    kj = lax.broadcasted_iota(I32, (WINDOW, 2 * WINDOW), 1)
    band = (kj - qi >= 1) & (kj - qi <= WINDOW)
    sink3 = jnp.concatenate([jnp.full((1, 1, 1), sinks_ref[j + Q_TILES * s], F32)
                             for j in range(Q_TILES) for s in range(2)], axis=0)
    n_g = 2 * Q_TILES
    fill = jnp.where(kj[0:1][None] == 0, sink3, NEG_INF)
    slot = lax.broadcasted_iota(I32, (2 * WINDOW, 2 * LANES), 0) == 0
    slot_v = slot & (lax.broadcasted_iota(I32, (2 * WINDOW, 2 * LANES), 1) < LANES)

    v_cats, masks, scores = [], [], []
    for c in range(n_sub):
        r0 = c * WINDOW
        if c == 0:
            k_cat = jnp.concatenate([kprev[...], k_c[0:WINDOW]], axis=0)
            v_cat = jnp.concatenate([v_first, v_c[0:WINDOW]], axis=0)
            masks.append(band & (kj + (row0 - WINDOW) >= 0))
        else:
            k_cat = k_c[r0 - WINDOW:r0 + WINDOW]
            v_cat = v_c[r0 - WINDOW:r0 + WINDOW]
            masks.append(band)
        v_cats.append(jnp.where(slot_v, jnp.zeros_like(v_cat), v_cat))
        q_c = q3[:, r0:r0 + WINDOW, :]
        q_all = jnp.concatenate([jnp.where(left, q_c, 0.0), jnp.where(left, 0.0, q_c)], axis=1)
        scores.append(_mm(q_all.reshape(n_g * WINDOW, LANES), k_cat, nt=True))
    kprev[...] = k_c[rows - WINDOW:]
    vprev[...] = v[rows - WINDOW:].astype(cdt)

    probs = []
    for c in range(n_sub):
        s = jnp.where(masks[c][None], scores[c].reshape(n_g, WINDOW, 2 * WINDOW), fill)
        m = jnp.max(s, axis=-1, keepdims=True)
        probs.append(jnp.exp(s - m).astype(cdt).reshape(n_g * WINDOW, 2 * WINDOW))
    applied = [_mm(probs[c], v_cats[c]) for c in range(n_sub)]
    for c in range(n_sub):
        r0 = c * WINDOW
        o = (applied[c][:, :LANES] / applied[c][:, LANES:]).reshape(Q_TILES, 2 * WINDOW, LANES)
        a = jnp.where(left, o[:, :WINDOW], o[:, WINDOW:])
        for j in range(Q_TILES):
            mix[r0:r0 + WINDOW, j * LANES:(j + 1) * LANES] = a[j].astype(cdt)

    base = POOL_PAD + POOL_HALO
    ext = base + rows
    uext[base:ext, :] = u
    pos1 = (lax.broadcasted_iota(I32, (rows, 1), 0) + row0 + 1).astype(F32)
    lvl_of = {1: lvl[0:1], 2: lvl[1:3], 3: lvl[3:5]}
    for gi, w in enumerate(POOL_WINDOWS):
        cols = slice(gi * POOL_GC, (gi + 1) * POOL_GC)
        src, src_cols = uext, cols
        for level in range(1, gi + 2):
            sft = 1 << (level - 1)
            if level <= gi:
                dst = lvl_of[gi][(level - 1) % 2]
                dst[POOL_PAD:ext, :] = src[POOL_PAD:ext, src_cols] + src[POOL_PAD - sft:ext - sft, src_cols]
                src, src_cols = dst, slice(None)
            else:
                wsum = src[base:ext, src_cols] + src[base - sft:ext - sft, src_cols]
        d = wsum / jnp.minimum(pos1, float(w)) - u[:, cols]
        y = _mm(d, wpool_ref[gi].astype(cdt)) * pscale_ref[:, cols]
        mix[:, ATTN_WIDTH + gi * POOL_GC:ATTN_WIDTH + (gi + 1) * POOL_GC] = y.astype(cdt)
    uext[POOL_PAD:base, :] = u[rows - POOL_HALO:, :]

    x1_ref[...] = x + _mm(mix[...], wout_c[...])


def _mixer_prompt(x_full, row_offset, rows, cdt, sinks, tabs, g_attn, w_in, qn, kn, hmean, w_pool, pscale, w_out,
                  x1_into=None):
    t = x_full.shape[0] - row_offset
    assert t % rows == 0 and row_offset % rows == 0 and rows % WINDOW == 0 and rows >= POOL_HALO
    blk0, n_steps = row_offset // rows, t // rows
    const = lambda shape: pl.BlockSpec(shape, lambda i, *_: (0,) * len(shape))
    row_blk = lambda width: pl.BlockSpec((rows, width), lambda i, *_: (blk0 + i, 0))
    aliased = x1_into is not None
    operands = (sinks, x_full, *tabs, g_attn, w_in, qn, kn, hmean, w_pool, pscale, w_out)
    if aliased:
        assert x1_into.shape == x_full.shape
        x1_spec = pl.BlockSpec((rows, D_MODEL), lambda i, *_: (blk0 + n_steps - 1, 0))
        x1_shape = x1_into.shape
        extra_specs, extra_args, aliases = [pl.BlockSpec(memory_space=pl.ANY)], [x1_into], {len(operands): 0}
    else:
        x1_spec = pl.BlockSpec((rows, D_MODEL), lambda i, *_: (i, 0))
        x1_shape = (t, D_MODEL)
        extra_specs, extra_args, aliases = [], [], {}
    grid_spec = pltpu.PrefetchScalarGridSpec(
        num_scalar_prefetch=1,
        grid=(n_steps,),
        in_specs=[row_blk(D_MODEL), row_blk(LANES), row_blk(LANES), row_blk(LANES),
                  const((1, D_MODEL)), const((D_MODEL, IN_WIDTH)), const((1, LANES)), const((1, LANES)),
                  const((LANES, LANES)), const((len(POOL_WINDOWS), POOL_GC, POOL_GC)), const((1, POOL_WIDTH)),
                  const((D_MODEL, D_MODEL))] + extra_specs,
        out_specs=[x1_spec, const((WINDOW, KV_WIDTH)), const((WINDOW, KV_WIDTH)),
                   const((POOL_HALO, POOL_WIDTH))],
        scratch_shapes=[pltpu.VMEM((WINDOW, KV_WIDTH), cdt), pltpu.VMEM((WINDOW, KV_WIDTH), cdt),
                        pltpu.VMEM((POOL_PAD + POOL_HALO + rows, POOL_WIDTH), F32), pltpu.VMEM((rows, D_MODEL), cdt),
                        pltpu.VMEM((D_MODEL, IN_WIDTH), cdt), pltpu.VMEM((D_MODEL, D_MODEL), cdt)]
        + [pltpu.VMEM((POOL_PAD + POOL_HALO + rows, POOL_GC), F32)] * 5,
    )
    return pl.pallas_call(
        functools.partial(_mixer_prompt_kernel, row_offset=row_offset, aliased=aliased),
        grid_spec=grid_spec,
        out_shape=[jax.ShapeDtypeStruct(x1_shape, F32), jax.ShapeDtypeStruct((WINDOW, KV_WIDTH), F32),
                   jax.ShapeDtypeStruct((WINDOW, KV_WIDTH), F32), jax.ShapeDtypeStruct((POOL_HALO, POOL_WIDTH), F32)],
        input_output_aliases=aliases,
        compiler_params=pltpu.CompilerParams(dimension_semantics=("arbitrary",), vmem_limit_bytes=VMEM_LIMIT),
        name="mixer_prompt",
    )(*operands, *extra_args)


def _mixer_sample_kernel(x_ref, ck_ref, cv_ref, st_ref, cos_ref, sa_ref, sb_ref, sink8_ref, g_ref, win_ref, qn_ref,
                         kn_ref, hm_ref, wpool_ref, pscale_ref, wout_ref,
                         x1_ref, nk_ref, nv_ref, nst_ref, win_c, wout_c, *, pos):
    nb = x_ref.shape[0]
    wb = ck_ref.shape[1]

    @pl.when(pl.program_id(0) == 0)
    def _():
        _prep_mixer_weights(win_ref, wout_ref, win_c, wout_c)

    x = x_ref[...]
    h = _rms(x, g_ref[...]).astype(BF16)
    proj = jnp.dot(h, win_c[...], preferred_element_type=F32)
    cos, sin_a, sin_b = cos_ref[...], sa_ref[...], sb_ref[...]
    hmean = hm_ref[...].astype(BF16)
    k = _head_norm_rope(proj[:, ATTN_WIDTH:ATTN_WIDTH + KV_WIDTH], hmean, kn_ref[...], cos, sin_a, sin_b)
    v = proj[:, ATTN_WIDTH + KV_WIDTH:ATTN_WIDTH + 2 * KV_WIDTH]
    u = proj[:, ATTN_WIDTH + 2 * KV_WIDTH:]

    nk_ref[:, 0:wb - 1, :] = ck_ref[:, 1:wb, :]
    nv_ref[:, 0:wb - 1, :] = cv_ref[:, 1:wb, :]
    for b in range(nb):
        nk_ref[b, wb - 1:wb, :] = k[b:b + 1, :]
        nv_ref[b, wb - 1:wb, :] = v[b:b + 1, :]

    r8 = lax.broadcasted_iota(I32, (nb * 8, LANES), 0)
    lane8 = lax.broadcasted_iota(I32, (nb * 8, LANES), 1)
    keep = (lane8 < HEAD_DIM) == (r8 % 2 == 0)
    rep = (lax.broadcasted_iota(I32, (nb * 8, nb), 0) // 8 == lax.broadcasted_iota(I32, (nb * 8, nb), 1)).astype(BF16)
    q8 = jnp.zeros((nb * 8, LANES), F32)
    scale = HEAD_DIM ** -0.5
    for j in range(Q_TILES):
        qt = _head_norm_rope(proj[:, j * LANES:(j + 1) * LANES], hmean, qn_ref[...], cos, sin_a, sin_b) * scale
        qrep = jnp.dot(rep, qt.astype(BF16), preferred_element_type=F32)
        q8 = jnp.where(keep & ((r8 % 8) // 2 == j), qrep, q8)
    q8 = q8.astype(BF16)

    sink = jnp.concatenate([sink8_ref[:, 0:1]] * nb, axis=0)
    ones_bf = jnp.ones((wb, LANES), BF16)
    assert pos >= wb - 1 and wb <= WINDOW
    s = jnp.concatenate([_mm(q8[b * 8:(b + 1) * 8], nk_ref[b].astype(BF16), nt=True) for b in range(nb)], axis=0)
    m = jnp.maximum(jnp.max(s, axis=-1, keepdims=True), sink)
    e = jnp.exp(s - m).astype(BF16)
    den = _mm(e, ones_bf) + jnp.exp(sink - m)
    o = jnp.concatenate([_mm(e[b * 8:(b + 1) * 8], nv_ref[b].astype(BF16)) for b in range(nb)], axis=0) / den
    o8m = jnp.where(keep, o, 0.0).astype(BF16)

    a_tiles = []
    sel_r = lax.broadcasted_iota(I32, (nb, nb * 8), 1)
    sel_b = lax.broadcasted_iota(I32, (nb, nb * 8), 0)
    for j in range(Q_TILES):
        sel = ((sel_r // 8 == sel_b) & ((sel_r % 8) // 2 == j)).astype(BF16)
        a_tiles.append(jnp.dot(sel, o8m, preferred_element_type=F32))

    z_tiles = []
    for gi, w in enumerate(POOL_WINDOWS):
        cols = slice(gi * POOL_GC, (gi + 1) * POOL_GC)
        wsum = u[:, cols]
        for sft in range(1, w):
            wsum = wsum + st_ref[POOL_PREFIX - sft, :, cols]
        d = wsum / float(min(pos + 1, w)) - u[:, cols]
        z_tiles.append(jnp.dot(d.astype(BF16), wpool_ref[gi].astype(BF16), preferred_element_type=F32)
                       * pscale_ref[:, cols])
    nst_ref[0:POOL_PREFIX - 1] = st_ref[1:POOL_PREFIX]
    nst_ref[POOL_PREFIX - 1] = u

    mixv = jnp.concatenate(a_tiles + z_tiles, axis=1).astype(BF16)
    x1_ref[...] = x + jnp.dot(mixv, wout_c[...], preferred_element_type=F32)


def _mixer_sample(x, ck, cv, st, pos, tabs, sink8, g_attn, w_in, qn, kn, hmean, w_pool, pscale, w_out):
    n, wb = ck.shape[0], ck.shape[1]
    nb = SAMPLE_CHUNK
    assert n % nb == 0
    const = lambda shape: pl.BlockSpec(shape, lambda i: (0,) * len(shape))
    cache_blk = pl.BlockSpec((nb, wb, KV_WIDTH), lambda i: (i, 0, 0))
    st_blk = pl.BlockSpec((POOL_PREFIX, nb, POOL_WIDTH), lambda i: (0, i, 0))
    x_blk = pl.BlockSpec((nb, D_MODEL), lambda i: (i, 0))
    return pl.pallas_call(
        functools.partial(_mixer_sample_kernel, pos=pos),
        grid=(n // nb,),
        in_specs=[x_blk, cache_blk, cache_blk, st_blk, const((1, LANES)), const((1, LANES)), const((1, LANES)),
                  const((8, LANES)), const((1, D_MODEL)), const((D_MODEL, IN_WIDTH)), const((1, LANES)),
                  const((1, LANES)), const((LANES, LANES)), const((len(POOL_WINDOWS), POOL_GC, POOL_GC)),
                  const((1, POOL_WIDTH)), const((D_MODEL, D_MODEL))],
        out_specs=[x_blk, cache_blk, cache_blk, st_blk],
        out_shape=[jax.ShapeDtypeStruct((n, D_MODEL), F32), jax.ShapeDtypeStruct(ck.shape, F32),
                   jax.ShapeDtypeStruct(cv.shape, F32), jax.ShapeDtypeStruct(st.shape, F32)],
        scratch_shapes=[pltpu.VMEM((D_MODEL, IN_WIDTH), BF16), pltpu.VMEM((D_MODEL, D_MODEL), BF16)],
        compiler_params=pltpu.CompilerParams(dimension_semantics=("arbitrary",), vmem_limit_bytes=VMEM_LIMIT),
        name="mixer_sample",
    )(x, ck, cv, st, *tabs, sink8, g_attn, w_in, qn, kn, hmean, w_pool, pscale, w_out)


def _block_plan(cnt, bm, n_lanes):
    e_sub = lax.broadcasted_iota(I32, (N_EXPERTS, LANES), 0)
    e_lane = lax.broadcasted_iota(I32, (N_EXPERTS, LANES), 1)
    padded = jnp.floor((cnt + (bm - 1)) / bm) * bm
    padded_lane = jnp.sum(jnp.where(e_sub == e_lane, padded, 0.0), axis=0, keepdims=True)
    pad_end = jnp.sum(jnp.where(e_lane <= e_sub, padded_lane, 0.0), axis=1, keepdims=True)
    pad_start = pad_end - padded
    blk_start = lax.broadcasted_iota(I32, (N_EXPERTS, n_lanes), 1).astype(F32) * bm
    blk_e = jnp.minimum(jnp.sum((pad_end <= blk_start).astype(F32), axis=0, keepdims=True), N_EXPERTS - 1.0)
    mine = lax.broadcasted_iota(I32, (N_EXPERTS, n_lanes), 0).astype(F32) == blk_e
    last = jnp.sum(jnp.where(mine, pad_start + cnt, 0.0), axis=0, keepdims=True)
    blk_valid = jnp.clip(last - blk_start[0:1], 0.0, float(bm))
    e_blk = lax.broadcasted_iota(I32, (N_EXPERTS, n_lanes), 0).astype(F32)
    later = jnp.min(jnp.where((e_blk > blk_e) & (cnt > 0.0), e_blk, float(N_EXPERTS)), axis=0, keepdims=True)
    blk_next = jnp.where(later < N_EXPERTS, later, -1.0)
    return pad_start, blk_e, blk_valid, blk_next


def _route_kernel(x1_ref, g_ref, wr_ref, br_ref, cnt_in_ref, hf_ref, idx_ref, gate_ref, pos_ref, cnt_ref,
                  pstart_ref, blke_ref, blkv_ref, blkn_ref, counts, *, bm):
    i = pl.program_id(0)
    rows = x1_ref.shape[0]

    @pl.when(i == 0)
    def _():
        counts[...] = cnt_in_ref[...]

    h = _rms(x1_ref[...], g_ref[...])
    hf_ref[...] = _pack_halves(h)

    logits = lax.dot_general(wr_ref[...], h, (((1,), (1,)), ((), ())), preferred_element_type=F32,
                             precision=lax.Precision.HIGHEST) + br_ref[...]
    eid = lax.broadcasted_iota(I32, (N_EXPERTS, rows), 0).astype(F32)
    work = logits
    vals, hots = [], []
    for kk in range(TOP_K):
        m = jnp.max(work, axis=0, keepdims=True)
        first = jnp.min(jnp.where(work == m, eid, float(N_EXPERTS)), axis=0, keepdims=True)
        hot = eid == first
        work = jnp.where(hot, -jnp.inf, work)
        vals.append(m)
        hots.append(hot)
        idx_ref[kk:kk + 1, :] = first.astype(I32)
    es = [jnp.exp(vv - vals[0]) for vv in vals]
    den = es[0] + es[1] + es[2] + es[3]
    for kk in range(TOP_K):
        gate_ref[kk:kk + 1, :] = es[kk] / den

    chosen = hots[0] | hots[1] | hots[2] | hots[3]
    before = (lax.broadcasted_iota(I32, (rows, rows), 0) < lax.broadcasted_iota(I32, (rows, rows), 1)).astype(BF16)
    rank = jnp.dot(chosen.astype(BF16), before, preferred_element_type=F32) + counts[...]
    for kk in range(TOP_K):
        pos_ref[kk:kk + 1, :] = jnp.sum(jnp.where(hots[kk], rank, 0.0), axis=0, keepdims=True).astype(I32)
    counts[...] = counts[...] + jnp.sum(chosen.astype(F32), axis=1, keepdims=True)
    cnt_ref[...] = counts[...]

    @pl.when(i == pl.num_programs(0) - 1)
    def _():
        pad_start, blk_e, blk_valid, blk_next = _block_plan(counts[...], bm, blke_ref.shape[1])
        pstart_ref[...] = pad_start.astype(I32)
        blke_ref[...] = blk_e.astype(I32)
        blkv_ref[...] = blk_valid.astype(I32)
        blkn_ref[...] = blk_next.astype(I32)


def _route(x1, g_ffn, wr_t, br, cnt_in, rows, bm, n_blocks):
    n = x1.shape[0]
    assert n % rows == 0
    nb_lanes = -(-n_blocks // LANES) * LANES
    const = lambda shape: pl.BlockSpec(shape, lambda i: (0,) * len(shape))
    tok_blk = pl.BlockSpec((TOP_K, rows), lambda i: (0, i))
    return pl.pallas_call(
        functools.partial(_route_kernel, bm=bm),
        grid=(n // rows,),
        in_specs=[pl.BlockSpec((rows, D_MODEL), lambda i: (i, 0)), const((1, D_MODEL)), const((N_EXPERTS, D_MODEL)),
                  const((N_EXPERTS, 1)), const((N_EXPERTS, 1))],
        out_specs=[pl.BlockSpec((rows, D_MODEL // 2), lambda i: (i, 0)), tok_blk, tok_blk, tok_blk,
                   const((N_EXPERTS, 1)), const((N_EXPERTS, 1)), const((1, nb_lanes)), const((1, nb_lanes)),
                   const((1, nb_lanes))],
        out_shape=[jax.ShapeDtypeStruct((n, D_MODEL // 2), U32), jax.ShapeDtypeStruct((TOP_K, n), I32),
                   jax.ShapeDtypeStruct((TOP_K, n), F32), jax.ShapeDtypeStruct((TOP_K, n), I32),
                   jax.ShapeDtypeStruct((N_EXPERTS, 1), F32), jax.ShapeDtypeStruct((N_EXPERTS, 1), I32),
                   jax.ShapeDtypeStruct((1, nb_lanes), I32), jax.ShapeDtypeStruct((1, nb_lanes), I32),
                   jax.ShapeDtypeStruct((1, nb_lanes), I32)],
        scratch_shapes=[pltpu.VMEM((N_EXPERTS, 1), F32)],
        compiler_params=pltpu.CompilerParams(dimension_semantics=("arbitrary",), vmem_limit_bytes=VMEM_LIMIT),
        name="route",
    )(x1, g_ffn, wr_t, br, cnt_in)


def _sc_mesh():
    return plsc.VectorSubcoreMesh(core_axis_name="core", subcore_axis_name="subcore")


def _sc_worker_id():
    return lax.axis_index("core") * SC_SUBCORES + lax.axis_index("subcore")


def _scatter_rows(xa, xb, dest, n_rows):
    ch = SC_CHUNK
    na, w = xa.shape
    n = na + xb.shape[0]
    nk = dest.shape[0]
    assert na % ch == 0 and n % ch == 0 and dest.shape[1] == n and xb.shape[1] == w and xa.dtype == xb.dtype
    n_chunks = n // ch
    dest_c = dest.reshape(nk, n_chunks, ch).transpose(1, 0, 2).reshape(-1)
    dma = pltpu.SemaphoreType.DMA

    @pl.kernel(out_type=jax.ShapeDtypeStruct((n_rows, w), xa.dtype), mesh=_sc_mesh(),
               scratch_types=[pltpu.VMEM((nk * ch,), I32), pltpu.VMEM((ch, w), xa.dtype), dma] + [dma] * nk)
    def scatter_kernel(xa_hbm, xb_hbm, d_hbm, o_hbm, idx_v, buf, sem_i, *sem_s):
        wid = _sc_worker_id()

        @pl.loop(0, -(-n_chunks // SC_WORKERS))
        def _(j):
            c = j * SC_WORKERS + wid

            @pl.when(c < n_chunks)
            def _():
                load_idx = pltpu.make_async_copy(d_hbm.at[pl.ds(c * (nk * ch), nk * ch)], idx_v, sem_i)
                load_idx.start()

                @pl.when(c < na // ch)
                def _():
                    pltpu.sync_copy(xa_hbm.at[pl.ds(c * ch, ch)], buf)

                @pl.when(c >= na // ch)
                def _():
                    pltpu.sync_copy(xb_hbm.at[pl.ds(c * ch - na, ch)], buf)

                load_idx.wait()
                puts = [pltpu.make_async_copy(buf, o_hbm.at[idx_v.at[pl.ds(kk * ch, ch)]], sem_s[kk])
                        for kk in range(nk)]
                for cp in puts:
                    cp.start()
                for cp in puts:
                    cp.wait()

    return scatter_kernel(xa, xb, dest_c)


def _gather_rows(src, idx):
    ch = SC_CHUNK
    m = idx.shape[0]
    w = src.shape[1]
    per = m // SC_WORKERS
    n_pairs = per // (2 * ch)
    assert m % SC_WORKERS == 0 and per % (2 * ch) == 0
    dma = pltpu.SemaphoreType.DMA

    @pl.kernel(out_type=jax.ShapeDtypeStruct((m, w), src.dtype), mesh=_sc_mesh(),
               scratch_types=[pltpu.VMEM((per,), I32), pltpu.VMEM((ch, w), src.dtype), pltpu.VMEM((ch, w), src.dtype),
                              dma, dma, dma, dma])
    def gather_kernel(s_hbm, i_hbm, o_hbm, idx_v, buf_a, buf_b, sem_ga, sem_gb, sem_wa, sem_wb):
        base = _sc_worker_id() * per
        pltpu.sync_copy(i_hbm.at[pl.ds(base, per)], idx_v)

        def fetch(j, buf, sem):
            return pltpu.make_async_copy(s_hbm.at[idx_v.at[pl.ds(j * ch, ch)]], buf, sem)

        def put(j, buf, sem):
            return pltpu.make_async_copy(buf, o_hbm.at[pl.ds(base + j * ch, ch)], sem)

        fetch(0, buf_a, sem_ga).start()

        @pl.loop(0, n_pairs)
        def _(p):
            j0 = 2 * p
            j1 = j0 + 1

            @pl.when(p > 0)
            def _():
                put(j1 - 2, buf_b, sem_wb).wait()

            fetch(j1, buf_b, sem_gb).start()
            fetch(j0, buf_a, sem_ga).wait()
            put(j0, buf_a, sem_wa).start()
            fetch(j1, buf_b, sem_gb).wait()
            put(j1, buf_b, sem_wb).start()
            put(j0, buf_a, sem_wa).wait()

            @pl.when(p + 1 < n_pairs)
            def _():
                fetch(j0 + 2, buf_a, sem_ga).start()

        put(2 * n_pairs - 1, buf_b, sem_wb).wait()

    return gather_kernel(src, idx)


def _moe_kernel(blk_e_ref, blk_valid_ref, blk_next_ref, xs_ref, wgu_hbm, bgu_ref, wd_hbm, bd_ref, y_ref,
                wgu_f32, wd_f32, wgu_bf, wd_bf, sems, *, e0):
    i = pl.program_id(0)
    e = blk_e_ref[i]
    n_valid = blk_valid_ref[i]
    used = n_valid > 0
    new_expert = (i == 0) | (blk_e_ref[jnp.maximum(i - 1, 0)] != e)

    def weight_copies(expert):
        return (pltpu.make_async_copy(wgu_hbm.at[e0 + expert], wgu_f32, sems.at[0]),
                pltpu.make_async_copy(wd_hbm.at[e0 + expert], wd_f32, sems.at[1]))

    @pl.when(used & (i == 0))
    def _():
        for cp in weight_copies(e):
            cp.start()

    @pl.when(used & new_expert)
    def _():
        for cp in weight_copies(e):
            cp.wait()
        wgu_bf[...] = wgu_f32[...].astype(BF16)
        wd_bf[...] = wd_f32[...].astype(BF16)
        nxt = blk_next_ref[i]

        @pl.when(nxt >= 0)
        def _():
            for cp in weight_copies(nxt):
                cp.start(priority=1)

    half = xs_ref.shape[0] // 2

    def expert_mlp(n_rows):
        valid = lax.broadcasted_iota(I32, (n_rows, xs_ref.shape[1]), 0) < n_valid
        xb = _unpack_halves(jnp.where(valid, xs_ref[0:n_rows, :], jnp.uint32(0))).astype(BF16)
        gu = jnp.dot(xb, wgu_bf[...], preferred_element_type=F32) + bgu_ref[0]
        g = jnp.minimum(gu[:, :D_FF], SWIGLU_LIMIT)
        up = jnp.clip(gu[:, D_FF:], -SWIGLU_LIMIT, SWIGLU_LIMIT)
        act = (up + 1.0) * (g * jax.nn.sigmoid(SWIGLU_ALPHA * g))
        y = jnp.dot(act.astype(BF16), wd_bf[...], preferred_element_type=F32) + bd_ref[0]
        y_ref[0:n_rows, :] = _pack_halves(y)

    @pl.when(n_valid > half)
    def _():
        expert_mlp(2 * half)

    @pl.when(used & (n_valid <= half))
    def _():
        expert_mlp(half)
        y_ref[half:, :] = jnp.zeros((half, y_ref.shape[1]), y_ref.dtype)

    @pl.when(jnp.logical_not(used))
    def _():
        y_ref[...] = jnp.zeros_like(y_ref)


def _moe(xs, blk_e, blk_valid, blk_next, layer, w_gu, b_gu, w_d, b_d):
    n_rows = xs.shape[0]
    bm = MOE_BM
    assert n_rows % bm == 0
    e0 = layer * N_EXPERTS
    n_we = w_gu.shape[0] * w_gu.shape[1]
    any_spec = pl.BlockSpec(memory_space=pl.ANY)
    grid_spec = pltpu.PrefetchScalarGridSpec(
        num_scalar_prefetch=3,
        grid=(n_rows // bm,),
        in_specs=[pl.BlockSpec((bm, D_MODEL // 2), lambda i, be, bv, bn: (i, 0)),
                  any_spec,
                  pl.BlockSpec((1, 1, 2 * D_FF), lambda i, be, bv, bn: (e0 + be[i], 0, 0)),
                  any_spec,
                  pl.BlockSpec((1, 1, D_MODEL), lambda i, be, bv, bn: (e0 + be[i], 0, 0))],
        out_specs=pl.BlockSpec((bm, D_MODEL // 2), lambda i, be, bv, bn: (i, 0)),
        scratch_shapes=[pltpu.VMEM((D_MODEL, 2 * D_FF), F32), pltpu.VMEM((D_FF, D_MODEL), F32),
                        pltpu.VMEM((D_MODEL, 2 * D_FF), BF16), pltpu.VMEM((D_FF, D_MODEL), BF16),
                        pltpu.SemaphoreType.DMA((2,))],
    )
    return pl.pallas_call(
        functools.partial(_moe_kernel, e0=e0),
        grid_spec=grid_spec,
        out_shape=jax.ShapeDtypeStruct((n_rows, D_MODEL // 2), U32),
        compiler_params=pltpu.CompilerParams(dimension_semantics=("arbitrary",), vmem_limit_bytes=VMEM_LIMIT),
        name="moe_experts",
    )(blk_e, blk_valid, blk_next, xs, w_gu.reshape(n_we, D_MODEL, 2 * D_FF), b_gu.reshape(n_we, 1, 2 * D_FF),
      w_d.reshape(n_we, D_FF, D_MODEL), b_d.reshape(n_we, 1, D_MODEL))


def _ple_kernel(x1_ref, y0_ref, y1_ref, y2_ref, y3_ref, gates_ref, p_ref, g_ref, wg_ref, wp_ref, *rest):
    out_ref, wg_bf, wp_bf = rest[-3:]

    @pl.when(pl.program_id(0) == 0)
    def _():
        wg_bf[...] = wg_ref[...].astype(BF16)
        wp_bf[...] = wp_ref[...].astype(BF16)

    x2 = x1_ref[...]
    gates = gates_ref[...]
    for kk, y_ref in enumerate((y0_ref, y1_ref, y2_ref, y3_ref)):
        x2 = x2 + _unpack_halves(y_ref[...]) * gates[:, kk:kk + 1]
    hp = _rms(x2, g_ref[...]).astype(BF16)
    gate = 0.5 * jnp.tanh(0.5 * jnp.dot(hp, wg_bf[...], preferred_element_type=F32)) + 0.5
    pp = jnp.dot(p_ref[...].astype(BF16), wp_bf[...], preferred_element_type=F32)
    out_ref[...] = x2 + gate * pp


def _ple(x1, tok0, n, y_tok, y0, gates_t, p_all, p0, g_ple, w_gate, w_proj, rows, out_into=None):
    assert n % rows == 0 and tok0 % rows == 0 and y0 % rows == 0 and p0 % rows == 0
    const = lambda shape: pl.BlockSpec(shape, lambda i: (0,) * len(shape))
    tok_blk = lambda width: pl.BlockSpec((rows, width), lambda i: (tok0 // rows + i, 0))
    y_blk = lambda kk: pl.BlockSpec((rows, D_MODEL // 2), lambda i: ((y0 + kk * n) // rows + i, 0))
    operands = (x1, y_tok, y_tok, y_tok, y_tok, gates_t, p_all, g_ple, w_gate, w_proj)
    extra_specs, extra_args, aliases = [], [], {}
    if out_into is not None:
        assert out_into.shape == x1.shape
        extra_specs, extra_args, aliases = [pl.BlockSpec(memory_space=pl.ANY)], [out_into], {len(operands): 0}
    return pl.pallas_call(
        _ple_kernel,
        grid=(n // rows,),
        in_specs=[tok_blk(D_MODEL), y_blk(0), y_blk(1), y_blk(2), y_blk(3), tok_blk(TOP_K),
                  pl.BlockSpec((rows, PLE_DIM), lambda i: (p0 // rows + i, 0)),
                  const((1, D_MODEL)), const((D_MODEL, D_MODEL)), const((PLE_DIM, D_MODEL))] + extra_specs,
        out_specs=tok_blk(D_MODEL),
        out_shape=jax.ShapeDtypeStruct(x1.shape, F32),
        scratch_shapes=[pltpu.VMEM((D_MODEL, D_MODEL), BF16), pltpu.VMEM((PLE_DIM, D_MODEL), BF16)],
        input_output_aliases=aliases,
        compiler_params=pltpu.CompilerParams(dimension_semantics=("arbitrary",), vmem_limit_bytes=VMEM_LIMIT),
        name="combine_ple",
    )(*operands, *extra_args)


def _rope_tables_const(seq):
    half = ROT_DIM // 2
    d = np.arange(LANES) % HEAD_DIM
    inv = (np.float64(ROPE_THETA) ** (-np.arange(half, dtype=np.float64) / half)).astype(np.float32)
    inv_lane = np.where(d < ROT_DIM, inv[d % half], np.float32(0.0)).astype(np.float32)
    ang = np.arange(seq, dtype=np.float32)[:, None] * inv_lane[None, :]
    cos = np.cos(ang.astype(np.float64)).astype(np.float32)
    sin = np.sin(ang.astype(np.float64)).astype(np.float32)
    sin_a = np.where((d >= half) & (d < ROT_DIM), sin, np.float32(0.0))
    sin_b = np.where(d < half, -sin, np.float32(0.0))
    return tuple(jnp.asarray(a) for a in (cos, sin_a, sin_b))


def _rope_tables(pos):
    half = ROT_DIM // 2
    d = np.arange(LANES) % HEAD_DIM
    inv = ROPE_THETA ** (-jnp.arange(half, dtype=F32) / half)
    inv_lane = jnp.where(d < ROT_DIM, inv[d % half], 0.0)
    ang = pos.astype(F32)[:, None] * inv_lane[None, :]
    cos, sin = jnp.cos(ang), jnp.sin(ang)
    sin_a = jnp.where((d >= half) & (d < ROT_DIM), sin, 0.0)
    sin_b = jnp.where(d < half, -sin, 0.0)
    return cos, sin_a, sin_b


def _layer(layer, xp, xs, ck, cv, st, p_prompt_all, p_sample_all, past_len, refine_tail, tabs_p, tabs_s, finish_state,
           norm_attn, w_in, q_norm, k_norm, attn_sinks, w_pool, pool_scale, w_out,
           norm_ffn, w_router, b_router, w_gate_up_all, b_gate_up_all, w_down_all, b_down_all,
           norm_ple, w_ple_gate, w_ple_proj):
    t, ns = xp.shape[0], xs.shape[0]
    g_attn = norm_attn.reshape(1, D_MODEL)
    qn = jnp.tile(q_norm, 2).reshape(1, LANES)
    kn = jnp.tile(k_norm, 2).reshape(1, LANES)
    lane = np.arange(LANES)
    hmean = jnp.asarray((lane[:, None] // HEAD_DIM == lane[None, :] // HEAD_DIM) / HEAD_DIM, F32)
    pscale = pool_scale.reshape(1, POOL_WIDTH)
    sink8 = jnp.broadcast_to(attn_sinks.reshape(2, Q_TILES).T.reshape(8, 1), (8, LANES))

    mix_args = (g_attn, w_in, qn, kn, hmean, w_pool, pscale, w_out)
    x1p, nk_p, nv_p, nu_p = _mixer_prompt(xp, 0, MIX_ROWS, BF16, attn_sinks, tabs_p, *mix_args)
    if refine_tail:
        x1p, nk_p, nv_p, nu_p = _mixer_prompt(xp, t - 2 * MIX_TAIL_ROWS, MIX_TAIL_ROWS, F32, attn_sinks, tabs_p,
                                              *mix_args, x1_into=x1p)
    st_t = jnp.transpose(st, (1, 0, 2))
    x1s, nk_s, nv_s, nst_t = _mixer_sample(xs, ck.reshape(ns, -1, KV_WIDTH), cv.reshape(ns, -1, KV_WIDTH), st_t,
                                           past_len, tabs_s, sink8, *mix_args)

    g_ffn = norm_ffn.reshape(1, D_MODEL)
    wr_t = w_router.T
    br = b_router.reshape(N_EXPERTS, 1)
    n_tok = t + ns
    bm = MOE_BM
    n_blocks = -(-(n_tok * TOP_K + N_EXPERTS * (bm - 1)) // bm)
    hf_p, idx_p, gate_p, pos_p, cnt_p = _route(x1p, g_ffn, wr_t, br, jnp.zeros((N_EXPERTS, 1), F32),
                                               ROUTE_ROWS, bm, n_blocks)[:5]
    hf_s, idx_s, gate_s, pos_s, _, pad_start, blk_e, blk_valid, blk_next = _route(x1s, g_ffn, wr_t, br, cnt_p, ns,
                                                                                  bm, n_blocks)
    idx = jnp.concatenate([idx_p, idx_s], axis=1)
    pos = jnp.concatenate([pos_p, pos_s], axis=1)
    blk_e, blk_valid, blk_next = (a[0, :n_blocks] for a in (blk_e, blk_valid, blk_next))
    state = finish_state(nk_p, nv_p, nu_p[POOL_HALO - POOL_PREFIX:], nk_s, nv_s, jnp.transpose(nst_t, (1, 0, 2)))
    blk_e, state = lax.optimization_barrier((blk_e, state))
    start_of = jnp.sum(jnp.where(idx[None] == jnp.arange(N_EXPERTS, dtype=I32)[:, None, None],
                                 pad_start.reshape(N_EXPERTS, 1, 1), 0), axis=0)
    dest = start_of + pos

    xs_rows = _scatter_rows(hf_p, hf_s, dest, n_blocks * bm)
    y = _moe(xs_rows, blk_e, blk_valid, blk_next, layer, w_gate_up_all, b_gate_up_all, w_down_all, b_down_all)

    tq = t // COMBINE_PARTS
    unit = SC_WORKERS * SC_CHUNK * 2
    assert t % COMBINE_PARTS == 0 and (TOP_K * tq) % unit == 0 and tq % PLE_ROWS == 0
    g_ple = norm_ple.reshape(1, D_MODEL)
    gates_p = gate_p.T
    yp = None
    for part in range(COMBINE_PARTS):
        back = dest[:, part * tq:(part + 1) * tq].reshape(-1)
        if part == COMBINE_PARTS - 1:
            back = jnp.concatenate([back, dest[:, t:].reshape(-1)])
            back = jnp.concatenate([back, jnp.arange(-back.shape[0] % unit, dtype=I32)])
        y_part = _gather_rows(y, back)
        yp = _ple(x1p, part * tq, tq, y_part, 0, gates_p, p_prompt_all, layer * t + part * tq, g_ple,
                  w_ple_gate, w_ple_proj, PLE_ROWS, out_into=yp)
    ys = _ple(x1s, 0, ns, y_part, TOP_K * tq, gate_s.T, p_sample_all, layer * ns, g_ple, w_ple_gate, w_ple_proj, ns)
    return yp, ys, state


def kernel(x_prompt, x_sample, cache_k, cache_v, state_pool, p_prompt, p_sample, norm_attn, w_in, q_norm, k_norm,
           attn_sinks, w_pool, pool_scale, w_out, norm_ffn, w_router, b_router, w_gate_up, b_gate_up, w_down, b_down,
           norm_ple, w_ple_gate, w_ple_proj):
    depth = norm_attn.shape[0]
    batch, seq, d = x_prompt.shape
    ns, dec_seq, _ = x_sample.shape
    wb = cache_k.shape[2]
    assert batch == 1 and dec_seq == 1 and d == D_MODEL and wb == WINDOW
    assert cache_k.shape[3:] == (N_KV_HEADS, HEAD_DIM) and state_pool.shape[2:] == (POOL_PREFIX, POOL_WIDTH)
    past_len = PAST_LEN
    yp = x_prompt.reshape(seq, d)
    ys = x_sample.reshape(ns, d)
    p_prompt_all = p_prompt.reshape(depth * seq, PLE_DIM)
    p_sample_all = p_sample.reshape(depth * ns, PLE_DIM)
    tabs_p = _rope_tables_const(seq)
    tabs_s = _rope_tables(jnp.full((1,), past_len))
    kv_p, kv_s = (1, WINDOW, N_KV_HEADS, HEAD_DIM), (ns, wb, N_KV_HEADS, HEAD_DIM)
    state_shapes = (kv_p, kv_p, (1, POOL_PREFIX, POOL_WIDTH), kv_s, kv_s, (ns, POOL_PREFIX, POOL_WIDTH))
    per_layer = []

    def shaped(*state):
        return tuple(a.reshape(shp) for a, shp in zip(state, state_shapes))

    def stacked(*state):
        prevs = list(zip(*per_layer)) or [()] * len(state_shapes)
        return tuple(jnp.stack(list(prev) + [cur]) for prev, cur in zip(prevs, shaped(*state)))

    for i in range(depth):
        yp, ys, state = _layer(i, yp, ys, cache_k[i], cache_v[i], state_pool[i], p_prompt_all, p_sample_all, past_len,
                               i < depth - 1, tabs_p, tabs_s, shaped if i < depth - 1 else stacked,
                               norm_attn[i], w_in[i], q_norm[i], k_norm[i], attn_sinks[i], w_pool[i], pool_scale[i],
                               w_out[i], norm_ffn[i], w_router[i], b_router[i], w_gate_up, b_gate_up, w_down, b_down,
                               norm_ple[i], w_ple_gate[i], w_ple_proj[i])
        per_layer.append(state)
    return (yp.reshape(batch, seq, d), ys.reshape(ns, dec_seq, d)) + tuple(state)
```

```python
import functools

import jax
import jax.numpy as jnp
import numpy as np
from jax import lax
from jax.experimental import pallas as pl
from jax.experimental.pallas import tpu as pltpu
from jax.experimental.pallas import tpu_sc as plsc

F32 = jnp.float32
BF16 = jnp.bfloat16
U32 = jnp.uint32
I32 = jnp.int32

D_MODEL = 1024
HEAD_DIM = 64
N_HEADS = 8
N_KV_HEADS = 2
GROUP = N_HEADS // N_KV_HEADS
ATTN_WIDTH = N_HEADS * HEAD_DIM
KV_WIDTH = N_KV_HEADS * HEAD_DIM
POOL_WIDTH = 512
POOL_WINDOWS = (2, 4, 8, 16)
POOL_GC = POOL_WIDTH // len(POOL_WINDOWS)
POOL_PREFIX = max(POOL_WINDOWS) - 1
POOL_HALO = POOL_PREFIX + 1
POOL_PAD = 8
IN_WIDTH = ATTN_WIDTH + 2 * KV_WIDTH + POOL_WIDTH
WINDOW = 128
ROPE_THETA = 500000.0
ROT_DIM = HEAD_DIM // 4
N_EXPERTS = 32
TOP_K = 4
D_FF = 1024
SWIGLU_ALPHA = 1.702
SWIGLU_LIMIT = 7.0
PLE_DIM = 256
PAST_LEN = 16384
EPS = 1e-5
NEG_INF = -1e30

LANES = 128
Q_TILES = ATTN_WIDTH // LANES

MIX_ROWS = 1024
MIX_TAIL_ROWS = 128
SAMPLE_CHUNK = 16
ROUTE_ROWS = 1024
MOE_BM = 512
SC_CORES = 2
SC_SUBCORES = 16
SC_WORKERS = SC_CORES * SC_SUBCORES
SC_CHUNK = 64
PLE_ROWS = 1024
COMBINE_PARTS = 2
VMEM_LIMIT = 56 * 1024 * 1024


def _rms(x, g):
    return x * lax.rsqrt(jnp.mean(x * x, axis=-1, keepdims=True) + EPS) * g


def _pack_halves(x):
    w = x.shape[1] // 2
    lo = lax.bitcast_convert_type(x[:, :w].astype(BF16).astype(F32), U32) >> 16
    hi = lax.bitcast_convert_type(x[:, w:].astype(BF16).astype(F32), U32) & jnp.uint32(0xFFFF0000)
    return lo | hi


def _unpack_halves(packed):
    lo = lax.bitcast_convert_type(packed << 16, F32)
    hi = lax.bitcast_convert_type(packed & jnp.uint32(0xFFFF0000), F32)
    return jnp.concatenate([lo, hi], axis=1)


def _mm(a, b, nt=False):
    dims = (((1,), (1 if nt else 0,)), ((), ()))
    if b.dtype == F32:
        return lax.dot_general(a.astype(F32), b, dims, preferred_element_type=F32, precision=lax.Precision.HIGHEST)
    return lax.dot_general(a.astype(BF16), b, dims, preferred_element_type=F32)


def _head_norm_rope(t, hmean, gain, cos, sin_a, sin_b):
    t = t * lax.rsqrt(_mm(t * t, hmean) + EPS) * gain
    return t * cos + pltpu.roll(t, ROT_DIM // 2, axis=1) * sin_a + pltpu.roll(t, LANES - ROT_DIM // 2, axis=1) * sin_b


def _prep_mixer_weights(win_ref, wout_ref, win_c, wout_c):
    cdt = win_c.dtype
    for j in range(Q_TILES):
        for s in range(2):
            src = (j + Q_TILES * s) * HEAD_DIM
            dst = j * LANES + s * HEAD_DIM
            win_c[:, dst:dst + HEAD_DIM] = win_ref[:, src:src + HEAD_DIM].astype(cdt)
            wout_c[dst:dst + HEAD_DIM, :] = wout_ref[src:src + HEAD_DIM, :].astype(cdt)
    win_c[:, ATTN_WIDTH:] = win_ref[:, ATTN_WIDTH:].astype(cdt)
    wout_c[ATTN_WIDTH:, :] = wout_ref[ATTN_WIDTH:, :].astype(cdt)


def _mixer_prompt_kernel(sinks_ref, x_ref, cos_ref, sa_ref, sb_ref, g_ref, win_ref, qn_ref, kn_ref, hm_ref,
                         wpool_ref, pscale_ref, wout_ref, *rest, row_offset, aliased):
    (x1_ref, klast_ref, vlast_ref, ulast_ref, kprev, vprev, uext, mix, win_c, wout_c,
     *lvl) = rest[1:] if aliased else rest
    i = pl.program_id(0)
    rows = x_ref.shape[0]
    n_sub = rows // WINDOW
    cdt = win_c.dtype
    row0 = row_offset + i * rows

    @pl.when(i == 0)
    def _():
        _prep_mixer_weights(win_ref, wout_ref, win_c, wout_c)
        kprev[...] = jnp.zeros_like(kprev)
        vprev[...] = jnp.zeros_like(vprev)
        uext[0:POOL_PAD + POOL_HALO, :] = jnp.zeros((POOL_PAD + POOL_HALO, POOL_WIDTH), F32)
        for buf in lvl:
            buf[0:POOL_PAD, :] = jnp.zeros((POOL_PAD, POOL_GC), F32)

    x = x_ref[...]
    proj = _mm(_rms(x, g_ref[...]), win_c[...])
    cos, sin_a, sin_b = cos_ref[...], sa_ref[...], sb_ref[...]

    n_t = Q_TILES + 1
    t_all = jnp.concatenate([proj[:, j * LANES:(j + 1) * LANES] for j in range(n_t)], axis=0)
    t3 = (t_all * lax.rsqrt(_mm(t_all * t_all, hm_ref[...].astype(cdt)) + EPS)).reshape(n_t, rows, LANES)
    t3 = jnp.concatenate([t3[:Q_TILES] * (qn_ref[...] * HEAD_DIM ** -0.5), t3[Q_TILES:] * kn_ref[...]], axis=0)
    t2 = t3.reshape(n_t * rows, LANES)
    t3 = (t3 * cos + pltpu.roll(t2, ROT_DIM // 2, axis=1).reshape(n_t, rows, LANES) * sin_a
          + pltpu.roll(t2, LANES - ROT_DIM // 2, axis=1).reshape(n_t, rows, LANES) * sin_b)
    q3 = t3[:Q_TILES]
    k = t3[Q_TILES]
    v = proj[:, ATTN_WIDTH + KV_WIDTH:ATTN_WIDTH + 2 * KV_WIDTH]
    u = proj[:, ATTN_WIDTH + 2 * KV_WIDTH:]
    klast_ref[...] = k[rows - WINDOW:, :]
    vlast_ref[...] = v[rows - WINDOW:, :]
    ulast_ref[...] = u[rows - POOL_HALO:, :]
    k_c = k.astype(cdt)
    v_c = jnp.concatenate([v.astype(cdt), jnp.ones((rows, LANES), cdt)], axis=1)
    v_first = jnp.concatenate([vprev[...], jnp.ones((WINDOW, LANES), cdt)], axis=1)

    lane = lax.broadcasted_iota(I32, (WINDOW, LANES), 1)
    left = (lane < HEAD_DIM)[None]
    qi = lax.broadcasted_iota(I32, (WINDOW, 2 * WINDOW), 0)
    kj = lax.broadcasted_iota(I32, (WINDOW, 2 * WINDOW), 1)
    band = (kj - qi >= 1) & (kj - qi <= WINDOW)
    sink3 = jnp.concatenate([jnp.full((1, 1, 1), sinks_ref[j + Q_TILES * s], F32)
                             for j in range(Q_TILES) for s in range(2)], axis=0)
    n_g = 2 * Q_TILES
    fill = jnp.where(kj[0:1][None] == 0, sink3, NEG_INF)
    slot = lax.broadcasted_iota(I32, (2 * WINDOW, 2 * LANES), 0) == 0
    slot_v = slot & (lax.broadcasted_iota(I32, (2 * WINDOW, 2 * LANES), 1) < LANES)

    v_cats, masks, scores = [], [], []
    for c in range(n_sub):
        r0 = c * WINDOW
        if c == 0:
            k_cat = jnp.concatenate([kprev[...], k_c[0:WINDOW]], axis=0)
            v_cat = jnp.concatenate([v_first, v_c[0:WINDOW]], axis=0)
            masks.append(band & (kj + (row0 - WINDOW) >= 0))
        else:
            k_cat = k_c[r0 - WINDOW:r0 + WINDOW]
            v_cat = v_c[r0 - WINDOW:r0 + WINDOW]
            masks.append(band)
        v_cats.append(jnp.where(slot_v, jnp.zeros_like(v_cat), v_cat))
        q_c = q3[:, r0:r0 + WINDOW, :]
        q_all = jnp.concatenate([jnp.where(left, q_c, 0.0), jnp.where(left, 0.0, q_c)], axis=1)
        scores.append(_mm(q_all.reshape(n_g * WINDOW, LANES), k_cat, nt=True))
    kprev[...] = k_c[rows - WINDOW:]
    vprev[...] = v[rows - WINDOW:].astype(cdt)

    probs = []
    for c in range(n_sub):
        s = jnp.where(masks[c][None], scores[c].reshape(n_g, WINDOW, 2 * WINDOW), fill)
        m = jnp.max(s, axis=-1, keepdims=True)
        probs.append(jnp.exp(s - m).astype(cdt).reshape(n_g * WINDOW, 2 * WINDOW))
    applied = [_mm(probs[c], v_cats[c]) for c in range(n_sub)]
    for c in range(n_sub):
        r0 = c * WINDOW
        o = (applied[c][:, :LANES] / applied[c][:, LANES:]).reshape(Q_TILES, 2 * WINDOW, LANES)
        a = jnp.where(left, o[:, :WINDOW], o[:, WINDOW:])
        for j in range(Q_TILES):
            mix[r0:r0 + WINDOW, j * LANES:(j + 1) * LANES] = a[j].astype(cdt)

    base = POOL_PAD + POOL_HALO
    ext = base + rows
    uext[base:ext, :] = u
    pos1 = (lax.broadcasted_iota(I32, (rows, 1), 0) + row0 + 1).astype(F32)
    lvl_of = {1: lvl[0:1], 2: lvl[1:3], 3: lvl[3:5]}
    for gi, w in enumerate(POOL_WINDOWS):
        cols = slice(gi * POOL_GC, (gi + 1) * POOL_GC)
        src, src_cols = uext, cols
        for level in range(1, gi + 2):
            sft = 1 << (level - 1)
            if level <= gi:
                dst = lvl_of[gi][(level - 1) % 2]
                dst[POOL_PAD:ext, :] = src[POOL_PAD:ext, src_cols] + src[POOL_PAD - sft:ext - sft, src_cols]
                src, src_cols = dst, slice(None)
            else:
                wsum = src[base:ext, src_cols] + src[base - sft:ext - sft, src_cols]
        d = wsum / jnp.minimum(pos1, float(w)) - u[:, cols]
        y = _mm(d, wpool_ref[gi].astype(cdt)) * pscale_ref[:, cols]
        mix[:, ATTN_WIDTH + gi * POOL_GC:ATTN_WIDTH + (gi + 1) * POOL_GC] = y.astype(cdt)
    uext[POOL_PAD:base, :] = u[rows - POOL_HALO:, :]

    x1_ref[...] = x + _mm(mix[...], wout_c[...])


def _mixer_prompt(x_full, row_offset, rows, cdt, sinks, tabs, g_attn, w_in, qn, kn, hmean, w_pool, pscale, w_out,
                  x1_into=None):
    t = x_full.shape[0] - row_offset
    assert t % rows == 0 and row_offset % rows == 0 and rows % WINDOW == 0 and rows >= POOL_HALO
    blk0, n_steps = row_offset // rows, t // rows
    const = lambda shape: pl.BlockSpec(shape, lambda i, *_: (0,) * len(shape))
    row_blk = lambda width: pl.BlockSpec((rows, width), lambda i, *_: (blk0 + i, 0))
    aliased = x1_into is not None
    operands = (sinks, x_full, *tabs, g_attn, w_in, qn, kn, hmean, w_pool, pscale, w_out)
    if aliased:
        assert x1_into.shape == x_full.shape
        x1_spec = pl.BlockSpec((rows, D_MODEL), lambda i, *_: (blk0 + n_steps - 1, 0))
        x1_shape = x1_into.shape
        extra_specs, extra_args, aliases = [pl.BlockSpec(memory_space=pl.ANY)], [x1_into], {len(operands): 0}
    else:
        x1_spec = pl.BlockSpec((rows, D_MODEL), lambda i, *_: (i, 0))
        x1_shape = (t, D_MODEL)
        extra_specs, extra_args, aliases = [], [], {}
    grid_spec = pltpu.PrefetchScalarGridSpec(
        num_scalar_prefetch=1,
        grid=(n_steps,),
        in_specs=[row_blk(D_MODEL), row_blk(LANES), row_blk(LANES), row_blk(LANES),
                  const((1, D_MODEL)), const((D_MODEL, IN_WIDTH)), const((1, LANES)), const((1, LANES)),
                  const((LANES, LANES)), const((len(POOL_WINDOWS), POOL_GC, POOL_GC)), const((1, POOL_WIDTH)),
                  const((D_MODEL, D_MODEL))] + extra_specs,
        out_specs=[x1_spec, const((WINDOW, KV_WIDTH)), const((WINDOW, KV_WIDTH)),
                   const((POOL_HALO, POOL_WIDTH))],
        scratch_shapes=[pltpu.VMEM((WINDOW, KV_WIDTH), cdt), pltpu.VMEM((WINDOW, KV_WIDTH), cdt),
                        pltpu.VMEM((POOL_PAD + POOL_HALO + rows, POOL_WIDTH), F32), pltpu.VMEM((rows, D_MODEL), cdt),
                        pltpu.VMEM((D_MODEL, IN_WIDTH), cdt), pltpu.VMEM((D_MODEL, D_MODEL), cdt)]
        + [pltpu.VMEM((POOL_PAD + POOL_HALO + rows, POOL_GC), F32)] * 5,
    )
    return pl.pallas_call(
        functools.partial(_mixer_prompt_kernel, row_offset=row_offset, aliased=aliased),
        grid_spec=grid_spec,
        out_shape=[jax.ShapeDtypeStruct(x1_shape, F32), jax.ShapeDtypeStruct((WINDOW, KV_WIDTH), F32),
                   jax.ShapeDtypeStruct((WINDOW, KV_WIDTH), F32), jax.ShapeDtypeStruct((POOL_HALO, POOL_WIDTH), F32)],
        input_output_aliases=aliases,
        compiler_params=pltpu.CompilerParams(dimension_semantics=("arbitrary",), vmem_limit_bytes=VMEM_LIMIT),
        name="mixer_prompt",
    )(*operands, *extra_args)


def _mixer_sample_kernel(x_ref, ck_ref, cv_ref, st_ref, cos_ref, sa_ref, sb_ref, sink8_ref, g_ref, win_ref, qn_ref,
                         kn_ref, hm_ref, wpool_ref, pscale_ref, wout_ref,
                         x1_ref, nk_ref, nv_ref, nst_ref, win_c, wout_c, *, pos):
    nb = x_ref.shape[0]
    wb = ck_ref.shape[1]

    @pl.when(pl.program_id(0) == 0)
    def _():
        _prep_mixer_weights(win_ref, wout_ref, win_c, wout_c)

    x = x_ref[...]
    h = _rms(x, g_ref[...]).astype(BF16)
    proj = jnp.dot(h, win_c[...], preferred_element_type=F32)
    cos, sin_a, sin_b = cos_ref[...], sa_ref[...], sb_ref[...]
    hmean = hm_ref[...].astype(BF16)
    k = _head_norm_rope(proj[:, ATTN_WIDTH:ATTN_WIDTH + KV_WIDTH], hmean, kn_ref[...], cos, sin_a, sin_b)
    v = proj[:, ATTN_WIDTH + KV_WIDTH:ATTN_WIDTH + 2 * KV_WIDTH]
    u = proj[:, ATTN_WIDTH + 2 * KV_WIDTH:]

    nk_ref[:, 0:wb - 1, :] = ck_ref[:, 1:wb, :]
    nv_ref[:, 0:wb - 1, :] = cv_ref[:, 1:wb, :]
    for b in range(nb):
        nk_ref[b, wb - 1:wb, :] = k[b:b + 1, :]
        nv_ref[b, wb - 1:wb, :] = v[b:b + 1, :]

    r8 = lax.broadcasted_iota(I32, (nb * 8, LANES), 0)
    lane8 = lax.broadcasted_iota(I32, (nb * 8, LANES), 1)
    keep = (lane8 < HEAD_DIM) == (r8 % 2 == 0)
    rep = (lax.broadcasted_iota(I32, (nb * 8, nb), 0) // 8 == lax.broadcasted_iota(I32, (nb * 8, nb), 1)).astype(BF16)
    q8 = jnp.zeros((nb * 8, LANES), F32)
    scale = HEAD_DIM ** -0.5
    for j in range(Q_TILES):
        qt = _head_norm_rope(proj[:, j * LANES:(j + 1) * LANES], hmean, qn_ref[...], cos, sin_a, sin_b) * scale
        qrep = jnp.dot(rep, qt.astype(BF16), preferred_element_type=F32)
        q8 = jnp.where(keep & ((r8 % 8) // 2 == j), qrep, q8)
    q8 = q8.astype(BF16)

    sink = jnp.concatenate([sink8_ref[:, 0:1]] * nb, axis=0)
    ones_bf = jnp.ones((wb, LANES), BF16)
    assert pos >= wb - 1 and wb <= WINDOW
    s = jnp.concatenate([_mm(q8[b * 8:(b + 1) * 8], nk_ref[b].astype(BF16), nt=True) for b in range(nb)], axis=0)
    m = jnp.maximum(jnp.max(s, axis=-1, keepdims=True), sink)
    e = jnp.exp(s - m).astype(BF16)
    den = _mm(e, ones_bf) + jnp.exp(sink - m)
    o = jnp.concatenate([_mm(e[b * 8:(b + 1) * 8], nv_ref[b].astype(BF16)) for b in range(nb)], axis=0) / den
    o8m = jnp.where(keep, o, 0.0).astype(BF16)

    a_tiles = []
    sel_r = lax.broadcasted_iota(I32, (nb, nb * 8), 1)
    sel_b = lax.broadcasted_iota(I32, (nb, nb * 8), 0)
    for j in range(Q_TILES):
        sel = ((sel_r // 8 == sel_b) & ((sel_r % 8) // 2 == j)).astype(BF16)
        a_tiles.append(jnp.dot(sel, o8m, preferred_element_type=F32))

    z_tiles = []
    for gi, w in enumerate(POOL_WINDOWS):
        cols = slice(gi * POOL_GC, (gi + 1) * POOL_GC)
        wsum = u[:, cols]
        for sft in range(1, w):
            wsum = wsum + st_ref[POOL_PREFIX - sft, :, cols]
        d = wsum / float(min(pos + 1, w)) - u[:, cols]
        z_tiles.append(jnp.dot(d.astype(BF16), wpool_ref[gi].astype(BF16), preferred_element_type=F32)
                       * pscale_ref[:, cols])
    nst_ref[0:POOL_PREFIX - 1] = st_ref[1:POOL_PREFIX]
    nst_ref[POOL_PREFIX - 1] = u

    mixv = jnp.concatenate(a_tiles + z_tiles, axis=1).astype(BF16)
    x1_ref[...] = x + jnp.dot(mixv, wout_c[...], preferred_element_type=F32)


def _mixer_sample(x, ck, cv, st, pos, tabs, sink8, g_attn, w_in, qn, kn, hmean, w_pool, pscale, w_out):
    n, wb = ck.shape[0], ck.shape[1]
    nb = SAMPLE_CHUNK
    assert n % nb == 0
    const = lambda shape: pl.BlockSpec(shape, lambda i: (0,) * len(shape))
    cache_blk = pl.BlockSpec((nb, wb, KV_WIDTH), lambda i: (i, 0, 0))
    st_blk = pl.BlockSpec((POOL_PREFIX, nb, POOL_WIDTH), lambda i: (0, i, 0))
    x_blk = pl.BlockSpec((nb, D_MODEL), lambda i: (i, 0))
    return pl.pallas_call(
        functools.partial(_mixer_sample_kernel, pos=pos),
        grid=(n // nb,),
        in_specs=[x_blk, cache_blk, cache_blk, st_blk, const((1, LANES)), const((1, LANES)), const((1, LANES)),
                  const((8, LANES)), const((1, D_MODEL)), const((D_MODEL, IN_WIDTH)), const((1, LANES)),
                  const((1, LANES)), const((LANES, LANES)), const((len(POOL_WINDOWS), POOL_GC, POOL_GC)),
                  const((1, POOL_WIDTH)), const((D_MODEL, D_MODEL))],
        out_specs=[x_blk, cache_blk, cache_blk, st_blk],
        out_shape=[jax.ShapeDtypeStruct((n, D_MODEL), F32), jax.ShapeDtypeStruct(ck.shape, F32),
                   jax.ShapeDtypeStruct(cv.shape, F32), jax.ShapeDtypeStruct(st.shape, F32)],
        scratch_shapes=[pltpu.VMEM((D_MODEL, IN_WIDTH), BF16), pltpu.VMEM((D_MODEL, D_MODEL), BF16)],
        compiler_params=pltpu.CompilerParams(dimension_semantics=("arbitrary",), vmem_limit_bytes=VMEM_LIMIT),
        name="mixer_sample",
    )(x, ck, cv, st, *tabs, sink8, g_attn, w_in, qn, kn, hmean, w_pool, pscale, w_out)


def _block_plan(cnt, bm, n_lanes):
    e_sub = lax.broadcasted_iota(I32, (N_EXPERTS, LANES), 0)
    e_lane = lax.broadcasted_iota(I32, (N_EXPERTS, LANES), 1)
    padded = jnp.floor((cnt + (bm - 1)) / bm) * bm
    padded_lane = jnp.sum(jnp.where(e_sub == e_lane, padded, 0.0), axis=0, keepdims=True)
    pad_end = jnp.sum(jnp.where(e_lane <= e_sub, padded_lane, 0.0), axis=1, keepdims=True)
    pad_start = pad_end - padded
    blk_start = lax.broadcasted_iota(I32, (N_EXPERTS, n_lanes), 1).astype(F32) * bm
    blk_e = jnp.minimum(jnp.sum((pad_end <= blk_start).astype(F32), axis=0, keepdims=True), N_EXPERTS - 1.0)
    mine = lax.broadcasted_iota(I32, (N_EXPERTS, n_lanes), 0).astype(F32) == blk_e
    last = jnp.sum(jnp.where(mine, pad_start + cnt, 0.0), axis=0, keepdims=True)
    blk_valid = jnp.clip(last - blk_start[0:1], 0.0, float(bm))
    e_blk = lax.broadcasted_iota(I32, (N_EXPERTS, n_lanes), 0).astype(F32)
    later = jnp.min(jnp.where((e_blk > blk_e) & (cnt > 0.0), e_blk, float(N_EXPERTS)), axis=0, keepdims=True)
    blk_next = jnp.where(later < N_EXPERTS, later, -1.0)
    return pad_start, blk_e, blk_valid, blk_next


def _route_kernel(x1_ref, g_ref, wr_ref, br_ref, cnt_in_ref, hf_ref, idx_ref, gate_ref, pos_ref, cnt_ref,
                  pstart_ref, blke_ref, blkv_ref, blkn_ref, counts, *, bm):
    i = pl.program_id(0)
    rows = x1_ref.shape[0]

    @pl.when(i == 0)
    def _():
        counts[...] = cnt_in_ref[...]

    h = _rms(x1_ref[...], g_ref[...])
    hf_ref[...] = _pack_halves(h)

    wr = wr_ref[...]
    h_hi = h.astype(BF16)
    h_lo = (h - h_hi.astype(F32)).astype(BF16)
    w_hi = wr.astype(BF16)
    w_lo = (wr - w_hi.astype(F32)).astype(BF16)
    logits = (_mm(w_lo, h_lo, nt=True) + _mm(w_lo, h_hi, nt=True) + _mm(w_hi, h_lo, nt=True)
              + _mm(w_hi, h_hi, nt=True)) + br_ref[...]
    eid = lax.broadcasted_iota(I32, (N_EXPERTS, rows), 0).astype(F32)
    work = logits
    vals, hots = [], []
    for kk in range(TOP_K):
        m = jnp.max(work, axis=0, keepdims=True)
        first = jnp.min(jnp.where(work == m, eid, float(N_EXPERTS)), axis=0, keepdims=True)
        hot = eid == first
        work = jnp.where(hot, -jnp.inf, work)
        vals.append(m)
        hots.append(hot)
        idx_ref[kk:kk + 1, :] = first.astype(I32)
    es = [jnp.exp(vv - vals[0]) for vv in vals]
    den = es[0] + es[1] + es[2] + es[3]
    for kk in range(TOP_K):
        gate_ref[kk:kk + 1, :] = es[kk] / den

    chosen = hots[0] | hots[1] | hots[2] | hots[3]
    before = (lax.broadcasted_iota(I32, (rows, rows), 0) < lax.broadcasted_iota(I32, (rows, rows), 1)).astype(BF16)
    rank = jnp.dot(chosen.astype(BF16), before, preferred_element_type=F32) + counts[...]
    for kk in range(TOP_K):
        pos_ref[kk:kk + 1, :] = jnp.sum(jnp.where(hots[kk], rank, 0.0), axis=0, keepdims=True).astype(I32)
    counts[...] = counts[...] + jnp.sum(chosen.astype(F32), axis=1, keepdims=True)
    cnt_ref[...] = counts[...]

    @pl.when(i == pl.num_programs(0) - 1)
    def _():
        pad_start, blk_e, blk_valid, blk_next = _block_plan(counts[...], bm, blke_ref.shape[1])
        pstart_ref[...] = pad_start.astype(I32)
        blke_ref[...] = blk_e.astype(I32)
        blkv_ref[...] = blk_valid.astype(I32)
        blkn_ref[...] = blk_next.astype(I32)


def _route(x1, g_ffn, wr_t, br, cnt_in, rows, bm, n_blocks):
    n = x1.shape[0]
    assert n % rows == 0
    nb_lanes = -(-n_blocks // LANES) * LANES
    const = lambda shape: pl.BlockSpec(shape, lambda i: (0,) * len(shape))
    tok_blk = pl.BlockSpec((TOP_K, rows), lambda i: (0, i))
    return pl.pallas_call(
        functools.partial(_route_kernel, bm=bm),
        grid=(n // rows,),
        in_specs=[pl.BlockSpec((rows, D_MODEL), lambda i: (i, 0)), const((1, D_MODEL)), const((N_EXPERTS, D_MODEL)),
                  const((N_EXPERTS, 1)), const((N_EXPERTS, 1))],
        out_specs=[pl.BlockSpec((rows, D_MODEL // 2), lambda i: (i, 0)), tok_blk, tok_blk, tok_blk,
                   const((N_EXPERTS, 1)), const((N_EXPERTS, 1)), const((1, nb_lanes)), const((1, nb_lanes)),
                   const((1, nb_lanes))],
        out_shape=[jax.ShapeDtypeStruct((n, D_MODEL // 2), U32), jax.ShapeDtypeStruct((TOP_K, n), I32),
                   jax.ShapeDtypeStruct((TOP_K, n), F32), jax.ShapeDtypeStruct((TOP_K, n), I32),
                   jax.ShapeDtypeStruct((N_EXPERTS, 1), F32), jax.ShapeDtypeStruct((N_EXPERTS, 1), I32),
                   jax.ShapeDtypeStruct((1, nb_lanes), I32), jax.ShapeDtypeStruct((1, nb_lanes), I32),
                   jax.ShapeDtypeStruct((1, nb_lanes), I32)],
        scratch_shapes=[pltpu.VMEM((N_EXPERTS, 1), F32)],
        compiler_params=pltpu.CompilerParams(dimension_semantics=("arbitrary",), vmem_limit_bytes=VMEM_LIMIT),
        name="route",
    )(x1, g_ffn, wr_t, br, cnt_in)


def _sc_mesh():
    return plsc.VectorSubcoreMesh(core_axis_name="core", subcore_axis_name="subcore")


def _sc_worker_id():
    return lax.axis_index("core") * SC_SUBCORES + lax.axis_index("subcore")


def _scatter_rows(xa, xb, dest, n_rows):
    ch = SC_CHUNK
    na, w = xa.shape
    n = na + xb.shape[0]
    nk = dest.shape[0]
    assert na % ch == 0 and n % ch == 0 and dest.shape[1] == n and xb.shape[1] == w and xa.dtype == xb.dtype
    n_chunks = n // ch
    dest_c = dest.reshape(nk, n_chunks, ch).transpose(1, 0, 2).reshape(-1)
    dma = pltpu.SemaphoreType.DMA

    @pl.kernel(out_type=jax.ShapeDtypeStruct((n_rows, w), xa.dtype), mesh=_sc_mesh(),
               scratch_types=[pltpu.VMEM((nk * ch,), I32), pltpu.VMEM((ch, w), xa.dtype), dma] + [dma] * nk)
    def scatter_kernel(xa_hbm, xb_hbm, d_hbm, o_hbm, idx_v, buf, sem_i, *sem_s):
        wid = _sc_worker_id()

        @pl.loop(0, -(-n_chunks // SC_WORKERS))
        def _(j):
            c = j * SC_WORKERS + wid

            @pl.when(c < n_chunks)
            def _():
                load_idx = pltpu.make_async_copy(d_hbm.at[pl.ds(c * (nk * ch), nk * ch)], idx_v, sem_i)
                load_idx.start()

                @pl.when(c < na // ch)
                def _():
                    pltpu.sync_copy(xa_hbm.at[pl.ds(c * ch, ch)], buf)

                @pl.when(c >= na // ch)
                def _():
                    pltpu.sync_copy(xb_hbm.at[pl.ds(c * ch - na, ch)], buf)

                load_idx.wait()
                puts = [pltpu.make_async_copy(buf, o_hbm.at[idx_v.at[pl.ds(kk * ch, ch)]], sem_s[kk])
                        for kk in range(nk)]
                for cp in puts:
                    cp.start()
                for cp in puts:
                    cp.wait()

    return scatter_kernel(xa, xb, dest_c)


def _gather_rows(src, idx):
    ch = SC_CHUNK
    m = idx.shape[0]
    w = src.shape[1]
    per = m // SC_WORKERS
    n_pairs = per // (2 * ch)
    assert m % SC_WORKERS == 0 and per % (2 * ch) == 0
    dma = pltpu.SemaphoreType.DMA

    @pl.kernel(out_type=jax.ShapeDtypeStruct((m, w), src.dtype), mesh=_sc_mesh(),
               scratch_types=[pltpu.VMEM((per,), I32), pltpu.VMEM((ch, w), src.dtype), pltpu.VMEM((ch, w), src.dtype),
                              dma, dma, dma, dma])
    def gather_kernel(s_hbm, i_hbm, o_hbm, idx_v, buf_a, buf_b, sem_ga, sem_gb, sem_wa, sem_wb):
        base = _sc_worker_id() * per
        pltpu.sync_copy(i_hbm.at[pl.ds(base, per)], idx_v)

        def fetch(j, buf, sem):
            return pltpu.make_async_copy(s_hbm.at[idx_v.at[pl.ds(j * ch, ch)]], buf, sem)

        def put(j, buf, sem):
            return pltpu.make_async_copy(buf, o_hbm.at[pl.ds(base + j * ch, ch)], sem)

        fetch(0, buf_a, sem_ga).start()

        @pl.loop(0, n_pairs)
        def _(p):
            j0 = 2 * p
            j1 = j0 + 1

            @pl.when(p > 0)
            def _():
                put(j1 - 2, buf_b, sem_wb).wait()

            fetch(j1, buf_b, sem_gb).start()
            fetch(j0, buf_a, sem_ga).wait()
            put(j0, buf_a, sem_wa).start()
            fetch(j1, buf_b, sem_gb).wait()
            put(j1, buf_b, sem_wb).start()
            put(j0, buf_a, sem_wa).wait()

            @pl.when(p + 1 < n_pairs)
            def _():
                fetch(j0 + 2, buf_a, sem_ga).start()

        put(2 * n_pairs - 1, buf_b, sem_wb).wait()

    return gather_kernel(src, idx)


def _moe_kernel(blk_e_ref, blk_valid_ref, blk_next_ref, xs_ref, wgu_hbm, bgu_ref, wd_hbm, bd_ref, y_ref,
                wgu_f32, wd_f32, wgu_bf, wd_bf, sems, *, e0):
    i = pl.program_id(0)
    e = blk_e_ref[i]
    n_valid = blk_valid_ref[i]
    used = n_valid > 0
    new_expert = (i == 0) | (blk_e_ref[jnp.maximum(i - 1, 0)] != e)

    def weight_copies(expert):
        return (pltpu.make_async_copy(wgu_hbm.at[e0 + expert], wgu_f32, sems.at[0]),
                pltpu.make_async_copy(wd_hbm.at[e0 + expert], wd_f32, sems.at[1]))

    @pl.when(used & (i == 0))
    def _():
        for cp in weight_copies(e):
            cp.start()

    @pl.when(used & new_expert)
    def _():
        for cp in weight_copies(e):
            cp.wait()
        wgu_bf[...] = wgu_f32[...].astype(BF16)
        wd_bf[...] = wd_f32[...].astype(BF16)
        nxt = blk_next_ref[i]

        @pl.when(nxt >= 0)
        def _():
            for cp in weight_copies(nxt):
                cp.start(priority=1)

    half = xs_ref.shape[0] // 2

    def expert_mlp(n_rows):
        valid = lax.broadcasted_iota(I32, (n_rows, xs_ref.shape[1]), 0) < n_valid
        xb = _unpack_halves(jnp.where(valid, xs_ref[0:n_rows, :], jnp.uint32(0))).astype(BF16)
        gu = jnp.dot(xb, wgu_bf[...], preferred_element_type=F32) + bgu_ref[0]
        g = jnp.minimum(gu[:, :D_FF], SWIGLU_LIMIT)
        up = jnp.clip(gu[:, D_FF:], -SWIGLU_LIMIT, SWIGLU_LIMIT)
        act = (up + 1.0) * (g * jax.nn.sigmoid(SWIGLU_ALPHA * g))
        y = jnp.dot(act.astype(BF16), wd_bf[...], preferred_element_type=F32) + bd_ref[0]
        y_ref[0:n_rows, :] = _pack_halves(y)

    @pl.when(n_valid > half)
    def _():
        expert_mlp(2 * half)

    @pl.when(used & (n_valid <= half))
    def _():
        expert_mlp(half)
        y_ref[half:, :] = jnp.zeros((half, y_ref.shape[1]), y_ref.dtype)

    @pl.when(jnp.logical_not(used))
    def _():
        y_ref[...] = jnp.zeros_like(y_ref)


def _moe(xs, blk_e, blk_valid, blk_next, layer, w_gu, b_gu, w_d, b_d):
    n_rows = xs.shape[0]
    bm = MOE_BM
    assert n_rows % bm == 0
    e0 = layer * N_EXPERTS
    n_we = w_gu.shape[0] * w_gu.shape[1]
    any_spec = pl.BlockSpec(memory_space=pl.ANY)
    grid_spec = pltpu.PrefetchScalarGridSpec(
        num_scalar_prefetch=3,
        grid=(n_rows // bm,),
        in_specs=[pl.BlockSpec((bm, D_MODEL // 2), lambda i, be, bv, bn: (i, 0)),
                  any_spec,
                  pl.BlockSpec((1, 1, 2 * D_FF), lambda i, be, bv, bn: (e0 + be[i], 0, 0)),
                  any_spec,
                  pl.BlockSpec((1, 1, D_MODEL), lambda i, be, bv, bn: (e0 + be[i], 0, 0))],
        out_specs=pl.BlockSpec((bm, D_MODEL // 2), lambda i, be, bv, bn: (i, 0)),
        scratch_shapes=[pltpu.VMEM((D_MODEL, 2 * D_FF), F32), pltpu.VMEM((D_FF, D_MODEL), F32),
                        pltpu.VMEM((D_MODEL, 2 * D_FF), BF16), pltpu.VMEM((D_FF, D_MODEL), BF16),
                        pltpu.SemaphoreType.DMA((2,))],
    )
    return pl.pallas_call(
        functools.partial(_moe_kernel, e0=e0),
        grid_spec=grid_spec,
        out_shape=jax.ShapeDtypeStruct((n_rows, D_MODEL // 2), U32),
        compiler_params=pltpu.CompilerParams(dimension_semantics=("arbitrary",), vmem_limit_bytes=VMEM_LIMIT),
        name="moe_experts",
    )(blk_e, blk_valid, blk_next, xs, w_gu.reshape(n_we, D_MODEL, 2 * D_FF), b_gu.reshape(n_we, 1, 2 * D_FF),
      w_d.reshape(n_we, D_FF, D_MODEL), b_d.reshape(n_we, 1, D_MODEL))


def _ple_kernel(x1_ref, y0_ref, y1_ref, y2_ref, y3_ref, gates_ref, p_ref, g_ref, wg_ref, wp_ref, *rest):
    out_ref, wg_bf, wp_bf = rest[-3:]

    @pl.when(pl.program_id(0) == 0)
    def _():
        wg_bf[...] = wg_ref[...].astype(BF16)
        wp_bf[...] = wp_ref[...].astype(BF16)

    x2 = x1_ref[...]
    gates = gates_ref[...]
    for kk, y_ref in enumerate((y0_ref, y1_ref, y2_ref, y3_ref)):
        x2 = x2 + _unpack_halves(y_ref[...]) * gates[:, kk:kk + 1]
    hp = _rms(x2, g_ref[...]).astype(BF16)
    gate = 0.5 * jnp.tanh(0.5 * jnp.dot(hp, wg_bf[...], preferred_element_type=F32)) + 0.5
    pp = jnp.dot(p_ref[...].astype(BF16), wp_bf[...], preferred_element_type=F32)
    out_ref[...] = x2 + gate * pp


def _ple(x1, tok0, n, y_tok, y0, gates_t, p_all, p0, g_ple, w_gate, w_proj, rows, out_into=None):
    assert n % rows == 0 and tok0 % rows == 0 and y0 % rows == 0 and p0 % rows == 0
    const = lambda shape: pl.BlockSpec(shape, lambda i: (0,) * len(shape))
    tok_blk = lambda width: pl.BlockSpec((rows, width), lambda i: (tok0 // rows + i, 0))
    y_blk = lambda kk: pl.BlockSpec((rows, D_MODEL // 2), lambda i: ((y0 + kk * n) // rows + i, 0))
    operands = (x1, y_tok, y_tok, y_tok, y_tok, gates_t, p_all, g_ple, w_gate, w_proj)
    extra_specs, extra_args, aliases = [], [], {}
    if out_into is not None:
        assert out_into.shape == x1.shape
        extra_specs, extra_args, aliases = [pl.BlockSpec(memory_space=pl.ANY)], [out_into], {len(operands): 0}
    return pl.pallas_call(
        _ple_kernel,
        grid=(n // rows,),
        in_specs=[tok_blk(D_MODEL), y_blk(0), y_blk(1), y_blk(2), y_blk(3), tok_blk(TOP_K),
                  pl.BlockSpec((rows, PLE_DIM), lambda i: (p0 // rows + i, 0)),
                  const((1, D_MODEL)), const((D_MODEL, D_MODEL)), const((PLE_DIM, D_MODEL))] + extra_specs,
        out_specs=tok_blk(D_MODEL),
        out_shape=jax.ShapeDtypeStruct(x1.shape, F32),
        scratch_shapes=[pltpu.VMEM((D_MODEL, D_MODEL), BF16), pltpu.VMEM((PLE_DIM, D_MODEL), BF16)],
        input_output_aliases=aliases,
        compiler_params=pltpu.CompilerParams(dimension_semantics=("arbitrary",), vmem_limit_bytes=VMEM_LIMIT),
        name="combine_ple",
    )(*operands, *extra_args)


def _rope_tables_const(seq):
    half = ROT_DIM // 2
    d = np.arange(LANES) % HEAD_DIM
    inv = (np.float64(ROPE_THETA) ** (-np.arange(half, dtype=np.float64) / half)).astype(np.float32)
    inv_lane = np.where(d < ROT_DIM, inv[d % half], np.float32(0.0)).astype(np.float32)
    ang = np.arange(seq, dtype=np.float32)[:, None] * inv_lane[None, :]
    cos = np.cos(ang.astype(np.float64)).astype(np.float32)
    sin = np.sin(ang.astype(np.float64)).astype(np.float32)
    sin_a = np.where((d >= half) & (d < ROT_DIM), sin, np.float32(0.0))
    sin_b = np.where(d < half, -sin, np.float32(0.0))
    return tuple(jnp.asarray(a) for a in (cos, sin_a, sin_b))


def _rope_tables(pos):
    half = ROT_DIM // 2
    d = np.arange(LANES) % HEAD_DIM
    inv = ROPE_THETA ** (-jnp.arange(half, dtype=F32) / half)
    inv_lane = jnp.where(d < ROT_DIM, inv[d % half], 0.0)
    ang = pos.astype(F32)[:, None] * inv_lane[None, :]
    cos, sin = jnp.cos(ang), jnp.sin(ang)
    sin_a = jnp.where((d >= half) & (d < ROT_DIM), sin, 0.0)
    sin_b = jnp.where(d < half, -sin, 0.0)
    return cos, sin_a, sin_b


def _layer(layer, xp, xs, ck, cv, st, p_prompt_all, p_sample_all, past_len, refine_tail, tabs_p, tabs_s, finish_state,
           norm_attn, w_in, q_norm, k_norm, attn_sinks, w_pool, pool_scale, w_out,
           norm_ffn, w_router, b_router, w_gate_up_all, b_gate_up_all, w_down_all, b_down_all,
           norm_ple, w_ple_gate, w_ple_proj):
    t, ns = xp.shape[0], xs.shape[0]
    g_attn = norm_attn.reshape(1, D_MODEL)
    qn = jnp.tile(q_norm, 2).reshape(1, LANES)
    kn = jnp.tile(k_norm, 2).reshape(1, LANES)
    lane = np.arange(LANES)
    hmean = jnp.asarray((lane[:, None] // HEAD_DIM == lane[None, :] // HEAD_DIM) / HEAD_DIM, F32)
    pscale = pool_scale.reshape(1, POOL_WIDTH)
    sink8 = jnp.broadcast_to(attn_sinks.reshape(2, Q_TILES).T.reshape(8, 1), (8, LANES))

    mix_args = (g_attn, w_in, qn, kn, hmean, w_pool, pscale, w_out)
    x1p, nk_p, nv_p, nu_p = _mixer_prompt(xp, 0, MIX_ROWS, BF16, attn_sinks, tabs_p, *mix_args)
    if refine_tail:
        x1p, nk_p, nv_p, nu_p = _mixer_prompt(xp, t - 2 * MIX_TAIL_ROWS, MIX_TAIL_ROWS, F32, attn_sinks, tabs_p,
                                              *mix_args, x1_into=x1p)
    st_t = jnp.transpose(st, (1, 0, 2))
    x1s, nk_s, nv_s, nst_t = _mixer_sample(xs, ck.reshape(ns, -1, KV_WIDTH), cv.reshape(ns, -1, KV_WIDTH), st_t,
                                           past_len, tabs_s, sink8, *mix_args)

    g_ffn = norm_ffn.reshape(1, D_MODEL)
    wr_t = w_router.T
    br = b_router.reshape(N_EXPERTS, 1)
    n_tok = t + ns
    bm = MOE_BM
    n_blocks = -(-(n_tok * TOP_K + N_EXPERTS * (bm - 1)) // bm)
    hf_p, idx_p, gate_p, pos_p, cnt_p = _route(x1p, g_ffn, wr_t, br, jnp.zeros((N_EXPERTS, 1), F32),
                                               ROUTE_ROWS, bm, n_blocks)[:5]
    hf_s, idx_s, gate_s, pos_s, _, pad_start, blk_e, blk_valid, blk_next = _route(x1s, g_ffn, wr_t, br, cnt_p, ns,
                                                                                  bm, n_blocks)
    idx = jnp.concatenate([idx_p, idx_s], axis=1)
    pos = jnp.concatenate([pos_p, pos_s], axis=1)
    blk_e, blk_valid, blk_next = (a[0, :n_blocks] for a in (blk_e, blk_valid, blk_next))
    state = finish_state(nk_p, nv_p, nu_p[POOL_HALO - POOL_PREFIX:], nk_s, nv_s, jnp.transpose(nst_t, (1, 0, 2)))
    blk_e, state = lax.optimization_barrier((blk_e, state))
    start_of = jnp.sum(jnp.where(idx[None] == jnp.arange(N_EXPERTS, dtype=I32)[:, None, None],
                                 pad_start.reshape(N_EXPERTS, 1, 1), 0), axis=0)
    dest = start_of + pos

    xs_rows = _scatter_rows(hf_p, hf_s, dest, n_blocks * bm)
    y = _moe(xs_rows, blk_e, blk_valid, blk_next, layer, w_gate_up_all, b_gate_up_all, w_down_all, b_down_all)

    tq = t // COMBINE_PARTS
    unit = SC_WORKERS * SC_CHUNK * 2
    assert t % COMBINE_PARTS == 0 and (TOP_K * tq) % unit == 0 and tq % PLE_ROWS == 0
    g_ple = norm_ple.reshape(1, D_MODEL)
    gates_p = gate_p.T
    yp = None
    for part in range(COMBINE_PARTS):
        back = dest[:, part * tq:(part + 1) * tq].reshape(-1)
        if part == COMBINE_PARTS - 1:
            back = jnp.concatenate([back, dest[:, t:].reshape(-1)])
            back = jnp.concatenate([back, jnp.arange(-back.shape[0] % unit, dtype=I32)])
        y_part = _gather_rows(y, back)
        yp = _ple(x1p, part * tq, tq, y_part, 0, gates_p, p_prompt_all, layer * t + part * tq, g_ple,
                  w_ple_gate, w_ple_proj, PLE_ROWS, out_into=yp)
    ys = _ple(x1s, 0, ns, y_part, TOP_K * tq, gate_s.T, p_sample_all, layer * ns, g_ple, w_ple_gate, w_ple_proj, ns)
    return yp, ys, state


def kernel(x_prompt, x_sample, cache_k, cache_v, state_pool, p_prompt, p_sample, norm_attn, w_in, q_norm, k_norm,
           attn_sinks, w_pool, pool_scale, w_out, norm_ffn, w_router, b_router, w_gate_up, b_gate_up, w_down, b_down,
           norm_ple, w_ple_gate, w_ple_proj):
    depth = norm_attn.shape[0]
    batch, seq, d = x_prompt.shape
    ns, dec_seq, _ = x_sample.shape
    wb = cache_k.shape[2]
    assert batch == 1 and dec_seq == 1 and d == D_MODEL and wb == WINDOW
    assert cache_k.shape[3:] == (N_KV_HEADS, HEAD_DIM) and state_pool.shape[2:] == (POOL_PREFIX, POOL_WIDTH)
    past_len = PAST_LEN
    yp = x_prompt.reshape(seq, d)
    ys = x_sample.reshape(ns, d)
    p_prompt_all = p_prompt.reshape(depth * seq, PLE_DIM)
    p_sample_all = p_sample.reshape(depth * ns, PLE_DIM)
    tabs_p = _rope_tables_const(seq)
    tabs_s = _rope_tables(jnp.full((1,), past_len))
    kv_p, kv_s = (1, WINDOW, N_KV_HEADS, HEAD_DIM), (ns, wb, N_KV_HEADS, HEAD_DIM)
    state_shapes = (kv_p, kv_p, (1, POOL_PREFIX, POOL_WIDTH), kv_s, kv_s, (ns, POOL_PREFIX, POOL_WIDTH))
    per_layer = []

    def shaped(*state):
        return tuple(a.reshape(shp) for a, shp in zip(state, state_shapes))

    def stacked(*state):
        prevs = list(zip(*per_layer)) or [()] * len(state_shapes)
        return tuple(jnp.stack(list(prev) + [cur]) for prev, cur in zip(prevs, shaped(*state)))

    for i in range(depth):
        yp, ys, state = _layer(i, yp, ys, cache_k[i], cache_v[i], state_pool[i], p_prompt_all, p_sample_all, past_len,
                               i < depth - 1, tabs_p, tabs_s, shaped if i < depth - 1 else stacked,
                               norm_attn[i], w_in[i], q_norm[i], k_norm[i], attn_sinks[i], w_pool[i], pool_scale[i],
                               w_out[i], norm_ffn[i], w_router[i], b_router[i], w_gate_up, b_gate_up, w_down, b_down,
                               norm_ple[i], w_ple_gate[i], w_ple_proj[i])
        per_layer.append(state)
    return (yp.reshape(batch, seq, d), ys.reshape(ns, dec_seq, d)) + tuple(state)
```

```python
import functools

import jax
import jax.numpy as jnp
import numpy as np
from jax import lax
from jax.experimental import pallas as pl
from jax.experimental.pallas import tpu as pltpu
from jax.experimental.pallas import tpu_sc as plsc

F32 = jnp.float32
BF16 = jnp.bfloat16
U32 = jnp.uint32
I32 = jnp.int32

D_MODEL = 1024
HEAD_DIM = 64
N_HEADS = 8
N_KV_HEADS = 2
GROUP = N_HEADS // N_KV_HEADS
ATTN_WIDTH = N_HEADS * HEAD_DIM
KV_WIDTH = N_KV_HEADS * HEAD_DIM
POOL_WIDTH = 512
POOL_WINDOWS = (2, 4, 8, 16)
POOL_GC = POOL_WIDTH // len(POOL_WINDOWS)
POOL_PREFIX = max(POOL_WINDOWS) - 1
POOL_HALO = POOL_PREFIX + 1
POOL_PAD = 8
IN_WIDTH = ATTN_WIDTH + 2 * KV_WIDTH + POOL_WIDTH
WINDOW = 128
ROPE_THETA = 500000.0
ROT_DIM = HEAD_DIM // 4
N_EXPERTS = 32
TOP_K = 4
D_FF = 1024
SWIGLU_ALPHA = 1.702
SWIGLU_LIMIT = 7.0
PLE_DIM = 256
PAST_LEN = 16384
EPS = 1e-5
NEG_INF = -1e30

LANES = 128
Q_TILES = ATTN_WIDTH // LANES

MIX_ROWS = 1024
MIX_TAIL_ROWS = 128
SAMPLE_CHUNK = 16
ROUTE_ROWS = 1024
MOE_BM = 512
SC_CORES = 2
SC_SUBCORES = 16
SC_WORKERS = SC_CORES * SC_SUBCORES
SC_CHUNK = 64
PLE_ROWS = 1024
COMBINE_PARTS = 1
VMEM_LIMIT = 56 * 1024 * 1024


def _rms(x, g):
    return x * lax.rsqrt(jnp.mean(x * x, axis=-1, keepdims=True) + EPS) * g


def _pack_halves(x):
    w = x.shape[1] // 2
    lo = lax.bitcast_convert_type(x[:, :w].astype(BF16).astype(F32), U32) >> 16
    hi = lax.bitcast_convert_type(x[:, w:].astype(BF16).astype(F32), U32) & jnp.uint32(0xFFFF0000)
    return lo | hi


def _unpack_halves(packed):
    lo = lax.bitcast_convert_type(packed << 16, F32)
    hi = lax.bitcast_convert_type(packed & jnp.uint32(0xFFFF0000), F32)
    return jnp.concatenate([lo, hi], axis=1)


def _mm(a, b, nt=False):
    dims = (((1,), (1 if nt else 0,)), ((), ()))
    if b.dtype == F32:
        return lax.dot_general(a.astype(F32), b, dims, preferred_element_type=F32, precision=lax.Precision.HIGHEST)
    return lax.dot_general(a.astype(BF16), b, dims, preferred_element_type=F32)


def _head_norm_rope(t, hmean, gain, cos, sin_a, sin_b):
    t = t * lax.rsqrt(_mm(t * t, hmean) + EPS) * gain
    return t * cos + pltpu.roll(t, ROT_DIM // 2, axis=1) * sin_a + pltpu.roll(t, LANES - ROT_DIM // 2, axis=1) * sin_b


def _prep_mixer_weights(win_ref, wout_ref, win_c, wout_c):
    cdt = win_c.dtype
    for j in range(Q_TILES):
        for s in range(2):
            src = (j + Q_TILES * s) * HEAD_DIM
            dst = j * LANES + s * HEAD_DIM
            win_c[:, dst:dst + HEAD_DIM] = win_ref[:, src:src + HEAD_DIM].astype(cdt)
            wout_c[dst:dst + HEAD_DIM, :] = wout_ref[src:src + HEAD_DIM, :].astype(cdt)
    win_c[:, ATTN_WIDTH:] = win_ref[:, ATTN_WIDTH:].astype(cdt)
    wout_c[ATTN_WIDTH:, :] = wout_ref[ATTN_WIDTH:, :].astype(cdt)


def _mixer_prompt_kernel(sinks_ref, x_ref, cos_ref, sa_ref, sb_ref, g_ref, win_ref, qn_ref, kn_ref, hm_ref,
                         wpool_ref, pscale_ref, wout_ref, *rest, row_offset, aliased):
    (x1_ref, klast_ref, vlast_ref, ulast_ref, kprev, vprev, uext, mix, win_c, wout_c,
     *lvl) = rest[1:] if aliased else rest
    i = pl.program_id(0)
    rows = x_ref.shape[0]
    n_sub = rows // WINDOW
    cdt = win_c.dtype
    row0 = row_offset + i * rows

    @pl.when(i == 0)
    def _():
        _prep_mixer_weights(win_ref, wout_ref, win_c, wout_c)
        kprev[...] = jnp.zeros_like(kprev)
        vprev[...] = jnp.zeros_like(vprev)
        uext[0:POOL_PAD + POOL_HALO, :] = jnp.zeros((POOL_PAD + POOL_HALO, POOL_WIDTH), F32)
        for buf in lvl:
            buf[0:POOL_PAD, :] = jnp.zeros((POOL_PAD, POOL_GC), F32)

    x = x_ref[...]
    proj = _mm(_rms(x, g_ref[...]), win_c[...])
    cos, sin_a, sin_b = cos_ref[...], sa_ref[...], sb_ref[...]

    n_t = Q_TILES + 1
    t_all = jnp.concatenate([proj[:, j * LANES:(j + 1) * LANES] for j in range(n_t)], axis=0)
    t3 = (t_all * lax.rsqrt(_mm(t_all * t_all, hm_ref[...].astype(cdt)) + EPS)).reshape(n_t, rows, LANES)
    t3 = jnp.concatenate([t3[:Q_TILES] * (qn_ref[...] * HEAD_DIM ** -0.5), t3[Q_TILES:] * kn_ref[...]], axis=0)
    t2 = t3.reshape(n_t * rows, LANES)
    t3 = (t3 * cos + pltpu.roll(t2, ROT_DIM // 2, axis=1).reshape(n_t, rows, LANES) * sin_a
          + pltpu.roll(t2, LANES - ROT_DIM // 2, axis=1).reshape(n_t, rows, LANES) * sin_b)
    q3 = t3[:Q_TILES]
    k = t3[Q_TILES]
    v = proj[:, ATTN_WIDTH + KV_WIDTH:ATTN_WIDTH + 2 * KV_WIDTH]
    u = proj[:, ATTN_WIDTH + 2 * KV_WIDTH:]
    klast_ref[...] = k[rows - WINDOW:, :]
    vlast_ref[...] = v[rows - WINDOW:, :]
    ulast_ref[...] = u[rows - POOL_HALO:, :]
    k_c = k.astype(cdt)
    v_c = jnp.concatenate([v.astype(cdt), jnp.ones((rows, LANES), cdt)], axis=1)
    v_first = jnp.concatenate([vprev[...], jnp.ones((WINDOW, LANES), cdt)], axis=1)

    lane = lax.broadcasted_iota(I32, (WINDOW, LANES), 1)
    left = (lane < HEAD_DIM)[None]
    qi = lax.broadcasted_iota(I32, (WINDOW, 2 * WINDOW), 0)
    kj = lax.broadcasted_iota(I32, (WINDOW, 2 * WINDOW), 1)
    band = (kj - qi >= 1) & (kj - qi <= WINDOW)
    sink3 = jnp.concatenate([jnp.full((1, 1, 1), sinks_ref[j + Q_TILES * s], F32)
                             for j in range(Q_TILES) for s in range(2)], axis=0)
    n_g = 2 * Q_TILES
    fill = jnp.where(kj[0:1][None] == 0, sink3, NEG_INF)
    slot = lax.broadcasted_iota(I32, (2 * WINDOW, 2 * LANES), 0) == 0
    slot_v = slot & (lax.broadcasted_iota(I32, (2 * WINDOW, 2 * LANES), 1) < LANES)

    v_cats, masks, scores = [], [], []
    for c in range(n_sub):
        r0 = c * WINDOW
        if c == 0:
            k_cat = jnp.concatenate([kprev[...], k_c[0:WINDOW]], axis=0)
            v_cat = jnp.concatenate([v_first, v_c[0:WINDOW]], axis=0)
            masks.append(band & (kj + (row0 - WINDOW) >= 0))
        else:
            k_cat = k_c[r0 - WINDOW:r0 + WINDOW]
            v_cat = v_c[r0 - WINDOW:r0 + WINDOW]
            masks.append(band)
        v_cats.append(jnp.where(slot_v, jnp.zeros_like(v_cat), v_cat))
        q_c = q3[:, r0:r0 + WINDOW, :]
        q_all = jnp.concatenate([jnp.where(left, q_c, 0.0), jnp.where(left, 0.0, q_c)], axis=1)
        scores.append(_mm(q_all.reshape(n_g * WINDOW, LANES), k_cat, nt=True))
    kprev[...] = k_c[rows - WINDOW:]
    vprev[...] = v[rows - WINDOW:].astype(cdt)

    probs = []
    for c in range(n_sub):
        s = jnp.where(masks[c][None], scores[c].reshape(n_g, WINDOW, 2 * WINDOW), fill)
        m = jnp.max(s, axis=-1, keepdims=True)
        probs.append(jnp.exp(s - m).astype(cdt).reshape(n_g * WINDOW, 2 * WINDOW))
    applied = [_mm(probs[c], v_cats[c]) for c in range(n_sub)]
    for c in range(n_sub):
        r0 = c * WINDOW
        o = (applied[c][:, :LANES] / applied[c][:, LANES:]).reshape(Q_TILES, 2 * WINDOW, LANES)
        a = jnp.where(left, o[:, :WINDOW], o[:, WINDOW:])
        for j in range(Q_TILES):
            mix[r0:r0 + WINDOW, j * LANES:(j + 1) * LANES] = a[j].astype(cdt)

    base = POOL_PAD + POOL_HALO
    ext = base + rows
    uext[base:ext, :] = u
    pos1 = (lax.broadcasted_iota(I32, (rows, 1), 0) + row0 + 1).astype(F32)
    lvl_of = {1: lvl[0:1], 2: lvl[1:3], 3: lvl[3:5]}
    for gi, w in enumerate(POOL_WINDOWS):
        cols = slice(gi * POOL_GC, (gi + 1) * POOL_GC)
        src, src_cols = uext, cols
        for level in range(1, gi + 2):
            sft = 1 << (level - 1)
            if level <= gi:
                dst = lvl_of[gi][(level - 1) % 2]
                dst[POOL_PAD:ext, :] = src[POOL_PAD:ext, src_cols] + src[POOL_PAD - sft:ext - sft, src_cols]
                src, src_cols = dst, slice(None)
            else:
                wsum = src[base:ext, src_cols] + src[base - sft:ext - sft, src_cols]
        d = wsum / jnp.minimum(pos1, float(w)) - u[:, cols]
        y = _mm(d, wpool_ref[gi].astype(cdt)) * pscale_ref[:, cols]
        mix[:, ATTN_WIDTH + gi * POOL_GC:ATTN_WIDTH + (gi + 1) * POOL_GC] = y.astype(cdt)
    uext[POOL_PAD:base, :] = u[rows - POOL_HALO:, :]

    x1_ref[...] = x + _mm(mix[...], wout_c[...])


def _mixer_prompt(x_full, row_offset, rows, cdt, sinks, tabs, g_attn, w_in, qn, kn, hmean, w_pool, pscale, w_out,
                  x1_into=None):
    t = x_full.shape[0] - row_offset
    assert t % rows == 0 and row_offset % rows == 0 and rows % WINDOW == 0 and rows >= POOL_HALO
    blk0, n_steps = row_offset // rows, t // rows
    const = lambda shape: pl.BlockSpec(shape, lambda i, *_: (0,) * len(shape))
    row_blk = lambda width: pl.BlockSpec((rows, width), lambda i, *_: (blk0 + i, 0))
    aliased = x1_into is not None
    operands = (sinks, x_full, *tabs, g_attn, w_in, qn, kn, hmean, w_pool, pscale, w_out)
    if aliased:
        assert x1_into.shape == x_full.shape
        x1_spec = pl.BlockSpec((rows, D_MODEL), lambda i, *_: (blk0 + n_steps - 1, 0))
        x1_shape = x1_into.shape
        extra_specs, extra_args, aliases = [pl.BlockSpec(memory_space=pl.ANY)], [x1_into], {len(operands): 0}
    else:
        x1_spec = pl.BlockSpec((rows, D_MODEL), lambda i, *_: (i, 0))
        x1_shape = (t, D_MODEL)
        extra_specs, extra_args, aliases = [], [], {}
    grid_spec = pltpu.PrefetchScalarGridSpec(
        num_scalar_prefetch=1,
        grid=(n_steps,),
        in_specs=[row_blk(D_MODEL), row_blk(LANES), row_blk(LANES), row_blk(LANES),
                  const((1, D_MODEL)), const((D_MODEL, IN_WIDTH)), const((1, LANES)), const((1, LANES)),
                  const((LANES, LANES)), const((len(POOL_WINDOWS), POOL_GC, POOL_GC)), const((1, POOL_WIDTH)),
                  const((D_MODEL, D_MODEL))] + extra_specs,
        out_specs=[x1_spec, const((WINDOW, KV_WIDTH)), const((WINDOW, KV_WIDTH)),
                   const((POOL_HALO, POOL_WIDTH))],
        scratch_shapes=[pltpu.VMEM((WINDOW, KV_WIDTH), cdt), pltpu.VMEM((WINDOW, KV_WIDTH), cdt),
                        pltpu.VMEM((POOL_PAD + POOL_HALO + rows, POOL_WIDTH), F32), pltpu.VMEM((rows, D_MODEL), cdt),
                        pltpu.VMEM((D_MODEL, IN_WIDTH), cdt), pltpu.VMEM((D_MODEL, D_MODEL), cdt)]
        + [pltpu.VMEM((POOL_PAD + POOL_HALO + rows, POOL_GC), F32)] * 5,
    )
    return pl.pallas_call(
        functools.partial(_mixer_prompt_kernel, row_offset=row_offset, aliased=aliased),
        grid_spec=grid_spec,
        out_shape=[jax.ShapeDtypeStruct(x1_shape, F32), jax.ShapeDtypeStruct((WINDOW, KV_WIDTH), F32),
                   jax.ShapeDtypeStruct((WINDOW, KV_WIDTH), F32), jax.ShapeDtypeStruct((POOL_HALO, POOL_WIDTH), F32)],
        input_output_aliases=aliases,
        compiler_params=pltpu.CompilerParams(dimension_semantics=("arbitrary",), vmem_limit_bytes=VMEM_LIMIT),
        name="mixer_prompt",
    )(*operands, *extra_args)


def _mixer_sample_kernel(x_ref, ck_ref, cv_ref, st_ref, cos_ref, sa_ref, sb_ref, sink8_ref, g_ref, win_ref, qn_ref,
                         kn_ref, hm_ref, wpool_ref, pscale_ref, wout_ref,
                         x1_ref, nk_ref, nv_ref, nst_ref, win_c, wout_c, *, pos):
    nb = x_ref.shape[0]
    wb = ck_ref.shape[1]

    @pl.when(pl.program_id(0) == 0)
    def _():
        _prep_mixer_weights(win_ref, wout_ref, win_c, wout_c)

    x = x_ref[...]
    h = _rms(x, g_ref[...]).astype(BF16)
    proj = jnp.dot(h, win_c[...], preferred_element_type=F32)
    cos, sin_a, sin_b = cos_ref[...], sa_ref[...], sb_ref[...]
    hmean = hm_ref[...].astype(BF16)
    k = _head_norm_rope(proj[:, ATTN_WIDTH:ATTN_WIDTH + KV_WIDTH], hmean, kn_ref[...], cos, sin_a, sin_b)
    v = proj[:, ATTN_WIDTH + KV_WIDTH:ATTN_WIDTH + 2 * KV_WIDTH]
    u = proj[:, ATTN_WIDTH + 2 * KV_WIDTH:]

    nk_ref[:, 0:wb - 1, :] = ck_ref[:, 1:wb, :]
    nv_ref[:, 0:wb - 1, :] = cv_ref[:, 1:wb, :]
    for b in range(nb):
        nk_ref[b, wb - 1:wb, :] = k[b:b + 1, :]
        nv_ref[b, wb - 1:wb, :] = v[b:b + 1, :]

    r8 = lax.broadcasted_iota(I32, (nb * 8, LANES), 0)
    lane8 = lax.broadcasted_iota(I32, (nb * 8, LANES), 1)
    keep = (lane8 < HEAD_DIM) == (r8 % 2 == 0)
    rep = (lax.broadcasted_iota(I32, (nb * 8, nb), 0) // 8 == lax.broadcasted_iota(I32, (nb * 8, nb), 1)).astype(BF16)
    q8 = jnp.zeros((nb * 8, LANES), F32)
    scale = HEAD_DIM ** -0.5
    for j in range(Q_TILES):
        qt = _head_norm_rope(proj[:, j * LANES:(j + 1) * LANES], hmean, qn_ref[...], cos, sin_a, sin_b) * scale
        qrep = jnp.dot(rep, qt.astype(BF16), preferred_element_type=F32)
        q8 = jnp.where(keep & ((r8 % 8) // 2 == j), qrep, q8)
    q8 = q8.astype(BF16)

    sink = jnp.concatenate([sink8_ref[:, 0:1]] * nb, axis=0)
    ones_bf = jnp.ones((wb, LANES), BF16)
    assert pos >= wb - 1 and wb <= WINDOW
    s = jnp.concatenate([_mm(q8[b * 8:(b + 1) * 8], nk_ref[b].astype(BF16), nt=True) for b in range(nb)], axis=0)
    m = jnp.maximum(jnp.max(s, axis=-1, keepdims=True), sink)
    e = jnp.exp(s - m).astype(BF16)
    den = _mm(e, ones_bf) + jnp.exp(sink - m)
    o = jnp.concatenate([_mm(e[b * 8:(b + 1) * 8], nv_ref[b].astype(BF16)) for b in range(nb)], axis=0) / den
    o8m = jnp.where(keep, o, 0.0).astype(BF16)

    a_tiles = []
    sel_r = lax.broadcasted_iota(I32, (nb, nb * 8), 1)
    sel_b = lax.broadcasted_iota(I32, (nb, nb * 8), 0)
    for j in range(Q_TILES):
        sel = ((sel_r // 8 == sel_b) & ((sel_r % 8) // 2 == j)).astype(BF16)
        a_tiles.append(jnp.dot(sel, o8m, preferred_element_type=F32))

    z_tiles = []
    for gi, w in enumerate(POOL_WINDOWS):
        cols = slice(gi * POOL_GC, (gi + 1) * POOL_GC)
        wsum = u[:, cols]
        for sft in range(1, w):
            wsum = wsum + st_ref[POOL_PREFIX - sft, :, cols]
        d = wsum / float(min(pos + 1, w)) - u[:, cols]
        z_tiles.append(jnp.dot(d.astype(BF16), wpool_ref[gi].astype(BF16), preferred_element_type=F32)
                       * pscale_ref[:, cols])
    nst_ref[0:POOL_PREFIX - 1] = st_ref[1:POOL_PREFIX]
    nst_ref[POOL_PREFIX - 1] = u

    mixv = jnp.concatenate(a_tiles + z_tiles, axis=1).astype(BF16)
    x1_ref[...] = x + jnp.dot(mixv, wout_c[...], preferred_element_type=F32)


def _mixer_sample(x, ck, cv, st, pos, tabs, sink8, g_attn, w_in, qn, kn, hmean, w_pool, pscale, w_out):
    n, wb = ck.shape[0], ck.shape[1]
    nb = SAMPLE_CHUNK
    assert n % nb == 0
    const = lambda shape: pl.BlockSpec(shape, lambda i: (0,) * len(shape))
    cache_blk = pl.BlockSpec((nb, wb, KV_WIDTH), lambda i: (i, 0, 0))
    st_blk = pl.BlockSpec((POOL_PREFIX, nb, POOL_WIDTH), lambda i: (0, i, 0))
    x_blk = pl.BlockSpec((nb, D_MODEL), lambda i: (i, 0))
    return pl.pallas_call(
        functools.partial(_mixer_sample_kernel, pos=pos),
        grid=(n // nb,),
        in_specs=[x_blk, cache_blk, cache_blk, st_blk, const((1, LANES)), const((1, LANES)), const((1, LANES)),
                  const((8, LANES)), const((1, D_MODEL)), const((D_MODEL, IN_WIDTH)), const((1, LANES)),
                  const((1, LANES)), const((LANES, LANES)), const((len(POOL_WINDOWS), POOL_GC, POOL_GC)),
                  const((1, POOL_WIDTH)), const((D_MODEL, D_MODEL))],
        out_specs=[x_blk, cache_blk, cache_blk, st_blk],
        out_shape=[jax.ShapeDtypeStruct((n, D_MODEL), F32), jax.ShapeDtypeStruct(ck.shape, F32),
                   jax.ShapeDtypeStruct(cv.shape, F32), jax.ShapeDtypeStruct(st.shape, F32)],
        scratch_shapes=[pltpu.VMEM((D_MODEL, IN_WIDTH), BF16), pltpu.VMEM((D_MODEL, D_MODEL), BF16)],
        compiler_params=pltpu.CompilerParams(dimension_semantics=("arbitrary",), vmem_limit_bytes=VMEM_LIMIT),
        name="mixer_sample",
    )(x, ck, cv, st, *tabs, sink8, g_attn, w_in, qn, kn, hmean, w_pool, pscale, w_out)


def _block_plan(cnt, bm, n_lanes):
    e_sub = lax.broadcasted_iota(I32, (N_EXPERTS, LANES), 0)
    e_lane = lax.broadcasted_iota(I32, (N_EXPERTS, LANES), 1)
    padded = jnp.floor((cnt + (bm - 1)) / bm) * bm
    padded_lane = jnp.sum(jnp.where(e_sub == e_lane, padded, 0.0), axis=0, keepdims=True)
    pad_end = jnp.sum(jnp.where(e_lane <= e_sub, padded_lane, 0.0), axis=1, keepdims=True)
    pad_start = pad_end - padded
    blk_start = lax.broadcasted_iota(I32, (N_EXPERTS, n_lanes), 1).astype(F32) * bm
    blk_e = jnp.minimum(jnp.sum((pad_end <= blk_start).astype(F32), axis=0, keepdims=True), N_EXPERTS - 1.0)
    mine = lax.broadcasted_iota(I32, (N_EXPERTS, n_lanes), 0).astype(F32) == blk_e
    last = jnp.sum(jnp.where(mine, pad_start + cnt, 0.0), axis=0, keepdims=True)
    blk_valid = jnp.clip(last - blk_start[0:1], 0.0, float(bm))
    e_blk = lax.broadcasted_iota(I32, (N_EXPERTS, n_lanes), 0).astype(F32)
    later = jnp.min(jnp.where((e_blk > blk_e) & (cnt > 0.0), e_blk, float(N_EXPERTS)), axis=0, keepdims=True)
    blk_next = jnp.where(later < N_EXPERTS, later, -1.0)
    return pad_start, blk_e, blk_valid, blk_next


def _route_kernel(x1_ref, g_ref, wr_ref, br_ref, cnt_in_ref, hf_ref, idx_ref, gate_ref, pos_ref, cnt_ref,
                  pstart_ref, blke_ref, blkv_ref, blkn_ref, counts, *, bm):
    i = pl.program_id(0)
    rows = x1_ref.shape[0]

    @pl.when(i == 0)
    def _():
        counts[...] = cnt_in_ref[...]

    h = _rms(x1_ref[...], g_ref[...])
    hf_ref[...] = _pack_halves(h)

    wr = wr_ref[...]
    h_hi = h.astype(BF16)
    h_lo = (h - h_hi.astype(F32)).astype(BF16)
    w_hi = wr.astype(BF16)
    w_lo = (wr - w_hi.astype(F32)).astype(BF16)
    logits = (_mm(w_lo, h_lo, nt=True) + _mm(w_lo, h_hi, nt=True) + _mm(w_hi, h_lo, nt=True)
              + _mm(w_hi, h_hi, nt=True)) + br_ref[...]
    eid = lax.broadcasted_iota(I32, (N_EXPERTS, rows), 0).astype(F32)
    work = logits
    vals, hots = [], []
    for kk in range(TOP_K):
        m = jnp.max(work, axis=0, keepdims=True)
        first = jnp.min(jnp.where(work == m, eid, float(N_EXPERTS)), axis=0, keepdims=True)
        hot = eid == first
        work = jnp.where(hot, -jnp.inf, work)
        vals.append(m)
        hots.append(hot)
        idx_ref[kk:kk + 1, :] = first.astype(I32)
    es = [jnp.exp(vv - vals[0]) for vv in vals]
    den = es[0] + es[1] + es[2] + es[3]
    for kk in range(TOP_K):
        gate_ref[kk:kk + 1, :] = es[kk] / den

    chosen = hots[0] | hots[1] | hots[2] | hots[3]
    before = (lax.broadcasted_iota(I32, (rows, rows), 0) < lax.broadcasted_iota(I32, (rows, rows), 1)).astype(BF16)
    rank = jnp.dot(chosen.astype(BF16), before, preferred_element_type=F32) + counts[...]
    for kk in range(TOP_K):
        pos_ref[kk:kk + 1, :] = jnp.sum(jnp.where(hots[kk], rank, 0.0), axis=0, keepdims=True).astype(I32)
    counts[...] = counts[...] + jnp.sum(chosen.astype(F32), axis=1, keepdims=True)
    cnt_ref[...] = counts[...]

    @pl.when(i == pl.num_programs(0) - 1)
    def _():
        pad_start, blk_e, blk_valid, blk_next = _block_plan(counts[...], bm, blke_ref.shape[1])
        pstart_ref[...] = pad_start.astype(I32)
        blke_ref[...] = blk_e.astype(I32)
        blkv_ref[...] = blk_valid.astype(I32)
        blkn_ref[...] = blk_next.astype(I32)


def _route(x1, g_ffn, wr_t, br, cnt_in, rows, bm, n_blocks):
    n = x1.shape[0]
    assert n % rows == 0
    nb_lanes = -(-n_blocks // LANES) * LANES
    const = lambda shape: pl.BlockSpec(shape, lambda i: (0,) * len(shape))
    tok_blk = pl.BlockSpec((TOP_K, rows), lambda i: (0, i))
    return pl.pallas_call(
        functools.partial(_route_kernel, bm=bm),
        grid=(n // rows,),
        in_specs=[pl.BlockSpec((rows, D_MODEL), lambda i: (i, 0)), const((1, D_MODEL)), const((N_EXPERTS, D_MODEL)),
                  const((N_EXPERTS, 1)), const((N_EXPERTS, 1))],
        out_specs=[pl.BlockSpec((rows, D_MODEL // 2), lambda i: (i, 0)), tok_blk, tok_blk, tok_blk,
                   const((N_EXPERTS, 1)), const((N_EXPERTS, 1)), const((1, nb_lanes)), const((1, nb_lanes)),
                   const((1, nb_lanes))],
        out_shape=[jax.ShapeDtypeStruct((n, D_MODEL // 2), U32), jax.ShapeDtypeStruct((TOP_K, n), I32),
                   jax.ShapeDtypeStruct((TOP_K, n), F32), jax.ShapeDtypeStruct((TOP_K, n), I32),
                   jax.ShapeDtypeStruct((N_EXPERTS, 1), F32), jax.ShapeDtypeStruct((N_EXPERTS, 1), I32),
                   jax.ShapeDtypeStruct((1, nb_lanes), I32), jax.ShapeDtypeStruct((1, nb_lanes), I32),
                   jax.ShapeDtypeStruct((1, nb_lanes), I32)],
        scratch_shapes=[pltpu.VMEM((N_EXPERTS, 1), F32)],
        compiler_params=pltpu.CompilerParams(dimension_semantics=("arbitrary",), vmem_limit_bytes=VMEM_LIMIT),
        name="route",
    )(x1, g_ffn, wr_t, br, cnt_in)


def _sc_mesh():
    return plsc.VectorSubcoreMesh(core_axis_name="core", subcore_axis_name="subcore")


def _sc_worker_id():
    return lax.axis_index("core") * SC_SUBCORES + lax.axis_index("subcore")


def _scatter_rows(xa, xb, dest, n_rows):
    ch = SC_CHUNK
    na, w = xa.shape
    n = na + xb.shape[0]
    nk = dest.shape[0]
    assert na % ch == 0 and n % ch == 0 and dest.shape[1] == n and xb.shape[1] == w and xa.dtype == xb.dtype
    n_chunks = n // ch
    dest_c = dest.reshape(nk, n_chunks, ch).transpose(1, 0, 2).reshape(-1)
    dma = pltpu.SemaphoreType.DMA

    @pl.kernel(out_type=jax.ShapeDtypeStruct((n_rows, w), xa.dtype), mesh=_sc_mesh(),
               scratch_types=[pltpu.VMEM((nk * ch,), I32), pltpu.VMEM((ch, w), xa.dtype), dma] + [dma] * nk)
    def scatter_kernel(xa_hbm, xb_hbm, d_hbm, o_hbm, idx_v, buf, sem_i, *sem_s):
        wid = _sc_worker_id()

        @pl.loop(0, -(-n_chunks // SC_WORKERS))
        def _(j):
            c = j * SC_WORKERS + wid

            @pl.when(c < n_chunks)
            def _():
                load_idx = pltpu.make_async_copy(d_hbm.at[pl.ds(c * (nk * ch), nk * ch)], idx_v, sem_i)
                load_idx.start()

                @pl.when(c < na // ch)
                def _():
                    pltpu.sync_copy(xa_hbm.at[pl.ds(c * ch, ch)], buf)

                @pl.when(c >= na // ch)
                def _():
                    pltpu.sync_copy(xb_hbm.at[pl.ds(c * ch - na, ch)], buf)

                load_idx.wait()
                puts = [pltpu.make_async_copy(buf, o_hbm.at[idx_v.at[pl.ds(kk * ch, ch)]], sem_s[kk])
                        for kk in range(nk)]
                for cp in puts:
                    cp.start()
                for cp in puts:
                    cp.wait()

    return scatter_kernel(xa, xb, dest_c)


def _gather_rows(src, idx):
    ch = SC_CHUNK
    m = idx.shape[0]
    w = src.shape[1]
    per = m // SC_WORKERS
    n_pairs = per // (2 * ch)
    assert m % SC_WORKERS == 0 and per % (2 * ch) == 0
    dma = pltpu.SemaphoreType.DMA

    @pl.kernel(out_type=jax.ShapeDtypeStruct((m, w), src.dtype), mesh=_sc_mesh(),
               scratch_types=[pltpu.VMEM((per,), I32), pltpu.VMEM((ch, w), src.dtype), pltpu.VMEM((ch, w), src.dtype),
                              dma, dma, dma, dma])
    def gather_kernel(s_hbm, i_hbm, o_hbm, idx_v, buf_a, buf_b, sem_ga, sem_gb, sem_wa, sem_wb):
        base = _sc_worker_id() * per
        pltpu.sync_copy(i_hbm.at[pl.ds(base, per)], idx_v)

        def fetch(j, buf, sem):
            return pltpu.make_async_copy(s_hbm.at[idx_v.at[pl.ds(j * ch, ch)]], buf, sem)

        def put(j, buf, sem):
            return pltpu.make_async_copy(buf, o_hbm.at[pl.ds(base + j * ch, ch)], sem)

        fetch(0, buf_a, sem_ga).start()

        @pl.loop(0, n_pairs)
        def _(p):
            j0 = 2 * p
            j1 = j0 + 1

            @pl.when(p > 0)
            def _():
                put(j1 - 2, buf_b, sem_wb).wait()

            fetch(j1, buf_b, sem_gb).start()
            fetch(j0, buf_a, sem_ga).wait()
            put(j0, buf_a, sem_wa).start()
            fetch(j1, buf_b, sem_gb).wait()
            put(j1, buf_b, sem_wb).start()
            put(j0, buf_a, sem_wa).wait()

            @pl.when(p + 1 < n_pairs)
            def _():
                fetch(j0 + 2, buf_a, sem_ga).start()

        put(2 * n_pairs - 1, buf_b, sem_wb).wait()

    return gather_kernel(src, idx)


def _moe_kernel(blk_e_ref, blk_valid_ref, blk_next_ref, xs_ref, wgu_hbm, bgu_ref, wd_hbm, bd_ref, y_ref,
                wgu_f32, wd_f32, wgu_bf, wd_bf, sems, *, e0):
    i = pl.program_id(0)
    e = blk_e_ref[i]
    n_valid = blk_valid_ref[i]
    used = n_valid > 0
    new_expert = (i == 0) | (blk_e_ref[jnp.maximum(i - 1, 0)] != e)

    def weight_copies(expert):
        return (pltpu.make_async_copy(wgu_hbm.at[e0 + expert], wgu_f32, sems.at[0]),
                pltpu.make_async_copy(wd_hbm.at[e0 + expert], wd_f32, sems.at[1]))

    @pl.when(used & (i == 0))
    def _():
        for cp in weight_copies(e):
            cp.start()

    @pl.when(used & new_expert)
    def _():
        for cp in weight_copies(e):
            cp.wait()
        wgu_bf[...] = wgu_f32[...].astype(BF16)
        wd_bf[...] = wd_f32[...].astype(BF16)
        nxt = blk_next_ref[i]

        @pl.when(nxt >= 0)
        def _():
            for cp in weight_copies(nxt):
                cp.start(priority=1)

    half = xs_ref.shape[0] // 2

    def expert_mlp(n_rows):
        valid = lax.broadcasted_iota(I32, (n_rows, xs_ref.shape[1]), 0) < n_valid
        xb = _unpack_halves(jnp.where(valid, xs_ref[0:n_rows, :], jnp.uint32(0))).astype(BF16)
        gu = jnp.dot(xb, wgu_bf[...], preferred_element_type=F32) + bgu_ref[0]
        g = jnp.minimum(gu[:, :D_FF], SWIGLU_LIMIT)
        up = jnp.clip(gu[:, D_FF:], -SWIGLU_LIMIT, SWIGLU_LIMIT)
        act = (up + 1.0) * (g * jax.nn.sigmoid(SWIGLU_ALPHA * g))
        y = jnp.dot(act.astype(BF16), wd_bf[...], preferred_element_type=F32) + bd_ref[0]
        y_ref[0:n_rows, :] = _pack_halves(y)

    @pl.when(n_valid > half)
    def _():
        expert_mlp(2 * half)

    @pl.when(used & (n_valid <= half))
    def _():
        expert_mlp(half)
        y_ref[half:, :] = jnp.zeros((half, y_ref.shape[1]), y_ref.dtype)

    @pl.when(jnp.logical_not(used))
    def _():
        y_ref[...] = jnp.zeros_like(y_ref)


def _moe(xs, blk_e, blk_valid, blk_next, layer, w_gu, b_gu, w_d, b_d):
    n_rows = xs.shape[0]
    bm = MOE_BM
    assert n_rows % bm == 0
    e0 = layer * N_EXPERTS
    n_we = w_gu.shape[0] * w_gu.shape[1]
    any_spec = pl.BlockSpec(memory_space=pl.ANY)
    grid_spec = pltpu.PrefetchScalarGridSpec(
        num_scalar_prefetch=3,
        grid=(n_rows // bm,),
        in_specs=[pl.BlockSpec((bm, D_MODEL // 2), lambda i, be, bv, bn: (i, 0)),
                  any_spec,
                  pl.BlockSpec((1, 1, 2 * D_FF), lambda i, be, bv, bn: (e0 + be[i], 0, 0)),
                  any_spec,
                  pl.BlockSpec((1, 1, D_MODEL), lambda i, be, bv, bn: (e0 + be[i], 0, 0))],
        out_specs=pl.BlockSpec((bm, D_MODEL // 2), lambda i, be, bv, bn: (i, 0)),
        scratch_shapes=[pltpu.VMEM((D_MODEL, 2 * D_FF), F32), pltpu.VMEM((D_FF, D_MODEL), F32),
                        pltpu.VMEM((D_MODEL, 2 * D_FF), BF16), pltpu.VMEM((D_FF, D_MODEL), BF16),
                        pltpu.SemaphoreType.DMA((2,))],
    )
    return pl.pallas_call(
        functools.partial(_moe_kernel, e0=e0),
        grid_spec=grid_spec,
        out_shape=jax.ShapeDtypeStruct((n_rows, D_MODEL // 2), U32),
        compiler_params=pltpu.CompilerParams(dimension_semantics=("arbitrary",), vmem_limit_bytes=VMEM_LIMIT),
        name="moe_experts",
    )(blk_e, blk_valid, blk_next, xs, w_gu.reshape(n_we, D_MODEL, 2 * D_FF), b_gu.reshape(n_we, 1, 2 * D_FF),
      w_d.reshape(n_we, D_FF, D_MODEL), b_d.reshape(n_we, 1, D_MODEL))


def _ple_kernel(x1_ref, y0_ref, y1_ref, y2_ref, y3_ref, gates_ref, p_ref, g_ref, wg_ref, wp_ref, *rest):
    out_ref, wg_bf, wp_bf = rest[-3:]

    @pl.when(pl.program_id(0) == 0)
    def _():
        wg_bf[...] = wg_ref[...].astype(BF16)
        wp_bf[...] = wp_ref[...].astype(BF16)

    x2 = x1_ref[...]
    gates = gates_ref[...]
    for kk, y_ref in enumerate((y0_ref, y1_ref, y2_ref, y3_ref)):
        x2 = x2 + _unpack_halves(y_ref[...]) * gates[:, kk:kk + 1]
    hp = _rms(x2, g_ref[...]).astype(BF16)
    gate = 0.5 * jnp.tanh(0.5 * jnp.dot(hp, wg_bf[...], preferred_element_type=F32)) + 0.5
    pp = jnp.dot(p_ref[...].astype(BF16), wp_bf[...], preferred_element_type=F32)
    out_ref[...] = x2 + gate * pp


def _ple(x1, tok0, n, y_tok, y0, gates_t, p_all, p0, g_ple, w_gate, w_proj, rows, out_into=None):
    assert n % rows == 0 and tok0 % rows == 0 and y0 % rows == 0 and p0 % rows == 0
    const = lambda shape: pl.BlockSpec(shape, lambda i: (0,) * len(shape))
    tok_blk = lambda width: pl.BlockSpec((rows, width), lambda i: (tok0 // rows + i, 0))
    y_blk = lambda kk: pl.BlockSpec((rows, D_MODEL // 2), lambda i: ((y0 + kk * n) // rows + i, 0))
    operands = (x1, y_tok, y_tok, y_tok, y_tok, gates_t, p_all, g_ple, w_gate, w_proj)
    extra_specs, extra_args, aliases = [], [], {}
    if out_into is not None:
        assert out_into.shape == x1.shape
        extra_specs, extra_args, aliases = [pl.BlockSpec(memory_space=pl.ANY)], [out_into], {len(operands): 0}
    return pl.pallas_call(
        _ple_kernel,
        grid=(n // rows,),
        in_specs=[tok_blk(D_MODEL), y_blk(0), y_blk(1), y_blk(2), y_blk(3), tok_blk(TOP_K),
                  pl.BlockSpec((rows, PLE_DIM), lambda i: (p0 // rows + i, 0)),
                  const((1, D_MODEL)), const((D_MODEL, D_MODEL)), const((PLE_DIM, D_MODEL))] + extra_specs,
        out_specs=tok_blk(D_MODEL),
        out_shape=jax.ShapeDtypeStruct(x1.shape, F32),
        scratch_shapes=[pltpu.VMEM((D_MODEL, D_MODEL), BF16), pltpu.VMEM((PLE_DIM, D_MODEL), BF16)],
        input_output_aliases=aliases,
        compiler_params=pltpu.CompilerParams(dimension_semantics=("arbitrary",), vmem_limit_bytes=VMEM_LIMIT),
        name="combine_ple",
    )(*operands, *extra_args)


def _rope_tables_const(seq):
    half = ROT_DIM // 2
    d = np.arange(LANES) % HEAD_DIM
    inv = (np.float64(ROPE_THETA) ** (-np.arange(half, dtype=np.float64) / half)).astype(np.float32)
    inv_lane = np.where(d < ROT_DIM, inv[d % half], np.float32(0.0)).astype(np.float32)
    ang = np.arange(seq, dtype=np.float32)[:, None] * inv_lane[None, :]
    cos = np.cos(ang.astype(np.float64)).astype(np.float32)
    sin = np.sin(ang.astype(np.float64)).astype(np.float32)
    sin_a = np.where((d >= half) & (d < ROT_DIM), sin, np.float32(0.0))
    sin_b = np.where(d < half, -sin, np.float32(0.0))
    return tuple(jnp.asarray(a) for a in (cos, sin_a, sin_b))


def _rope_tables(pos):
    half = ROT_DIM // 2
    d = np.arange(LANES) % HEAD_DIM
    inv = ROPE_THETA ** (-jnp.arange(half, dtype=F32) / half)
    inv_lane = jnp.where(d < ROT_DIM, inv[d % half], 0.0)
    ang = pos.astype(F32)[:, None] * inv_lane[None, :]
    cos, sin = jnp.cos(ang), jnp.sin(ang)
    sin_a = jnp.where((d >= half) & (d < ROT_DIM), sin, 0.0)
    sin_b = jnp.where(d < half, -sin, 0.0)
    return cos, sin_a, sin_b


def _layer(layer, xp, xs, ck, cv, st, p_prompt_all, p_sample_all, past_len, refine_tail, tabs_p, tabs_s, finish_state,
           norm_attn, w_in, q_norm, k_norm, attn_sinks, w_pool, pool_scale, w_out,
           norm_ffn, w_router, b_router, w_gate_up_all, b_gate_up_all, w_down_all, b_down_all,
           norm_ple, w_ple_gate, w_ple_proj):
    t, ns = xp.shape[0], xs.shape[0]
    g_attn = norm_attn.reshape(1, D_MODEL)
    qn = jnp.tile(q_norm, 2).reshape(1, LANES)
    kn = jnp.tile(k_norm, 2).reshape(1, LANES)
    lane = np.arange(LANES)
    hmean = jnp.asarray((lane[:, None] // HEAD_DIM == lane[None, :] // HEAD_DIM) / HEAD_DIM, F32)
    pscale = pool_scale.reshape(1, POOL_WIDTH)
    sink8 = jnp.broadcast_to(attn_sinks.reshape(2, Q_TILES).T.reshape(8, 1), (8, LANES))

    mix_args = (g_attn, w_in, qn, kn, hmean, w_pool, pscale, w_out)
    x1p, nk_p, nv_p, nu_p = _mixer_prompt(xp, 0, MIX_ROWS, BF16, attn_sinks, tabs_p, *mix_args)
    if refine_tail:
        x1p, nk_p, nv_p, nu_p = _mixer_prompt(xp, t - 2 * MIX_TAIL_ROWS, MIX_TAIL_ROWS, F32, attn_sinks, tabs_p,
                                              *mix_args, x1_into=x1p)
    st_t = jnp.transpose(st, (1, 0, 2))
    x1s, nk_s, nv_s, nst_t = _mixer_sample(xs, ck.reshape(ns, -1, KV_WIDTH), cv.reshape(ns, -1, KV_WIDTH), st_t,
                                           past_len, tabs_s, sink8, *mix_args)

    g_ffn = norm_ffn.reshape(1, D_MODEL)
    wr_t = w_router.T
    br = b_router.reshape(N_EXPERTS, 1)
    n_tok = t + ns
    bm = MOE_BM
    n_blocks = -(-(n_tok * TOP_K + N_EXPERTS * (bm - 1)) // bm)
    hf_p, idx_p, gate_p, pos_p, cnt_p = _route(x1p, g_ffn, wr_t, br, jnp.zeros((N_EXPERTS, 1), F32),
                                               ROUTE_ROWS, bm, n_blocks)[:5]
    hf_s, idx_s, gate_s, pos_s, _, pad_start, blk_e, blk_valid, blk_next = _route(x1s, g_ffn, wr_t, br, cnt_p, ns,
                                                                                  bm, n_blocks)
    idx = jnp.concatenate([idx_p, idx_s], axis=1)
    pos = jnp.concatenate([pos_p, pos_s], axis=1)
    blk_e, blk_valid, blk_next = (a[0, :n_blocks] for a in (blk_e, blk_valid, blk_next))
    state = finish_state(nk_p, nv_p, nu_p[POOL_HALO - POOL_PREFIX:], nk_s, nv_s, jnp.transpose(nst_t, (1, 0, 2)))
    blk_e, state = lax.optimization_barrier((blk_e, state))
    start_of = jnp.sum(jnp.where(idx[None] == jnp.arange(N_EXPERTS, dtype=I32)[:, None, None],
                                 pad_start.reshape(N_EXPERTS, 1, 1), 0), axis=0)
    dest = start_of + pos

    xs_rows = _scatter_rows(hf_p, hf_s, dest, n_blocks * bm)
    y = _moe(xs_rows, blk_e, blk_valid, blk_next, layer, w_gate_up_all, b_gate_up_all, w_down_all, b_down_all)

    tq = t // COMBINE_PARTS
    unit = SC_WORKERS * SC_CHUNK * 2
    assert t % COMBINE_PARTS == 0 and (TOP_K * tq) % unit == 0 and tq % PLE_ROWS == 0
    g_ple = norm_ple.reshape(1, D_MODEL)
    gates_p = gate_p.T
    yp = None
    for part in range(COMBINE_PARTS):
        back = dest[:, part * tq:(part + 1) * tq].reshape(-1)
        if part == COMBINE_PARTS - 1:
            back = jnp.concatenate([back, dest[:, t:].reshape(-1)])
            back = jnp.concatenate([back, jnp.arange(-back.shape[0] % unit, dtype=I32)])
        y_part = _gather_rows(y, back)
        yp = _ple(x1p, part * tq, tq, y_part, 0, gates_p, p_prompt_all, layer * t + part * tq, g_ple,
                  w_ple_gate, w_ple_proj, PLE_ROWS, out_into=yp)
    ys = _ple(x1s, 0, ns, y_part, TOP_K * tq, gate_s.T, p_sample_all, layer * ns, g_ple, w_ple_gate, w_ple_proj, ns)
    return yp, ys, state


def kernel(x_prompt, x_sample, cache_k, cache_v, state_pool, p_prompt, p_sample, norm_attn, w_in, q_norm, k_norm,
           attn_sinks, w_pool, pool_scale, w_out, norm_ffn, w_router, b_router, w_gate_up, b_gate_up, w_down, b_down,
           norm_ple, w_ple_gate, w_ple_proj):
    depth = norm_attn.shape[0]
    batch, seq, d = x_prompt.shape
    ns, dec_seq, _ = x_sample.shape
    wb = cache_k.shape[2]
    assert batch == 1 and dec_seq == 1 and d == D_MODEL and wb == WINDOW
    assert cache_k.shape[3:] == (N_KV_HEADS, HEAD_DIM) and state_pool.shape[2:] == (POOL_PREFIX, POOL_WIDTH)
    past_len = PAST_LEN
    yp = x_prompt.reshape(seq, d)
    ys = x_sample.reshape(ns, d)
    p_prompt_all = p_prompt.reshape(depth * seq, PLE_DIM)
    p_sample_all = p_sample.reshape(depth * ns, PLE_DIM)
    tabs_p = _rope_tables_const(seq)
    tabs_s = _rope_tables(jnp.full((1,), past_len))
    kv_p, kv_s = (1, WINDOW, N_KV_HEADS, HEAD_DIM), (ns, wb, N_KV_HEADS, HEAD_DIM)
    state_shapes = (kv_p, kv_p, (1, POOL_PREFIX, POOL_WIDTH), kv_s, kv_s, (ns, POOL_PREFIX, POOL_WIDTH))
    per_layer = []

    def shaped(*state):
        return tuple(a.reshape(shp) for a, shp in zip(state, state_shapes))

    def stacked(*state):
        prevs = list(zip(*per_layer)) or [()] * len(state_shapes)
        return tuple(jnp.stack(list(prev) + [cur]) for prev, cur in zip(prevs, shaped(*state)))

    for i in range(depth):
        yp, ys, state = _layer(i, yp, ys, cache_k[i], cache_v[i], state_pool[i], p_prompt_all, p_sample_all, past_len,
                               i < depth - 1, tabs_p, tabs_s, shaped if i < depth - 1 else stacked,
                               norm_attn[i], w_in[i], q_norm[i], k_norm[i], attn_sinks[i], w_pool[i], pool_scale[i],
                               w_out[i], norm_ffn[i], w_router[i], b_router[i], w_gate_up, b_gate_up, w_down, b_down,
                               norm_ple[i], w_ple_gate[i], w_ple_proj[i])
        per_layer.append(state)
    return (yp.reshape(batch, seq, d), ys.reshape(ns, dec_seq, d)) + tuple(state)
```

```python
import functools

import jax
import jax.numpy as jnp
import numpy as np
from jax import lax
from jax.experimental import pallas as pl
from jax.experimental.pallas import tpu as pltpu
from jax.experimental.pallas import tpu_sc as plsc

F32 = jnp.float32
BF16 = jnp.bfloat16
U32 = jnp.uint32
I32 = jnp.int32

D_MODEL = 1024
HEAD_DIM = 64
N_HEADS = 8
N_KV_HEADS = 2
GROUP = N_HEADS // N_KV_HEADS
ATTN_WIDTH = N_HEADS * HEAD_DIM
KV_WIDTH = N_KV_HEADS * HEAD_DIM
POOL_WIDTH = 512
POOL_WINDOWS = (2, 4, 8, 16)
POOL_GC = POOL_WIDTH // len(POOL_WINDOWS)
POOL_PREFIX = max(POOL_WINDOWS) - 1
POOL_HALO = POOL_PREFIX + 1
POOL_PAD = 8
IN_WIDTH = ATTN_WIDTH + 2 * KV_WIDTH + POOL_WIDTH
WINDOW = 128
ROPE_THETA = 500000.0
ROT_DIM = HEAD_DIM // 4
N_EXPERTS = 32
TOP_K = 4
D_FF = 1024
SWIGLU_ALPHA = 1.702
SWIGLU_LIMIT = 7.0
PLE_DIM = 256
PAST_LEN = 16384
EPS = 1e-5
NEG_INF = -1e30

LANES = 128
Q_TILES = ATTN_WIDTH // LANES

MIX_ROWS = 1024
MIX_TAIL_ROWS = 128
SAMPLE_CHUNK = 16
ROUTE_ROWS = 1024
MOE_BM = 512
SC_CORES = 2
SC_SUBCORES = 16
SC_WORKERS = SC_CORES * SC_SUBCORES
SC_CHUNK = 64
PLE_ROWS = 1024
COMBINE_PARTS = 1
VMEM_LIMIT = 56 * 1024 * 1024


def _rms(x, g):
    return x * lax.rsqrt(jnp.mean(x * x, axis=-1, keepdims=True) + EPS) * g


def _pack_halves(x):
    w = x.shape[1] // 2
    lo = lax.bitcast_convert_type(x[:, :w].astype(BF16).astype(F32), U32) >> 16
    hi = lax.bitcast_convert_type(x[:, w:].astype(BF16).astype(F32), U32) & jnp.uint32(0xFFFF0000)
    return lo | hi


def _unpack_halves(packed):
    lo = lax.bitcast_convert_type(packed << 16, F32)
    hi = lax.bitcast_convert_type(packed & jnp.uint32(0xFFFF0000), F32)
    return jnp.concatenate([lo, hi], axis=1)


def _mm(a, b, nt=False):
    dims = (((1,), (1 if nt else 0,)), ((), ()))
    if b.dtype == F32:
        return lax.dot_general(a.astype(F32), b, dims, preferred_element_type=F32, precision=lax.Precision.HIGHEST)
    return lax.dot_general(a.astype(BF16), b, dims, preferred_element_type=F32)


def _head_norm_rope(t, hmean, gain, cos, sin_a, sin_b):
    t = t * lax.rsqrt(_mm(t * t, hmean) + EPS) * gain
    return t * cos + pltpu.roll(t, ROT_DIM // 2, axis=1) * sin_a + pltpu.roll(t, LANES - ROT_DIM // 2, axis=1) * sin_b


def _prep_mixer_weights(win_ref, wout_ref, win_c, wout_c):
    cdt = win_c.dtype
    for j in range(Q_TILES):
        for s in range(2):
            src = (j + Q_TILES * s) * HEAD_DIM
            dst = j * LANES + s * HEAD_DIM
            win_c[:, dst:dst + HEAD_DIM] = win_ref[:, src:src + HEAD_DIM].astype(cdt)
            wout_c[dst:dst + HEAD_DIM, :] = wout_ref[src:src + HEAD_DIM, :].astype(cdt)
    win_c[:, ATTN_WIDTH:] = win_ref[:, ATTN_WIDTH:].astype(cdt)
    wout_c[ATTN_WIDTH:, :] = wout_ref[ATTN_WIDTH:, :].astype(cdt)


def _mixer_prompt_kernel(sinks_ref, x_ref, cos_ref, sa_ref, sb_ref, g_ref, win_ref, qn_ref, kn_ref, hm_ref,
                         wpool_ref, pscale_ref, wout_ref, *rest, row_offset, aliased):
    (x1_ref, klast_ref, vlast_ref, ulast_ref, kprev, vprev, uext, mix, win_c, wout_c,
     *lvl) = rest[1:] if aliased else rest
    i = pl.program_id(0)
    rows = x_ref.shape[0]
    n_sub = rows // WINDOW
    cdt = win_c.dtype
    row0 = row_offset + i * rows

    @pl.when(i == 0)
    def _():
        _prep_mixer_weights(win_ref, wout_ref, win_c, wout_c)
        kprev[...] = jnp.zeros_like(kprev)
        vprev[...] = jnp.zeros_like(vprev)
        uext[0:POOL_PAD + POOL_HALO, :] = jnp.zeros((POOL_PAD + POOL_HALO, POOL_WIDTH), F32)
        for buf in lvl:
            buf[0:POOL_PAD, :] = jnp.zeros((POOL_PAD, POOL_GC), F32)

    x = x_ref[...]
    hn = _rms(x, g_ref[...]).astype(cdt)
    kv_end = ATTN_WIDTH + 2 * KV_WIDTH
    proj = _mm(hn, win_c[:, :kv_end])
    u = _mm(hn, win_c[:, kv_end:])
    cos, sin_a, sin_b = cos_ref[...], sa_ref[...], sb_ref[...]

    n_t = Q_TILES + 1
    t_all = jnp.concatenate([proj[:, j * LANES:(j + 1) * LANES] for j in range(n_t)], axis=0)
    t3 = (t_all * lax.rsqrt(_mm(t_all * t_all, hm_ref[...].astype(cdt)) + EPS)).reshape(n_t, rows, LANES)
    t3 = jnp.concatenate([t3[:Q_TILES] * (qn_ref[...] * HEAD_DIM ** -0.5), t3[Q_TILES:] * kn_ref[...]], axis=0)
    t2 = t3.reshape(n_t * rows, LANES)
    t3 = (t3 * cos + pltpu.roll(t2, ROT_DIM // 2, axis=1).reshape(n_t, rows, LANES) * sin_a
          + pltpu.roll(t2, LANES - ROT_DIM // 2, axis=1).reshape(n_t, rows, LANES) * sin_b)
    q3 = t3[:Q_TILES]
    k = t3[Q_TILES]
    v = proj[:, ATTN_WIDTH + KV_WIDTH:ATTN_WIDTH + 2 * KV_WIDTH]
    klast_ref[...] = k[rows - WINDOW:, :]
    vlast_ref[...] = v[rows - WINDOW:, :]
    ulast_ref[...] = u[rows - POOL_HALO:, :]
    k_c = k.astype(cdt)
    v_c = jnp.concatenate([v.astype(cdt), jnp.ones((rows, LANES), cdt)], axis=1)
    v_first = jnp.concatenate([vprev[...], jnp.ones((WINDOW, LANES), cdt)], axis=1)

    lane = lax.broadcasted_iota(I32, (WINDOW, LANES), 1)
    left = (lane < HEAD_DIM)[None]
    qi = lax.broadcasted_iota(I32, (WINDOW, 2 * WINDOW), 0)
    kj = lax.broadcasted_iota(I32, (WINDOW, 2 * WINDOW), 1)
    band = (kj - qi >= 1) & (kj - qi <= WINDOW)
    sink3 = jnp.concatenate([jnp.full((1, 1, 1), sinks_ref[j + Q_TILES * s], F32)
                             for j in range(Q_TILES) for s in range(2)], axis=0)
    n_g = 2 * Q_TILES
    fill = jnp.where(kj[0:1][None] == 0, sink3, NEG_INF)
    slot = lax.broadcasted_iota(I32, (2 * WINDOW, 2 * LANES), 0) == 0
    slot_v = slot & (lax.broadcasted_iota(I32, (2 * WINDOW, 2 * LANES), 1) < LANES)

    v_cats, masks, scores = [], [], []
    for c in range(n_sub):
        r0 = c * WINDOW
        if c == 0:
            k_cat = jnp.concatenate([kprev[...], k_c[0:WINDOW]], axis=0)
            v_cat = jnp.concatenate([v_first, v_c[0:WINDOW]], axis=0)
            masks.append(band & (kj + (row0 - WINDOW) >= 0))
        else:
            k_cat = k_c[r0 - WINDOW:r0 + WINDOW]
            v_cat = v_c[r0 - WINDOW:r0 + WINDOW]
            masks.append(band)
        v_cats.append(jnp.where(slot_v, jnp.zeros_like(v_cat), v_cat))
        q_c = q3[:, r0:r0 + WINDOW, :]
        q_all = jnp.concatenate([jnp.where(left, q_c, 0.0), jnp.where(left, 0.0, q_c)], axis=1)
        scores.append(_mm(q_all.reshape(n_g * WINDOW, LANES), k_cat, nt=True))
    kprev[...] = k_c[rows - WINDOW:]
    vprev[...] = v[rows - WINDOW:].astype(cdt)

    probs = []
    for c in range(n_sub):
        s = jnp.where(masks[c][None], scores[c].reshape(n_g, WINDOW, 2 * WINDOW), fill)
        m = jnp.max(s, axis=-1, keepdims=True)
        probs.append(jnp.exp(s - m).astype(cdt).reshape(n_g * WINDOW, 2 * WINDOW))
    applied = [_mm(probs[c], v_cats[c]) for c in range(n_sub)]
    for c in range(n_sub):
        r0 = c * WINDOW
        o = (applied[c][:, :LANES] / applied[c][:, LANES:]).reshape(Q_TILES, 2 * WINDOW, LANES)
        a = jnp.where(left, o[:, :WINDOW], o[:, WINDOW:])
        for j in range(Q_TILES):
            mix[r0:r0 + WINDOW, j * LANES:(j + 1) * LANES] = a[j].astype(cdt)

    base = POOL_PAD + POOL_HALO
    ext = base + rows
    uext[base:ext, :] = u
    pos1 = (lax.broadcasted_iota(I32, (rows, 1), 0) + row0 + 1).astype(F32)
    lvl_of = {1: lvl[0:1], 2: lvl[1:3], 3: lvl[3:5]}
    for gi, w in enumerate(POOL_WINDOWS):
        cols = slice(gi * POOL_GC, (gi + 1) * POOL_GC)
        src, src_cols = uext, cols
        for level in range(1, gi + 2):
            sft = 1 << (level - 1)
            if level <= gi:
                dst = lvl_of[gi][(level - 1) % 2]
                dst[POOL_PAD:ext, :] = src[POOL_PAD:ext, src_cols] + src[POOL_PAD - sft:ext - sft, src_cols]
                src, src_cols = dst, slice(None)
            else:
                wsum = src[base:ext, src_cols] + src[base - sft:ext - sft, src_cols]
        d = wsum / jnp.minimum(pos1, float(w)) - u[:, cols]
        y = _mm(d, wpool_ref[gi].astype(cdt)) * pscale_ref[:, cols]
        mix[:, ATTN_WIDTH + gi * POOL_GC:ATTN_WIDTH + (gi + 1) * POOL_GC] = y.astype(cdt)
    uext[POOL_PAD:base, :] = u[rows - POOL_HALO:, :]

    x1_ref[...] = (x + _mm(mix[:, :ATTN_WIDTH], wout_c[:ATTN_WIDTH, :])
                   + _mm(mix[:, ATTN_WIDTH:], wout_c[ATTN_WIDTH:, :]))


def _mixer_prompt(x_full, row_offset, rows, cdt, sinks, tabs, g_attn, w_in, qn, kn, hmean, w_pool, pscale, w_out,
                  x1_into=None):
    t = x_full.shape[0] - row_offset
    assert t % rows == 0 and row_offset % rows == 0 and rows % WINDOW == 0 and rows >= POOL_HALO
    blk0, n_steps = row_offset // rows, t // rows
    const = lambda shape: pl.BlockSpec(shape, lambda i, *_: (0,) * len(shape))
    row_blk = lambda width: pl.BlockSpec((rows, width), lambda i, *_: (blk0 + i, 0))
    aliased = x1_into is not None
    operands = (sinks, x_full, *tabs, g_attn, w_in, qn, kn, hmean, w_pool, pscale, w_out)
    if aliased:
        assert x1_into.shape == x_full.shape
        x1_spec = pl.BlockSpec((rows, D_MODEL), lambda i, *_: (blk0 + n_steps - 1, 0))
        x1_shape = x1_into.shape
        extra_specs, extra_args, aliases = [pl.BlockSpec(memory_space=pl.ANY)], [x1_into], {len(operands): 0}
    else:
        x1_spec = pl.BlockSpec((rows, D_MODEL), lambda i, *_: (i, 0))
        x1_shape = (t, D_MODEL)
        extra_specs, extra_args, aliases = [], [], {}
    grid_spec = pltpu.PrefetchScalarGridSpec(
        num_scalar_prefetch=1,
        grid=(n_steps,),
        in_specs=[row_blk(D_MODEL), row_blk(LANES), row_blk(LANES), row_blk(LANES),
                  const((1, D_MODEL)), const((D_MODEL, IN_WIDTH)), const((1, LANES)), const((1, LANES)),
                  const((LANES, LANES)), const((len(POOL_WINDOWS), POOL_GC, POOL_GC)), const((1, POOL_WIDTH)),
                  const((D_MODEL, D_MODEL))] + extra_specs,
        out_specs=[x1_spec, const((WINDOW, KV_WIDTH)), const((WINDOW, KV_WIDTH)),
                   const((POOL_HALO, POOL_WIDTH))],
        scratch_shapes=[pltpu.VMEM((WINDOW, KV_WIDTH), cdt), pltpu.VMEM((WINDOW, KV_WIDTH), cdt),
                        pltpu.VMEM((POOL_PAD + POOL_HALO + rows, POOL_WIDTH), F32), pltpu.VMEM((rows, D_MODEL), cdt),
                        pltpu.VMEM((D_MODEL, IN_WIDTH), cdt), pltpu.VMEM((D_MODEL, D_MODEL), cdt)]
        + [pltpu.VMEM((POOL_PAD + POOL_HALO + rows, POOL_GC), F32)] * 5,
    )
    return pl.pallas_call(
        functools.partial(_mixer_prompt_kernel, row_offset=row_offset, aliased=aliased),
        grid_spec=grid_spec,
        out_shape=[jax.ShapeDtypeStruct(x1_shape, F32), jax.ShapeDtypeStruct((WINDOW, KV_WIDTH), F32),
                   jax.ShapeDtypeStruct((WINDOW, KV_WIDTH), F32), jax.ShapeDtypeStruct((POOL_HALO, POOL_WIDTH), F32)],
        input_output_aliases=aliases,
        compiler_params=pltpu.CompilerParams(dimension_semantics=("arbitrary",), vmem_limit_bytes=VMEM_LIMIT),
        name="mixer_prompt",
    )(*operands, *extra_args)


def _mixer_sample_kernel(x_ref, ck_ref, cv_ref, st_ref, cos_ref, sa_ref, sb_ref, sink8_ref, g_ref, win_ref, qn_ref,
                         kn_ref, hm_ref, wpool_ref, pscale_ref, wout_ref,
                         x1_ref, nk_ref, nv_ref, nst_ref, win_c, wout_c, *, pos):
    nb = x_ref.shape[0]
    wb = ck_ref.shape[1]

    @pl.when(pl.program_id(0) == 0)
    def _():
        _prep_mixer_weights(win_ref, wout_ref, win_c, wout_c)

    x = x_ref[...]
    h = _rms(x, g_ref[...]).astype(BF16)
    proj = jnp.dot(h, win_c[...], preferred_element_type=F32)
    cos, sin_a, sin_b = cos_ref[...], sa_ref[...], sb_ref[...]
    hmean = hm_ref[...].astype(BF16)
    k = _head_norm_rope(proj[:, ATTN_WIDTH:ATTN_WIDTH + KV_WIDTH], hmean, kn_ref[...], cos, sin_a, sin_b)
    v = proj[:, ATTN_WIDTH + KV_WIDTH:ATTN_WIDTH + 2 * KV_WIDTH]
    u = proj[:, ATTN_WIDTH + 2 * KV_WIDTH:]

    nk_ref[:, 0:wb - 1, :] = ck_ref[:, 1:wb, :]
    nv_ref[:, 0:wb - 1, :] = cv_ref[:, 1:wb, :]
    for b in range(nb):
        nk_ref[b, wb - 1:wb, :] = k[b:b + 1, :]
        nv_ref[b, wb - 1:wb, :] = v[b:b + 1, :]

    r8 = lax.broadcasted_iota(I32, (nb * 8, LANES), 0)
    lane8 = lax.broadcasted_iota(I32, (nb * 8, LANES), 1)
    keep = (lane8 < HEAD_DIM) == (r8 % 2 == 0)
    rep = (lax.broadcasted_iota(I32, (nb * 8, nb), 0) // 8 == lax.broadcasted_iota(I32, (nb * 8, nb), 1)).astype(BF16)
    q8 = jnp.zeros((nb * 8, LANES), F32)
    scale = HEAD_DIM ** -0.5
    for j in range(Q_TILES):
        qt = _head_norm_rope(proj[:, j * LANES:(j + 1) * LANES], hmean, qn_ref[...], cos, sin_a, sin_b) * scale
        qrep = jnp.dot(rep, qt.astype(BF16), preferred_element_type=F32)
        q8 = jnp.where(keep & ((r8 % 8) // 2 == j), qrep, q8)
    q8 = q8.astype(BF16)

    sink = jnp.concatenate([sink8_ref[:, 0:1]] * nb, axis=0)
    ones_bf = jnp.ones((wb, LANES), BF16)
    assert pos >= wb - 1 and wb <= WINDOW
    s = jnp.concatenate([_mm(q8[b * 8:(b + 1) * 8], nk_ref[b].astype(BF16), nt=True) for b in range(nb)], axis=0)
    m = jnp.maximum(jnp.max(s, axis=-1, keepdims=True), sink)
    e = jnp.exp(s - m).astype(BF16)
    den = _mm(e, ones_bf) + jnp.exp(sink - m)
    o = jnp.concatenate([_mm(e[b * 8:(b + 1) * 8], nv_ref[b].astype(BF16)) for b in range(nb)], axis=0) / den
    o8m = jnp.where(keep, o, 0.0).astype(BF16)

    a_tiles = []
    sel_r = lax.broadcasted_iota(I32, (nb, nb * 8), 1)
    sel_b = lax.broadcasted_iota(I32, (nb, nb * 8), 0)
    for j in range(Q_TILES):
        sel = ((sel_r // 8 == sel_b) & ((sel_r % 8) // 2 == j)).astype(BF16)
        a_tiles.append(jnp.dot(sel, o8m, preferred_element_type=F32))

    z_tiles = []
    for gi, w in enumerate(POOL_WINDOWS):
        cols = slice(gi * POOL_GC, (gi + 1) * POOL_GC)
        wsum = u[:, cols]
        for sft in range(1, w):
            wsum = wsum + st_ref[POOL_PREFIX - sft, :, cols]
        d = wsum / float(min(pos + 1, w)) - u[:, cols]
        z_tiles.append(jnp.dot(d.astype(BF16), wpool_ref[gi].astype(BF16), preferred_element_type=F32)
                       * pscale_ref[:, cols])
    nst_ref[0:POOL_PREFIX - 1] = st_ref[1:POOL_PREFIX]
    nst_ref[POOL_PREFIX - 1] = u

    mixv = jnp.concatenate(a_tiles + z_tiles, axis=1).astype(BF16)
    x1_ref[...] = x + jnp.dot(mixv, wout_c[...], preferred_element_type=F32)


def _mixer_sample(x, ck, cv, st, pos, tabs, sink8, g_attn, w_in, qn, kn, hmean, w_pool, pscale, w_out):
    n, wb = ck.shape[0], ck.shape[1]
    nb = SAMPLE_CHUNK
    assert n % nb == 0
    const = lambda shape: pl.BlockSpec(shape, lambda i: (0,) * len(shape))
    cache_blk = pl.BlockSpec((nb, wb, KV_WIDTH), lambda i: (i, 0, 0))
    st_blk = pl.BlockSpec((POOL_PREFIX, nb, POOL_WIDTH), lambda i: (0, i, 0))
    x_blk = pl.BlockSpec((nb, D_MODEL), lambda i: (i, 0))
    return pl.pallas_call(
        functools.partial(_mixer_sample_kernel, pos=pos),
        grid=(n // nb,),
        in_specs=[x_blk, cache_blk, cache_blk, st_blk, const((1, LANES)), const((1, LANES)), const((1, LANES)),
                  const((8, LANES)), const((1, D_MODEL)), const((D_MODEL, IN_WIDTH)), const((1, LANES)),
                  const((1, LANES)), const((LANES, LANES)), const((len(POOL_WINDOWS), POOL_GC, POOL_GC)),
                  const((1, POOL_WIDTH)), const((D_MODEL, D_MODEL))],
        out_specs=[x_blk, cache_blk, cache_blk, st_blk],
        out_shape=[jax.ShapeDtypeStruct((n, D_MODEL), F32), jax.ShapeDtypeStruct(ck.shape, F32),
                   jax.ShapeDtypeStruct(cv.shape, F32), jax.ShapeDtypeStruct(st.shape, F32)],
        scratch_shapes=[pltpu.VMEM((D_MODEL, IN_WIDTH), BF16), pltpu.VMEM((D_MODEL, D_MODEL), BF16)],
        compiler_params=pltpu.CompilerParams(dimension_semantics=("arbitrary",), vmem_limit_bytes=VMEM_LIMIT),
        name="mixer_sample",
    )(x, ck, cv, st, *tabs, sink8, g_attn, w_in, qn, kn, hmean, w_pool, pscale, w_out)


def _block_plan(cnt, bm, n_lanes):
    e_sub = lax.broadcasted_iota(I32, (N_EXPERTS, LANES), 0)
    e_lane = lax.broadcasted_iota(I32, (N_EXPERTS, LANES), 1)
    padded = jnp.floor((cnt + (bm - 1)) / bm) * bm
    padded_lane = jnp.sum(jnp.where(e_sub == e_lane, padded, 0.0), axis=0, keepdims=True)
    pad_end = jnp.sum(jnp.where(e_lane <= e_sub, padded_lane, 0.0), axis=1, keepdims=True)
    pad_start = pad_end - padded
    blk_start = lax.broadcasted_iota(I32, (N_EXPERTS, n_lanes), 1).astype(F32) * bm
    blk_e = jnp.minimum(jnp.sum((pad_end <= blk_start).astype(F32), axis=0, keepdims=True), N_EXPERTS - 1.0)
    mine = lax.broadcasted_iota(I32, (N_EXPERTS, n_lanes), 0).astype(F32) == blk_e
    last = jnp.sum(jnp.where(mine, pad_start + cnt, 0.0), axis=0, keepdims=True)
    blk_valid = jnp.clip(last - blk_start[0:1], 0.0, float(bm))
    e_blk = lax.broadcasted_iota(I32, (N_EXPERTS, n_lanes), 0).astype(F32)
    later = jnp.min(jnp.where((e_blk > blk_e) & (cnt > 0.0), e_blk, float(N_EXPERTS)), axis=0, keepdims=True)
    blk_next = jnp.where(later < N_EXPERTS, later, -1.0)
    return pad_start, blk_e, blk_valid, blk_next


def _route_kernel(x1_ref, g_ref, wr_ref, br_ref, cnt_in_ref, hf_ref, idx_ref, gate_ref, pos_ref, cnt_ref,
                  pstart_ref, blke_ref, blkv_ref, blkn_ref, counts, *, bm):
    i = pl.program_id(0)
    rows = x1_ref.shape[0]

    @pl.when(i == 0)
    def _():
        counts[...] = cnt_in_ref[...]

    h = _rms(x1_ref[...], g_ref[...])
    hf_ref[...] = _pack_halves(h)

    wr = wr_ref[...]
    h_hi = h.astype(BF16)
    h_lo = (h - h_hi.astype(F32)).astype(BF16)
    w_hi = wr.astype(BF16)
    w_lo = (wr - w_hi.astype(F32)).astype(BF16)
    logits = (_mm(w_lo, h_lo, nt=True) + _mm(w_lo, h_hi, nt=True) + _mm(w_hi, h_lo, nt=True)
              + _mm(w_hi, h_hi, nt=True)) + br_ref[...]
    eid = lax.broadcasted_iota(I32, (N_EXPERTS, rows), 0).astype(F32)
    work = logits
    vals, hots = [], []
    for kk in range(TOP_K):
        m = jnp.max(work, axis=0, keepdims=True)
        first = jnp.min(jnp.where(work == m, eid, float(N_EXPERTS)), axis=0, keepdims=True)
        hot = eid == first
        work = jnp.where(hot, -jnp.inf, work)
        vals.append(m)
        hots.append(hot)
        idx_ref[kk:kk + 1, :] = first.astype(I32)
    es = [jnp.exp(vv - vals[0]) for vv in vals]
    den = es[0] + es[1] + es[2] + es[3]
    for kk in range(TOP_K):
        gate_ref[kk:kk + 1, :] = es[kk] / den

    chosen = hots[0] | hots[1] | hots[2] | hots[3]
    before = (lax.broadcasted_iota(I32, (rows, rows), 0) < lax.broadcasted_iota(I32, (rows, rows), 1)).astype(BF16)
    rank = jnp.dot(chosen.astype(BF16), before, preferred_element_type=F32) + counts[...]
    for kk in range(TOP_K):
        pos_ref[kk:kk + 1, :] = jnp.sum(jnp.where(hots[kk], rank, 0.0), axis=0, keepdims=True).astype(I32)
    counts[...] = counts[...] + jnp.sum(chosen.astype(F32), axis=1, keepdims=True)
    cnt_ref[...] = counts[...]

    @pl.when(i == pl.num_programs(0) - 1)
    def _():
        pad_start, blk_e, blk_valid, blk_next = _block_plan(counts[...], bm, blke_ref.shape[1])
        pstart_ref[...] = pad_start.astype(I32)
        blke_ref[...] = blk_e.astype(I32)
        blkv_ref[...] = blk_valid.astype(I32)
        blkn_ref[...] = blk_next.astype(I32)


def _route(x1, g_ffn, wr_t, br, cnt_in, rows, bm, n_blocks):
    n = x1.shape[0]
    assert n % rows == 0
    nb_lanes = -(-n_blocks // LANES) * LANES
    const = lambda shape: pl.BlockSpec(shape, lambda i: (0,) * len(shape))
    tok_blk = pl.BlockSpec((TOP_K, rows), lambda i: (0, i))
    return pl.pallas_call(
        functools.partial(_route_kernel, bm=bm),
        grid=(n // rows,),
        in_specs=[pl.BlockSpec((rows, D_MODEL), lambda i: (i, 0)), const((1, D_MODEL)), const((N_EXPERTS, D_MODEL)),
                  const((N_EXPERTS, 1)), const((N_EXPERTS, 1))],
        out_specs=[pl.BlockSpec((rows, D_MODEL // 2), lambda i: (i, 0)), tok_blk, tok_blk, tok_blk,
                   const((N_EXPERTS, 1)), const((N_EXPERTS, 1)), const((1, nb_lanes)), const((1, nb_lanes)),
                   const((1, nb_lanes))],
        out_shape=[jax.ShapeDtypeStruct((n, D_MODEL // 2), U32), jax.ShapeDtypeStruct((TOP_K, n), I32),
                   jax.ShapeDtypeStruct((TOP_K, n), F32), jax.ShapeDtypeStruct((TOP_K, n), I32),
                   jax.ShapeDtypeStruct((N_EXPERTS, 1), F32), jax.ShapeDtypeStruct((N_EXPERTS, 1), I32),
                   jax.ShapeDtypeStruct((1, nb_lanes), I32), jax.ShapeDtypeStruct((1, nb_lanes), I32),
                   jax.ShapeDtypeStruct((1, nb_lanes), I32)],
        scratch_shapes=[pltpu.VMEM((N_EXPERTS, 1), F32)],
        compiler_params=pltpu.CompilerParams(dimension_semantics=("arbitrary",), vmem_limit_bytes=VMEM_LIMIT),
        name="route",
    )(x1, g_ffn, wr_t, br, cnt_in)


def _sc_mesh():
    return plsc.VectorSubcoreMesh(core_axis_name="core", subcore_axis_name="subcore")


def _sc_worker_id():
    return lax.axis_index("core") * SC_SUBCORES + lax.axis_index("subcore")


def _scatter_rows(xa, xb, dest, n_rows):
    ch = SC_CHUNK
    na, w = xa.shape
    n = na + xb.shape[0]
    nk = dest.shape[0]
    assert na % ch == 0 and n % ch == 0 and dest.shape[1] == n and xb.shape[1] == w and xa.dtype == xb.dtype
    n_chunks = n // ch
    dest_c = dest.reshape(nk, n_chunks, ch).transpose(1, 0, 2).reshape(-1)
    dma = pltpu.SemaphoreType.DMA

    @pl.kernel(out_type=jax.ShapeDtypeStruct((n_rows, w), xa.dtype), mesh=_sc_mesh(),
               scratch_types=[pltpu.VMEM((nk * ch,), I32), pltpu.VMEM((ch, w), xa.dtype), dma] + [dma] * nk)
    def scatter_kernel(xa_hbm, xb_hbm, d_hbm, o_hbm, idx_v, buf, sem_i, *sem_s):
        wid = _sc_worker_id()

        @pl.loop(0, -(-n_chunks // SC_WORKERS))
        def _(j):
            c = j * SC_WORKERS + wid

            @pl.when(c < n_chunks)
            def _():
                load_idx = pltpu.make_async_copy(d_hbm.at[pl.ds(c * (nk * ch), nk * ch)], idx_v, sem_i)
                load_idx.start()

                @pl.when(c < na // ch)
                def _():
                    pltpu.sync_copy(xa_hbm.at[pl.ds(c * ch, ch)], buf)

                @pl.when(c >= na // ch)
                def _():
                    pltpu.sync_copy(xb_hbm.at[pl.ds(c * ch - na, ch)], buf)

                load_idx.wait()
                puts = [pltpu.make_async_copy(buf, o_hbm.at[idx_v.at[pl.ds(kk * ch, ch)]], sem_s[kk])
                        for kk in range(nk)]
                for cp in puts:
                    cp.start()
                for cp in puts:
                    cp.wait()

    return scatter_kernel(xa, xb, dest_c)


def _gather_rows(src, idx):
    ch = SC_CHUNK
    m = idx.shape[0]
    w = src.shape[1]
    per = m // SC_WORKERS
    n_pairs = per // (2 * ch)
    assert m % SC_WORKERS == 0 and per % (2 * ch) == 0
    dma = pltpu.SemaphoreType.DMA

    @pl.kernel(out_type=jax.ShapeDtypeStruct((m, w), src.dtype), mesh=_sc_mesh(),
               scratch_types=[pltpu.VMEM((per,), I32), pltpu.VMEM((ch, w), src.dtype), pltpu.VMEM((ch, w), src.dtype),
                              dma, dma, dma, dma])
    def gather_kernel(s_hbm, i_hbm, o_hbm, idx_v, buf_a, buf_b, sem_ga, sem_gb, sem_wa, sem_wb):
        base = _sc_worker_id() * per
        pltpu.sync_copy(i_hbm.at[pl.ds(base, per)], idx_v)

        def fetch(j, buf, sem):
            return pltpu.make_async_copy(s_hbm.at[idx_v.at[pl.ds(j * ch, ch)]], buf, sem)

        def put(j, buf, sem):
            return pltpu.make_async_copy(buf, o_hbm.at[pl.ds(base + j * ch, ch)], sem)

        fetch(0, buf_a, sem_ga).start()

        @pl.loop(0, n_pairs)
        def _(p):
            j0 = 2 * p
            j1 = j0 + 1

            @pl.when(p > 0)
            def _():
                put(j1 - 2, buf_b, sem_wb).wait()

            fetch(j1, buf_b, sem_gb).start()
            fetch(j0, buf_a, sem_ga).wait()
            put(j0, buf_a, sem_wa).start()
            fetch(j1, buf_b, sem_gb).wait()
            put(j1, buf_b, sem_wb).start()
            put(j0, buf_a, sem_wa).wait()

            @pl.when(p + 1 < n_pairs)
            def _():
                fetch(j0 + 2, buf_a, sem_ga).start()

        put(2 * n_pairs - 1, buf_b, sem_wb).wait()

    return gather_kernel(src, idx)


def _moe_kernel(blk_e_ref, blk_valid_ref, blk_next_ref, xs_ref, wgu_hbm, bgu_ref, wd_hbm, bd_ref, y_ref,
                wgu_f32, wd_f32, wgu_bf, wd_bf, sems, *, e0):
    i = pl.program_id(0)
    e = blk_e_ref[i]
    n_valid = blk_valid_ref[i]
    used = n_valid > 0
    new_expert = (i == 0) | (blk_e_ref[jnp.maximum(i - 1, 0)] != e)

    def weight_copies(expert):
        return (pltpu.make_async_copy(wgu_hbm.at[e0 + expert], wgu_f32, sems.at[0]),
                pltpu.make_async_copy(wd_hbm.at[e0 + expert], wd_f32, sems.at[1]))

    @pl.when(used & (i == 0))
    def _():
        for cp in weight_copies(e):
            cp.start()

    @pl.when(used & new_expert)
    def _():
        for cp in weight_copies(e):
            cp.wait()
        wgu_bf[...] = wgu_f32[...].astype(BF16)
        wd_bf[...] = wd_f32[...].astype(BF16)
        nxt = blk_next_ref[i]

        @pl.when(nxt >= 0)
        def _():
            for cp in weight_copies(nxt):
                cp.start(priority=1)

    half = xs_ref.shape[0] // 2

    def expert_mlp(n_rows):
        valid = lax.broadcasted_iota(I32, (n_rows, xs_ref.shape[1]), 0) < n_valid
        xb = _unpack_halves(jnp.where(valid, xs_ref[0:n_rows, :], jnp.uint32(0))).astype(BF16)
        gu = jnp.dot(xb, wgu_bf[...], preferred_element_type=F32) + bgu_ref[0]
        g = jnp.minimum(gu[:, :D_FF], SWIGLU_LIMIT)
        up = jnp.clip(gu[:, D_FF:], -SWIGLU_LIMIT, SWIGLU_LIMIT)
        act = (up + 1.0) * (g * jax.nn.sigmoid(SWIGLU_ALPHA * g))
        y = jnp.dot(act.astype(BF16), wd_bf[...], preferred_element_type=F32) + bd_ref[0]
        y_ref[0:n_rows, :] = _pack_halves(y)

    @pl.when(n_valid > half)
    def _():
        expert_mlp(2 * half)

    @pl.when(used & (n_valid <= half))
    def _():
        expert_mlp(half)
        y_ref[half:, :] = jnp.zeros((half, y_ref.shape[1]), y_ref.dtype)

    @pl.when(jnp.logical_not(used))
    def _():
        y_ref[...] = jnp.zeros_like(y_ref)


def _moe(xs, blk_e, blk_valid, blk_next, layer, w_gu, b_gu, w_d, b_d):
    n_rows = xs.shape[0]
    bm = MOE_BM
    assert n_rows % bm == 0
    e0 = layer * N_EXPERTS
    n_we = w_gu.shape[0] * w_gu.shape[1]
    any_spec = pl.BlockSpec(memory_space=pl.ANY)
    grid_spec = pltpu.PrefetchScalarGridSpec(
        num_scalar_prefetch=3,
        grid=(n_rows // bm,),
        in_specs=[pl.BlockSpec((bm, D_MODEL // 2), lambda i, be, bv, bn: (i, 0)),
                  any_spec,
                  pl.BlockSpec((1, 1, 2 * D_FF), lambda i, be, bv, bn: (e0 + be[i], 0, 0)),
                  any_spec,
                  pl.BlockSpec((1, 1, D_MODEL), lambda i, be, bv, bn: (e0 + be[i], 0, 0))],
        out_specs=pl.BlockSpec((bm, D_MODEL // 2), lambda i, be, bv, bn: (i, 0)),
        scratch_shapes=[pltpu.VMEM((D_MODEL, 2 * D_FF), F32), pltpu.VMEM((D_FF, D_MODEL), F32),
                        pltpu.VMEM((D_MODEL, 2 * D_FF), BF16), pltpu.VMEM((D_FF, D_MODEL), BF16),
                        pltpu.SemaphoreType.DMA((2,))],
    )
    return pl.pallas_call(
        functools.partial(_moe_kernel, e0=e0),
        grid_spec=grid_spec,
        out_shape=jax.ShapeDtypeStruct((n_rows, D_MODEL // 2), U32),
        compiler_params=pltpu.CompilerParams(dimension_semantics=("arbitrary",), vmem_limit_bytes=VMEM_LIMIT),
        name="moe_experts",
    )(blk_e, blk_valid, blk_next, xs, w_gu.reshape(n_we, D_MODEL, 2 * D_FF), b_gu.reshape(n_we, 1, 2 * D_FF),
      w_d.reshape(n_we, D_FF, D_MODEL), b_d.reshape(n_we, 1, D_MODEL))


def _ple_kernel(x1_ref, y0_ref, y1_ref, y2_ref, y3_ref, gates_ref, p_ref, g_ref, wg_ref, wp_ref, *rest):
    out_ref, wg_bf, wp_bf = rest[-3:]

    @pl.when(pl.program_id(0) == 0)
    def _():
        wg_bf[...] = wg_ref[...].astype(BF16)
        wp_bf[...] = wp_ref[...].astype(BF16)

    x2 = x1_ref[...]
    gates = gates_ref[...]
    for kk, y_ref in enumerate((y0_ref, y1_ref, y2_ref, y3_ref)):
        x2 = x2 + _unpack_halves(y_ref[...]) * gates[:, kk:kk + 1]
    hp = _rms(x2, g_ref[...]).astype(BF16)
    gate = 0.5 * jnp.tanh(0.5 * jnp.dot(hp, wg_bf[...], preferred_element_type=F32)) + 0.5
    pp = jnp.dot(p_ref[...].astype(BF16), wp_bf[...], preferred_element_type=F32)
    out_ref[...] = x2 + gate * pp


def _ple(x1, tok0, n, y_tok, y0, gates_t, p_all, p0, g_ple, w_gate, w_proj, rows, out_into=None):
    assert n % rows == 0 and tok0 % rows == 0 and y0 % rows == 0 and p0 % rows == 0
    const = lambda shape: pl.BlockSpec(shape, lambda i: (0,) * len(shape))
    tok_blk = lambda width: pl.BlockSpec((rows, width), lambda i: (tok0 // rows + i, 0))
    y_blk = lambda kk: pl.BlockSpec((rows, D_MODEL // 2), lambda i: ((y0 + kk * n) // rows + i, 0))
    operands = (x1, y_tok, y_tok, y_tok, y_tok, gates_t, p_all, g_ple, w_gate, w_proj)
    extra_specs, extra_args, aliases = [], [], {}
    if out_into is not None:
        assert out_into.shape == x1.shape
        extra_specs, extra_args, aliases = [pl.BlockSpec(memory_space=pl.ANY)], [out_into], {len(operands): 0}
    return pl.pallas_call(
        _ple_kernel,
        grid=(n // rows,),
        in_specs=[tok_blk(D_MODEL), y_blk(0), y_blk(1), y_blk(2), y_blk(3), tok_blk(TOP_K),
                  pl.BlockSpec((rows, PLE_DIM), lambda i: (p0 // rows + i, 0)),
                  const((1, D_MODEL)), const((D_MODEL, D_MODEL)), const((PLE_DIM, D_MODEL))] + extra_specs,
        out_specs=tok_blk(D_MODEL),
        out_shape=jax.ShapeDtypeStruct(x1.shape, F32),
        scratch_shapes=[pltpu.VMEM((D_MODEL, D_MODEL), BF16), pltpu.VMEM((PLE_DIM, D_MODEL), BF16)],
        input_output_aliases=aliases,
        compiler_params=pltpu.CompilerParams(dimension_semantics=("arbitrary",), vmem_limit_bytes=VMEM_LIMIT),
        name="combine_ple",
    )(*operands, *extra_args)


def _rope_tables_const(seq):
    half = ROT_DIM // 2
    d = np.arange(LANES) % HEAD_DIM
    inv = (np.float64(ROPE_THETA) ** (-np.arange(half, dtype=np.float64) / half)).astype(np.float32)
    inv_lane = np.where(d < ROT_DIM, inv[d % half], np.float32(0.0)).astype(np.float32)
    ang = np.arange(seq, dtype=np.float32)[:, None] * inv_lane[None, :]
    cos = np.cos(ang.astype(np.float64)).astype(np.float32)
    sin = np.sin(ang.astype(np.float64)).astype(np.float32)
    sin_a = np.where((d >= half) & (d < ROT_DIM), sin, np.float32(0.0))
    sin_b = np.where(d < half, -sin, np.float32(0.0))
    return tuple(jnp.asarray(a) for a in (cos, sin_a, sin_b))


def _rope_tables(pos):
    half = ROT_DIM // 2
    d = np.arange(LANES) % HEAD_DIM
    inv = ROPE_THETA ** (-jnp.arange(half, dtype=F32) / half)
    inv_lane = jnp.where(d < ROT_DIM, inv[d % half], 0.0)
    ang = pos.astype(F32)[:, None] * inv_lane[None, :]
    cos, sin = jnp.cos(ang), jnp.sin(ang)
    sin_a = jnp.where((d >= half) & (d < ROT_DIM), sin, 0.0)
    sin_b = jnp.where(d < half, -sin, 0.0)
    return cos, sin_a, sin_b


def _layer(layer, xp, xs, ck, cv, st, p_prompt_all, p_sample_all, past_len, refine_tail, tabs_p, tabs_s, finish_state,
           norm_attn, w_in, q_norm, k_norm, attn_sinks, w_pool, pool_scale, w_out,
           norm_ffn, w_router, b_router, w_gate_up_all, b_gate_up_all, w_down_all, b_down_all,
           norm_ple, w_ple_gate, w_ple_proj):
    t, ns = xp.shape[0], xs.shape[0]
    g_attn = norm_attn.reshape(1, D_MODEL)
    qn = jnp.tile(q_norm, 2).reshape(1, LANES)
    kn = jnp.tile(k_norm, 2).reshape(1, LANES)
    lane = np.arange(LANES)
    hmean = jnp.asarray((lane[:, None] // HEAD_DIM == lane[None, :] // HEAD_DIM) / HEAD_DIM, F32)
    pscale = pool_scale.reshape(1, POOL_WIDTH)
    sink8 = jnp.broadcast_to(attn_sinks.reshape(2, Q_TILES).T.reshape(8, 1), (8, LANES))

    mix_args = (g_attn, w_in, qn, kn, hmean, w_pool, pscale, w_out)
    x1p, nk_p, nv_p, nu_p = _mixer_prompt(xp, 0, MIX_ROWS, BF16, attn_sinks, tabs_p, *mix_args)
    if refine_tail:
        x1p, nk_p, nv_p, nu_p = _mixer_prompt(xp, t - 2 * MIX_TAIL_ROWS, MIX_TAIL_ROWS, F32, attn_sinks, tabs_p,
                                              *mix_args, x1_into=x1p)
    st_t = jnp.transpose(st, (1, 0, 2))
    x1s, nk_s, nv_s, nst_t = _mixer_sample(xs, ck.reshape(ns, -1, KV_WIDTH), cv.reshape(ns, -1, KV_WIDTH), st_t,
                                           past_len, tabs_s, sink8, *mix_args)

    g_ffn = norm_ffn.reshape(1, D_MODEL)
    wr_t = w_router.T
    br = b_router.reshape(N_EXPERTS, 1)
    n_tok = t + ns
    bm = MOE_BM
    n_blocks = -(-(n_tok * TOP_K + N_EXPERTS * (bm - 1)) // bm)
    hf_p, idx_p, gate_p, pos_p, cnt_p = _route(x1p, g_ffn, wr_t, br, jnp.zeros((N_EXPERTS, 1), F32),
                                               ROUTE_ROWS, bm, n_blocks)[:5]
    hf_s, idx_s, gate_s, pos_s, _, pad_start, blk_e, blk_valid, blk_next = _route(x1s, g_ffn, wr_t, br, cnt_p, ns,
                                                                                  bm, n_blocks)
    idx = jnp.concatenate([idx_p, idx_s], axis=1)
    pos = jnp.concatenate([pos_p, pos_s], axis=1)
    blk_e, blk_valid, blk_next = (a[0, :n_blocks] for a in (blk_e, blk_valid, blk_next))
    state = finish_state(nk_p, nv_p, nu_p[POOL_HALO - POOL_PREFIX:], nk_s, nv_s, jnp.transpose(nst_t, (1, 0, 2)))
    blk_e, state = lax.optimization_barrier((blk_e, state))
    start_of = jnp.sum(jnp.where(idx[None] == jnp.arange(N_EXPERTS, dtype=I32)[:, None, None],
                                 pad_start.reshape(N_EXPERTS, 1, 1), 0), axis=0)
    dest = start_of + pos

    xs_rows = _scatter_rows(hf_p, hf_s, dest, n_blocks * bm)
    y = _moe(xs_rows, blk_e, blk_valid, blk_next, layer, w_gate_up_all, b_gate_up_all, w_down_all, b_down_all)

    tq = t // COMBINE_PARTS
    unit = SC_WORKERS * SC_CHUNK * 2
    assert t % COMBINE_PARTS == 0 and (TOP_K * tq) % unit == 0 and tq % PLE_ROWS == 0
    g_ple = norm_ple.reshape(1, D_MODEL)
    gates_p = gate_p.T
    yp = None
    for part in range(COMBINE_PARTS):
        back = dest[:, part * tq:(part + 1) * tq].reshape(-1)
        if part == COMBINE_PARTS - 1:
            back = jnp.concatenate([back, dest[:, t:].reshape(-1)])
            back = jnp.concatenate([back, jnp.arange(-back.shape[0] % unit, dtype=I32)])
        y_part = _gather_rows(y, back)
        yp = _ple(x1p, part * tq, tq, y_part, 0, gates_p, p_prompt_all, layer * t + part * tq, g_ple,
                  w_ple_gate, w_ple_proj, PLE_ROWS, out_into=yp)
    ys = _ple(x1s, 0, ns, y_part, TOP_K * tq, gate_s.T, p_sample_all, layer * ns, g_ple, w_ple_gate, w_ple_proj, ns)
    return yp, ys, state


def kernel(x_prompt, x_sample, cache_k, cache_v, state_pool, p_prompt, p_sample, norm_attn, w_in, q_norm, k_norm,
           attn_sinks, w_pool, pool_scale, w_out, norm_ffn, w_router, b_router, w_gate_up, b_gate_up, w_down, b_down,
           norm_ple, w_ple_gate, w_ple_proj):
    depth = norm_attn.shape[0]
    batch, seq, d = x_prompt.shape
    ns, dec_seq, _ = x_sample.shape
    wb = cache_k.shape[2]
    assert batch == 1 and dec_seq == 1 and d == D_MODEL and wb == WINDOW
    assert cache_k.shape[3:] == (N_KV_HEADS, HEAD_DIM) and state_pool.shape[2:] == (POOL_PREFIX, POOL_WIDTH)
    past_len = PAST_LEN
    yp = x_prompt.reshape(seq, d)
    ys = x_sample.reshape(ns, d)
    p_prompt_all = p_prompt.reshape(depth * seq, PLE_DIM)
    p_sample_all = p_sample.reshape(depth * ns, PLE_DIM)
    tabs_p = _rope_tables_const(seq)
    tabs_s = _rope_tables(jnp.full((1,), past_len))
    kv_p, kv_s = (1, WINDOW, N_KV_HEADS, HEAD_DIM), (ns, wb, N_KV_HEADS, HEAD_DIM)
    state_shapes = (kv_p, kv_p, (1, POOL_PREFIX, POOL_WIDTH), kv_s, kv_s, (ns, POOL_PREFIX, POOL_WIDTH))
    per_layer = []

    def shaped(*state):
        return tuple(a.reshape(shp) for a, shp in zip(state, state_shapes))

    def stacked(*state):
        prevs = list(zip(*per_layer)) or [()] * len(state_shapes)
        return tuple(jnp.stack(list(prev) + [cur]) for prev, cur in zip(prevs, shaped(*state)))

    for i in range(depth):
        yp, ys, state = _layer(i, yp, ys, cache_k[i], cache_v[i], state_pool[i], p_prompt_all, p_sample_all, past_len,
                               i < depth - 1, tabs_p, tabs_s, shaped if i < depth - 1 else stacked,
                               norm_attn[i], w_in[i], q_norm[i], k_norm[i], attn_sinks[i], w_pool[i], pool_scale[i],
                               w_out[i], norm_ffn[i], w_router[i], b_router[i], w_gate_up, b_gate_up, w_down, b_down,
                               norm_ple[i], w_ple_gate[i], w_ple_proj[i])
        per_layer.append(state)
    return (yp.reshape(batch, seq, d), ys.reshape(ns, dec_seq, d)) + tuple(state)
```
